```python
import math
import jax, jax.numpy as jnp
from jax import lax
import numpy as np

D_MODEL = 1024
BATCH = 8
SEQ = 2048
DEPTH = 1
DEC_BATCH = 128
DEC_SEQ = 8
PAST_LEN = 16384
PAGE_SIZE = 128

D_PLE = 256
A_HEADS = 8
A_HEAD_DIM = 64
A_WIDTH = A_HEADS * A_HEAD_DIM
DECAY_LORA = 64
AAA_LORA = 64
GATE_LORA = 128
GN_EPS_RWKV = 64e-5
B_HEADS = 4
B_QK_DIM = 64
B_V_DIM = 128
B_QK_WIDTH = B_HEADS * B_QK_DIM
B_WIDTH = B_HEADS * B_V_DIM
RET_CHUNK = 128
ROPE_BASE = 10000.0
GN_EPS = 1e-5
MIX_WIDTH = A_WIDTH + B_WIDTH
SHIFT_WIDTH = 3 * A_WIDTH + DECAY_LORA + AAA_LORA + GATE_LORA
IN_WIDTH = SHIFT_WIDTH + 2 * B_QK_WIDTH + 2 * B_WIDTH
N_GROUPS = 4
EXPERTS_PER_GROUP = 4
N_EXPERTS = N_GROUPS * EXPERTS_PER_GROUP
D_EXPERT = 256
TOP_K_INNER = 2
DEEPNORM_ALPHA = (2 * DEPTH) ** 0.25
DEEPNORM_BETA = (8 * DEPTH) ** -0.25
LN_EPS = 1e-5

kernel_name = "hybrid_rwkv7_retention_hmoe_step"


def layer_norm(x, g, b):
    xf = x.astype(jnp.float32)
    mu = xf.mean(-1, keepdims=True)
    var = jnp.mean(jnp.square(xf - mu), -1, keepdims=True)
    return ((xf - mu) * lax.rsqrt(var + LN_EPS) * g + b).astype(x.dtype)


def rms_norm(x, g):
    xf = x.astype(jnp.float32)
    return (xf * lax.rsqrt(jnp.mean(jnp.square(xf), -1, keepdims=True) + LN_EPS) * g).astype(x.dtype)


def head_norm(y, g, b, eps):
    mu = y.mean(-1, keepdims=True)
    var = jnp.mean(jnp.square(y - mu), -1, keepdims=True)
    return (y - mu) * lax.rsqrt(var + eps) * g + b


def rotary(x, pos):
    half = x.shape[-1] // 2
    inv = ROPE_BASE ** (-jnp.arange(half, dtype=jnp.float32) / half)
    ang = pos.astype(jnp.float32)[:, None] * inv[None, :]
    cos = jnp.cos(ang)[:, None, :]
    sin = jnp.sin(ang)[:, None, :]
    x1, x2 = x[..., :half], x[..., half:]
    return jnp.concatenate([x1 * cos - x2 * sin, x1 * sin + x2 * cos], axis=-1)


def rwkv7_scan(r, w, k, v, kk, a, s0):
    def step(S, inp):
        r_t, w_t, k_t, v_t, kk_t, a_t = inp
        sa = jnp.einsum('bhvk,bhk->bhv', S, -kk_t)
        S = (S * w_t[:, :, None, :] + sa[..., None] * (kk_t * a_t)[:, :, None, :]
             + v_t[..., None] * k_t[:, :, None, :])
        return S, jnp.einsum('bhvk,bhk->bhv', S, r_t)
    xs = tuple(jnp.moveaxis(t, 1, 0) for t in (r, w, k, v, kk, a))
    S, ys = lax.scan(step, s0, xs)
    return jnp.moveaxis(ys, 0, 1), S


def retention_chunked(q, k, v, s0, chunk):
    Bsz, T, H, _ = q.shape
    dv = v.shape[-1]
    nc = T // chunk
    lg = jnp.log1p(-jnp.exp2(-5.0 - jnp.arange(H, dtype=jnp.float32)))
    idx = jnp.arange(chunk, dtype=jnp.float32)
    diff = idx[:, None] - idx[None, :]
    intra = jnp.where(diff >= 0, jnp.exp(lg[:, None, None] * jnp.maximum(diff, 0.0)), 0.0)
    cross = jnp.exp(lg[:, None] * (idx + 1.0))
    kdec = jnp.exp(lg[:, None] * (chunk - 1.0 - idx))
    cdec = jnp.exp(lg * chunk)

    def to_chunks(t):
        return t.reshape(Bsz, nc, chunk, H, t.shape[-1]).transpose(1, 0, 3, 2, 4)

    def step(S, inp):
        qc, kc, vc = inp
        sc = jnp.einsum('bhid,bhjd->bhij', qc, kc) * intra
        o = (jnp.einsum('bhij,bhjv->bhiv', sc, vc)
             + jnp.einsum('bhid,bhdv->bhiv', qc, S) * cross[:, :, None])
        S = S * cdec[:, None, None] + jnp.einsum('bhjd,bhjv->bhdv', kc * kdec[:, :, None], vc)
        return S, o

    S, o = lax.scan(step, s0, (to_chunks(q), to_chunks(k), to_chunks(v)))
    o = o.transpose(1, 0, 3, 2, 4).reshape(Bsz, T, H, dv)
    return o, S


def hybrid_mixer(x, x_prev, wkv0, ret0, pos0, w_in, mu_shift, w_decay_up, decay_base, w_aaa_up,
                 aaa_base, w_gate_up, k_k, k_a, r_k, lnx_g, lnx_b, ret_gn_g, ret_gn_b, w_out):
    f32 = jnp.float32
    Bsz, T, _ = x.shape
    x_ext = jnp.concatenate([x_prev[:, None, :].astype(x.dtype), x], axis=1)
    proj = jnp.einsum('btd,de->bte', x_ext, w_in).astype(f32)
    cur = proj[:, 1:]
    sc = cur[..., :SHIFT_WIDTH]
    sh = sc + (proj[:, :-1, :SHIFT_WIDTH] - sc) * mu_shift.astype(f32)
    r, k, v, w_lo, a_lo, g_lo = jnp.split(
        sh, (A_WIDTH, 2 * A_WIDTH, 3 * A_WIDTH, 3 * A_WIDTH + DECAY_LORA,
             3 * A_WIDTH + DECAY_LORA + AAA_LORA), axis=-1)
    log_w = -jax.nn.softplus(-(decay_base + jnp.tanh(w_lo) @ w_decay_up)) - 0.5
    decay = jnp.exp(-jnp.exp(log_w.astype(f32)))
    a = jax.nn.sigmoid(aaa_base + a_lo @ w_aaa_up).astype(f32)
    g = (jax.nn.sigmoid(g_lo) @ w_gate_up).astype(f32)

    def hs(t):
        return t.reshape(Bsz, T, A_HEADS, A_HEAD_DIM)
    kk = hs(k * k_k).astype(f32)
    kk = kk / jnp.maximum(jnp.sqrt(jnp.sum(jnp.square(kk), -1, keepdims=True)), 1e-12)
    k = (k * (1.0 + (a - 1.0) * k_a)).astype(f32)
    r4, k4, v4, a4, w4 = hs(r), hs(k), hs(v), hs(a), hs(decay)
    y_a, wkv1 = rwkv7_scan(r4, w4, k4, v4, kk, a4, wkv0.astype(f32))
    y_a = head_norm(y_a, lnx_g, lnx_b, GN_EPS_RWKV) + jnp.sum(r4 * k4 * r_k, -1, keepdims=True) * v4
    y_a = y_a.reshape(Bsz, T, A_WIDTH) * g
    q_b, k_b, v_b, g_b = jnp.split(
        cur[..., SHIFT_WIDTH:], (B_QK_WIDTH, 2 * B_QK_WIDTH, 2 * B_QK_WIDTH + B_WIDTH), axis=-1)
    pos = pos0 + jnp.arange(T, dtype=jnp.int32)
    q_b = rotary(q_b.reshape(Bsz, T, B_HEADS, B_QK_DIM), pos)
    k_b = rotary(k_b.reshape(Bsz, T, B_HEADS, B_QK_DIM), pos) * (B_QK_DIM ** -0.5)
    v_b = v_b.reshape(Bsz, T, B_HEADS, B_V_DIM)
    chunk = RET_CHUNK if T % RET_CHUNK == 0 else T
    o_b, ret1 = retention_chunked(q_b, k_b, v_b, ret0.astype(f32), chunk)
    y_b = head_norm(o_b, ret_gn_g, ret_gn_b, GN_EPS).reshape(Bsz, T, B_WIDTH) * jax.nn.silu(g_b)
    y = jnp.concatenate([y_a, y_b], axis=-1).astype(x.dtype)
    return jnp.einsum('bte,ed->btd', y, w_out), wkv1, ret1


def hier_moe(x, router_coarse_w, router_coarse_b, router_fine_w, router_fine_b, expert_w1, expert_w3, expert_w2):
    Bsz, T, D = x.shape
    xf = x.reshape(Bsz * T, D)
    n = xf.shape[0]
    rows = jnp.arange(n)
    cl = (xf @ router_coarse_w + router_coarse_b).astype(jnp.float32)
    grp = jnp.argmax(cl, axis=-1)
    gprob = jax.nn.softmax(cl, axis=-1)[rows, grp]
    fl = (xf @ router_fine_w + router_fine_b).astype(jnp.float32).reshape(n, N_GROUPS, EXPERTS_PER_GROUP)
    top_v, top_i = lax.top_k(fl[rows, grp], TOP_K_INNER)
    top_w = jax.nn.softmax(top_v, axis=-1) * gprob[:, None]
    expert_idx = grp[:, None] * EXPERTS_PER_GROUP + top_i
    gate = jnp.sum(jax.nn.one_hot(expert_idx, N_EXPERTS, dtype=jnp.float32) * top_w[..., None], axis=1)
    h = jax.nn.silu(jnp.einsum('nd,edf->nef', xf, expert_w1)) * jnp.einsum('nd,edf->nef', xf, expert_w3)
    y = jnp.einsum('nef,efd->nd', h * gate[..., None].astype(h.dtype), expert_w2)
    return y.reshape(Bsz, T, D)


def trunk_layer(x, p, x_prev, wkv0, ret0, pos0,
                w_in, mu_shift, w_decay_up, decay_base, w_aaa_up, aaa_base, w_gate_up, k_k, k_a, r_k,
                lnx_g, lnx_b, ret_gn_g, ret_gn_b, w_out, ln1_g, ln1_b,
                router_coarse_w, router_coarse_b, router_fine_w, router_fine_b,
                expert_w1, expert_w3, expert_w2, ln2_g, ln2_b, w_ple, w_ple_gate, ple_norm_g):
    mix, wkv1, ret1 = hybrid_mixer(x, x_prev, wkv0, ret0, pos0, w_in, mu_shift, w_decay_up, decay_base,
                                   w_aaa_up, aaa_base, w_gate_up, k_k, k_a, r_k, lnx_g, lnx_b,
                                   ret_gn_g, ret_gn_b, w_out)
    h = layer_norm(DEEPNORM_ALPHA * x + mix, ln1_g, ln1_b)
    ffn = hier_moe(h, router_coarse_w, router_coarse_b, router_fine_w, router_fine_b, expert_w1, expert_w3, expert_w2)
    h = layer_norm(DEEPNORM_ALPHA * h + ffn, ln2_g, ln2_b)
    ple = jnp.einsum('btp,pd->btd', p, w_ple) * jax.nn.sigmoid(jnp.einsum('btd,de->bte', h, w_ple_gate))
    h = h + rms_norm(ple, ple_norm_g)
    return h, x[:, -1], wkv1, ret1


def setup_inputs(seed: int = 0) -> dict:
    key = jax.random.key(seed)
    ks = iter(jax.random.split(key, 48))
    f32 = jnp.float32
    L = DEPTH

    def nrm(shape, scale):
        return scale * jax.random.normal(next(ks), shape, f32)

    def gain(shape):
        return 1.0 + nrm(shape, 0.02)

    return {
        "x_prompt": nrm((BATCH, SEQ, D_MODEL), 1.0),
        "x_sample": nrm((DEC_BATCH, DEC_SEQ, D_MODEL), 1.0),
        "p_prompt": nrm((DEPTH, BATCH, SEQ, D_PLE), 1.0),
        "p_sample": nrm((DEPTH, DEC_BATCH, DEC_SEQ, D_PLE), 1.0),
        "state_wkv": nrm((DEPTH, DEC_BATCH, A_HEADS, A_HEAD_DIM, A_HEAD_DIM), 0.3),
        "state_shift": nrm((DEPTH, DEC_BATCH, D_MODEL), 1.0),
        "state_ret": nrm((DEPTH, DEC_BATCH, B_HEADS, B_QK_DIM, B_V_DIM), 0.3),
        "w_in": nrm((L, D_MODEL, IN_WIDTH), D_MODEL ** -0.5),
        "mu_shift": jax.random.uniform(next(ks), (L, SHIFT_WIDTH), f32),
        "w_decay_up": nrm((L, DECAY_LORA, A_WIDTH), 0.5 * DECAY_LORA ** -0.5),
        "decay_base": jax.random.uniform(next(ks), (L, A_WIDTH), f32, -6.0, -1.0),
        "w_aaa_up": nrm((L, AAA_LORA, A_WIDTH), 0.5 * AAA_LORA ** -0.5),
        "aaa_base": nrm((L, A_WIDTH), 0.1),
        "w_gate_up": nrm((L, GATE_LORA, A_WIDTH), GATE_LORA ** -0.5),
        "k_k": 0.85 + nrm((L, A_WIDTH), 0.02),
        "k_a": gain((L, A_WIDTH)),
        "r_k": nrm((L, A_HEADS, A_HEAD_DIM), 0.1),
        "lnx_g": gain((L, A_HEADS, A_HEAD_DIM)),
        "lnx_b": nrm((L, A_HEADS, A_HEAD_DIM), 0.02),
        "ret_gn_g": gain((L, B_HEADS, B_V_DIM)),
        "ret_gn_b": nrm((L, B_HEADS, B_V_DIM), 0.02),
        "w_out": nrm((L, MIX_WIDTH, D_MODEL), DEEPNORM_BETA * MIX_WIDTH ** -0.5),
        "ln1_g": gain((L, D_MODEL)),
        "ln1_b": nrm((L, D_MODEL), 0.02),
        "router_coarse_w": nrm((L, D_MODEL, N_GROUPS), D_MODEL ** -0.5),
        "router_coarse_b": nrm((L, N_GROUPS), 0.01),
        "router_fine_w": nrm((L, D_MODEL, N_EXPERTS), D_MODEL ** -0.5),
        "router_fine_b": nrm((L, N_EXPERTS), 0.01),
        "expert_w1": nrm((L, N_EXPERTS, D_MODEL, D_EXPERT), D_MODEL ** -0.5),
        "expert_w3": nrm((L, N_EXPERTS, D_MODEL, D_EXPERT), D_MODEL ** -0.5),
        "expert_w2": nrm((L, N_EXPERTS, D_EXPERT, D_MODEL), DEEPNORM_BETA * D_EXPERT ** -0.5),
        "ln2_g": gain((L, D_MODEL)),
        "ln2_b": nrm((L, D_MODEL), 0.02),
        "w_ple": nrm((L, D_PLE, D_MODEL), D_PLE ** -0.5),
        "w_ple_gate": nrm((L, D_MODEL, D_MODEL), D_MODEL ** -0.5),
        "ple_norm_g": gain((L, D_MODEL)),
    }


def reference(x_prompt, x_sample, p_prompt, p_sample, state_wkv, state_shift, state_ret,
              w_in, mu_shift, w_decay_up, decay_base, w_aaa_up, aaa_base, w_gate_up, k_k, k_a, r_k,
              lnx_g, lnx_b, ret_gn_g, ret_gn_b, w_out, ln1_g, ln1_b,
              router_coarse_w, router_coarse_b, router_fine_w, router_fine_b,
              expert_w1, expert_w3, expert_w2, ln2_g, ln2_b, w_ple, w_ple_gate, ple_norm_g):
    yp, ys = x_prompt, x_sample
    nb = x_prompt.shape[0]
    wkv_p, shift_p, ret_p, wkv_s, shift_s, ret_s = [], [], [], [], [], []
    for i in range(DEPTH):
        lw = (w_in[i], mu_shift[i], w_decay_up[i], decay_base[i], w_aaa_up[i], aaa_base[i], w_gate_up[i],
              k_k[i], k_a[i], r_k[i], lnx_g[i], lnx_b[i], ret_gn_g[i], ret_gn_b[i], w_out[i], ln1_g[i], ln1_b[i],
              router_coarse_w[i], router_coarse_b[i], router_fine_w[i], router_fine_b[i],
              expert_w1[i], expert_w3[i], expert_w2[i], ln2_g[i], ln2_b[i], w_ple[i], w_ple_gate[i], ple_norm_g[i])
        zero_shift = jnp.zeros((nb, D_MODEL), yp.dtype)
        zero_wkv = jnp.zeros((nb, A_HEADS, A_HEAD_DIM, A_HEAD_DIM), jnp.float32)
        zero_ret = jnp.zeros((nb, B_HEADS, B_QK_DIM, B_V_DIM), jnp.float32)
        yp, sp, wp, rp = trunk_layer(yp, p_prompt[i], zero_shift, zero_wkv, zero_ret, 0, *lw)
        ys, ss, wsm, rsm = trunk_layer(ys, p_sample[i], state_shift[i], state_wkv[i], state_ret[i], PAST_LEN, *lw)
        wkv_p.append(wp); shift_p.append(sp); ret_p.append(rp)
        wkv_s.append(wsm); shift_s.append(ss); ret_s.append(rsm)
    new_wkv_prompt = jnp.stack(wkv_p, 0)
    new_shift_prompt = jnp.stack(shift_p, 0)
    new_ret_prompt = jnp.stack(ret_p, 0)
    new_wkv_sample = jnp.stack(wkv_s, 0)
    new_shift_sample = jnp.stack(shift_s, 0)
    new_ret_sample = jnp.stack(ret_s, 0)
    return (yp, ys, new_wkv_prompt, new_shift_prompt, new_ret_prompt, new_wkv_sample, new_shift_sample, new_ret_sample)
```

```python
import functools
import math

import numpy as np
import jax
import jax.numpy as jnp
from jax import lax
from jax.experimental import pallas as pl
from jax.experimental.pallas import tpu as pltpu

F32 = jnp.float32
BF16 = jnp.bfloat16

D_MODEL = 1024
D_PLE = 256
A_HEADS = 8
A_HEAD_DIM = 64
A_WIDTH = A_HEADS * A_HEAD_DIM
DECAY_LORA = 64
AAA_LORA = 64
GATE_LORA = 128
GN_EPS_RWKV = 64e-5
B_HEADS = 4
B_QK_DIM = 64
B_V_DIM = 128
B_QK_WIDTH = B_HEADS * B_QK_DIM
B_WIDTH = B_HEADS * B_V_DIM
ROPE_BASE = 10000.0
GN_EPS = 1e-5
SHIFT_WIDTH = 3 * A_WIDTH + DECAY_LORA + AAA_LORA + GATE_LORA
IN_WIDTH = SHIFT_WIDTH + 2 * B_QK_WIDTH + 2 * B_WIDTH
N_GROUPS = 4
EXPERTS_PER_GROUP = 4
N_EXPERTS = N_GROUPS * EXPERTS_PER_GROUP
D_EXPERT = 256
DEPTH = 1
PAST_LEN = 16384
DEEPNORM_ALPHA = (2 * DEPTH) ** 0.25
LN_EPS = 1e-5

V7X_LANES = 128
V7X_SUBLANES = 8
V7X_VMEM_LIMIT_BYTES = 56 * 1024 * 1024

PROJ_ROWS = 256
CHUNK_ROWS = 64
POST_ROWS = 512
FFN_ROWS = 512
FFN_EXPERTS_PER_STEP = 2
ROUTER_LANES = V7X_LANES
FINE_LANE0 = N_GROUPS


def _dot(a, b):
    return jnp.dot(a.astype(BF16), b.astype(BF16), preferred_element_type=F32)


def _dot_nt(a, b):
    return lax.dot_general(a.astype(BF16), b.astype(BF16), (((1,), (1,)), ((), ())), preferred_element_type=F32)


def _dot_tn(a, b):
    return lax.dot_general(a.astype(BF16), b.astype(BF16), (((0,), (0,)), ((), ())), preferred_element_type=F32)


def _split2(x):
    hi = x.astype(BF16)
    lo = (x - hi.astype(F32)).astype(BF16)
    return hi, lo


def _split3(x):
    hi = x.astype(BF16)
    r1 = x - hi.astype(F32)
    mid = r1.astype(BF16)
    lo = (r1 - mid.astype(F32)).astype(BF16)
    return hi, mid, lo


def _sigmoid(x):
    return 1.0 / (1.0 + jnp.exp(-x))


def _group_ones(width, group):
    r = lax.broadcasted_iota(jnp.int32, (width, width), 0) // group
    c = lax.broadcasted_iota(jnp.int32, (width, width), 1) // group
    return jnp.where(r == c, 1.0, 0.0).astype(BF16)


def _group_sum(x, ones):
    hi, lo = _split2(x)
    return (jnp.dot(hi, ones, preferred_element_type=F32) + jnp.dot(lo, ones, preferred_element_type=F32))


def _proj_kernel(carry_mode, seq_len, tiles_per_seq,
                 x_ref, xp_ref, w_ref, mu_ref, wdec_ref, dbase_ref, waaa_ref, abase_ref, wgate_ref,
                 kk_ref, ka_ref, rk_ref, cos_ref, sin_ref,
                 r_o, lw_o, k_o, v_o, al_o, be_o, g_o, bonus_o, qb_o, kb_o, vb_o, gb_o,
                 carry_scr):
    tm = x_ref.shape[0]
    row = lax.broadcasted_iota(jnp.int32, (tm, SHIFT_WIDTH), 0)
    if carry_mode:
        xp = jnp.broadcast_to(xp_ref[0], (V7X_SUBLANES, D_MODEL))
        xb = jnp.concatenate([x_ref[...], xp], axis=0).astype(BF16)
        proj_all = jnp.dot(xb, w_ref[...], preferred_element_type=F32)
        proj = proj_all[:tm]
        cur_s = proj[:, :SHIFT_WIDTH]
        rolled = pltpu.roll(cur_s, 1, 0)
        j = pl.program_id(0) % tiles_per_seq

        @pl.when(pl.program_id(0) == 0)
        def _():
            carry_scr[...] = jnp.zeros_like(carry_scr)

        first = jnp.where(j == 0, proj_all[tm + V7X_SUBLANES - 1:, :SHIFT_WIDTH],
                          carry_scr[V7X_SUBLANES - 1:V7X_SUBLANES, :])
        prev = jnp.where(row == 0, first, rolled)
        carry_scr[...] = cur_s[tm - V7X_SUBLANES:, :]
    else:
        xb = x_ref[...].astype(BF16)
        proj = jnp.dot(xb, w_ref[...], preferred_element_type=F32)
        cur_s = proj[:, :SHIFT_WIDTH]
        rolled = pltpu.roll(cur_s, 1, 0)
        first = jnp.dot(xp_ref[...].astype(BF16), w_ref[:, :SHIFT_WIDTH], preferred_element_type=F32)
        prev = jnp.where((row & (seq_len - 1)) == 0, first, rolled)
    sh = cur_s + (prev - cur_s) * mu_ref[...]

    r = sh[:, :A_WIDTH]
    k0 = sh[:, A_WIDTH:2 * A_WIDTH]
    v = sh[:, 2 * A_WIDTH:3 * A_WIDTH]
    o = 3 * A_WIDTH
    w_lo = sh[:, o:o + DECAY_LORA]
    a_lo = sh[:, o + DECAY_LORA:o + DECAY_LORA + AAA_LORA]
    g_lo = sh[:, o + DECAY_LORA + AAA_LORA:SHIFT_WIDTH]

    z = -(dbase_ref[...] + _dot(jnp.tanh(w_lo), wdec_ref[...]))
    softplus = jnp.maximum(z, 0.0) + jnp.log1p(jnp.exp(-jnp.abs(z)))
    log_w = -softplus - 0.5
    lw = -jnp.exp(log_w)
    a = _sigmoid(abase_ref[...] + _dot(a_lo, waaa_ref[...]))
    g = _dot(_sigmoid(g_lo), wgate_ref[...])

    ones64 = _group_ones(A_WIDTH, A_HEAD_DIM)
    kk0 = k0 * kk_ref[...]
    ssq = _group_sum(kk0 * kk0, ones64)
    kk = kk0 / jnp.maximum(jnp.sqrt(ssq), 1e-12)
    k = k0 * (1.0 + (a - 1.0) * ka_ref[...])
    bonus = _group_sum(r * k * rk_ref[...], ones64) * v

    r_o[...] = r
    lw_o[...] = lw
    k_o[...] = k
    v_o[...] = v
    al_o[...] = -kk
    be_o[...] = kk * a
    g_o[...] = g
    bonus_o[...] = bonus

    o = SHIFT_WIDTH
    q_b = proj[:, o:o + B_QK_WIDTH]
    k_b = proj[:, o + B_QK_WIDTH:o + 2 * B_QK_WIDTH]
    v_b = proj[:, o + 2 * B_QK_WIDTH:o + 2 * B_QK_WIDTH + B_WIDTH]
    g_b = proj[:, o + 2 * B_QK_WIDTH + B_WIDTH:]
    lane = lax.broadcasted_iota(jnp.int32, (tm, B_QK_WIDTH), 1)
    first_half = (lane & (B_QK_DIM - 1)) < (B_QK_DIM // 2)
    cos = cos_ref[...]
    sin = sin_ref[...]

    def rot(t):
        swapped = jnp.where(first_half, pltpu.roll(t, B_QK_WIDTH - B_QK_DIM // 2, 1), pltpu.roll(t, B_QK_DIM // 2, 1))
        return t * cos + swapped * sin

    qb_o[...] = rot(q_b)
    kb_o[...] = rot(k_b) * (B_QK_DIM ** -0.5)
    vb_o[...] = v_b
    gb_o[...] = g_b * _sigmoid(g_b)


def _proj(x2, x_prev, seq_len, pos0, W):
    n = x2.shape[0]
    tm = PROJ_ROWS
    assert n % tm == 0
    carry_mode = seq_len % tm == 0
    if carry_mode:
        tiles_per_seq = seq_len // tm
        xp = x_prev.reshape(-1, 1, D_MODEL)
        xp_spec = pl.BlockSpec((1, 1, D_MODEL), lambda i: (i // tiles_per_seq, 0, 0))
        tab_rows = seq_len
    else:
        assert tm % seq_len == 0 and seq_len & (seq_len - 1) == 0
        tiles_per_seq = 1
        xp = jnp.repeat(x_prev, seq_len, axis=0)
        xp_spec = pl.BlockSpec((tm, D_MODEL), lambda i: (i, 0))
        tab_rows = tm
    half = B_QK_DIM // 2
    inv = ROPE_BASE ** (-jnp.arange(half, dtype=F32) / half)
    pos = (pos0 + jnp.arange(seq_len, dtype=jnp.int32)).astype(F32)
    ang = pos[:, None] * inv[None, :]
    cos = jnp.tile(jnp.concatenate([jnp.cos(ang), jnp.cos(ang)], -1), (tab_rows // seq_len, B_HEADS))
    sin = jnp.tile(jnp.concatenate([-jnp.sin(ang), jnp.sin(ang)], -1), (tab_rows // seq_len, B_HEADS))
    tab_tiles = tab_rows // tm
    tab_spec = pl.BlockSpec((tm, B_QK_WIDTH), lambda i: (i % tab_tiles, 0))

    def full(a):
        return pl.BlockSpec(a.shape, lambda i: (0,) * a.ndim)

    def rows(width):
        return pl.BlockSpec((tm, width), lambda i: (i, 0))

    params = [W["w_in"], W["mu_shift"], W["w_decay_up"], W["decay_base"], W["w_aaa_up"], W["aaa_base"],
              W["w_gate_up"], W["k_k"], W["k_a"], W["r_k"]]
    widths = [A_WIDTH] * 8 + [B_QK_WIDTH, B_QK_WIDTH, B_WIDTH, B_WIDTH]
    outs = pl.pallas_call(
        functools.partial(_proj_kernel, carry_mode, seq_len, tiles_per_seq),
        grid=(n // tm,),
        in_specs=[rows(D_MODEL), xp_spec] + [full(a) for a in params] + [tab_spec, tab_spec],
        out_specs=[rows(w) for w in widths],
        out_shape=[jax.ShapeDtypeStruct((n, w), F32) for w in widths],
        scratch_shapes=[pltpu.VMEM((V7X_SUBLANES, SHIFT_WIDTH), F32)],
        compiler_params=pltpu.CompilerParams(dimension_semantics=("arbitrary",),
                                             vmem_limit_bytes=V7X_VMEM_LIMIT_BYTES),
        name="proj",
    )(x2, xp, *params, cos, sin)
    return outs


def _mixer_kernel(seqs, clen,
                  r_ref, lw_ref, k_ref, v_ref, al_ref, be_ref, qb_ref, kb_ref, vb_ref, wkv0_ref, ret0_ref,
                  ya_o, ob_o, wkv_o, ret_o, s_scr, r_scr):
    R = seqs * clen
    log2c = int(math.log2(clen))
    c_idx = pl.program_id(1)
    hd = A_HEAD_DIM

    @pl.when(c_idx == 0)
    def _():
        for h in range(A_HEADS):
            s_scr[h] = jnp.concatenate([wkv0_ref[i, h] for i in range(seqs)], axis=1) if seqs > 1 else wkv0_ref[0, h]
        for h in range(B_HEADS):
            r_scr[h] = ret0_ref[:, h].reshape(seqs * B_QK_DIM, B_V_DIM)

    row = lax.broadcasted_iota(jnp.int32, (R, R), 0)
    col = lax.broadcasted_iota(jnp.int32, (R, R), 1)
    same = (row >> log2c) == (col >> log2c)
    incl = same & (col <= row)
    strict = same & (col < row)
    m_incl = jnp.where(incl, 1.0, 0.0).astype(BF16)
    m_same = jnp.where(same, 1.0, 0.0).astype(BF16)
    eye = jnp.where(row == col, 1.0, 0.0).astype(F32)

    def expand(t):
        if seqs == 1:
            return t
        w = t.shape[1]
        wide = jnp.concatenate([t] * seqs, axis=1)
        rr = lax.broadcasted_iota(jnp.int32, wide.shape, 0) >> log2c
        cc = lax.broadcasted_iota(jnp.int32, wide.shape, 1) // w
        return jnp.where(rr == cc, wide, 0.0)

    lw = lw_ref[...]
    parts = _split3(lw)
    c = sum(jnp.dot(m_incl, p, preferred_element_type=F32) for p in parts)
    cend = sum(jnp.dot(m_same, p, preferred_element_type=F32) for p in parts)
    r = r_ref[...]
    k = k_ref[...]
    v = v_ref[...]
    al = al_ref[...]
    be = be_ref[...]
    a_bar = al * jnp.exp(c - lw)
    r_bar = r * jnp.exp(c)
    einv = jnp.exp(-c)
    b_til = be * einv
    k_til = k * einv
    edec = jnp.exp(cend - c)
    b_dec = be * edec
    k_dec = k * edec
    d_end = jnp.exp(cend)
    last_row = (lax.broadcasted_iota(jnp.int32, (R, seqs * hd), 0) & (clen - 1)) == clen - 1

    ys = []
    for h in range(A_HEADS):
        sl = slice(h * hd, (h + 1) * hd)
        lhs = jnp.concatenate([a_bar[:, sl], r_bar[:, sl]], axis=0)
        rhs = jnp.concatenate([b_til[:, sl], k_til[:, sl]], axis=0)
        amat = _dot_nt(lhs, rhs)
        a_ab = jnp.where(strict, amat[:R, :R], 0.0)
        a_ak = jnp.where(strict, amat[:R, R:], 0.0)
        a_rb = jnp.where(incl, amat[R:, :R], 0.0)
        a_rk = jnp.where(incl, amat[R:, R:], 0.0)
        tinv = eye + a_ab
        pw = _dot(a_ab, a_ab) if log2c > 1 else None
        for it in range(log2c - 1):
            if it < log2c - 2:
                tp = _dot(pw, jnp.concatenate([tinv, pw], axis=1))
                tinv = tinv + tp[:, :R]
                pw = tp[:, R:]
            else:
                tinv = tinv + _dot(pw, tinv)
        vh = v[:, sl]
        av = _dot(a_ak, vh)
        wu = _dot(tinv, jnp.concatenate([a_bar[:, sl], av], axis=1))
        w_mat = wu[:, :hd]
        u0 = wu[:, hd:]
        state = s_scr[h]
        ws = _dot_nt(jnp.concatenate([expand(w_mat), expand(r_bar[:, sl])], axis=0), state)
        u = ws[:R] + u0
        uv = jnp.concatenate([u, vh], axis=0)
        y = ws[R:] + _dot(jnp.concatenate([a_rb, a_rk], axis=1), uv)
        ys.append(y)
        d_row = jnp.sum(jnp.where(last_row, expand(d_end[:, sl]), 0.0), axis=0, keepdims=True)
        bk = jnp.concatenate([expand(b_dec[:, sl]), expand(k_dec[:, sl])], axis=0)
        s_scr[h] = state * d_row + _dot_tn(uv, bk)
    ya_o[...] = jnp.concatenate(ys, axis=1)

    qb = qb_ref[...]
    kb = kb_ref[...]
    vb = vb_ref[...]
    diff = (row - col).astype(F32)
    pos_v = (lax.broadcasted_iota(jnp.int32, (R, B_V_DIM), 0) & (clen - 1)).astype(F32)
    pos_k = (lax.broadcasted_iota(jnp.int32, (R, B_QK_DIM), 0) & (clen - 1)).astype(F32)
    os_ = []
    for h in range(B_HEADS):
        lg = float(np.log1p(-np.exp2(-5.0 - h)))
        q = qb[:, h * B_QK_DIM:(h + 1) * B_QK_DIM]
        kh = kb[:, h * B_QK_DIM:(h + 1) * B_QK_DIM]
        vh = vb[:, h * B_V_DIM:(h + 1) * B_V_DIM]
        intra = jnp.where(incl, jnp.exp(lg * diff), 0.0)
        cross = jnp.exp(lg * (pos_v + 1.0))
        kdec = jnp.exp(lg * (clen - 1.0 - pos_k))
        sc = _dot_nt(q, kh) * intra
        st = r_scr[h]
        os_.append(_dot(sc, vh) + _dot(expand(q), st) * cross)
        r_scr[h] = st * float(np.exp(lg * clen)) + _dot_tn(expand(kh * kdec), vh)
    ob_o[...] = jnp.concatenate(os_, axis=1)

    @pl.when(c_idx == pl.num_programs(1) - 1)
    def _():
        for h in range(A_HEADS):
            st = s_scr[h]
            for i in range(seqs):
                wkv_o[i, h] = st[:, i * hd:(i + 1) * hd]
        for h in range(B_HEADS):
            ret_o[:, h] = r_scr[h].reshape(seqs, B_QK_DIM, B_V_DIM)


def _mixer(ops, wkv0, ret0, n_seq, seq_len):
    r, lw, k, v, al, be, qb, kb, vb = ops
    n = r.shape[0]
    R = CHUNK_ROWS
    if seq_len >= R:
        assert seq_len % R == 0
        seqs, clen, nchunks = 1, R, seq_len // R
    else:
        assert R % seq_len == 0 and seq_len & (seq_len - 1) == 0 and n_seq % (R // seq_len) == 0
        seqs, clen, nchunks = R // seq_len, seq_len, 1
    ntiles = n_seq // seqs

    def rows(width):
        return pl.BlockSpec((R, width), lambda i, c: (i * nchunks + c, 0))

    wkv_spec = pl.BlockSpec((seqs, A_HEADS, A_HEAD_DIM, A_HEAD_DIM), lambda i, c: (i, 0, 0, 0))
    ret_spec = pl.BlockSpec((seqs, B_HEADS, B_QK_DIM, B_V_DIM), lambda i, c: (i, 0, 0, 0))
    ya, ob, wkv1, ret1 = pl.pallas_call(
        functools.partial(_mixer_kernel, seqs, clen),
        grid=(ntiles, nchunks),
        in_specs=[rows(A_WIDTH)] * 6 + [rows(B_QK_WIDTH), rows(B_QK_WIDTH), rows(B_WIDTH), wkv_spec, ret_spec],
        out_specs=[rows(A_WIDTH), rows(B_WIDTH), wkv_spec, ret_spec],
        out_shape=[jax.ShapeDtypeStruct((n, A_WIDTH), F32), jax.ShapeDtypeStruct((n, B_WIDTH), F32),
                   jax.ShapeDtypeStruct(wkv0.shape, F32), jax.ShapeDtypeStruct(ret0.shape, F32)],
        scratch_shapes=[pltpu.VMEM((A_HEADS, A_HEAD_DIM, seqs * A_HEAD_DIM), F32),
                        pltpu.VMEM((B_HEADS, seqs * B_QK_DIM, B_V_DIM), F32)],
        compiler_params=pltpu.CompilerParams(dimension_semantics=("parallel", "arbitrary"),
                                             vmem_limit_bytes=V7X_VMEM_LIMIT_BYTES),
        name="mixer",
    )(r, lw, k, v, al, be, qb, kb, vb, wkv0, ret0)
    return ya, ob, wkv1, ret1


def _layer_norm(z, g, b):
    mu = jnp.mean(z, axis=-1, keepdims=True)
    d = z - mu
    var = jnp.mean(d * d, axis=-1, keepdims=True)
    return d * lax.rsqrt(var + LN_EPS) * g + b


def _post_kernel(ya_ref, ob_ref, bonus_ref, g_ref, gb_ref, x_ref, lnxg_ref, lnxb_ref, rgg_ref, rgb_ref,
                 wout_ref, ln1g_ref, ln1b_ref, wr_ref, br_ref, h_o, gate_o):
    tm = x_ref.shape[0]

    def head_norm(t, group, eps, gg, bb):
        ones = _group_ones(t.shape[1], group)
        mu = _group_sum(t, ones) * (1.0 / group)
        d = t - mu
        var = _group_sum(d * d, ones) * (1.0 / group)
        return d * lax.rsqrt(var + eps) * gg + bb

    y_a = (head_norm(ya_ref[...], A_HEAD_DIM, GN_EPS_RWKV, lnxg_ref[...], lnxb_ref[...]) + bonus_ref[...]) * g_ref[...]
    y_b = head_norm(ob_ref[...], B_V_DIM, GN_EPS, rgg_ref[...], rgb_ref[...]) * gb_ref[...]
    y = jnp.concatenate([y_a, y_b], axis=1)
    mix = _dot(y, wout_ref[...])
    h = _layer_norm(DEEPNORM_ALPHA * x_ref[...] + mix, ln1g_ref[...], ln1b_ref[...])
    h_o[...] = h

    h_hi, h_lo = _split2(h)
    w_hi, w_lo = _split2(wr_ref[...])
    logits = (jnp.dot(h_hi, w_hi, preferred_element_type=F32) + jnp.dot(h_hi, w_lo, preferred_element_type=F32)
              + jnp.dot(h_lo, w_hi, preferred_element_type=F32)) + br_ref[...]
    lane = lax.broadcasted_iota(jnp.int32, (tm, ROUTER_LANES), 1)
    neg = -jnp.inf
    big = ROUTER_LANES
    cl = jnp.where(lane < N_GROUPS, logits, neg)
    cmax = jnp.max(cl, axis=-1, keepdims=True)
    grp = jnp.min(jnp.where(cl == cmax, lane, big), axis=-1, keepdims=True)
    gprob = 1.0 / jnp.sum(jnp.exp(cl - cmax), axis=-1, keepdims=True)
    lo_lane = FINE_LANE0 + grp * EXPERTS_PER_GROUP
    fv = jnp.where((lane >= lo_lane) & (lane < lo_lane + EXPERTS_PER_GROUP), logits, neg)
    m1 = jnp.max(fv, axis=-1, keepdims=True)
    i1 = jnp.min(jnp.where(fv == m1, lane, big), axis=-1, keepdims=True)
    fv2 = jnp.where(lane == i1, neg, fv)
    m2 = jnp.max(fv2, axis=-1, keepdims=True)
    i2 = jnp.min(jnp.where(fv2 == m2, lane, big), axis=-1, keepdims=True)
    e2 = jnp.exp(m2 - m1)
    w1 = gprob / (1.0 + e2)
    w2 = gprob * e2 / (1.0 + e2)
    gate_o[...] = jnp.where(lane == i1, w1, 0.0) + jnp.where(lane == i2, w2, 0.0)


def _post(ya, ob, bonus, g, gb, x2, W):
    n = x2.shape[0]
    tm = POST_ROWS
    assert n % tm == 0

    def full(a):
        return pl.BlockSpec(a.shape, lambda i: (0,) * a.ndim)

    def rows(width):
        return pl.BlockSpec((tm, width), lambda i: (i, 0))

    params = [W["lnx_g"], W["lnx_b"], W["ret_gn_g"], W["ret_gn_b"], W["w_out"], W["ln1_g"], W["ln1_b"],
              W["w_router"], W["b_router"]]
    return pl.pallas_call(
        _post_kernel,
        grid=(n // tm,),
        in_specs=[rows(A_WIDTH), rows(B_WIDTH), rows(A_WIDTH), rows(A_WIDTH), rows(B_WIDTH), rows(D_MODEL)]
        + [full(a) for a in params],
        out_specs=[rows(D_MODEL), rows(ROUTER_LANES)],
        out_shape=[jax.ShapeDtypeStruct((n, D_MODEL), F32), jax.ShapeDtypeStruct((n, ROUTER_LANES), F32)],
        compiler_params=pltpu.CompilerParams(dimension_semantics=("parallel",),
                                             vmem_limit_bytes=V7X_VMEM_LIMIT_BYTES),
        name="post",
    )(ya, ob, bonus, g, gb, x2, *params)


def _ffn_kernel(h_ref, gate_ref, p_ref, w1_ref, w3_ref, w2_ref, ln2g_ref, ln2b_ref, wple_ref, wpg_ref, pleg_ref,
                o_ref, hb_scr, acc_scr):
    tm = h_ref.shape[0]
    e_idx = pl.program_id(1)

    @pl.when(e_idx == 0)
    def _():
        hb_scr[...] = h_ref[...].astype(BF16)
        acc_scr[...] = jnp.zeros_like(acc_scr)

    hb = hb_scr[...]
    gate = gate_ref[...]
    lane = lax.broadcasted_iota(jnp.int32, gate.shape, 1)
    for j in range(FFN_EXPERTS_PER_STEP):
        e = e_idx * FFN_EXPERTS_PER_STEP + j
        ge = jnp.sum(jnp.where(lane == FINE_LANE0 + e, gate, 0.0), axis=-1, keepdims=True)
        a = jnp.dot(hb, w1_ref[j], preferred_element_type=F32)
        b = jnp.dot(hb, w3_ref[j], preferred_element_type=F32)
        hid = (a * _sigmoid(a)) * b * ge
        acc_scr[...] += jnp.dot(hid.astype(BF16), w2_ref[j], preferred_element_type=F32)

    @pl.when(e_idx == pl.num_programs(1) - 1)
    def _():
        h2 = _layer_norm(DEEPNORM_ALPHA * h_ref[...] + acc_scr[...], ln2g_ref[...], ln2b_ref[...])
        ple = _dot(p_ref[...], wple_ref[...]) * _sigmoid(_dot(h2, wpg_ref[...]))
        ms = jnp.mean(ple * ple, axis=-1, keepdims=True)
        o_ref[...] = h2 + ple * lax.rsqrt(ms + LN_EPS) * pleg_ref[...]


def _ffn(h, gate, p2, W):
    n = h.shape[0]
    tm = FFN_ROWS
    epb = FFN_EXPERTS_PER_STEP
    assert n % tm == 0 and N_EXPERTS % epb == 0

    def full(a):
        return pl.BlockSpec(a.shape, lambda i, e: (0,) * a.ndim)

    def rows(width):
        return pl.BlockSpec((tm, width), lambda i, e: (i, 0))

    params = [W["ln2_g"], W["ln2_b"], W["w_ple"], W["w_ple_gate"], W["ple_norm_g"]]
    return pl.pallas_call(
        _ffn_kernel,
        grid=(n // tm, N_EXPERTS // epb),
        in_specs=[rows(D_MODEL), rows(ROUTER_LANES), rows(D_PLE),
                  pl.BlockSpec((epb, D_MODEL, D_EXPERT), lambda i, e: (e, 0, 0)),
                  pl.BlockSpec((epb, D_MODEL, D_EXPERT), lambda i, e: (e, 0, 0)),
                  pl.BlockSpec((epb, D_EXPERT, D_MODEL), lambda i, e: (e, 0, 0))]
        + [full(a) for a in params],
        out_specs=rows(D_MODEL),
        out_shape=jax.ShapeDtypeStruct((n, D_MODEL), F32),
        scratch_shapes=[pltpu.VMEM((tm, D_MODEL), BF16), pltpu.VMEM((tm, D_MODEL), F32)],
        compiler_params=pltpu.CompilerParams(dimension_semantics=("parallel", "arbitrary"),
                                             vmem_limit_bytes=V7X_VMEM_LIMIT_BYTES),
        name="ffn",
    )(h, gate, p2, W["expert_w1"], W["expert_w3"], W["expert_w2"], *params)


def _layer(x, p, x_prev, wkv0, ret0, pos0, W):
    n_seq, seq_len, _ = x.shape
    n = n_seq * seq_len
    x2 = x.reshape(n, D_MODEL)
    r, lw, k, v, al, be, g, bonus, qb, kb, vb, gb = _proj(x2, x_prev, seq_len, pos0, W)
    ya, ob, wkv1, ret1 = _mixer((r, lw, k, v, al, be, qb, kb, vb), wkv0, ret0, n_seq, seq_len)
    h, gate = _post(ya, ob, bonus, g, gb, x2, W)
    out = _ffn(h, gate, p.reshape(n, D_PLE), W)
    return out.reshape(n_seq, seq_len, D_MODEL), x[:, -1], wkv1, ret1


def _prep_weights(i, w_in, mu_shift, w_decay_up, decay_base, w_aaa_up, aaa_base, w_gate_up, k_k, k_a, r_k,
                  lnx_g, lnx_b, ret_gn_g, ret_gn_b, w_out, ln1_g, ln1_b,
                  router_coarse_w, router_coarse_b, router_fine_w, router_fine_b,
                  expert_w1, expert_w3, expert_w2, ln2_g, ln2_b, w_ple, w_ple_gate, ple_norm_g):
    def row(a):
        return a[i].reshape(1, -1).astype(F32)

    pad = ROUTER_LANES - N_GROUPS - N_EXPERTS
    w_router = jnp.concatenate([router_coarse_w[i], router_fine_w[i], jnp.zeros((D_MODEL, pad), F32)], axis=1)
    b_router = jnp.concatenate([router_coarse_b[i], router_fine_b[i], jnp.zeros((pad,), F32)]).reshape(1, -1)
    return {
        "w_in": w_in[i].astype(BF16), "mu_shift": row(mu_shift), "w_decay_up": w_decay_up[i].astype(BF16),
        "decay_base": row(decay_base), "w_aaa_up": w_aaa_up[i].astype(BF16), "aaa_base": row(aaa_base),
        "w_gate_up": w_gate_up[i].astype(BF16), "k_k": row(k_k), "k_a": row(k_a), "r_k": row(r_k),
        "lnx_g": row(lnx_g), "lnx_b": row(lnx_b), "ret_gn_g": row(ret_gn_g), "ret_gn_b": row(ret_gn_b),
        "w_out": w_out[i].astype(BF16), "ln1_g": row(ln1_g), "ln1_b": row(ln1_b),
        "w_router": w_router, "b_router": b_router,
        "expert_w1": expert_w1[i].astype(BF16), "expert_w3": expert_w3[i].astype(BF16),
        "expert_w2": expert_w2[i].astype(BF16), "ln2_g": row(ln2_g), "ln2_b": row(ln2_b),
        "w_ple": w_ple[i].astype(BF16), "w_ple_gate": w_ple_gate[i].astype(BF16), "ple_norm_g": row(ple_norm_g),
    }


def kernel(x_prompt, x_sample, p_prompt, p_sample, state_wkv, state_shift, state_ret, w_in, mu_shift, w_decay_up, decay_base, w_aaa_up, aaa_base, w_gate_up, k_k, k_a, r_k, lnx_g, lnx_b, ret_gn_g, ret_gn_b, w_out, ln1_g, ln1_b, router_coarse_w, router_coarse_b, router_fine_w, router_fine_b, expert_w1, expert_w3, expert_w2, ln2_g, ln2_b, w_ple, w_ple_gate, ple_norm_g):
    yp, ys = x_prompt, x_sample
    nb = x_prompt.shape[0]
    depth = w_in.shape[0]
    wkv_p, shift_p, ret_p, wkv_s, shift_s, ret_s = [], [], [], [], [], []
    for i in range(depth):
        W = _prep_weights(i, w_in, mu_shift, w_decay_up, decay_base, w_aaa_up, aaa_base, w_gate_up, k_k, k_a, r_k,
                          lnx_g, lnx_b, ret_gn_g, ret_gn_b, w_out, ln1_g, ln1_b,
                          router_coarse_w, router_coarse_b, router_fine_w, router_fine_b,
                          expert_w1, expert_w3, expert_w2, ln2_g, ln2_b, w_ple, w_ple_gate, ple_norm_g)
        yp, sp, wp, rp = _layer(yp, p_prompt[i], jnp.zeros((nb, D_MODEL), F32),
                                jnp.zeros((nb, A_HEADS, A_HEAD_DIM, A_HEAD_DIM), F32),
                                jnp.zeros((nb, B_HEADS, B_QK_DIM, B_V_DIM), F32), 0, W)
        ys, ss, wsm, rsm = _layer(ys, p_sample[i], state_shift[i], state_wkv[i], state_ret[i], PAST_LEN, W)
        wkv_p.append(wp); shift_p.append(sp); ret_p.append(rp)
        wkv_s.append(wsm); shift_s.append(ss); ret_s.append(rsm)
    return (yp, ys, jnp.stack(wkv_p, 0), jnp.stack(shift_p, 0), jnp.stack(ret_p, 0),
            jnp.stack(wkv_s, 0), jnp.stack(shift_s, 0), jnp.stack(ret_s, 0))
```

```python
import functools
import math

import numpy as np
import jax
import jax.numpy as jnp
from jax import lax
from jax.experimental import pallas as pl
from jax.experimental.pallas import tpu as pltpu

F32 = jnp.float32
BF16 = jnp.bfloat16

D_MODEL = 1024
D_PLE = 256
A_HEADS = 8
A_HEAD_DIM = 64
A_WIDTH = A_HEADS * A_HEAD_DIM
DECAY_LORA = 64
AAA_LORA = 64
GATE_LORA = 128
GN_EPS_RWKV = 64e-5
B_HEADS = 4
B_QK_DIM = 64
B_V_DIM = 128
B_QK_WIDTH = B_HEADS * B_QK_DIM
B_WIDTH = B_HEADS * B_V_DIM
ROPE_BASE = 10000.0
GN_EPS = 1e-5
SHIFT_WIDTH = 3 * A_WIDTH + DECAY_LORA + AAA_LORA + GATE_LORA
IN_WIDTH = SHIFT_WIDTH + 2 * B_QK_WIDTH + 2 * B_WIDTH
N_GROUPS = 4
EXPERTS_PER_GROUP = 4
N_EXPERTS = N_GROUPS * EXPERTS_PER_GROUP
D_EXPERT = 256
DEPTH = 1
PAST_LEN = 16384
DEEPNORM_ALPHA = (2 * DEPTH) ** 0.25
LN_EPS = 1e-5

V7X_LANES = 128
V7X_SUBLANES = 8
V7X_VMEM_LIMIT_BYTES = 56 * 1024 * 1024

PROJ_ROWS = 256
CHUNK_ROWS = 64
POST_ROWS = 512
FFN_ROWS = 512
FFN_EXPERTS_PER_STEP = 2
ROUTER_LANES = V7X_LANES
FINE_LANE0 = N_GROUPS


def _dot(a, b):
    return jnp.dot(a.astype(BF16), b.astype(BF16), preferred_element_type=F32)


def _dot_nt(a, b):
    return lax.dot_general(a.astype(BF16), b.astype(BF16), (((1,), (1,)), ((), ())), preferred_element_type=F32)


def _dot_tn(a, b):
    return lax.dot_general(a.astype(BF16), b.astype(BF16), (((0,), (0,)), ((), ())), preferred_element_type=F32)


def _split2(x):
    hi = x.astype(BF16)
    lo = (x - hi.astype(F32)).astype(BF16)
    return hi, lo


def _split3(x):
    hi = x.astype(BF16)
    r1 = x - hi.astype(F32)
    mid = r1.astype(BF16)
    lo = (r1 - mid.astype(F32)).astype(BF16)
    return hi, mid, lo


def _sigmoid(x):
    return 1.0 / (1.0 + jnp.exp(-x))


def _group_ones(width, group):
    r = lax.broadcasted_iota(jnp.int32, (width, width), 0) // group
    c = lax.broadcasted_iota(jnp.int32, (width, width), 1) // group
    return jnp.where(r == c, 1.0, 0.0).astype(BF16)


def _group_sum(x, ones):
    hi, lo = _split2(x)
    return (jnp.dot(hi, ones, preferred_element_type=F32) + jnp.dot(lo, ones, preferred_element_type=F32))


def _proj_kernel(carry_mode, seq_len, tiles_per_seq,
                 x_ref, xp_ref, w_ref, mu_ref, wdec_ref, dbase_ref, waaa_ref, abase_ref, wgate_ref,
                 kk_ref, ka_ref, rk_ref, cos_ref, sin_ref,
                 r_o, lw_o, k_o, v_o, al_o, be_o, g_o, bonus_o, qb_o, kb_o, vb_o, gb_o,
                 carry_scr):
    tm = x_ref.shape[0]
    row = lax.broadcasted_iota(jnp.int32, (tm, SHIFT_WIDTH), 0)
    if carry_mode:
        xp = jnp.broadcast_to(xp_ref[0], (V7X_SUBLANES, D_MODEL))
        xb = jnp.concatenate([x_ref[...], xp], axis=0).astype(BF16)
        proj_all = jnp.dot(xb, w_ref[...], preferred_element_type=F32)
        proj = proj_all[:tm]
        cur_s = proj[:, :SHIFT_WIDTH]
        rolled = pltpu.roll(cur_s, 1, 0)
        j = pl.program_id(0) % tiles_per_seq

        @pl.when(pl.program_id(0) == 0)
        def _():
            carry_scr[...] = jnp.zeros_like(carry_scr)

        first = jnp.where(j == 0, proj_all[tm + V7X_SUBLANES - 1:, :SHIFT_WIDTH],
                          carry_scr[V7X_SUBLANES - 1:V7X_SUBLANES, :])
        prev = jnp.where(row == 0, first, rolled)
        carry_scr[...] = cur_s[tm - V7X_SUBLANES:, :]
    else:
        xb = x_ref[...].astype(BF16)
        proj = jnp.dot(xb, w_ref[...], preferred_element_type=F32)
        cur_s = proj[:, :SHIFT_WIDTH]
        rolled = pltpu.roll(cur_s, 1, 0)
        first = jnp.dot(xp_ref[...].astype(BF16), w_ref[:, :SHIFT_WIDTH], preferred_element_type=F32)
        prev = jnp.where((row & (seq_len - 1)) == 0, first, rolled)
    sh = cur_s + (prev - cur_s) * mu_ref[...]

    r = sh[:, :A_WIDTH]
    k0 = sh[:, A_WIDTH:2 * A_WIDTH]
    v = sh[:, 2 * A_WIDTH:3 * A_WIDTH]
    o = 3 * A_WIDTH
    w_lo = sh[:, o:o + DECAY_LORA]
    a_lo = sh[:, o + DECAY_LORA:o + DECAY_LORA + AAA_LORA]
    g_lo = sh[:, o + DECAY_LORA + AAA_LORA:SHIFT_WIDTH]

    z = -(dbase_ref[...] + _dot(jnp.tanh(w_lo), wdec_ref[...]))
    softplus = jnp.maximum(z, 0.0) + jnp.log1p(jnp.exp(-jnp.abs(z)))
    log_w = -softplus - 0.5
    lw = -jnp.exp(log_w)
    a = _sigmoid(abase_ref[...] + _dot(a_lo, waaa_ref[...]))
    g = _dot(_sigmoid(g_lo), wgate_ref[...])

    ones64 = _group_ones(A_WIDTH, A_HEAD_DIM)
    kk0 = k0 * kk_ref[...]
    ssq = _group_sum(kk0 * kk0, ones64)
    kk = kk0 / jnp.maximum(jnp.sqrt(ssq), 1e-12)
    k = k0 * (1.0 + (a - 1.0) * ka_ref[...])
    bonus = _group_sum(r * k * rk_ref[...], ones64) * v

    r_o[...] = r
    lw_o[...] = lw
    k_o[...] = k
    v_o[...] = v
    al_o[...] = -kk
    be_o[...] = kk * a
    g_o[...] = g
    bonus_o[...] = bonus

    o = SHIFT_WIDTH
    q_b = proj[:, o:o + B_QK_WIDTH]
    k_b = proj[:, o + B_QK_WIDTH:o + 2 * B_QK_WIDTH]
    v_b = proj[:, o + 2 * B_QK_WIDTH:o + 2 * B_QK_WIDTH + B_WIDTH]
    g_b = proj[:, o + 2 * B_QK_WIDTH + B_WIDTH:]
    lane = lax.broadcasted_iota(jnp.int32, (tm, B_QK_WIDTH), 1)
    first_half = (lane & (B_QK_DIM - 1)) < (B_QK_DIM // 2)
    cos = cos_ref[...]
    sin = sin_ref[...]

    def rot(t):
        swapped = jnp.where(first_half, pltpu.roll(t, B_QK_WIDTH - B_QK_DIM // 2, 1), pltpu.roll(t, B_QK_DIM // 2, 1))
        return t * cos + swapped * sin

    qb_o[...] = rot(q_b)
    kb_o[...] = rot(k_b) * (B_QK_DIM ** -0.5)
    vb_o[...] = v_b
    gb_o[...] = g_b * _sigmoid(g_b)


def _proj(x2, x_prev, seq_len, pos0, W):
    n = x2.shape[0]
    tm = PROJ_ROWS
    assert n % tm == 0
    carry_mode = seq_len % tm == 0
    if carry_mode:
        tiles_per_seq = seq_len // tm
        xp = x_prev.reshape(-1, 1, D_MODEL)
        xp_spec = pl.BlockSpec((1, 1, D_MODEL), lambda i: (i // tiles_per_seq, 0, 0))
        tab_rows = seq_len
    else:
        assert tm % seq_len == 0 and seq_len & (seq_len - 1) == 0
        tiles_per_seq = 1
        xp = jnp.repeat(x_prev, seq_len, axis=0)
        xp_spec = pl.BlockSpec((tm, D_MODEL), lambda i: (i, 0))
        tab_rows = tm
    half = B_QK_DIM // 2
    inv = ROPE_BASE ** (-jnp.arange(half, dtype=F32) / half)
    pos = (pos0 + jnp.arange(seq_len, dtype=jnp.int32)).astype(F32)
    ang = pos[:, None] * inv[None, :]
    cos = jnp.tile(jnp.concatenate([jnp.cos(ang), jnp.cos(ang)], -1), (tab_rows // seq_len, B_HEADS))
    sin = jnp.tile(jnp.concatenate([-jnp.sin(ang), jnp.sin(ang)], -1), (tab_rows // seq_len, B_HEADS))
    tab_tiles = tab_rows // tm
    tab_spec = pl.BlockSpec((tm, B_QK_WIDTH), lambda i: (i % tab_tiles, 0))

    def full(a):
        return pl.BlockSpec(a.shape, lambda i: (0,) * a.ndim)

    def rows(width):
        return pl.BlockSpec((tm, width), lambda i: (i, 0))

    params = [W["w_in"], W["mu_shift"], W["w_decay_up"], W["decay_base"], W["w_aaa_up"], W["aaa_base"],
              W["w_gate_up"], W["k_k"], W["k_a"], W["r_k"]]
    widths = [A_WIDTH] * 8 + [B_QK_WIDTH, B_QK_WIDTH, B_WIDTH, B_WIDTH]
    outs = pl.pallas_call(
        functools.partial(_proj_kernel, carry_mode, seq_len, tiles_per_seq),
        grid=(n // tm,),
        in_specs=[rows(D_MODEL), xp_spec] + [full(a) for a in params] + [tab_spec, tab_spec],
        out_specs=[rows(w) for w in widths],
        out_shape=[jax.ShapeDtypeStruct((n, w), F32) for w in widths],
        scratch_shapes=[pltpu.VMEM((V7X_SUBLANES, SHIFT_WIDTH), F32)],
        compiler_params=pltpu.CompilerParams(dimension_semantics=("arbitrary",),
                                             vmem_limit_bytes=V7X_VMEM_LIMIT_BYTES),
        name="proj",
    )(x2, xp, *params, cos, sin)
    return outs


def _mixer_kernel(seqs, clen,
                  r_ref, lw_ref, k_ref, v_ref, al_ref, be_ref, qb_ref, kb_ref, vb_ref, wkv0_ref, ret0_ref,
                  ya_o, ob_o, wkv_o, ret_o, s_scr, r_scr):
    R = seqs * clen
    log2c = int(math.log2(clen))
    c_idx = pl.program_id(1)
    hd = A_HEAD_DIM

    @pl.when(c_idx == 0)
    def _():
        for h in range(A_HEADS):
            s_scr[h] = jnp.concatenate([wkv0_ref[i, h] for i in range(seqs)], axis=1) if seqs > 1 else wkv0_ref[0, h]
        for h in range(B_HEADS):
            r_scr[h] = ret0_ref[:, h].reshape(seqs * B_QK_DIM, B_V_DIM)

    row = lax.broadcasted_iota(jnp.int32, (R, R), 0)
    col = lax.broadcasted_iota(jnp.int32, (R, R), 1)
    same = (row >> log2c) == (col >> log2c)
    incl = same & (col <= row)
    strict = same & (col < row)
    m_incl = jnp.where(incl, 1.0, 0.0).astype(BF16)
    m_same = jnp.where(same, 1.0, 0.0).astype(BF16)
    eye = jnp.where(row == col, 1.0, 0.0).astype(F32)

    def expand(t):
        if seqs == 1:
            return t
        w = t.shape[1]
        wide = jnp.concatenate([t] * seqs, axis=1)
        rr = lax.broadcasted_iota(jnp.int32, wide.shape, 0) >> log2c
        cc = lax.broadcasted_iota(jnp.int32, wide.shape, 1) // w
        return jnp.where(rr == cc, wide, 0.0)

    lw = lw_ref[...]
    parts = _split3(lw)
    c = sum(jnp.dot(m_incl, p, preferred_element_type=F32) for p in parts)
    cend = sum(jnp.dot(m_same, p, preferred_element_type=F32) for p in parts)
    r = r_ref[...]
    k = k_ref[...]
    v = v_ref[...]
    al = al_ref[...]
    be = be_ref[...]
    a_bar = al * jnp.exp(c - lw)
    r_bar = r * jnp.exp(c)
    einv = jnp.exp(-c)
    b_til = be * einv
    k_til = k * einv
    edec = jnp.exp(cend - c)
    b_dec = be * edec
    k_dec = k * edec
    d_end = jnp.exp(cend)
    last_row = (lax.broadcasted_iota(jnp.int32, (R, seqs * hd), 0) & (clen - 1)) == clen - 1

    H = range(A_HEADS)
    sls = [slice(h * hd, (h + 1) * hd) for h in H]
    amats = [_dot_nt(jnp.concatenate([a_bar[:, sl], r_bar[:, sl]], axis=0),
                     jnp.concatenate([b_til[:, sl], k_til[:, sl]], axis=0)) for sl in sls]

    qb = qb_ref[...]
    kb = kb_ref[...]
    vb = vb_ref[...]
    G = range(B_HEADS)
    lgs = [float(np.log1p(-np.exp2(-5.0 - h))) for h in G]
    qs = [qb[:, h * B_QK_DIM:(h + 1) * B_QK_DIM] for h in G]
    khs = [kb[:, h * B_QK_DIM:(h + 1) * B_QK_DIM] for h in G]
    vbs = [vb[:, h * B_V_DIM:(h + 1) * B_V_DIM] for h in G]
    diff = (row - col).astype(F32)
    pos_v = (lax.broadcasted_iota(jnp.int32, (R, B_V_DIM), 0) & (clen - 1)).astype(F32)
    pos_k = (lax.broadcasted_iota(jnp.int32, (R, B_QK_DIM), 0) & (clen - 1)).astype(F32)
    rstates = [r_scr[h] for h in G]
    scs = [_dot_nt(qs[h], khs[h]) * jnp.where(incl, jnp.exp(lgs[h] * diff), 0.0) for h in G]
    qst = [_dot(expand(qs[h]), rstates[h]) * jnp.exp(lgs[h] * (pos_v + 1.0)) for h in G]

    a_ab = [jnp.where(strict, m[:R, :R], 0.0) for m in amats]
    a_ak = [jnp.where(strict, m[:R, R:], 0.0) for m in amats]
    a_rb = [jnp.where(incl, m[R:, :R], 0.0) for m in amats]
    a_rk = [jnp.where(incl, m[R:, R:], 0.0) for m in amats]
    tinv = [eye + a for a in a_ab]
    vhs = [v[:, sl] for sl in sls]
    av = [_dot(a_ak[h], vhs[h]) for h in H]
    if log2c > 1:
        pw = [_dot(a, a) for a in a_ab]
    for it in range(log2c - 1):
        if it < log2c - 2:
            tp = [_dot(pw[h], jnp.concatenate([tinv[h], pw[h]], axis=1)) for h in H]
            tinv = [tinv[h] + tp[h][:, :R] for h in H]
            pw = [tp[h][:, R:] for h in H]
        else:
            tinv = [tinv[h] + _dot(pw[h], tinv[h]) for h in H]

    os_ = [_dot(scs[h], vbs[h]) + qst[h] for h in G]
    for h in G:
        kdec = jnp.exp(lgs[h] * (clen - 1.0 - pos_k))
        r_scr[h] = rstates[h] * float(np.exp(lgs[h] * clen)) + _dot_tn(expand(khs[h] * kdec), vbs[h])
    ob_o[...] = jnp.concatenate(os_, axis=1)

    wu = [_dot(tinv[h], jnp.concatenate([a_bar[:, sls[h]], av[h]], axis=1)) for h in H]
    states = [s_scr[h] for h in H]
    ws = [_dot_nt(jnp.concatenate([expand(wu[h][:, :hd]), expand(r_bar[:, sls[h]])], axis=0), states[h]) for h in H]
    uv = [jnp.concatenate([ws[h][:R] + wu[h][:, hd:], vhs[h]], axis=0) for h in H]
    ys = [ws[h][R:] + _dot(jnp.concatenate([a_rb[h], a_rk[h]], axis=1), uv[h]) for h in H]
    ya_o[...] = jnp.concatenate(ys, axis=1)
    for h in H:
        d_row = jnp.sum(jnp.where(last_row, expand(d_end[:, sls[h]]), 0.0), axis=0, keepdims=True)
        bk = jnp.concatenate([expand(b_dec[:, sls[h]]), expand(k_dec[:, sls[h]])], axis=0)
        s_scr[h] = states[h] * d_row + _dot_tn(uv[h], bk)

    @pl.when(c_idx == pl.num_programs(1) - 1)
    def _():
        for h in range(A_HEADS):
            st = s_scr[h]
            for i in range(seqs):
                wkv_o[i, h] = st[:, i * hd:(i + 1) * hd]
        for h in range(B_HEADS):
            ret_o[:, h] = r_scr[h].reshape(seqs, B_QK_DIM, B_V_DIM)


def _mixer(ops, wkv0, ret0, n_seq, seq_len):
    r, lw, k, v, al, be, qb, kb, vb = ops
    n = r.shape[0]
    R = CHUNK_ROWS
    if seq_len >= R:
        assert seq_len % R == 0
        seqs, clen, nchunks = 1, R, seq_len // R
    else:
        assert R % seq_len == 0 and seq_len & (seq_len - 1) == 0 and n_seq % (R // seq_len) == 0
        seqs, clen, nchunks = R // seq_len, seq_len, 1
    ntiles = n_seq // seqs

    def rows(width):
        return pl.BlockSpec((R, width), lambda i, c: (i * nchunks + c, 0))

    wkv_spec = pl.BlockSpec((seqs, A_HEADS, A_HEAD_DIM, A_HEAD_DIM), lambda i, c: (i, 0, 0, 0))
    ret_spec = pl.BlockSpec((seqs, B_HEADS, B_QK_DIM, B_V_DIM), lambda i, c: (i, 0, 0, 0))
    ya, ob, wkv1, ret1 = pl.pallas_call(
        functools.partial(_mixer_kernel, seqs, clen),
        grid=(ntiles, nchunks),
        in_specs=[rows(A_WIDTH)] * 6 + [rows(B_QK_WIDTH), rows(B_QK_WIDTH), rows(B_WIDTH), wkv_spec, ret_spec],
        out_specs=[rows(A_WIDTH), rows(B_WIDTH), wkv_spec, ret_spec],
        out_shape=[jax.ShapeDtypeStruct((n, A_WIDTH), F32), jax.ShapeDtypeStruct((n, B_WIDTH), F32),
                   jax.ShapeDtypeStruct(wkv0.shape, F32), jax.ShapeDtypeStruct(ret0.shape, F32)],
        scratch_shapes=[pltpu.VMEM((A_HEADS, A_HEAD_DIM, seqs * A_HEAD_DIM), F32),
                        pltpu.VMEM((B_HEADS, seqs * B_QK_DIM, B_V_DIM), F32)],
        compiler_params=pltpu.CompilerParams(dimension_semantics=("parallel", "arbitrary"),
                                             vmem_limit_bytes=V7X_VMEM_LIMIT_BYTES),
        name="mixer",
    )(r, lw, k, v, al, be, qb, kb, vb, wkv0, ret0)
    return ya, ob, wkv1, ret1


def _layer_norm(z, g, b):
    mu = jnp.mean(z, axis=-1, keepdims=True)
    d = z - mu
    var = jnp.mean(d * d, axis=-1, keepdims=True)
    return d * lax.rsqrt(var + LN_EPS) * g + b


def _post_kernel(ya_ref, ob_ref, bonus_ref, g_ref, gb_ref, x_ref, lnxg_ref, lnxb_ref, rgg_ref, rgb_ref,
                 wout_ref, ln1g_ref, ln1b_ref, wr_ref, br_ref, h_o, gate_o):
    tm = x_ref.shape[0]

    def head_norm(t, group, eps, gg, bb):
        ones = _group_ones(t.shape[1], group)
        mu = _group_sum(t, ones) * (1.0 / group)
        d = t - mu
        var = _group_sum(d * d, ones) * (1.0 / group)
        return d * lax.rsqrt(var + eps) * gg + bb

    y_a = (head_norm(ya_ref[...], A_HEAD_DIM, GN_EPS_RWKV, lnxg_ref[...], lnxb_ref[...]) + bonus_ref[...]) * g_ref[...]
    y_b = head_norm(ob_ref[...], B_V_DIM, GN_EPS, rgg_ref[...], rgb_ref[...]) * gb_ref[...]
    y = jnp.concatenate([y_a, y_b], axis=1)
    mix = _dot(y, wout_ref[...])
    h = _layer_norm(DEEPNORM_ALPHA * x_ref[...] + mix, ln1g_ref[...], ln1b_ref[...])
    h_o[...] = h

    h_hi, h_lo = _split2(h)
    w_hi, w_lo = _split2(wr_ref[...])
    logits = (jnp.dot(h_hi, w_hi, preferred_element_type=F32) + jnp.dot(h_hi, w_lo, preferred_element_type=F32)
              + jnp.dot(h_lo, w_hi, preferred_element_type=F32)) + br_ref[...]
    lane = lax.broadcasted_iota(jnp.int32, (tm, ROUTER_LANES), 1)
    neg = -jnp.inf
    big = ROUTER_LANES
    cl = jnp.where(lane < N_GROUPS, logits, neg)
    cmax = jnp.max(cl, axis=-1, keepdims=True)
    grp = jnp.min(jnp.where(cl == cmax, lane, big), axis=-1, keepdims=True)
    gprob = 1.0 / jnp.sum(jnp.exp(cl - cmax), axis=-1, keepdims=True)
    lo_lane = FINE_LANE0 + grp * EXPERTS_PER_GROUP
    fv = jnp.where((lane >= lo_lane) & (lane < lo_lane + EXPERTS_PER_GROUP), logits, neg)
    m1 = jnp.max(fv, axis=-1, keepdims=True)
    i1 = jnp.min(jnp.where(fv == m1, lane, big), axis=-1, keepdims=True)
    fv2 = jnp.where(lane == i1, neg, fv)
    m2 = jnp.max(fv2, axis=-1, keepdims=True)
    i2 = jnp.min(jnp.where(fv2 == m2, lane, big), axis=-1, keepdims=True)
    e2 = jnp.exp(m2 - m1)
    w1 = gprob / (1.0 + e2)
    w2 = gprob * e2 / (1.0 + e2)
    gate_o[...] = jnp.where(lane == i1, w1, 0.0) + jnp.where(lane == i2, w2, 0.0)


def _post(ya, ob, bonus, g, gb, x2, W):
    n = x2.shape[0]
    tm = POST_ROWS
    assert n % tm == 0

    def full(a):
        return pl.BlockSpec(a.shape, lambda i: (0,) * a.ndim)

    def rows(width):
        return pl.BlockSpec((tm, width), lambda i: (i, 0))

    params = [W["lnx_g"], W["lnx_b"], W["ret_gn_g"], W["ret_gn_b"], W["w_out"], W["ln1_g"], W["ln1_b"],
              W["w_router"], W["b_router"]]
    return pl.pallas_call(
        _post_kernel,
        grid=(n // tm,),
        in_specs=[rows(A_WIDTH), rows(B_WIDTH), rows(A_WIDTH), rows(A_WIDTH), rows(B_WIDTH), rows(D_MODEL)]
        + [full(a) for a in params],
        out_specs=[rows(D_MODEL), rows(ROUTER_LANES)],
        out_shape=[jax.ShapeDtypeStruct((n, D_MODEL), F32), jax.ShapeDtypeStruct((n, ROUTER_LANES), F32)],
        compiler_params=pltpu.CompilerParams(dimension_semantics=("parallel",),
                                             vmem_limit_bytes=V7X_VMEM_LIMIT_BYTES),
        name="post",
    )(ya, ob, bonus, g, gb, x2, *params)


def _ffn_kernel(h_ref, gate_ref, p_ref, w1_ref, w3_ref, w2_ref, ln2g_ref, ln2b_ref, wple_ref, wpg_ref, pleg_ref,
                o_ref, hb_scr, acc_scr):
    tm = h_ref.shape[0]
    e_idx = pl.program_id(1)

    @pl.when(e_idx == 0)
    def _():
        hb_scr[...] = h_ref[...].astype(BF16)
        acc_scr[...] = jnp.zeros_like(acc_scr)

    hb = hb_scr[...]
    gate = gate_ref[...]
    lane = lax.broadcasted_iota(jnp.int32, gate.shape, 1)
    for j in range(FFN_EXPERTS_PER_STEP):
        e = e_idx * FFN_EXPERTS_PER_STEP + j
        ge = jnp.sum(jnp.where(lane == FINE_LANE0 + e, gate, 0.0), axis=-1, keepdims=True)
        a = jnp.dot(hb, w1_ref[j], preferred_element_type=F32)
        b = jnp.dot(hb, w3_ref[j], preferred_element_type=F32)
        hid = (a * _sigmoid(a)) * b * ge
        acc_scr[...] += jnp.dot(hid.astype(BF16), w2_ref[j], preferred_element_type=F32)

    @pl.when(e_idx == pl.num_programs(1) - 1)
    def _():
        h2 = _layer_norm(DEEPNORM_ALPHA * h_ref[...] + acc_scr[...], ln2g_ref[...], ln2b_ref[...])
        ple = _dot(p_ref[...], wple_ref[...]) * _sigmoid(_dot(h2, wpg_ref[...]))
        ms = jnp.mean(ple * ple, axis=-1, keepdims=True)
        o_ref[...] = h2 + ple * lax.rsqrt(ms + LN_EPS) * pleg_ref[...]


def _ffn(h, gate, p2, W):
    n = h.shape[0]
    tm = FFN_ROWS
    epb = FFN_EXPERTS_PER_STEP
    assert n % tm == 0 and N_EXPERTS % epb == 0

    def full(a):
        return pl.BlockSpec(a.shape, lambda i, e: (0,) * a.ndim)

    def rows(width):
        return pl.BlockSpec((tm, width), lambda i, e: (i, 0))

    params = [W["ln2_g"], W["ln2_b"], W["w_ple"], W["w_ple_gate"], W["ple_norm_g"]]
    return pl.pallas_call(
        _ffn_kernel,
        grid=(n // tm, N_EXPERTS // epb),
        in_specs=[rows(D_MODEL), rows(ROUTER_LANES), rows(D_PLE),
                  pl.BlockSpec((epb, D_MODEL, D_EXPERT), lambda i, e: (e, 0, 0)),
                  pl.BlockSpec((epb, D_MODEL, D_EXPERT), lambda i, e: (e, 0, 0)),
                  pl.BlockSpec((epb, D_EXPERT, D_MODEL), lambda i, e: (e, 0, 0))]
        + [full(a) for a in params],
        out_specs=rows(D_MODEL),
        out_shape=jax.ShapeDtypeStruct((n, D_MODEL), F32),
        scratch_shapes=[pltpu.VMEM((tm, D_MODEL), BF16), pltpu.VMEM((tm, D_MODEL), F32)],
        compiler_params=pltpu.CompilerParams(dimension_semantics=("parallel", "arbitrary"),
                                             vmem_limit_bytes=V7X_VMEM_LIMIT_BYTES),
        name="ffn",
    )(h, gate, p2, W["expert_w1"], W["expert_w3"], W["expert_w2"], *params)


def _layer(x, p, x_prev, wkv0, ret0, pos0, W):
    n_seq, seq_len, _ = x.shape
    n = n_seq * seq_len
    x2 = x.reshape(n, D_MODEL)
    r, lw, k, v, al, be, g, bonus, qb, kb, vb, gb = _proj(x2, x_prev, seq_len, pos0, W)
    ya, ob, wkv1, ret1 = _mixer((r, lw, k, v, al, be, qb, kb, vb), wkv0, ret0, n_seq, seq_len)
    h, gate = _post(ya, ob, bonus, g, gb, x2, W)
    out = _ffn(h, gate, p.reshape(n, D_PLE), W)
    return out.reshape(n_seq, seq_len, D_MODEL), x[:, -1], wkv1, ret1


def _prep_weights(i, w_in, mu_shift, w_decay_up, decay_base, w_aaa_up, aaa_base, w_gate_up, k_k, k_a, r_k,
                  lnx_g, lnx_b, ret_gn_g, ret_gn_b, w_out, ln1_g, ln1_b,
                  router_coarse_w, router_coarse_b, router_fine_w, router_fine_b,
                  expert_w1, expert_w3, expert_w2, ln2_g, ln2_b, w_ple, w_ple_gate, ple_norm_g):
    def row(a):
        return a[i].reshape(1, -1).astype(F32)

    pad = ROUTER_LANES - N_GROUPS - N_EXPERTS
    w_router = jnp.concatenate([router_coarse_w[i], router_fine_w[i], jnp.zeros((D_MODEL, pad), F32)], axis=1)
    b_router = jnp.concatenate([router_coarse_b[i], router_fine_b[i], jnp.zeros((pad,), F32)]).reshape(1, -1)
    return {
        "w_in": w_in[i].astype(BF16), "mu_shift": row(mu_shift), "w_decay_up": w_decay_up[i].astype(BF16),
        "decay_base": row(decay_base), "w_aaa_up": w_aaa_up[i].astype(BF16), "aaa_base": row(aaa_base),
        "w_gate_up": w_gate_up[i].astype(BF16), "k_k": row(k_k), "k_a": row(k_a), "r_k": row(r_k),
        "lnx_g": row(lnx_g), "lnx_b": row(lnx_b), "ret_gn_g": row(ret_gn_g), "ret_gn_b": row(ret_gn_b),
        "w_out": w_out[i].astype(BF16), "ln1_g": row(ln1_g), "ln1_b": row(ln1_b),
        "w_router": w_router, "b_router": b_router,
        "expert_w1": expert_w1[i].astype(BF16), "expert_w3": expert_w3[i].astype(BF16),
        "expert_w2": expert_w2[i].astype(BF16), "ln2_g": row(ln2_g), "ln2_b": row(ln2_b),
        "w_ple": w_ple[i].astype(BF16), "w_ple_gate": w_ple_gate[i].astype(BF16), "ple_norm_g": row(ple_norm_g),
    }


def kernel(x_prompt, x_sample, p_prompt, p_sample, state_wkv, state_shift, state_ret, w_in, mu_shift, w_decay_up, decay_base, w_aaa_up, aaa_base, w_gate_up, k_k, k_a, r_k, lnx_g, lnx_b, ret_gn_g, ret_gn_b, w_out, ln1_g, ln1_b, router_coarse_w, router_coarse_b, router_fine_w, router_fine_b, expert_w1, expert_w3, expert_w2, ln2_g, ln2_b, w_ple, w_ple_gate, ple_norm_g):
    yp, ys = x_prompt, x_sample
    nb = x_prompt.shape[0]
    depth = w_in.shape[0]
    wkv_p, shift_p, ret_p, wkv_s, shift_s, ret_s = [], [], [], [], [], []
    for i in range(depth):
        W = _prep_weights(i, w_in, mu_shift, w_decay_up, decay_base, w_aaa_up, aaa_base, w_gate_up, k_k, k_a, r_k,
                          lnx_g, lnx_b, ret_gn_g, ret_gn_b, w_out, ln1_g, ln1_b,
                          router_coarse_w, router_coarse_b, router_fine_w, router_fine_b,
                          expert_w1, expert_w3, expert_w2, ln2_g, ln2_b, w_ple, w_ple_gate, ple_norm_g)
        yp, sp, wp, rp = _layer(yp, p_prompt[i], jnp.zeros((nb, D_MODEL), F32),
                                jnp.zeros((nb, A_HEADS, A_HEAD_DIM, A_HEAD_DIM), F32),
                                jnp.zeros((nb, B_HEADS, B_QK_DIM, B_V_DIM), F32), 0, W)
        ys, ss, wsm, rsm = _layer(ys, p_sample[i], state_shift[i], state_wkv[i], state_ret[i], PAST_LEN, W)
        wkv_p.append(wp); shift_p.append(sp); ret_p.append(rp)
        wkv_s.append(wsm); shift_s.append(ss); ret_s.append(rsm)
    return (yp, ys, jnp.stack(wkv_p, 0), jnp.stack(shift_p, 0), jnp.stack(ret_p, 0),
            jnp.stack(wkv_s, 0), jnp.stack(shift_s, 0), jnp.stack(ret_s, 0))
```

```python
import functools
import math

import numpy as np
import jax
import jax.numpy as jnp
from jax import lax
from jax.experimental import pallas as pl
from jax.experimental.pallas import tpu as pltpu

F32 = jnp.float32
BF16 = jnp.bfloat16

D_MODEL = 1024
D_PLE = 256
A_HEADS = 8
A_HEAD_DIM = 64
A_WIDTH = A_HEADS * A_HEAD_DIM
DECAY_LORA = 64
AAA_LORA = 64
GATE_LORA = 128
GN_EPS_RWKV = 64e-5
B_HEADS = 4
B_QK_DIM = 64
B_V_DIM = 128
B_QK_WIDTH = B_HEADS * B_QK_DIM
B_WIDTH = B_HEADS * B_V_DIM
ROPE_BASE = 10000.0
GN_EPS = 1e-5
SHIFT_WIDTH = 3 * A_WIDTH + DECAY_LORA + AAA_LORA + GATE_LORA
IN_WIDTH = SHIFT_WIDTH + 2 * B_QK_WIDTH + 2 * B_WIDTH
N_GROUPS = 4
EXPERTS_PER_GROUP = 4
N_EXPERTS = N_GROUPS * EXPERTS_PER_GROUP
D_EXPERT = 256
DEPTH = 1
PAST_LEN = 16384
DEEPNORM_ALPHA = (2 * DEPTH) ** 0.25
LN_EPS = 1e-5

V7X_LANES = 128
V7X_SUBLANES = 8
V7X_VMEM_LIMIT_BYTES = 56 * 1024 * 1024

PROJ_ROWS = 256
CHUNK_ROWS = 64
MIXER_TILES = 4
MIXER_TILES_PACKED = 2
POST_ROWS = 512
FFN_ROWS = 512
FFN_EXPERTS_PER_STEP = 2
ROUTER_LANES = V7X_LANES
FINE_LANE0 = N_GROUPS


def _dot(a, b):
    return jnp.dot(a.astype(BF16), b.astype(BF16), preferred_element_type=F32)


def _dot_nt(a, b):
    return lax.dot_general(a.astype(BF16), b.astype(BF16), (((1,), (1,)), ((), ())), preferred_element_type=F32)


def _dot_tn(a, b):
    return lax.dot_general(a.astype(BF16), b.astype(BF16), (((0,), (0,)), ((), ())), preferred_element_type=F32)


def _split2(x):
    hi = x.astype(BF16)
    lo = (x - hi.astype(F32)).astype(BF16)
    return hi, lo


def _split3(x):
    hi = x.astype(BF16)
    r1 = x - hi.astype(F32)
    mid = r1.astype(BF16)
    lo = (r1 - mid.astype(F32)).astype(BF16)
    return hi, mid, lo


def _sigmoid(x):
    return 1.0 / (1.0 + jnp.exp(-x))


def _group_ones(width, group):
    r = lax.broadcasted_iota(jnp.int32, (width, width), 0) // group
    c = lax.broadcasted_iota(jnp.int32, (width, width), 1) // group
    return jnp.where(r == c, 1.0, 0.0).astype(BF16)


def _group_sum(x, ones):
    hi, lo = _split2(x)
    return (jnp.dot(hi, ones, preferred_element_type=F32) + jnp.dot(lo, ones, preferred_element_type=F32))


def _proj_kernel(carry_mode, seq_len, tiles_per_seq,
                 x_ref, xp_ref, w_ref, mu_ref, wdec_ref, dbase_ref, waaa_ref, abase_ref, wgate_ref,
                 kk_ref, ka_ref, rk_ref, cos_ref, sin_ref,
                 r_o, lw_o, k_o, v_o, al_o, be_o, g_o, bonus_o, qb_o, kb_o, vb_o, gb_o,
                 carry_scr):
    tm = x_ref.shape[0]
    row = lax.broadcasted_iota(jnp.int32, (tm, SHIFT_WIDTH), 0)
    if carry_mode:
        xp = jnp.broadcast_to(xp_ref[0], (V7X_SUBLANES, D_MODEL))
        xb = jnp.concatenate([x_ref[...], xp], axis=0).astype(BF16)
        proj_all = jnp.dot(xb, w_ref[...], preferred_element_type=F32)
        proj = proj_all[:tm]
        cur_s = proj[:, :SHIFT_WIDTH]
        rolled = pltpu.roll(cur_s, 1, 0)
        j = pl.program_id(0) % tiles_per_seq

        @pl.when(pl.program_id(0) == 0)
        def _():
            carry_scr[...] = jnp.zeros_like(carry_scr)

        first = jnp.where(j == 0, proj_all[tm + V7X_SUBLANES - 1:, :SHIFT_WIDTH],
                          carry_scr[V7X_SUBLANES - 1:V7X_SUBLANES, :])
        prev = jnp.where(row == 0, first, rolled)
        carry_scr[...] = cur_s[tm - V7X_SUBLANES:, :]
    else:
        xb = x_ref[...].astype(BF16)
        proj = jnp.dot(xb, w_ref[...], preferred_element_type=F32)
        cur_s = proj[:, :SHIFT_WIDTH]
        rolled = pltpu.roll(cur_s, 1, 0)
        first = jnp.dot(xp_ref[...].astype(BF16), w_ref[:, :SHIFT_WIDTH], preferred_element_type=F32)
        prev = jnp.where((row & (seq_len - 1)) == 0, first, rolled)
    sh = cur_s + (prev - cur_s) * mu_ref[...]

    r = sh[:, :A_WIDTH]
    k0 = sh[:, A_WIDTH:2 * A_WIDTH]
    v = sh[:, 2 * A_WIDTH:3 * A_WIDTH]
    o = 3 * A_WIDTH
    w_lo = sh[:, o:o + DECAY_LORA]
    a_lo = sh[:, o + DECAY_LORA:o + DECAY_LORA + AAA_LORA]
    g_lo = sh[:, o + DECAY_LORA + AAA_LORA:SHIFT_WIDTH]

    z = -(dbase_ref[...] + _dot(jnp.tanh(w_lo), wdec_ref[...]))
    softplus = jnp.maximum(z, 0.0) + jnp.log1p(jnp.exp(-jnp.abs(z)))
    log_w = -softplus - 0.5
    lw = -jnp.exp(log_w)
    a = _sigmoid(abase_ref[...] + _dot(a_lo, waaa_ref[...]))
    g = _dot(_sigmoid(g_lo), wgate_ref[...])

    ones64 = _group_ones(A_WIDTH, A_HEAD_DIM)
    kk0 = k0 * kk_ref[...]
    ssq = _group_sum(kk0 * kk0, ones64)
    kk = kk0 / jnp.maximum(jnp.sqrt(ssq), 1e-12)
    k = k0 * (1.0 + (a - 1.0) * ka_ref[...])
    bonus = _group_sum(r * k * rk_ref[...], ones64) * v

    r_o[...] = r
    lw_o[...] = lw
    k_o[...] = k
    v_o[...] = v
    al_o[...] = -kk
    be_o[...] = kk * a
    g_o[...] = g
    bonus_o[...] = bonus

    o = SHIFT_WIDTH
    q_b = proj[:, o:o + B_QK_WIDTH]
    k_b = proj[:, o + B_QK_WIDTH:o + 2 * B_QK_WIDTH]
    v_b = proj[:, o + 2 * B_QK_WIDTH:o + 2 * B_QK_WIDTH + B_WIDTH]
    g_b = proj[:, o + 2 * B_QK_WIDTH + B_WIDTH:]
    lane = lax.broadcasted_iota(jnp.int32, (tm, B_QK_WIDTH), 1)
    first_half = (lane & (B_QK_DIM - 1)) < (B_QK_DIM // 2)
    cos = cos_ref[...]
    sin = sin_ref[...]

    def rot(t):
        swapped = jnp.where(first_half, pltpu.roll(t, B_QK_WIDTH - B_QK_DIM // 2, 1), pltpu.roll(t, B_QK_DIM // 2, 1))
        return t * cos + swapped * sin

    qb_o[...] = rot(q_b)
    kb_o[...] = rot(k_b) * (B_QK_DIM ** -0.5)
    vb_o[...] = v_b
    gb_o[...] = g_b * _sigmoid(g_b)


def _proj(x2, x_prev, seq_len, pos0, W):
    n = x2.shape[0]
    tm = PROJ_ROWS
    assert n % tm == 0
    carry_mode = seq_len % tm == 0
    if carry_mode:
        tiles_per_seq = seq_len // tm
        xp = x_prev.reshape(-1, 1, D_MODEL)
        xp_spec = pl.BlockSpec((1, 1, D_MODEL), lambda i: (i // tiles_per_seq, 0, 0))
        tab_rows = seq_len
    else:
        assert tm % seq_len == 0 and seq_len & (seq_len - 1) == 0
        tiles_per_seq = 1
        xp = jnp.repeat(x_prev, seq_len, axis=0)
        xp_spec = pl.BlockSpec((tm, D_MODEL), lambda i: (i, 0))
        tab_rows = tm
    half = B_QK_DIM // 2
    inv = ROPE_BASE ** (-jnp.arange(half, dtype=F32) / half)
    pos = (pos0 + jnp.arange(seq_len, dtype=jnp.int32)).astype(F32)
    ang = pos[:, None] * inv[None, :]
    cos = jnp.tile(jnp.concatenate([jnp.cos(ang), jnp.cos(ang)], -1), (tab_rows // seq_len, B_HEADS))
    sin = jnp.tile(jnp.concatenate([-jnp.sin(ang), jnp.sin(ang)], -1), (tab_rows // seq_len, B_HEADS))
    tab_tiles = tab_rows // tm
    tab_spec = pl.BlockSpec((tm, B_QK_WIDTH), lambda i: (i % tab_tiles, 0))

    def full(a):
        return pl.BlockSpec(a.shape, lambda i: (0,) * a.ndim)

    def rows(width):
        return pl.BlockSpec((tm, width), lambda i: (i, 0))

    params = [W["w_in"], W["mu_shift"], W["w_decay_up"], W["decay_base"], W["w_aaa_up"], W["aaa_base"],
              W["w_gate_up"], W["k_k"], W["k_a"], W["r_k"]]
    widths = [A_WIDTH] * 8 + [B_QK_WIDTH, B_QK_WIDTH, B_WIDTH, B_WIDTH]
    outs = pl.pallas_call(
        functools.partial(_proj_kernel, carry_mode, seq_len, tiles_per_seq),
        grid=(n // tm,),
        in_specs=[rows(D_MODEL), xp_spec] + [full(a) for a in params] + [tab_spec, tab_spec],
        out_specs=[rows(w) for w in widths],
        out_shape=[jax.ShapeDtypeStruct((n, w), F32) for w in widths],
        scratch_shapes=[pltpu.VMEM((V7X_SUBLANES, SHIFT_WIDTH), F32)],
        compiler_params=pltpu.CompilerParams(dimension_semantics=("arbitrary",),
                                             vmem_limit_bytes=V7X_VMEM_LIMIT_BYTES),
        name="proj",
    )(x2, xp, *params, cos, sin)
    return outs


def _mixer_kernel(nb, seqs, clen,
                  r_ref, lw_ref, k_ref, v_ref, al_ref, be_ref, qb_ref, kb_ref, vb_ref, wkv0_ref, ret0_ref,
                  ya_o, ob_o, wkv_o, ret_o, s_scr, r_scr):
    R = seqs * clen
    log2c = int(math.log2(clen))
    c_idx = pl.program_id(1)
    hd = A_HEAD_DIM
    TH = [(t, h) for t in range(nb) for h in range(A_HEADS)]
    TG = [(t, h) for t in range(nb) for h in range(B_HEADS)]

    @pl.when(c_idx == 0)
    def _():
        for t, h in TH:
            blocks = [wkv0_ref[t * seqs + i, h] for i in range(seqs)]
            s_scr[t * A_HEADS + h] = jnp.concatenate(blocks, axis=1) if seqs > 1 else blocks[0]
        for t, h in TG:
            r_scr[t * B_HEADS + h] = ret0_ref[t * seqs:(t + 1) * seqs, h].reshape(seqs * B_QK_DIM, B_V_DIM)

    row = lax.broadcasted_iota(jnp.int32, (R, R), 0)
    col = lax.broadcasted_iota(jnp.int32, (R, R), 1)
    same = (row >> log2c) == (col >> log2c)
    incl = same & (col <= row)
    strict = same & (col < row)
    m_incl = jnp.where(incl, 1.0, 0.0).astype(BF16)
    m_same = jnp.where(same, 1.0, 0.0).astype(BF16)
    eye = jnp.where(row == col, 1.0, 0.0).astype(F32)

    def expand(t):
        if seqs == 1:
            return t
        w = t.shape[1]
        wide = jnp.concatenate([t] * seqs, axis=1)
        rr = lax.broadcasted_iota(jnp.int32, wide.shape, 0) >> log2c
        cc = lax.broadcasted_iota(jnp.int32, wide.shape, 1) // w
        return jnp.where(rr == cc, wide, 0.0)

    def head(x, h):
        return x[:, h * hd:(h + 1) * hd]

    a_bar, r_bar, b_til, k_til, b_dec, k_dec, d_end, vv = [], [], [], [], [], [], [], []
    for t in range(nb):
        lw = lw_ref[t]
        parts = _split3(lw)
        c = sum(jnp.dot(m_incl, p, preferred_element_type=F32) for p in parts)
        cend = sum(jnp.dot(m_same, p, preferred_element_type=F32) for p in parts)
        einv = jnp.exp(-c)
        edec = jnp.exp(cend - c)
        a_bar.append(al_ref[t] * jnp.exp(c - lw))
        r_bar.append(r_ref[t] * jnp.exp(c))
        b_til.append(be_ref[t] * einv)
        k_til.append(k_ref[t] * einv)
        b_dec.append(be_ref[t] * edec)
        k_dec.append(k_ref[t] * edec)
        d_end.append(jnp.exp(cend))
        vv.append(v_ref[t])
    last_row = (lax.broadcasted_iota(jnp.int32, (R, seqs * hd), 0) & (clen - 1)) == clen - 1

    amats = [_dot_nt(jnp.concatenate([head(a_bar[t], h), head(r_bar[t], h)], axis=0),
                     jnp.concatenate([head(b_til[t], h), head(k_til[t], h)], axis=0)) for t, h in TH]

    lgs = [float(np.log1p(-np.exp2(-5.0 - h))) for h in range(B_HEADS)]
    qs = [qb_ref[t][:, h * B_QK_DIM:(h + 1) * B_QK_DIM] for t, h in TG]
    khs = [kb_ref[t][:, h * B_QK_DIM:(h + 1) * B_QK_DIM] for t, h in TG]
    vbs = [vb_ref[t][:, h * B_V_DIM:(h + 1) * B_V_DIM] for t, h in TG]
    diff = (row - col).astype(F32)
    pos_v = (lax.broadcasted_iota(jnp.int32, (R, B_V_DIM), 0) & (clen - 1)).astype(F32)
    pos_k = (lax.broadcasted_iota(jnp.int32, (R, B_QK_DIM), 0) & (clen - 1)).astype(F32)
    intra = [jnp.where(incl, jnp.exp(lg * diff), 0.0) for lg in lgs]
    cross = [jnp.exp(lg * (pos_v + 1.0)) for lg in lgs]
    kdec = [jnp.exp(lg * (clen - 1.0 - pos_k)) for lg in lgs]
    rstates = [r_scr[t * B_HEADS + h] for t, h in TG]
    scs = [_dot_nt(qs[i], khs[i]) * intra[h] for i, (t, h) in enumerate(TG)]
    qst = [_dot(expand(qs[i]), rstates[i]) * cross[h] for i, (t, h) in enumerate(TG)]

    a_ab = [jnp.where(strict, m[:R, :R], 0.0) for m in amats]
    a_ak = [jnp.where(strict, m[:R, R:], 0.0) for m in amats]
    a_rb = [jnp.where(incl, m[R:, :R], 0.0) for m in amats]
    a_rk = [jnp.where(incl, m[R:, R:], 0.0) for m in amats]
    n = len(TH)
    tinv = [eye + a for a in a_ab]
    vhs = [head(vv[t], h) for t, h in TH]
    av = [_dot(a_ak[i], vhs[i]) for i in range(n)]
    if log2c > 1:
        pw = [_dot(a, a) for a in a_ab]
    for it in range(log2c - 1):
        if it < log2c - 2:
            tp = [_dot(pw[i], jnp.concatenate([tinv[i], pw[i]], axis=1)) for i in range(n)]
            tinv = [tinv[i] + tp[i][:, :R] for i in range(n)]
            pw = [tp[i][:, R:] for i in range(n)]
        else:
            tinv = [tinv[i] + _dot(pw[i], tinv[i]) for i in range(n)]

    os_ = [_dot(scs[i], vbs[i]) + qst[i] for i in range(len(TG))]
    for i, (t, h) in enumerate(TG):
        r_scr[t * B_HEADS + h] = (rstates[i] * float(np.exp(lgs[h] * clen))
                                  + _dot_tn(expand(khs[i] * kdec[h]), vbs[i]))
    for t in range(nb):
        ob_o[t] = jnp.concatenate(os_[t * B_HEADS:(t + 1) * B_HEADS], axis=1)

    wu = [_dot(tinv[i], jnp.concatenate([head(a_bar[t], h), av[i]], axis=1)) for i, (t, h) in enumerate(TH)]
    states = [s_scr[t * A_HEADS + h] for t, h in TH]
    ws = [_dot_nt(jnp.concatenate([expand(wu[i][:, :hd]), expand(head(r_bar[t], h))], axis=0), states[i])
          for i, (t, h) in enumerate(TH)]
    uv = [jnp.concatenate([ws[i][:R] + wu[i][:, hd:], vhs[i]], axis=0) for i in range(n)]
    ys = [ws[i][R:] + _dot(jnp.concatenate([a_rb[i], a_rk[i]], axis=1), uv[i]) for i in range(n)]
    for t in range(nb):
        ya_o[t] = jnp.concatenate(ys[t * A_HEADS:(t + 1) * A_HEADS], axis=1)
    for i, (t, h) in enumerate(TH):
        d_row = jnp.sum(jnp.where(last_row, expand(head(d_end[t], h)), 0.0), axis=0, keepdims=True)
        bk = jnp.concatenate([expand(head(b_dec[t], h)), expand(head(k_dec[t], h))], axis=0)
        s_scr[t * A_HEADS + h] = states[i] * d_row + _dot_tn(uv[i], bk)

    @pl.when(c_idx == pl.num_programs(1) - 1)
    def _():
        for t, h in TH:
            st = s_scr[t * A_HEADS + h]
            for i in range(seqs):
                wkv_o[t * seqs + i, h] = st[:, i * hd:(i + 1) * hd]
        for t, h in TG:
            ret_o[t * seqs:(t + 1) * seqs, h] = r_scr[t * B_HEADS + h].reshape(seqs, B_QK_DIM, B_V_DIM)


def _mixer(ops, wkv0, ret0, n_seq, seq_len):
    n = ops[0].shape[0]
    R = CHUNK_ROWS
    if seq_len >= R:
        assert seq_len % R == 0
        seqs, clen, nchunks = 1, R, seq_len // R
    else:
        assert R % seq_len == 0 and seq_len & (seq_len - 1) == 0 and n_seq % (R // seq_len) == 0
        seqs, clen, nchunks = R // seq_len, seq_len, 1
    ntiles = n_seq // seqs
    nb = MIXER_TILES if seqs == 1 else MIXER_TILES_PACKED
    assert ntiles % nb == 0
    ops3 = [a.reshape(ntiles, nchunks * R, a.shape[1]) for a in ops]

    def rows(width):
        return pl.BlockSpec((nb, R, width), lambda i, c: (i, c, 0))

    wkv_spec = pl.BlockSpec((nb * seqs, A_HEADS, A_HEAD_DIM, A_HEAD_DIM), lambda i, c: (i, 0, 0, 0))
    ret_spec = pl.BlockSpec((nb * seqs, B_HEADS, B_QK_DIM, B_V_DIM), lambda i, c: (i, 0, 0, 0))
    ya, ob, wkv1, ret1 = pl.pallas_call(
        functools.partial(_mixer_kernel, nb, seqs, clen),
        grid=(ntiles // nb, nchunks),
        in_specs=[rows(A_WIDTH)] * 6 + [rows(B_QK_WIDTH), rows(B_QK_WIDTH), rows(B_WIDTH), wkv_spec, ret_spec],
        out_specs=[rows(A_WIDTH), rows(B_WIDTH), wkv_spec, ret_spec],
        out_shape=[jax.ShapeDtypeStruct((ntiles, nchunks * R, A_WIDTH), F32),
                   jax.ShapeDtypeStruct((ntiles, nchunks * R, B_WIDTH), F32),
                   jax.ShapeDtypeStruct(wkv0.shape, F32), jax.ShapeDtypeStruct(ret0.shape, F32)],
        scratch_shapes=[pltpu.VMEM((nb * A_HEADS, A_HEAD_DIM, seqs * A_HEAD_DIM), F32),
                        pltpu.VMEM((nb * B_HEADS, seqs * B_QK_DIM, B_V_DIM), F32)],
        compiler_params=pltpu.CompilerParams(dimension_semantics=("parallel", "arbitrary"),
                                             vmem_limit_bytes=V7X_VMEM_LIMIT_BYTES),
        name="mixer",
    )(*ops3, wkv0, ret0)
    return ya.reshape(n, A_WIDTH), ob.reshape(n, B_WIDTH), wkv1, ret1


def _layer_norm(z, g, b):
    mu = jnp.mean(z, axis=-1, keepdims=True)
    d = z - mu
    var = jnp.mean(d * d, axis=-1, keepdims=True)
    return d * lax.rsqrt(var + LN_EPS) * g + b


def _post_kernel(ya_ref, ob_ref, bonus_ref, g_ref, gb_ref, x_ref, lnxg_ref, lnxb_ref, rgg_ref, rgb_ref,
                 wout_ref, ln1g_ref, ln1b_ref, wr_ref, br_ref, h_o, gate_o):
    tm = x_ref.shape[0]

    def head_norm(t, group, eps, gg, bb):
        ones = _group_ones(t.shape[1], group)
        mu = _group_sum(t, ones) * (1.0 / group)
        d = t - mu
        var = _group_sum(d * d, ones) * (1.0 / group)
        return d * lax.rsqrt(var + eps) * gg + bb

    y_a = (head_norm(ya_ref[...], A_HEAD_DIM, GN_EPS_RWKV, lnxg_ref[...], lnxb_ref[...]) + bonus_ref[...]) * g_ref[...]
    y_b = head_norm(ob_ref[...], B_V_DIM, GN_EPS, rgg_ref[...], rgb_ref[...]) * gb_ref[...]
    y = jnp.concatenate([y_a, y_b], axis=1)
    mix = _dot(y, wout_ref[...])
    h = _layer_norm(DEEPNORM_ALPHA * x_ref[...] + mix, ln1g_ref[...], ln1b_ref[...])
    h_o[...] = h

    h_hi, h_lo = _split2(h)
    w_hi, w_lo = _split2(wr_ref[...])
    logits = (jnp.dot(h_hi, w_hi, preferred_element_type=F32) + jnp.dot(h_hi, w_lo, preferred_element_type=F32)
              + jnp.dot(h_lo, w_hi, preferred_element_type=F32)) + br_ref[...]
    lane = lax.broadcasted_iota(jnp.int32, (tm, ROUTER_LANES), 1)
    neg = -jnp.inf
    big = ROUTER_LANES
    cl = jnp.where(lane < N_GROUPS, logits, neg)
    cmax = jnp.max(cl, axis=-1, keepdims=True)
    grp = jnp.min(jnp.where(cl == cmax, lane, big), axis=-1, keepdims=True)
    gprob = 1.0 / jnp.sum(jnp.exp(cl - cmax), axis=-1, keepdims=True)
    lo_lane = FINE_LANE0 + grp * EXPERTS_PER_GROUP
    fv = jnp.where((lane >= lo_lane) & (lane < lo_lane + EXPERTS_PER_GROUP), logits, neg)
    m1 = jnp.max(fv, axis=-1, keepdims=True)
    i1 = jnp.min(jnp.where(fv == m1, lane, big), axis=-1, keepdims=True)
    fv2 = jnp.where(lane == i1, neg, fv)
    m2 = jnp.max(fv2, axis=-1, keepdims=True)
    i2 = jnp.min(jnp.where(fv2 == m2, lane, big), axis=-1, keepdims=True)
    e2 = jnp.exp(m2 - m1)
    w1 = gprob / (1.0 + e2)
    w2 = gprob * e2 / (1.0 + e2)
    gate_o[...] = jnp.where(lane == i1, w1, 0.0) + jnp.where(lane == i2, w2, 0.0)


def _post(ya, ob, bonus, g, gb, x2, W):
    n = x2.shape[0]
    tm = POST_ROWS
    assert n % tm == 0

    def full(a):
        return pl.BlockSpec(a.shape, lambda i: (0,) * a.ndim)

    def rows(width):
        return pl.BlockSpec((tm, width), lambda i: (i, 0))

    params = [W["lnx_g"], W["lnx_b"], W["ret_gn_g"], W["ret_gn_b"], W["w_out"], W["ln1_g"], W["ln1_b"],
              W["w_router"], W["b_router"]]
    return pl.pallas_call(
        _post_kernel,
        grid=(n // tm,),
        in_specs=[rows(A_WIDTH), rows(B_WIDTH), rows(A_WIDTH), rows(A_WIDTH), rows(B_WIDTH), rows(D_MODEL)]
        + [full(a) for a in params],
        out_specs=[rows(D_MODEL), rows(ROUTER_LANES)],
        out_shape=[jax.ShapeDtypeStruct((n, D_MODEL), F32), jax.ShapeDtypeStruct((n, ROUTER_LANES), F32)],
        compiler_params=pltpu.CompilerParams(dimension_semantics=("parallel",),
                                             vmem_limit_bytes=V7X_VMEM_LIMIT_BYTES),
        name="post",
    )(ya, ob, bonus, g, gb, x2, *params)


def _ffn_kernel(h_ref, gate_ref, p_ref, w1_ref, w3_ref, w2_ref, ln2g_ref, ln2b_ref, wple_ref, wpg_ref, pleg_ref,
                o_ref, hb_scr, acc_scr):
    tm = h_ref.shape[0]
    e_idx = pl.program_id(1)

    @pl.when(e_idx == 0)
    def _():
        hb_scr[...] = h_ref[...].astype(BF16)
        acc_scr[...] = jnp.zeros_like(acc_scr)

    hb = hb_scr[...]
    gate = gate_ref[...]
    lane = lax.broadcasted_iota(jnp.int32, gate.shape, 1)
    for j in range(FFN_EXPERTS_PER_STEP):
        e = e_idx * FFN_EXPERTS_PER_STEP + j
        ge = jnp.sum(jnp.where(lane == FINE_LANE0 + e, gate, 0.0), axis=-1, keepdims=True)
        a = jnp.dot(hb, w1_ref[j], preferred_element_type=F32)
        b = jnp.dot(hb, w3_ref[j], preferred_element_type=F32)
        hid = (a * _sigmoid(a)) * b * ge
        acc_scr[...] += jnp.dot(hid.astype(BF16), w2_ref[j], preferred_element_type=F32)

    @pl.when(e_idx == pl.num_programs(1) - 1)
    def _():
        h2 = _layer_norm(DEEPNORM_ALPHA * h_ref[...] + acc_scr[...], ln2g_ref[...], ln2b_ref[...])
        ple = _dot(p_ref[...], wple_ref[...]) * _sigmoid(_dot(h2, wpg_ref[...]))
        ms = jnp.mean(ple * ple, axis=-1, keepdims=True)
        o_ref[...] = h2 + ple * lax.rsqrt(ms + LN_EPS) * pleg_ref[...]


def _ffn(h, gate, p2, W):
    n = h.shape[0]
    tm = FFN_ROWS
    epb = FFN_EXPERTS_PER_STEP
    assert n % tm == 0 and N_EXPERTS % epb == 0

    def full(a):
        return pl.BlockSpec(a.shape, lambda i, e: (0,) * a.ndim)

    def rows(width):
        return pl.BlockSpec((tm, width), lambda i, e: (i, 0))

    params = [W["ln2_g"], W["ln2_b"], W["w_ple"], W["w_ple_gate"], W["ple_norm_g"]]
    return pl.pallas_call(
        _ffn_kernel,
        grid=(n // tm, N_EXPERTS // epb),
        in_specs=[rows(D_MODEL), rows(ROUTER_LANES), rows(D_PLE),
                  pl.BlockSpec((epb, D_MODEL, D_EXPERT), lambda i, e: (e, 0, 0)),
                  pl.BlockSpec((epb, D_MODEL, D_EXPERT), lambda i, e: (e, 0, 0)),
                  pl.BlockSpec((epb, D_EXPERT, D_MODEL), lambda i, e: (e, 0, 0))]
        + [full(a) for a in params],
        out_specs=rows(D_MODEL),
        out_shape=jax.ShapeDtypeStruct((n, D_MODEL), F32),
        scratch_shapes=[pltpu.VMEM((tm, D_MODEL), BF16), pltpu.VMEM((tm, D_MODEL), F32)],
        compiler_params=pltpu.CompilerParams(dimension_semantics=("parallel", "arbitrary"),
                                             vmem_limit_bytes=V7X_VMEM_LIMIT_BYTES),
        name="ffn",
    )(h, gate, p2, W["expert_w1"], W["expert_w3"], W["expert_w2"], *params)


def _layer(x, p, x_prev, wkv0, ret0, pos0, W):
    n_seq, seq_len, _ = x.shape
    n = n_seq * seq_len
    x2 = x.reshape(n, D_MODEL)
    r, lw, k, v, al, be, g, bonus, qb, kb, vb, gb = _proj(x2, x_prev, seq_len, pos0, W)
    ya, ob, wkv1, ret1 = _mixer((r, lw, k, v, al, be, qb, kb, vb), wkv0, ret0, n_seq, seq_len)
    h, gate = _post(ya, ob, bonus, g, gb, x2, W)
    out = _ffn(h, gate, p.reshape(n, D_PLE), W)
    return out.reshape(n_seq, seq_len, D_MODEL), x[:, -1], wkv1, ret1


def _prep_weights(i, w_in, mu_shift, w_decay_up, decay_base, w_aaa_up, aaa_base, w_gate_up, k_k, k_a, r_k,
                  lnx_g, lnx_b, ret_gn_g, ret_gn_b, w_out, ln1_g, ln1_b,
                  router_coarse_w, router_coarse_b, router_fine_w, router_fine_b,
                  expert_w1, expert_w3, expert_w2, ln2_g, ln2_b, w_ple, w_ple_gate, ple_norm_g):
    def row(a):
        return a[i].reshape(1, -1).astype(F32)

    pad = ROUTER_LANES - N_GROUPS - N_EXPERTS
    w_router = jnp.concatenate([router_coarse_w[i], router_fine_w[i], jnp.zeros((D_MODEL, pad), F32)], axis=1)
    b_router = jnp.concatenate([router_coarse_b[i], router_fine_b[i], jnp.zeros((pad,), F32)]).reshape(1, -1)
    return {
        "w_in": w_in[i].astype(BF16), "mu_shift": row(mu_shift), "w_decay_up": w_decay_up[i].astype(BF16),
        "decay_base": row(decay_base), "w_aaa_up": w_aaa_up[i].astype(BF16), "aaa_base": row(aaa_base),
        "w_gate_up": w_gate_up[i].astype(BF16), "k_k": row(k_k), "k_a": row(k_a), "r_k": row(r_k),
        "lnx_g": row(lnx_g), "lnx_b": row(lnx_b), "ret_gn_g": row(ret_gn_g), "ret_gn_b": row(ret_gn_b),
        "w_out": w_out[i].astype(BF16), "ln1_g": row(ln1_g), "ln1_b": row(ln1_b),
        "w_router": w_router, "b_router": b_router,
        "expert_w1": expert_w1[i].astype(BF16), "expert_w3": expert_w3[i].astype(BF16),
        "expert_w2": expert_w2[i].astype(BF16), "ln2_g": row(ln2_g), "ln2_b": row(ln2_b),
        "w_ple": w_ple[i].astype(BF16), "w_ple_gate": w_ple_gate[i].astype(BF16), "ple_norm_g": row(ple_norm_g),
    }


def kernel(x_prompt, x_sample, p_prompt, p_sample, state_wkv, state_shift, state_ret, w_in, mu_shift, w_decay_up, decay_base, w_aaa_up, aaa_base, w_gate_up, k_k, k_a, r_k, lnx_g, lnx_b, ret_gn_g, ret_gn_b, w_out, ln1_g, ln1_b, router_coarse_w, router_coarse_b, router_fine_w, router_fine_b, expert_w1, expert_w3, expert_w2, ln2_g, ln2_b, w_ple, w_ple_gate, ple_norm_g):
    yp, ys = x_prompt, x_sample
    nb = x_prompt.shape[0]
    depth = w_in.shape[0]
    wkv_p, shift_p, ret_p, wkv_s, shift_s, ret_s = [], [], [], [], [], []
    for i in range(depth):
        W = _prep_weights(i, w_in, mu_shift, w_decay_up, decay_base, w_aaa_up, aaa_base, w_gate_up, k_k, k_a, r_k,
                          lnx_g, lnx_b, ret_gn_g, ret_gn_b, w_out, ln1_g, ln1_b,
                          router_coarse_w, router_coarse_b, router_fine_w, router_fine_b,
                          expert_w1, expert_w3, expert_w2, ln2_g, ln2_b, w_ple, w_ple_gate, ple_norm_g)
        yp, sp, wp, rp = _layer(yp, p_prompt[i], jnp.zeros((nb, D_MODEL), F32),
                                jnp.zeros((nb, A_HEADS, A_HEAD_DIM, A_HEAD_DIM), F32),
                                jnp.zeros((nb, B_HEADS, B_QK_DIM, B_V_DIM), F32), 0, W)
        ys, ss, wsm, rsm = _layer(ys, p_sample[i], state_shift[i], state_wkv[i], state_ret[i], PAST_LEN, W)
        wkv_p.append(wp); shift_p.append(sp); ret_p.append(rp)
        wkv_s.append(wsm); shift_s.append(ss); ret_s.append(rsm)
    return (yp, ys, jnp.stack(wkv_p, 0), jnp.stack(shift_p, 0), jnp.stack(ret_p, 0),
            jnp.stack(wkv_s, 0), jnp.stack(shift_s, 0), jnp.stack(ret_s, 0))
```

```python
import functools
import math

import numpy as np
import jax
import jax.numpy as jnp
from jax import lax
from jax.experimental import pallas as pl
from jax.experimental.pallas import tpu as pltpu

F32 = jnp.float32
BF16 = jnp.bfloat16

D_MODEL = 1024
D_PLE = 256
A_HEADS = 8
A_HEAD_DIM = 64
A_WIDTH = A_HEADS * A_HEAD_DIM
DECAY_LORA = 64
AAA_LORA = 64
GATE_LORA = 128
GN_EPS_RWKV = 64e-5
B_HEADS = 4
B_QK_DIM = 64
B_V_DIM = 128
B_QK_WIDTH = B_HEADS * B_QK_DIM
B_WIDTH = B_HEADS * B_V_DIM
ROPE_BASE = 10000.0
GN_EPS = 1e-5
SHIFT_WIDTH = 3 * A_WIDTH + DECAY_LORA + AAA_LORA + GATE_LORA
IN_WIDTH = SHIFT_WIDTH + 2 * B_QK_WIDTH + 2 * B_WIDTH
N_GROUPS = 4
EXPERTS_PER_GROUP = 4
N_EXPERTS = N_GROUPS * EXPERTS_PER_GROUP
D_EXPERT = 256
DEPTH = 1
PAST_LEN = 16384
DEEPNORM_ALPHA = (2 * DEPTH) ** 0.25
LN_EPS = 1e-5

V7X_LANES = 128
V7X_SUBLANES = 8
V7X_VMEM_LIMIT_BYTES = 56 * 1024 * 1024

PROJ_ROWS = 256
CHUNK_ROWS = 64
MIXER_TILES = 4
MIXER_TILES_PACKED = 2
POST_ROWS = 512
FFN_ROWS = 1024
FFN_EXPERTS_PER_STEP = 2
ROUTER_LANES = V7X_LANES
FINE_LANE0 = N_GROUPS


def _dot(a, b):
    return jnp.dot(a.astype(BF16), b.astype(BF16), preferred_element_type=F32)


def _dot_nt(a, b):
    return lax.dot_general(a.astype(BF16), b.astype(BF16), (((1,), (1,)), ((), ())), preferred_element_type=F32)


def _dot_tn(a, b):
    return lax.dot_general(a.astype(BF16), b.astype(BF16), (((0,), (0,)), ((), ())), preferred_element_type=F32)


def _split2(x):
    hi = x.astype(BF16)
    lo = (x - hi.astype(F32)).astype(BF16)
    return hi, lo


def _split3(x):
    hi = x.astype(BF16)
    r1 = x - hi.astype(F32)
    mid = r1.astype(BF16)
    lo = (r1 - mid.astype(F32)).astype(BF16)
    return hi, mid, lo


def _sigmoid(x):
    return 1.0 / (1.0 + jnp.exp(-x))


def _group_ones(width, group):
    r = lax.broadcasted_iota(jnp.int32, (width, width), 0) // group
    c = lax.broadcasted_iota(jnp.int32, (width, width), 1) // group
    return jnp.where(r == c, 1.0, 0.0).astype(BF16)


def _group_sum(x, ones):
    return jnp.dot(x.astype(BF16), ones, preferred_element_type=F32)


def _proj_kernel(carry_mode, seq_len, tiles_per_seq,
                 x_ref, xp_ref, w_ref, mu_ref, wdec_ref, dbase_ref, waaa_ref, abase_ref, wgate_ref,
                 kk_ref, ka_ref, rk_ref, cos_ref, sin_ref,
                 r_o, lw_o, k_o, v_o, al_o, be_o, g_o, bonus_o, qb_o, kb_o, vb_o, gb_o,
                 carry_scr):
    tm = x_ref.shape[0]
    row = lax.broadcasted_iota(jnp.int32, (tm, SHIFT_WIDTH), 0)
    if carry_mode:
        xp = jnp.broadcast_to(xp_ref[0], (V7X_SUBLANES, D_MODEL))
        xb = jnp.concatenate([x_ref[...], xp], axis=0).astype(BF16)
        proj_all = jnp.dot(xb, w_ref[...], preferred_element_type=F32)
        proj = proj_all[:tm]
        cur_s = proj[:, :SHIFT_WIDTH]
        rolled = pltpu.roll(cur_s, 1, 0)
        j = pl.program_id(0) % tiles_per_seq

        @pl.when(pl.program_id(0) == 0)
        def _():
            carry_scr[...] = jnp.zeros_like(carry_scr)

        first = jnp.where(j == 0, proj_all[tm + V7X_SUBLANES - 1:, :SHIFT_WIDTH],
                          carry_scr[V7X_SUBLANES - 1:V7X_SUBLANES, :])
        prev = jnp.where(row == 0, first, rolled)
        carry_scr[...] = cur_s[tm - V7X_SUBLANES:, :]
    else:
        xb = x_ref[...].astype(BF16)
        proj = jnp.dot(xb, w_ref[...], preferred_element_type=F32)
        cur_s = proj[:, :SHIFT_WIDTH]
        rolled = pltpu.roll(cur_s, 1, 0)
        first = jnp.dot(xp_ref[...].astype(BF16), w_ref[:, :SHIFT_WIDTH], preferred_element_type=F32)
        prev = jnp.where((row & (seq_len - 1)) == 0, first, rolled)
    sh = cur_s + (prev - cur_s) * mu_ref[...]

    r = sh[:, :A_WIDTH]
    k0 = sh[:, A_WIDTH:2 * A_WIDTH]
    v = sh[:, 2 * A_WIDTH:3 * A_WIDTH]
    o = 3 * A_WIDTH
    w_lo = sh[:, o:o + DECAY_LORA]
    a_lo = sh[:, o + DECAY_LORA:o + DECAY_LORA + AAA_LORA]
    g_lo = sh[:, o + DECAY_LORA + AAA_LORA:SHIFT_WIDTH]

    z = -(dbase_ref[...] + _dot(jnp.tanh(w_lo), wdec_ref[...]))
    softplus = jnp.maximum(z, 0.0) + jnp.log1p(jnp.exp(-jnp.abs(z)))
    log_w = -softplus - 0.5
    lw = -jnp.exp(log_w)
    a = _sigmoid(abase_ref[...] + _dot(a_lo, waaa_ref[...]))
    g = _dot(_sigmoid(g_lo), wgate_ref[...])

    ones64 = _group_ones(A_WIDTH, A_HEAD_DIM)
    kk0 = k0 * kk_ref[...]
    ssq = _group_sum(kk0 * kk0, ones64)
    kk = kk0 / jnp.maximum(jnp.sqrt(ssq), 1e-12)
    k = k0 * (1.0 + (a - 1.0) * ka_ref[...])
    bonus = _group_sum(r * k * rk_ref[...], ones64) * v

    r_o[...] = r
    lw_o[...] = lw
    k_o[...] = k
    v_o[...] = v
    al_o[...] = -kk
    be_o[...] = kk * a
    g_o[...] = g
    bonus_o[...] = bonus

    o = SHIFT_WIDTH
    q_b = proj[:, o:o + B_QK_WIDTH]
    k_b = proj[:, o + B_QK_WIDTH:o + 2 * B_QK_WIDTH]
    v_b = proj[:, o + 2 * B_QK_WIDTH:o + 2 * B_QK_WIDTH + B_WIDTH]
    g_b = proj[:, o + 2 * B_QK_WIDTH + B_WIDTH:]
    lane = lax.broadcasted_iota(jnp.int32, (tm, B_QK_WIDTH), 1)
    first_half = (lane & (B_QK_DIM - 1)) < (B_QK_DIM // 2)
    cos = cos_ref[...]
    sin = sin_ref[...]

    def rot(t):
        swapped = jnp.where(first_half, pltpu.roll(t, B_QK_WIDTH - B_QK_DIM // 2, 1), pltpu.roll(t, B_QK_DIM // 2, 1))
        return t * cos + swapped * sin

    qb_o[...] = rot(q_b)
    kb_o[...] = rot(k_b) * (B_QK_DIM ** -0.5)
    vb_o[...] = v_b
    gb_o[...] = g_b * _sigmoid(g_b)


def _proj(x2, x_prev, seq_len, pos0, W):
    n = x2.shape[0]
    tm = PROJ_ROWS
    assert n % tm == 0
    carry_mode = seq_len % tm == 0
    if carry_mode:
        tiles_per_seq = seq_len // tm
        xp = x_prev.reshape(-1, 1, D_MODEL)
        xp_spec = pl.BlockSpec((1, 1, D_MODEL), lambda i: (i // tiles_per_seq, 0, 0))
        tab_rows = seq_len
    else:
        assert tm % seq_len == 0 and seq_len & (seq_len - 1) == 0
        tiles_per_seq = 1
        xp = jnp.repeat(x_prev, seq_len, axis=0)
        xp_spec = pl.BlockSpec((tm, D_MODEL), lambda i: (i, 0))
        tab_rows = tm
    half = B_QK_DIM // 2
    inv = ROPE_BASE ** (-jnp.arange(half, dtype=F32) / half)
    pos = (pos0 + jnp.arange(seq_len, dtype=jnp.int32)).astype(F32)
    ang = pos[:, None] * inv[None, :]
    cos = jnp.tile(jnp.concatenate([jnp.cos(ang), jnp.cos(ang)], -1), (tab_rows // seq_len, B_HEADS))
    sin = jnp.tile(jnp.concatenate([-jnp.sin(ang), jnp.sin(ang)], -1), (tab_rows // seq_len, B_HEADS))
    tab_tiles = tab_rows // tm
    tab_spec = pl.BlockSpec((tm, B_QK_WIDTH), lambda i: (i % tab_tiles, 0))

    def full(a):
        return pl.BlockSpec(a.shape, lambda i: (0,) * a.ndim)

    def rows(width):
        return pl.BlockSpec((tm, width), lambda i: (i, 0))

    params = [W["w_in"], W["mu_shift"], W["w_decay_up"], W["decay_base"], W["w_aaa_up"], W["aaa_base"],
              W["w_gate_up"], W["k_k"], W["k_a"], W["r_k"]]
    widths = [A_WIDTH] * 8 + [B_QK_WIDTH, B_QK_WIDTH, B_WIDTH, B_WIDTH]
    outs = pl.pallas_call(
        functools.partial(_proj_kernel, carry_mode, seq_len, tiles_per_seq),
        grid=(n // tm,),
        in_specs=[rows(D_MODEL), xp_spec] + [full(a) for a in params] + [tab_spec, tab_spec],
        out_specs=[rows(w) for w in widths],
        out_shape=[jax.ShapeDtypeStruct((n, w), F32) for w in widths],
        scratch_shapes=[pltpu.VMEM((V7X_SUBLANES, SHIFT_WIDTH), F32)],
        compiler_params=pltpu.CompilerParams(dimension_semantics=("arbitrary",),
                                             vmem_limit_bytes=V7X_VMEM_LIMIT_BYTES),
        name="proj",
    )(x2, xp, *params, cos, sin)
    return outs


def _mixer_kernel(nb, seqs, clen,
                  r_ref, lw_ref, k_ref, v_ref, al_ref, be_ref, qb_ref, kb_ref, vb_ref, wkv0_ref, ret0_ref,
                  ya_o, ob_o, wkv_o, ret_o, s_scr, r_scr):
    R = seqs * clen
    log2c = int(math.log2(clen))
    c_idx = pl.program_id(1)
    hd = A_HEAD_DIM
    TH = [(t, h) for t in range(nb) for h in range(A_HEADS)]
    TG = [(t, h) for t in range(nb) for h in range(B_HEADS)]

    @pl.when(c_idx == 0)
    def _():
        for t, h in TH:
            blocks = [wkv0_ref[t * seqs + i, h] for i in range(seqs)]
            s_scr[t * A_HEADS + h] = jnp.concatenate(blocks, axis=1) if seqs > 1 else blocks[0]
        for t, h in TG:
            r_scr[t * B_HEADS + h] = ret0_ref[t * seqs:(t + 1) * seqs, h].reshape(seqs * B_QK_DIM, B_V_DIM)

    row = lax.broadcasted_iota(jnp.int32, (R, R), 0)
    col = lax.broadcasted_iota(jnp.int32, (R, R), 1)
    same = (row >> log2c) == (col >> log2c)
    incl = same & (col <= row)
    strict = same & (col < row)
    m_incl = jnp.where(incl, 1.0, 0.0).astype(BF16)
    m_same = jnp.where(same, 1.0, 0.0).astype(BF16)
    eye = jnp.where(row == col, 1.0, 0.0).astype(F32)

    def expand(t):
        if seqs == 1:
            return t
        w = t.shape[1]
        wide = jnp.concatenate([t] * seqs, axis=1)
        rr = lax.broadcasted_iota(jnp.int32, wide.shape, 0) >> log2c
        cc = lax.broadcasted_iota(jnp.int32, wide.shape, 1) // w
        return jnp.where(rr == cc, wide, 0.0)

    def head(x, h):
        return x[:, h * hd:(h + 1) * hd]

    a_bar, r_bar, b_til, k_til, b_dec, k_dec, d_end, vv = [], [], [], [], [], [], [], []
    for t in range(nb):
        lw = lw_ref[t]
        parts = _split3(lw)
        c = sum(jnp.dot(m_incl, p, preferred_element_type=F32) for p in parts)
        cend = sum(jnp.dot(m_same, p, preferred_element_type=F32) for p in parts)
        einv = jnp.exp(-c)
        edec = jnp.exp(cend - c)
        a_bar.append(al_ref[t] * jnp.exp(c - lw))
        r_bar.append(r_ref[t] * jnp.exp(c))
        b_til.append(be_ref[t] * einv)
        k_til.append(k_ref[t] * einv)
        b_dec.append(be_ref[t] * edec)
        k_dec.append(k_ref[t] * edec)
        d_end.append(jnp.exp(cend))
        vv.append(v_ref[t])
    last_row = (lax.broadcasted_iota(jnp.int32, (R, seqs * hd), 0) & (clen - 1)) == clen - 1

    amats = [_dot_nt(jnp.concatenate([head(a_bar[t], h), head(r_bar[t], h)], axis=0),
                     jnp.concatenate([head(b_til[t], h), head(k_til[t], h)], axis=0)) for t, h in TH]

    lgs = [float(np.log1p(-np.exp2(-5.0 - h))) for h in range(B_HEADS)]
    qs = [qb_ref[t][:, h * B_QK_DIM:(h + 1) * B_QK_DIM] for t, h in TG]
    khs = [kb_ref[t][:, h * B_QK_DIM:(h + 1) * B_QK_DIM] for t, h in TG]
    vbs = [vb_ref[t][:, h * B_V_DIM:(h + 1) * B_V_DIM] for t, h in TG]
    diff = (row - col).astype(F32)
    pos_v = (lax.broadcasted_iota(jnp.int32, (R, B_V_DIM), 0) & (clen - 1)).astype(F32)
    pos_k = (lax.broadcasted_iota(jnp.int32, (R, B_QK_DIM), 0) & (clen - 1)).astype(F32)
    intra = [jnp.where(incl, jnp.exp(lg * diff), 0.0) for lg in lgs]
    cross = [jnp.exp(lg * (pos_v + 1.0)) for lg in lgs]
    kdec = [jnp.exp(lg * (clen - 1.0 - pos_k)) for lg in lgs]
    rstates = [r_scr[t * B_HEADS + h] for t, h in TG]
    scs = [_dot_nt(qs[i], khs[i]) * intra[h] for i, (t, h) in enumerate(TG)]
    qst = [_dot(expand(qs[i]), rstates[i]) * cross[h] for i, (t, h) in enumerate(TG)]

    a_ab = [jnp.where(strict, m[:R, :R], 0.0) for m in amats]
    a_ak = [jnp.where(strict, m[:R, R:], 0.0) for m in amats]
    a_rb = [jnp.where(incl, m[R:, :R], 0.0) for m in amats]
    a_rk = [jnp.where(incl, m[R:, R:], 0.0) for m in amats]
    n = len(TH)
    tinv = [eye + a for a in a_ab]
    vhs = [head(vv[t], h) for t, h in TH]
    av = [_dot(a_ak[i], vhs[i]) for i in range(n)]
    if log2c > 1:
        pw = [_dot(a, a) for a in a_ab]
    for it in range(log2c - 1):
        if it < log2c - 2:
            tp = [_dot(pw[i], jnp.concatenate([tinv[i], pw[i]], axis=1)) for i in range(n)]
            tinv = [tinv[i] + tp[i][:, :R] for i in range(n)]
            pw = [tp[i][:, R:] for i in range(n)]
        else:
            tinv = [tinv[i] + _dot(pw[i], tinv[i]) for i in range(n)]

    os_ = [_dot(scs[i], vbs[i]) + qst[i] for i in range(len(TG))]
    for i, (t, h) in enumerate(TG):
        r_scr[t * B_HEADS + h] = (rstates[i] * float(np.exp(lgs[h] * clen))
                                  + _dot_tn(expand(khs[i] * kdec[h]), vbs[i]))
    for t in range(nb):
        ob_o[t] = jnp.concatenate(os_[t * B_HEADS:(t + 1) * B_HEADS], axis=1)

    wu = [_dot(tinv[i], jnp.concatenate([head(a_bar[t], h), av[i]], axis=1)) for i, (t, h) in enumerate(TH)]
    states = [s_scr[t * A_HEADS + h] for t, h in TH]
    ws = [_dot_nt(jnp.concatenate([expand(wu[i][:, :hd]), expand(head(r_bar[t], h))], axis=0), states[i])
          for i, (t, h) in enumerate(TH)]
    uv = [jnp.concatenate([ws[i][:R] + wu[i][:, hd:], vhs[i]], axis=0) for i in range(n)]
    ys = [ws[i][R:] + _dot(jnp.concatenate([a_rb[i], a_rk[i]], axis=1), uv[i]) for i in range(n)]
    for t in range(nb):
        ya_o[t] = jnp.concatenate(ys[t * A_HEADS:(t + 1) * A_HEADS], axis=1)
    for i, (t, h) in enumerate(TH):
        d_row = jnp.sum(jnp.where(last_row, expand(head(d_end[t], h)), 0.0), axis=0, keepdims=True)
        bk = jnp.concatenate([expand(head(b_dec[t], h)), expand(head(k_dec[t], h))], axis=0)
        s_scr[t * A_HEADS + h] = states[i] * d_row + _dot_tn(uv[i], bk)

    @pl.when(c_idx == pl.num_programs(1) - 1)
    def _():
        for t, h in TH:
            st = s_scr[t * A_HEADS + h]
            for i in range(seqs):
                wkv_o[t * seqs + i, h] = st[:, i * hd:(i + 1) * hd]
        for t, h in TG:
            ret_o[t * seqs:(t + 1) * seqs, h] = r_scr[t * B_HEADS + h].reshape(seqs, B_QK_DIM, B_V_DIM)


def _mixer(ops, wkv0, ret0, n_seq, seq_len):
    n = ops[0].shape[0]
    R = CHUNK_ROWS
    if seq_len >= R:
        assert seq_len % R == 0
        seqs, clen, nchunks = 1, R, seq_len // R
    else:
        assert R % seq_len == 0 and seq_len & (seq_len - 1) == 0 and n_seq % (R // seq_len) == 0
        seqs, clen, nchunks = R // seq_len, seq_len, 1
    ntiles = n_seq // seqs
    nb = MIXER_TILES if seqs == 1 else MIXER_TILES_PACKED
    assert ntiles % nb == 0
    ops3 = [a.reshape(ntiles, nchunks * R, a.shape[1]) for a in ops]

    def rows(width):
        return pl.BlockSpec((nb, R, width), lambda i, c: (i, c, 0))

    wkv_spec = pl.BlockSpec((nb * seqs, A_HEADS, A_HEAD_DIM, A_HEAD_DIM), lambda i, c: (i, 0, 0, 0))
    ret_spec = pl.BlockSpec((nb * seqs, B_HEADS, B_QK_DIM, B_V_DIM), lambda i, c: (i, 0, 0, 0))
    ya, ob, wkv1, ret1 = pl.pallas_call(
        functools.partial(_mixer_kernel, nb, seqs, clen),
        grid=(ntiles // nb, nchunks),
        in_specs=[rows(A_WIDTH)] * 6 + [rows(B_QK_WIDTH), rows(B_QK_WIDTH), rows(B_WIDTH), wkv_spec, ret_spec],
        out_specs=[rows(A_WIDTH), rows(B_WIDTH), wkv_spec, ret_spec],
        out_shape=[jax.ShapeDtypeStruct((ntiles, nchunks * R, A_WIDTH), F32),
                   jax.ShapeDtypeStruct((ntiles, nchunks * R, B_WIDTH), F32),
                   jax.ShapeDtypeStruct(wkv0.shape, F32), jax.ShapeDtypeStruct(ret0.shape, F32)],
        scratch_shapes=[pltpu.VMEM((nb * A_HEADS, A_HEAD_DIM, seqs * A_HEAD_DIM), F32),
                        pltpu.VMEM((nb * B_HEADS, seqs * B_QK_DIM, B_V_DIM), F32)],
        compiler_params=pltpu.CompilerParams(dimension_semantics=("parallel", "arbitrary"),
                                             vmem_limit_bytes=V7X_VMEM_LIMIT_BYTES),
        name="mixer",
    )(*ops3, wkv0, ret0)
    return ya.reshape(n, A_WIDTH), ob.reshape(n, B_WIDTH), wkv1, ret1


def _layer_norm(z, g, b):
    mu = jnp.mean(z, axis=-1, keepdims=True)
    d = z - mu
    var = jnp.mean(d * d, axis=-1, keepdims=True)
    return d * lax.rsqrt(var + LN_EPS) * g + b


def _post_kernel(ya_ref, ob_ref, bonus_ref, g_ref, gb_ref, x_ref, lnxg_ref, lnxb_ref, rgg_ref, rgb_ref,
                 wout_ref, ln1g_ref, ln1b_ref, wr_ref, br_ref, h_o, gate_o):
    tm = x_ref.shape[0]

    def head_norm(t, group, eps, gg, bb):
        ones = _group_ones(t.shape[1], group)
        mu = _group_sum(t, ones) * (1.0 / group)
        d = t - mu
        var = _group_sum(d * d, ones) * (1.0 / group)
        return d * lax.rsqrt(var + eps) * gg + bb

    y_a = (head_norm(ya_ref[...], A_HEAD_DIM, GN_EPS_RWKV, lnxg_ref[...], lnxb_ref[...]) + bonus_ref[...]) * g_ref[...]
    y_b = head_norm(ob_ref[...], B_V_DIM, GN_EPS, rgg_ref[...], rgb_ref[...]) * gb_ref[...]
    y = jnp.concatenate([y_a, y_b], axis=1)
    mix = _dot(y, wout_ref[...])
    h = _layer_norm(DEEPNORM_ALPHA * x_ref[...] + mix, ln1g_ref[...], ln1b_ref[...])
    h_o[...] = h

    h_hi, h_lo = _split2(h)
    w_hi, w_lo = _split2(wr_ref[...])
    logits = (jnp.dot(h_hi, w_hi, preferred_element_type=F32) + jnp.dot(h_hi, w_lo, preferred_element_type=F32)
              + jnp.dot(h_lo, w_hi, preferred_element_type=F32)) + br_ref[...]
    lane = lax.broadcasted_iota(jnp.int32, (tm, ROUTER_LANES), 1)
    neg = -jnp.inf
    big = ROUTER_LANES
    cl = jnp.where(lane < N_GROUPS, logits, neg)
    cmax = jnp.max(cl, axis=-1, keepdims=True)
    grp = jnp.min(jnp.where(cl == cmax, lane, big), axis=-1, keepdims=True)
    gprob = 1.0 / jnp.sum(jnp.exp(cl - cmax), axis=-1, keepdims=True)
    lo_lane = FINE_LANE0 + grp * EXPERTS_PER_GROUP
    fv = jnp.where((lane >= lo_lane) & (lane < lo_lane + EXPERTS_PER_GROUP), logits, neg)
    m1 = jnp.max(fv, axis=-1, keepdims=True)
    i1 = jnp.min(jnp.where(fv == m1, lane, big), axis=-1, keepdims=True)
    fv2 = jnp.where(lane == i1, neg, fv)
    m2 = jnp.max(fv2, axis=-1, keepdims=True)
    i2 = jnp.min(jnp.where(fv2 == m2, lane, big), axis=-1, keepdims=True)
    e2 = jnp.exp(m2 - m1)
    w1 = gprob / (1.0 + e2)
    w2 = gprob * e2 / (1.0 + e2)
    gate_o[...] = jnp.where(lane == i1, w1, 0.0) + jnp.where(lane == i2, w2, 0.0)


def _post(ya, ob, bonus, g, gb, x2, W):
    n = x2.shape[0]
    tm = POST_ROWS
    assert n % tm == 0

    def full(a):
        return pl.BlockSpec(a.shape, lambda i: (0,) * a.ndim)

    def rows(width):
        return pl.BlockSpec((tm, width), lambda i: (i, 0))

    params = [W["lnx_g"], W["lnx_b"], W["ret_gn_g"], W["ret_gn_b"], W["w_out"], W["ln1_g"], W["ln1_b"],
              W["w_router"], W["b_router"]]
    return pl.pallas_call(
        _post_kernel,
        grid=(n // tm,),
        in_specs=[rows(A_WIDTH), rows(B_WIDTH), rows(A_WIDTH), rows(A_WIDTH), rows(B_WIDTH), rows(D_MODEL)]
        + [full(a) for a in params],
        out_specs=[rows(D_MODEL), rows(ROUTER_LANES)],
        out_shape=[jax.ShapeDtypeStruct((n, D_MODEL), F32), jax.ShapeDtypeStruct((n, ROUTER_LANES), F32)],
        compiler_params=pltpu.CompilerParams(dimension_semantics=("parallel",),
                                             vmem_limit_bytes=V7X_VMEM_LIMIT_BYTES),
        name="post",
    )(ya, ob, bonus, g, gb, x2, *params)


def _ffn_kernel(h_ref, gate_ref, p_ref, w1_ref, w3_ref, w2_ref, ln2g_ref, ln2b_ref, wple_ref, wpg_ref, pleg_ref,
                o_ref, hb_scr, acc_scr):
    tm = h_ref.shape[0]
    e_idx = pl.program_id(1)

    @pl.when(e_idx == 0)
    def _():
        hb_scr[...] = h_ref[...].astype(BF16)
        acc_scr[...] = jnp.zeros_like(acc_scr)

    hb = hb_scr[...]
    gate = gate_ref[...]
    lane = lax.broadcasted_iota(jnp.int32, gate.shape, 1)
    for j in range(FFN_EXPERTS_PER_STEP):
        e = e_idx * FFN_EXPERTS_PER_STEP + j
        ge = jnp.sum(jnp.where(lane == FINE_LANE0 + e, gate, 0.0), axis=-1, keepdims=True)
        a = jnp.dot(hb, w1_ref[j], preferred_element_type=F32)
        b = jnp.dot(hb, w3_ref[j], preferred_element_type=F32)
        hid = (a * _sigmoid(a)) * b * ge
        acc_scr[...] += jnp.dot(hid.astype(BF16), w2_ref[j], preferred_element_type=F32)

    @pl.when(e_idx == pl.num_programs(1) - 1)
    def _():
        h2 = _layer_norm(DEEPNORM_ALPHA * h_ref[...] + acc_scr[...], ln2g_ref[...], ln2b_ref[...])
        ple = _dot(p_ref[...], wple_ref[...]) * _sigmoid(_dot(h2, wpg_ref[...]))
        ms = jnp.mean(ple * ple, axis=-1, keepdims=True)
        o_ref[...] = h2 + ple * lax.rsqrt(ms + LN_EPS) * pleg_ref[...]


def _ffn(h, gate, p2, W):
    n = h.shape[0]
    tm = FFN_ROWS
    epb = FFN_EXPERTS_PER_STEP
    assert n % tm == 0 and N_EXPERTS % epb == 0

    def full(a):
        return pl.BlockSpec(a.shape, lambda i, e: (0,) * a.ndim)

    def rows(width):
        return pl.BlockSpec((tm, width), lambda i, e: (i, 0))

    params = [W["ln2_g"], W["ln2_b"], W["w_ple"], W["w_ple_gate"], W["ple_norm_g"]]
    return pl.pallas_call(
        _ffn_kernel,
        grid=(n // tm, N_EXPERTS // epb),
        in_specs=[rows(D_MODEL), rows(ROUTER_LANES), rows(D_PLE),
                  pl.BlockSpec((epb, D_MODEL, D_EXPERT), lambda i, e: (e, 0, 0)),
                  pl.BlockSpec((epb, D_MODEL, D_EXPERT), lambda i, e: (e, 0, 0)),
                  pl.BlockSpec((epb, D_EXPERT, D_MODEL), lambda i, e: (e, 0, 0))]
        + [full(a) for a in params],
        out_specs=rows(D_MODEL),
        out_shape=jax.ShapeDtypeStruct((n, D_MODEL), F32),
        scratch_shapes=[pltpu.VMEM((tm, D_MODEL), BF16), pltpu.VMEM((tm, D_MODEL), F32)],
        compiler_params=pltpu.CompilerParams(dimension_semantics=("parallel", "arbitrary"),
                                             vmem_limit_bytes=V7X_VMEM_LIMIT_BYTES),
        name="ffn",
    )(h, gate, p2, W["expert_w1"], W["expert_w3"], W["expert_w2"], *params)


def _layer(x, p, x_prev, wkv0, ret0, pos0, W):
    n_seq, seq_len, _ = x.shape
    n = n_seq * seq_len
    x2 = x.reshape(n, D_MODEL)
    r, lw, k, v, al, be, g, bonus, qb, kb, vb, gb = _proj(x2, x_prev, seq_len, pos0, W)
    ya, ob, wkv1, ret1 = _mixer((r, lw, k, v, al, be, qb, kb, vb), wkv0, ret0, n_seq, seq_len)
    h, gate = _post(ya, ob, bonus, g, gb, x2, W)
    out = _ffn(h, gate, p.reshape(n, D_PLE), W)
    return out.reshape(n_seq, seq_len, D_MODEL), x[:, -1], wkv1, ret1


def _prep_weights(i, w_in, mu_shift, w_decay_up, decay_base, w_aaa_up, aaa_base, w_gate_up, k_k, k_a, r_k,
                  lnx_g, lnx_b, ret_gn_g, ret_gn_b, w_out, ln1_g, ln1_b,
                  router_coarse_w, router_coarse_b, router_fine_w, router_fine_b,
                  expert_w1, expert_w3, expert_w2, ln2_g, ln2_b, w_ple, w_ple_gate, ple_norm_g):
    def row(a):
        return a[i].reshape(1, -1).astype(F32)

    pad = ROUTER_LANES - N_GROUPS - N_EXPERTS
    w_router = jnp.concatenate([router_coarse_w[i], router_fine_w[i], jnp.zeros((D_MODEL, pad), F32)], axis=1)
    b_router = jnp.concatenate([router_coarse_b[i], router_fine_b[i], jnp.zeros((pad,), F32)]).reshape(1, -1)
    return {
        "w_in": w_in[i].astype(BF16), "mu_shift": row(mu_shift), "w_decay_up": w_decay_up[i].astype(BF16),
        "decay_base": row(decay_base), "w_aaa_up": w_aaa_up[i].astype(BF16), "aaa_base": row(aaa_base),
        "w_gate_up": w_gate_up[i].astype(BF16), "k_k": row(k_k), "k_a": row(k_a), "r_k": row(r_k),
        "lnx_g": row(lnx_g), "lnx_b": row(lnx_b), "ret_gn_g": row(ret_gn_g), "ret_gn_b": row(ret_gn_b),
        "w_out": w_out[i].astype(BF16), "ln1_g": row(ln1_g), "ln1_b": row(ln1_b),
        "w_router": w_router, "b_router": b_router,
        "expert_w1": expert_w1[i].astype(BF16), "expert_w3": expert_w3[i].astype(BF16),
        "expert_w2": expert_w2[i].astype(BF16), "ln2_g": row(ln2_g), "ln2_b": row(ln2_b),
        "w_ple": w_ple[i].astype(BF16), "w_ple_gate": w_ple_gate[i].astype(BF16), "ple_norm_g": row(ple_norm_g),
    }


def kernel(x_prompt, x_sample, p_prompt, p_sample, state_wkv, state_shift, state_ret, w_in, mu_shift, w_decay_up, decay_base, w_aaa_up, aaa_base, w_gate_up, k_k, k_a, r_k, lnx_g, lnx_b, ret_gn_g, ret_gn_b, w_out, ln1_g, ln1_b, router_coarse_w, router_coarse_b, router_fine_w, router_fine_b, expert_w1, expert_w3, expert_w2, ln2_g, ln2_b, w_ple, w_ple_gate, ple_norm_g):
    yp, ys = x_prompt, x_sample
    nb = x_prompt.shape[0]
    depth = w_in.shape[0]
    wkv_p, shift_p, ret_p, wkv_s, shift_s, ret_s = [], [], [], [], [], []
    for i in range(depth):
        W = _prep_weights(i, w_in, mu_shift, w_decay_up, decay_base, w_aaa_up, aaa_base, w_gate_up, k_k, k_a, r_k,
                          lnx_g, lnx_b, ret_gn_g, ret_gn_b, w_out, ln1_g, ln1_b,
                          router_coarse_w, router_coarse_b, router_fine_w, router_fine_b,
                          expert_w1, expert_w3, expert_w2, ln2_g, ln2_b, w_ple, w_ple_gate, ple_norm_g)
        yp, sp, wp, rp = _layer(yp, p_prompt[i], jnp.zeros((nb, D_MODEL), F32),
                                jnp.zeros((nb, A_HEADS, A_HEAD_DIM, A_HEAD_DIM), F32),
                                jnp.zeros((nb, B_HEADS, B_QK_DIM, B_V_DIM), F32), 0, W)
        ys, ss, wsm, rsm = _layer(ys, p_sample[i], state_shift[i], state_wkv[i], state_ret[i], PAST_LEN, W)
        wkv_p.append(wp); shift_p.append(sp); ret_p.append(rp)
        wkv_s.append(wsm); shift_s.append(ss); ret_s.append(rsm)
    return (yp, ys, jnp.stack(wkv_p, 0), jnp.stack(shift_p, 0), jnp.stack(ret_p, 0),
            jnp.stack(wkv_s, 0), jnp.stack(shift_s, 0), jnp.stack(ret_s, 0))
```

```python
import functools
import math

import numpy as np
import jax
import jax.numpy as jnp
from jax import lax
from jax.experimental import pallas as pl
from jax.experimental.pallas import tpu as pltpu

F32 = jnp.float32
BF16 = jnp.bfloat16

D_MODEL = 1024
D_PLE = 256
A_HEADS = 8
A_HEAD_DIM = 64
A_WIDTH = A_HEADS * A_HEAD_DIM
DECAY_LORA = 64
AAA_LORA = 64
GATE_LORA = 128
GN_EPS_RWKV = 64e-5
B_HEADS = 4
B_QK_DIM = 64
B_V_DIM = 128
B_QK_WIDTH = B_HEADS * B_QK_DIM
B_WIDTH = B_HEADS * B_V_DIM
ROPE_BASE = 10000.0
GN_EPS = 1e-5
SHIFT_WIDTH = 3 * A_WIDTH + DECAY_LORA + AAA_LORA + GATE_LORA
IN_WIDTH = SHIFT_WIDTH + 2 * B_QK_WIDTH + 2 * B_WIDTH
N_GROUPS = 4
EXPERTS_PER_GROUP = 4
N_EXPERTS = N_GROUPS * EXPERTS_PER_GROUP
D_EXPERT = 256
DEPTH = 1
PAST_LEN = 16384
DEEPNORM_ALPHA = (2 * DEPTH) ** 0.25
LN_EPS = 1e-5

V7X_LANES = 128
V7X_SUBLANES = 8
V7X_VMEM_LIMIT_BYTES = 56 * 1024 * 1024

PROJ_ROWS = 512
LOG_DECAY_OUT = 1
CHUNK_ROWS = 64
MIXER_TILES = 4
MIXER_TILES_PACKED = 2
POST_ROWS = 512
FFN_ROWS = 1024
FFN_EXPERTS_PER_STEP = 2
ROUTER_LANES = V7X_LANES
FINE_LANE0 = N_GROUPS


def _dot(a, b):
    return jnp.dot(a.astype(BF16), b.astype(BF16), preferred_element_type=F32)


def _dot_nt(a, b):
    return lax.dot_general(a.astype(BF16), b.astype(BF16), (((1,), (1,)), ((), ())), preferred_element_type=F32)


def _dot_tn(a, b):
    return lax.dot_general(a.astype(BF16), b.astype(BF16), (((0,), (0,)), ((), ())), preferred_element_type=F32)


def _split2(x):
    hi = x.astype(BF16)
    lo = (x - hi.astype(F32)).astype(BF16)
    return hi, lo


def _split3(x):
    hi = x.astype(BF16)
    r1 = x - hi.astype(F32)
    mid = r1.astype(BF16)
    lo = (r1 - mid.astype(F32)).astype(BF16)
    return hi, mid, lo


def _sigmoid(x):
    return 1.0 / (1.0 + jnp.exp(-x))


def _group_ones(width, group):
    r = lax.broadcasted_iota(jnp.int32, (width, width), 0) // group
    c = lax.broadcasted_iota(jnp.int32, (width, width), 1) // group
    return jnp.where(r == c, 1.0, 0.0).astype(BF16)


def _group_sum(x, ones):
    return jnp.dot(x.astype(BF16), ones, preferred_element_type=F32)


def _proj_kernel(carry_mode, seq_len, tiles_per_seq,
                 x_ref, xp_ref, w_ref, mu_ref, wdec_ref, dbase_ref, waaa_ref, abase_ref, wgate_ref,
                 kk_ref, ka_ref, rk_ref, cos_ref, sin_ref,
                 r_o, lw_o, k_o, v_o, al_o, be_o, g_o, bonus_o, qb_o, kb_o, vb_o, gb_o,
                 carry_scr):
    tm = x_ref.shape[0]
    row = lax.broadcasted_iota(jnp.int32, (tm, SHIFT_WIDTH), 0)
    if carry_mode:
        xp = jnp.broadcast_to(xp_ref[0], (V7X_SUBLANES, D_MODEL))
        xb = jnp.concatenate([x_ref[...], xp], axis=0).astype(BF16)
        proj_all = jnp.dot(xb, w_ref[...], preferred_element_type=F32)
        proj = proj_all[:tm]
        cur_s = proj[:, :SHIFT_WIDTH]
        rolled = pltpu.roll(cur_s, 1, 0)
        j = pl.program_id(0) % tiles_per_seq

        @pl.when(pl.program_id(0) == 0)
        def _():
            carry_scr[...] = jnp.zeros_like(carry_scr)

        first = jnp.where(j == 0, proj_all[tm + V7X_SUBLANES - 1:, :SHIFT_WIDTH],
                          carry_scr[V7X_SUBLANES - 1:V7X_SUBLANES, :])
        prev = jnp.where(row == 0, first, rolled)
        carry_scr[...] = cur_s[tm - V7X_SUBLANES:, :]
    else:
        xb = x_ref[...].astype(BF16)
        proj = jnp.dot(xb, w_ref[...], preferred_element_type=F32)
        cur_s = proj[:, :SHIFT_WIDTH]
        rolled = pltpu.roll(cur_s, 1, 0)
        first = jnp.dot(xp_ref[...].astype(BF16), w_ref[:, :SHIFT_WIDTH], preferred_element_type=F32)
        prev = jnp.where((row & (seq_len - 1)) == 0, first, rolled)
    sh = cur_s + (prev - cur_s) * mu_ref[...]

    r = sh[:, :A_WIDTH]
    k0 = sh[:, A_WIDTH:2 * A_WIDTH]
    v = sh[:, 2 * A_WIDTH:3 * A_WIDTH]
    o = 3 * A_WIDTH
    w_lo = sh[:, o:o + DECAY_LORA]
    a_lo = sh[:, o + DECAY_LORA:o + DECAY_LORA + AAA_LORA]
    g_lo = sh[:, o + DECAY_LORA + AAA_LORA:SHIFT_WIDTH]

    z = -(dbase_ref[...] + _dot(jnp.tanh(w_lo), wdec_ref[...]))
    softplus = jnp.maximum(z, 0.0) + jnp.log1p(jnp.exp(-jnp.abs(z)))
    log_w = -softplus - 0.5
    lw = -jnp.exp(log_w)
    a = _sigmoid(abase_ref[...] + _dot(a_lo, waaa_ref[...]))
    g = _dot(_sigmoid(g_lo), wgate_ref[...])

    ones64 = _group_ones(A_WIDTH, A_HEAD_DIM)
    kk0 = k0 * kk_ref[...]
    ssq = _group_sum(kk0 * kk0, ones64)
    kk = kk0 / jnp.maximum(jnp.sqrt(ssq), 1e-12)
    k = k0 * (1.0 + (a - 1.0) * ka_ref[...])
    bonus = _group_sum(r * k * rk_ref[...], ones64) * v

    r_o[...] = (r).astype(r_o.dtype)
    lw_o[...] = lw
    k_o[...] = (k).astype(k_o.dtype)
    v_o[...] = (v).astype(v_o.dtype)
    al_o[...] = (-kk).astype(al_o.dtype)
    be_o[...] = (kk * a).astype(be_o.dtype)
    g_o[...] = (g).astype(g_o.dtype)
    bonus_o[...] = (bonus).astype(bonus_o.dtype)

    o = SHIFT_WIDTH
    q_b = proj[:, o:o + B_QK_WIDTH]
    k_b = proj[:, o + B_QK_WIDTH:o + 2 * B_QK_WIDTH]
    v_b = proj[:, o + 2 * B_QK_WIDTH:o + 2 * B_QK_WIDTH + B_WIDTH]
    g_b = proj[:, o + 2 * B_QK_WIDTH + B_WIDTH:]
    lane = lax.broadcasted_iota(jnp.int32, (tm, B_QK_WIDTH), 1)
    first_half = (lane & (B_QK_DIM - 1)) < (B_QK_DIM // 2)
    cos = cos_ref[...]
    sin = sin_ref[...]

    def rot(t):
        swapped = jnp.where(first_half, pltpu.roll(t, B_QK_WIDTH - B_QK_DIM // 2, 1), pltpu.roll(t, B_QK_DIM // 2, 1))
        return t * cos + swapped * sin

    qb_o[...] = (rot(q_b)).astype(qb_o.dtype)
    kb_o[...] = (rot(k_b) * (B_QK_DIM ** -0.5)).astype(kb_o.dtype)
    vb_o[...] = (v_b).astype(vb_o.dtype)
    gb_o[...] = (g_b * _sigmoid(g_b)).astype(gb_o.dtype)


def _proj(x2, x_prev, seq_len, pos0, W):
    n = x2.shape[0]
    tm = PROJ_ROWS
    assert n % tm == 0
    carry_mode = seq_len % tm == 0
    if carry_mode:
        tiles_per_seq = seq_len // tm
        xp = x_prev.reshape(-1, 1, D_MODEL)
        xp_spec = pl.BlockSpec((1, 1, D_MODEL), lambda i: (i // tiles_per_seq, 0, 0))
        tab_rows = seq_len
    else:
        assert tm % seq_len == 0 and seq_len & (seq_len - 1) == 0
        tiles_per_seq = 1
        xp = jnp.repeat(x_prev, seq_len, axis=0)
        xp_spec = pl.BlockSpec((tm, D_MODEL), lambda i: (i, 0))
        tab_rows = tm
    half = B_QK_DIM // 2
    inv = ROPE_BASE ** (-jnp.arange(half, dtype=F32) / half)
    pos = (pos0 + jnp.arange(seq_len, dtype=jnp.int32)).astype(F32)
    ang = pos[:, None] * inv[None, :]
    cos = jnp.tile(jnp.concatenate([jnp.cos(ang), jnp.cos(ang)], -1), (tab_rows // seq_len, B_HEADS))
    sin = jnp.tile(jnp.concatenate([-jnp.sin(ang), jnp.sin(ang)], -1), (tab_rows // seq_len, B_HEADS))
    tab_tiles = tab_rows // tm
    tab_spec = pl.BlockSpec((tm, B_QK_WIDTH), lambda i: (i % tab_tiles, 0))

    def full(a):
        return pl.BlockSpec(a.shape, lambda i: (0,) * a.ndim)

    def rows(width):
        return pl.BlockSpec((tm, width), lambda i: (i, 0))

    params = [W["w_in"], W["mu_shift"], W["w_decay_up"], W["decay_base"], W["w_aaa_up"], W["aaa_base"],
              W["w_gate_up"], W["k_k"], W["k_a"], W["r_k"]]
    widths = [A_WIDTH] * 8 + [B_QK_WIDTH, B_QK_WIDTH, B_WIDTH, B_WIDTH]
    outs = pl.pallas_call(
        functools.partial(_proj_kernel, carry_mode, seq_len, tiles_per_seq),
        grid=(n // tm,),
        in_specs=[rows(D_MODEL), xp_spec, pl.BlockSpec(params[0].shape, lambda i: (0, 0), pipeline_mode=pl.Buffered(1))]
        + [full(a) for a in params[1:]] + [tab_spec, tab_spec],
        out_specs=[rows(w) for w in widths],
        out_shape=[jax.ShapeDtypeStruct((n, w), F32 if i == LOG_DECAY_OUT else BF16) for i, w in enumerate(widths)],
        scratch_shapes=[pltpu.VMEM((V7X_SUBLANES, SHIFT_WIDTH), F32)],
        compiler_params=pltpu.CompilerParams(dimension_semantics=("arbitrary",),
                                             vmem_limit_bytes=V7X_VMEM_LIMIT_BYTES),
        name="proj",
    )(x2, xp, *params, cos, sin)
    return outs


def _mixer_kernel(nb, seqs, clen,
                  r_ref, lw_ref, k_ref, v_ref, al_ref, be_ref, qb_ref, kb_ref, vb_ref, wkv0_ref, ret0_ref,
                  ya_o, ob_o, wkv_o, ret_o, s_scr, r_scr):
    R = seqs * clen
    log2c = int(math.log2(clen))
    c_idx = pl.program_id(1)
    hd = A_HEAD_DIM
    TH = [(t, h) for t in range(nb) for h in range(A_HEADS)]
    TG = [(t, h) for t in range(nb) for h in range(B_HEADS)]

    @pl.when(c_idx == 0)
    def _():
        for t, h in TH:
            blocks = [wkv0_ref[t * seqs + i, h] for i in range(seqs)]
            s_scr[t * A_HEADS + h] = jnp.concatenate(blocks, axis=1) if seqs > 1 else blocks[0]
        for t, h in TG:
            r_scr[t * B_HEADS + h] = ret0_ref[t * seqs:(t + 1) * seqs, h].reshape(seqs * B_QK_DIM, B_V_DIM)

    row = lax.broadcasted_iota(jnp.int32, (R, R), 0)
    col = lax.broadcasted_iota(jnp.int32, (R, R), 1)
    same = (row >> log2c) == (col >> log2c)
    incl = same & (col <= row)
    strict = same & (col < row)
    m_incl = jnp.where(incl, 1.0, 0.0).astype(BF16)
    m_same = jnp.where(same, 1.0, 0.0).astype(BF16)
    eye = jnp.where(row == col, 1.0, 0.0).astype(F32)

    def expand(t):
        if seqs == 1:
            return t
        w = t.shape[1]
        wide = jnp.concatenate([t] * seqs, axis=1)
        rr = lax.broadcasted_iota(jnp.int32, wide.shape, 0) >> log2c
        cc = lax.broadcasted_iota(jnp.int32, wide.shape, 1) // w
        return jnp.where(rr == cc, wide, 0.0)

    def head(x, h):
        return x[:, h * hd:(h + 1) * hd]

    a_bar, r_bar, b_til, k_til, b_dec, k_dec, d_end, vv = [], [], [], [], [], [], [], []
    for t in range(nb):
        lw = lw_ref[t]
        parts = _split3(lw)
        c = sum(jnp.dot(m_incl, p, preferred_element_type=F32) for p in parts)
        cend = sum(jnp.dot(m_same, p, preferred_element_type=F32) for p in parts)
        einv = jnp.exp(-c)
        edec = jnp.exp(cend - c)
        a_bar.append(al_ref[t] * jnp.exp(c - lw))
        r_bar.append(r_ref[t] * jnp.exp(c))
        b_til.append(be_ref[t] * einv)
        k_til.append(k_ref[t] * einv)
        b_dec.append(be_ref[t] * edec)
        k_dec.append(k_ref[t] * edec)
        d_end.append(jnp.exp(cend))
        vv.append(v_ref[t].astype(F32))
    last_row = (lax.broadcasted_iota(jnp.int32, (R, seqs * hd), 0) & (clen - 1)) == clen - 1

    amats = [_dot_nt(jnp.concatenate([head(a_bar[t], h), head(r_bar[t], h)], axis=0),
                     jnp.concatenate([head(b_til[t], h), head(k_til[t], h)], axis=0)) for t, h in TH]

    lgs = [float(np.log1p(-np.exp2(-5.0 - h))) for h in range(B_HEADS)]
    qb = [qb_ref[t].astype(F32) for t in range(nb)]
    kb = [kb_ref[t].astype(F32) for t in range(nb)]
    qs = [qb[t][:, h * B_QK_DIM:(h + 1) * B_QK_DIM] for t, h in TG]
    khs = [kb[t][:, h * B_QK_DIM:(h + 1) * B_QK_DIM] for t, h in TG]
    vbs = [vb_ref[t][:, h * B_V_DIM:(h + 1) * B_V_DIM] for t, h in TG]
    diff = (row - col).astype(F32)
    pos_v = (lax.broadcasted_iota(jnp.int32, (R, B_V_DIM), 0) & (clen - 1)).astype(F32)
    pos_k = (lax.broadcasted_iota(jnp.int32, (R, B_QK_DIM), 0) & (clen - 1)).astype(F32)
    intra = [jnp.where(incl, jnp.exp(lg * diff), 0.0) for lg in lgs]
    cross = [jnp.exp(lg * (pos_v + 1.0)) for lg in lgs]
    kdec = [jnp.exp(lg * (clen - 1.0 - pos_k)) for lg in lgs]
    rstates = [r_scr[t * B_HEADS + h] for t, h in TG]
    scs = [_dot_nt(qs[i], khs[i]) * intra[h] for i, (t, h) in enumerate(TG)]
    qst = [_dot(expand(qs[i]), rstates[i]) * cross[h] for i, (t, h) in enumerate(TG)]

    a_ab = [jnp.where(strict, m[:R, :R], 0.0) for m in amats]
    a_ak = [jnp.where(strict, m[:R, R:], 0.0) for m in amats]
    a_rb = [jnp.where(incl, m[R:, :R], 0.0) for m in amats]
    a_rk = [jnp.where(incl, m[R:, R:], 0.0) for m in amats]
    n = len(TH)
    tinv = [eye + a for a in a_ab]
    vhs = [head(vv[t], h) for t, h in TH]
    av = [_dot(a_ak[i], vhs[i]) for i in range(n)]
    if log2c > 1:
        pw = [_dot(a, a) for a in a_ab]
    for it in range(log2c - 1):
        if it < log2c - 2:
            tp = [_dot(pw[i], jnp.concatenate([tinv[i], pw[i]], axis=1)) for i in range(n)]
            tinv = [tinv[i] + tp[i][:, :R] for i in range(n)]
            pw = [tp[i][:, R:] for i in range(n)]
        else:
            tinv = [tinv[i] + _dot(pw[i], tinv[i]) for i in range(n)]

    os_ = [_dot(scs[i], vbs[i]) + qst[i] for i in range(len(TG))]
    for i, (t, h) in enumerate(TG):
        r_scr[t * B_HEADS + h] = (rstates[i] * float(np.exp(lgs[h] * clen))
                                  + _dot_tn(expand(khs[i] * kdec[h]), vbs[i]))
    for t in range(nb):
        ob_o[t] = jnp.concatenate(os_[t * B_HEADS:(t + 1) * B_HEADS], axis=1)

    wu = [_dot(tinv[i], jnp.concatenate([head(a_bar[t], h), av[i]], axis=1)) for i, (t, h) in enumerate(TH)]
    states = [s_scr[t * A_HEADS + h] for t, h in TH]
    ws = [_dot_nt(jnp.concatenate([expand(wu[i][:, :hd]), expand(head(r_bar[t], h))], axis=0), states[i])
          for i, (t, h) in enumerate(TH)]
    uv = [jnp.concatenate([ws[i][:R] + wu[i][:, hd:], vhs[i]], axis=0) for i in range(n)]
    ys = [ws[i][R:] + _dot(jnp.concatenate([a_rb[i], a_rk[i]], axis=1), uv[i]) for i in range(n)]
    for t in range(nb):
        ya_o[t] = jnp.concatenate(ys[t * A_HEADS:(t + 1) * A_HEADS], axis=1)
    for i, (t, h) in enumerate(TH):
        d_row = jnp.sum(jnp.where(last_row, expand(head(d_end[t], h)), 0.0), axis=0, keepdims=True)
        bk = jnp.concatenate([expand(head(b_dec[t], h)), expand(head(k_dec[t], h))], axis=0)
        s_scr[t * A_HEADS + h] = states[i] * d_row + _dot_tn(uv[i], bk)

    @pl.when(c_idx == pl.num_programs(1) - 1)
    def _():
        for t, h in TH:
            st = s_scr[t * A_HEADS + h]
            for i in range(seqs):
                wkv_o[t * seqs + i, h] = st[:, i * hd:(i + 1) * hd]
        for t, h in TG:
            ret_o[t * seqs:(t + 1) * seqs, h] = r_scr[t * B_HEADS + h].reshape(seqs, B_QK_DIM, B_V_DIM)


def _mixer(ops, wkv0, ret0, n_seq, seq_len):
    n = ops[0].shape[0]
    R = CHUNK_ROWS
    if seq_len >= R:
        assert seq_len % R == 0
        seqs, clen, nchunks = 1, R, seq_len // R
    else:
        assert R % seq_len == 0 and seq_len & (seq_len - 1) == 0 and n_seq % (R // seq_len) == 0
        seqs, clen, nchunks = R // seq_len, seq_len, 1
    ntiles = n_seq // seqs
    nb = MIXER_TILES if seqs == 1 else MIXER_TILES_PACKED
    assert ntiles % nb == 0
    ops3 = [a.reshape(ntiles, nchunks * R, a.shape[1]) for a in ops]

    def rows(width):
        return pl.BlockSpec((nb, R, width), lambda i, c: (i, c, 0))

    wkv_spec = pl.BlockSpec((nb * seqs, A_HEADS, A_HEAD_DIM, A_HEAD_DIM), lambda i, c: (i, 0, 0, 0))
    ret_spec = pl.BlockSpec((nb * seqs, B_HEADS, B_QK_DIM, B_V_DIM), lambda i, c: (i, 0, 0, 0))
    ya, ob, wkv1, ret1 = pl.pallas_call(
        functools.partial(_mixer_kernel, nb, seqs, clen),
        grid=(ntiles // nb, nchunks),
        in_specs=[rows(A_WIDTH)] * 6 + [rows(B_QK_WIDTH), rows(B_QK_WIDTH), rows(B_WIDTH), wkv_spec, ret_spec],
        out_specs=[rows(A_WIDTH), rows(B_WIDTH), wkv_spec, ret_spec],
        out_shape=[jax.ShapeDtypeStruct((ntiles, nchunks * R, A_WIDTH), F32),
                   jax.ShapeDtypeStruct((ntiles, nchunks * R, B_WIDTH), F32),
                   jax.ShapeDtypeStruct(wkv0.shape, F32), jax.ShapeDtypeStruct(ret0.shape, F32)],
        scratch_shapes=[pltpu.VMEM((nb * A_HEADS, A_HEAD_DIM, seqs * A_HEAD_DIM), F32),
                        pltpu.VMEM((nb * B_HEADS, seqs * B_QK_DIM, B_V_DIM), F32)],
        compiler_params=pltpu.CompilerParams(dimension_semantics=("parallel", "arbitrary"),
                                             vmem_limit_bytes=V7X_VMEM_LIMIT_BYTES),
        name="mixer",
    )(*ops3, wkv0, ret0)
    return ya.reshape(n, A_WIDTH), ob.reshape(n, B_WIDTH), wkv1, ret1


def _layer_norm(z, g, b):
    mu = jnp.mean(z, axis=-1, keepdims=True)
    d = z - mu
    var = jnp.mean(d * d, axis=-1, keepdims=True)
    return d * lax.rsqrt(var + LN_EPS) * g + b


def _post_kernel(ya_ref, ob_ref, bonus_ref, g_ref, gb_ref, x_ref, lnxg_ref, lnxb_ref, rgg_ref, rgb_ref,
                 wout_ref, ln1g_ref, ln1b_ref, wr_ref, br_ref, h_o, gate_o):
    tm = x_ref.shape[0]

    def head_norm(t, group, eps, gg, bb):
        ones = _group_ones(t.shape[1], group)
        mu = _group_sum(t, ones) * (1.0 / group)
        d = t - mu
        var = _group_sum(d * d, ones) * (1.0 / group)
        return d * lax.rsqrt(var + eps) * gg + bb

    y_a = (head_norm(ya_ref[...], A_HEAD_DIM, GN_EPS_RWKV, lnxg_ref[...], lnxb_ref[...]) + bonus_ref[...]) * g_ref[...]
    y_b = head_norm(ob_ref[...], B_V_DIM, GN_EPS, rgg_ref[...], rgb_ref[...]) * gb_ref[...]
    y = jnp.concatenate([y_a, y_b], axis=1)
    mix = _dot(y, wout_ref[...])
    h = _layer_norm(DEEPNORM_ALPHA * x_ref[...] + mix, ln1g_ref[...], ln1b_ref[...])
    h_o[...] = h

    h_hi, h_lo = _split2(h)
    w_hi, w_lo = _split2(wr_ref[...])
    logits = (jnp.dot(h_hi, w_hi, preferred_element_type=F32) + jnp.dot(h_hi, w_lo, preferred_element_type=F32)
              + jnp.dot(h_lo, w_hi, preferred_element_type=F32)) + br_ref[...]
    lane = lax.broadcasted_iota(jnp.int32, (tm, ROUTER_LANES), 1)
    neg = -jnp.inf
    big = ROUTER_LANES
    cl = jnp.where(lane < N_GROUPS, logits, neg)
    cmax = jnp.max(cl, axis=-1, keepdims=True)
    grp = jnp.min(jnp.where(cl == cmax, lane, big), axis=-1, keepdims=True)
    gprob = 1.0 / jnp.sum(jnp.exp(cl - cmax), axis=-1, keepdims=True)
    lo_lane = FINE_LANE0 + grp * EXPERTS_PER_GROUP
    fv = jnp.where((lane >= lo_lane) & (lane < lo_lane + EXPERTS_PER_GROUP), logits, neg)
    m1 = jnp.max(fv, axis=-1, keepdims=True)
    i1 = jnp.min(jnp.where(fv == m1, lane, big), axis=-1, keepdims=True)
    fv2 = jnp.where(lane == i1, neg, fv)
    m2 = jnp.max(fv2, axis=-1, keepdims=True)
    i2 = jnp.min(jnp.where(fv2 == m2, lane, big), axis=-1, keepdims=True)
    e2 = jnp.exp(m2 - m1)
    w1 = gprob / (1.0 + e2)
    w2 = gprob * e2 / (1.0 + e2)
    gate_o[...] = jnp.where(lane == i1, w1, 0.0) + jnp.where(lane == i2, w2, 0.0)


def _post(ya, ob, bonus, g, gb, x2, W):
    n = x2.shape[0]
    tm = POST_ROWS
    assert n % tm == 0

    def full(a):
        return pl.BlockSpec(a.shape, lambda i: (0,) * a.ndim)

    def rows(width):
        return pl.BlockSpec((tm, width), lambda i: (i, 0))

    params = [W["lnx_g"], W["lnx_b"], W["ret_gn_g"], W["ret_gn_b"], W["w_out"], W["ln1_g"], W["ln1_b"],
              W["w_router"], W["b_router"]]
    return pl.pallas_call(
        _post_kernel,
        grid=(n // tm,),
        in_specs=[rows(A_WIDTH), rows(B_WIDTH), rows(A_WIDTH), rows(A_WIDTH), rows(B_WIDTH), rows(D_MODEL)]
        + [full(a) for a in params],
        out_specs=[rows(D_MODEL), rows(ROUTER_LANES)],
        out_shape=[jax.ShapeDtypeStruct((n, D_MODEL), F32), jax.ShapeDtypeStruct((n, ROUTER_LANES), F32)],
        compiler_params=pltpu.CompilerParams(dimension_semantics=("parallel",),
                                             vmem_limit_bytes=V7X_VMEM_LIMIT_BYTES),
        name="post",
    )(ya, ob, bonus, g, gb, x2, *params)


def _ffn_kernel(h_ref, gate_ref, p_ref, w1_ref, w3_ref, w2_ref, ln2g_ref, ln2b_ref, wple_ref, wpg_ref, pleg_ref,
                o_ref, hb_scr, acc_scr):
    tm = h_ref.shape[0]
    e_idx = pl.program_id(1)

    @pl.when(e_idx == 0)
    def _():
        hb_scr[...] = h_ref[...].astype(BF16)
        acc_scr[...] = jnp.zeros_like(acc_scr)

    hb = hb_scr[...]
    gate = gate_ref[...]
    lane = lax.broadcasted_iota(jnp.int32, gate.shape, 1)
    for j in range(FFN_EXPERTS_PER_STEP):
        e = e_idx * FFN_EXPERTS_PER_STEP + j
        ge = jnp.sum(jnp.where(lane == FINE_LANE0 + e, gate, 0.0), axis=-1, keepdims=True)
        a = jnp.dot(hb, w1_ref[j], preferred_element_type=F32)
        b = jnp.dot(hb, w3_ref[j], preferred_element_type=F32)
        hid = (a * _sigmoid(a)) * b * ge
        acc_scr[...] += jnp.dot(hid.astype(BF16), w2_ref[j], preferred_element_type=F32)

    @pl.when(e_idx == pl.num_programs(1) - 1)
    def _():
        h2 = _layer_norm(DEEPNORM_ALPHA * h_ref[...] + acc_scr[...], ln2g_ref[...], ln2b_ref[...])
        ple = _dot(p_ref[...], wple_ref[...]) * _sigmoid(_dot(h2, wpg_ref[...]))
        ms = jnp.mean(ple * ple, axis=-1, keepdims=True)
        o_ref[...] = h2 + ple * lax.rsqrt(ms + LN_EPS) * pleg_ref[...]


def _ffn(h, gate, p2, W):
    n = h.shape[0]
    tm = FFN_ROWS
    epb = FFN_EXPERTS_PER_STEP
    assert n % tm == 0 and N_EXPERTS % epb == 0

    def full(a):
        return pl.BlockSpec(a.shape, lambda i, e: (0,) * a.ndim)

    def rows(width):
        return pl.BlockSpec((tm, width), lambda i, e: (i, 0))

    params = [W["ln2_g"], W["ln2_b"], W["w_ple"], W["w_ple_gate"], W["ple_norm_g"]]
    return pl.pallas_call(
        _ffn_kernel,
        grid=(n // tm, N_EXPERTS // epb),
        in_specs=[rows(D_MODEL), rows(ROUTER_LANES), rows(D_PLE),
                  pl.BlockSpec((epb, D_MODEL, D_EXPERT), lambda i, e: (e, 0, 0)),
                  pl.BlockSpec((epb, D_MODEL, D_EXPERT), lambda i, e: (e, 0, 0)),
                  pl.BlockSpec((epb, D_EXPERT, D_MODEL), lambda i, e: (e, 0, 0))]
        + [full(a) for a in params],
        out_specs=rows(D_MODEL),
        out_shape=jax.ShapeDtypeStruct((n, D_MODEL), F32),
        scratch_shapes=[pltpu.VMEM((tm, D_MODEL), BF16), pltpu.VMEM((tm, D_MODEL), F32)],
        compiler_params=pltpu.CompilerParams(dimension_semantics=("parallel", "arbitrary"),
                                             vmem_limit_bytes=V7X_VMEM_LIMIT_BYTES),
        name="ffn",
    )(h, gate, p2, W["expert_w1"], W["expert_w3"], W["expert_w2"], *params)


def _layer(x, p, x_prev, wkv0, ret0, pos0, W):
    n_seq, seq_len, _ = x.shape
    n = n_seq * seq_len
    x2 = x.reshape(n, D_MODEL)
    r, lw, k, v, al, be, g, bonus, qb, kb, vb, gb = _proj(x2, x_prev, seq_len, pos0, W)
    ya, ob, wkv1, ret1 = _mixer((r, lw, k, v, al, be, qb, kb, vb), wkv0, ret0, n_seq, seq_len)
    h, gate = _post(ya, ob, bonus, g, gb, x2, W)
    out = _ffn(h, gate, p.reshape(n, D_PLE), W)
    return out.reshape(n_seq, seq_len, D_MODEL), x[:, -1], wkv1, ret1


def _prep_weights(i, w_in, mu_shift, w_decay_up, decay_base, w_aaa_up, aaa_base, w_gate_up, k_k, k_a, r_k,
                  lnx_g, lnx_b, ret_gn_g, ret_gn_b, w_out, ln1_g, ln1_b,
                  router_coarse_w, router_coarse_b, router_fine_w, router_fine_b,
                  expert_w1, expert_w3, expert_w2, ln2_g, ln2_b, w_ple, w_ple_gate, ple_norm_g):
    def row(a):
        return a[i].reshape(1, -1).astype(F32)

    pad = ROUTER_LANES - N_GROUPS - N_EXPERTS
    w_router = jnp.concatenate([router_coarse_w[i], router_fine_w[i], jnp.zeros((D_MODEL, pad), F32)], axis=1)
    b_router = jnp.concatenate([router_coarse_b[i], router_fine_b[i], jnp.zeros((pad,), F32)]).reshape(1, -1)
    return {
        "w_in": w_in[i].astype(BF16), "mu_shift": row(mu_shift), "w_decay_up": w_decay_up[i].astype(BF16),
        "decay_base": row(decay_base), "w_aaa_up": w_aaa_up[i].astype(BF16), "aaa_base": row(aaa_base),
        "w_gate_up": w_gate_up[i].astype(BF16), "k_k": row(k_k), "k_a": row(k_a), "r_k": row(r_k),
        "lnx_g": row(lnx_g), "lnx_b": row(lnx_b), "ret_gn_g": row(ret_gn_g), "ret_gn_b": row(ret_gn_b),
        "w_out": w_out[i].astype(BF16), "ln1_g": row(ln1_g), "ln1_b": row(ln1_b),
        "w_router": w_router, "b_router": b_router,
        "expert_w1": expert_w1[i].astype(BF16), "expert_w3": expert_w3[i].astype(BF16),
        "expert_w2": expert_w2[i].astype(BF16), "ln2_g": row(ln2_g), "ln2_b": row(ln2_b),
        "w_ple": w_ple[i].astype(BF16), "w_ple_gate": w_ple_gate[i].astype(BF16), "ple_norm_g": row(ple_norm_g),
    }


def kernel(x_prompt, x_sample, p_prompt, p_sample, state_wkv, state_shift, state_ret, w_in, mu_shift, w_decay_up, decay_base, w_aaa_up, aaa_base, w_gate_up, k_k, k_a, r_k, lnx_g, lnx_b, ret_gn_g, ret_gn_b, w_out, ln1_g, ln1_b, router_coarse_w, router_coarse_b, router_fine_w, router_fine_b, expert_w1, expert_w3, expert_w2, ln2_g, ln2_b, w_ple, w_ple_gate, ple_norm_g):
    yp, ys = x_prompt, x_sample
    nb = x_prompt.shape[0]
    depth = w_in.shape[0]
    wkv_p, shift_p, ret_p, wkv_s, shift_s, ret_s = [], [], [], [], [], []
    for i in range(depth):
        W = _prep_weights(i, w_in, mu_shift, w_decay_up, decay_base, w_aaa_up, aaa_base, w_gate_up, k_k, k_a, r_k,
                          lnx_g, lnx_b, ret_gn_g, ret_gn_b, w_out, ln1_g, ln1_b,
                          router_coarse_w, router_coarse_b, router_fine_w, router_fine_b,
                          expert_w1, expert_w3, expert_w2, ln2_g, ln2_b, w_ple, w_ple_gate, ple_norm_g)
        yp, sp, wp, rp = _layer(yp, p_prompt[i], jnp.zeros((nb, D_MODEL), F32),
                                jnp.zeros((nb, A_HEADS, A_HEAD_DIM, A_HEAD_DIM), F32),
                                jnp.zeros((nb, B_HEADS, B_QK_DIM, B_V_DIM), F32), 0, W)
        ys, ss, wsm, rsm = _layer(ys, p_sample[i], state_shift[i], state_wkv[i], state_ret[i], PAST_LEN, W)
        wkv_p.append(wp); shift_p.append(sp); ret_p.append(rp)
        wkv_s.append(wsm); shift_s.append(ss); ret_s.append(rsm)
    return (yp, ys, jnp.stack(wkv_p, 0), jnp.stack(shift_p, 0), jnp.stack(ret_p, 0),
            jnp.stack(wkv_s, 0), jnp.stack(shift_s, 0), jnp.stack(ret_s, 0))
```

```python
import functools
import math

import numpy as np
import jax
import jax.numpy as jnp
from jax import lax
from jax.experimental import pallas as pl
from jax.experimental.pallas import tpu as pltpu

F32 = jnp.float32
BF16 = jnp.bfloat16

D_MODEL = 1024
D_PLE = 256
A_HEADS = 8
A_HEAD_DIM = 64
A_WIDTH = A_HEADS * A_HEAD_DIM
DECAY_LORA = 64
AAA_LORA = 64
GATE_LORA = 128
GN_EPS_RWKV = 64e-5
B_HEADS = 4
B_QK_DIM = 64
B_V_DIM = 128
B_QK_WIDTH = B_HEADS * B_QK_DIM
B_WIDTH = B_HEADS * B_V_DIM
ROPE_BASE = 10000.0
GN_EPS = 1e-5
SHIFT_WIDTH = 3 * A_WIDTH + DECAY_LORA + AAA_LORA + GATE_LORA
IN_WIDTH = SHIFT_WIDTH + 2 * B_QK_WIDTH + 2 * B_WIDTH
N_GROUPS = 4
EXPERTS_PER_GROUP = 4
N_EXPERTS = N_GROUPS * EXPERTS_PER_GROUP
D_EXPERT = 256
DEPTH = 1
PAST_LEN = 16384
DEEPNORM_ALPHA = (2 * DEPTH) ** 0.25
LN_EPS = 1e-5

V7X_LANES = 128
V7X_SUBLANES = 8
V7X_VMEM_LIMIT_BYTES = 56 * 1024 * 1024

PROJ_ROWS = 512
LOG_DECAY_OUT = 1
CHUNK_ROWS = 64
MIXER_TILES = 4
MIXER_TILES_PACKED = 2
POST_ROWS = 512
SORT_ROWS = 512
SORT_ALIGN = 16
SORT_LOCAL = 640
EXPERT_ROWS = 1024
GROUP_LANE = 0
ROUTER_LANES = V7X_LANES
FINE_LANE0 = N_GROUPS


def _dot(a, b):
    return jnp.dot(a.astype(BF16), b.astype(BF16), preferred_element_type=F32)


def _dot_nt(a, b):
    return lax.dot_general(a.astype(BF16), b.astype(BF16), (((1,), (1,)), ((), ())), preferred_element_type=F32)


def _dot_tn(a, b):
    return lax.dot_general(a.astype(BF16), b.astype(BF16), (((0,), (0,)), ((), ())), preferred_element_type=F32)


def _split2(x):
    hi = x.astype(BF16)
    lo = (x - hi.astype(F32)).astype(BF16)
    return hi, lo


def _split3(x):
    hi = x.astype(BF16)
    r1 = x - hi.astype(F32)
    mid = r1.astype(BF16)
    lo = (r1 - mid.astype(F32)).astype(BF16)
    return hi, mid, lo


def _sigmoid(x):
    return 1.0 / (1.0 + jnp.exp(-x))


def _group_ones(width, group):
    r = lax.broadcasted_iota(jnp.int32, (width, width), 0) // group
    c = lax.broadcasted_iota(jnp.int32, (width, width), 1) // group
    return jnp.where(r == c, 1.0, 0.0).astype(BF16)


def _group_sum(x, ones):
    return jnp.dot(x.astype(BF16), ones, preferred_element_type=F32)


def _proj_kernel(carry_mode, seq_len, tiles_per_seq,
                 x_ref, xp_ref, w_ref, mu_ref, wdec_ref, dbase_ref, waaa_ref, abase_ref, wgate_ref,
                 kk_ref, ka_ref, rk_ref, cos_ref, sin_ref,
                 r_o, lw_o, k_o, v_o, al_o, be_o, g_o, bonus_o, qb_o, kb_o, vb_o, gb_o,
                 carry_scr):
    tm = x_ref.shape[0]
    row = lax.broadcasted_iota(jnp.int32, (tm, SHIFT_WIDTH), 0)
    if carry_mode:
        xp = jnp.broadcast_to(xp_ref[0], (V7X_SUBLANES, D_MODEL))
        xb = jnp.concatenate([x_ref[...], xp], axis=0).astype(BF16)
        proj_all = jnp.dot(xb, w_ref[...], preferred_element_type=F32)
        proj = proj_all[:tm]
        cur_s = proj[:, :SHIFT_WIDTH]
        rolled = pltpu.roll(cur_s, 1, 0)
        j = pl.program_id(0) % tiles_per_seq

        @pl.when(pl.program_id(0) == 0)
        def _():
            carry_scr[...] = jnp.zeros_like(carry_scr)

        first = jnp.where(j == 0, proj_all[tm + V7X_SUBLANES - 1:, :SHIFT_WIDTH],
                          carry_scr[V7X_SUBLANES - 1:V7X_SUBLANES, :])
        prev = jnp.where(row == 0, first, rolled)
        carry_scr[...] = cur_s[tm - V7X_SUBLANES:, :]
    else:
        xb = x_ref[...].astype(BF16)
        proj = jnp.dot(xb, w_ref[...], preferred_element_type=F32)
        cur_s = proj[:, :SHIFT_WIDTH]
        rolled = pltpu.roll(cur_s, 1, 0)
        first = jnp.dot(xp_ref[...].astype(BF16), w_ref[:, :SHIFT_WIDTH], preferred_element_type=F32)
        prev = jnp.where((row & (seq_len - 1)) == 0, first, rolled)
    sh = cur_s + (prev - cur_s) * mu_ref[...]

    r = sh[:, :A_WIDTH]
    k0 = sh[:, A_WIDTH:2 * A_WIDTH]
    v = sh[:, 2 * A_WIDTH:3 * A_WIDTH]
    o = 3 * A_WIDTH
    w_lo = sh[:, o:o + DECAY_LORA]
    a_lo = sh[:, o + DECAY_LORA:o + DECAY_LORA + AAA_LORA]
    g_lo = sh[:, o + DECAY_LORA + AAA_LORA:SHIFT_WIDTH]

    z = -(dbase_ref[...] + _dot(jnp.tanh(w_lo), wdec_ref[...]))
    softplus = jnp.maximum(z, 0.0) + jnp.log1p(jnp.exp(-jnp.abs(z)))
    log_w = -softplus - 0.5
    lw = -jnp.exp(log_w)
    a = _sigmoid(abase_ref[...] + _dot(a_lo, waaa_ref[...]))
    g = _dot(_sigmoid(g_lo), wgate_ref[...])

    ones64 = _group_ones(A_WIDTH, A_HEAD_DIM)
    kk0 = k0 * kk_ref[...]
    ssq = _group_sum(kk0 * kk0, ones64)
    kk = kk0 / jnp.maximum(jnp.sqrt(ssq), 1e-12)
    k = k0 * (1.0 + (a - 1.0) * ka_ref[...])
    bonus = _group_sum(r * k * rk_ref[...], ones64) * v

    r_o[...] = (r).astype(r_o.dtype)
    lw_o[...] = lw
    k_o[...] = (k).astype(k_o.dtype)
    v_o[...] = (v).astype(v_o.dtype)
    al_o[...] = (-kk).astype(al_o.dtype)
    be_o[...] = (kk * a).astype(be_o.dtype)
    g_o[...] = (g).astype(g_o.dtype)
    bonus_o[...] = (bonus).astype(bonus_o.dtype)

    o = SHIFT_WIDTH
    q_b = proj[:, o:o + B_QK_WIDTH]
    k_b = proj[:, o + B_QK_WIDTH:o + 2 * B_QK_WIDTH]
    v_b = proj[:, o + 2 * B_QK_WIDTH:o + 2 * B_QK_WIDTH + B_WIDTH]
    g_b = proj[:, o + 2 * B_QK_WIDTH + B_WIDTH:]
    lane = lax.broadcasted_iota(jnp.int32, (tm, B_QK_WIDTH), 1)
    first_half = (lane & (B_QK_DIM - 1)) < (B_QK_DIM // 2)
    cos = cos_ref[...]
    sin = sin_ref[...]

    def rot(t):
        swapped = jnp.where(first_half, pltpu.roll(t, B_QK_WIDTH - B_QK_DIM // 2, 1), pltpu.roll(t, B_QK_DIM // 2, 1))
        return t * cos + swapped * sin

    qb_o[...] = (rot(q_b)).astype(qb_o.dtype)
    kb_o[...] = (rot(k_b) * (B_QK_DIM ** -0.5)).astype(kb_o.dtype)
    vb_o[...] = (v_b).astype(vb_o.dtype)
    gb_o[...] = (g_b * _sigmoid(g_b)).astype(gb_o.dtype)


def _proj(x2, x_prev, seq_len, pos0, W):
    n = x2.shape[0]
    tm = PROJ_ROWS
    assert n % tm == 0
    carry_mode = seq_len % tm == 0
    if carry_mode:
        tiles_per_seq = seq_len // tm
        xp = x_prev.reshape(-1, 1, D_MODEL)
        xp_spec = pl.BlockSpec((1, 1, D_MODEL), lambda i: (i // tiles_per_seq, 0, 0))
        tab_rows = seq_len
    else:
        assert tm % seq_len == 0 and seq_len & (seq_len - 1) == 0
        tiles_per_seq = 1
        xp = jnp.repeat(x_prev, seq_len, axis=0)
        xp_spec = pl.BlockSpec((tm, D_MODEL), lambda i: (i, 0))
        tab_rows = tm
    half = B_QK_DIM // 2
    inv = ROPE_BASE ** (-jnp.arange(half, dtype=F32) / half)
    pos = (pos0 + jnp.arange(seq_len, dtype=jnp.int32)).astype(F32)
    ang = pos[:, None] * inv[None, :]
    cos = jnp.tile(jnp.concatenate([jnp.cos(ang), jnp.cos(ang)], -1), (tab_rows // seq_len, B_HEADS))
    sin = jnp.tile(jnp.concatenate([-jnp.sin(ang), jnp.sin(ang)], -1), (tab_rows // seq_len, B_HEADS))
    tab_tiles = tab_rows // tm
    tab_spec = pl.BlockSpec((tm, B_QK_WIDTH), lambda i: (i % tab_tiles, 0))

    def full(a):
        return pl.BlockSpec(a.shape, lambda i: (0,) * a.ndim)

    def rows(width):
        return pl.BlockSpec((tm, width), lambda i: (i, 0))

    params = [W["w_in"], W["mu_shift"], W["w_decay_up"], W["decay_base"], W["w_aaa_up"], W["aaa_base"],
              W["w_gate_up"], W["k_k"], W["k_a"], W["r_k"]]
    widths = [A_WIDTH] * 8 + [B_QK_WIDTH, B_QK_WIDTH, B_WIDTH, B_WIDTH]
    outs = pl.pallas_call(
        functools.partial(_proj_kernel, carry_mode, seq_len, tiles_per_seq),
        grid=(n // tm,),
        in_specs=[rows(D_MODEL), xp_spec, pl.BlockSpec(params[0].shape, lambda i: (0, 0), pipeline_mode=pl.Buffered(1))]
        + [full(a) for a in params[1:]] + [tab_spec, tab_spec],
        out_specs=[rows(w) for w in widths],
        out_shape=[jax.ShapeDtypeStruct((n, w), F32 if i == LOG_DECAY_OUT else BF16) for i, w in enumerate(widths)],
        scratch_shapes=[pltpu.VMEM((V7X_SUBLANES, SHIFT_WIDTH), F32)],
        compiler_params=pltpu.CompilerParams(dimension_semantics=("arbitrary",),
                                             vmem_limit_bytes=V7X_VMEM_LIMIT_BYTES),
        name="proj",
    )(x2, xp, *params, cos, sin)
    return outs


def _mixer_kernel(nb, seqs, clen,
                  r_ref, lw_ref, k_ref, v_ref, al_ref, be_ref, qb_ref, kb_ref, vb_ref, wkv0_ref, ret0_ref,
                  ya_o, ob_o, wkv_o, ret_o, s_scr, r_scr):
    R = seqs * clen
    log2c = int(math.log2(clen))
    c_idx = pl.program_id(1)
    hd = A_HEAD_DIM
    TH = [(t, h) for t in range(nb) for h in range(A_HEADS)]
    TG = [(t, h) for t in range(nb) for h in range(B_HEADS)]

    @pl.when(c_idx == 0)
    def _():
        for t, h in TH:
            blocks = [wkv0_ref[t * seqs + i, h] for i in range(seqs)]
            s_scr[t * A_HEADS + h] = jnp.concatenate(blocks, axis=1) if seqs > 1 else blocks[0]
        for t, h in TG:
            r_scr[t * B_HEADS + h] = ret0_ref[t * seqs:(t + 1) * seqs, h].reshape(seqs * B_QK_DIM, B_V_DIM)

    row = lax.broadcasted_iota(jnp.int32, (R, R), 0)
    col = lax.broadcasted_iota(jnp.int32, (R, R), 1)
    same = (row >> log2c) == (col >> log2c)
    incl = same & (col <= row)
    strict = same & (col < row)
    m_incl = jnp.where(incl, 1.0, 0.0).astype(BF16)
    m_same = jnp.where(same, 1.0, 0.0).astype(BF16)
    eye = jnp.where(row == col, 1.0, 0.0).astype(F32)

    def expand(t):
        if seqs == 1:
            return t
        w = t.shape[1]
        wide = jnp.concatenate([t] * seqs, axis=1)
        rr = lax.broadcasted_iota(jnp.int32, wide.shape, 0) >> log2c
        cc = lax.broadcasted_iota(jnp.int32, wide.shape, 1) // w
        return jnp.where(rr == cc, wide, 0.0)

    def head(x, h):
        return x[:, h * hd:(h + 1) * hd]

    a_bar, r_bar, b_til, k_til, b_dec, k_dec, d_end, vv = [], [], [], [], [], [], [], []
    for t in range(nb):
        lw = lw_ref[t]
        parts = _split3(lw)
        c = sum(jnp.dot(m_incl, p, preferred_element_type=F32) for p in parts)
        cend = sum(jnp.dot(m_same, p, preferred_element_type=F32) for p in parts)
        einv = jnp.exp(-c)
        edec = jnp.exp(cend - c)
        a_bar.append(al_ref[t] * jnp.exp(c - lw))
        r_bar.append(r_ref[t] * jnp.exp(c))
        b_til.append(be_ref[t] * einv)
        k_til.append(k_ref[t] * einv)
        b_dec.append(be_ref[t] * edec)
        k_dec.append(k_ref[t] * edec)
        d_end.append(jnp.exp(cend))
        vv.append(v_ref[t].astype(F32))
    last_row = (lax.broadcasted_iota(jnp.int32, (R, seqs * hd), 0) & (clen - 1)) == clen - 1

    amats = [_dot_nt(jnp.concatenate([head(a_bar[t], h), head(r_bar[t], h)], axis=0),
                     jnp.concatenate([head(b_til[t], h), head(k_til[t], h)], axis=0)) for t, h in TH]

    lgs = [float(np.log1p(-np.exp2(-5.0 - h))) for h in range(B_HEADS)]
    qb = [qb_ref[t].astype(F32) for t in range(nb)]
    kb = [kb_ref[t].astype(F32) for t in range(nb)]
    qs = [qb[t][:, h * B_QK_DIM:(h + 1) * B_QK_DIM] for t, h in TG]
    khs = [kb[t][:, h * B_QK_DIM:(h + 1) * B_QK_DIM] for t, h in TG]
    vbs = [vb_ref[t][:, h * B_V_DIM:(h + 1) * B_V_DIM] for t, h in TG]
    diff = (row - col).astype(F32)
    pos_v = (lax.broadcasted_iota(jnp.int32, (R, B_V_DIM), 0) & (clen - 1)).astype(F32)
    pos_k = (lax.broadcasted_iota(jnp.int32, (R, B_QK_DIM), 0) & (clen - 1)).astype(F32)
    intra = [jnp.where(incl, jnp.exp(lg * diff), 0.0) for lg in lgs]
    cross = [jnp.exp(lg * (pos_v + 1.0)) for lg in lgs]
    kdec = [jnp.exp(lg * (clen - 1.0 - pos_k)) for lg in lgs]
    rstates = [r_scr[t * B_HEADS + h] for t, h in TG]
    scs = [_dot_nt(qs[i], khs[i]) * intra[h] for i, (t, h) in enumerate(TG)]
    qst = [_dot(expand(qs[i]), rstates[i]) * cross[h] for i, (t, h) in enumerate(TG)]

    a_ab = [jnp.where(strict, m[:R, :R], 0.0) for m in amats]
    a_ak = [jnp.where(strict, m[:R, R:], 0.0) for m in amats]
    a_rb = [jnp.where(incl, m[R:, :R], 0.0) for m in amats]
    a_rk = [jnp.where(incl, m[R:, R:], 0.0) for m in amats]
    n = len(TH)
    tinv = [eye + a for a in a_ab]
    vhs = [head(vv[t], h) for t, h in TH]
    av = [_dot(a_ak[i], vhs[i]) for i in range(n)]
    if log2c > 1:
        pw = [_dot(a, a) for a in a_ab]
    for it in range(log2c - 1):
        if it < log2c - 2:
            tp = [_dot(pw[i], jnp.concatenate([tinv[i], pw[i]], axis=1)) for i in range(n)]
            tinv = [tinv[i] + tp[i][:, :R] for i in range(n)]
            pw = [tp[i][:, R:] for i in range(n)]
        else:
            tinv = [tinv[i] + _dot(pw[i], tinv[i]) for i in range(n)]

    os_ = [_dot(scs[i], vbs[i]) + qst[i] for i in range(len(TG))]
    for i, (t, h) in enumerate(TG):
        r_scr[t * B_HEADS + h] = (rstates[i] * float(np.exp(lgs[h] * clen))
                                  + _dot_tn(expand(khs[i] * kdec[h]), vbs[i]))
    for t in range(nb):
        ob_o[t] = jnp.concatenate(os_[t * B_HEADS:(t + 1) * B_HEADS], axis=1)

    wu = [_dot(tinv[i], jnp.concatenate([head(a_bar[t], h), av[i]], axis=1)) for i, (t, h) in enumerate(TH)]
    states = [s_scr[t * A_HEADS + h] for t, h in TH]
    ws = [_dot_nt(jnp.concatenate([expand(wu[i][:, :hd]), expand(head(r_bar[t], h))], axis=0), states[i])
          for i, (t, h) in enumerate(TH)]
    uv = [jnp.concatenate([ws[i][:R] + wu[i][:, hd:], vhs[i]], axis=0) for i in range(n)]
    ys = [ws[i][R:] + _dot(jnp.concatenate([a_rb[i], a_rk[i]], axis=1), uv[i]) for i in range(n)]
    for t in range(nb):
        ya_o[t] = jnp.concatenate(ys[t * A_HEADS:(t + 1) * A_HEADS], axis=1)
    for i, (t, h) in enumerate(TH):
        d_row = jnp.sum(jnp.where(last_row, expand(head(d_end[t], h)), 0.0), axis=0, keepdims=True)
        bk = jnp.concatenate([expand(head(b_dec[t], h)), expand(head(k_dec[t], h))], axis=0)
        s_scr[t * A_HEADS + h] = states[i] * d_row + _dot_tn(uv[i], bk)

    @pl.when(c_idx == pl.num_programs(1) - 1)
    def _():
        for t, h in TH:
            st = s_scr[t * A_HEADS + h]
            for i in range(seqs):
                wkv_o[t * seqs + i, h] = st[:, i * hd:(i + 1) * hd]
        for t, h in TG:
            ret_o[t * seqs:(t + 1) * seqs, h] = r_scr[t * B_HEADS + h].reshape(seqs, B_QK_DIM, B_V_DIM)


def _mixer(ops, wkv0, ret0, n_seq, seq_len):
    n = ops[0].shape[0]
    R = CHUNK_ROWS
    if seq_len >= R:
        assert seq_len % R == 0
        seqs, clen, nchunks = 1, R, seq_len // R
    else:
        assert R % seq_len == 0 and seq_len & (seq_len - 1) == 0 and n_seq % (R // seq_len) == 0
        seqs, clen, nchunks = R // seq_len, seq_len, 1
    ntiles = n_seq // seqs
    nb = MIXER_TILES if seqs == 1 else MIXER_TILES_PACKED
    assert ntiles % nb == 0
    ops3 = [a.reshape(ntiles, nchunks * R, a.shape[1]) for a in ops]

    def rows(width):
        return pl.BlockSpec((nb, R, width), lambda i, c: (i, c, 0))

    wkv_spec = pl.BlockSpec((nb * seqs, A_HEADS, A_HEAD_DIM, A_HEAD_DIM), lambda i, c: (i, 0, 0, 0))
    ret_spec = pl.BlockSpec((nb * seqs, B_HEADS, B_QK_DIM, B_V_DIM), lambda i, c: (i, 0, 0, 0))
    ya, ob, wkv1, ret1 = pl.pallas_call(
        functools.partial(_mixer_kernel, nb, seqs, clen),
        grid=(ntiles // nb, nchunks),
        in_specs=[rows(A_WIDTH)] * 6 + [rows(B_QK_WIDTH), rows(B_QK_WIDTH), rows(B_WIDTH), wkv_spec, ret_spec],
        out_specs=[rows(A_WIDTH), rows(B_WIDTH), wkv_spec, ret_spec],
        out_shape=[jax.ShapeDtypeStruct((ntiles, nchunks * R, A_WIDTH), F32),
                   jax.ShapeDtypeStruct((ntiles, nchunks * R, B_WIDTH), F32),
                   jax.ShapeDtypeStruct(wkv0.shape, F32), jax.ShapeDtypeStruct(ret0.shape, F32)],
        scratch_shapes=[pltpu.VMEM((nb * A_HEADS, A_HEAD_DIM, seqs * A_HEAD_DIM), F32),
                        pltpu.VMEM((nb * B_HEADS, seqs * B_QK_DIM, B_V_DIM), F32)],
        compiler_params=pltpu.CompilerParams(dimension_semantics=("parallel", "arbitrary"),
                                             vmem_limit_bytes=V7X_VMEM_LIMIT_BYTES),
        name="mixer",
    )(*ops3, wkv0, ret0)
    return ya.reshape(n, A_WIDTH), ob.reshape(n, B_WIDTH), wkv1, ret1


def _layer_norm(z, g, b):
    mu = jnp.mean(z, axis=-1, keepdims=True)
    d = z - mu
    var = jnp.mean(d * d, axis=-1, keepdims=True)
    return d * lax.rsqrt(var + LN_EPS) * g + b


def _post_kernel(ya_ref, ob_ref, bonus_ref, g_ref, gb_ref, x_ref, lnxg_ref, lnxb_ref, rgg_ref, rgb_ref,
                 wout_ref, ln1g_ref, ln1b_ref, wr_ref, br_ref, h_o, gate_o, cnt_o):
    tm = x_ref.shape[0]

    def head_norm(t, group, eps, gg, bb):
        ones = _group_ones(t.shape[1], group)
        mu = _group_sum(t, ones) * (1.0 / group)
        d = t - mu
        var = _group_sum(d * d, ones) * (1.0 / group)
        return d * lax.rsqrt(var + eps) * gg + bb

    y_a = (head_norm(ya_ref[...], A_HEAD_DIM, GN_EPS_RWKV, lnxg_ref[...], lnxb_ref[...]) + bonus_ref[...]) * g_ref[...]
    y_b = head_norm(ob_ref[...], B_V_DIM, GN_EPS, rgg_ref[...], rgb_ref[...]) * gb_ref[...]
    y = jnp.concatenate([y_a, y_b], axis=1)
    mix = _dot(y, wout_ref[...])
    h = _layer_norm(DEEPNORM_ALPHA * x_ref[...] + mix, ln1g_ref[...], ln1b_ref[...])
    h_o[...] = h

    h_hi, h_lo = _split2(h)
    w_hi, w_lo = _split2(wr_ref[...])
    logits = (jnp.dot(h_hi, w_hi, preferred_element_type=F32) + jnp.dot(h_hi, w_lo, preferred_element_type=F32)
              + jnp.dot(h_lo, w_hi, preferred_element_type=F32)) + br_ref[...]
    lane = lax.broadcasted_iota(jnp.int32, (tm, ROUTER_LANES), 1)
    neg = -jnp.inf
    big = ROUTER_LANES
    cl = jnp.where(lane < N_GROUPS, logits, neg)
    cmax = jnp.max(cl, axis=-1, keepdims=True)
    grp = jnp.min(jnp.where(cl == cmax, lane, big), axis=-1, keepdims=True)
    gprob = 1.0 / jnp.sum(jnp.exp(cl - cmax), axis=-1, keepdims=True)
    lo_lane = FINE_LANE0 + grp * EXPERTS_PER_GROUP
    fv = jnp.where((lane >= lo_lane) & (lane < lo_lane + EXPERTS_PER_GROUP), logits, neg)
    m1 = jnp.max(fv, axis=-1, keepdims=True)
    i1 = jnp.min(jnp.where(fv == m1, lane, big), axis=-1, keepdims=True)
    fv2 = jnp.where(lane == i1, neg, fv)
    m2 = jnp.max(fv2, axis=-1, keepdims=True)
    i2 = jnp.min(jnp.where(fv2 == m2, lane, big), axis=-1, keepdims=True)
    e2 = jnp.exp(m2 - m1)
    w1 = gprob / (1.0 + e2)
    w2 = gprob * e2 / (1.0 + e2)
    gate_o[...] = (jnp.where(lane == i1, w1, 0.0) + jnp.where(lane == i2, w2, 0.0)
                   + jnp.where(lane == GROUP_LANE, grp.astype(F32), 0.0))
    onehot = jnp.where((lane == grp) & (lane < N_GROUPS), 1.0, 0.0)
    cnt_o[0] = jnp.broadcast_to(jnp.sum(onehot, axis=0, keepdims=True), (V7X_SUBLANES, ROUTER_LANES)).astype(jnp.int32)


def _post(ya, ob, bonus, g, gb, x2, W):
    n = x2.shape[0]
    tm = POST_ROWS
    assert n % tm == 0

    def full(a):
        return pl.BlockSpec(a.shape, lambda i: (0,) * a.ndim)

    def rows(width):
        return pl.BlockSpec((tm, width), lambda i: (i, 0))

    params = [W["lnx_g"], W["lnx_b"], W["ret_gn_g"], W["ret_gn_b"], W["w_out"], W["ln1_g"], W["ln1_b"],
              W["w_router"], W["b_router"]]
    return pl.pallas_call(
        _post_kernel,
        grid=(n // tm,),
        in_specs=[rows(A_WIDTH), rows(B_WIDTH), rows(A_WIDTH), rows(A_WIDTH), rows(B_WIDTH), rows(D_MODEL)]
        + [full(a) for a in params],
        out_specs=[rows(D_MODEL), rows(ROUTER_LANES),
                   pl.BlockSpec((1, V7X_SUBLANES, ROUTER_LANES), lambda i: (i, 0, 0))],
        out_shape=[jax.ShapeDtypeStruct((n, D_MODEL), F32), jax.ShapeDtypeStruct((n, ROUTER_LANES), F32),
                   jax.ShapeDtypeStruct((n // tm, V7X_SUBLANES, ROUTER_LANES), jnp.int32)],
        compiler_params=pltpu.CompilerParams(dimension_semantics=("parallel",),
                                             vmem_limit_bytes=V7X_VMEM_LIMIT_BYTES),
        name="post",
    )(ya, ob, bonus, g, gb, x2, *params)


def _sort_positions(gate, lofs):
    tm = gate.shape[0]
    lane = lax.broadcasted_iota(jnp.int32, (tm, ROUTER_LANES), 1)
    grp = gate[:, GROUP_LANE:GROUP_LANE + 1].astype(jnp.int32)
    onehot = jnp.where((lane == grp) & (lane < N_GROUPS), 1.0, 0.0)
    r = lax.broadcasted_iota(jnp.int32, (tm, tm), 0)
    c = lax.broadcasted_iota(jnp.int32, (tm, tm), 1)
    earlier = jnp.where(c < r, 1.0, 0.0).astype(BF16)
    prefix = jnp.dot(earlier, onehot.astype(BF16), preferred_element_type=F32)
    base = jnp.zeros((tm, ROUTER_LANES), F32)
    for g in range(N_GROUPS):
        base = jnp.where(lane == g, lofs[g].astype(F32), base)
    return jnp.sum(onehot * (base + prefix), axis=1, keepdims=True).astype(jnp.int32)


def _run_copies(plan_ref, i, src_of, dst_of, sem):
    def each(action):
        for g in range(N_GROUPS):
            goff = plan_ref[i, g]
            pieces = plan_ref[i, N_GROUPS + g] // SORT_ALIGN
            lofs = plan_ref[i, 2 * N_GROUPS + g]

            def body(k, carry):
                lo = pl.multiple_of(lofs + k * SORT_ALIGN, SORT_ALIGN)
                go = pl.multiple_of(goff + k * SORT_ALIGN, SORT_ALIGN)
                cp = pltpu.make_async_copy(src_of(lo, go), dst_of(lo, go), sem)
                cp.start() if action == "start" else cp.wait()
                return carry

            lax.fori_loop(0, pieces, body, 0)

    each("start")
    each("wait")


def _dispatch_kernel(plan_ref, h_ref, gate_ref, hs_in, gs_in, hs_o, gs_o, h_loc, g_loc, sem):
    del hs_in, gs_in
    i = pl.program_id(0)
    tm = h_ref.shape[0]
    lofs = [plan_ref[i, 2 * N_GROUPS + g] for g in range(N_GROUPS)]
    pos = _sort_positions(gate_ref[...], lofs)
    onehot_t = jnp.where(lax.broadcasted_iota(jnp.int32, (tm, SORT_LOCAL), 1) == pos, 1.0, 0.0).astype(BF16)
    idx = lax.broadcasted_iota(jnp.int32, (V7X_SUBLANES, SORT_LOCAL), 1)
    pos_row = (_dot_nt((idx // V7X_LANES).astype(F32), onehot_t) * float(V7X_LANES)
               + _dot_nt((idx % V7X_LANES).astype(F32), onehot_t))[0:1].astype(jnp.int32)
    perm = jnp.where(lax.broadcasted_iota(jnp.int32, (SORT_LOCAL, tm), 0) == pos_row, 1.0, 0.0).astype(BF16)
    h_loc[...] = jnp.dot(perm, h_ref[...].astype(BF16), preferred_element_type=F32).astype(BF16)
    g_loc[...] = sum(jnp.dot(perm, p, preferred_element_type=F32) for p in _split3(gate_ref[...]))

    def copies(src, dst):
        _run_copies(plan_ref, i, lambda lo, go: src.at[pl.ds(lo, SORT_ALIGN)],
                    lambda lo, go: dst.at[pl.ds(go, SORT_ALIGN)], sem)

    copies(h_loc, hs_o)
    copies(g_loc, gs_o)


def _experts_kernel(tile_group_ref, n_valid_ref, hs_ref, gs_ref, w1_ref, w3_ref, w2_ref, ys_o):
    j = pl.program_id(0)

    @pl.when(j < n_valid_ref[0])
    def _():
        x = hs_ref[...]
        gs = gs_ref[...]
        lane = lax.broadcasted_iota(jnp.int32, gs.shape, 1)
        g = tile_group_ref[j]
        acc = jnp.zeros(ys_o.shape, F32)
        for e in range(EXPERTS_PER_GROUP):
            ge = jnp.sum(jnp.where(lane == FINE_LANE0 + g * EXPERTS_PER_GROUP + e, gs, 0.0), axis=-1, keepdims=True)
            a = jnp.dot(x, w1_ref[e], preferred_element_type=F32)
            b = jnp.dot(x, w3_ref[e], preferred_element_type=F32)
            hid = (a * _sigmoid(a)) * b * ge
            acc = acc + jnp.dot(hid.astype(BF16), w2_ref[e], preferred_element_type=F32)
        ys_o[...] = acc.astype(ys_o.dtype)

    @pl.when(j >= n_valid_ref[0])
    def _():
        ys_o[...] = jnp.zeros_like(ys_o)


def _combine_kernel(plan_ref, h_ref, gate_ref, p_ref, ys_ref, ln2g_ref, ln2b_ref, wple_ref, wpg_ref, pleg_ref,
                    o_ref, y_loc, sem):
    i = pl.program_id(0)
    tm = h_ref.shape[0]
    y_loc[...] = jnp.zeros_like(y_loc)
    lofs = [plan_ref[i, 2 * N_GROUPS + g] for g in range(N_GROUPS)]
    _run_copies(plan_ref, i, lambda lo, go: ys_ref.at[pl.ds(go, SORT_ALIGN)],
                lambda lo, go: y_loc.at[pl.ds(lo, SORT_ALIGN)], sem)
    pos = _sort_positions(gate_ref[...], lofs)
    onehot_t = jnp.where(lax.broadcasted_iota(jnp.int32, (tm, SORT_LOCAL), 1) == pos, 1.0, 0.0).astype(BF16)
    ffn = jnp.dot(onehot_t, y_loc[...], preferred_element_type=F32)
    h2 = _layer_norm(DEEPNORM_ALPHA * h_ref[...] + ffn, ln2g_ref[...], ln2b_ref[...])
    ple = _dot(p_ref[...], wple_ref[...]) * _sigmoid(_dot(h2, wpg_ref[...]))
    ms = jnp.mean(ple * ple, axis=-1, keepdims=True)
    o_ref[...] = h2 + ple * lax.rsqrt(ms + LN_EPS) * pleg_ref[...]


def _ffn(h, gate, counts, p2, W):
    n = h.shape[0]
    tm = SORT_ROWS
    assert n % tm == 0 and tm == POST_ROWS and SORT_LOCAL >= tm + N_GROUPS * SORT_ALIGN
    ntiles = n // tm
    cnt = counts[:, 0, :N_GROUPS]
    run = (cnt + SORT_ALIGN - 1) // SORT_ALIGN * SORT_ALIGN
    lofs = jnp.cumsum(run, axis=1) - run
    seg = (jnp.sum(run, axis=0) + EXPERT_ROWS - 1) // EXPERT_ROWS * EXPERT_ROWS
    gbase = jnp.cumsum(seg) - seg
    goff = gbase[None, :] + jnp.cumsum(run, axis=0) - run
    plan = jnp.concatenate([goff, run, lofs], axis=1).astype(jnp.int32)
    max_tiles = (n + ntiles * N_GROUPS * (SORT_ALIGN - 1)) // EXPERT_ROWS + N_GROUPS
    cap = max_tiles * EXPERT_ROWS
    n_valid = (jnp.sum(seg) // EXPERT_ROWS).astype(jnp.int32).reshape(1)
    tile_start = jnp.arange(max_tiles, dtype=jnp.int32) * EXPERT_ROWS
    tile_group = jnp.clip(jnp.sum(tile_start[:, None] >= (gbase + seg)[None, :], axis=1), 0, N_GROUPS - 1).astype(jnp.int32)

    cparams = dict(vmem_limit_bytes=V7X_VMEM_LIMIT_BYTES)
    any_spec = pl.BlockSpec(memory_space=pl.ANY)
    hs, gs = pl.pallas_call(
        _dispatch_kernel,
        grid_spec=pltpu.PrefetchScalarGridSpec(
            num_scalar_prefetch=1, grid=(ntiles,),
            in_specs=[pl.BlockSpec((tm, D_MODEL), lambda i, plan: (i, 0)),
                      pl.BlockSpec((tm, ROUTER_LANES), lambda i, plan: (i, 0)), any_spec, any_spec],
            out_specs=[any_spec, any_spec],
            scratch_shapes=[pltpu.VMEM((SORT_LOCAL, D_MODEL), BF16), pltpu.VMEM((SORT_LOCAL, ROUTER_LANES), F32),
                            pltpu.SemaphoreType.DMA(())]),
        out_shape=[jax.ShapeDtypeStruct((cap, D_MODEL), BF16), jax.ShapeDtypeStruct((cap, ROUTER_LANES), F32)],
        input_output_aliases={3: 0, 4: 1},
        compiler_params=pltpu.CompilerParams(dimension_semantics=("arbitrary",), **cparams),
        name="dispatch",
    )(plan, h, gate, jnp.zeros((cap, D_MODEL), BF16), jnp.zeros((cap, ROUTER_LANES), F32))

    def tile_rows(width):
        return pl.BlockSpec((EXPERT_ROWS, width), lambda j, tg, nv: (jnp.minimum(j, nv[0] - 1), 0))

    def group_w(shape):
        return pl.BlockSpec((EXPERTS_PER_GROUP,) + shape, lambda j, tg, nv: (tg[jnp.minimum(j, nv[0] - 1)], 0, 0))

    ys = pl.pallas_call(
        _experts_kernel,
        grid_spec=pltpu.PrefetchScalarGridSpec(
            num_scalar_prefetch=2, grid=(max_tiles,),
            in_specs=[tile_rows(D_MODEL), tile_rows(ROUTER_LANES), group_w((D_MODEL, D_EXPERT)),
                      group_w((D_MODEL, D_EXPERT)), group_w((D_EXPERT, D_MODEL))],
            out_specs=pl.BlockSpec((EXPERT_ROWS, D_MODEL), lambda j, tg, nv: (j, 0))),
        out_shape=jax.ShapeDtypeStruct((cap, D_MODEL), BF16),
        compiler_params=pltpu.CompilerParams(dimension_semantics=("arbitrary",), **cparams),
        name="experts",
    )(tile_group, n_valid, hs, gs, W["expert_w1"], W["expert_w3"], W["expert_w2"])

    def full(a):
        return pl.BlockSpec(a.shape, lambda i, plan: (0,) * a.ndim)

    params = [W["ln2_g"], W["ln2_b"], W["w_ple"], W["w_ple_gate"], W["ple_norm_g"]]
    return pl.pallas_call(
        _combine_kernel,
        grid_spec=pltpu.PrefetchScalarGridSpec(
            num_scalar_prefetch=1, grid=(ntiles,),
            in_specs=[pl.BlockSpec((tm, D_MODEL), lambda i, plan: (i, 0)),
                      pl.BlockSpec((tm, ROUTER_LANES), lambda i, plan: (i, 0)),
                      pl.BlockSpec((tm, D_PLE), lambda i, plan: (i, 0)), any_spec] + [full(a) for a in params],
            out_specs=pl.BlockSpec((tm, D_MODEL), lambda i, plan: (i, 0)),
            scratch_shapes=[pltpu.VMEM((SORT_LOCAL, D_MODEL), BF16), pltpu.SemaphoreType.DMA(())]),
        out_shape=jax.ShapeDtypeStruct((n, D_MODEL), F32),
        compiler_params=pltpu.CompilerParams(dimension_semantics=("arbitrary",), **cparams),
        name="combine",
    )(plan, h, gate, p2, ys, *params)


def _layer(x, p, x_prev, wkv0, ret0, pos0, W):
    n_seq, seq_len, _ = x.shape
    n = n_seq * seq_len
    x2 = x.reshape(n, D_MODEL)
    r, lw, k, v, al, be, g, bonus, qb, kb, vb, gb = _proj(x2, x_prev, seq_len, pos0, W)
    ya, ob, wkv1, ret1 = _mixer((r, lw, k, v, al, be, qb, kb, vb), wkv0, ret0, n_seq, seq_len)
    h, gate, counts = _post(ya, ob, bonus, g, gb, x2, W)
    out = _ffn(h, gate, counts, p.reshape(n, D_PLE), W)
    return out.reshape(n_seq, seq_len, D_MODEL), x[:, -1], wkv1, ret1


def _prep_weights(i, w_in, mu_shift, w_decay_up, decay_base, w_aaa_up, aaa_base, w_gate_up, k_k, k_a, r_k,
                  lnx_g, lnx_b, ret_gn_g, ret_gn_b, w_out, ln1_g, ln1_b,
                  router_coarse_w, router_coarse_b, router_fine_w, router_fine_b,
                  expert_w1, expert_w3, expert_w2, ln2_g, ln2_b, w_ple, w_ple_gate, ple_norm_g):
    def row(a):
        return a[i].reshape(1, -1).astype(F32)

    pad = ROUTER_LANES - N_GROUPS - N_EXPERTS
    w_router = jnp.concatenate([router_coarse_w[i], router_fine_w[i], jnp.zeros((D_MODEL, pad), F32)], axis=1)
    b_router = jnp.concatenate([router_coarse_b[i], router_fine_b[i], jnp.zeros((pad,), F32)]).reshape(1, -1)
    return {
        "w_in": w_in[i].astype(BF16), "mu_shift": row(mu_shift), "w_decay_up": w_decay_up[i].astype(BF16),
        "decay_base": row(decay_base), "w_aaa_up": w_aaa_up[i].astype(BF16), "aaa_base": row(aaa_base),
        "w_gate_up": w_gate_up[i].astype(BF16), "k_k": row(k_k), "k_a": row(k_a), "r_k": row(r_k),
        "lnx_g": row(lnx_g), "lnx_b": row(lnx_b), "ret_gn_g": row(ret_gn_g), "ret_gn_b": row(ret_gn_b),
        "w_out": w_out[i].astype(BF16), "ln1_g": row(ln1_g), "ln1_b": row(ln1_b),
        "w_router": w_router, "b_router": b_router,
        "expert_w1": expert_w1[i].astype(BF16), "expert_w3": expert_w3[i].astype(BF16),
        "expert_w2": expert_w2[i].astype(BF16), "ln2_g": row(ln2_g), "ln2_b": row(ln2_b),
        "w_ple": w_ple[i].astype(BF16), "w_ple_gate": w_ple_gate[i].astype(BF16), "ple_norm_g": row(ple_norm_g),
    }


def kernel(x_prompt, x_sample, p_prompt, p_sample, state_wkv, state_shift, state_ret, w_in, mu_shift, w_decay_up, decay_base, w_aaa_up, aaa_base, w_gate_up, k_k, k_a, r_k, lnx_g, lnx_b, ret_gn_g, ret_gn_b, w_out, ln1_g, ln1_b, router_coarse_w, router_coarse_b, router_fine_w, router_fine_b, expert_w1, expert_w3, expert_w2, ln2_g, ln2_b, w_ple, w_ple_gate, ple_norm_g):
    yp, ys = x_prompt, x_sample
    nb = x_prompt.shape[0]
    depth = w_in.shape[0]
    wkv_p, shift_p, ret_p, wkv_s, shift_s, ret_s = [], [], [], [], [], []
    for i in range(depth):
        W = _prep_weights(i, w_in, mu_shift, w_decay_up, decay_base, w_aaa_up, aaa_base, w_gate_up, k_k, k_a, r_k,
                          lnx_g, lnx_b, ret_gn_g, ret_gn_b, w_out, ln1_g, ln1_b,
                          router_coarse_w, router_coarse_b, router_fine_w, router_fine_b,
                          expert_w1, expert_w3, expert_w2, ln2_g, ln2_b, w_ple, w_ple_gate, ple_norm_g)
        yp, sp, wp, rp = _layer(yp, p_prompt[i], jnp.zeros((nb, D_MODEL), F32),
                                jnp.zeros((nb, A_HEADS, A_HEAD_DIM, A_HEAD_DIM), F32),
                                jnp.zeros((nb, B_HEADS, B_QK_DIM, B_V_DIM), F32), 0, W)
        ys, ss, wsm, rsm = _layer(ys, p_sample[i], state_shift[i], state_wkv[i], state_ret[i], PAST_LEN, W)
        wkv_p.append(wp); shift_p.append(sp); ret_p.append(rp)
        wkv_s.append(wsm); shift_s.append(ss); ret_s.append(rsm)
    return (yp, ys, jnp.stack(wkv_p, 0), jnp.stack(shift_p, 0), jnp.stack(ret_p, 0),
            jnp.stack(wkv_s, 0), jnp.stack(shift_s, 0), jnp.stack(ret_s, 0))
```

```python
import functools
import math

import numpy as np
import jax
import jax.numpy as jnp
from jax import lax
from jax.experimental import pallas as pl
from jax.experimental.pallas import tpu as pltpu

F32 = jnp.float32
BF16 = jnp.bfloat16

D_MODEL = 1024
D_PLE = 256
A_HEADS = 8
A_HEAD_DIM = 64
A_WIDTH = A_HEADS * A_HEAD_DIM
DECAY_LORA = 64
AAA_LORA = 64
GATE_LORA = 128
GN_EPS_RWKV = 64e-5
B_HEADS = 4
B_QK_DIM = 64
B_V_DIM = 128
B_QK_WIDTH = B_HEADS * B_QK_DIM
B_WIDTH = B_HEADS * B_V_DIM
ROPE_BASE = 10000.0
GN_EPS = 1e-5
SHIFT_WIDTH = 3 * A_WIDTH + DECAY_LORA + AAA_LORA + GATE_LORA
IN_WIDTH = SHIFT_WIDTH + 2 * B_QK_WIDTH + 2 * B_WIDTH
N_GROUPS = 4
EXPERTS_PER_GROUP = 4
N_EXPERTS = N_GROUPS * EXPERTS_PER_GROUP
D_EXPERT = 256
DEPTH = 1
PAST_LEN = 16384
DEEPNORM_ALPHA = (2 * DEPTH) ** 0.25
LN_EPS = 1e-5

V7X_LANES = 128
V7X_SUBLANES = 8
V7X_VMEM_LIMIT_BYTES = 56 * 1024 * 1024

PROJ_ROWS = 512
LOG_DECAY_OUT = 1
CHUNK_ROWS = 64
MIXER_TILES = 4
MIXER_TILES_PACKED = 2
POST_ROWS = 512
SORT_ROWS = 512
SORT_ALIGN = 16
SORT_LOCAL = 640
EXPERT_ROWS = 1024
EXPERT_ROWS_SHORT = 256
GROUP_LANE = 0
ROUTER_LANES = V7X_LANES
FINE_LANE0 = N_GROUPS


def _dot(a, b):
    return jnp.dot(a.astype(BF16), b.astype(BF16), preferred_element_type=F32)


def _dot_nt(a, b):
    return lax.dot_general(a.astype(BF16), b.astype(BF16), (((1,), (1,)), ((), ())), preferred_element_type=F32)


def _dot_tn(a, b):
    return lax.dot_general(a.astype(BF16), b.astype(BF16), (((0,), (0,)), ((), ())), preferred_element_type=F32)


def _split2(x):
    hi = x.astype(BF16)
    lo = (x - hi.astype(F32)).astype(BF16)
    return hi, lo


def _split3(x):
    hi = x.astype(BF16)
    r1 = x - hi.astype(F32)
    mid = r1.astype(BF16)
    lo = (r1 - mid.astype(F32)).astype(BF16)
    return hi, mid, lo


def _sigmoid(x):
    return 1.0 / (1.0 + jnp.exp(-x))


def _group_ones(width, group):
    r = lax.broadcasted_iota(jnp.int32, (width, width), 0) // group
    c = lax.broadcasted_iota(jnp.int32, (width, width), 1) // group
    return jnp.where(r == c, 1.0, 0.0).astype(BF16)


def _group_sum(x, ones):
    return jnp.dot(x.astype(BF16), ones, preferred_element_type=F32)


def _proj_kernel(carry_mode, seq_len, tiles_per_seq,
                 x_ref, xp_ref, w_ref, mu_ref, wdec_ref, dbase_ref, waaa_ref, abase_ref, wgate_ref,
                 kk_ref, ka_ref, rk_ref, cos_ref, sin_ref,
                 r_o, lw_o, k_o, v_o, al_o, be_o, g_o, bonus_o, qb_o, kb_o, vb_o, gb_o,
                 carry_scr):
    tm = x_ref.shape[0]
    row = lax.broadcasted_iota(jnp.int32, (tm, SHIFT_WIDTH), 0)
    if carry_mode:
        xp = jnp.broadcast_to(xp_ref[0], (V7X_SUBLANES, D_MODEL))
        xb = jnp.concatenate([x_ref[...], xp], axis=0).astype(BF16)
        proj_all = jnp.dot(xb, w_ref[...], preferred_element_type=F32)
        proj = proj_all[:tm]
        cur_s = proj[:, :SHIFT_WIDTH]
        rolled = pltpu.roll(cur_s, 1, 0)
        j = pl.program_id(0) % tiles_per_seq

        @pl.when(pl.program_id(0) == 0)
        def _():
            carry_scr[...] = jnp.zeros_like(carry_scr)

        first = jnp.where(j == 0, proj_all[tm + V7X_SUBLANES - 1:, :SHIFT_WIDTH],
                          carry_scr[V7X_SUBLANES - 1:V7X_SUBLANES, :])
        prev = jnp.where(row == 0, first, rolled)
        carry_scr[...] = cur_s[tm - V7X_SUBLANES:, :]
    else:
        xb = x_ref[...].astype(BF16)
        proj = jnp.dot(xb, w_ref[...], preferred_element_type=F32)
        cur_s = proj[:, :SHIFT_WIDTH]
        rolled = pltpu.roll(cur_s, 1, 0)
        first = jnp.dot(xp_ref[...].astype(BF16), w_ref[:, :SHIFT_WIDTH], preferred_element_type=F32)
        prev = jnp.where((row & (seq_len - 1)) == 0, first, rolled)
    sh = cur_s + (prev - cur_s) * mu_ref[...]

    r = sh[:, :A_WIDTH]
    k0 = sh[:, A_WIDTH:2 * A_WIDTH]
    v = sh[:, 2 * A_WIDTH:3 * A_WIDTH]
    o = 3 * A_WIDTH
    w_lo = sh[:, o:o + DECAY_LORA]
    a_lo = sh[:, o + DECAY_LORA:o + DECAY_LORA + AAA_LORA]
    g_lo = sh[:, o + DECAY_LORA + AAA_LORA:SHIFT_WIDTH]

    z = -(dbase_ref[...] + _dot(jnp.tanh(w_lo), wdec_ref[...]))
    softplus = jnp.maximum(z, 0.0) + jnp.log1p(jnp.exp(-jnp.abs(z)))
    log_w = -softplus - 0.5
    lw = -jnp.exp(log_w)
    a = _sigmoid(abase_ref[...] + _dot(a_lo, waaa_ref[...]))
    g = _dot(_sigmoid(g_lo), wgate_ref[...])

    ones64 = _group_ones(A_WIDTH, A_HEAD_DIM)
    kk0 = k0 * kk_ref[...]
    ssq = _group_sum(kk0 * kk0, ones64)
    kk = kk0 / jnp.maximum(jnp.sqrt(ssq), 1e-12)
    k = k0 * (1.0 + (a - 1.0) * ka_ref[...])
    bonus = _group_sum(r * k * rk_ref[...], ones64) * v

    r_o[...] = (r).astype(r_o.dtype)
    lw_o[...] = lw
    k_o[...] = (k).astype(k_o.dtype)
    v_o[...] = (v).astype(v_o.dtype)
    al_o[...] = (-kk).astype(al_o.dtype)
    be_o[...] = (kk * a).astype(be_o.dtype)
    g_o[...] = (g).astype(g_o.dtype)
    bonus_o[...] = (bonus).astype(bonus_o.dtype)

    o = SHIFT_WIDTH
    q_b = proj[:, o:o + B_QK_WIDTH]
    k_b = proj[:, o + B_QK_WIDTH:o + 2 * B_QK_WIDTH]
    v_b = proj[:, o + 2 * B_QK_WIDTH:o + 2 * B_QK_WIDTH + B_WIDTH]
    g_b = proj[:, o + 2 * B_QK_WIDTH + B_WIDTH:]
    lane = lax.broadcasted_iota(jnp.int32, (tm, B_QK_WIDTH), 1)
    first_half = (lane & (B_QK_DIM - 1)) < (B_QK_DIM // 2)
    cos = cos_ref[...]
    sin = sin_ref[...]

    def rot(t):
        swapped = jnp.where(first_half, pltpu.roll(t, B_QK_WIDTH - B_QK_DIM // 2, 1), pltpu.roll(t, B_QK_DIM // 2, 1))
        return t * cos + swapped * sin

    qb_o[...] = (rot(q_b)).astype(qb_o.dtype)
    kb_o[...] = (rot(k_b) * (B_QK_DIM ** -0.5)).astype(kb_o.dtype)
    vb_o[...] = (v_b).astype(vb_o.dtype)
    gb_o[...] = (g_b * _sigmoid(g_b)).astype(gb_o.dtype)


def _proj(x2, x_prev, seq_len, pos0, W):
    n = x2.shape[0]
    tm = PROJ_ROWS
    assert n % tm == 0
    carry_mode = seq_len % tm == 0
    if carry_mode:
        tiles_per_seq = seq_len // tm
        xp = x_prev.reshape(-1, 1, D_MODEL)
        xp_spec = pl.BlockSpec((1, 1, D_MODEL), lambda i: (i // tiles_per_seq, 0, 0))
        tab_rows = seq_len
    else:
        assert tm % seq_len == 0 and seq_len & (seq_len - 1) == 0
        tiles_per_seq = 1
        xp = jnp.repeat(x_prev, seq_len, axis=0)
        xp_spec = pl.BlockSpec((tm, D_MODEL), lambda i: (i, 0))
        tab_rows = tm
    half = B_QK_DIM // 2
    inv = ROPE_BASE ** (-jnp.arange(half, dtype=F32) / half)
    pos = (pos0 + jnp.arange(seq_len, dtype=jnp.int32)).astype(F32)
    ang = pos[:, None] * inv[None, :]
    cos = jnp.tile(jnp.concatenate([jnp.cos(ang), jnp.cos(ang)], -1), (tab_rows // seq_len, B_HEADS))
    sin = jnp.tile(jnp.concatenate([-jnp.sin(ang), jnp.sin(ang)], -1), (tab_rows // seq_len, B_HEADS))
    tab_tiles = tab_rows // tm
    tab_spec = pl.BlockSpec((tm, B_QK_WIDTH), lambda i: (i % tab_tiles, 0))

    def full(a):
        return pl.BlockSpec(a.shape, lambda i: (0,) * a.ndim)

    def rows(width):
        return pl.BlockSpec((tm, width), lambda i: (i, 0))

    params = [W["w_in"], W["mu_shift"], W["w_decay_up"], W["decay_base"], W["w_aaa_up"], W["aaa_base"],
              W["w_gate_up"], W["k_k"], W["k_a"], W["r_k"]]
    widths = [A_WIDTH] * 8 + [B_QK_WIDTH, B_QK_WIDTH, B_WIDTH, B_WIDTH]
    outs = pl.pallas_call(
        functools.partial(_proj_kernel, carry_mode, seq_len, tiles_per_seq),
        grid=(n // tm,),
        in_specs=[rows(D_MODEL), xp_spec, pl.BlockSpec(params[0].shape, lambda i: (0, 0), pipeline_mode=pl.Buffered(1))]
        + [full(a) for a in params[1:]] + [tab_spec, tab_spec],
        out_specs=[rows(w) for w in widths],
        out_shape=[jax.ShapeDtypeStruct((n, w), F32 if i == LOG_DECAY_OUT else BF16) for i, w in enumerate(widths)],
        scratch_shapes=[pltpu.VMEM((V7X_SUBLANES, SHIFT_WIDTH), F32)],
        compiler_params=pltpu.CompilerParams(dimension_semantics=("arbitrary",),
                                             vmem_limit_bytes=V7X_VMEM_LIMIT_BYTES),
        name="proj",
    )(x2, xp, *params, cos, sin)
    return outs


def _mixer_kernel(nb, seqs, clen,
                  r_ref, lw_ref, k_ref, v_ref, al_ref, be_ref, qb_ref, kb_ref, vb_ref, wkv0_ref, ret0_ref,
                  ya_o, ob_o, wkv_o, ret_o, s_scr, r_scr):
    R = seqs * clen
    log2c = int(math.log2(clen))
    c_idx = pl.program_id(1)
    hd = A_HEAD_DIM
    TH = [(t, h) for t in range(nb) for h in range(A_HEADS)]
    TG = [(t, h) for t in range(nb) for h in range(B_HEADS)]

    @pl.when(c_idx == 0)
    def _():
        for t, h in TH:
            blocks = [wkv0_ref[t * seqs + i, h] for i in range(seqs)]
            s_scr[t * A_HEADS + h] = jnp.concatenate(blocks, axis=1) if seqs > 1 else blocks[0]
        for t, h in TG:
            r_scr[t * B_HEADS + h] = ret0_ref[t * seqs:(t + 1) * seqs, h].reshape(seqs * B_QK_DIM, B_V_DIM)

    row = lax.broadcasted_iota(jnp.int32, (R, R), 0)
    col = lax.broadcasted_iota(jnp.int32, (R, R), 1)
    same = (row >> log2c) == (col >> log2c)
    incl = same & (col <= row)
    strict = same & (col < row)
    m_incl = jnp.where(incl, 1.0, 0.0).astype(BF16)
    m_same = jnp.where(same, 1.0, 0.0).astype(BF16)
    eye = jnp.where(row == col, 1.0, 0.0).astype(F32)

    def expand(t):
        if seqs == 1:
            return t
        w = t.shape[1]
        wide = jnp.concatenate([t] * seqs, axis=1)
        rr = lax.broadcasted_iota(jnp.int32, wide.shape, 0) >> log2c
        cc = lax.broadcasted_iota(jnp.int32, wide.shape, 1) // w
        return jnp.where(rr == cc, wide, 0.0)

    def head(x, h):
        return x[:, h * hd:(h + 1) * hd]

    a_bar, r_bar, b_til, k_til, b_dec, k_dec, d_end, vv = [], [], [], [], [], [], [], []
    for t in range(nb):
        lw = lw_ref[t]
        parts = _split3(lw)
        c = sum(jnp.dot(m_incl, p, preferred_element_type=F32) for p in parts)
        cend = sum(jnp.dot(m_same, p, preferred_element_type=F32) for p in parts)
        einv = jnp.exp(-c)
        edec = jnp.exp(cend - c)
        a_bar.append(al_ref[t] * jnp.exp(c - lw))
        r_bar.append(r_ref[t] * jnp.exp(c))
        b_til.append(be_ref[t] * einv)
        k_til.append(k_ref[t] * einv)
        b_dec.append(be_ref[t] * edec)
        k_dec.append(k_ref[t] * edec)
        d_end.append(jnp.exp(cend))
        vv.append(v_ref[t].astype(F32))
    last_row = (lax.broadcasted_iota(jnp.int32, (R, seqs * hd), 0) & (clen - 1)) == clen - 1

    amats = [_dot_nt(jnp.concatenate([head(a_bar[t], h), head(r_bar[t], h)], axis=0),
                     jnp.concatenate([head(b_til[t], h), head(k_til[t], h)], axis=0)) for t, h in TH]

    lgs = [float(np.log1p(-np.exp2(-5.0 - h))) for h in range(B_HEADS)]
    qb = [qb_ref[t].astype(F32) for t in range(nb)]
    kb = [kb_ref[t].astype(F32) for t in range(nb)]
    qs = [qb[t][:, h * B_QK_DIM:(h + 1) * B_QK_DIM] for t, h in TG]
    khs = [kb[t][:, h * B_QK_DIM:(h + 1) * B_QK_DIM] for t, h in TG]
    vbs = [vb_ref[t][:, h * B_V_DIM:(h + 1) * B_V_DIM] for t, h in TG]
    diff = (row - col).astype(F32)
    pos_v = (lax.broadcasted_iota(jnp.int32, (R, B_V_DIM), 0) & (clen - 1)).astype(F32)
    pos_k = (lax.broadcasted_iota(jnp.int32, (R, B_QK_DIM), 0) & (clen - 1)).astype(F32)
    intra = [jnp.where(incl, jnp.exp(lg * diff), 0.0) for lg in lgs]
    cross = [jnp.exp(lg * (pos_v + 1.0)) for lg in lgs]
    kdec = [jnp.exp(lg * (clen - 1.0 - pos_k)) for lg in lgs]
    rstates = [r_scr[t * B_HEADS + h] for t, h in TG]
    scs = [_dot_nt(qs[i], khs[i]) * intra[h] for i, (t, h) in enumerate(TG)]
    qst = [_dot(expand(qs[i]), rstates[i]) * cross[h] for i, (t, h) in enumerate(TG)]

    a_ab = [jnp.where(strict, m[:R, :R], 0.0) for m in amats]
    a_ak = [jnp.where(strict, m[:R, R:], 0.0) for m in amats]
    a_rb = [jnp.where(incl, m[R:, :R], 0.0) for m in amats]
    a_rk = [jnp.where(incl, m[R:, R:], 0.0) for m in amats]
    n = len(TH)
    tinv = [eye + a for a in a_ab]
    vhs = [head(vv[t], h) for t, h in TH]
    av = [_dot(a_ak[i], vhs[i]) for i in range(n)]
    if log2c > 1:
        pw = [_dot(a, a) for a in a_ab]
    for it in range(log2c - 1):
        if it < log2c - 2:
            tp = [_dot(pw[i], jnp.concatenate([tinv[i], pw[i]], axis=1)) for i in range(n)]
            tinv = [tinv[i] + tp[i][:, :R] for i in range(n)]
            pw = [tp[i][:, R:] for i in range(n)]
        else:
            tinv = [tinv[i] + _dot(pw[i], tinv[i]) for i in range(n)]

    os_ = [_dot(scs[i], vbs[i]) + qst[i] for i in range(len(TG))]
    for i, (t, h) in enumerate(TG):
        r_scr[t * B_HEADS + h] = (rstates[i] * float(np.exp(lgs[h] * clen))
                                  + _dot_tn(expand(khs[i] * kdec[h]), vbs[i]))
    for t in range(nb):
        ob_o[t] = jnp.concatenate(os_[t * B_HEADS:(t + 1) * B_HEADS], axis=1)

    wu = [_dot(tinv[i], jnp.concatenate([head(a_bar[t], h), av[i]], axis=1)) for i, (t, h) in enumerate(TH)]
    states = [s_scr[t * A_HEADS + h] for t, h in TH]
    ws = [_dot_nt(jnp.concatenate([expand(wu[i][:, :hd]), expand(head(r_bar[t], h))], axis=0), states[i])
          for i, (t, h) in enumerate(TH)]
    uv = [jnp.concatenate([ws[i][:R] + wu[i][:, hd:], vhs[i]], axis=0) for i in range(n)]
    ys = [ws[i][R:] + _dot(jnp.concatenate([a_rb[i], a_rk[i]], axis=1), uv[i]) for i in range(n)]
    for t in range(nb):
        ya_o[t] = jnp.concatenate(ys[t * A_HEADS:(t + 1) * A_HEADS], axis=1)
    for i, (t, h) in enumerate(TH):
        d_row = jnp.sum(jnp.where(last_row, expand(head(d_end[t], h)), 0.0), axis=0, keepdims=True)
        bk = jnp.concatenate([expand(head(b_dec[t], h)), expand(head(k_dec[t], h))], axis=0)
        s_scr[t * A_HEADS + h] = states[i] * d_row + _dot_tn(uv[i], bk)

    @pl.when(c_idx == pl.num_programs(1) - 1)
    def _():
        for t, h in TH:
            st = s_scr[t * A_HEADS + h]
            for i in range(seqs):
                wkv_o[t * seqs + i, h] = st[:, i * hd:(i + 1) * hd]
        for t, h in TG:
            ret_o[t * seqs:(t + 1) * seqs, h] = r_scr[t * B_HEADS + h].reshape(seqs, B_QK_DIM, B_V_DIM)


def _mixer(ops, wkv0, ret0, n_seq, seq_len):
    n = ops[0].shape[0]
    R = CHUNK_ROWS
    if seq_len >= R:
        assert seq_len % R == 0
        seqs, clen, nchunks = 1, R, seq_len // R
    else:
        assert R % seq_len == 0 and seq_len & (seq_len - 1) == 0 and n_seq % (R // seq_len) == 0
        seqs, clen, nchunks = R // seq_len, seq_len, 1
    ntiles = n_seq // seqs
    nb = MIXER_TILES if seqs == 1 else MIXER_TILES_PACKED
    assert ntiles % nb == 0
    ops3 = [a.reshape(ntiles, nchunks * R, a.shape[1]) for a in ops]

    def rows(width):
        return pl.BlockSpec((nb, R, width), lambda i, c: (i, c, 0))

    wkv_spec = pl.BlockSpec((nb * seqs, A_HEADS, A_HEAD_DIM, A_HEAD_DIM), lambda i, c: (i, 0, 0, 0))
    ret_spec = pl.BlockSpec((nb * seqs, B_HEADS, B_QK_DIM, B_V_DIM), lambda i, c: (i, 0, 0, 0))
    ya, ob, wkv1, ret1 = pl.pallas_call(
        functools.partial(_mixer_kernel, nb, seqs, clen),
        grid=(ntiles // nb, nchunks),
        in_specs=[rows(A_WIDTH)] * 6 + [rows(B_QK_WIDTH), rows(B_QK_WIDTH), rows(B_WIDTH), wkv_spec, ret_spec],
        out_specs=[rows(A_WIDTH), rows(B_WIDTH), wkv_spec, ret_spec],
        out_shape=[jax.ShapeDtypeStruct((ntiles, nchunks * R, A_WIDTH), F32),
                   jax.ShapeDtypeStruct((ntiles, nchunks * R, B_WIDTH), F32),
                   jax.ShapeDtypeStruct(wkv0.shape, F32), jax.ShapeDtypeStruct(ret0.shape, F32)],
        scratch_shapes=[pltpu.VMEM((nb * A_HEADS, A_HEAD_DIM, seqs * A_HEAD_DIM), F32),
                        pltpu.VMEM((nb * B_HEADS, seqs * B_QK_DIM, B_V_DIM), F32)],
        compiler_params=pltpu.CompilerParams(dimension_semantics=("parallel", "arbitrary"),
                                             vmem_limit_bytes=V7X_VMEM_LIMIT_BYTES),
        name="mixer",
    )(*ops3, wkv0, ret0)
    return ya.reshape(n, A_WIDTH), ob.reshape(n, B_WIDTH), wkv1, ret1


def _layer_norm(z, g, b):
    mu = jnp.mean(z, axis=-1, keepdims=True)
    d = z - mu
    var = jnp.mean(d * d, axis=-1, keepdims=True)
    return d * lax.rsqrt(var + LN_EPS) * g + b


def _post_kernel(ya_ref, ob_ref, bonus_ref, g_ref, gb_ref, x_ref, lnxg_ref, lnxb_ref, rgg_ref, rgb_ref,
                 wout_ref, ln1g_ref, ln1b_ref, wr_ref, br_ref, h_o, gate_o, cnt_o):
    tm = x_ref.shape[0]

    def head_norm(t, group, eps, gg, bb):
        ones = _group_ones(t.shape[1], group)
        mu = _group_sum(t, ones) * (1.0 / group)
        d = t - mu
        var = _group_sum(d * d, ones) * (1.0 / group)
        return d * lax.rsqrt(var + eps) * gg + bb

    y_a = (head_norm(ya_ref[...], A_HEAD_DIM, GN_EPS_RWKV, lnxg_ref[...], lnxb_ref[...]) + bonus_ref[...]) * g_ref[...]
    y_b = head_norm(ob_ref[...], B_V_DIM, GN_EPS, rgg_ref[...], rgb_ref[...]) * gb_ref[...]
    y = jnp.concatenate([y_a, y_b], axis=1)
    mix = _dot(y, wout_ref[...])
    h = _layer_norm(DEEPNORM_ALPHA * x_ref[...] + mix, ln1g_ref[...], ln1b_ref[...])
    h_o[...] = h

    h_hi, h_lo = _split2(h)
    w_hi, w_lo = _split2(wr_ref[...])
    logits = (jnp.dot(h_hi, w_hi, preferred_element_type=F32) + jnp.dot(h_hi, w_lo, preferred_element_type=F32)
              + jnp.dot(h_lo, w_hi, preferred_element_type=F32)) + br_ref[...]
    lane = lax.broadcasted_iota(jnp.int32, (tm, ROUTER_LANES), 1)
    neg = -jnp.inf
    big = ROUTER_LANES
    cl = jnp.where(lane < N_GROUPS, logits, neg)
    cmax = jnp.max(cl, axis=-1, keepdims=True)
    grp = jnp.min(jnp.where(cl == cmax, lane, big), axis=-1, keepdims=True)
    gprob = 1.0 / jnp.sum(jnp.exp(cl - cmax), axis=-1, keepdims=True)
    lo_lane = FINE_LANE0 + grp * EXPERTS_PER_GROUP
    fv = jnp.where((lane >= lo_lane) & (lane < lo_lane + EXPERTS_PER_GROUP), logits, neg)
    m1 = jnp.max(fv, axis=-1, keepdims=True)
    i1 = jnp.min(jnp.where(fv == m1, lane, big), axis=-1, keepdims=True)
    fv2 = jnp.where(lane == i1, neg, fv)
    m2 = jnp.max(fv2, axis=-1, keepdims=True)
    i2 = jnp.min(jnp.where(fv2 == m2, lane, big), axis=-1, keepdims=True)
    e2 = jnp.exp(m2 - m1)
    w1 = gprob / (1.0 + e2)
    w2 = gprob * e2 / (1.0 + e2)
    gate_o[...] = (jnp.where(lane == i1, w1, 0.0) + jnp.where(lane == i2, w2, 0.0)
                   + jnp.where(lane == GROUP_LANE, grp.astype(F32), 0.0))
    onehot = jnp.where((lane == grp) & (lane < N_GROUPS), 1.0, 0.0)
    cnt_o[0] = jnp.broadcast_to(jnp.sum(onehot, axis=0, keepdims=True), (V7X_SUBLANES, ROUTER_LANES)).astype(jnp.int32)


def _post(ya, ob, bonus, g, gb, x2, W):
    n = x2.shape[0]
    tm = POST_ROWS
    assert n % tm == 0

    def full(a):
        return pl.BlockSpec(a.shape, lambda i: (0,) * a.ndim)

    def rows(width):
        return pl.BlockSpec((tm, width), lambda i: (i, 0))

    params = [W["lnx_g"], W["lnx_b"], W["ret_gn_g"], W["ret_gn_b"], W["w_out"], W["ln1_g"], W["ln1_b"],
              W["w_router"], W["b_router"]]
    return pl.pallas_call(
        _post_kernel,
        grid=(n // tm,),
        in_specs=[rows(A_WIDTH), rows(B_WIDTH), rows(A_WIDTH), rows(A_WIDTH), rows(B_WIDTH), rows(D_MODEL)]
        + [full(a) for a in params],
        out_specs=[rows(D_MODEL), rows(ROUTER_LANES),
                   pl.BlockSpec((1, V7X_SUBLANES, ROUTER_LANES), lambda i: (i, 0, 0))],
        out_shape=[jax.ShapeDtypeStruct((n, D_MODEL), F32), jax.ShapeDtypeStruct((n, ROUTER_LANES), F32),
                   jax.ShapeDtypeStruct((n // tm, V7X_SUBLANES, ROUTER_LANES), jnp.int32)],
        compiler_params=pltpu.CompilerParams(dimension_semantics=("parallel",),
                                             vmem_limit_bytes=V7X_VMEM_LIMIT_BYTES),
        name="post",
    )(ya, ob, bonus, g, gb, x2, *params)


def _sort_positions(gate, lofs):
    tm = gate.shape[0]
    lane = lax.broadcasted_iota(jnp.int32, (tm, ROUTER_LANES), 1)
    grp = gate[:, GROUP_LANE:GROUP_LANE + 1].astype(jnp.int32)
    onehot = jnp.where((lane == grp) & (lane < N_GROUPS), 1.0, 0.0)
    r = lax.broadcasted_iota(jnp.int32, (tm, tm), 0)
    c = lax.broadcasted_iota(jnp.int32, (tm, tm), 1)
    earlier = jnp.where(c < r, 1.0, 0.0).astype(BF16)
    prefix = jnp.dot(earlier, onehot.astype(BF16), preferred_element_type=F32)
    base = jnp.zeros((tm, ROUTER_LANES), F32)
    for g in range(N_GROUPS):
        base = jnp.where(lane == g, lofs[g].astype(F32), base)
    return jnp.sum(onehot * (base + prefix), axis=1, keepdims=True).astype(jnp.int32)


def _run_copies(action, plan_ref, i, src_of, dst_of, sem):
    for g in range(N_GROUPS):
        goff = plan_ref[i, g]
        pieces = plan_ref[i, N_GROUPS + g] // SORT_ALIGN
        lofs = plan_ref[i, 2 * N_GROUPS + g]

        def body(k, carry):
            lo = pl.multiple_of(lofs + k * SORT_ALIGN, SORT_ALIGN)
            go = pl.multiple_of(goff + k * SORT_ALIGN, SORT_ALIGN)
            cp = pltpu.make_async_copy(src_of(lo, go), dst_of(lo, go), sem)
            cp.start() if action == "start" else cp.wait()
            return carry

        lax.fori_loop(0, pieces, body, 0)


def _dispatch_kernel(plan_ref, h_ref, gate_ref, hs_in, gs_in, hs_o, gs_o, h_loc, g_loc, sem):
    del hs_in, gs_in
    i = pl.program_id(0)
    tm = h_ref.shape[0]
    lofs = [plan_ref[i, 2 * N_GROUPS + g] for g in range(N_GROUPS)]
    pos = _sort_positions(gate_ref[...], lofs)
    onehot_t = jnp.where(lax.broadcasted_iota(jnp.int32, (tm, SORT_LOCAL), 1) == pos, 1.0, 0.0).astype(BF16)
    idx = lax.broadcasted_iota(jnp.int32, (V7X_SUBLANES, SORT_LOCAL), 1)
    pos_row = (_dot_nt((idx // V7X_LANES).astype(F32), onehot_t) * float(V7X_LANES)
               + _dot_nt((idx % V7X_LANES).astype(F32), onehot_t))[0:1].astype(jnp.int32)
    perm = jnp.where(lax.broadcasted_iota(jnp.int32, (SORT_LOCAL, tm), 0) == pos_row, 1.0, 0.0).astype(BF16)
    h_sorted = jnp.dot(perm, h_ref[...].astype(BF16), preferred_element_type=F32).astype(BF16)
    g_sorted = sum(jnp.dot(perm, p, preferred_element_type=F32) for p in _split3(gate_ref[...]))

    def copies(action, tile):
        for src, dst in ((h_loc, hs_o), (g_loc, gs_o)):
            _run_copies(action, plan_ref, tile, lambda lo, go: src.at[pl.ds(lo, SORT_ALIGN)],
                        lambda lo, go: dst.at[pl.ds(go, SORT_ALIGN)], sem)

    @pl.when(i > 0)
    def _():
        copies("wait", i - 1)

    h_loc[...] = h_sorted
    g_loc[...] = g_sorted
    copies("start", i)

    @pl.when(i == pl.num_programs(0) - 1)
    def _():
        copies("wait", i)


def _experts_kernel(tile_group_ref, n_valid_ref, hs_ref, gs_ref, w1_ref, w3_ref, w2_ref, ys_o):
    j = pl.program_id(0)

    @pl.when(j < n_valid_ref[0])
    def _():
        x = hs_ref[...]
        gs = gs_ref[...]
        lane = lax.broadcasted_iota(jnp.int32, gs.shape, 1)
        g = tile_group_ref[j]
        acc = jnp.zeros(ys_o.shape, F32)
        for e in range(EXPERTS_PER_GROUP):
            ge = jnp.sum(jnp.where(lane == FINE_LANE0 + g * EXPERTS_PER_GROUP + e, gs, 0.0), axis=-1, keepdims=True)
            a = jnp.dot(x, w1_ref[e], preferred_element_type=F32)
            b = jnp.dot(x, w3_ref[e], preferred_element_type=F32)
            hid = (a * _sigmoid(a)) * b * ge
            acc = acc + jnp.dot(hid.astype(BF16), w2_ref[e], preferred_element_type=F32)
        ys_o[...] = acc.astype(ys_o.dtype)

    @pl.when(j >= n_valid_ref[0])
    def _():
        ys_o[...] = jnp.zeros_like(ys_o)


def _combine_kernel(plan_ref, h_ref, gate_ref, p_ref, ys_ref, ln2g_ref, ln2b_ref, wple_ref, wpg_ref, pleg_ref,
                    o_ref, y_loc, sem):
    i = pl.program_id(0)
    tm = h_ref.shape[0]
    y_loc[...] = jnp.zeros_like(y_loc)
    lofs = [plan_ref[i, 2 * N_GROUPS + g] for g in range(N_GROUPS)]
    def copies(action):
        _run_copies(action, plan_ref, i, lambda lo, go: ys_ref.at[pl.ds(go, SORT_ALIGN)],
                    lambda lo, go: y_loc.at[pl.ds(lo, SORT_ALIGN)], sem)

    copies("start")
    pos = _sort_positions(gate_ref[...], lofs)
    onehot_t = jnp.where(lax.broadcasted_iota(jnp.int32, (tm, SORT_LOCAL), 1) == pos, 1.0, 0.0).astype(BF16)
    ple_in = _dot(p_ref[...], wple_ref[...])
    copies("wait")
    ffn = jnp.dot(onehot_t, y_loc[...], preferred_element_type=F32)
    h2 = _layer_norm(DEEPNORM_ALPHA * h_ref[...] + ffn, ln2g_ref[...], ln2b_ref[...])
    ple = ple_in * _sigmoid(_dot(h2, wpg_ref[...]))
    ms = jnp.mean(ple * ple, axis=-1, keepdims=True)
    o_ref[...] = h2 + ple * lax.rsqrt(ms + LN_EPS) * pleg_ref[...]


def _ffn(h, gate, counts, p2, W):
    n = h.shape[0]
    tm = SORT_ROWS
    assert n % tm == 0 and tm == POST_ROWS and SORT_LOCAL >= tm + N_GROUPS * SORT_ALIGN
    ntiles = n // tm
    er = EXPERT_ROWS if n >= 2 * N_GROUPS * EXPERT_ROWS else EXPERT_ROWS_SHORT
    cnt = counts[:, 0, :N_GROUPS]
    run = (cnt + SORT_ALIGN - 1) // SORT_ALIGN * SORT_ALIGN
    lofs = jnp.cumsum(run, axis=1) - run
    seg = (jnp.sum(run, axis=0) + er - 1) // er * er
    gbase = jnp.cumsum(seg) - seg
    goff = gbase[None, :] + jnp.cumsum(run, axis=0) - run
    plan = jnp.concatenate([goff, run, lofs], axis=1).astype(jnp.int32)
    max_tiles = (n + ntiles * N_GROUPS * (SORT_ALIGN - 1)) // er + N_GROUPS
    cap = max_tiles * er
    n_valid = (jnp.sum(seg) // er).astype(jnp.int32).reshape(1)
    tile_start = jnp.arange(max_tiles, dtype=jnp.int32) * er
    tile_group = jnp.clip(jnp.sum(tile_start[:, None] >= (gbase + seg)[None, :], axis=1), 0, N_GROUPS - 1).astype(jnp.int32)

    cparams = dict(vmem_limit_bytes=V7X_VMEM_LIMIT_BYTES)
    any_spec = pl.BlockSpec(memory_space=pl.ANY)
    hs, gs = pl.pallas_call(
        _dispatch_kernel,
        grid_spec=pltpu.PrefetchScalarGridSpec(
            num_scalar_prefetch=1, grid=(ntiles,),
            in_specs=[pl.BlockSpec((tm, D_MODEL), lambda i, plan: (i, 0)),
                      pl.BlockSpec((tm, ROUTER_LANES), lambda i, plan: (i, 0)), any_spec, any_spec],
            out_specs=[any_spec, any_spec],
            scratch_shapes=[pltpu.VMEM((SORT_LOCAL, D_MODEL), BF16), pltpu.VMEM((SORT_LOCAL, ROUTER_LANES), F32),
                            pltpu.SemaphoreType.DMA(())]),
        out_shape=[jax.ShapeDtypeStruct((cap, D_MODEL), BF16), jax.ShapeDtypeStruct((cap, ROUTER_LANES), F32)],
        input_output_aliases={3: 0, 4: 1},
        compiler_params=pltpu.CompilerParams(dimension_semantics=("arbitrary",), **cparams),
        name="dispatch",
    )(plan, h, gate, jnp.zeros((cap, D_MODEL), BF16), jnp.zeros((cap, ROUTER_LANES), F32))

    def tile_rows(width):
        return pl.BlockSpec((er, width), lambda j, tg, nv: (jnp.minimum(j, nv[0] - 1), 0))

    def group_w(shape):
        return pl.BlockSpec((EXPERTS_PER_GROUP,) + shape, lambda j, tg, nv: (tg[jnp.minimum(j, nv[0] - 1)], 0, 0))

    ys = pl.pallas_call(
        _experts_kernel,
        grid_spec=pltpu.PrefetchScalarGridSpec(
            num_scalar_prefetch=2, grid=(max_tiles,),
            in_specs=[tile_rows(D_MODEL), tile_rows(ROUTER_LANES), group_w((D_MODEL, D_EXPERT)),
                      group_w((D_MODEL, D_EXPERT)), group_w((D_EXPERT, D_MODEL))],
            out_specs=pl.BlockSpec((er, D_MODEL), lambda j, tg, nv: (j, 0))),
        out_shape=jax.ShapeDtypeStruct((cap, D_MODEL), BF16),
        compiler_params=pltpu.CompilerParams(dimension_semantics=("arbitrary",), **cparams),
        name="experts",
    )(tile_group, n_valid, hs, gs, W["expert_w1"], W["expert_w3"], W["expert_w2"])

    def full(a):
        return pl.BlockSpec(a.shape, lambda i, plan: (0,) * a.ndim)

    params = [W["ln2_g"], W["ln2_b"], W["w_ple"], W["w_ple_gate"], W["ple_norm_g"]]
    return pl.pallas_call(
        _combine_kernel,
        grid_spec=pltpu.PrefetchScalarGridSpec(
            num_scalar_prefetch=1, grid=(ntiles,),
            in_specs=[pl.BlockSpec((tm, D_MODEL), lambda i, plan: (i, 0)),
                      pl.BlockSpec((tm, ROUTER_LANES), lambda i, plan: (i, 0)),
                      pl.BlockSpec((tm, D_PLE), lambda i, plan: (i, 0)), any_spec] + [full(a) for a in params],
            out_specs=pl.BlockSpec((tm, D_MODEL), lambda i, plan: (i, 0)),
            scratch_shapes=[pltpu.VMEM((SORT_LOCAL, D_MODEL), BF16), pltpu.SemaphoreType.DMA(())]),
        out_shape=jax.ShapeDtypeStruct((n, D_MODEL), F32),
        compiler_params=pltpu.CompilerParams(dimension_semantics=("arbitrary",), **cparams),
        name="combine",
    )(plan, h, gate, p2, ys, *params)


def _layer(x, p, x_prev, wkv0, ret0, pos0, W):
    n_seq, seq_len, _ = x.shape
    n = n_seq * seq_len
    x2 = x.reshape(n, D_MODEL)
    r, lw, k, v, al, be, g, bonus, qb, kb, vb, gb = _proj(x2, x_prev, seq_len, pos0, W)
    ya, ob, wkv1, ret1 = _mixer((r, lw, k, v, al, be, qb, kb, vb), wkv0, ret0, n_seq, seq_len)
    h, gate, counts = _post(ya, ob, bonus, g, gb, x2, W)
    out = _ffn(h, gate, counts, p.reshape(n, D_PLE), W)
    return out.reshape(n_seq, seq_len, D_MODEL), x[:, -1], wkv1, ret1


def _prep_weights(i, w_in, mu_shift, w_decay_up, decay_base, w_aaa_up, aaa_base, w_gate_up, k_k, k_a, r_k,
                  lnx_g, lnx_b, ret_gn_g, ret_gn_b, w_out, ln1_g, ln1_b,
                  router_coarse_w, router_coarse_b, router_fine_w, router_fine_b,
                  expert_w1, expert_w3, expert_w2, ln2_g, ln2_b, w_ple, w_ple_gate, ple_norm_g):
    def row(a):
        return a[i].reshape(1, -1).astype(F32)

    pad = ROUTER_LANES - N_GROUPS - N_EXPERTS
    w_router = jnp.concatenate([router_coarse_w[i], router_fine_w[i], jnp.zeros((D_MODEL, pad), F32)], axis=1)
    b_router = jnp.concatenate([router_coarse_b[i], router_fine_b[i], jnp.zeros((pad,), F32)]).reshape(1, -1)
    return {
        "w_in": w_in[i].astype(BF16), "mu_shift": row(mu_shift), "w_decay_up": w_decay_up[i].astype(BF16),
        "decay_base": row(decay_base), "w_aaa_up": w_aaa_up[i].astype(BF16), "aaa_base": row(aaa_base),
        "w_gate_up": w_gate_up[i].astype(BF16), "k_k": row(k_k), "k_a": row(k_a), "r_k": row(r_k),
        "lnx_g": row(lnx_g), "lnx_b": row(lnx_b), "ret_gn_g": row(ret_gn_g), "ret_gn_b": row(ret_gn_b),
        "w_out": w_out[i].astype(BF16), "ln1_g": row(ln1_g), "ln1_b": row(ln1_b),
        "w_router": w_router, "b_router": b_router,
        "expert_w1": expert_w1[i].astype(BF16), "expert_w3": expert_w3[i].astype(BF16),
        "expert_w2": expert_w2[i].astype(BF16), "ln2_g": row(ln2_g), "ln2_b": row(ln2_b),
        "w_ple": w_ple[i].astype(BF16), "w_ple_gate": w_ple_gate[i].astype(BF16), "ple_norm_g": row(ple_norm_g),
    }


def kernel(x_prompt, x_sample, p_prompt, p_sample, state_wkv, state_shift, state_ret, w_in, mu_shift, w_decay_up, decay_base, w_aaa_up, aaa_base, w_gate_up, k_k, k_a, r_k, lnx_g, lnx_b, ret_gn_g, ret_gn_b, w_out, ln1_g, ln1_b, router_coarse_w, router_coarse_b, router_fine_w, router_fine_b, expert_w1, expert_w3, expert_w2, ln2_g, ln2_b, w_ple, w_ple_gate, ple_norm_g):
    yp, ys = x_prompt, x_sample
    nb = x_prompt.shape[0]
    depth = w_in.shape[0]
    wkv_p, shift_p, ret_p, wkv_s, shift_s, ret_s = [], [], [], [], [], []
    for i in range(depth):
        W = _prep_weights(i, w_in, mu_shift, w_decay_up, decay_base, w_aaa_up, aaa_base, w_gate_up, k_k, k_a, r_k,
                          lnx_g, lnx_b, ret_gn_g, ret_gn_b, w_out, ln1_g, ln1_b,
                          router_coarse_w, router_coarse_b, router_fine_w, router_fine_b,
                          expert_w1, expert_w3, expert_w2, ln2_g, ln2_b, w_ple, w_ple_gate, ple_norm_g)
        yp, sp, wp, rp = _layer(yp, p_prompt[i], jnp.zeros((nb, D_MODEL), F32),
                                jnp.zeros((nb, A_HEADS, A_HEAD_DIM, A_HEAD_DIM), F32),
                                jnp.zeros((nb, B_HEADS, B_QK_DIM, B_V_DIM), F32), 0, W)
        ys, ss, wsm, rsm = _layer(ys, p_sample[i], state_shift[i], state_wkv[i], state_ret[i], PAST_LEN, W)
        wkv_p.append(wp); shift_p.append(sp); ret_p.append(rp)
        wkv_s.append(wsm); shift_s.append(ss); ret_s.append(rsm)
    return (yp, ys, jnp.stack(wkv_p, 0), jnp.stack(shift_p, 0), jnp.stack(ret_p, 0),
            jnp.stack(wkv_s, 0), jnp.stack(shift_s, 0), jnp.stack(ret_s, 0))
```

```python
import functools
import math

import numpy as np
import jax
import jax.numpy as jnp
from jax import lax
from jax.experimental import pallas as pl
from jax.experimental.pallas import tpu as pltpu

F32 = jnp.float32
BF16 = jnp.bfloat16

D_MODEL = 1024
D_PLE = 256
A_HEADS = 8
A_HEAD_DIM = 64
A_WIDTH = A_HEADS * A_HEAD_DIM
DECAY_LORA = 64
AAA_LORA = 64
GATE_LORA = 128
GN_EPS_RWKV = 64e-5
B_HEADS = 4
B_QK_DIM = 64
B_V_DIM = 128
B_QK_WIDTH = B_HEADS * B_QK_DIM
B_WIDTH = B_HEADS * B_V_DIM
ROPE_BASE = 10000.0
GN_EPS = 1e-5
SHIFT_WIDTH = 3 * A_WIDTH + DECAY_LORA + AAA_LORA + GATE_LORA
IN_WIDTH = SHIFT_WIDTH + 2 * B_QK_WIDTH + 2 * B_WIDTH
N_GROUPS = 4
EXPERTS_PER_GROUP = 4
N_EXPERTS = N_GROUPS * EXPERTS_PER_GROUP
D_EXPERT = 256
DEPTH = 1
PAST_LEN = 16384
DEEPNORM_ALPHA = (2 * DEPTH) ** 0.25
LN_EPS = 1e-5

V7X_LANES = 128
V7X_SUBLANES = 8
V7X_VMEM_LIMIT_BYTES = 56 * 1024 * 1024

PROJ_ROWS = 512
LOG_DECAY_OUT = 1
CHUNK_ROWS = 64
MIXER_TILES = 4
MIXER_TILES_PACKED = 2
POST_ROWS = 512
SORT_ROWS = 512
SORT_ALIGN = 16
SORT_LOCAL = 640
EXPERT_ROWS = 1024
EXPERT_ROWS_SHORT = 256
RUN_BITS = 6
GROUP_LANE = 0
ROUTER_LANES = V7X_LANES
FINE_LANE0 = N_GROUPS


def _dot(a, b):
    return jnp.dot(a.astype(BF16), b.astype(BF16), preferred_element_type=F32)


def _dot_nt(a, b):
    return lax.dot_general(a.astype(BF16), b.astype(BF16), (((1,), (1,)), ((), ())), preferred_element_type=F32)


def _dot_tn(a, b):
    return lax.dot_general(a.astype(BF16), b.astype(BF16), (((0,), (0,)), ((), ())), preferred_element_type=F32)


def _split2(x):
    hi = x.astype(BF16)
    lo = (x - hi.astype(F32)).astype(BF16)
    return hi, lo


def _split3(x):
    hi = x.astype(BF16)
    r1 = x - hi.astype(F32)
    mid = r1.astype(BF16)
    lo = (r1 - mid.astype(F32)).astype(BF16)
    return hi, mid, lo


def _sigmoid(x):
    return 1.0 / (1.0 + jnp.exp(-x))


def _group_ones(width, group):
    r = lax.broadcasted_iota(jnp.int32, (width, width), 0) // group
    c = lax.broadcasted_iota(jnp.int32, (width, width), 1) // group
    return jnp.where(r == c, 1.0, 0.0).astype(BF16)


def _group_sum(x, ones):
    return jnp.dot(x.astype(BF16), ones, preferred_element_type=F32)


def _proj_kernel(carry_mode, seq_len, tiles_per_seq,
                 x_ref, xp_ref, w_ref, mu_ref, wdec_ref, dbase_ref, waaa_ref, abase_ref, wgate_ref,
                 kk_ref, ka_ref, rk_ref, cos_ref, sin_ref,
                 r_o, lw_o, k_o, v_o, al_o, be_o, g_o, bonus_o, qb_o, kb_o, vb_o, gb_o,
                 carry_scr):
    tm = x_ref.shape[0]
    row = lax.broadcasted_iota(jnp.int32, (tm, SHIFT_WIDTH), 0)
    if carry_mode:
        xp = jnp.broadcast_to(xp_ref[0], (V7X_SUBLANES, D_MODEL))
        xb = jnp.concatenate([x_ref[...], xp], axis=0).astype(BF16)
        proj_all = jnp.dot(xb, w_ref[...], preferred_element_type=F32)
        proj = proj_all[:tm]
        cur_s = proj[:, :SHIFT_WIDTH]
        rolled = pltpu.roll(cur_s, 1, 0)
        j = pl.program_id(0) % tiles_per_seq

        @pl.when(pl.program_id(0) == 0)
        def _():
            carry_scr[...] = jnp.zeros_like(carry_scr)

        first = jnp.where(j == 0, proj_all[tm + V7X_SUBLANES - 1:, :SHIFT_WIDTH],
                          carry_scr[V7X_SUBLANES - 1:V7X_SUBLANES, :])
        prev = jnp.where(row == 0, first, rolled)
        carry_scr[...] = cur_s[tm - V7X_SUBLANES:, :]
    else:
        xb = x_ref[...].astype(BF16)
        proj = jnp.dot(xb, w_ref[...], preferred_element_type=F32)
        cur_s = proj[:, :SHIFT_WIDTH]
        rolled = pltpu.roll(cur_s, 1, 0)
        first = jnp.dot(xp_ref[...].astype(BF16), w_ref[:, :SHIFT_WIDTH], preferred_element_type=F32)
        prev = jnp.where((row & (seq_len - 1)) == 0, first, rolled)
    sh = cur_s + (prev - cur_s) * mu_ref[...]

    r = sh[:, :A_WIDTH]
    k0 = sh[:, A_WIDTH:2 * A_WIDTH]
    v = sh[:, 2 * A_WIDTH:3 * A_WIDTH]
    o = 3 * A_WIDTH
    w_lo = sh[:, o:o + DECAY_LORA]
    a_lo = sh[:, o + DECAY_LORA:o + DECAY_LORA + AAA_LORA]
    g_lo = sh[:, o + DECAY_LORA + AAA_LORA:SHIFT_WIDTH]

    z = -(dbase_ref[...] + _dot(jnp.tanh(w_lo), wdec_ref[...]))
    softplus = jnp.maximum(z, 0.0) + jnp.log(1.0 + jnp.exp(-jnp.abs(z)))
    log_w = -softplus - 0.5
    lw = -jnp.exp(log_w)
    a = _sigmoid(abase_ref[...] + _dot(a_lo, waaa_ref[...]))
    g = _dot(_sigmoid(g_lo), wgate_ref[...])

    ones64 = _group_ones(A_WIDTH, A_HEAD_DIM)
    kk0 = k0 * kk_ref[...]
    ssq = _group_sum(kk0 * kk0, ones64)
    kk = kk0 * jnp.minimum(lax.rsqrt(ssq), 1e12)
    k = k0 * (1.0 + (a - 1.0) * ka_ref[...])
    bonus = _group_sum(r * k * rk_ref[...], ones64) * v

    r_o[...] = (r).astype(r_o.dtype)
    lw_o[...] = lw
    k_o[...] = (k).astype(k_o.dtype)
    v_o[...] = (v).astype(v_o.dtype)
    al_o[...] = (-kk).astype(al_o.dtype)
    be_o[...] = (kk * a).astype(be_o.dtype)
    g_o[...] = (g).astype(g_o.dtype)
    bonus_o[...] = (bonus).astype(bonus_o.dtype)

    o = SHIFT_WIDTH
    q_b = proj[:, o:o + B_QK_WIDTH]
    k_b = proj[:, o + B_QK_WIDTH:o + 2 * B_QK_WIDTH]
    v_b = proj[:, o + 2 * B_QK_WIDTH:o + 2 * B_QK_WIDTH + B_WIDTH]
    g_b = proj[:, o + 2 * B_QK_WIDTH + B_WIDTH:]
    lane = lax.broadcasted_iota(jnp.int32, (tm, B_QK_WIDTH), 1)
    first_half = (lane & (B_QK_DIM - 1)) < (B_QK_DIM // 2)
    cos = cos_ref[...]
    sin = sin_ref[...]

    def rot(t):
        swapped = jnp.where(first_half, pltpu.roll(t, B_QK_WIDTH - B_QK_DIM // 2, 1), pltpu.roll(t, B_QK_DIM // 2, 1))
        return t * cos + swapped * sin

    qb_o[...] = (rot(q_b)).astype(qb_o.dtype)
    kb_o[...] = (rot(k_b) * (B_QK_DIM ** -0.5)).astype(kb_o.dtype)
    vb_o[...] = (v_b).astype(vb_o.dtype)
    gb_o[...] = (g_b * _sigmoid(g_b)).astype(gb_o.dtype)


def _proj(x2, x_prev, seq_len, pos0, W):
    n = x2.shape[0]
    tm = PROJ_ROWS
    assert n % tm == 0
    carry_mode = seq_len % tm == 0
    if carry_mode:
        tiles_per_seq = seq_len // tm
        xp = x_prev.reshape(-1, 1, D_MODEL)
        xp_spec = pl.BlockSpec((1, 1, D_MODEL), lambda i: (i // tiles_per_seq, 0, 0))
        tab_rows = seq_len
    else:
        assert tm % seq_len == 0 and seq_len & (seq_len - 1) == 0
        tiles_per_seq = 1
        xp = jnp.repeat(x_prev, seq_len, axis=0)
        xp_spec = pl.BlockSpec((tm, D_MODEL), lambda i: (i, 0))
        tab_rows = tm
    half = B_QK_DIM // 2
    inv = ROPE_BASE ** (-jnp.arange(half, dtype=F32) / half)
    pos = (pos0 + jnp.arange(seq_len, dtype=jnp.int32)).astype(F32)
    ang = pos[:, None] * inv[None, :]
    cos = jnp.tile(jnp.concatenate([jnp.cos(ang), jnp.cos(ang)], -1), (tab_rows // seq_len, B_HEADS))
    sin = jnp.tile(jnp.concatenate([-jnp.sin(ang), jnp.sin(ang)], -1), (tab_rows // seq_len, B_HEADS))
    tab_tiles = tab_rows // tm
    tab_spec = pl.BlockSpec((tm, B_QK_WIDTH), lambda i: (i % tab_tiles, 0))

    def full(a):
        return pl.BlockSpec(a.shape, lambda i: (0,) * a.ndim)

    def rows(width):
        return pl.BlockSpec((tm, width), lambda i: (i, 0))

    params = [W["w_in"], W["mu_shift"], W["w_decay_up"], W["decay_base"], W["w_aaa_up"], W["aaa_base"],
              W["w_gate_up"], W["k_k"], W["k_a"], W["r_k"]]
    widths = [A_WIDTH] * 8 + [B_QK_WIDTH, B_QK_WIDTH, B_WIDTH, B_WIDTH]
    outs = pl.pallas_call(
        functools.partial(_proj_kernel, carry_mode, seq_len, tiles_per_seq),
        grid=(n // tm,),
        in_specs=[rows(D_MODEL), xp_spec, pl.BlockSpec(params[0].shape, lambda i: (0, 0), pipeline_mode=pl.Buffered(1))]
        + [full(a) for a in params[1:]] + [tab_spec, tab_spec],
        out_specs=[rows(w) for w in widths],
        out_shape=[jax.ShapeDtypeStruct((n, w), F32 if i == LOG_DECAY_OUT else BF16) for i, w in enumerate(widths)],
        scratch_shapes=[pltpu.VMEM((V7X_SUBLANES, SHIFT_WIDTH), F32)],
        compiler_params=pltpu.CompilerParams(dimension_semantics=("arbitrary",),
                                             vmem_limit_bytes=V7X_VMEM_LIMIT_BYTES),
        name="proj",
    )(x2, xp, *params, cos, sin)
    return outs


def _mixer_kernel(nb, seqs, clen,
                  r_ref, lw_ref, k_ref, v_ref, al_ref, be_ref, qb_ref, kb_ref, vb_ref, wkv0_ref, ret0_ref,
                  ya_o, ob_o, wkv_o, ret_o, s_scr, r_scr):
    R = seqs * clen
    log2c = int(math.log2(clen))
    c_idx = pl.program_id(1)
    hd = A_HEAD_DIM
    TH = [(t, h) for t in range(nb) for h in range(A_HEADS)]
    TG = [(t, h) for t in range(nb) for h in range(B_HEADS)]

    @pl.when(c_idx == 0)
    def _():
        for t, h in TH:
            blocks = [wkv0_ref[t * seqs + i, h] for i in range(seqs)]
            s_scr[t * A_HEADS + h] = jnp.concatenate(blocks, axis=1) if seqs > 1 else blocks[0]
        for t, h in TG:
            r_scr[t * B_HEADS + h] = ret0_ref[t * seqs:(t + 1) * seqs, h].reshape(seqs * B_QK_DIM, B_V_DIM)

    row = lax.broadcasted_iota(jnp.int32, (R, R), 0)
    col = lax.broadcasted_iota(jnp.int32, (R, R), 1)
    same = (row >> log2c) == (col >> log2c)
    incl = same & (col <= row)
    strict = same & (col < row)
    m_incl = jnp.where(incl, 1.0, 0.0).astype(BF16)
    m_same = jnp.where(same, 1.0, 0.0).astype(BF16)
    eye = jnp.where(row == col, 1.0, 0.0).astype(F32)

    def expand(t):
        if seqs == 1:
            return t
        w = t.shape[1]
        wide = jnp.concatenate([t] * seqs, axis=1)
        rr = lax.broadcasted_iota(jnp.int32, wide.shape, 0) >> log2c
        cc = lax.broadcasted_iota(jnp.int32, wide.shape, 1) // w
        return jnp.where(rr == cc, wide, 0.0)

    def head(x, h):
        return x[:, h * hd:(h + 1) * hd]

    a_bar, r_bar, b_til, k_til, b_dec, k_dec, d_end, vv = [], [], [], [], [], [], [], []
    for t in range(nb):
        lw = lw_ref[t]
        parts = _split3(lw)
        c = sum(jnp.dot(m_incl, p, preferred_element_type=F32) for p in parts)
        cend = sum(jnp.dot(m_same, p, preferred_element_type=F32) for p in parts)
        einv = jnp.exp(-c)
        edec = jnp.exp(cend - c)
        a_bar.append(al_ref[t] * jnp.exp(c - lw))
        r_bar.append(r_ref[t] * jnp.exp(c))
        b_til.append(be_ref[t] * einv)
        k_til.append(k_ref[t] * einv)
        b_dec.append(be_ref[t] * edec)
        k_dec.append(k_ref[t] * edec)
        d_end.append(jnp.exp(cend))
        vv.append(v_ref[t].astype(F32))
    last_row = (lax.broadcasted_iota(jnp.int32, (R, seqs * hd), 0) & (clen - 1)) == clen - 1

    amats = [_dot_nt(jnp.concatenate([head(a_bar[t], h), head(r_bar[t], h)], axis=0),
                     jnp.concatenate([head(b_til[t], h), head(k_til[t], h)], axis=0)) for t, h in TH]

    lgs = [float(np.log1p(-np.exp2(-5.0 - h))) for h in range(B_HEADS)]
    qb = [qb_ref[t].astype(F32) for t in range(nb)]
    kb = [kb_ref[t].astype(F32) for t in range(nb)]
    qs = [qb[t][:, h * B_QK_DIM:(h + 1) * B_QK_DIM] for t, h in TG]
    khs = [kb[t][:, h * B_QK_DIM:(h + 1) * B_QK_DIM] for t, h in TG]
    vbs = [vb_ref[t][:, h * B_V_DIM:(h + 1) * B_V_DIM] for t, h in TG]
    diff = (row - col).astype(F32)
    pos_v = (lax.broadcasted_iota(jnp.int32, (R, B_V_DIM), 0) & (clen - 1)).astype(F32)
    pos_k = (lax.broadcasted_iota(jnp.int32, (R, B_QK_DIM), 0) & (clen - 1)).astype(F32)
    intra = [jnp.where(incl, jnp.exp(lg * diff), 0.0) for lg in lgs]
    cross = [jnp.exp(lg * (pos_v + 1.0)) for lg in lgs]
    kdec = [jnp.exp(lg * (clen - 1.0 - pos_k)) for lg in lgs]
    rstates = [r_scr[t * B_HEADS + h] for t, h in TG]
    scs = [_dot_nt(qs[i], khs[i]) * intra[h] for i, (t, h) in enumerate(TG)]
    qst = [_dot(expand(qs[i]), rstates[i]) * cross[h] for i, (t, h) in enumerate(TG)]

    a_ab = [jnp.where(strict, m[:R, :R], 0.0) for m in amats]
    a_ak = [jnp.where(strict, m[:R, R:], 0.0) for m in amats]
    a_rb = [jnp.where(incl, m[R:, :R], 0.0) for m in amats]
    a_rk = [jnp.where(incl, m[R:, R:], 0.0) for m in amats]
    n = len(TH)
    tinv = [eye + a for a in a_ab]
    vhs = [head(vv[t], h) for t, h in TH]
    av = [_dot(a_ak[i], vhs[i]) for i in range(n)]
    if log2c > 1:
        pw = [_dot(a, a) for a in a_ab]
    for it in range(log2c - 1):
        if it < log2c - 2:
            tp = [_dot(pw[i], jnp.concatenate([tinv[i], pw[i]], axis=1)) for i in range(n)]
            tinv = [tinv[i] + tp[i][:, :R] for i in range(n)]
            pw = [tp[i][:, R:] for i in range(n)]
        else:
            tinv = [tinv[i] + _dot(pw[i], tinv[i]) for i in range(n)]

    os_ = [_dot(scs[i], vbs[i]) + qst[i] for i in range(len(TG))]
    for i, (t, h) in enumerate(TG):
        r_scr[t * B_HEADS + h] = (rstates[i] * float(np.exp(lgs[h] * clen))
                                  + _dot_tn(expand(khs[i] * kdec[h]), vbs[i]))
    for t in range(nb):
        ob_o[t] = jnp.concatenate(os_[t * B_HEADS:(t + 1) * B_HEADS], axis=1)

    wu = [_dot(tinv[i], jnp.concatenate([head(a_bar[t], h), av[i]], axis=1)) for i, (t, h) in enumerate(TH)]
    states = [s_scr[t * A_HEADS + h] for t, h in TH]
    ws = [_dot_nt(jnp.concatenate([expand(wu[i][:, :hd]), expand(head(r_bar[t], h))], axis=0), states[i])
          for i, (t, h) in enumerate(TH)]
    uv = [jnp.concatenate([ws[i][:R] + wu[i][:, hd:], vhs[i]], axis=0) for i in range(n)]
    ys = [ws[i][R:] + _dot(jnp.concatenate([a_rb[i], a_rk[i]], axis=1), uv[i]) for i in range(n)]
    for t in range(nb):
        ya_o[t] = jnp.concatenate(ys[t * A_HEADS:(t + 1) * A_HEADS], axis=1)
    for i, (t, h) in enumerate(TH):
        d_row = jnp.sum(jnp.where(last_row, expand(head(d_end[t], h)), 0.0), axis=0, keepdims=True)
        bk = jnp.concatenate([expand(head(b_dec[t], h)), expand(head(k_dec[t], h))], axis=0)
        s_scr[t * A_HEADS + h] = states[i] * d_row + _dot_tn(uv[i], bk)

    @pl.when(c_idx == pl.num_programs(1) - 1)
    def _():
        for t, h in TH:
            st = s_scr[t * A_HEADS + h]
            for i in range(seqs):
                wkv_o[t * seqs + i, h] = st[:, i * hd:(i + 1) * hd]
        for t, h in TG:
            ret_o[t * seqs:(t + 1) * seqs, h] = r_scr[t * B_HEADS + h].reshape(seqs, B_QK_DIM, B_V_DIM)


def _mixer(ops, wkv0, ret0, n_seq, seq_len):
    n = ops[0].shape[0]
    R = CHUNK_ROWS
    if seq_len >= R:
        assert seq_len % R == 0
        seqs, clen, nchunks = 1, R, seq_len // R
    else:
        assert R % seq_len == 0 and seq_len & (seq_len - 1) == 0 and n_seq % (R // seq_len) == 0
        seqs, clen, nchunks = R // seq_len, seq_len, 1
    ntiles = n_seq // seqs
    nb = MIXER_TILES if seqs == 1 else MIXER_TILES_PACKED
    assert ntiles % nb == 0
    ops3 = [a.reshape(ntiles, nchunks * R, a.shape[1]) for a in ops]

    def rows(width):
        return pl.BlockSpec((nb, R, width), lambda i, c: (i, c, 0))

    wkv_spec = pl.BlockSpec((nb * seqs, A_HEADS, A_HEAD_DIM, A_HEAD_DIM), lambda i, c: (i, 0, 0, 0))
    ret_spec = pl.BlockSpec((nb * seqs, B_HEADS, B_QK_DIM, B_V_DIM), lambda i, c: (i, 0, 0, 0))
    ya, ob, wkv1, ret1 = pl.pallas_call(
        functools.partial(_mixer_kernel, nb, seqs, clen),
        grid=(ntiles // nb, nchunks),
        in_specs=[rows(A_WIDTH)] * 6 + [rows(B_QK_WIDTH), rows(B_QK_WIDTH), rows(B_WIDTH), wkv_spec, ret_spec],
        out_specs=[rows(A_WIDTH), rows(B_WIDTH), wkv_spec, ret_spec],
        out_shape=[jax.ShapeDtypeStruct((ntiles, nchunks * R, A_WIDTH), F32),
                   jax.ShapeDtypeStruct((ntiles, nchunks * R, B_WIDTH), F32),
                   jax.ShapeDtypeStruct(wkv0.shape, F32), jax.ShapeDtypeStruct(ret0.shape, F32)],
        scratch_shapes=[pltpu.VMEM((nb * A_HEADS, A_HEAD_DIM, seqs * A_HEAD_DIM), F32),
                        pltpu.VMEM((nb * B_HEADS, seqs * B_QK_DIM, B_V_DIM), F32)],
        compiler_params=pltpu.CompilerParams(dimension_semantics=("parallel", "arbitrary"),
                                             vmem_limit_bytes=V7X_VMEM_LIMIT_BYTES),
        name="mixer",
    )(*ops3, wkv0, ret0)
    return ya.reshape(n, A_WIDTH), ob.reshape(n, B_WIDTH), wkv1, ret1


def _layer_norm(z, g, b):
    mu = jnp.mean(z, axis=-1, keepdims=True)
    d = z - mu
    var = jnp.mean(d * d, axis=-1, keepdims=True)
    return d * lax.rsqrt(var + LN_EPS) * g + b


def _post_kernel(ya_ref, ob_ref, bonus_ref, g_ref, gb_ref, x_ref, lnxg_ref, lnxb_ref, rgg_ref, rgb_ref,
                 wout_ref, ln1g_ref, ln1b_ref, wr_ref, br_ref, h_o, gate_o, cnt_o):
    tm = x_ref.shape[0]

    def head_norm(t, group, eps, gg, bb):
        ones = _group_ones(t.shape[1], group)
        mu = _group_sum(t, ones) * (1.0 / group)
        d = t - mu
        var = _group_sum(d * d, ones) * (1.0 / group)
        return d * lax.rsqrt(var + eps) * gg + bb

    y_a = (head_norm(ya_ref[...], A_HEAD_DIM, GN_EPS_RWKV, lnxg_ref[...], lnxb_ref[...]) + bonus_ref[...]) * g_ref[...]
    y_b = head_norm(ob_ref[...], B_V_DIM, GN_EPS, rgg_ref[...], rgb_ref[...]) * gb_ref[...]
    y = jnp.concatenate([y_a, y_b], axis=1)
    mix = _dot(y, wout_ref[...])
    h = _layer_norm(DEEPNORM_ALPHA * x_ref[...] + mix, ln1g_ref[...], ln1b_ref[...])
    h_o[...] = h

    h_hi, h_lo = _split2(h)
    w_hi, w_lo = _split2(wr_ref[...])
    logits = (jnp.dot(h_hi, w_hi, preferred_element_type=F32) + jnp.dot(h_hi, w_lo, preferred_element_type=F32)
              + jnp.dot(h_lo, w_hi, preferred_element_type=F32)) + br_ref[...]
    lane = lax.broadcasted_iota(jnp.int32, (tm, ROUTER_LANES), 1)
    neg = -jnp.inf
    big = ROUTER_LANES
    cl = jnp.where(lane < N_GROUPS, logits, neg)
    cmax = jnp.max(cl, axis=-1, keepdims=True)
    grp = jnp.min(jnp.where(cl == cmax, lane, big), axis=-1, keepdims=True)
    gprob = 1.0 / jnp.sum(jnp.exp(cl - cmax), axis=-1, keepdims=True)
    lo_lane = FINE_LANE0 + grp * EXPERTS_PER_GROUP
    fv = jnp.where((lane >= lo_lane) & (lane < lo_lane + EXPERTS_PER_GROUP), logits, neg)
    m1 = jnp.max(fv, axis=-1, keepdims=True)
    i1 = jnp.min(jnp.where(fv == m1, lane, big), axis=-1, keepdims=True)
    fv2 = jnp.where(lane == i1, neg, fv)
    m2 = jnp.max(fv2, axis=-1, keepdims=True)
    i2 = jnp.min(jnp.where(fv2 == m2, lane, big), axis=-1, keepdims=True)
    e2 = jnp.exp(m2 - m1)
    w1 = gprob / (1.0 + e2)
    w2 = gprob * e2 / (1.0 + e2)
    gate_o[...] = (jnp.where(lane == i1, w1, 0.0) + jnp.where(lane == i2, w2, 0.0)
                   + jnp.where(lane == GROUP_LANE, grp.astype(F32), 0.0))
    onehot = jnp.where((lane == grp) & (lane < N_GROUPS), 1.0, 0.0)
    cnt_o[0] = jnp.broadcast_to(jnp.sum(onehot, axis=0, keepdims=True), (V7X_SUBLANES, ROUTER_LANES)).astype(jnp.int32)


def _post(ya, ob, bonus, g, gb, x2, W):
    n = x2.shape[0]
    tm = POST_ROWS
    assert n % tm == 0

    def full(a):
        return pl.BlockSpec(a.shape, lambda i: (0,) * a.ndim)

    def rows(width):
        return pl.BlockSpec((tm, width), lambda i: (i, 0))

    params = [W["lnx_g"], W["lnx_b"], W["ret_gn_g"], W["ret_gn_b"], W["w_out"], W["ln1_g"], W["ln1_b"],
              W["w_router"], W["b_router"]]
    return pl.pallas_call(
        _post_kernel,
        grid=(n // tm,),
        in_specs=[rows(A_WIDTH), rows(B_WIDTH), rows(A_WIDTH), rows(A_WIDTH), rows(B_WIDTH), rows(D_MODEL)]
        + [full(a) for a in params],
        out_specs=[rows(D_MODEL), rows(ROUTER_LANES),
                   pl.BlockSpec((1, V7X_SUBLANES, ROUTER_LANES), lambda i: (i, 0, 0))],
        out_shape=[jax.ShapeDtypeStruct((n, D_MODEL), F32), jax.ShapeDtypeStruct((n, ROUTER_LANES), F32),
                   jax.ShapeDtypeStruct((n // tm, V7X_SUBLANES, ROUTER_LANES), jnp.int32)],
        compiler_params=pltpu.CompilerParams(dimension_semantics=("parallel",),
                                             vmem_limit_bytes=V7X_VMEM_LIMIT_BYTES),
        name="post",
    )(ya, ob, bonus, g, gb, x2, *params)


def _sort_positions(gate, lofs):
    tm = gate.shape[0]
    lane = lax.broadcasted_iota(jnp.int32, (tm, ROUTER_LANES), 1)
    grp = gate[:, GROUP_LANE:GROUP_LANE + 1].astype(jnp.int32)
    onehot = jnp.where((lane == grp) & (lane < N_GROUPS), 1.0, 0.0)
    r = lax.broadcasted_iota(jnp.int32, (tm, tm), 0)
    c = lax.broadcasted_iota(jnp.int32, (tm, tm), 1)
    earlier = jnp.where(c < r, 1.0, 0.0).astype(BF16)
    prefix = jnp.dot(earlier, onehot.astype(BF16), preferred_element_type=F32)
    base = jnp.zeros((tm, ROUTER_LANES), F32)
    for g in range(N_GROUPS):
        base = jnp.where(lane == g, lofs[g].astype(F32), base)
    return jnp.sum(onehot * (base + prefix), axis=1, keepdims=True).astype(jnp.int32)


def _run_copies(action, plan_ref, i, src_of, dst_of, sem):
    for g in range(N_GROUPS):
        goff = plan_ref[i, g]
        k = plan_ref[i, N_GROUPS + g] // SORT_ALIGN
        lofs = plan_ref[i, 2 * N_GROUPS + g]
        for b in reversed(range(RUN_BITS)):
            size = SORT_ALIGN << b

            @pl.when(((k >> b) & 1) == 1)
            def _():
                done = ((k >> (b + 1)) << (b + 1)) * SORT_ALIGN
                lo = pl.multiple_of(lofs + done, SORT_ALIGN)
                go = pl.multiple_of(goff + done, SORT_ALIGN)
                cp = pltpu.make_async_copy(src_of(lo, go, size), dst_of(lo, go, size), sem)
                cp.start() if action == "start" else cp.wait()


def _dispatch_kernel(plan_ref, h_ref, gate_ref, hs_in, gs_in, hs_o, gs_o, h_loc, g_loc, sem):
    del hs_in, gs_in
    i = pl.program_id(0)
    tm = h_ref.shape[0]
    lofs = [plan_ref[i, 2 * N_GROUPS + g] for g in range(N_GROUPS)]
    pos = _sort_positions(gate_ref[...], lofs)
    onehot_t = jnp.where(lax.broadcasted_iota(jnp.int32, (tm, SORT_LOCAL), 1) == pos, 1.0, 0.0).astype(BF16)
    idx = lax.broadcasted_iota(jnp.int32, (V7X_SUBLANES, SORT_LOCAL), 1)
    pos_row = (_dot_nt((idx // V7X_LANES).astype(F32), onehot_t) * float(V7X_LANES)
               + _dot_nt((idx % V7X_LANES).astype(F32), onehot_t))[0:1].astype(jnp.int32)
    perm = jnp.where(lax.broadcasted_iota(jnp.int32, (SORT_LOCAL, tm), 0) == pos_row, 1.0, 0.0).astype(BF16)
    h_sorted = jnp.dot(perm, h_ref[...].astype(BF16), preferred_element_type=F32).astype(BF16)
    g_sorted = sum(jnp.dot(perm, p, preferred_element_type=F32) for p in _split3(gate_ref[...]))

    def copies(action, tile):
        for src, dst in ((h_loc, hs_o), (g_loc, gs_o)):
            _run_copies(action, plan_ref, tile, lambda lo, go, sz: src.at[pl.ds(lo, sz)],
                        lambda lo, go, sz: dst.at[pl.ds(go, sz)], sem)

    @pl.when(i > 0)
    def _():
        copies("wait", i - 1)

    h_loc[...] = h_sorted
    g_loc[...] = g_sorted
    copies("start", i)

    @pl.when(i == pl.num_programs(0) - 1)
    def _():
        copies("wait", i)


def _experts_kernel(tile_group_ref, n_valid_ref, hs_ref, gs_ref, w1_ref, w3_ref, w2_ref, ys_o):
    j = pl.program_id(0)

    @pl.when(j < n_valid_ref[0])
    def _():
        x = hs_ref[...]
        gs = gs_ref[...]
        lane = lax.broadcasted_iota(jnp.int32, gs.shape, 1)
        g = tile_group_ref[j]
        acc = jnp.zeros(ys_o.shape, F32)
        for e in range(EXPERTS_PER_GROUP):
            ge = jnp.sum(jnp.where(lane == FINE_LANE0 + g * EXPERTS_PER_GROUP + e, gs, 0.0), axis=-1, keepdims=True)
            a = jnp.dot(x, w1_ref[e], preferred_element_type=F32)
            b = jnp.dot(x, w3_ref[e], preferred_element_type=F32)
            hid = (a * _sigmoid(a)) * b * ge
            acc = acc + jnp.dot(hid.astype(BF16), w2_ref[e], preferred_element_type=F32)
        ys_o[...] = acc.astype(ys_o.dtype)

    @pl.when(j >= n_valid_ref[0])
    def _():
        ys_o[...] = jnp.zeros_like(ys_o)


def _combine_kernel(plan_ref, h_ref, gate_ref, p_ref, ys_ref, ln2g_ref, ln2b_ref, wple_ref, wpg_ref, pleg_ref,
                    o_ref, y_loc, sem):
    i = pl.program_id(0)
    tm = h_ref.shape[0]
    y_loc[...] = jnp.zeros_like(y_loc)
    lofs = [plan_ref[i, 2 * N_GROUPS + g] for g in range(N_GROUPS)]
    def copies(action):
        _run_copies(action, plan_ref, i, lambda lo, go, sz: ys_ref.at[pl.ds(go, sz)],
                    lambda lo, go, sz: y_loc.at[pl.ds(lo, sz)], sem)

    copies("start")
    pos = _sort_positions(gate_ref[...], lofs)
    onehot_t = jnp.where(lax.broadcasted_iota(jnp.int32, (tm, SORT_LOCAL), 1) == pos, 1.0, 0.0).astype(BF16)
    ple_in = _dot(p_ref[...], wple_ref[...])
    copies("wait")
    ffn = jnp.dot(onehot_t, y_loc[...], preferred_element_type=F32)
    h2 = _layer_norm(DEEPNORM_ALPHA * h_ref[...] + ffn, ln2g_ref[...], ln2b_ref[...])
    ple = ple_in * _sigmoid(_dot(h2, wpg_ref[...]))
    ms = jnp.mean(ple * ple, axis=-1, keepdims=True)
    o_ref[...] = h2 + ple * lax.rsqrt(ms + LN_EPS) * pleg_ref[...]


def _ffn(h, gate, counts, p2, W):
    n = h.shape[0]
    tm = SORT_ROWS
    assert n % tm == 0 and tm == POST_ROWS and SORT_LOCAL >= tm + N_GROUPS * SORT_ALIGN
    ntiles = n // tm
    er = EXPERT_ROWS if n >= 2 * N_GROUPS * EXPERT_ROWS else EXPERT_ROWS_SHORT
    cnt = counts[:, 0, :N_GROUPS]
    run = (cnt + SORT_ALIGN - 1) // SORT_ALIGN * SORT_ALIGN
    lofs = jnp.cumsum(run, axis=1) - run
    seg = (jnp.sum(run, axis=0) + er - 1) // er * er
    gbase = jnp.cumsum(seg) - seg
    goff = gbase[None, :] + jnp.cumsum(run, axis=0) - run
    plan = jnp.concatenate([goff, run, lofs], axis=1).astype(jnp.int32)
    max_tiles = (n + ntiles * N_GROUPS * (SORT_ALIGN - 1)) // er + N_GROUPS
    cap = max_tiles * er
    n_valid = (jnp.sum(seg) // er).astype(jnp.int32).reshape(1)
    tile_start = jnp.arange(max_tiles, dtype=jnp.int32) * er
    tile_group = jnp.clip(jnp.sum(tile_start[:, None] >= (gbase + seg)[None, :], axis=1), 0, N_GROUPS - 1).astype(jnp.int32)

    cparams = dict(vmem_limit_bytes=V7X_VMEM_LIMIT_BYTES)
    any_spec = pl.BlockSpec(memory_space=pl.ANY)
    hs, gs = pl.pallas_call(
        _dispatch_kernel,
        grid_spec=pltpu.PrefetchScalarGridSpec(
            num_scalar_prefetch=1, grid=(ntiles,),
            in_specs=[pl.BlockSpec((tm, D_MODEL), lambda i, plan: (i, 0)),
                      pl.BlockSpec((tm, ROUTER_LANES), lambda i, plan: (i, 0)), any_spec, any_spec],
            out_specs=[any_spec, any_spec],
            scratch_shapes=[pltpu.VMEM((SORT_LOCAL, D_MODEL), BF16), pltpu.VMEM((SORT_LOCAL, ROUTER_LANES), F32),
                            pltpu.SemaphoreType.DMA(())]),
        out_shape=[jax.ShapeDtypeStruct((cap, D_MODEL), BF16), jax.ShapeDtypeStruct((cap, ROUTER_LANES), F32)],
        input_output_aliases={3: 0, 4: 1},
        compiler_params=pltpu.CompilerParams(dimension_semantics=("arbitrary",), **cparams),
        name="dispatch",
    )(plan, h, gate, jnp.zeros((cap, D_MODEL), BF16), jnp.zeros((cap, ROUTER_LANES), F32))

    def tile_rows(width):
        return pl.BlockSpec((er, width), lambda j, tg, nv: (jnp.minimum(j, nv[0] - 1), 0))

    def group_w(shape):
        return pl.BlockSpec((EXPERTS_PER_GROUP,) + shape, lambda j, tg, nv: (tg[jnp.minimum(j, nv[0] - 1)], 0, 0))

    ys = pl.pallas_call(
        _experts_kernel,
        grid_spec=pltpu.PrefetchScalarGridSpec(
            num_scalar_prefetch=2, grid=(max_tiles,),
            in_specs=[tile_rows(D_MODEL), tile_rows(ROUTER_LANES), group_w((D_MODEL, D_EXPERT)),
                      group_w((D_MODEL, D_EXPERT)), group_w((D_EXPERT, D_MODEL))],
            out_specs=pl.BlockSpec((er, D_MODEL), lambda j, tg, nv: (j, 0))),
        out_shape=jax.ShapeDtypeStruct((cap, D_MODEL), BF16),
        compiler_params=pltpu.CompilerParams(dimension_semantics=("arbitrary",), **cparams),
        name="experts",
    )(tile_group, n_valid, hs, gs, W["expert_w1"], W["expert_w3"], W["expert_w2"])

    def full(a):
        return pl.BlockSpec(a.shape, lambda i, plan: (0,) * a.ndim)

    params = [W["ln2_g"], W["ln2_b"], W["w_ple"], W["w_ple_gate"], W["ple_norm_g"]]
    return pl.pallas_call(
        _combine_kernel,
        grid_spec=pltpu.PrefetchScalarGridSpec(
            num_scalar_prefetch=1, grid=(ntiles,),
            in_specs=[pl.BlockSpec((tm, D_MODEL), lambda i, plan: (i, 0)),
                      pl.BlockSpec((tm, ROUTER_LANES), lambda i, plan: (i, 0)),
                      pl.BlockSpec((tm, D_PLE), lambda i, plan: (i, 0)), any_spec] + [full(a) for a in params],
            out_specs=pl.BlockSpec((tm, D_MODEL), lambda i, plan: (i, 0)),
            scratch_shapes=[pltpu.VMEM((SORT_LOCAL, D_MODEL), BF16), pltpu.SemaphoreType.DMA(())]),
        out_shape=jax.ShapeDtypeStruct((n, D_MODEL), F32),
        compiler_params=pltpu.CompilerParams(dimension_semantics=("arbitrary",), **cparams),
        name="combine",
    )(plan, h, gate, p2, ys, *params)


def _layer(x, p, x_prev, wkv0, ret0, pos0, W):
    n_seq, seq_len, _ = x.shape
    n = n_seq * seq_len
    x2 = x.reshape(n, D_MODEL)
    r, lw, k, v, al, be, g, bonus, qb, kb, vb, gb = _proj(x2, x_prev, seq_len, pos0, W)
    ya, ob, wkv1, ret1 = _mixer((r, lw, k, v, al, be, qb, kb, vb), wkv0, ret0, n_seq, seq_len)
    h, gate, counts = _post(ya, ob, bonus, g, gb, x2, W)
    out = _ffn(h, gate, counts, p.reshape(n, D_PLE), W)
    return out.reshape(n_seq, seq_len, D_MODEL), x[:, -1], wkv1, ret1


def _prep_weights(i, w_in, mu_shift, w_decay_up, decay_base, w_aaa_up, aaa_base, w_gate_up, k_k, k_a, r_k,
                  lnx_g, lnx_b, ret_gn_g, ret_gn_b, w_out, ln1_g, ln1_b,
                  router_coarse_w, router_coarse_b, router_fine_w, router_fine_b,
                  expert_w1, expert_w3, expert_w2, ln2_g, ln2_b, w_ple, w_ple_gate, ple_norm_g):
    def row(a):
        return a[i].reshape(1, -1).astype(F32)

    pad = ROUTER_LANES - N_GROUPS - N_EXPERTS
    w_router = jnp.concatenate([router_coarse_w[i], router_fine_w[i], jnp.zeros((D_MODEL, pad), F32)], axis=1)
    b_router = jnp.concatenate([router_coarse_b[i], router_fine_b[i], jnp.zeros((pad,), F32)]).reshape(1, -1)
    return {
        "w_in": w_in[i].astype(BF16), "mu_shift": row(mu_shift), "w_decay_up": w_decay_up[i].astype(BF16),
        "decay_base": row(decay_base), "w_aaa_up": w_aaa_up[i].astype(BF16), "aaa_base": row(aaa_base),
        "w_gate_up": w_gate_up[i].astype(BF16), "k_k": row(k_k), "k_a": row(k_a), "r_k": row(r_k),
        "lnx_g": row(lnx_g), "lnx_b": row(lnx_b), "ret_gn_g": row(ret_gn_g), "ret_gn_b": row(ret_gn_b),
        "w_out": w_out[i].astype(BF16), "ln1_g": row(ln1_g), "ln1_b": row(ln1_b),
        "w_router": w_router, "b_router": b_router,
        "expert_w1": expert_w1[i].astype(BF16), "expert_w3": expert_w3[i].astype(BF16),
        "expert_w2": expert_w2[i].astype(BF16), "ln2_g": row(ln2_g), "ln2_b": row(ln2_b),
        "w_ple": w_ple[i].astype(BF16), "w_ple_gate": w_ple_gate[i].astype(BF16), "ple_norm_g": row(ple_norm_g),
    }


def kernel(x_prompt, x_sample, p_prompt, p_sample, state_wkv, state_shift, state_ret, w_in, mu_shift, w_decay_up, decay_base, w_aaa_up, aaa_base, w_gate_up, k_k, k_a, r_k, lnx_g, lnx_b, ret_gn_g, ret_gn_b, w_out, ln1_g, ln1_b, router_coarse_w, router_coarse_b, router_fine_w, router_fine_b, expert_w1, expert_w3, expert_w2, ln2_g, ln2_b, w_ple, w_ple_gate, ple_norm_g):
    yp, ys = x_prompt, x_sample
    nb = x_prompt.shape[0]
    depth = w_in.shape[0]
    wkv_p, shift_p, ret_p, wkv_s, shift_s, ret_s = [], [], [], [], [], []
    for i in range(depth):
        W = _prep_weights(i, w_in, mu_shift, w_decay_up, decay_base, w_aaa_up, aaa_base, w_gate_up, k_k, k_a, r_k,
                          lnx_g, lnx_b, ret_gn_g, ret_gn_b, w_out, ln1_g, ln1_b,
                          router_coarse_w, router_coarse_b, router_fine_w, router_fine_b,
                          expert_w1, expert_w3, expert_w2, ln2_g, ln2_b, w_ple, w_ple_gate, ple_norm_g)
        yp, sp, wp, rp = _layer(yp, p_prompt[i], jnp.zeros((nb, D_MODEL), F32),
                                jnp.zeros((nb, A_HEADS, A_HEAD_DIM, A_HEAD_DIM), F32),
                                jnp.zeros((nb, B_HEADS, B_QK_DIM, B_V_DIM), F32), 0, W)
        ys, ss, wsm, rsm = _layer(ys, p_sample[i], state_shift[i], state_wkv[i], state_ret[i], PAST_LEN, W)
        wkv_p.append(wp); shift_p.append(sp); ret_p.append(rp)
        wkv_s.append(wsm); shift_s.append(ss); ret_s.append(rsm)
    return (yp, ys, jnp.stack(wkv_p, 0), jnp.stack(shift_p, 0), jnp.stack(ret_p, 0),
            jnp.stack(wkv_s, 0), jnp.stack(shift_s, 0), jnp.stack(ret_s, 0))
```

```python
import functools
import math

import numpy as np
import jax
import jax.numpy as jnp
from jax import lax
from jax.experimental import pallas as pl
from jax.experimental.pallas import tpu as pltpu

F32 = jnp.float32
BF16 = jnp.bfloat16

D_MODEL = 1024
D_PLE = 256
A_HEADS = 8
A_HEAD_DIM = 64
A_WIDTH = A_HEADS * A_HEAD_DIM
DECAY_LORA = 64
AAA_LORA = 64
GATE_LORA = 128
GN_EPS_RWKV = 64e-5
B_HEADS = 4
B_QK_DIM = 64
B_V_DIM = 128
B_QK_WIDTH = B_HEADS * B_QK_DIM
B_WIDTH = B_HEADS * B_V_DIM
ROPE_BASE = 10000.0
GN_EPS = 1e-5
SHIFT_WIDTH = 3 * A_WIDTH + DECAY_LORA + AAA_LORA + GATE_LORA
IN_WIDTH = SHIFT_WIDTH + 2 * B_QK_WIDTH + 2 * B_WIDTH
N_GROUPS = 4
EXPERTS_PER_GROUP = 4
N_EXPERTS = N_GROUPS * EXPERTS_PER_GROUP
D_EXPERT = 256
DEPTH = 1
PAST_LEN = 16384
DEEPNORM_ALPHA = (2 * DEPTH) ** 0.25
LN_EPS = 1e-5

V7X_LANES = 128
V7X_SUBLANES = 8
V7X_VMEM_LIMIT_BYTES = 56 * 1024 * 1024

PROJ_ROWS = 512
LOG_DECAY_OUT = 1
CHUNK_ROWS = 64
MIXER_TILES = 4
MIXER_TILES_PACKED = 2
POST_ROWS = 512
SORT_ROWS = 512
SORT_ALIGN = 16
SORT_LOCAL = 640
EXPERT_ROWS = 1024
EXPERT_ROWS_SHORT = 256
RUN_BITS = 6
GROUP_LANE = 0
ROUTER_LANES = V7X_LANES
FINE_LANE0 = N_GROUPS


def _dot(a, b):
    return jnp.dot(a.astype(BF16), b.astype(BF16), preferred_element_type=F32)


def _dot_nt(a, b):
    return lax.dot_general(a.astype(BF16), b.astype(BF16), (((1,), (1,)), ((), ())), preferred_element_type=F32)


def _dot_tn(a, b):
    return lax.dot_general(a.astype(BF16), b.astype(BF16), (((0,), (0,)), ((), ())), preferred_element_type=F32)


def _split2(x):
    hi = x.astype(BF16)
    lo = (x - hi.astype(F32)).astype(BF16)
    return hi, lo


def _split3(x):
    hi = x.astype(BF16)
    r1 = x - hi.astype(F32)
    mid = r1.astype(BF16)
    lo = (r1 - mid.astype(F32)).astype(BF16)
    return hi, mid, lo


def _sigmoid(x):
    return 1.0 / (1.0 + jnp.exp(-x))


def _group_ones(width, group):
    r = lax.broadcasted_iota(jnp.int32, (width, width), 0) // group
    c = lax.broadcasted_iota(jnp.int32, (width, width), 1) // group
    return jnp.where(r == c, 1.0, 0.0).astype(BF16)


def _group_sum(x, ones):
    return jnp.dot(x.astype(BF16), ones, preferred_element_type=F32)


def _proj_kernel(carry_mode, seq_len, tiles_per_seq,
                 x_ref, xp_ref, w_ref, mu_ref, wdec_ref, dbase_ref, waaa_ref, abase_ref, wgate_ref,
                 kk_ref, ka_ref, rk_ref, cos_ref, sin_ref,
                 r_o, lw_o, k_o, v_o, al_o, be_o, g_o, bonus_o, qb_o, kb_o, vb_o, gb_o,
                 carry_scr):
    tm = x_ref.shape[0]
    if carry_mode:
        xp = jnp.broadcast_to(xp_ref[0], (V7X_SUBLANES, D_MODEL))
        xb = jnp.concatenate([x_ref[...], xp], axis=0).astype(BF16)
        j = pl.program_id(0) % tiles_per_seq

        @pl.when(pl.program_id(0) == 0)
        def _():
            carry_scr[...] = jnp.zeros_like(carry_scr)
    else:
        xb = x_ref[...].astype(BF16)
        xpb = xp_ref[...].astype(BF16)

    def project(lo, hi):
        return jnp.dot(xb, w_ref[:, lo:hi], preferred_element_type=F32)

    def shifted(p, lo, hi):
        cur = p[:tm]
        row = lax.broadcasted_iota(jnp.int32, cur.shape, 0)
        rolled = pltpu.roll(cur, 1, 0)
        if carry_mode:
            first = jnp.where(j == 0, p[tm + V7X_SUBLANES - 1:], carry_scr[V7X_SUBLANES - 1:V7X_SUBLANES, lo:hi])
            prev = jnp.where(row == 0, first, rolled)
            carry_scr[:, lo:hi] = cur[tm - V7X_SUBLANES:]
        else:
            first = jnp.dot(xpb, w_ref[:, lo:hi], preferred_element_type=F32)
            prev = jnp.where((row & (seq_len - 1)) == 0, first, rolled)
        return cur + (prev - cur) * mu_ref[:, lo:hi]

    c_r, c_k, c_v, c_l = 0, A_WIDTH, 2 * A_WIDTH, 3 * A_WIDTH
    p_lora = project(c_l, SHIFT_WIDTH)
    p_k = project(c_k, c_v)

    lora = shifted(p_lora, c_l, SHIFT_WIDTH)
    w_lo = lora[:, :DECAY_LORA]
    a_lo = lora[:, DECAY_LORA:DECAY_LORA + AAA_LORA]
    g_lo = lora[:, DECAY_LORA + AAA_LORA:]
    z = -(dbase_ref[...] + _dot(jnp.tanh(w_lo), wdec_ref[...]))
    softplus = jnp.maximum(z, 0.0) + jnp.log(1.0 + jnp.exp(-jnp.abs(z)))
    log_w = -softplus - 0.5
    lw_o[...] = -jnp.exp(log_w)
    a = _sigmoid(abase_ref[...] + _dot(a_lo, waaa_ref[...]))
    g_o[...] = (_dot(_sigmoid(g_lo), wgate_ref[...])).astype(g_o.dtype)

    p_r = project(c_r, c_k)

    ones64 = _group_ones(A_WIDTH, A_HEAD_DIM)
    k0 = shifted(p_k, c_k, c_v)
    kk0 = k0 * kk_ref[...]
    ssq = _group_sum(kk0 * kk0, ones64)
    kk = kk0 * jnp.minimum(lax.rsqrt(ssq), 1e12)
    k = k0 * (1.0 + (a - 1.0) * ka_ref[...])
    k_o[...] = (k).astype(k_o.dtype)
    al_o[...] = (-kk).astype(al_o.dtype)
    be_o[...] = (kk * a).astype(be_o.dtype)

    p_v = project(c_v, c_l)

    r = shifted(p_r, c_r, c_k)
    r_o[...] = (r).astype(r_o.dtype)
    rk_sum = _group_sum(r * k * rk_ref[...], ones64)

    o = SHIFT_WIDTH
    p_qk = project(o, o + 2 * B_QK_WIDTH)[:tm]

    v = shifted(p_v, c_v, c_l)
    v_o[...] = (v).astype(v_o.dtype)
    bonus_o[...] = (rk_sum * v).astype(bonus_o.dtype)

    p_vb = project(o + 2 * B_QK_WIDTH, o + 2 * B_QK_WIDTH + B_WIDTH)[:tm]

    q_b = p_qk[:, :B_QK_WIDTH]
    k_b = p_qk[:, B_QK_WIDTH:]
    lane = lax.broadcasted_iota(jnp.int32, (tm, B_QK_WIDTH), 1)
    first_half = (lane & (B_QK_DIM - 1)) < (B_QK_DIM // 2)
    cos = cos_ref[...]
    sin = sin_ref[...]

    def rot(t):
        swapped = jnp.where(first_half, pltpu.roll(t, B_QK_WIDTH - B_QK_DIM // 2, 1), pltpu.roll(t, B_QK_DIM // 2, 1))
        return t * cos + swapped * sin

    qb_o[...] = (rot(q_b)).astype(qb_o.dtype)
    kb_o[...] = (rot(k_b) * (B_QK_DIM ** -0.5)).astype(kb_o.dtype)

    p_gb = project(o + 2 * B_QK_WIDTH + B_WIDTH, IN_WIDTH)[:tm]
    vb_o[...] = (p_vb).astype(vb_o.dtype)
    gb_o[...] = (p_gb * _sigmoid(p_gb)).astype(gb_o.dtype)


def _proj(x2, x_prev, seq_len, pos0, W):
    n = x2.shape[0]
    tm = PROJ_ROWS
    assert n % tm == 0
    carry_mode = seq_len % tm == 0
    if carry_mode:
        tiles_per_seq = seq_len // tm
        xp = x_prev.reshape(-1, 1, D_MODEL)
        xp_spec = pl.BlockSpec((1, 1, D_MODEL), lambda i: (i // tiles_per_seq, 0, 0))
        tab_rows = seq_len
    else:
        assert tm % seq_len == 0 and seq_len & (seq_len - 1) == 0
        tiles_per_seq = 1
        xp = jnp.repeat(x_prev, seq_len, axis=0)
        xp_spec = pl.BlockSpec((tm, D_MODEL), lambda i: (i, 0))
        tab_rows = tm
    half = B_QK_DIM // 2
    inv = ROPE_BASE ** (-jnp.arange(half, dtype=F32) / half)
    pos = (pos0 + jnp.arange(seq_len, dtype=jnp.int32)).astype(F32)
    ang = pos[:, None] * inv[None, :]
    cos = jnp.tile(jnp.concatenate([jnp.cos(ang), jnp.cos(ang)], -1), (tab_rows // seq_len, B_HEADS))
    sin = jnp.tile(jnp.concatenate([-jnp.sin(ang), jnp.sin(ang)], -1), (tab_rows // seq_len, B_HEADS))
    tab_tiles = tab_rows // tm
    tab_spec = pl.BlockSpec((tm, B_QK_WIDTH), lambda i: (i % tab_tiles, 0))

    def full(a):
        return pl.BlockSpec(a.shape, lambda i: (0,) * a.ndim)

    def rows(width):
        return pl.BlockSpec((tm, width), lambda i: (i, 0))

    params = [W["w_in"], W["mu_shift"], W["w_decay_up"], W["decay_base"], W["w_aaa_up"], W["aaa_base"],
              W["w_gate_up"], W["k_k"], W["k_a"], W["r_k"]]
    widths = [A_WIDTH] * 8 + [B_QK_WIDTH, B_QK_WIDTH, B_WIDTH, B_WIDTH]
    outs = pl.pallas_call(
        functools.partial(_proj_kernel, carry_mode, seq_len, tiles_per_seq),
        grid=(n // tm,),
        in_specs=[rows(D_MODEL), xp_spec, pl.BlockSpec(params[0].shape, lambda i: (0, 0), pipeline_mode=pl.Buffered(1))]
        + [full(a) for a in params[1:]] + [tab_spec, tab_spec],
        out_specs=[rows(w) for w in widths],
        out_shape=[jax.ShapeDtypeStruct((n, w), F32 if i == LOG_DECAY_OUT else BF16) for i, w in enumerate(widths)],
        scratch_shapes=[pltpu.VMEM((V7X_SUBLANES, SHIFT_WIDTH), F32)],
        compiler_params=pltpu.CompilerParams(dimension_semantics=("arbitrary",),
                                             vmem_limit_bytes=V7X_VMEM_LIMIT_BYTES),
        name="proj",
    )(x2, xp, *params, cos, sin)
    return outs


def _mixer_kernel(nb, seqs, clen,
                  r_ref, lw_ref, k_ref, v_ref, al_ref, be_ref, qb_ref, kb_ref, vb_ref, wkv0_ref, ret0_ref,
                  ya_o, ob_o, wkv_o, ret_o, s_scr, r_scr):
    R = seqs * clen
    log2c = int(math.log2(clen))
    c_idx = pl.program_id(1)
    hd = A_HEAD_DIM
    TH = [(t, h) for t in range(nb) for h in range(A_HEADS)]
    TG = [(t, h) for t in range(nb) for h in range(B_HEADS)]

    @pl.when(c_idx == 0)
    def _():
        for t, h in TH:
            blocks = [wkv0_ref[t * seqs + i, h] for i in range(seqs)]
            s_scr[t * A_HEADS + h] = jnp.concatenate(blocks, axis=1) if seqs > 1 else blocks[0]
        for t, h in TG:
            r_scr[t * B_HEADS + h] = ret0_ref[t * seqs:(t + 1) * seqs, h].reshape(seqs * B_QK_DIM, B_V_DIM)

    row = lax.broadcasted_iota(jnp.int32, (R, R), 0)
    col = lax.broadcasted_iota(jnp.int32, (R, R), 1)
    same = (row >> log2c) == (col >> log2c)
    incl = same & (col <= row)
    strict = same & (col < row)
    m_incl = jnp.where(incl, 1.0, 0.0).astype(BF16)
    m_same = jnp.where(same, 1.0, 0.0).astype(BF16)
    eye = jnp.where(row == col, 1.0, 0.0).astype(F32)

    def expand(t):
        if seqs == 1:
            return t
        w = t.shape[1]
        wide = jnp.concatenate([t] * seqs, axis=1)
        rr = lax.broadcasted_iota(jnp.int32, wide.shape, 0) >> log2c
        cc = lax.broadcasted_iota(jnp.int32, wide.shape, 1) // w
        return jnp.where(rr == cc, wide, 0.0)

    def head(x, h):
        return x[:, h * hd:(h + 1) * hd]

    a_bar, r_bar, b_til, k_til, b_dec, k_dec, d_end, vv = [], [], [], [], [], [], [], []
    for t in range(nb):
        lw = lw_ref[t]
        parts = _split3(lw)
        c = sum(jnp.dot(m_incl, p, preferred_element_type=F32) for p in parts)
        cend = sum(jnp.dot(m_same, p, preferred_element_type=F32) for p in parts)
        einv = jnp.exp(-c)
        edec = jnp.exp(cend - c)
        a_bar.append(al_ref[t] * jnp.exp(c - lw))
        r_bar.append(r_ref[t] * jnp.exp(c))
        b_til.append(be_ref[t] * einv)
        k_til.append(k_ref[t] * einv)
        b_dec.append(be_ref[t] * edec)
        k_dec.append(k_ref[t] * edec)
        d_end.append(jnp.exp(cend))
        vv.append(v_ref[t].astype(F32))
    last_row = (lax.broadcasted_iota(jnp.int32, (R, seqs * hd), 0) & (clen - 1)) == clen - 1

    amats = [_dot_nt(jnp.concatenate([head(a_bar[t], h), head(r_bar[t], h)], axis=0),
                     jnp.concatenate([head(b_til[t], h), head(k_til[t], h)], axis=0)) for t, h in TH]

    lgs = [float(np.log1p(-np.exp2(-5.0 - h))) for h in range(B_HEADS)]
    qb = [qb_ref[t].astype(F32) for t in range(nb)]
    kb = [kb_ref[t].astype(F32) for t in range(nb)]
    qs = [qb[t][:, h * B_QK_DIM:(h + 1) * B_QK_DIM] for t, h in TG]
    khs = [kb[t][:, h * B_QK_DIM:(h + 1) * B_QK_DIM] for t, h in TG]
    vbs = [vb_ref[t][:, h * B_V_DIM:(h + 1) * B_V_DIM] for t, h in TG]
    diff = (row - col).astype(F32)
    pos_v = (lax.broadcasted_iota(jnp.int32, (R, B_V_DIM), 0) & (clen - 1)).astype(F32)
    pos_k = (lax.broadcasted_iota(jnp.int32, (R, B_QK_DIM), 0) & (clen - 1)).astype(F32)
    intra = [jnp.where(incl, jnp.exp(lg * diff), 0.0) for lg in lgs]
    cross = [jnp.exp(lg * (pos_v + 1.0)) for lg in lgs]
    kdec = [jnp.exp(lg * (clen - 1.0 - pos_k)) for lg in lgs]
    rstates = [r_scr[t * B_HEADS + h] for t, h in TG]
    scs = [_dot_nt(qs[i], khs[i]) * intra[h] for i, (t, h) in enumerate(TG)]
    qst = [_dot(expand(qs[i]), rstates[i]) * cross[h] for i, (t, h) in enumerate(TG)]

    a_ab = [jnp.where(strict, m[:R, :R], 0.0) for m in amats]
    a_ak = [jnp.where(strict, m[:R, R:], 0.0) for m in amats]
    a_rb = [jnp.where(incl, m[R:, :R], 0.0) for m in amats]
    a_rk = [jnp.where(incl, m[R:, R:], 0.0) for m in amats]
    n = len(TH)
    tinv = [eye + a for a in a_ab]
    vhs = [head(vv[t], h) for t, h in TH]
    av = [_dot(a_ak[i], vhs[i]) for i in range(n)]
    if log2c > 1:
        pw = [_dot(a, a) for a in a_ab]
    for it in range(log2c - 1):
        if it < log2c - 2:
            tp = [_dot(pw[i], jnp.concatenate([tinv[i], pw[i]], axis=1)) for i in range(n)]
            tinv = [tinv[i] + tp[i][:, :R] for i in range(n)]
            pw = [tp[i][:, R:] for i in range(n)]
        else:
            tinv = [tinv[i] + _dot(pw[i], tinv[i]) for i in range(n)]

    os_ = [_dot(scs[i], vbs[i]) + qst[i] for i in range(len(TG))]
    for i, (t, h) in enumerate(TG):
        r_scr[t * B_HEADS + h] = (rstates[i] * float(np.exp(lgs[h] * clen))
                                  + _dot_tn(expand(khs[i] * kdec[h]), vbs[i]))
    for t in range(nb):
        ob_o[t] = jnp.concatenate(os_[t * B_HEADS:(t + 1) * B_HEADS], axis=1)

    wu = [_dot(tinv[i], jnp.concatenate([head(a_bar[t], h), av[i]], axis=1)) for i, (t, h) in enumerate(TH)]
    states = [s_scr[t * A_HEADS + h] for t, h in TH]
    ws = [_dot_nt(jnp.concatenate([expand(wu[i][:, :hd]), expand(head(r_bar[t], h))], axis=0), states[i])
          for i, (t, h) in enumerate(TH)]
    uv = [jnp.concatenate([ws[i][:R] + wu[i][:, hd:], vhs[i]], axis=0) for i in range(n)]
    ys = [ws[i][R:] + _dot(jnp.concatenate([a_rb[i], a_rk[i]], axis=1), uv[i]) for i in range(n)]
    for t in range(nb):
        ya_o[t] = jnp.concatenate(ys[t * A_HEADS:(t + 1) * A_HEADS], axis=1)
    for i, (t, h) in enumerate(TH):
        d_row = jnp.sum(jnp.where(last_row, expand(head(d_end[t], h)), 0.0), axis=0, keepdims=True)
        bk = jnp.concatenate([expand(head(b_dec[t], h)), expand(head(k_dec[t], h))], axis=0)
        s_scr[t * A_HEADS + h] = states[i] * d_row + _dot_tn(uv[i], bk)

    @pl.when(c_idx == pl.num_programs(1) - 1)
    def _():
        for t, h in TH:
            st = s_scr[t * A_HEADS + h]
            for i in range(seqs):
                wkv_o[t * seqs + i, h] = st[:, i * hd:(i + 1) * hd]
        for t, h in TG:
            ret_o[t * seqs:(t + 1) * seqs, h] = r_scr[t * B_HEADS + h].reshape(seqs, B_QK_DIM, B_V_DIM)


def _mixer(ops, wkv0, ret0, n_seq, seq_len):
    n = ops[0].shape[0]
    R = CHUNK_ROWS
    if seq_len >= R:
        assert seq_len % R == 0
        seqs, clen, nchunks = 1, R, seq_len // R
    else:
        assert R % seq_len == 0 and seq_len & (seq_len - 1) == 0 and n_seq % (R // seq_len) == 0
        seqs, clen, nchunks = R // seq_len, seq_len, 1
    ntiles = n_seq // seqs
    nb = MIXER_TILES if seqs == 1 else MIXER_TILES_PACKED
    assert ntiles % nb == 0
    ops3 = [a.reshape(ntiles, nchunks * R, a.shape[1]) for a in ops]

    def rows(width):
        return pl.BlockSpec((nb, R, width), lambda i, c: (i, c, 0))

    wkv_spec = pl.BlockSpec((nb * seqs, A_HEADS, A_HEAD_DIM, A_HEAD_DIM), lambda i, c: (i, 0, 0, 0))
    ret_spec = pl.BlockSpec((nb * seqs, B_HEADS, B_QK_DIM, B_V_DIM), lambda i, c: (i, 0, 0, 0))
    ya, ob, wkv1, ret1 = pl.pallas_call(
        functools.partial(_mixer_kernel, nb, seqs, clen),
        grid=(ntiles // nb, nchunks),
        in_specs=[rows(A_WIDTH)] * 6 + [rows(B_QK_WIDTH), rows(B_QK_WIDTH), rows(B_WIDTH), wkv_spec, ret_spec],
        out_specs=[rows(A_WIDTH), rows(B_WIDTH), wkv_spec, ret_spec],
        out_shape=[jax.ShapeDtypeStruct((ntiles, nchunks * R, A_WIDTH), F32),
                   jax.ShapeDtypeStruct((ntiles, nchunks * R, B_WIDTH), F32),
                   jax.ShapeDtypeStruct(wkv0.shape, F32), jax.ShapeDtypeStruct(ret0.shape, F32)],
        scratch_shapes=[pltpu.VMEM((nb * A_HEADS, A_HEAD_DIM, seqs * A_HEAD_DIM), F32),
                        pltpu.VMEM((nb * B_HEADS, seqs * B_QK_DIM, B_V_DIM), F32)],
        compiler_params=pltpu.CompilerParams(dimension_semantics=("parallel", "arbitrary"),
                                             vmem_limit_bytes=V7X_VMEM_LIMIT_BYTES),
        name="mixer",
    )(*ops3, wkv0, ret0)
    return ya.reshape(n, A_WIDTH), ob.reshape(n, B_WIDTH), wkv1, ret1


def _layer_norm(z, g, b):
    mu = jnp.mean(z, axis=-1, keepdims=True)
    d = z - mu
    var = jnp.mean(d * d, axis=-1, keepdims=True)
    return d * lax.rsqrt(var + LN_EPS) * g + b


def _post_kernel(ya_ref, ob_ref, bonus_ref, g_ref, gb_ref, x_ref, lnxg_ref, lnxb_ref, rgg_ref, rgb_ref,
                 wout_ref, ln1g_ref, ln1b_ref, wr_ref, br_ref, h_o, gate_o, cnt_o):
    tm = x_ref.shape[0]

    def head_norm(t, group, eps, gg, bb):
        ones = _group_ones(t.shape[1], group)
        mu = _group_sum(t, ones) * (1.0 / group)
        d = t - mu
        var = _group_sum(d * d, ones) * (1.0 / group)
        return d * lax.rsqrt(var + eps) * gg + bb

    y_a = (head_norm(ya_ref[...], A_HEAD_DIM, GN_EPS_RWKV, lnxg_ref[...], lnxb_ref[...]) + bonus_ref[...]) * g_ref[...]
    y_b = head_norm(ob_ref[...], B_V_DIM, GN_EPS, rgg_ref[...], rgb_ref[...]) * gb_ref[...]
    y = jnp.concatenate([y_a, y_b], axis=1)
    mix = _dot(y, wout_ref[...])
    h = _layer_norm(DEEPNORM_ALPHA * x_ref[...] + mix, ln1g_ref[...], ln1b_ref[...])
    h_o[...] = h

    h_hi, h_lo = _split2(h)
    w_hi, w_lo = _split2(wr_ref[...])
    logits = (jnp.dot(h_hi, w_hi, preferred_element_type=F32) + jnp.dot(h_hi, w_lo, preferred_element_type=F32)
              + jnp.dot(h_lo, w_hi, preferred_element_type=F32)) + br_ref[...]
    lane = lax.broadcasted_iota(jnp.int32, (tm, ROUTER_LANES), 1)
    neg = -jnp.inf
    big = ROUTER_LANES
    cl = jnp.where(lane < N_GROUPS, logits, neg)
    cmax = jnp.max(cl, axis=-1, keepdims=True)
    grp = jnp.min(jnp.where(cl == cmax, lane, big), axis=-1, keepdims=True)
    gprob = 1.0 / jnp.sum(jnp.exp(cl - cmax), axis=-1, keepdims=True)
    lo_lane = FINE_LANE0 + grp * EXPERTS_PER_GROUP
    fv = jnp.where((lane >= lo_lane) & (lane < lo_lane + EXPERTS_PER_GROUP), logits, neg)
    m1 = jnp.max(fv, axis=-1, keepdims=True)
    i1 = jnp.min(jnp.where(fv == m1, lane, big), axis=-1, keepdims=True)
    fv2 = jnp.where(lane == i1, neg, fv)
    m2 = jnp.max(fv2, axis=-1, keepdims=True)
    i2 = jnp.min(jnp.where(fv2 == m2, lane, big), axis=-1, keepdims=True)
    e2 = jnp.exp(m2 - m1)
    w1 = gprob / (1.0 + e2)
    w2 = gprob * e2 / (1.0 + e2)
    gate_o[...] = (jnp.where(lane == i1, w1, 0.0) + jnp.where(lane == i2, w2, 0.0)
                   + jnp.where(lane == GROUP_LANE, grp.astype(F32), 0.0))
    onehot = jnp.where((lane == grp) & (lane < N_GROUPS), 1.0, 0.0)
    cnt_o[0] = jnp.broadcast_to(jnp.sum(onehot, axis=0, keepdims=True), (V7X_SUBLANES, ROUTER_LANES)).astype(jnp.int32)


def _post(ya, ob, bonus, g, gb, x2, W):
    n = x2.shape[0]
    tm = POST_ROWS
    assert n % tm == 0

    def full(a):
        return pl.BlockSpec(a.shape, lambda i: (0,) * a.ndim)

    def rows(width):
        return pl.BlockSpec((tm, width), lambda i: (i, 0))

    params = [W["lnx_g"], W["lnx_b"], W["ret_gn_g"], W["ret_gn_b"], W["w_out"], W["ln1_g"], W["ln1_b"],
              W["w_router"], W["b_router"]]
    return pl.pallas_call(
        _post_kernel,
        grid=(n // tm,),
        in_specs=[rows(A_WIDTH), rows(B_WIDTH), rows(A_WIDTH), rows(A_WIDTH), rows(B_WIDTH), rows(D_MODEL)]
        + [full(a) for a in params],
        out_specs=[rows(D_MODEL), rows(ROUTER_LANES),
                   pl.BlockSpec((1, V7X_SUBLANES, ROUTER_LANES), lambda i: (i, 0, 0))],
        out_shape=[jax.ShapeDtypeStruct((n, D_MODEL), F32), jax.ShapeDtypeStruct((n, ROUTER_LANES), F32),
                   jax.ShapeDtypeStruct((n // tm, V7X_SUBLANES, ROUTER_LANES), jnp.int32)],
        compiler_params=pltpu.CompilerParams(dimension_semantics=("parallel",),
                                             vmem_limit_bytes=V7X_VMEM_LIMIT_BYTES),
        name="post",
    )(ya, ob, bonus, g, gb, x2, *params)


def _sort_positions(gate, lofs):
    tm = gate.shape[0]
    lane = lax.broadcasted_iota(jnp.int32, (tm, ROUTER_LANES), 1)
    grp = gate[:, GROUP_LANE:GROUP_LANE + 1].astype(jnp.int32)
    onehot = jnp.where((lane == grp) & (lane < N_GROUPS), 1.0, 0.0)
    r = lax.broadcasted_iota(jnp.int32, (tm, tm), 0)
    c = lax.broadcasted_iota(jnp.int32, (tm, tm), 1)
    earlier = jnp.where(c < r, 1.0, 0.0).astype(BF16)
    prefix = jnp.dot(earlier, onehot.astype(BF16), preferred_element_type=F32)
    base = jnp.zeros((tm, ROUTER_LANES), F32)
    for g in range(N_GROUPS):
        base = jnp.where(lane == g, lofs[g].astype(F32), base)
    return jnp.sum(onehot * (base + prefix), axis=1, keepdims=True).astype(jnp.int32)


def _run_copies(action, plan_ref, i, src_of, dst_of, sem):
    for g in range(N_GROUPS):
        goff = plan_ref[i, g]
        k = plan_ref[i, N_GROUPS + g] // SORT_ALIGN
        lofs = plan_ref[i, 2 * N_GROUPS + g]
        for b in reversed(range(RUN_BITS)):
            size = SORT_ALIGN << b

            @pl.when(((k >> b) & 1) == 1)
            def _():
                done = ((k >> (b + 1)) << (b + 1)) * SORT_ALIGN
                lo = pl.multiple_of(lofs + done, SORT_ALIGN)
                go = pl.multiple_of(goff + done, SORT_ALIGN)
                cp = pltpu.make_async_copy(src_of(lo, go, size), dst_of(lo, go, size), sem)
                cp.start() if action == "start" else cp.wait()


def _dispatch_kernel(plan_ref, h_ref, gate_ref, hs_in, gs_in, hs_o, gs_o, h_loc, g_loc, sem):
    del hs_in, gs_in
    i = pl.program_id(0)
    tm = h_ref.shape[0]
    lofs = [plan_ref[i, 2 * N_GROUPS + g] for g in range(N_GROUPS)]
    pos = _sort_positions(gate_ref[...], lofs)
    onehot_t = jnp.where(lax.broadcasted_iota(jnp.int32, (tm, SORT_LOCAL), 1) == pos, 1.0, 0.0).astype(BF16)
    idx = lax.broadcasted_iota(jnp.int32, (V7X_SUBLANES, SORT_LOCAL), 1)
    pos_row = (_dot_nt((idx // V7X_LANES).astype(F32), onehot_t) * float(V7X_LANES)
               + _dot_nt((idx % V7X_LANES).astype(F32), onehot_t))[0:1].astype(jnp.int32)
    perm = jnp.where(lax.broadcasted_iota(jnp.int32, (SORT_LOCAL, tm), 0) == pos_row, 1.0, 0.0).astype(BF16)
    h_sorted = jnp.dot(perm, h_ref[...].astype(BF16), preferred_element_type=F32).astype(BF16)
    g_sorted = sum(jnp.dot(perm, p, preferred_element_type=F32) for p in _split3(gate_ref[...]))

    def copies(action, tile):
        for src, dst in ((h_loc, hs_o), (g_loc, gs_o)):
            _run_copies(action, plan_ref, tile, lambda lo, go, sz: src.at[pl.ds(lo, sz)],
                        lambda lo, go, sz: dst.at[pl.ds(go, sz)], sem)

    @pl.when(i > 0)
    def _():
        copies("wait", i - 1)

    h_loc[...] = h_sorted
    g_loc[...] = g_sorted
    copies("start", i)

    @pl.when(i == pl.num_programs(0) - 1)
    def _():
        copies("wait", i)


def _experts_kernel(tile_group_ref, n_valid_ref, hs_ref, gs_ref, w1_ref, w3_ref, w2_ref, ys_o):
    j = pl.program_id(0)

    @pl.when(j < n_valid_ref[0])
    def _():
        x = hs_ref[...]
        gs = gs_ref[...]
        lane = lax.broadcasted_iota(jnp.int32, gs.shape, 1)
        g = tile_group_ref[j]
        acc = jnp.zeros(ys_o.shape, F32)
        for e in range(EXPERTS_PER_GROUP):
            ge = jnp.sum(jnp.where(lane == FINE_LANE0 + g * EXPERTS_PER_GROUP + e, gs, 0.0), axis=-1, keepdims=True)
            a = jnp.dot(x, w1_ref[e], preferred_element_type=F32)
            b = jnp.dot(x, w3_ref[e], preferred_element_type=F32)
            hid = (a * _sigmoid(a)) * b * ge
            acc = acc + jnp.dot(hid.astype(BF16), w2_ref[e], preferred_element_type=F32)
        ys_o[...] = acc.astype(ys_o.dtype)

    @pl.when(j >= n_valid_ref[0])
    def _():
        ys_o[...] = jnp.zeros_like(ys_o)


def _combine_kernel(plan_ref, h_ref, gate_ref, p_ref, ys_ref, ln2g_ref, ln2b_ref, wple_ref, wpg_ref, pleg_ref,
                    o_ref, y_loc, sem):
    i = pl.program_id(0)
    tm = h_ref.shape[0]
    lofs = [plan_ref[i, 2 * N_GROUPS + g] for g in range(N_GROUPS)]

    def fetch(tile):
        slot = tile % 2
        y_loc[slot] = jnp.zeros(y_loc.shape[1:], y_loc.dtype)
        _run_copies("start", plan_ref, tile, lambda lo, go, sz: ys_ref.at[pl.ds(go, sz)],
                    lambda lo, go, sz: y_loc.at[slot, pl.ds(lo, sz)], sem.at[slot])

    @pl.when(i == 0)
    def _():
        fetch(i)

    @pl.when(i + 1 < pl.num_programs(0))
    def _():
        fetch(i + 1)

    pos = _sort_positions(gate_ref[...], lofs)
    onehot_t = jnp.where(lax.broadcasted_iota(jnp.int32, (tm, SORT_LOCAL), 1) == pos, 1.0, 0.0).astype(BF16)
    ple_in = _dot(p_ref[...], wple_ref[...])
    slot = i % 2
    _run_copies("wait", plan_ref, i, lambda lo, go, sz: ys_ref.at[pl.ds(go, sz)],
                lambda lo, go, sz: y_loc.at[slot, pl.ds(lo, sz)], sem.at[slot])
    ffn = jnp.dot(onehot_t, y_loc[slot], preferred_element_type=F32)
    h2 = _layer_norm(DEEPNORM_ALPHA * h_ref[...] + ffn, ln2g_ref[...], ln2b_ref[...])
    ple = ple_in * _sigmoid(_dot(h2, wpg_ref[...]))
    ms = jnp.mean(ple * ple, axis=-1, keepdims=True)
    o_ref[...] = h2 + ple * lax.rsqrt(ms + LN_EPS) * pleg_ref[...]


def _ffn(h, gate, counts, p2, W):
    n = h.shape[0]
    tm = SORT_ROWS
    assert n % tm == 0 and tm == POST_ROWS and SORT_LOCAL >= tm + N_GROUPS * SORT_ALIGN
    ntiles = n // tm
    er = EXPERT_ROWS if n >= 2 * N_GROUPS * EXPERT_ROWS else EXPERT_ROWS_SHORT
    cnt = counts[:, 0, :N_GROUPS]
    run = (cnt + SORT_ALIGN - 1) // SORT_ALIGN * SORT_ALIGN
    lofs = jnp.cumsum(run, axis=1) - run
    seg = (jnp.sum(run, axis=0) + er - 1) // er * er
    gbase = jnp.cumsum(seg) - seg
    goff = gbase[None, :] + jnp.cumsum(run, axis=0) - run
    plan = jnp.concatenate([goff, run, lofs], axis=1).astype(jnp.int32)
    max_tiles = (n + ntiles * N_GROUPS * (SORT_ALIGN - 1)) // er + N_GROUPS
    cap = max_tiles * er
    n_valid = (jnp.sum(seg) // er).astype(jnp.int32).reshape(1)
    tile_start = jnp.arange(max_tiles, dtype=jnp.int32) * er
    tile_group = jnp.clip(jnp.sum(tile_start[:, None] >= (gbase + seg)[None, :], axis=1), 0, N_GROUPS - 1).astype(jnp.int32)

    cparams = dict(vmem_limit_bytes=V7X_VMEM_LIMIT_BYTES)
    any_spec = pl.BlockSpec(memory_space=pl.ANY)
    hs, gs = pl.pallas_call(
        _dispatch_kernel,
        grid_spec=pltpu.PrefetchScalarGridSpec(
            num_scalar_prefetch=1, grid=(ntiles,),
            in_specs=[pl.BlockSpec((tm, D_MODEL), lambda i, plan: (i, 0)),
                      pl.BlockSpec((tm, ROUTER_LANES), lambda i, plan: (i, 0)), any_spec, any_spec],
            out_specs=[any_spec, any_spec],
            scratch_shapes=[pltpu.VMEM((SORT_LOCAL, D_MODEL), BF16), pltpu.VMEM((SORT_LOCAL, ROUTER_LANES), F32),
                            pltpu.SemaphoreType.DMA(())]),
        out_shape=[jax.ShapeDtypeStruct((cap, D_MODEL), BF16), jax.ShapeDtypeStruct((cap, ROUTER_LANES), F32)],
        input_output_aliases={3: 0, 4: 1},
        compiler_params=pltpu.CompilerParams(dimension_semantics=("arbitrary",), **cparams),
        name="dispatch",
    )(plan, h, gate, jnp.zeros((cap, D_MODEL), BF16), jnp.zeros((cap, ROUTER_LANES), F32))

    def tile_rows(width):
        return pl.BlockSpec((er, width), lambda j, tg, nv: (jnp.minimum(j, nv[0] - 1), 0))

    def group_w(shape):
        return pl.BlockSpec((EXPERTS_PER_GROUP,) + shape, lambda j, tg, nv: (tg[jnp.minimum(j, nv[0] - 1)], 0, 0))

    ys = pl.pallas_call(
        _experts_kernel,
        grid_spec=pltpu.PrefetchScalarGridSpec(
            num_scalar_prefetch=2, grid=(max_tiles,),
            in_specs=[tile_rows(D_MODEL), tile_rows(ROUTER_LANES), group_w((D_MODEL, D_EXPERT)),
                      group_w((D_MODEL, D_EXPERT)), group_w((D_EXPERT, D_MODEL))],
            out_specs=pl.BlockSpec((er, D_MODEL), lambda j, tg, nv: (j, 0))),
        out_shape=jax.ShapeDtypeStruct((cap, D_MODEL), BF16),
        compiler_params=pltpu.CompilerParams(dimension_semantics=("arbitrary",), **cparams),
        name="experts",
    )(tile_group, n_valid, hs, gs, W["expert_w1"], W["expert_w3"], W["expert_w2"])

    def full(a):
        return pl.BlockSpec(a.shape, lambda i, plan: (0,) * a.ndim)

    params = [W["ln2_g"], W["ln2_b"], W["w_ple"], W["w_ple_gate"], W["ple_norm_g"]]
    return pl.pallas_call(
        _combine_kernel,
        grid_spec=pltpu.PrefetchScalarGridSpec(
            num_scalar_prefetch=1, grid=(ntiles,),
            in_specs=[pl.BlockSpec((tm, D_MODEL), lambda i, plan: (i, 0)),
                      pl.BlockSpec((tm, ROUTER_LANES), lambda i, plan: (i, 0)),
                      pl.BlockSpec((tm, D_PLE), lambda i, plan: (i, 0)), any_spec] + [full(a) for a in params],
            out_specs=pl.BlockSpec((tm, D_MODEL), lambda i, plan: (i, 0)),
            scratch_shapes=[pltpu.VMEM((2, SORT_LOCAL, D_MODEL), BF16), pltpu.SemaphoreType.DMA((2,))]),
        out_shape=jax.ShapeDtypeStruct((n, D_MODEL), F32),
        compiler_params=pltpu.CompilerParams(dimension_semantics=("arbitrary",), **cparams),
        name="combine",
    )(plan, h, gate, p2, ys, *params)


def _layer(x, p, x_prev, wkv0, ret0, pos0, W):
    n_seq, seq_len, _ = x.shape
    n = n_seq * seq_len
    x2 = x.reshape(n, D_MODEL)
    r, lw, k, v, al, be, g, bonus, qb, kb, vb, gb = _proj(x2, x_prev, seq_len, pos0, W)
    ya, ob, wkv1, ret1 = _mixer((r, lw, k, v, al, be, qb, kb, vb), wkv0, ret0, n_seq, seq_len)
    h, gate, counts = _post(ya, ob, bonus, g, gb, x2, W)
    out = _ffn(h, gate, counts, p.reshape(n, D_PLE), W)
    return out.reshape(n_seq, seq_len, D_MODEL), x[:, -1], wkv1, ret1


def _prep_weights(i, w_in, mu_shift, w_decay_up, decay_base, w_aaa_up, aaa_base, w_gate_up, k_k, k_a, r_k,
                  lnx_g, lnx_b, ret_gn_g, ret_gn_b, w_out, ln1_g, ln1_b,
                  router_coarse_w, router_coarse_b, router_fine_w, router_fine_b,
                  expert_w1, expert_w3, expert_w2, ln2_g, ln2_b, w_ple, w_ple_gate, ple_norm_g):
    def row(a):
        return a[i].reshape(1, -1).astype(F32)

    pad = ROUTER_LANES - N_GROUPS - N_EXPERTS
    w_router = jnp.concatenate([router_coarse_w[i], router_fine_w[i], jnp.zeros((D_MODEL, pad), F32)], axis=1)
    b_router = jnp.concatenate([router_coarse_b[i], router_fine_b[i], jnp.zeros((pad,), F32)]).reshape(1, -1)
    return {
        "w_in": w_in[i].astype(BF16), "mu_shift": row(mu_shift), "w_decay_up": w_decay_up[i].astype(BF16),
        "decay_base": row(decay_base), "w_aaa_up": w_aaa_up[i].astype(BF16), "aaa_base": row(aaa_base),
        "w_gate_up": w_gate_up[i].astype(BF16), "k_k": row(k_k), "k_a": row(k_a), "r_k": row(r_k),
        "lnx_g": row(lnx_g), "lnx_b": row(lnx_b), "ret_gn_g": row(ret_gn_g), "ret_gn_b": row(ret_gn_b),
        "w_out": w_out[i].astype(BF16), "ln1_g": row(ln1_g), "ln1_b": row(ln1_b),
        "w_router": w_router, "b_router": b_router,
        "expert_w1": expert_w1[i].astype(BF16), "expert_w3": expert_w3[i].astype(BF16),
        "expert_w2": expert_w2[i].astype(BF16), "ln2_g": row(ln2_g), "ln2_b": row(ln2_b),
        "w_ple": w_ple[i].astype(BF16), "w_ple_gate": w_ple_gate[i].astype(BF16), "ple_norm_g": row(ple_norm_g),
    }


def kernel(x_prompt, x_sample, p_prompt, p_sample, state_wkv, state_shift, state_ret, w_in, mu_shift, w_decay_up, decay_base, w_aaa_up, aaa_base, w_gate_up, k_k, k_a, r_k, lnx_g, lnx_b, ret_gn_g, ret_gn_b, w_out, ln1_g, ln1_b, router_coarse_w, router_coarse_b, router_fine_w, router_fine_b, expert_w1, expert_w3, expert_w2, ln2_g, ln2_b, w_ple, w_ple_gate, ple_norm_g):
    yp, ys = x_prompt, x_sample
    nb = x_prompt.shape[0]
    depth = w_in.shape[0]
    wkv_p, shift_p, ret_p, wkv_s, shift_s, ret_s = [], [], [], [], [], []
    for i in range(depth):
        W = _prep_weights(i, w_in, mu_shift, w_decay_up, decay_base, w_aaa_up, aaa_base, w_gate_up, k_k, k_a, r_k,
                          lnx_g, lnx_b, ret_gn_g, ret_gn_b, w_out, ln1_g, ln1_b,
                          router_coarse_w, router_coarse_b, router_fine_w, router_fine_b,
                          expert_w1, expert_w3, expert_w2, ln2_g, ln2_b, w_ple, w_ple_gate, ple_norm_g)
        yp, sp, wp, rp = _layer(yp, p_prompt[i], jnp.zeros((nb, D_MODEL), F32),
                                jnp.zeros((nb, A_HEADS, A_HEAD_DIM, A_HEAD_DIM), F32),
                                jnp.zeros((nb, B_HEADS, B_QK_DIM, B_V_DIM), F32), 0, W)
        ys, ss, wsm, rsm = _layer(ys, p_sample[i], state_shift[i], state_wkv[i], state_ret[i], PAST_LEN, W)
        wkv_p.append(wp); shift_p.append(sp); ret_p.append(rp)
        wkv_s.append(wsm); shift_s.append(ss); ret_s.append(rsm)
    return (yp, ys, jnp.stack(wkv_p, 0), jnp.stack(shift_p, 0), jnp.stack(ret_p, 0),
            jnp.stack(wkv_s, 0), jnp.stack(shift_s, 0), jnp.stack(ret_s, 0))
```

```python
import functools
import math

import numpy as np
import jax
import jax.numpy as jnp
from jax import lax
from jax.experimental import pallas as pl
from jax.experimental.pallas import tpu as pltpu

F32 = jnp.float32
BF16 = jnp.bfloat16

D_MODEL = 1024
D_PLE = 256
A_HEADS = 8
A_HEAD_DIM = 64
A_WIDTH = A_HEADS * A_HEAD_DIM
DECAY_LORA = 64
AAA_LORA = 64
GATE_LORA = 128
GN_EPS_RWKV = 64e-5
B_HEADS = 4
B_QK_DIM = 64
B_V_DIM = 128
B_QK_WIDTH = B_HEADS * B_QK_DIM
B_WIDTH = B_HEADS * B_V_DIM
ROPE_BASE = 10000.0
GN_EPS = 1e-5
SHIFT_WIDTH = 3 * A_WIDTH + DECAY_LORA + AAA_LORA + GATE_LORA
IN_WIDTH = SHIFT_WIDTH + 2 * B_QK_WIDTH + 2 * B_WIDTH
N_GROUPS = 4
EXPERTS_PER_GROUP = 4
N_EXPERTS = N_GROUPS * EXPERTS_PER_GROUP
D_EXPERT = 256
DEPTH = 1
PAST_LEN = 16384
DEEPNORM_ALPHA = (2 * DEPTH) ** 0.25
LN_EPS = 1e-5

V7X_LANES = 128
V7X_SUBLANES = 8
V7X_VMEM_LIMIT_BYTES = 56 * 1024 * 1024

PROJ_ROWS = 512
LOG_DECAY_OUT = 1
CHUNK_ROWS = 64
MIXER_TILES = 4
MIXER_TILES_PACKED = 2
POST_ROWS = 512
SORT_ROWS = 512
SORT_ALIGN = 16
SORT_LOCAL = 640
EXPERT_ROWS = 1024
EXPERT_ROWS_SHORT = 256
GAP_BITS = 6
RUN_BITS = 6
GROUP_LANE = 0
ROUTER_LANES = V7X_LANES
FINE_LANE0 = N_GROUPS


def _dot(a, b):
    return jnp.dot(a.astype(BF16), b.astype(BF16), preferred_element_type=F32)


def _dot_nt(a, b):
    return lax.dot_general(a.astype(BF16), b.astype(BF16), (((1,), (1,)), ((), ())), preferred_element_type=F32)


def _dot_tn(a, b):
    return lax.dot_general(a.astype(BF16), b.astype(BF16), (((0,), (0,)), ((), ())), preferred_element_type=F32)


def _split2(x):
    hi = x.astype(BF16)
    lo = (x - hi.astype(F32)).astype(BF16)
    return hi, lo


def _split3(x):
    hi = x.astype(BF16)
    r1 = x - hi.astype(F32)
    mid = r1.astype(BF16)
    lo = (r1 - mid.astype(F32)).astype(BF16)
    return hi, mid, lo


def _sigmoid(x):
    return 1.0 / (1.0 + jnp.exp(-x))


def _group_ones(width, group):
    r = lax.broadcasted_iota(jnp.int32, (width, width), 0) // group
    c = lax.broadcasted_iota(jnp.int32, (width, width), 1) // group
    return jnp.where(r == c, 1.0, 0.0).astype(BF16)


def _group_sum(x, ones):
    return jnp.dot(x.astype(BF16), ones, preferred_element_type=F32)


def _proj_kernel(carry_mode, seq_len, tiles_per_seq,
                 x_ref, xp_ref, w_ref, mu_ref, wdec_ref, dbase_ref, waaa_ref, abase_ref, wgate_ref,
                 kk_ref, ka_ref, rk_ref, cos_ref, sin_ref,
                 r_o, lw_o, k_o, v_o, al_o, be_o, g_o, bonus_o, qb_o, kb_o, vb_o, gb_o,
                 carry_scr):
    tm = x_ref.shape[0]
    if carry_mode:
        xp = jnp.broadcast_to(xp_ref[0], (V7X_SUBLANES, D_MODEL))
        xb = jnp.concatenate([x_ref[...], xp], axis=0).astype(BF16)
        j = pl.program_id(0) % tiles_per_seq

        @pl.when(pl.program_id(0) == 0)
        def _():
            carry_scr[...] = jnp.zeros_like(carry_scr)
    else:
        xb = x_ref[...].astype(BF16)
        xpb = xp_ref[...].astype(BF16)

    def project(lo, hi):
        return jnp.dot(xb, w_ref[:, lo:hi], preferred_element_type=F32)

    def shifted(p, lo, hi):
        cur = p[:tm]
        row = lax.broadcasted_iota(jnp.int32, cur.shape, 0)
        rolled = pltpu.roll(cur, 1, 0)
        if carry_mode:
            first = jnp.where(j == 0, p[tm + V7X_SUBLANES - 1:], carry_scr[V7X_SUBLANES - 1:V7X_SUBLANES, lo:hi])
            prev = jnp.where(row == 0, first, rolled)
            carry_scr[:, lo:hi] = cur[tm - V7X_SUBLANES:]
        else:
            first = jnp.dot(xpb, w_ref[:, lo:hi], preferred_element_type=F32)
            prev = jnp.where((row & (seq_len - 1)) == 0, first, rolled)
        return cur + (prev - cur) * mu_ref[:, lo:hi]

    c_r, c_k, c_v, c_l = 0, A_WIDTH, 2 * A_WIDTH, 3 * A_WIDTH
    p_lora = project(c_l, SHIFT_WIDTH)
    p_k = project(c_k, c_v)

    lora = shifted(p_lora, c_l, SHIFT_WIDTH)
    w_lo = lora[:, :DECAY_LORA]
    a_lo = lora[:, DECAY_LORA:DECAY_LORA + AAA_LORA]
    g_lo = lora[:, DECAY_LORA + AAA_LORA:]
    z = -(dbase_ref[...] + _dot(jnp.tanh(w_lo), wdec_ref[...]))
    softplus = jnp.maximum(z, 0.0) + jnp.log(1.0 + jnp.exp(-jnp.abs(z)))
    log_w = -softplus - 0.5
    lw_o[...] = -jnp.exp(log_w)
    a = _sigmoid(abase_ref[...] + _dot(a_lo, waaa_ref[...]))
    g_o[...] = (_dot(_sigmoid(g_lo), wgate_ref[...])).astype(g_o.dtype)

    p_r = project(c_r, c_k)

    ones64 = _group_ones(A_WIDTH, A_HEAD_DIM)
    k0 = shifted(p_k, c_k, c_v)
    kk0 = k0 * kk_ref[...]
    ssq = _group_sum(kk0 * kk0, ones64)
    kk = kk0 * jnp.minimum(lax.rsqrt(ssq), 1e12)
    k = k0 * (1.0 + (a - 1.0) * ka_ref[...])
    k_o[...] = (k).astype(k_o.dtype)
    al_o[...] = (-kk).astype(al_o.dtype)
    be_o[...] = (kk * a).astype(be_o.dtype)

    p_v = project(c_v, c_l)

    r = shifted(p_r, c_r, c_k)
    r_o[...] = (r).astype(r_o.dtype)
    rk_sum = _group_sum(r * k * rk_ref[...], ones64)

    o = SHIFT_WIDTH
    p_qk = project(o, o + 2 * B_QK_WIDTH)[:tm]

    v = shifted(p_v, c_v, c_l)
    v_o[...] = (v).astype(v_o.dtype)
    bonus_o[...] = (rk_sum * v).astype(bonus_o.dtype)

    p_vb = project(o + 2 * B_QK_WIDTH, o + 2 * B_QK_WIDTH + B_WIDTH)[:tm]

    q_b = p_qk[:, :B_QK_WIDTH]
    k_b = p_qk[:, B_QK_WIDTH:]
    lane = lax.broadcasted_iota(jnp.int32, (tm, B_QK_WIDTH), 1)
    first_half = (lane & (B_QK_DIM - 1)) < (B_QK_DIM // 2)
    cos = cos_ref[...]
    sin = sin_ref[...]

    def rot(t):
        swapped = jnp.where(first_half, pltpu.roll(t, B_QK_WIDTH - B_QK_DIM // 2, 1), pltpu.roll(t, B_QK_DIM // 2, 1))
        return t * cos + swapped * sin

    qb_o[...] = (rot(q_b)).astype(qb_o.dtype)
    kb_o[...] = (rot(k_b) * (B_QK_DIM ** -0.5)).astype(kb_o.dtype)

    p_gb = project(o + 2 * B_QK_WIDTH + B_WIDTH, IN_WIDTH)[:tm]
    vb_o[...] = (p_vb).astype(vb_o.dtype)
    gb_o[...] = (p_gb * _sigmoid(p_gb)).astype(gb_o.dtype)


def _proj(x2, x_prev, seq_len, pos0, W):
    n = x2.shape[0]
    tm = PROJ_ROWS
    assert n % tm == 0
    carry_mode = seq_len % tm == 0
    if carry_mode:
        tiles_per_seq = seq_len // tm
        xp = x_prev.reshape(-1, 1, D_MODEL)
        xp_spec = pl.BlockSpec((1, 1, D_MODEL), lambda i: (i // tiles_per_seq, 0, 0))
        tab_rows = seq_len
    else:
        assert tm % seq_len == 0 and seq_len & (seq_len - 1) == 0
        tiles_per_seq = 1
        xp = jnp.repeat(x_prev, seq_len, axis=0)
        xp_spec = pl.BlockSpec((tm, D_MODEL), lambda i: (i, 0))
        tab_rows = tm
    half = B_QK_DIM // 2
    inv = ROPE_BASE ** (-jnp.arange(half, dtype=F32) / half)
    pos = (pos0 + jnp.arange(seq_len, dtype=jnp.int32)).astype(F32)
    ang = pos[:, None] * inv[None, :]
    cos = jnp.tile(jnp.concatenate([jnp.cos(ang), jnp.cos(ang)], -1), (tab_rows // seq_len, B_HEADS))
    sin = jnp.tile(jnp.concatenate([-jnp.sin(ang), jnp.sin(ang)], -1), (tab_rows // seq_len, B_HEADS))
    tab_tiles = tab_rows // tm
    tab_spec = pl.BlockSpec((tm, B_QK_WIDTH), lambda i: (i % tab_tiles, 0))

    def full(a):
        return pl.BlockSpec(a.shape, lambda i: (0,) * a.ndim)

    def rows(width):
        return pl.BlockSpec((tm, width), lambda i: (i, 0))

    params = [W["w_in"], W["mu_shift"], W["w_decay_up"], W["decay_base"], W["w_aaa_up"], W["aaa_base"],
              W["w_gate_up"], W["k_k"], W["k_a"], W["r_k"]]
    widths = [A_WIDTH] * 8 + [B_QK_WIDTH, B_QK_WIDTH, B_WIDTH, B_WIDTH]
    outs = pl.pallas_call(
        functools.partial(_proj_kernel, carry_mode, seq_len, tiles_per_seq),
        grid=(n // tm,),
        in_specs=[rows(D_MODEL), xp_spec, pl.BlockSpec(params[0].shape, lambda i: (0, 0), pipeline_mode=pl.Buffered(1))]
        + [full(a) for a in params[1:]] + [tab_spec, tab_spec],
        out_specs=[rows(w) for w in widths],
        out_shape=[jax.ShapeDtypeStruct((n, w), F32 if i == LOG_DECAY_OUT else BF16) for i, w in enumerate(widths)],
        scratch_shapes=[pltpu.VMEM((V7X_SUBLANES, SHIFT_WIDTH), F32)],
        compiler_params=pltpu.CompilerParams(dimension_semantics=("arbitrary",),
                                             vmem_limit_bytes=V7X_VMEM_LIMIT_BYTES),
        name="proj",
    )(x2, xp, *params, cos, sin)
    return outs


def _mixer_kernel(nb, seqs, clen,
                  r_ref, lw_ref, k_ref, v_ref, al_ref, be_ref, qb_ref, kb_ref, vb_ref, wkv0_ref, ret0_ref,
                  ya_o, ob_o, wkv_o, ret_o, s_scr, r_scr):
    R = seqs * clen
    log2c = int(math.log2(clen))
    c_idx = pl.program_id(1)
    hd = A_HEAD_DIM
    TH = [(t, h) for t in range(nb) for h in range(A_HEADS)]
    TG = [(t, h) for t in range(nb) for h in range(B_HEADS)]

    @pl.when(c_idx == 0)
    def _():
        for t, h in TH:
            blocks = [wkv0_ref[t * seqs + i, h] for i in range(seqs)]
            s_scr[t * A_HEADS + h] = jnp.concatenate(blocks, axis=1) if seqs > 1 else blocks[0]
        for t, h in TG:
            r_scr[t * B_HEADS + h] = ret0_ref[t * seqs:(t + 1) * seqs, h].reshape(seqs * B_QK_DIM, B_V_DIM)

    row = lax.broadcasted_iota(jnp.int32, (R, R), 0)
    col = lax.broadcasted_iota(jnp.int32, (R, R), 1)
    same = (row >> log2c) == (col >> log2c)
    incl = same & (col <= row)
    strict = same & (col < row)
    m_incl = jnp.where(incl, 1.0, 0.0).astype(BF16)
    m_same = jnp.where(same, 1.0, 0.0).astype(BF16)
    eye = jnp.where(row == col, 1.0, 0.0).astype(F32)

    def expand(t):
        if seqs == 1:
            return t
        w = t.shape[1]
        wide = jnp.concatenate([t] * seqs, axis=1)
        rr = lax.broadcasted_iota(jnp.int32, wide.shape, 0) >> log2c
        cc = lax.broadcasted_iota(jnp.int32, wide.shape, 1) // w
        return jnp.where(rr == cc, wide, 0.0)

    def head(x, h):
        return x[:, h * hd:(h + 1) * hd]

    a_bar, r_bar, b_til, k_til, b_dec, k_dec, d_end, vv = [], [], [], [], [], [], [], []
    for t in range(nb):
        lw = lw_ref[t]
        parts = _split3(lw)
        c = sum(jnp.dot(m_incl, p, preferred_element_type=F32) for p in parts)
        cend = sum(jnp.dot(m_same, p, preferred_element_type=F32) for p in parts)
        einv = jnp.exp(-c)
        edec = jnp.exp(cend - c)
        a_bar.append(al_ref[t] * jnp.exp(c - lw))
        r_bar.append(r_ref[t] * jnp.exp(c))
        b_til.append(be_ref[t] * einv)
        k_til.append(k_ref[t] * einv)
        b_dec.append(be_ref[t] * edec)
        k_dec.append(k_ref[t] * edec)
        d_end.append(jnp.exp(cend))
        vv.append(v_ref[t].astype(F32))
    last_row = (lax.broadcasted_iota(jnp.int32, (R, seqs * hd), 0) & (clen - 1)) == clen - 1

    amats = [_dot_nt(jnp.concatenate([head(a_bar[t], h), head(r_bar[t], h)], axis=0),
                     jnp.concatenate([head(b_til[t], h), head(k_til[t], h)], axis=0)) for t, h in TH]

    lgs = [float(np.log1p(-np.exp2(-5.0 - h))) for h in range(B_HEADS)]
    qb = [qb_ref[t].astype(F32) for t in range(nb)]
    kb = [kb_ref[t].astype(F32) for t in range(nb)]
    qs = [qb[t][:, h * B_QK_DIM:(h + 1) * B_QK_DIM] for t, h in TG]
    khs = [kb[t][:, h * B_QK_DIM:(h + 1) * B_QK_DIM] for t, h in TG]
    vbs = [vb_ref[t][:, h * B_V_DIM:(h + 1) * B_V_DIM] for t, h in TG]
    diff = (row - col).astype(F32)
    pos_v = (lax.broadcasted_iota(jnp.int32, (R, B_V_DIM), 0) & (clen - 1)).astype(F32)
    pos_k = (lax.broadcasted_iota(jnp.int32, (R, B_QK_DIM), 0) & (clen - 1)).astype(F32)
    intra = [jnp.where(incl, jnp.exp(lg * diff), 0.0) for lg in lgs]
    cross = [jnp.exp(lg * (pos_v + 1.0)) for lg in lgs]
    kdec = [jnp.exp(lg * (clen - 1.0 - pos_k)) for lg in lgs]
    rstates = [r_scr[t * B_HEADS + h] for t, h in TG]
    scs = [_dot_nt(qs[i], khs[i]) * intra[h] for i, (t, h) in enumerate(TG)]
    qst = [_dot(expand(qs[i]), rstates[i]) * cross[h] for i, (t, h) in enumerate(TG)]

    a_ab = [jnp.where(strict, m[:R, :R], 0.0) for m in amats]
    a_ak = [jnp.where(strict, m[:R, R:], 0.0) for m in amats]
    a_rb = [jnp.where(incl, m[R:, :R], 0.0) for m in amats]
    a_rk = [jnp.where(incl, m[R:, R:], 0.0) for m in amats]
    n = len(TH)
    tinv = [eye + a for a in a_ab]
    vhs = [head(vv[t], h) for t, h in TH]
    av = [_dot(a_ak[i], vhs[i]) for i in range(n)]
    if log2c > 1:
        pw = [_dot(a, a) for a in a_ab]
    for it in range(log2c - 1):
        if it < log2c - 2:
            tp = [_dot(pw[i], jnp.concatenate([tinv[i], pw[i]], axis=1)) for i in range(n)]
            tinv = [tinv[i] + tp[i][:, :R] for i in range(n)]
            pw = [tp[i][:, R:] for i in range(n)]
        else:
            tinv = [tinv[i] + _dot(pw[i], tinv[i]) for i in range(n)]

    os_ = [_dot(scs[i], vbs[i]) + qst[i] for i in range(len(TG))]
    for i, (t, h) in enumerate(TG):
        r_scr[t * B_HEADS + h] = (rstates[i] * float(np.exp(lgs[h] * clen))
                                  + _dot_tn(expand(khs[i] * kdec[h]), vbs[i]))
    for t in range(nb):
        ob_o[t] = jnp.concatenate(os_[t * B_HEADS:(t + 1) * B_HEADS], axis=1)

    wu = [_dot(tinv[i], jnp.concatenate([head(a_bar[t], h), av[i]], axis=1)) for i, (t, h) in enumerate(TH)]
    states = [s_scr[t * A_HEADS + h] for t, h in TH]
    ws = [_dot_nt(jnp.concatenate([expand(wu[i][:, :hd]), expand(head(r_bar[t], h))], axis=0), states[i])
          for i, (t, h) in enumerate(TH)]
    uv = [jnp.concatenate([ws[i][:R] + wu[i][:, hd:], vhs[i]], axis=0) for i in range(n)]
    ys = [ws[i][R:] + _dot(jnp.concatenate([a_rb[i], a_rk[i]], axis=1), uv[i]) for i in range(n)]
    for t in range(nb):
        ya_o[t] = jnp.concatenate(ys[t * A_HEADS:(t + 1) * A_HEADS], axis=1)
    for i, (t, h) in enumerate(TH):
        d_row = jnp.sum(jnp.where(last_row, expand(head(d_end[t], h)), 0.0), axis=0, keepdims=True)
        bk = jnp.concatenate([expand(head(b_dec[t], h)), expand(head(k_dec[t], h))], axis=0)
        s_scr[t * A_HEADS + h] = states[i] * d_row + _dot_tn(uv[i], bk)

    @pl.when(c_idx == pl.num_programs(1) - 1)
    def _():
        for t, h in TH:
            st = s_scr[t * A_HEADS + h]
            for i in range(seqs):
                wkv_o[t * seqs + i, h] = st[:, i * hd:(i + 1) * hd]
        for t, h in TG:
            ret_o[t * seqs:(t + 1) * seqs, h] = r_scr[t * B_HEADS + h].reshape(seqs, B_QK_DIM, B_V_DIM)


def _mixer(ops, wkv0, ret0, n_seq, seq_len):
    n = ops[0].shape[0]
    R = CHUNK_ROWS
    if seq_len >= R:
        assert seq_len % R == 0
        seqs, clen, nchunks = 1, R, seq_len // R
    else:
        assert R % seq_len == 0 and seq_len & (seq_len - 1) == 0 and n_seq % (R // seq_len) == 0
        seqs, clen, nchunks = R // seq_len, seq_len, 1
    ntiles = n_seq // seqs
    nb = MIXER_TILES if seqs == 1 else MIXER_TILES_PACKED
    assert ntiles % nb == 0
    ops3 = [a.reshape(ntiles, nchunks * R, a.shape[1]) for a in ops]

    def rows(width):
        return pl.BlockSpec((nb, R, width), lambda i, c: (i, c, 0))

    wkv_spec = pl.BlockSpec((nb * seqs, A_HEADS, A_HEAD_DIM, A_HEAD_DIM), lambda i, c: (i, 0, 0, 0))
    ret_spec = pl.BlockSpec((nb * seqs, B_HEADS, B_QK_DIM, B_V_DIM), lambda i, c: (i, 0, 0, 0))
    ya, ob, wkv1, ret1 = pl.pallas_call(
        functools.partial(_mixer_kernel, nb, seqs, clen),
        grid=(ntiles // nb, nchunks),
        in_specs=[rows(A_WIDTH)] * 6 + [rows(B_QK_WIDTH), rows(B_QK_WIDTH), rows(B_WIDTH), wkv_spec, ret_spec],
        out_specs=[rows(A_WIDTH), rows(B_WIDTH), wkv_spec, ret_spec],
        out_shape=[jax.ShapeDtypeStruct((ntiles, nchunks * R, A_WIDTH), F32),
                   jax.ShapeDtypeStruct((ntiles, nchunks * R, B_WIDTH), F32),
                   jax.ShapeDtypeStruct(wkv0.shape, F32), jax.ShapeDtypeStruct(ret0.shape, F32)],
        scratch_shapes=[pltpu.VMEM((nb * A_HEADS, A_HEAD_DIM, seqs * A_HEAD_DIM), F32),
                        pltpu.VMEM((nb * B_HEADS, seqs * B_QK_DIM, B_V_DIM), F32)],
        compiler_params=pltpu.CompilerParams(dimension_semantics=("parallel", "arbitrary"),
                                             vmem_limit_bytes=V7X_VMEM_LIMIT_BYTES),
        name="mixer",
    )(*ops3, wkv0, ret0)
    return ya.reshape(n, A_WIDTH), ob.reshape(n, B_WIDTH), wkv1, ret1


def _layer_norm(z, g, b):
    mu = jnp.mean(z, axis=-1, keepdims=True)
    d = z - mu
    var = jnp.mean(d * d, axis=-1, keepdims=True)
    return d * lax.rsqrt(var + LN_EPS) * g + b


def _post_kernel(ya_ref, ob_ref, bonus_ref, g_ref, gb_ref, x_ref, lnxg_ref, lnxb_ref, rgg_ref, rgb_ref,
                 wout_ref, ln1g_ref, ln1b_ref, wr_ref, br_ref, h_o, gate_o, cnt_o):
    tm = x_ref.shape[0]

    def head_norm(t, group, eps, gg, bb):
        ones = _group_ones(t.shape[1], group)
        mu = _group_sum(t, ones) * (1.0 / group)
        d = t - mu
        var = _group_sum(d * d, ones) * (1.0 / group)
        return d * lax.rsqrt(var + eps) * gg + bb

    y_a = (head_norm(ya_ref[...], A_HEAD_DIM, GN_EPS_RWKV, lnxg_ref[...], lnxb_ref[...]) + bonus_ref[...]) * g_ref[...]
    y_b = head_norm(ob_ref[...], B_V_DIM, GN_EPS, rgg_ref[...], rgb_ref[...]) * gb_ref[...]
    y = jnp.concatenate([y_a, y_b], axis=1)
    mix = _dot(y, wout_ref[...])
    h = _layer_norm(DEEPNORM_ALPHA * x_ref[...] + mix, ln1g_ref[...], ln1b_ref[...])
    h_o[...] = h

    h_hi, h_lo = _split2(h)
    w_hi, w_lo = _split2(wr_ref[...])
    logits = (jnp.dot(h_hi, w_hi, preferred_element_type=F32) + jnp.dot(h_hi, w_lo, preferred_element_type=F32)
              + jnp.dot(h_lo, w_hi, preferred_element_type=F32)) + br_ref[...]
    lane = lax.broadcasted_iota(jnp.int32, (tm, ROUTER_LANES), 1)
    neg = -jnp.inf
    big = ROUTER_LANES
    cl = jnp.where(lane < N_GROUPS, logits, neg)
    cmax = jnp.max(cl, axis=-1, keepdims=True)
    grp = jnp.min(jnp.where(cl == cmax, lane, big), axis=-1, keepdims=True)
    gprob = 1.0 / jnp.sum(jnp.exp(cl - cmax), axis=-1, keepdims=True)
    lo_lane = FINE_LANE0 + grp * EXPERTS_PER_GROUP
    fv = jnp.where((lane >= lo_lane) & (lane < lo_lane + EXPERTS_PER_GROUP), logits, neg)
    m1 = jnp.max(fv, axis=-1, keepdims=True)
    i1 = jnp.min(jnp.where(fv == m1, lane, big), axis=-1, keepdims=True)
    fv2 = jnp.where(lane == i1, neg, fv)
    m2 = jnp.max(fv2, axis=-1, keepdims=True)
    i2 = jnp.min(jnp.where(fv2 == m2, lane, big), axis=-1, keepdims=True)
    e2 = jnp.exp(m2 - m1)
    w1 = gprob / (1.0 + e2)
    w2 = gprob * e2 / (1.0 + e2)
    gate_o[...] = (jnp.where(lane == i1, w1, 0.0) + jnp.where(lane == i2, w2, 0.0)
                   + jnp.where(lane == GROUP_LANE, grp.astype(F32), 0.0))
    onehot = jnp.where((lane == grp) & (lane < N_GROUPS), 1.0, 0.0)
    cnt_o[0] = jnp.broadcast_to(jnp.sum(onehot, axis=0, keepdims=True), (V7X_SUBLANES, ROUTER_LANES)).astype(jnp.int32)


def _post(ya, ob, bonus, g, gb, x2, W):
    n = x2.shape[0]
    tm = POST_ROWS
    assert n % tm == 0

    def full(a):
        return pl.BlockSpec(a.shape, lambda i: (0,) * a.ndim)

    def rows(width):
        return pl.BlockSpec((tm, width), lambda i: (i, 0))

    params = [W["lnx_g"], W["lnx_b"], W["ret_gn_g"], W["ret_gn_b"], W["w_out"], W["ln1_g"], W["ln1_b"],
              W["w_router"], W["b_router"]]
    return pl.pallas_call(
        _post_kernel,
        grid=(n // tm,),
        in_specs=[rows(A_WIDTH), rows(B_WIDTH), rows(A_WIDTH), rows(A_WIDTH), rows(B_WIDTH), rows(D_MODEL)]
        + [full(a) for a in params],
        out_specs=[rows(D_MODEL), rows(ROUTER_LANES),
                   pl.BlockSpec((1, V7X_SUBLANES, ROUTER_LANES), lambda i: (i, 0, 0))],
        out_shape=[jax.ShapeDtypeStruct((n, D_MODEL), F32), jax.ShapeDtypeStruct((n, ROUTER_LANES), F32),
                   jax.ShapeDtypeStruct((n // tm, V7X_SUBLANES, ROUTER_LANES), jnp.int32)],
        compiler_params=pltpu.CompilerParams(dimension_semantics=("parallel",),
                                             vmem_limit_bytes=V7X_VMEM_LIMIT_BYTES),
        name="post",
    )(ya, ob, bonus, g, gb, x2, *params)


def _sort_positions(gate, lofs):
    tm = gate.shape[0]
    lane = lax.broadcasted_iota(jnp.int32, (tm, ROUTER_LANES), 1)
    grp = gate[:, GROUP_LANE:GROUP_LANE + 1].astype(jnp.int32)
    onehot = jnp.where((lane == grp) & (lane < N_GROUPS), 1.0, 0.0)
    r = lax.broadcasted_iota(jnp.int32, (tm, tm), 0)
    c = lax.broadcasted_iota(jnp.int32, (tm, tm), 1)
    earlier = jnp.where(c < r, 1.0, 0.0).astype(BF16)
    prefix = jnp.dot(earlier, onehot.astype(BF16), preferred_element_type=F32)
    base = jnp.zeros((tm, ROUTER_LANES), F32)
    for g in range(N_GROUPS):
        base = jnp.where(lane == g, lofs[g].astype(F32), base)
    return jnp.sum(onehot * (base + prefix), axis=1, keepdims=True).astype(jnp.int32)


def _piece_copies(action, rows, bits, copy_of):
    k = rows // SORT_ALIGN
    for b in reversed(range(bits)):
        size = SORT_ALIGN << b

        @pl.when(((k >> b) & 1) == 1)
        def _():
            done = ((k >> (b + 1)) << (b + 1)) * SORT_ALIGN
            cp = copy_of(done, size)
            cp.start() if action == "start" else cp.wait()


def _run_copies(action, plan_ref, i, src_of, dst_of, sem):
    for g in range(N_GROUPS):
        goff = plan_ref[i, g]
        lofs = plan_ref[i, 2 * N_GROUPS + g]

        def copy_of(done, size, goff=goff, lofs=lofs):
            lo = pl.multiple_of(lofs + done, SORT_ALIGN)
            go = pl.multiple_of(goff + done, SORT_ALIGN)
            return pltpu.make_async_copy(src_of(lo, go, size), dst_of(lo, go, size), sem)

        _piece_copies(action, plan_ref[i, N_GROUPS + g], RUN_BITS, copy_of)


def _dispatch_kernel(plan_ref, gap_ref, h_ref, gate_ref, hs_o, gs_o, h_loc, g_loc, sem):
    i = pl.program_id(0)
    tm = h_ref.shape[0]
    lofs = [plan_ref[i, 2 * N_GROUPS + g] for g in range(N_GROUPS)]
    pos = _sort_positions(gate_ref[...], lofs)
    onehot_t = jnp.where(lax.broadcasted_iota(jnp.int32, (tm, SORT_LOCAL), 1) == pos, 1.0, 0.0).astype(BF16)
    idx = lax.broadcasted_iota(jnp.int32, (V7X_SUBLANES, SORT_LOCAL), 1)
    pos_row = (_dot_nt((idx // V7X_LANES).astype(F32), onehot_t) * float(V7X_LANES)
               + _dot_nt((idx % V7X_LANES).astype(F32), onehot_t))[0:1].astype(jnp.int32)
    perm = jnp.where(lax.broadcasted_iota(jnp.int32, (SORT_LOCAL, tm), 0) == pos_row, 1.0, 0.0).astype(BF16)
    h_sorted = jnp.dot(perm, h_ref[...].astype(BF16), preferred_element_type=F32).astype(BF16)
    g_sorted = sum(jnp.dot(perm, p, preferred_element_type=F32) for p in _split3(gate_ref[...]))

    def copies(action, tile):
        for src, dst in ((h_loc, hs_o), (g_loc, gs_o)):
            _run_copies(action, plan_ref, tile, lambda lo, go, sz: src.at[pl.ds(lo, sz)],
                        lambda lo, go, sz: dst.at[pl.ds(go, sz)], sem)

    @pl.when(i > 0)
    def _():
        copies("wait", i - 1)

    h_loc[...] = h_sorted
    g_loc[...] = g_sorted
    copies("start", i)

    @pl.when(i == pl.num_programs(0) - 1)
    def _():
        copies("wait", i)
        h_loc[...] = jnp.zeros_like(h_loc)
        g_loc[...] = jnp.zeros_like(g_loc)
        for action in ("start", "wait"):
            for g in range(N_GROUPS):
                for src, dst in ((h_loc, hs_o), (g_loc, gs_o)):
                    def copy_of(done, size, g=g, src=src, dst=dst):
                        go = pl.multiple_of(gap_ref[g] + done, SORT_ALIGN)
                        return pltpu.make_async_copy(src.at[pl.ds(0, size)], dst.at[pl.ds(go, size)], sem)

                    _piece_copies(action, gap_ref[N_GROUPS + g], GAP_BITS, copy_of)
            for src, dst in ((h_loc, hs_o), (g_loc, gs_o)):
                def body(k, carry, src=src, dst=dst):
                    go = pl.multiple_of(gap_ref[2 * N_GROUPS] + k * EXPERT_ROWS_SHORT, SORT_ALIGN)
                    cp = pltpu.make_async_copy(src.at[pl.ds(0, EXPERT_ROWS_SHORT)], dst.at[pl.ds(go, EXPERT_ROWS_SHORT)], sem)
                    cp.start() if action == "start" else cp.wait()
                    return carry

                lax.fori_loop(0, gap_ref[2 * N_GROUPS + 1] // EXPERT_ROWS_SHORT, body, 0)


def _experts_kernel(tile_group_ref, n_valid_ref, hs_ref, gs_ref, w1_ref, w3_ref, w2_ref, ys_o, w1_b, w3_b, w2_b):
    j = pl.program_id(0)
    last = n_valid_ref[0] - 1
    g = tile_group_ref[jnp.minimum(j, last)]
    g_prev = tile_group_ref[jnp.minimum(jnp.maximum(j - 1, 0), last)]

    @pl.when((j == 0) | (g != g_prev))
    def _():
        w1_b[...] = w1_ref[...].astype(BF16)
        w3_b[...] = w3_ref[...].astype(BF16)
        w2_b[...] = w2_ref[...].astype(BF16)

    @pl.when(j <= last)
    def _():
        x = hs_ref[...]
        gs = gs_ref[...]
        lane = lax.broadcasted_iota(jnp.int32, gs.shape, 1)
        acc = jnp.zeros(ys_o.shape, F32)
        for e in range(EXPERTS_PER_GROUP):
            ge = jnp.sum(jnp.where(lane == FINE_LANE0 + g * EXPERTS_PER_GROUP + e, gs, 0.0), axis=-1, keepdims=True)
            a = jnp.dot(x, w1_b[e], preferred_element_type=F32)
            b = jnp.dot(x, w3_b[e], preferred_element_type=F32)
            hid = (a * _sigmoid(a)) * b * ge
            acc = acc + jnp.dot(hid.astype(BF16), w2_b[e], preferred_element_type=F32)
        ys_o[...] = acc.astype(ys_o.dtype)

    @pl.when(j > last)
    def _():
        ys_o[...] = jnp.zeros_like(ys_o)


def _combine_kernel(plan_ref, h_ref, gate_ref, p_ref, ys_ref, ln2g_ref, ln2b_ref, wple_ref, wpg_ref, pleg_ref,
                    o_ref, y_loc, sem):
    i = pl.program_id(0)
    tm = h_ref.shape[0]
    lofs = [plan_ref[i, 2 * N_GROUPS + g] for g in range(N_GROUPS)]

    def fetch(tile):
        slot = tile % 2
        y_loc[slot] = jnp.zeros(y_loc.shape[1:], y_loc.dtype)
        _run_copies("start", plan_ref, tile, lambda lo, go, sz: ys_ref.at[pl.ds(go, sz)],
                    lambda lo, go, sz: y_loc.at[slot, pl.ds(lo, sz)], sem.at[slot])

    @pl.when(i == 0)
    def _():
        fetch(i)

    @pl.when(i + 1 < pl.num_programs(0))
    def _():
        fetch(i + 1)

    pos = _sort_positions(gate_ref[...], lofs)
    onehot_t = jnp.where(lax.broadcasted_iota(jnp.int32, (tm, SORT_LOCAL), 1) == pos, 1.0, 0.0).astype(BF16)
    ple_in = _dot(p_ref[...], wple_ref[...])
    slot = i % 2
    _run_copies("wait", plan_ref, i, lambda lo, go, sz: ys_ref.at[pl.ds(go, sz)],
                lambda lo, go, sz: y_loc.at[slot, pl.ds(lo, sz)], sem.at[slot])
    ffn = jnp.dot(onehot_t, y_loc[slot], preferred_element_type=F32)
    h2 = _layer_norm(DEEPNORM_ALPHA * h_ref[...] + ffn, ln2g_ref[...], ln2b_ref[...])
    ple = ple_in * _sigmoid(_dot(h2, wpg_ref[...]))
    ms = jnp.mean(ple * ple, axis=-1, keepdims=True)
    o_ref[...] = h2 + ple * lax.rsqrt(ms + LN_EPS) * pleg_ref[...]


def _ffn(h, gate, counts, p2, W):
    n = h.shape[0]
    tm = SORT_ROWS
    assert n % tm == 0 and tm == POST_ROWS and SORT_LOCAL >= tm + N_GROUPS * SORT_ALIGN
    ntiles = n // tm
    er = EXPERT_ROWS if n >= 2 * N_GROUPS * EXPERT_ROWS else EXPERT_ROWS_SHORT
    assert er <= SORT_ALIGN << GAP_BITS and SORT_ALIGN << (GAP_BITS - 1) <= SORT_LOCAL
    cnt = counts[:, 0, :N_GROUPS]
    run = (cnt + SORT_ALIGN - 1) // SORT_ALIGN * SORT_ALIGN
    lofs = jnp.cumsum(run, axis=1) - run
    seg = (jnp.sum(run, axis=0) + er - 1) // er * er
    gbase = jnp.cumsum(seg) - seg
    goff = gbase[None, :] + jnp.cumsum(run, axis=0) - run
    plan = jnp.concatenate([goff, run, lofs], axis=1).astype(jnp.int32)
    max_tiles = (n + ntiles * N_GROUPS * (SORT_ALIGN - 1)) // er + N_GROUPS
    cap = max_tiles * er
    n_valid = (jnp.sum(seg) // er).astype(jnp.int32).reshape(1)
    tile_start = jnp.arange(max_tiles, dtype=jnp.int32) * er
    tile_group = jnp.clip(jnp.sum(tile_start[:, None] >= (gbase + seg)[None, :], axis=1), 0, N_GROUPS - 1).astype(jnp.int32)

    cparams = dict(vmem_limit_bytes=V7X_VMEM_LIMIT_BYTES)
    any_spec = pl.BlockSpec(memory_space=pl.ANY)
    total = jnp.sum(run, axis=0)
    assert er % EXPERT_ROWS_SHORT == 0 and EXPERT_ROWS_SHORT <= SORT_LOCAL
    used = jnp.sum(seg)
    gaps = jnp.concatenate([gbase + total, seg - total, jnp.stack([used, cap - used])]).astype(jnp.int32)
    hs, gs = pl.pallas_call(
        _dispatch_kernel,
        grid_spec=pltpu.PrefetchScalarGridSpec(
            num_scalar_prefetch=2, grid=(ntiles,),
            in_specs=[pl.BlockSpec((tm, D_MODEL), lambda i, plan, gaps: (i, 0)),
                      pl.BlockSpec((tm, ROUTER_LANES), lambda i, plan, gaps: (i, 0))],
            out_specs=[any_spec, any_spec],
            scratch_shapes=[pltpu.VMEM((SORT_LOCAL, D_MODEL), BF16), pltpu.VMEM((SORT_LOCAL, ROUTER_LANES), F32),
                            pltpu.SemaphoreType.DMA(())]),
        out_shape=[jax.ShapeDtypeStruct((cap, D_MODEL), BF16), jax.ShapeDtypeStruct((cap, ROUTER_LANES), F32)],
        compiler_params=pltpu.CompilerParams(dimension_semantics=("arbitrary",), **cparams),
        name="dispatch",
    )(plan, gaps, h, gate)

    def tile_rows(width):
        return pl.BlockSpec((er, width), lambda j, tg, nv: (jnp.minimum(j, nv[0] - 1), 0))

    def group_w(shape):
        return pl.BlockSpec((EXPERTS_PER_GROUP,) + shape, lambda j, tg, nv: (tg[jnp.minimum(j, nv[0] - 1)], 0, 0))

    ys = pl.pallas_call(
        _experts_kernel,
        grid_spec=pltpu.PrefetchScalarGridSpec(
            num_scalar_prefetch=2, grid=(max_tiles,),
            in_specs=[tile_rows(D_MODEL), tile_rows(ROUTER_LANES), group_w((D_MODEL, D_EXPERT)),
                      group_w((D_MODEL, D_EXPERT)), group_w((D_EXPERT, D_MODEL))],
            out_specs=pl.BlockSpec((er, D_MODEL), lambda j, tg, nv: (j, 0)),
            scratch_shapes=[pltpu.VMEM((EXPERTS_PER_GROUP, D_MODEL, D_EXPERT), BF16),
                            pltpu.VMEM((EXPERTS_PER_GROUP, D_MODEL, D_EXPERT), BF16),
                            pltpu.VMEM((EXPERTS_PER_GROUP, D_EXPERT, D_MODEL), BF16)]),
        out_shape=jax.ShapeDtypeStruct((cap, D_MODEL), BF16),
        compiler_params=pltpu.CompilerParams(dimension_semantics=("arbitrary",), **cparams),
        name="experts",
    )(tile_group, n_valid, hs, gs, W["expert_w1"], W["expert_w3"], W["expert_w2"])

    def full(a):
        return pl.BlockSpec(a.shape, lambda i, plan: (0,) * a.ndim)

    params = [W["ln2_g"], W["ln2_b"], W["w_ple"], W["w_ple_gate"], W["ple_norm_g"]]
    return pl.pallas_call(
        _combine_kernel,
        grid_spec=pltpu.PrefetchScalarGridSpec(
            num_scalar_prefetch=1, grid=(ntiles,),
            in_specs=[pl.BlockSpec((tm, D_MODEL), lambda i, plan: (i, 0)),
                      pl.BlockSpec((tm, ROUTER_LANES), lambda i, plan: (i, 0)),
                      pl.BlockSpec((tm, D_PLE), lambda i, plan: (i, 0)), any_spec] + [full(a) for a in params],
            out_specs=pl.BlockSpec((tm, D_MODEL), lambda i, plan: (i, 0)),
            scratch_shapes=[pltpu.VMEM((2, SORT_LOCAL, D_MODEL), BF16), pltpu.SemaphoreType.DMA((2,))]),
        out_shape=jax.ShapeDtypeStruct((n, D_MODEL), F32),
        compiler_params=pltpu.CompilerParams(dimension_semantics=("arbitrary",), **cparams),
        name="combine",
    )(plan, h, gate, p2, ys, *params)


def _layer(x, p, x_prev, wkv0, ret0, pos0, W):
    n_seq, seq_len, _ = x.shape
    n = n_seq * seq_len
    x2 = x.reshape(n, D_MODEL)
    r, lw, k, v, al, be, g, bonus, qb, kb, vb, gb = _proj(x2, x_prev, seq_len, pos0, W)
    ya, ob, wkv1, ret1 = _mixer((r, lw, k, v, al, be, qb, kb, vb), wkv0, ret0, n_seq, seq_len)
    h, gate, counts = _post(ya, ob, bonus, g, gb, x2, W)
    out = _ffn(h, gate, counts, p.reshape(n, D_PLE), W)
    return out.reshape(n_seq, seq_len, D_MODEL), x[:, -1], wkv1, ret1


def _prep_weights(i, w_in, mu_shift, w_decay_up, decay_base, w_aaa_up, aaa_base, w_gate_up, k_k, k_a, r_k,
                  lnx_g, lnx_b, ret_gn_g, ret_gn_b, w_out, ln1_g, ln1_b,
                  router_coarse_w, router_coarse_b, router_fine_w, router_fine_b,
                  expert_w1, expert_w3, expert_w2, ln2_g, ln2_b, w_ple, w_ple_gate, ple_norm_g):
    def row(a):
        return a[i].reshape(1, -1).astype(F32)

    pad = ROUTER_LANES - N_GROUPS - N_EXPERTS
    w_router = jnp.concatenate([router_coarse_w[i], router_fine_w[i], jnp.zeros((D_MODEL, pad), F32)], axis=1)
    b_router = jnp.concatenate([router_coarse_b[i], router_fine_b[i], jnp.zeros((pad,), F32)]).reshape(1, -1)
    return {
        "w_in": w_in[i].astype(BF16), "mu_shift": row(mu_shift), "w_decay_up": w_decay_up[i].astype(BF16),
        "decay_base": row(decay_base), "w_aaa_up": w_aaa_up[i].astype(BF16), "aaa_base": row(aaa_base),
        "w_gate_up": w_gate_up[i].astype(BF16), "k_k": row(k_k), "k_a": row(k_a), "r_k": row(r_k),
        "lnx_g": row(lnx_g), "lnx_b": row(lnx_b), "ret_gn_g": row(ret_gn_g), "ret_gn_b": row(ret_gn_b),
        "w_out": w_out[i].astype(BF16), "ln1_g": row(ln1_g), "ln1_b": row(ln1_b),
        "w_router": w_router, "b_router": b_router,
        "expert_w1": expert_w1[i], "expert_w3": expert_w3[i], "expert_w2": expert_w2[i], "ln2_g": row(ln2_g), "ln2_b": row(ln2_b),
        "w_ple": w_ple[i].astype(BF16), "w_ple_gate": w_ple_gate[i].astype(BF16), "ple_norm_g": row(ple_norm_g),
    }


def kernel(x_prompt, x_sample, p_prompt, p_sample, state_wkv, state_shift, state_ret, w_in, mu_shift, w_decay_up, decay_base, w_aaa_up, aaa_base, w_gate_up, k_k, k_a, r_k, lnx_g, lnx_b, ret_gn_g, ret_gn_b, w_out, ln1_g, ln1_b, router_coarse_w, router_coarse_b, router_fine_w, router_fine_b, expert_w1, expert_w3, expert_w2, ln2_g, ln2_b, w_ple, w_ple_gate, ple_norm_g):
    yp, ys = x_prompt, x_sample
    nb = x_prompt.shape[0]
    depth = w_in.shape[0]
    wkv_p, shift_p, ret_p, wkv_s, shift_s, ret_s = [], [], [], [], [], []
    for i in range(depth):
        W = _prep_weights(i, w_in, mu_shift, w_decay_up, decay_base, w_aaa_up, aaa_base, w_gate_up, k_k, k_a, r_k,
                          lnx_g, lnx_b, ret_gn_g, ret_gn_b, w_out, ln1_g, ln1_b,
                          router_coarse_w, router_coarse_b, router_fine_w, router_fine_b,
                          expert_w1, expert_w3, expert_w2, ln2_g, ln2_b, w_ple, w_ple_gate, ple_norm_g)
        yp, sp, wp, rp = _layer(yp, p_prompt[i], jnp.zeros((nb, D_MODEL), F32),
                                jnp.zeros((nb, A_HEADS, A_HEAD_DIM, A_HEAD_DIM), F32),
                                jnp.zeros((nb, B_HEADS, B_QK_DIM, B_V_DIM), F32), 0, W)
        ys, ss, wsm, rsm = _layer(ys, p_sample[i], state_shift[i], state_wkv[i], state_ret[i], PAST_LEN, W)
        wkv_p.append(wp); shift_p.append(sp); ret_p.append(rp)
        wkv_s.append(wsm); shift_s.append(ss); ret_s.append(rsm)
    return (yp, ys, jnp.stack(wkv_p, 0), jnp.stack(shift_p, 0), jnp.stack(ret_p, 0),
            jnp.stack(wkv_s, 0), jnp.stack(shift_s, 0), jnp.stack(ret_s, 0))
```

```python
import functools
import math

import numpy as np
import jax
import jax.numpy as jnp
from jax import lax
from jax.experimental import pallas as pl
from jax.experimental.pallas import tpu as pltpu

F32 = jnp.float32
BF16 = jnp.bfloat16

D_MODEL = 1024
D_PLE = 256
A_HEADS = 8
A_HEAD_DIM = 64
A_WIDTH = A_HEADS * A_HEAD_DIM
DECAY_LORA = 64
AAA_LORA = 64
GATE_LORA = 128
GN_EPS_RWKV = 64e-5
B_HEADS = 4
B_QK_DIM = 64
B_V_DIM = 128
B_QK_WIDTH = B_HEADS * B_QK_DIM
B_WIDTH = B_HEADS * B_V_DIM
ROPE_BASE = 10000.0
GN_EPS = 1e-5
SHIFT_WIDTH = 3 * A_WIDTH + DECAY_LORA + AAA_LORA + GATE_LORA
IN_WIDTH = SHIFT_WIDTH + 2 * B_QK_WIDTH + 2 * B_WIDTH
N_GROUPS = 4
EXPERTS_PER_GROUP = 4
N_EXPERTS = N_GROUPS * EXPERTS_PER_GROUP
D_EXPERT = 256
DEPTH = 1
PAST_LEN = 16384
DEEPNORM_ALPHA = (2 * DEPTH) ** 0.25
LN_EPS = 1e-5

V7X_LANES = 128
V7X_SUBLANES = 8
V7X_VMEM_LIMIT_BYTES = 56 * 1024 * 1024

PROJ_ROWS = 512
LOG_DECAY_OUT = 1
CHUNK_ROWS = 64
MIXER_TILES = 4
MIXER_TILES_PACKED = 2
POST_ROWS = 512
SORT_ROWS = 512
SORT_ALIGN = 16
SORT_LOCAL = 640
EXPERT_ROWS = 1024
EXPERT_ROWS_SHORT = 256
GAP_BITS = 6
RUN_BITS = 6
GROUP_LANE = 0
ROUTER_LANES = V7X_LANES
FINE_LANE0 = N_GROUPS


def _dot(a, b):
    return jnp.dot(a.astype(BF16), b.astype(BF16), preferred_element_type=F32)


def _dot_nt(a, b):
    return lax.dot_general(a.astype(BF16), b.astype(BF16), (((1,), (1,)), ((), ())), preferred_element_type=F32)


def _dot_tn(a, b):
    return lax.dot_general(a.astype(BF16), b.astype(BF16), (((0,), (0,)), ((), ())), preferred_element_type=F32)


def _split2(x):
    hi = x.astype(BF16)
    lo = (x - hi.astype(F32)).astype(BF16)
    return hi, lo


def _split3(x):
    hi = x.astype(BF16)
    r1 = x - hi.astype(F32)
    mid = r1.astype(BF16)
    lo = (r1 - mid.astype(F32)).astype(BF16)
    return hi, mid, lo


def _sigmoid(x):
    return 1.0 / (1.0 + jnp.exp(-x))


def _group_ones(width, group):
    r = lax.broadcasted_iota(jnp.int32, (width, width), 0) // group
    c = lax.broadcasted_iota(jnp.int32, (width, width), 1) // group
    return jnp.where(r == c, 1.0, 0.0).astype(BF16)


def _group_sum(x, ones):
    return jnp.dot(x.astype(BF16), ones, preferred_element_type=F32)


def _proj_kernel(carry_mode, seq_len, tiles_per_seq,
                 x_ref, xp_ref, w_ref, mu_ref, wdec_ref, dbase_ref, waaa_ref, abase_ref, wgate_ref,
                 kk_ref, ka_ref, rk_ref, cos_ref, sin_ref,
                 r_o, lw_o, k_o, v_o, al_o, be_o, g_o, bonus_o, qb_o, kb_o, vb_o, gb_o,
                 carry_scr):
    tm = x_ref.shape[0]
    if carry_mode:
        xp = jnp.broadcast_to(xp_ref[0], (V7X_SUBLANES, D_MODEL))
        xb = jnp.concatenate([x_ref[...], xp], axis=0).astype(BF16)
        j = pl.program_id(0) % tiles_per_seq

        @pl.when(pl.program_id(0) == 0)
        def _():
            carry_scr[...] = jnp.zeros_like(carry_scr)
    else:
        xb = x_ref[...].astype(BF16)
        xpb = xp_ref[...].astype(BF16)

    def project(lo, hi):
        return jnp.dot(xb, w_ref[:, lo:hi], preferred_element_type=F32)

    def shifted(p, lo, hi):
        cur = p[:tm]
        row = lax.broadcasted_iota(jnp.int32, cur.shape, 0)
        rolled = pltpu.roll(cur, 1, 0)
        if carry_mode:
            first = jnp.where(j == 0, p[tm + V7X_SUBLANES - 1:], carry_scr[V7X_SUBLANES - 1:V7X_SUBLANES, lo:hi])
            prev = jnp.where(row == 0, first, rolled)
            carry_scr[:, lo:hi] = cur[tm - V7X_SUBLANES:]
        else:
            first = jnp.dot(xpb, w_ref[:, lo:hi], preferred_element_type=F32)
            prev = jnp.where((row & (seq_len - 1)) == 0, first, rolled)
        return cur + (prev - cur) * mu_ref[:, lo:hi]

    c_r, c_k, c_v, c_l = 0, A_WIDTH, 2 * A_WIDTH, 3 * A_WIDTH
    p_lora = project(c_l, SHIFT_WIDTH)
    p_k = project(c_k, c_v)

    lora = shifted(p_lora, c_l, SHIFT_WIDTH)
    w_lo = lora[:, :DECAY_LORA]
    a_lo = lora[:, DECAY_LORA:DECAY_LORA + AAA_LORA]
    g_lo = lora[:, DECAY_LORA + AAA_LORA:]
    z = -(dbase_ref[...] + _dot(jnp.tanh(w_lo), wdec_ref[...]))
    softplus = jnp.maximum(z, 0.0) + jnp.log(1.0 + jnp.exp(-jnp.abs(z)))
    log_w = -softplus - 0.5
    lw_o[...] = -jnp.exp(log_w)
    a = _sigmoid(abase_ref[...] + _dot(a_lo, waaa_ref[...]))
    g_o[...] = (_dot(_sigmoid(g_lo), wgate_ref[...])).astype(g_o.dtype)

    p_r = project(c_r, c_k)

    ones64 = _group_ones(A_WIDTH, A_HEAD_DIM)
    k0 = shifted(p_k, c_k, c_v)
    kk0 = k0 * kk_ref[...]
    ssq = _group_sum(kk0 * kk0, ones64)
    kk = kk0 * jnp.minimum(lax.rsqrt(ssq), 1e12)
    k = k0 * (1.0 + (a - 1.0) * ka_ref[...])
    k_o[...] = (k).astype(k_o.dtype)
    al_o[...] = (-kk).astype(al_o.dtype)
    be_o[...] = (kk * a).astype(be_o.dtype)

    p_v = project(c_v, c_l)

    r = shifted(p_r, c_r, c_k)
    r_o[...] = (r).astype(r_o.dtype)
    rk_sum = _group_sum(r * k * rk_ref[...], ones64)

    o = SHIFT_WIDTH
    p_qk = project(o, o + 2 * B_QK_WIDTH)[:tm]

    v = shifted(p_v, c_v, c_l)
    v_o[...] = (v).astype(v_o.dtype)
    bonus_o[...] = (rk_sum * v).astype(bonus_o.dtype)

    p_vb = project(o + 2 * B_QK_WIDTH, o + 2 * B_QK_WIDTH + B_WIDTH)[:tm]

    q_b = p_qk[:, :B_QK_WIDTH]
    k_b = p_qk[:, B_QK_WIDTH:]
    lane = lax.broadcasted_iota(jnp.int32, (tm, B_QK_WIDTH), 1)
    first_half = (lane & (B_QK_DIM - 1)) < (B_QK_DIM // 2)
    cos = cos_ref[...]
    sin = sin_ref[...]

    def rot(t):
        swapped = jnp.where(first_half, pltpu.roll(t, B_QK_WIDTH - B_QK_DIM // 2, 1), pltpu.roll(t, B_QK_DIM // 2, 1))
        return t * cos + swapped * sin

    qb_o[...] = (rot(q_b)).astype(qb_o.dtype)
    kb_o[...] = (rot(k_b) * (B_QK_DIM ** -0.5)).astype(kb_o.dtype)

    p_gb = project(o + 2 * B_QK_WIDTH + B_WIDTH, IN_WIDTH)[:tm]
    vb_o[...] = (p_vb).astype(vb_o.dtype)
    gb_o[...] = (p_gb * _sigmoid(p_gb)).astype(gb_o.dtype)


def _proj(x2, x_prev, seq_len, pos0, W):
    n = x2.shape[0]
    tm = PROJ_ROWS
    assert n % tm == 0
    carry_mode = seq_len % tm == 0
    if carry_mode:
        tiles_per_seq = seq_len // tm
        xp = x_prev.reshape(-1, 1, D_MODEL)
        xp_spec = pl.BlockSpec((1, 1, D_MODEL), lambda i: (i // tiles_per_seq, 0, 0))
        tab_rows = seq_len
    else:
        assert tm % seq_len == 0 and seq_len & (seq_len - 1) == 0
        tiles_per_seq = 1
        xp = jnp.repeat(x_prev, seq_len, axis=0)
        xp_spec = pl.BlockSpec((tm, D_MODEL), lambda i: (i, 0))
        tab_rows = tm
    half = B_QK_DIM // 2
    inv = ROPE_BASE ** (-jnp.arange(half, dtype=F32) / half)
    pos = (pos0 + jnp.arange(seq_len, dtype=jnp.int32)).astype(F32)
    ang = pos[:, None] * inv[None, :]
    cos = jnp.tile(jnp.concatenate([jnp.cos(ang), jnp.cos(ang)], -1), (tab_rows // seq_len, B_HEADS))
    sin = jnp.tile(jnp.concatenate([-jnp.sin(ang), jnp.sin(ang)], -1), (tab_rows // seq_len, B_HEADS))
    tab_tiles = tab_rows // tm
    tab_spec = pl.BlockSpec((tm, B_QK_WIDTH), lambda i: (i % tab_tiles, 0))

    def full(a):
        return pl.BlockSpec(a.shape, lambda i: (0,) * a.ndim)

    def rows(width):
        return pl.BlockSpec((tm, width), lambda i: (i, 0))

    params = [W["w_in"], W["mu_shift"], W["w_decay_up"], W["decay_base"], W["w_aaa_up"], W["aaa_base"],
              W["w_gate_up"], W["k_k"], W["k_a"], W["r_k"]]
    widths = [A_WIDTH] * 8 + [B_QK_WIDTH, B_QK_WIDTH, B_WIDTH, B_WIDTH]
    outs = pl.pallas_call(
        functools.partial(_proj_kernel, carry_mode, seq_len, tiles_per_seq),
        grid=(n // tm,),
        in_specs=[rows(D_MODEL), xp_spec, pl.BlockSpec(params[0].shape, lambda i: (0, 0), pipeline_mode=pl.Buffered(1))]
        + [full(a) for a in params[1:]] + [tab_spec, tab_spec],
        out_specs=[rows(w) for w in widths],
        out_shape=[jax.ShapeDtypeStruct((n, w), F32 if i == LOG_DECAY_OUT else BF16) for i, w in enumerate(widths)],
        scratch_shapes=[pltpu.VMEM((V7X_SUBLANES, SHIFT_WIDTH), F32)],
        compiler_params=pltpu.CompilerParams(dimension_semantics=("arbitrary",),
                                             vmem_limit_bytes=V7X_VMEM_LIMIT_BYTES),
        name="proj",
    )(x2, xp, *params, cos, sin)
    return outs


def _mixer_kernel(nb, seqs, clen,
                  r_ref, lw_ref, k_ref, v_ref, al_ref, be_ref, qb_ref, kb_ref, vb_ref, wkv0_ref, ret0_ref,
                  ya_o, ob_o, wkv_o, ret_o, s_scr, r_scr):
    R = seqs * clen
    log2c = int(math.log2(clen))
    c_idx = pl.program_id(1)
    hd = A_HEAD_DIM
    TH = [(t, h) for t in range(nb) for h in range(A_HEADS)]
    TG = [(t, h) for t in range(nb) for h in range(B_HEADS)]

    @pl.when(c_idx == 0)
    def _():
        for t, h in TH:
            blocks = [wkv0_ref[t * seqs + i, h] for i in range(seqs)]
            s_scr[t * A_HEADS + h] = jnp.concatenate(blocks, axis=1) if seqs > 1 else blocks[0]
        for t, h in TG:
            r_scr[t * B_HEADS + h] = ret0_ref[t * seqs:(t + 1) * seqs, h].reshape(seqs * B_QK_DIM, B_V_DIM)

    row = lax.broadcasted_iota(jnp.int32, (R, R), 0)
    col = lax.broadcasted_iota(jnp.int32, (R, R), 1)
    same = (row >> log2c) == (col >> log2c)
    incl = same & (col <= row)
    strict = same & (col < row)
    m_incl = jnp.where(incl, 1.0, 0.0).astype(BF16)
    m_same = jnp.where(same, 1.0, 0.0).astype(BF16)
    eye = jnp.where(row == col, 1.0, 0.0).astype(F32)

    def expand(t):
        if seqs == 1:
            return t
        w = t.shape[1]
        wide = jnp.concatenate([t] * seqs, axis=1)
        rr = lax.broadcasted_iota(jnp.int32, wide.shape, 0) >> log2c
        cc = lax.broadcasted_iota(jnp.int32, wide.shape, 1) // w
        return jnp.where(rr == cc, wide, 0.0)

    def head(x, h):
        return x[:, h * hd:(h + 1) * hd]

    a_bar, r_bar, b_til, k_til, b_dec, k_dec, d_end, vv = [], [], [], [], [], [], [], []
    for t in range(nb):
        lw = lw_ref[t]
        parts = _split3(lw)
        c = sum(jnp.dot(m_incl, p, preferred_element_type=F32) for p in parts)
        cend = sum(jnp.dot(m_same, p, preferred_element_type=F32) for p in parts)
        einv = jnp.exp(-c)
        edec = jnp.exp(cend - c)
        a_bar.append(al_ref[t] * jnp.exp(c - lw))
        r_bar.append(r_ref[t] * jnp.exp(c))
        b_til.append(be_ref[t] * einv)
        k_til.append(k_ref[t] * einv)
        b_dec.append(be_ref[t] * edec)
        k_dec.append(k_ref[t] * edec)
        d_end.append(jnp.exp(cend))
        vv.append(v_ref[t].astype(F32))
    last_row = (lax.broadcasted_iota(jnp.int32, (R, seqs * hd), 0) & (clen - 1)) == clen - 1

    amats = [_dot_nt(jnp.concatenate([head(a_bar[t], h), head(r_bar[t], h)], axis=0),
                     jnp.concatenate([head(b_til[t], h), head(k_til[t], h)], axis=0)) for t, h in TH]

    lgs = [float(np.log1p(-np.exp2(-5.0 - h))) for h in range(B_HEADS)]
    qb = [qb_ref[t].astype(F32) for t in range(nb)]
    kb = [kb_ref[t].astype(F32) for t in range(nb)]
    qs = [qb[t][:, h * B_QK_DIM:(h + 1) * B_QK_DIM] for t, h in TG]
    khs = [kb[t][:, h * B_QK_DIM:(h + 1) * B_QK_DIM] for t, h in TG]
    vbs = [vb_ref[t][:, h * B_V_DIM:(h + 1) * B_V_DIM] for t, h in TG]
    diff = (row - col).astype(F32)
    pos_v = (lax.broadcasted_iota(jnp.int32, (R, B_V_DIM), 0) & (clen - 1)).astype(F32)
    pos_k = (lax.broadcasted_iota(jnp.int32, (R, B_QK_DIM), 0) & (clen - 1)).astype(F32)
    intra = [jnp.where(incl, jnp.exp(lg * diff), 0.0) for lg in lgs]
    cross = [jnp.exp(lg * (pos_v + 1.0)) for lg in lgs]
    kdec = [jnp.exp(lg * (clen - 1.0 - pos_k)) for lg in lgs]
    rstates = [r_scr[t * B_HEADS + h] for t, h in TG]
    scs = [_dot_nt(qs[i], khs[i]) * intra[h] for i, (t, h) in enumerate(TG)]
    qst = [_dot(expand(qs[i]), rstates[i]) * cross[h] for i, (t, h) in enumerate(TG)]

    a_ab = [jnp.where(strict, m[:R, :R], 0.0) for m in amats]
    a_ak = [jnp.where(strict, m[:R, R:], 0.0) for m in amats]
    a_rb = [jnp.where(incl, m[R:, :R], 0.0) for m in amats]
    a_rk = [jnp.where(incl, m[R:, R:], 0.0) for m in amats]
    n = len(TH)
    tinv = [eye + a for a in a_ab]
    vhs = [head(vv[t], h) for t, h in TH]
    av = [_dot(a_ak[i], vhs[i]) for i in range(n)]
    if log2c > 1:
        pw = [_dot(a, a) for a in a_ab]
    for it in range(log2c - 1):
        if it < log2c - 2:
            tp = [_dot(pw[i], jnp.concatenate([tinv[i], pw[i]], axis=1)) for i in range(n)]
            tinv = [tinv[i] + tp[i][:, :R] for i in range(n)]
            pw = [tp[i][:, R:] for i in range(n)]
        else:
            tinv = [tinv[i] + _dot(pw[i], tinv[i]) for i in range(n)]

    os_ = [_dot(scs[i], vbs[i]) + qst[i] for i in range(len(TG))]
    for i, (t, h) in enumerate(TG):
        r_scr[t * B_HEADS + h] = (rstates[i] * float(np.exp(lgs[h] * clen))
                                  + _dot_tn(expand(khs[i] * kdec[h]), vbs[i]))
    for t in range(nb):
        ob_o[t] = jnp.concatenate(os_[t * B_HEADS:(t + 1) * B_HEADS], axis=1)

    wu = [_dot(tinv[i], jnp.concatenate([head(a_bar[t], h), av[i]], axis=1)) for i, (t, h) in enumerate(TH)]
    states = [s_scr[t * A_HEADS + h] for t, h in TH]
    ws = [_dot_nt(jnp.concatenate([expand(wu[i][:, :hd]), expand(head(r_bar[t], h))], axis=0), states[i])
          for i, (t, h) in enumerate(TH)]
    uv = [jnp.concatenate([ws[i][:R] + wu[i][:, hd:], vhs[i]], axis=0) for i in range(n)]
    ys = [ws[i][R:] + _dot(jnp.concatenate([a_rb[i], a_rk[i]], axis=1), uv[i]) for i in range(n)]
    for t in range(nb):
        ya_o[t] = jnp.concatenate(ys[t * A_HEADS:(t + 1) * A_HEADS], axis=1)
    for i, (t, h) in enumerate(TH):
        d_row = jnp.sum(jnp.where(last_row, expand(head(d_end[t], h)), 0.0), axis=0, keepdims=True)
        bk = jnp.concatenate([expand(head(b_dec[t], h)), expand(head(k_dec[t], h))], axis=0)
        s_scr[t * A_HEADS + h] = states[i] * d_row + _dot_tn(uv[i], bk)

    @pl.when(c_idx == pl.num_programs(1) - 1)
    def _():
        for t, h in TH:
            st = s_scr[t * A_HEADS + h]
            for i in range(seqs):
                wkv_o[t * seqs + i, h] = st[:, i * hd:(i + 1) * hd]
        for t, h in TG:
            ret_o[t * seqs:(t + 1) * seqs, h] = r_scr[t * B_HEADS + h].reshape(seqs, B_QK_DIM, B_V_DIM)


def _mixer(ops, wkv0, ret0, n_seq, seq_len):
    n = ops[0].shape[0]
    R = CHUNK_ROWS
    if seq_len >= R:
        assert seq_len % R == 0
        seqs, clen, nchunks = 1, R, seq_len // R
    else:
        assert R % seq_len == 0 and seq_len & (seq_len - 1) == 0 and n_seq % (R // seq_len) == 0
        seqs, clen, nchunks = R // seq_len, seq_len, 1
    ntiles = n_seq // seqs
    nb = MIXER_TILES if seqs == 1 else MIXER_TILES_PACKED
    assert ntiles % nb == 0
    ops3 = [a.reshape(ntiles, nchunks * R, a.shape[1]) for a in ops]

    def rows(width):
        return pl.BlockSpec((nb, R, width), lambda i, c: (i, c, 0))

    wkv_spec = pl.BlockSpec((nb * seqs, A_HEADS, A_HEAD_DIM, A_HEAD_DIM), lambda i, c: (i, 0, 0, 0))
    ret_spec = pl.BlockSpec((nb * seqs, B_HEADS, B_QK_DIM, B_V_DIM), lambda i, c: (i, 0, 0, 0))
    ya, ob, wkv1, ret1 = pl.pallas_call(
        functools.partial(_mixer_kernel, nb, seqs, clen),
        grid=(ntiles // nb, nchunks),
        in_specs=[rows(A_WIDTH)] * 6 + [rows(B_QK_WIDTH), rows(B_QK_WIDTH), rows(B_WIDTH), wkv_spec, ret_spec],
        out_specs=[rows(A_WIDTH), rows(B_WIDTH), wkv_spec, ret_spec],
        out_shape=[jax.ShapeDtypeStruct((ntiles, nchunks * R, A_WIDTH), F32),
                   jax.ShapeDtypeStruct((ntiles, nchunks * R, B_WIDTH), F32),
                   jax.ShapeDtypeStruct(wkv0.shape, F32), jax.ShapeDtypeStruct(ret0.shape, F32)],
        scratch_shapes=[pltpu.VMEM((nb * A_HEADS, A_HEAD_DIM, seqs * A_HEAD_DIM), F32),
                        pltpu.VMEM((nb * B_HEADS, seqs * B_QK_DIM, B_V_DIM), F32)],
        compiler_params=pltpu.CompilerParams(dimension_semantics=("parallel", "arbitrary"),
                                             vmem_limit_bytes=V7X_VMEM_LIMIT_BYTES),
        name="mixer",
    )(*ops3, wkv0, ret0)
    return ya.reshape(n, A_WIDTH), ob.reshape(n, B_WIDTH), wkv1, ret1


def _layer_norm(z, g, b):
    mu = jnp.mean(z, axis=-1, keepdims=True)
    d = z - mu
    var = jnp.mean(d * d, axis=-1, keepdims=True)
    return d * lax.rsqrt(var + LN_EPS) * g + b


def _post_tile(ya_ref, ob_ref, bonus_ref, g_ref, gb_ref, x_ref, lnxg_ref, lnxb_ref, rgg_ref, rgb_ref,
               wout_ref, ln1g_ref, ln1b_ref, wr_ref, br_ref, h_o, gate_o, cnt_o):
    tm = x_ref.shape[0]

    def head_norm(t, group, eps, gg, bb):
        ones = _group_ones(t.shape[1], group)
        mu = _group_sum(t, ones) * (1.0 / group)
        d = t - mu
        var = _group_sum(d * d, ones) * (1.0 / group)
        return d * lax.rsqrt(var + eps) * gg + bb

    y_a = (head_norm(ya_ref[...], A_HEAD_DIM, GN_EPS_RWKV, lnxg_ref[...], lnxb_ref[...]) + bonus_ref[...]) * g_ref[...]
    y_b = head_norm(ob_ref[...], B_V_DIM, GN_EPS, rgg_ref[...], rgb_ref[...]) * gb_ref[...]
    y = jnp.concatenate([y_a, y_b], axis=1)
    mix = _dot(y, wout_ref[...])
    h = _layer_norm(DEEPNORM_ALPHA * x_ref[...] + mix, ln1g_ref[...], ln1b_ref[...])
    h_o[...] = h

    h_hi, h_lo = _split2(h)
    w_hi, w_lo = _split2(wr_ref[...])
    logits = (jnp.dot(h_hi, w_hi, preferred_element_type=F32) + jnp.dot(h_hi, w_lo, preferred_element_type=F32)
              + jnp.dot(h_lo, w_hi, preferred_element_type=F32)) + br_ref[...]
    lane = lax.broadcasted_iota(jnp.int32, (tm, ROUTER_LANES), 1)
    neg = -jnp.inf
    big = ROUTER_LANES
    cl = jnp.where(lane < N_GROUPS, logits, neg)
    cmax = jnp.max(cl, axis=-1, keepdims=True)
    grp = jnp.min(jnp.where(cl == cmax, lane, big), axis=-1, keepdims=True)
    gprob = 1.0 / jnp.sum(jnp.exp(cl - cmax), axis=-1, keepdims=True)
    lo_lane = FINE_LANE0 + grp * EXPERTS_PER_GROUP
    fv = jnp.where((lane >= lo_lane) & (lane < lo_lane + EXPERTS_PER_GROUP), logits, neg)
    m1 = jnp.max(fv, axis=-1, keepdims=True)
    i1 = jnp.min(jnp.where(fv == m1, lane, big), axis=-1, keepdims=True)
    fv2 = jnp.where(lane == i1, neg, fv)
    m2 = jnp.max(fv2, axis=-1, keepdims=True)
    i2 = jnp.min(jnp.where(fv2 == m2, lane, big), axis=-1, keepdims=True)
    e2 = jnp.exp(m2 - m1)
    w1 = gprob / (1.0 + e2)
    w2 = gprob * e2 / (1.0 + e2)
    gate_o[...] = (jnp.where(lane == i1, w1, 0.0) + jnp.where(lane == i2, w2, 0.0)
                   + jnp.where(lane == GROUP_LANE, grp.astype(F32), 0.0))
    onehot = jnp.where((lane == grp) & (lane < N_GROUPS), 1.0, 0.0)
    cnt_o[0] = jnp.broadcast_to(jnp.sum(onehot, axis=0, keepdims=True), (V7X_SUBLANES, ROUTER_LANES)).astype(jnp.int32)


POST_OPERANDS = 6


def _post_kernel(group_tiles, *refs):
    per_group, shared = refs[:POST_OPERANDS * len(group_tiles)], refs[POST_OPERANDS * len(group_tiles):]
    i = pl.program_id(0)
    start = 0
    for k, tiles in enumerate(group_tiles):
        @pl.when((i >= start) & (i < start + tiles))
        def _(k=k):
            _post_tile(*per_group[POST_OPERANDS * k:POST_OPERANDS * (k + 1)], *shared)

        start += tiles


def _group_rows(tm, width, start, tiles):
    return pl.BlockSpec((tm, width), lambda i, *_: (jnp.clip(i - start, 0, tiles - 1), 0))


def _post(groups, W):
    tm = POST_ROWS
    group_tiles = tuple(g[-1].shape[0] // tm for g in groups)
    assert all(g[-1].shape[0] % tm == 0 for g in groups)
    n = tm * sum(group_tiles)

    def full(a):
        return pl.BlockSpec(a.shape, lambda i: (0,) * a.ndim)

    def rows(width):
        return pl.BlockSpec((tm, width), lambda i: (i, 0))

    widths = [A_WIDTH, B_WIDTH, A_WIDTH, A_WIDTH, B_WIDTH, D_MODEL]
    in_specs, start = [], 0
    for tiles in group_tiles:
        in_specs += [_group_rows(tm, w, start, tiles) for w in widths]
        start += tiles
    params = [W["lnx_g"], W["lnx_b"], W["ret_gn_g"], W["ret_gn_b"], W["w_out"], W["ln1_g"], W["ln1_b"],
              W["w_router"], W["b_router"]]
    return pl.pallas_call(
        functools.partial(_post_kernel, group_tiles),
        grid=(n // tm,),
        in_specs=in_specs + [full(a) for a in params],
        out_specs=[rows(D_MODEL), rows(ROUTER_LANES),
                   pl.BlockSpec((1, V7X_SUBLANES, ROUTER_LANES), lambda i: (i, 0, 0))],
        out_shape=[jax.ShapeDtypeStruct((n, D_MODEL), F32), jax.ShapeDtypeStruct((n, ROUTER_LANES), F32),
                   jax.ShapeDtypeStruct((n // tm, V7X_SUBLANES, ROUTER_LANES), jnp.int32)],
        compiler_params=pltpu.CompilerParams(dimension_semantics=("parallel",),
                                             vmem_limit_bytes=V7X_VMEM_LIMIT_BYTES),
        name="post",
    )(*[a for g in groups for a in g], *params)


def _sort_positions(gate, lofs):
    tm = gate.shape[0]
    lane = lax.broadcasted_iota(jnp.int32, (tm, ROUTER_LANES), 1)
    grp = gate[:, GROUP_LANE:GROUP_LANE + 1].astype(jnp.int32)
    onehot = jnp.where((lane == grp) & (lane < N_GROUPS), 1.0, 0.0)
    r = lax.broadcasted_iota(jnp.int32, (tm, tm), 0)
    c = lax.broadcasted_iota(jnp.int32, (tm, tm), 1)
    earlier = jnp.where(c < r, 1.0, 0.0).astype(BF16)
    prefix = jnp.dot(earlier, onehot.astype(BF16), preferred_element_type=F32)
    base = jnp.zeros((tm, ROUTER_LANES), F32)
    for g in range(N_GROUPS):
        base = jnp.where(lane == g, lofs[g].astype(F32), base)
    return jnp.sum(onehot * (base + prefix), axis=1, keepdims=True).astype(jnp.int32)


def _piece_copies(action, rows, bits, copy_of):
    k = rows // SORT_ALIGN
    for b in reversed(range(bits)):
        size = SORT_ALIGN << b

        @pl.when(((k >> b) & 1) == 1)
        def _():
            done = ((k >> (b + 1)) << (b + 1)) * SORT_ALIGN
            cp = copy_of(done, size)
            cp.start() if action == "start" else cp.wait()


def _run_copies(action, plan_ref, i, src_of, dst_of, sem):
    for g in range(N_GROUPS):
        goff = plan_ref[i, g]
        lofs = plan_ref[i, 2 * N_GROUPS + g]

        def copy_of(done, size, goff=goff, lofs=lofs):
            lo = pl.multiple_of(lofs + done, SORT_ALIGN)
            go = pl.multiple_of(goff + done, SORT_ALIGN)
            return pltpu.make_async_copy(src_of(lo, go, size), dst_of(lo, go, size), sem)

        _piece_copies(action, plan_ref[i, N_GROUPS + g], RUN_BITS, copy_of)


def _dispatch_kernel(plan_ref, gap_ref, h_ref, gate_ref, hs_o, gs_o, h_loc, g_loc, sem):
    i = pl.program_id(0)
    tm = h_ref.shape[0]
    lofs = [plan_ref[i, 2 * N_GROUPS + g] for g in range(N_GROUPS)]
    pos = _sort_positions(gate_ref[...], lofs)
    onehot_t = jnp.where(lax.broadcasted_iota(jnp.int32, (tm, SORT_LOCAL), 1) == pos, 1.0, 0.0).astype(BF16)
    idx = lax.broadcasted_iota(jnp.int32, (V7X_SUBLANES, SORT_LOCAL), 1)
    pos_row = (_dot_nt((idx // V7X_LANES).astype(F32), onehot_t) * float(V7X_LANES)
               + _dot_nt((idx % V7X_LANES).astype(F32), onehot_t))[0:1].astype(jnp.int32)
    perm = jnp.where(lax.broadcasted_iota(jnp.int32, (SORT_LOCAL, tm), 0) == pos_row, 1.0, 0.0).astype(BF16)
    h_sorted = jnp.dot(perm, h_ref[...].astype(BF16), preferred_element_type=F32).astype(BF16)
    g_sorted = sum(jnp.dot(perm, p, preferred_element_type=F32) for p in _split3(gate_ref[...]))

    def copies(action, tile):
        for src, dst in ((h_loc, hs_o), (g_loc, gs_o)):
            _run_copies(action, plan_ref, tile, lambda lo, go, sz: src.at[pl.ds(lo, sz)],
                        lambda lo, go, sz: dst.at[pl.ds(go, sz)], sem)

    @pl.when(i > 0)
    def _():
        copies("wait", i - 1)

    h_loc[...] = h_sorted
    g_loc[...] = g_sorted
    copies("start", i)

    @pl.when(i == pl.num_programs(0) - 1)
    def _():
        copies("wait", i)
        h_loc[...] = jnp.zeros_like(h_loc)
        g_loc[...] = jnp.zeros_like(g_loc)
        for action in ("start", "wait"):
            for g in range(N_GROUPS):
                for src, dst in ((h_loc, hs_o), (g_loc, gs_o)):
                    def copy_of(done, size, g=g, src=src, dst=dst):
                        go = pl.multiple_of(gap_ref[g] + done, SORT_ALIGN)
                        return pltpu.make_async_copy(src.at[pl.ds(0, size)], dst.at[pl.ds(go, size)], sem)

                    _piece_copies(action, gap_ref[N_GROUPS + g], GAP_BITS, copy_of)
            for src, dst in ((h_loc, hs_o), (g_loc, gs_o)):
                def body(k, carry, src=src, dst=dst):
                    go = pl.multiple_of(gap_ref[2 * N_GROUPS] + k * EXPERT_ROWS_SHORT, SORT_ALIGN)
                    cp = pltpu.make_async_copy(src.at[pl.ds(0, EXPERT_ROWS_SHORT)], dst.at[pl.ds(go, EXPERT_ROWS_SHORT)], sem)
                    cp.start() if action == "start" else cp.wait()
                    return carry

                lax.fori_loop(0, gap_ref[2 * N_GROUPS + 1] // EXPERT_ROWS_SHORT, body, 0)


def _experts_kernel(tile_group_ref, n_valid_ref, hs_ref, gs_ref, w1_ref, w3_ref, w2_ref, ys_o, w1_b, w3_b, w2_b):
    j = pl.program_id(0)
    last = n_valid_ref[0] - 1
    g = tile_group_ref[jnp.minimum(j, last)]
    g_prev = tile_group_ref[jnp.minimum(jnp.maximum(j - 1, 0), last)]

    @pl.when((j == 0) | (g != g_prev))
    def _():
        w1_b[...] = w1_ref[...].astype(BF16)
        w3_b[...] = w3_ref[...].astype(BF16)
        w2_b[...] = w2_ref[...].astype(BF16)

    @pl.when(j <= last)
    def _():
        x = hs_ref[...]
        gs = gs_ref[...]
        lane = lax.broadcasted_iota(jnp.int32, gs.shape, 1)
        acc = jnp.zeros(ys_o.shape, F32)
        for e in range(EXPERTS_PER_GROUP):
            ge = jnp.sum(jnp.where(lane == FINE_LANE0 + g * EXPERTS_PER_GROUP + e, gs, 0.0), axis=-1, keepdims=True)
            a = jnp.dot(x, w1_b[e], preferred_element_type=F32)
            b = jnp.dot(x, w3_b[e], preferred_element_type=F32)
            hid = (a * _sigmoid(a)) * b * ge
            acc = acc + jnp.dot(hid.astype(BF16), w2_b[e], preferred_element_type=F32)
        ys_o[...] = acc.astype(ys_o.dtype)

    @pl.when(j > last)
    def _():
        ys_o[...] = jnp.zeros_like(ys_o)


def _combine_kernel(group_tiles, plan_ref, h_ref, gate_ref, *refs):
    ng = len(group_tiles)
    p_refs, (ys_ref, ln2g_ref, ln2b_ref, wple_ref, wpg_ref, pleg_ref) = refs[:ng], refs[ng:ng + 6]
    o_refs, (y_loc, sem) = refs[ng + 6:2 * ng + 6], refs[2 * ng + 6:]
    i = pl.program_id(0)
    starts = [sum(group_tiles[:k]) for k in range(ng)]

    def in_group(k):
        return (i >= starts[k]) & (i < starts[k] + group_tiles[k])

    tm = h_ref.shape[0]
    lofs = [plan_ref[i, 2 * N_GROUPS + g] for g in range(N_GROUPS)]

    def fetch(tile):
        slot = tile % 2
        y_loc[slot] = jnp.zeros(y_loc.shape[1:], y_loc.dtype)
        _run_copies("start", plan_ref, tile, lambda lo, go, sz: ys_ref.at[pl.ds(go, sz)],
                    lambda lo, go, sz: y_loc.at[slot, pl.ds(lo, sz)], sem.at[slot])

    @pl.when(i == 0)
    def _():
        fetch(i)

    @pl.when(i + 1 < pl.num_programs(0))
    def _():
        fetch(i + 1)

    pos = _sort_positions(gate_ref[...], lofs)
    onehot_t = jnp.where(lax.broadcasted_iota(jnp.int32, (tm, SORT_LOCAL), 1) == pos, 1.0, 0.0).astype(BF16)
    p_tile = p_refs[0][...]
    for k in range(1, ng):
        p_tile = jnp.where(in_group(k), p_refs[k][...], p_tile)
    ple_in = _dot(p_tile, wple_ref[...])
    slot = i % 2
    _run_copies("wait", plan_ref, i, lambda lo, go, sz: ys_ref.at[pl.ds(go, sz)],
                lambda lo, go, sz: y_loc.at[slot, pl.ds(lo, sz)], sem.at[slot])
    ffn = jnp.dot(onehot_t, y_loc[slot], preferred_element_type=F32)
    h2 = _layer_norm(DEEPNORM_ALPHA * h_ref[...] + ffn, ln2g_ref[...], ln2b_ref[...])
    ple = ple_in * _sigmoid(_dot(h2, wpg_ref[...]))
    ms = jnp.mean(ple * ple, axis=-1, keepdims=True)
    out = h2 + ple * lax.rsqrt(ms + LN_EPS) * pleg_ref[...]
    for k in range(ng):
        @pl.when(in_group(k))
        def _(k=k):
            o_refs[k][...] = out


def _ffn(h, gate, counts, p_groups, W):
    n = h.shape[0]
    tm = SORT_ROWS
    assert n % tm == 0 and tm == POST_ROWS and SORT_LOCAL >= tm + N_GROUPS * SORT_ALIGN
    ntiles = n // tm
    er = EXPERT_ROWS if n >= 2 * N_GROUPS * EXPERT_ROWS else EXPERT_ROWS_SHORT
    assert er <= SORT_ALIGN << GAP_BITS and SORT_ALIGN << (GAP_BITS - 1) <= SORT_LOCAL
    cnt = counts[:, 0, :N_GROUPS]
    run = (cnt + SORT_ALIGN - 1) // SORT_ALIGN * SORT_ALIGN
    lofs = jnp.cumsum(run, axis=1) - run
    seg = (jnp.sum(run, axis=0) + er - 1) // er * er
    gbase = jnp.cumsum(seg) - seg
    goff = gbase[None, :] + jnp.cumsum(run, axis=0) - run
    plan = jnp.concatenate([goff, run, lofs], axis=1).astype(jnp.int32)
    max_tiles = (n + ntiles * N_GROUPS * (SORT_ALIGN - 1)) // er + N_GROUPS
    cap = max_tiles * er
    n_valid = (jnp.sum(seg) // er).astype(jnp.int32).reshape(1)
    tile_start = jnp.arange(max_tiles, dtype=jnp.int32) * er
    tile_group = jnp.clip(jnp.sum(tile_start[:, None] >= (gbase + seg)[None, :], axis=1), 0, N_GROUPS - 1).astype(jnp.int32)

    cparams = dict(vmem_limit_bytes=V7X_VMEM_LIMIT_BYTES)
    any_spec = pl.BlockSpec(memory_space=pl.ANY)
    total = jnp.sum(run, axis=0)
    assert er % EXPERT_ROWS_SHORT == 0 and EXPERT_ROWS_SHORT <= SORT_LOCAL
    used = jnp.sum(seg)
    gaps = jnp.concatenate([gbase + total, seg - total, jnp.stack([used, cap - used])]).astype(jnp.int32)
    hs, gs = pl.pallas_call(
        _dispatch_kernel,
        grid_spec=pltpu.PrefetchScalarGridSpec(
            num_scalar_prefetch=2, grid=(ntiles,),
            in_specs=[pl.BlockSpec((tm, D_MODEL), lambda i, plan, gaps: (i, 0)),
                      pl.BlockSpec((tm, ROUTER_LANES), lambda i, plan, gaps: (i, 0))],
            out_specs=[any_spec, any_spec],
            scratch_shapes=[pltpu.VMEM((SORT_LOCAL, D_MODEL), BF16), pltpu.VMEM((SORT_LOCAL, ROUTER_LANES), F32),
                            pltpu.SemaphoreType.DMA(())]),
        out_shape=[jax.ShapeDtypeStruct((cap, D_MODEL), BF16), jax.ShapeDtypeStruct((cap, ROUTER_LANES), F32)],
        compiler_params=pltpu.CompilerParams(dimension_semantics=("arbitrary",), **cparams),
        name="dispatch",
    )(plan, gaps, h, gate)

    def tile_rows(width):
        return pl.BlockSpec((er, width), lambda j, tg, nv: (jnp.minimum(j, nv[0] - 1), 0))

    def group_w(shape):
        return pl.BlockSpec((EXPERTS_PER_GROUP,) + shape, lambda j, tg, nv: (tg[jnp.minimum(j, nv[0] - 1)], 0, 0))

    ys = pl.pallas_call(
        _experts_kernel,
        grid_spec=pltpu.PrefetchScalarGridSpec(
            num_scalar_prefetch=2, grid=(max_tiles,),
            in_specs=[tile_rows(D_MODEL), tile_rows(ROUTER_LANES), group_w((D_MODEL, D_EXPERT)),
                      group_w((D_MODEL, D_EXPERT)), group_w((D_EXPERT, D_MODEL))],
            out_specs=pl.BlockSpec((er, D_MODEL), lambda j, tg, nv: (j, 0)),
            scratch_shapes=[pltpu.VMEM((EXPERTS_PER_GROUP, D_MODEL, D_EXPERT), BF16),
                            pltpu.VMEM((EXPERTS_PER_GROUP, D_MODEL, D_EXPERT), BF16),
                            pltpu.VMEM((EXPERTS_PER_GROUP, D_EXPERT, D_MODEL), BF16)]),
        out_shape=jax.ShapeDtypeStruct((cap, D_MODEL), BF16),
        compiler_params=pltpu.CompilerParams(dimension_semantics=("arbitrary",), **cparams),
        name="experts",
    )(tile_group, n_valid, hs, gs, W["expert_w1"], W["expert_w3"], W["expert_w2"])

    def full(a):
        return pl.BlockSpec(a.shape, lambda i, plan: (0,) * a.ndim)

    params = [W["ln2_g"], W["ln2_b"], W["w_ple"], W["w_ple_gate"], W["ple_norm_g"]]
    group_tiles = tuple(pg.shape[0] // tm for pg in p_groups)
    assert sum(group_tiles) == ntiles
    starts = [sum(group_tiles[:k]) for k in range(len(group_tiles))]
    return pl.pallas_call(
        functools.partial(_combine_kernel, group_tiles),
        grid_spec=pltpu.PrefetchScalarGridSpec(
            num_scalar_prefetch=1, grid=(ntiles,),
            in_specs=[pl.BlockSpec((tm, D_MODEL), lambda i, plan: (i, 0)),
                      pl.BlockSpec((tm, ROUTER_LANES), lambda i, plan: (i, 0))]
            + [_group_rows(tm, D_PLE, st, t) for st, t in zip(starts, group_tiles)]
            + [any_spec] + [full(a) for a in params],
            out_specs=[_group_rows(tm, D_MODEL, st, t) for st, t in zip(starts, group_tiles)],
            scratch_shapes=[pltpu.VMEM((2, SORT_LOCAL, D_MODEL), BF16), pltpu.SemaphoreType.DMA((2,))]),
        out_shape=[jax.ShapeDtypeStruct((t * tm, D_MODEL), F32) for t in group_tiles],
        compiler_params=pltpu.CompilerParams(dimension_semantics=("arbitrary",), **cparams),
        name="combine",
    )(plan, h, gate, *p_groups, ys, *params)


def _mix(x, x_prev, wkv0, ret0, pos0, W):
    n_seq, seq_len, _ = x.shape
    x2 = x.reshape(n_seq * seq_len, D_MODEL)
    r, lw, k, v, al, be, g, bonus, qb, kb, vb, gb = _proj(x2, x_prev, seq_len, pos0, W)
    ya, ob, wkv1, ret1 = _mixer((r, lw, k, v, al, be, qb, kb, vb), wkv0, ret0, n_seq, seq_len)
    return (ya, ob, bonus, g, gb, x2), wkv1, ret1


def _prep_weights(i, w_in, mu_shift, w_decay_up, decay_base, w_aaa_up, aaa_base, w_gate_up, k_k, k_a, r_k,
                  lnx_g, lnx_b, ret_gn_g, ret_gn_b, w_out, ln1_g, ln1_b,
                  router_coarse_w, router_coarse_b, router_fine_w, router_fine_b,
                  expert_w1, expert_w3, expert_w2, ln2_g, ln2_b, w_ple, w_ple_gate, ple_norm_g):
    def row(a):
        return a[i].reshape(1, -1).astype(F32)

    pad = ROUTER_LANES - N_GROUPS - N_EXPERTS
    w_router = jnp.concatenate([router_coarse_w[i], router_fine_w[i], jnp.zeros((D_MODEL, pad), F32)], axis=1)
    b_router = jnp.concatenate([router_coarse_b[i], router_fine_b[i], jnp.zeros((pad,), F32)]).reshape(1, -1)
    return {
        "w_in": w_in[i].astype(BF16), "mu_shift": row(mu_shift), "w_decay_up": w_decay_up[i].astype(BF16),
        "decay_base": row(decay_base), "w_aaa_up": w_aaa_up[i].astype(BF16), "aaa_base": row(aaa_base),
        "w_gate_up": w_gate_up[i].astype(BF16), "k_k": row(k_k), "k_a": row(k_a), "r_k": row(r_k),
        "lnx_g": row(lnx_g), "lnx_b": row(lnx_b), "ret_gn_g": row(ret_gn_g), "ret_gn_b": row(ret_gn_b),
        "w_out": w_out[i].astype(BF16), "ln1_g": row(ln1_g), "ln1_b": row(ln1_b),
        "w_router": w_router, "b_router": b_router,
        "expert_w1": expert_w1[i], "expert_w3": expert_w3[i], "expert_w2": expert_w2[i], "ln2_g": row(ln2_g), "ln2_b": row(ln2_b),
        "w_ple": w_ple[i].astype(BF16), "w_ple_gate": w_ple_gate[i].astype(BF16), "ple_norm_g": row(ple_norm_g),
    }


def kernel(x_prompt, x_sample, p_prompt, p_sample, state_wkv, state_shift, state_ret, w_in, mu_shift, w_decay_up, decay_base, w_aaa_up, aaa_base, w_gate_up, k_k, k_a, r_k, lnx_g, lnx_b, ret_gn_g, ret_gn_b, w_out, ln1_g, ln1_b, router_coarse_w, router_coarse_b, router_fine_w, router_fine_b, expert_w1, expert_w3, expert_w2, ln2_g, ln2_b, w_ple, w_ple_gate, ple_norm_g):
    yp, ys = x_prompt, x_sample
    nb = x_prompt.shape[0]
    depth = w_in.shape[0]
    wkv_p, shift_p, ret_p, wkv_s, shift_s, ret_s = [], [], [], [], [], []
    for i in range(depth):
        W = _prep_weights(i, w_in, mu_shift, w_decay_up, decay_base, w_aaa_up, aaa_base, w_gate_up, k_k, k_a, r_k,
                          lnx_g, lnx_b, ret_gn_g, ret_gn_b, w_out, ln1_g, ln1_b,
                          router_coarse_w, router_coarse_b, router_fine_w, router_fine_b,
                          expert_w1, expert_w3, expert_w2, ln2_g, ln2_b, w_ple, w_ple_gate, ple_norm_g)
        ops_p, wp, rp = _mix(yp, jnp.zeros((nb, D_MODEL), F32), jnp.zeros((nb, A_HEADS, A_HEAD_DIM, A_HEAD_DIM), F32),
                             jnp.zeros((nb, B_HEADS, B_QK_DIM, B_V_DIM), F32), 0, W)
        ops_s, wsm, rsm = _mix(ys, state_shift[i], state_wkv[i], state_ret[i], PAST_LEN, W)
        sp, ss = yp[:, -1], ys[:, -1]
        h, gate, counts = _post([ops_p, ops_s], W)
        out_p, out_s = _ffn(h, gate, counts, [p_prompt[i].reshape(-1, D_PLE), p_sample[i].reshape(-1, D_PLE)], W)
        yp, ys = out_p.reshape(yp.shape), out_s.reshape(ys.shape)
        wkv_p.append(wp); shift_p.append(sp); ret_p.append(rp)
        wkv_s.append(wsm); shift_s.append(ss); ret_s.append(rsm)
    return (yp, ys, jnp.stack(wkv_p, 0), jnp.stack(shift_p, 0), jnp.stack(ret_p, 0),
            jnp.stack(wkv_s, 0), jnp.stack(shift_s, 0), jnp.stack(ret_s, 0))
```

```python
import functools
import math

import numpy as np
import jax
import jax.numpy as jnp
from jax import lax
from jax.experimental import pallas as pl
from jax.experimental.pallas import tpu as pltpu

F32 = jnp.float32
BF16 = jnp.bfloat16

D_MODEL = 1024
D_PLE = 256
A_HEADS = 8
A_HEAD_DIM = 64
A_WIDTH = A_HEADS * A_HEAD_DIM
DECAY_LORA = 64
AAA_LORA = 64
GATE_LORA = 128
GN_EPS_RWKV = 64e-5
B_HEADS = 4
B_QK_DIM = 64
B_V_DIM = 128
B_QK_WIDTH = B_HEADS * B_QK_DIM
B_WIDTH = B_HEADS * B_V_DIM
ROPE_BASE = 10000.0
GN_EPS = 1e-5
SHIFT_WIDTH = 3 * A_WIDTH + DECAY_LORA + AAA_LORA + GATE_LORA
IN_WIDTH = SHIFT_WIDTH + 2 * B_QK_WIDTH + 2 * B_WIDTH
N_GROUPS = 4
EXPERTS_PER_GROUP = 4
N_EXPERTS = N_GROUPS * EXPERTS_PER_GROUP
D_EXPERT = 256
DEPTH = 1
PAST_LEN = 16384
DEEPNORM_ALPHA = (2 * DEPTH) ** 0.25
LN_EPS = 1e-5

V7X_LANES = 128
V7X_SUBLANES = 8
V7X_VMEM_LIMIT_BYTES = 56 * 1024 * 1024

PROJ_ROWS = 512
LOG_DECAY_OUT = 1
CHUNK_ROWS = 64
MIXER_TILES = 4
MIXER_TILES_PACKED = 2
POST_ROWS = 512
SORT_ROWS = 512
SORT_ALIGN = 16
SORT_LOCAL = 640
EXPERT_ROWS = 1024
EXPERT_ROWS_SHORT = 256
GAP_BITS = 6
RUN_BITS = 6
GROUP_LANE = 0
ROUTER_LANES = V7X_LANES
FINE_LANE0 = N_GROUPS
SORTED_WIDTH = D_MODEL + 3 * ROUTER_LANES


def _dot(a, b):
    return jnp.dot(a.astype(BF16), b.astype(BF16), preferred_element_type=F32)


def _dot_nt(a, b):
    return lax.dot_general(a.astype(BF16), b.astype(BF16), (((1,), (1,)), ((), ())), preferred_element_type=F32)


def _dot_tn(a, b):
    return lax.dot_general(a.astype(BF16), b.astype(BF16), (((0,), (0,)), ((), ())), preferred_element_type=F32)


def _split2(x):
    hi = x.astype(BF16)
    lo = (x - hi.astype(F32)).astype(BF16)
    return hi, lo


def _split3(x):
    hi = x.astype(BF16)
    r1 = x - hi.astype(F32)
    mid = r1.astype(BF16)
    lo = (r1 - mid.astype(F32)).astype(BF16)
    return hi, mid, lo


def _sigmoid(x):
    return 1.0 / (1.0 + jnp.exp(-x))


def _group_ones(width, group):
    r = lax.broadcasted_iota(jnp.int32, (width, width), 0) // group
    c = lax.broadcasted_iota(jnp.int32, (width, width), 1) // group
    return jnp.where(r == c, 1.0, 0.0).astype(BF16)


def _group_sum(x, ones):
    return jnp.dot(x.astype(BF16), ones, preferred_element_type=F32)


def _proj_kernel(carry_mode, seq_len, tiles_per_seq,
                 x_ref, xp_ref, w_ref, mu_ref, wdec_ref, dbase_ref, waaa_ref, abase_ref, wgate_ref,
                 kk_ref, ka_ref, rk_ref, cos_ref, sin_ref,
                 r_o, lw_o, k_o, v_o, al_o, be_o, g_o, bonus_o, qb_o, kb_o, vb_o, gb_o,
                 carry_scr):
    tm = x_ref.shape[0]
    if carry_mode:
        xp = jnp.broadcast_to(xp_ref[0], (V7X_SUBLANES, D_MODEL))
        xb = jnp.concatenate([x_ref[...], xp], axis=0).astype(BF16)
        j = pl.program_id(0) % tiles_per_seq

        @pl.when(pl.program_id(0) == 0)
        def _():
            carry_scr[...] = jnp.zeros_like(carry_scr)
    else:
        xb = x_ref[...].astype(BF16)
        xpb = xp_ref[...].astype(BF16)

    def project(lo, hi):
        return jnp.dot(xb, w_ref[:, lo:hi], preferred_element_type=F32)

    def shifted(p, lo, hi):
        cur = p[:tm]
        row = lax.broadcasted_iota(jnp.int32, cur.shape, 0)
        rolled = pltpu.roll(cur, 1, 0)
        if carry_mode:
            first = jnp.where(j == 0, p[tm + V7X_SUBLANES - 1:], carry_scr[V7X_SUBLANES - 1:V7X_SUBLANES, lo:hi])
            prev = jnp.where(row == 0, first, rolled)
            carry_scr[:, lo:hi] = cur[tm - V7X_SUBLANES:]
        else:
            first = jnp.dot(xpb, w_ref[:, lo:hi], preferred_element_type=F32)
            prev = jnp.where((row & (seq_len - 1)) == 0, first, rolled)
        return cur + (prev - cur) * mu_ref[:, lo:hi]

    c_r, c_k, c_v, c_l = 0, A_WIDTH, 2 * A_WIDTH, 3 * A_WIDTH
    p_lora = project(c_l, SHIFT_WIDTH)
    p_k = project(c_k, c_v)

    lora = shifted(p_lora, c_l, SHIFT_WIDTH)
    w_lo = lora[:, :DECAY_LORA]
    a_lo = lora[:, DECAY_LORA:DECAY_LORA + AAA_LORA]
    g_lo = lora[:, DECAY_LORA + AAA_LORA:]
    z = -(dbase_ref[...] + _dot(jnp.tanh(w_lo), wdec_ref[...]))
    softplus = jnp.maximum(z, 0.0) + jnp.log(1.0 + jnp.exp(-jnp.abs(z)))
    log_w = -softplus - 0.5
    lw_o[...] = -jnp.exp(log_w)
    a = _sigmoid(abase_ref[...] + _dot(a_lo, waaa_ref[...]))
    g_o[...] = (_dot(_sigmoid(g_lo), wgate_ref[...])).astype(g_o.dtype)

    p_r = project(c_r, c_k)

    ones64 = _group_ones(A_WIDTH, A_HEAD_DIM)
    k0 = shifted(p_k, c_k, c_v)
    kk0 = k0 * kk_ref[...]
    ssq = _group_sum(kk0 * kk0, ones64)
    kk = kk0 * jnp.minimum(lax.rsqrt(ssq), 1e12)
    k = k0 * (1.0 + (a - 1.0) * ka_ref[...])
    k_o[...] = (k).astype(k_o.dtype)
    al_o[...] = (-kk).astype(al_o.dtype)
    be_o[...] = (kk * a).astype(be_o.dtype)

    p_v = project(c_v, c_l)

    r = shifted(p_r, c_r, c_k)
    r_o[...] = (r).astype(r_o.dtype)
    rk_sum = _group_sum(r * k * rk_ref[...], ones64)

    o = SHIFT_WIDTH
    p_qk = project(o, o + 2 * B_QK_WIDTH)[:tm]

    v = shifted(p_v, c_v, c_l)
    v_o[...] = (v).astype(v_o.dtype)
    bonus_o[...] = (rk_sum * v).astype(bonus_o.dtype)

    p_vb = project(o + 2 * B_QK_WIDTH, o + 2 * B_QK_WIDTH + B_WIDTH)[:tm]

    q_b = p_qk[:, :B_QK_WIDTH]
    k_b = p_qk[:, B_QK_WIDTH:]
    lane = lax.broadcasted_iota(jnp.int32, (tm, B_QK_WIDTH), 1)
    first_half = (lane & (B_QK_DIM - 1)) < (B_QK_DIM // 2)
    cos = cos_ref[...]
    sin = sin_ref[...]

    def rot(t):
        swapped = jnp.where(first_half, pltpu.roll(t, B_QK_WIDTH - B_QK_DIM // 2, 1), pltpu.roll(t, B_QK_DIM // 2, 1))
        return t * cos + swapped * sin

    qb_o[...] = (rot(q_b)).astype(qb_o.dtype)
    kb_o[...] = (rot(k_b) * (B_QK_DIM ** -0.5)).astype(kb_o.dtype)

    p_gb = project(o + 2 * B_QK_WIDTH + B_WIDTH, IN_WIDTH)[:tm]
    vb_o[...] = (p_vb).astype(vb_o.dtype)
    gb_o[...] = (p_gb * _sigmoid(p_gb)).astype(gb_o.dtype)


def _proj(x2, x_prev, seq_len, pos0, W):
    n = x2.shape[0]
    tm = PROJ_ROWS
    assert n % tm == 0
    carry_mode = seq_len % tm == 0
    if carry_mode:
        tiles_per_seq = seq_len // tm
        xp = x_prev.reshape(-1, 1, D_MODEL)
        xp_spec = pl.BlockSpec((1, 1, D_MODEL), lambda i: (i // tiles_per_seq, 0, 0))
        tab_rows = seq_len
    else:
        assert tm % seq_len == 0 and seq_len & (seq_len - 1) == 0
        tiles_per_seq = 1
        xp = jnp.repeat(x_prev, seq_len, axis=0)
        xp_spec = pl.BlockSpec((tm, D_MODEL), lambda i: (i, 0))
        tab_rows = tm
    half = B_QK_DIM // 2
    inv = ROPE_BASE ** (-jnp.arange(half, dtype=F32) / half)
    pos = (pos0 + jnp.arange(seq_len, dtype=jnp.int32)).astype(F32)
    ang = pos[:, None] * inv[None, :]
    cos = jnp.tile(jnp.concatenate([jnp.cos(ang), jnp.cos(ang)], -1), (tab_rows // seq_len, B_HEADS))
    sin = jnp.tile(jnp.concatenate([-jnp.sin(ang), jnp.sin(ang)], -1), (tab_rows // seq_len, B_HEADS))
    tab_tiles = tab_rows // tm
    tab_spec = pl.BlockSpec((tm, B_QK_WIDTH), lambda i: (i % tab_tiles, 0))

    def full(a):
        return pl.BlockSpec(a.shape, lambda i: (0,) * a.ndim)

    def rows(width):
        return pl.BlockSpec((tm, width), lambda i: (i, 0))

    params = [W["w_in"], W["mu_shift"], W["w_decay_up"], W["decay_base"], W["w_aaa_up"], W["aaa_base"],
              W["w_gate_up"], W["k_k"], W["k_a"], W["r_k"]]
    widths = [A_WIDTH] * 8 + [B_QK_WIDTH, B_QK_WIDTH, B_WIDTH, B_WIDTH]
    outs = pl.pallas_call(
        functools.partial(_proj_kernel, carry_mode, seq_len, tiles_per_seq),
        grid=(n // tm,),
        in_specs=[rows(D_MODEL), xp_spec, pl.BlockSpec(params[0].shape, lambda i: (0, 0), pipeline_mode=pl.Buffered(1))]
        + [full(a) for a in params[1:]] + [tab_spec, tab_spec],
        out_specs=[rows(w) for w in widths],
        out_shape=[jax.ShapeDtypeStruct((n, w), F32 if i == LOG_DECAY_OUT else BF16) for i, w in enumerate(widths)],
        scratch_shapes=[pltpu.VMEM((V7X_SUBLANES, SHIFT_WIDTH), F32)],
        compiler_params=pltpu.CompilerParams(dimension_semantics=("arbitrary",),
                                             vmem_limit_bytes=V7X_VMEM_LIMIT_BYTES),
        name="proj",
    )(x2, xp, *params, cos, sin)
    return outs


def _mixer_kernel(nb, seqs, clen,
                  r_ref, lw_ref, k_ref, v_ref, al_ref, be_ref, qb_ref, kb_ref, vb_ref, wkv0_ref, ret0_ref,
                  ya_o, ob_o, wkv_o, ret_o, s_scr, r_scr):
    R = seqs * clen
    log2c = int(math.log2(clen))
    c_idx = pl.program_id(1)
    hd = A_HEAD_DIM
    TH = [(t, h) for t in range(nb) for h in range(A_HEADS)]
    TG = [(t, h) for t in range(nb) for h in range(B_HEADS)]

    @pl.when(c_idx == 0)
    def _():
        for t, h in TH:
            blocks = [wkv0_ref[t * seqs + i, h] for i in range(seqs)]
            s_scr[t * A_HEADS + h] = jnp.concatenate(blocks, axis=1) if seqs > 1 else blocks[0]
        for t, h in TG:
            r_scr[t * B_HEADS + h] = ret0_ref[t * seqs:(t + 1) * seqs, h].reshape(seqs * B_QK_DIM, B_V_DIM)

    row = lax.broadcasted_iota(jnp.int32, (R, R), 0)
    col = lax.broadcasted_iota(jnp.int32, (R, R), 1)
    same = (row >> log2c) == (col >> log2c)
    incl = same & (col <= row)
    strict = same & (col < row)
    m_incl = jnp.where(incl, 1.0, 0.0).astype(BF16)
    m_same = jnp.where(same, 1.0, 0.0).astype(BF16)
    eye = jnp.where(row == col, 1.0, 0.0).astype(F32)

    def expand(t):
        if seqs == 1:
            return t
        w = t.shape[1]
        wide = jnp.concatenate([t] * seqs, axis=1)
        rr = lax.broadcasted_iota(jnp.int32, wide.shape, 0) >> log2c
        cc = lax.broadcasted_iota(jnp.int32, wide.shape, 1) // w
        return jnp.where(rr == cc, wide, 0.0)

    def head(x, h):
        return x[:, h * hd:(h + 1) * hd]

    a_bar, r_bar, b_til, k_til, b_dec, k_dec, d_end, vv = [], [], [], [], [], [], [], []
    for t in range(nb):
        lw = lw_ref[t]
        parts = _split3(lw)
        c = sum(jnp.dot(m_incl, p, preferred_element_type=F32) for p in parts)
        cend = sum(jnp.dot(m_same, p, preferred_element_type=F32) for p in parts)
        einv = jnp.exp(-c)
        edec = jnp.exp(cend - c)
        a_bar.append(al_ref[t] * jnp.exp(c - lw))
        r_bar.append(r_ref[t] * jnp.exp(c))
        b_til.append(be_ref[t] * einv)
        k_til.append(k_ref[t] * einv)
        b_dec.append(be_ref[t] * edec)
        k_dec.append(k_ref[t] * edec)
        d_end.append(jnp.exp(cend))
        vv.append(v_ref[t].astype(F32))
    last_row = (lax.broadcasted_iota(jnp.int32, (R, seqs * hd), 0) & (clen - 1)) == clen - 1

    amats = [_dot_nt(jnp.concatenate([head(a_bar[t], h), head(r_bar[t], h)], axis=0),
                     jnp.concatenate([head(b_til[t], h), head(k_til[t], h)], axis=0)) for t, h in TH]

    lgs = [float(np.log1p(-np.exp2(-5.0 - h))) for h in range(B_HEADS)]
    qb = [qb_ref[t].astype(F32) for t in range(nb)]
    kb = [kb_ref[t].astype(F32) for t in range(nb)]
    qs = [qb[t][:, h * B_QK_DIM:(h + 1) * B_QK_DIM] for t, h in TG]
    khs = [kb[t][:, h * B_QK_DIM:(h + 1) * B_QK_DIM] for t, h in TG]
    vbs = [vb_ref[t][:, h * B_V_DIM:(h + 1) * B_V_DIM] for t, h in TG]
    diff = (row - col).astype(F32)
    pos_v = (lax.broadcasted_iota(jnp.int32, (R, B_V_DIM), 0) & (clen - 1)).astype(F32)
    pos_k = (lax.broadcasted_iota(jnp.int32, (R, B_QK_DIM), 0) & (clen - 1)).astype(F32)
    intra = [jnp.where(incl, jnp.exp(lg * diff), 0.0) for lg in lgs]
    cross = [jnp.exp(lg * (pos_v + 1.0)) for lg in lgs]
    kdec = [jnp.exp(lg * (clen - 1.0 - pos_k)) for lg in lgs]
    rstates = [r_scr[t * B_HEADS + h] for t, h in TG]
    scs = [_dot_nt(qs[i], khs[i]) * intra[h] for i, (t, h) in enumerate(TG)]
    qst = [_dot(expand(qs[i]), rstates[i]) * cross[h] for i, (t, h) in enumerate(TG)]

    a_ab = [jnp.where(strict, m[:R, :R], 0.0) for m in amats]
    a_ak = [jnp.where(strict, m[:R, R:], 0.0) for m in amats]
    a_rb = [jnp.where(incl, m[R:, :R], 0.0) for m in amats]
    a_rk = [jnp.where(incl, m[R:, R:], 0.0) for m in amats]
    n = len(TH)
    tinv = [eye + a for a in a_ab]
    vhs = [head(vv[t], h) for t, h in TH]
    av = [_dot(a_ak[i], vhs[i]) for i in range(n)]
    if log2c > 1:
        pw = [_dot(a, a) for a in a_ab]
    for it in range(log2c - 1):
        if it < log2c - 2:
            tp = [_dot(pw[i], jnp.concatenate([tinv[i], pw[i]], axis=1)) for i in range(n)]
            tinv = [tinv[i] + tp[i][:, :R] for i in range(n)]
            pw = [tp[i][:, R:] for i in range(n)]
        else:
            tinv = [tinv[i] + _dot(pw[i], tinv[i]) for i in range(n)]

    os_ = [_dot(scs[i], vbs[i]) + qst[i] for i in range(len(TG))]
    for i, (t, h) in enumerate(TG):
        r_scr[t * B_HEADS + h] = (rstates[i] * float(np.exp(lgs[h] * clen))
                                  + _dot_tn(expand(khs[i] * kdec[h]), vbs[i]))
    for t in range(nb):
        ob_o[t] = jnp.concatenate(os_[t * B_HEADS:(t + 1) * B_HEADS], axis=1)

    wu = [_dot(tinv[i], jnp.concatenate([head(a_bar[t], h), av[i]], axis=1)) for i, (t, h) in enumerate(TH)]
    states = [s_scr[t * A_HEADS + h] for t, h in TH]
    ws = [_dot_nt(jnp.concatenate([expand(wu[i][:, :hd]), expand(head(r_bar[t], h))], axis=0), states[i])
          for i, (t, h) in enumerate(TH)]
    uv = [jnp.concatenate([ws[i][:R] + wu[i][:, hd:], vhs[i]], axis=0) for i in range(n)]
    ys = [ws[i][R:] + _dot(jnp.concatenate([a_rb[i], a_rk[i]], axis=1), uv[i]) for i in range(n)]
    for t in range(nb):
        ya_o[t] = jnp.concatenate(ys[t * A_HEADS:(t + 1) * A_HEADS], axis=1)
    for i, (t, h) in enumerate(TH):
        d_row = jnp.sum(jnp.where(last_row, expand(head(d_end[t], h)), 0.0), axis=0, keepdims=True)
        bk = jnp.concatenate([expand(head(b_dec[t], h)), expand(head(k_dec[t], h))], axis=0)
        s_scr[t * A_HEADS + h] = states[i] * d_row + _dot_tn(uv[i], bk)

    @pl.when(c_idx == pl.num_programs(1) - 1)
    def _():
        for t, h in TH:
            st = s_scr[t * A_HEADS + h]
            for i in range(seqs):
                wkv_o[t * seqs + i, h] = st[:, i * hd:(i + 1) * hd]
        for t, h in TG:
            ret_o[t * seqs:(t + 1) * seqs, h] = r_scr[t * B_HEADS + h].reshape(seqs, B_QK_DIM, B_V_DIM)


def _mixer(ops, wkv0, ret0, n_seq, seq_len):
    n = ops[0].shape[0]
    R = CHUNK_ROWS
    if seq_len >= R:
        assert seq_len % R == 0
        seqs, clen, nchunks = 1, R, seq_len // R
    else:
        assert R % seq_len == 0 and seq_len & (seq_len - 1) == 0 and n_seq % (R // seq_len) == 0
        seqs, clen, nchunks = R // seq_len, seq_len, 1
    ntiles = n_seq // seqs
    nb = MIXER_TILES if seqs == 1 else MIXER_TILES_PACKED
    assert ntiles % nb == 0
    ops3 = [a.reshape(ntiles, nchunks * R, a.shape[1]) for a in ops]

    def rows(width):
        return pl.BlockSpec((nb, R, width), lambda i, c: (i, c, 0))

    wkv_spec = pl.BlockSpec((nb * seqs, A_HEADS, A_HEAD_DIM, A_HEAD_DIM), lambda i, c: (i, 0, 0, 0))
    ret_spec = pl.BlockSpec((nb * seqs, B_HEADS, B_QK_DIM, B_V_DIM), lambda i, c: (i, 0, 0, 0))
    ya, ob, wkv1, ret1 = pl.pallas_call(
        functools.partial(_mixer_kernel, nb, seqs, clen),
        grid=(ntiles // nb, nchunks),
        in_specs=[rows(A_WIDTH)] * 6 + [rows(B_QK_WIDTH), rows(B_QK_WIDTH), rows(B_WIDTH), wkv_spec, ret_spec],
        out_specs=[rows(A_WIDTH), rows(B_WIDTH), wkv_spec, ret_spec],
        out_shape=[jax.ShapeDtypeStruct((ntiles, nchunks * R, A_WIDTH), F32),
                   jax.ShapeDtypeStruct((ntiles, nchunks * R, B_WIDTH), F32),
                   jax.ShapeDtypeStruct(wkv0.shape, F32), jax.ShapeDtypeStruct(ret0.shape, F32)],
        scratch_shapes=[pltpu.VMEM((nb * A_HEADS, A_HEAD_DIM, seqs * A_HEAD_DIM), F32),
                        pltpu.VMEM((nb * B_HEADS, seqs * B_QK_DIM, B_V_DIM), F32)],
        compiler_params=pltpu.CompilerParams(dimension_semantics=("parallel", "arbitrary"),
                                             vmem_limit_bytes=V7X_VMEM_LIMIT_BYTES),
        name="mixer",
    )(*ops3, wkv0, ret0)
    return ya.reshape(n, A_WIDTH), ob.reshape(n, B_WIDTH), wkv1, ret1


def _layer_norm(z, g, b):
    mu = jnp.mean(z, axis=-1, keepdims=True)
    d = z - mu
    var = jnp.mean(d * d, axis=-1, keepdims=True)
    return d * lax.rsqrt(var + LN_EPS) * g + b


def _post_tile(ya_ref, ob_ref, bonus_ref, g_ref, gb_ref, x_ref, lnxg_ref, lnxb_ref, rgg_ref, rgb_ref,
               wout_ref, ln1g_ref, ln1b_ref, wr_ref, br_ref, h_o, gate_o, cnt_o):
    tm = x_ref.shape[0]

    def head_norm(t, group, eps, gg, bb):
        ones = _group_ones(t.shape[1], group)
        mu = _group_sum(t, ones) * (1.0 / group)
        d = t - mu
        var = _group_sum(d * d, ones) * (1.0 / group)
        return d * lax.rsqrt(var + eps) * gg + bb

    y_a = (head_norm(ya_ref[...], A_HEAD_DIM, GN_EPS_RWKV, lnxg_ref[...], lnxb_ref[...]) + bonus_ref[...]) * g_ref[...]
    y_b = head_norm(ob_ref[...], B_V_DIM, GN_EPS, rgg_ref[...], rgb_ref[...]) * gb_ref[...]
    y = jnp.concatenate([y_a, y_b], axis=1)
    mix = _dot(y, wout_ref[...])
    h = _layer_norm(DEEPNORM_ALPHA * x_ref[...] + mix, ln1g_ref[...], ln1b_ref[...])
    h_o[...] = h

    h_hi, h_lo = _split2(h)
    w_hi, w_lo = _split2(wr_ref[...])
    logits = (jnp.dot(h_hi, w_hi, preferred_element_type=F32) + jnp.dot(h_hi, w_lo, preferred_element_type=F32)
              + jnp.dot(h_lo, w_hi, preferred_element_type=F32)) + br_ref[...]
    lane = lax.broadcasted_iota(jnp.int32, (tm, ROUTER_LANES), 1)
    neg = -jnp.inf
    big = ROUTER_LANES
    cl = jnp.where(lane < N_GROUPS, logits, neg)
    cmax = jnp.max(cl, axis=-1, keepdims=True)
    grp = jnp.min(jnp.where(cl == cmax, lane, big), axis=-1, keepdims=True)
    gprob = 1.0 / jnp.sum(jnp.exp(cl - cmax), axis=-1, keepdims=True)
    lo_lane = FINE_LANE0 + grp * EXPERTS_PER_GROUP
    fv = jnp.where((lane >= lo_lane) & (lane < lo_lane + EXPERTS_PER_GROUP), logits, neg)
    m1 = jnp.max(fv, axis=-1, keepdims=True)
    i1 = jnp.min(jnp.where(fv == m1, lane, big), axis=-1, keepdims=True)
    fv2 = jnp.where(lane == i1, neg, fv)
    m2 = jnp.max(fv2, axis=-1, keepdims=True)
    i2 = jnp.min(jnp.where(fv2 == m2, lane, big), axis=-1, keepdims=True)
    e2 = jnp.exp(m2 - m1)
    w1 = gprob / (1.0 + e2)
    w2 = gprob * e2 / (1.0 + e2)
    gate_o[...] = (jnp.where(lane == i1, w1, 0.0) + jnp.where(lane == i2, w2, 0.0)
                   + jnp.where(lane == GROUP_LANE, grp.astype(F32), 0.0))
    onehot = jnp.where((lane == grp) & (lane < N_GROUPS), 1.0, 0.0)
    cnt_o[0] = jnp.broadcast_to(jnp.sum(onehot, axis=0, keepdims=True), (V7X_SUBLANES, ROUTER_LANES)).astype(jnp.int32)


POST_OPERANDS = 6


def _post_kernel(group_tiles, *refs):
    per_group, shared = refs[:POST_OPERANDS * len(group_tiles)], refs[POST_OPERANDS * len(group_tiles):]
    i = pl.program_id(0)
    start = 0
    for k, tiles in enumerate(group_tiles):
        @pl.when((i >= start) & (i < start + tiles))
        def _(k=k):
            _post_tile(*per_group[POST_OPERANDS * k:POST_OPERANDS * (k + 1)], *shared)

        start += tiles


def _group_rows(tm, width, start, tiles):
    return pl.BlockSpec((tm, width), lambda i, *_: (jnp.clip(i - start, 0, tiles - 1), 0))


def _post(groups, W):
    tm = POST_ROWS
    group_tiles = tuple(g[-1].shape[0] // tm for g in groups)
    assert all(g[-1].shape[0] % tm == 0 for g in groups)
    n = tm * sum(group_tiles)

    def full(a):
        return pl.BlockSpec(a.shape, lambda i: (0,) * a.ndim)

    def rows(width):
        return pl.BlockSpec((tm, width), lambda i: (i, 0))

    widths = [A_WIDTH, B_WIDTH, A_WIDTH, A_WIDTH, B_WIDTH, D_MODEL]
    in_specs, start = [], 0
    for tiles in group_tiles:
        in_specs += [_group_rows(tm, w, start, tiles) for w in widths]
        start += tiles
    params = [W["lnx_g"], W["lnx_b"], W["ret_gn_g"], W["ret_gn_b"], W["w_out"], W["ln1_g"], W["ln1_b"],
              W["w_router"], W["b_router"]]
    return pl.pallas_call(
        functools.partial(_post_kernel, group_tiles),
        grid=(n // tm,),
        in_specs=in_specs + [full(a) for a in params],
        out_specs=[rows(D_MODEL), rows(ROUTER_LANES),
                   pl.BlockSpec((1, V7X_SUBLANES, ROUTER_LANES), lambda i: (i, 0, 0))],
        out_shape=[jax.ShapeDtypeStruct((n, D_MODEL), F32), jax.ShapeDtypeStruct((n, ROUTER_LANES), F32),
                   jax.ShapeDtypeStruct((n // tm, V7X_SUBLANES, ROUTER_LANES), jnp.int32)],
        compiler_params=pltpu.CompilerParams(dimension_semantics=("parallel",),
                                             vmem_limit_bytes=V7X_VMEM_LIMIT_BYTES),
        name="post",
    )(*[a for g in groups for a in g], *params)


def _sort_positions(gate, lofs):
    tm = gate.shape[0]
    lane = lax.broadcasted_iota(jnp.int32, (tm, ROUTER_LANES), 1)
    grp = gate[:, GROUP_LANE:GROUP_LANE + 1].astype(jnp.int32)
    onehot = jnp.where((lane == grp) & (lane < N_GROUPS), 1.0, 0.0)
    r = lax.broadcasted_iota(jnp.int32, (tm, tm), 0)
    c = lax.broadcasted_iota(jnp.int32, (tm, tm), 1)
    earlier = jnp.where(c < r, 1.0, 0.0).astype(BF16)
    prefix = jnp.dot(earlier, onehot.astype(BF16), preferred_element_type=F32)
    base = jnp.zeros((tm, ROUTER_LANES), F32)
    for g in range(N_GROUPS):
        base = jnp.where(lane == g, lofs[g].astype(F32), base)
    return jnp.sum(onehot * (base + prefix), axis=1, keepdims=True).astype(jnp.int32)


def _piece_copies(action, rows, bits, copy_of):
    k = rows // SORT_ALIGN
    for b in reversed(range(bits)):
        size = SORT_ALIGN << b

        @pl.when(((k >> b) & 1) == 1)
        def _():
            done = ((k >> (b + 1)) << (b + 1)) * SORT_ALIGN
            cp = copy_of(done, size)
            cp.start() if action == "start" else cp.wait()


def _run_copies(action, plan_ref, i, src_of, dst_of, sem):
    for g in range(N_GROUPS):
        goff = plan_ref[i, g]
        lofs = plan_ref[i, 2 * N_GROUPS + g]

        def copy_of(done, size, goff=goff, lofs=lofs):
            lo = pl.multiple_of(lofs + done, SORT_ALIGN)
            go = pl.multiple_of(goff + done, SORT_ALIGN)
            return pltpu.make_async_copy(src_of(lo, go, size), dst_of(lo, go, size), sem)

        _piece_copies(action, plan_ref[i, N_GROUPS + g], RUN_BITS, copy_of)


def _dispatch_kernel(plan_ref, gap_ref, h_ref, gate_ref, hs_o, h_loc, sem):
    i = pl.program_id(0)
    tm = h_ref.shape[0]
    lofs = [plan_ref[i, 2 * N_GROUPS + g] for g in range(N_GROUPS)]
    pos = _sort_positions(gate_ref[...], lofs)
    onehot_t = jnp.where(lax.broadcasted_iota(jnp.int32, (tm, SORT_LOCAL), 1) == pos, 1.0, 0.0).astype(BF16)
    idx = lax.broadcasted_iota(jnp.int32, (V7X_SUBLANES, SORT_LOCAL), 1)
    pos_row = (_dot_nt((idx // V7X_LANES).astype(F32), onehot_t) * float(V7X_LANES)
               + _dot_nt((idx % V7X_LANES).astype(F32), onehot_t))[0:1].astype(jnp.int32)
    perm = jnp.where(lax.broadcasted_iota(jnp.int32, (SORT_LOCAL, tm), 0) == pos_row, 1.0, 0.0).astype(BF16)
    payload = jnp.concatenate([h_ref[...].astype(BF16)] + list(_split3(gate_ref[...])), axis=1)
    sorted_rows = jnp.dot(perm, payload, preferred_element_type=F32).astype(BF16)

    def copies(action, tile):
        _run_copies(action, plan_ref, tile, lambda lo, go, sz: h_loc.at[pl.ds(lo, sz)],
                    lambda lo, go, sz: hs_o.at[pl.ds(go, sz)], sem)

    @pl.when(i > 0)
    def _():
        copies("wait", i - 1)

    h_loc[...] = sorted_rows
    copies("start", i)

    @pl.when(i == pl.num_programs(0) - 1)
    def _():
        copies("wait", i)
        h_loc[...] = jnp.zeros_like(h_loc)
        for action in ("start", "wait"):
            for g in range(N_GROUPS):
                def copy_of(done, size, g=g):
                    go = pl.multiple_of(gap_ref[g] + done, SORT_ALIGN)
                    return pltpu.make_async_copy(h_loc.at[pl.ds(0, size)], hs_o.at[pl.ds(go, size)], sem)

                _piece_copies(action, gap_ref[N_GROUPS + g], GAP_BITS, copy_of)

            def body(k, carry):
                go = pl.multiple_of(gap_ref[2 * N_GROUPS] + k * EXPERT_ROWS_SHORT, SORT_ALIGN)
                cp = pltpu.make_async_copy(h_loc.at[pl.ds(0, EXPERT_ROWS_SHORT)], hs_o.at[pl.ds(go, EXPERT_ROWS_SHORT)], sem)
                cp.start() if action == "start" else cp.wait()
                return carry

            lax.fori_loop(0, gap_ref[2 * N_GROUPS + 1] // EXPERT_ROWS_SHORT, body, 0)


def _experts_kernel(tile_group_ref, n_valid_ref, hs_ref, w1_ref, w3_ref, w2_ref, ys_o, w1_b, w3_b, w2_b):
    j = pl.program_id(0)
    last = n_valid_ref[0] - 1
    g = tile_group_ref[jnp.minimum(j, last)]
    g_prev = tile_group_ref[jnp.minimum(jnp.maximum(j - 1, 0), last)]

    @pl.when((j == 0) | (g != g_prev))
    def _():
        w1_b[...] = w1_ref[...].astype(BF16)
        w3_b[...] = w3_ref[...].astype(BF16)
        w2_b[...] = w2_ref[...].astype(BF16)

    @pl.when(j <= last)
    def _():
        x = hs_ref[:, :D_MODEL]
        gs = sum(hs_ref[:, D_MODEL + t * ROUTER_LANES:D_MODEL + (t + 1) * ROUTER_LANES].astype(F32) for t in range(3))
        lane = lax.broadcasted_iota(jnp.int32, gs.shape, 1)
        acc = jnp.zeros(ys_o.shape, F32)
        for e in range(EXPERTS_PER_GROUP):
            ge = jnp.sum(jnp.where(lane == FINE_LANE0 + g * EXPERTS_PER_GROUP + e, gs, 0.0), axis=-1, keepdims=True)
            a = jnp.dot(x, w1_b[e], preferred_element_type=F32)
            b = jnp.dot(x, w3_b[e], preferred_element_type=F32)
            hid = (a * _sigmoid(a)) * b * ge
            acc = acc + jnp.dot(hid.astype(BF16), w2_b[e], preferred_element_type=F32)
        ys_o[...] = acc.astype(ys_o.dtype)

    @pl.when(j > last)
    def _():
        ys_o[...] = jnp.zeros_like(ys_o)


def _combine_kernel(group_tiles, plan_ref, h_ref, gate_ref, *refs):
    ng = len(group_tiles)
    p_refs, (ys_ref, ln2g_ref, ln2b_ref, wple_ref, wpg_ref, pleg_ref) = refs[:ng], refs[ng:ng + 6]
    o_refs, (y_loc, sem) = refs[ng + 6:2 * ng + 6], refs[2 * ng + 6:]
    i = pl.program_id(0)
    starts = [sum(group_tiles[:k]) for k in range(ng)]

    def in_group(k):
        return (i >= starts[k]) & (i < starts[k] + group_tiles[k])

    tm = h_ref.shape[0]
    lofs = [plan_ref[i, 2 * N_GROUPS + g] for g in range(N_GROUPS)]

    def fetch(tile):
        slot = tile % 2
        y_loc[slot] = jnp.zeros(y_loc.shape[1:], y_loc.dtype)
        _run_copies("start", plan_ref, tile, lambda lo, go, sz: ys_ref.at[pl.ds(go, sz)],
                    lambda lo, go, sz: y_loc.at[slot, pl.ds(lo, sz)], sem.at[slot])

    @pl.when(i == 0)
    def _():
        fetch(i)

    @pl.when(i + 1 < pl.num_programs(0))
    def _():
        fetch(i + 1)

    pos = _sort_positions(gate_ref[...], lofs)
    onehot_t = jnp.where(lax.broadcasted_iota(jnp.int32, (tm, SORT_LOCAL), 1) == pos, 1.0, 0.0).astype(BF16)
    p_tile = p_refs[0][...]
    for k in range(1, ng):
        p_tile = jnp.where(in_group(k), p_refs[k][...], p_tile)
    ple_in = _dot(p_tile, wple_ref[...])
    slot = i % 2
    _run_copies("wait", plan_ref, i, lambda lo, go, sz: ys_ref.at[pl.ds(go, sz)],
                lambda lo, go, sz: y_loc.at[slot, pl.ds(lo, sz)], sem.at[slot])
    ffn = jnp.dot(onehot_t, y_loc[slot], preferred_element_type=F32)
    h2 = _layer_norm(DEEPNORM_ALPHA * h_ref[...] + ffn, ln2g_ref[...], ln2b_ref[...])
    ple = ple_in * _sigmoid(_dot(h2, wpg_ref[...]))
    ms = jnp.mean(ple * ple, axis=-1, keepdims=True)
    out = h2 + ple * lax.rsqrt(ms + LN_EPS) * pleg_ref[...]
    for k in range(ng):
        @pl.when(in_group(k))
        def _(k=k):
            o_refs[k][...] = out


def _ffn(h, gate, counts, p_groups, W):
    n = h.shape[0]
    tm = SORT_ROWS
    assert n % tm == 0 and tm == POST_ROWS and SORT_LOCAL >= tm + N_GROUPS * SORT_ALIGN
    ntiles = n // tm
    er = EXPERT_ROWS if n >= 2 * N_GROUPS * EXPERT_ROWS else EXPERT_ROWS_SHORT
    assert er <= SORT_ALIGN << GAP_BITS and SORT_ALIGN << (GAP_BITS - 1) <= SORT_LOCAL
    cnt = counts[:, 0, :N_GROUPS]
    run = (cnt + SORT_ALIGN - 1) // SORT_ALIGN * SORT_ALIGN
    lofs = jnp.cumsum(run, axis=1) - run
    seg = (jnp.sum(run, axis=0) + er - 1) // er * er
    gbase = jnp.cumsum(seg) - seg
    goff = gbase[None, :] + jnp.cumsum(run, axis=0) - run
    plan = jnp.concatenate([goff, run, lofs], axis=1).astype(jnp.int32)
    max_tiles = (n + ntiles * N_GROUPS * (SORT_ALIGN - 1)) // er + N_GROUPS
    cap = max_tiles * er
    n_valid = (jnp.sum(seg) // er).astype(jnp.int32).reshape(1)
    tile_start = jnp.arange(max_tiles, dtype=jnp.int32) * er
    tile_group = jnp.clip(jnp.sum(tile_start[:, None] >= (gbase + seg)[None, :], axis=1), 0, N_GROUPS - 1).astype(jnp.int32)

    cparams = dict(vmem_limit_bytes=V7X_VMEM_LIMIT_BYTES)
    any_spec = pl.BlockSpec(memory_space=pl.ANY)
    total = jnp.sum(run, axis=0)
    assert er % EXPERT_ROWS_SHORT == 0 and EXPERT_ROWS_SHORT <= SORT_LOCAL
    used = jnp.sum(seg)
    gaps = jnp.concatenate([gbase + total, seg - total, jnp.stack([used, cap - used])]).astype(jnp.int32)
    hs = pl.pallas_call(
        _dispatch_kernel,
        grid_spec=pltpu.PrefetchScalarGridSpec(
            num_scalar_prefetch=2, grid=(ntiles,),
            in_specs=[pl.BlockSpec((tm, D_MODEL), lambda i, plan, gaps: (i, 0)),
                      pl.BlockSpec((tm, ROUTER_LANES), lambda i, plan, gaps: (i, 0))],
            out_specs=any_spec,
            scratch_shapes=[pltpu.VMEM((SORT_LOCAL, SORTED_WIDTH), BF16), pltpu.SemaphoreType.DMA(())]),
        out_shape=jax.ShapeDtypeStruct((cap, SORTED_WIDTH), BF16),
        compiler_params=pltpu.CompilerParams(dimension_semantics=("arbitrary",), **cparams),
        name="dispatch",
    )(plan, gaps, h, gate)

    def tile_rows(width):
        return pl.BlockSpec((er, width), lambda j, tg, nv: (jnp.minimum(j, nv[0] - 1), 0))

    def group_w(shape):
        return pl.BlockSpec((EXPERTS_PER_GROUP,) + shape, lambda j, tg, nv: (tg[jnp.minimum(j, nv[0] - 1)], 0, 0))

    ys = pl.pallas_call(
        _experts_kernel,
        grid_spec=pltpu.PrefetchScalarGridSpec(
            num_scalar_prefetch=2, grid=(max_tiles,),
            in_specs=[tile_rows(SORTED_WIDTH), group_w((D_MODEL, D_EXPERT)),
                      group_w((D_MODEL, D_EXPERT)), group_w((D_EXPERT, D_MODEL))],
            out_specs=pl.BlockSpec((er, D_MODEL), lambda j, tg, nv: (j, 0)),
            scratch_shapes=[pltpu.VMEM((EXPERTS_PER_GROUP, D_MODEL, D_EXPERT), BF16),
                            pltpu.VMEM((EXPERTS_PER_GROUP, D_MODEL, D_EXPERT), BF16),
                            pltpu.VMEM((EXPERTS_PER_GROUP, D_EXPERT, D_MODEL), BF16)]),
        out_shape=jax.ShapeDtypeStruct((cap, D_MODEL), BF16),
        compiler_params=pltpu.CompilerParams(dimension_semantics=("arbitrary",), **cparams),
        name="experts",
    )(tile_group, n_valid, hs, W["expert_w1"], W["expert_w3"], W["expert_w2"])

    def full(a):
        return pl.BlockSpec(a.shape, lambda i, plan: (0,) * a.ndim)

    params = [W["ln2_g"], W["ln2_b"], W["w_ple"], W["w_ple_gate"], W["ple_norm_g"]]
    group_tiles = tuple(pg.shape[0] // tm for pg in p_groups)
    assert sum(group_tiles) == ntiles
    starts = [sum(group_tiles[:k]) for k in range(len(group_tiles))]
    return pl.pallas_call(
        functools.partial(_combine_kernel, group_tiles),
        grid_spec=pltpu.PrefetchScalarGridSpec(
            num_scalar_prefetch=1, grid=(ntiles,),
            in_specs=[pl.BlockSpec((tm, D_MODEL), lambda i, plan: (i, 0)),
                      pl.BlockSpec((tm, ROUTER_LANES), lambda i, plan: (i, 0))]
            + [_group_rows(tm, D_PLE, st, t) for st, t in zip(starts, group_tiles)]
            + [any_spec] + [full(a) for a in params],
            out_specs=[_group_rows(tm, D_MODEL, st, t) for st, t in zip(starts, group_tiles)],
            scratch_shapes=[pltpu.VMEM((2, SORT_LOCAL, D_MODEL), BF16), pltpu.SemaphoreType.DMA((2,))]),
        out_shape=[jax.ShapeDtypeStruct((t * tm, D_MODEL), F32) for t in group_tiles],
        compiler_params=pltpu.CompilerParams(dimension_semantics=("arbitrary",), **cparams),
        name="combine",
    )(plan, h, gate, *p_groups, ys, *params)


def _mix(x, x_prev, wkv0, ret0, pos0, W):
    n_seq, seq_len, _ = x.shape
    x2 = x.reshape(n_seq * seq_len, D_MODEL)
    r, lw, k, v, al, be, g, bonus, qb, kb, vb, gb = _proj(x2, x_prev, seq_len, pos0, W)
    ya, ob, wkv1, ret1 = _mixer((r, lw, k, v, al, be, qb, kb, vb), wkv0, ret0, n_seq, seq_len)
    return (ya, ob, bonus, g, gb, x2), wkv1, ret1


def _prep_weights(i, w_in, mu_shift, w_decay_up, decay_base, w_aaa_up, aaa_base, w_gate_up, k_k, k_a, r_k,
                  lnx_g, lnx_b, ret_gn_g, ret_gn_b, w_out, ln1_g, ln1_b,
                  router_coarse_w, router_coarse_b, router_fine_w, router_fine_b,
                  expert_w1, expert_w3, expert_w2, ln2_g, ln2_b, w_ple, w_ple_gate, ple_norm_g):
    def row(a):
        return a[i].reshape(1, -1).astype(F32)

    pad = ROUTER_LANES - N_GROUPS - N_EXPERTS
    w_router = jnp.concatenate([router_coarse_w[i], router_fine_w[i], jnp.zeros((D_MODEL, pad), F32)], axis=1)
    b_router = jnp.concatenate([router_coarse_b[i], router_fine_b[i], jnp.zeros((pad,), F32)]).reshape(1, -1)
    return {
        "w_in": w_in[i].astype(BF16), "mu_shift": row(mu_shift), "w_decay_up": w_decay_up[i].astype(BF16),
        "decay_base": row(decay_base), "w_aaa_up": w_aaa_up[i].astype(BF16), "aaa_base": row(aaa_base),
        "w_gate_up": w_gate_up[i].astype(BF16), "k_k": row(k_k), "k_a": row(k_a), "r_k": row(r_k),
        "lnx_g": row(lnx_g), "lnx_b": row(lnx_b), "ret_gn_g": row(ret_gn_g), "ret_gn_b": row(ret_gn_b),
        "w_out": w_out[i].astype(BF16), "ln1_g": row(ln1_g), "ln1_b": row(ln1_b),
        "w_router": w_router, "b_router": b_router,
        "expert_w1": expert_w1[i], "expert_w3": expert_w3[i], "expert_w2": expert_w2[i], "ln2_g": row(ln2_g), "ln2_b": row(ln2_b),
        "w_ple": w_ple[i].astype(BF16), "w_ple_gate": w_ple_gate[i].astype(BF16), "ple_norm_g": row(ple_norm_g),
    }


def kernel(x_prompt, x_sample, p_prompt, p_sample, state_wkv, state_shift, state_ret, w_in, mu_shift, w_decay_up, decay_base, w_aaa_up, aaa_base, w_gate_up, k_k, k_a, r_k, lnx_g, lnx_b, ret_gn_g, ret_gn_b, w_out, ln1_g, ln1_b, router_coarse_w, router_coarse_b, router_fine_w, router_fine_b, expert_w1, expert_w3, expert_w2, ln2_g, ln2_b, w_ple, w_ple_gate, ple_norm_g):
    yp, ys = x_prompt, x_sample
    nb = x_prompt.shape[0]
    depth = w_in.shape[0]
    wkv_p, shift_p, ret_p, wkv_s, shift_s, ret_s = [], [], [], [], [], []
    for i in range(depth):
        W = _prep_weights(i, w_in, mu_shift, w_decay_up, decay_base, w_aaa_up, aaa_base, w_gate_up, k_k, k_a, r_k,
                          lnx_g, lnx_b, ret_gn_g, ret_gn_b, w_out, ln1_g, ln1_b,
                          router_coarse_w, router_coarse_b, router_fine_w, router_fine_b,
                          expert_w1, expert_w3, expert_w2, ln2_g, ln2_b, w_ple, w_ple_gate, ple_norm_g)
        ops_p, wp, rp = _mix(yp, jnp.zeros((nb, D_MODEL), F32), jnp.zeros((nb, A_HEADS, A_HEAD_DIM, A_HEAD_DIM), F32),
                             jnp.zeros((nb, B_HEADS, B_QK_DIM, B_V_DIM), F32), 0, W)
        ops_s, wsm, rsm = _mix(ys, state_shift[i], state_wkv[i], state_ret[i], PAST_LEN, W)
        sp, ss = yp[:, -1], ys[:, -1]
        h, gate, counts = _post([ops_p, ops_s], W)
        out_p, out_s = _ffn(h, gate, counts, [p_prompt[i].reshape(-1, D_PLE), p_sample[i].reshape(-1, D_PLE)], W)
        yp, ys = out_p.reshape(yp.shape), out_s.reshape(ys.shape)
        wkv_p.append(wp); shift_p.append(sp); ret_p.append(rp)
        wkv_s.append(wsm); shift_s.append(ss); ret_s.append(rsm)
    return (yp, ys, jnp.stack(wkv_p, 0), jnp.stack(shift_p, 0), jnp.stack(ret_p, 0),
            jnp.stack(wkv_s, 0), jnp.stack(shift_s, 0), jnp.stack(ret_s, 0))
```

```python
import functools
import math

import numpy as np
import jax
import jax.numpy as jnp
from jax import lax
from jax.experimental import pallas as pl
from jax.experimental.pallas import tpu as pltpu

F32 = jnp.float32
BF16 = jnp.bfloat16

D_MODEL = 1024
D_PLE = 256
A_HEADS = 8
A_HEAD_DIM = 64
A_WIDTH = A_HEADS * A_HEAD_DIM
DECAY_LORA = 64
AAA_LORA = 64
GATE_LORA = 128
GN_EPS_RWKV = 64e-5
B_HEADS = 4
B_QK_DIM = 64
B_V_DIM = 128
B_QK_WIDTH = B_HEADS * B_QK_DIM
B_WIDTH = B_HEADS * B_V_DIM
ROPE_BASE = 10000.0
GN_EPS = 1e-5
SHIFT_WIDTH = 3 * A_WIDTH + DECAY_LORA + AAA_LORA + GATE_LORA
IN_WIDTH = SHIFT_WIDTH + 2 * B_QK_WIDTH + 2 * B_WIDTH
N_GROUPS = 4
EXPERTS_PER_GROUP = 4
N_EXPERTS = N_GROUPS * EXPERTS_PER_GROUP
D_EXPERT = 256
DEPTH = 1
PAST_LEN = 16384
DEEPNORM_ALPHA = (2 * DEPTH) ** 0.25
LN_EPS = 1e-5

V7X_LANES = 128
V7X_SUBLANES = 8
V7X_VMEM_LIMIT_BYTES = 56 * 1024 * 1024

PROJ_ROWS = 1024
LOG_DECAY_OUT = 1
CHUNK_ROWS = 64
MIXER_TILES = 4
MIXER_TILES_PACKED = 2
POST_ROWS = 512
SORT_ROWS = 512
SORT_ALIGN = 16
SORT_LOCAL = 640
EXPERT_ROWS = 1024
EXPERT_ROWS_SHORT = 256
GAP_BITS = 6
RUN_BITS = 6
GROUP_LANE = 0
ROUTER_LANES = V7X_LANES
FINE_LANE0 = N_GROUPS
SORTED_WIDTH = D_MODEL + 3 * ROUTER_LANES


def _dot(a, b):
    return jnp.dot(a.astype(BF16), b.astype(BF16), preferred_element_type=F32)


def _dot_nt(a, b):
    return lax.dot_general(a.astype(BF16), b.astype(BF16), (((1,), (1,)), ((), ())), preferred_element_type=F32)


def _dot_tn(a, b):
    return lax.dot_general(a.astype(BF16), b.astype(BF16), (((0,), (0,)), ((), ())), preferred_element_type=F32)


def _split2(x):
    hi = x.astype(BF16)
    lo = (x - hi.astype(F32)).astype(BF16)
    return hi, lo


def _split3(x):
    hi = x.astype(BF16)
    r1 = x - hi.astype(F32)
    mid = r1.astype(BF16)
    lo = (r1 - mid.astype(F32)).astype(BF16)
    return hi, mid, lo


def _sigmoid(x):
    return 1.0 / (1.0 + jnp.exp(-x))


def _group_ones(width, group):
    r = lax.broadcasted_iota(jnp.int32, (width, width), 0) // group
    c = lax.broadcasted_iota(jnp.int32, (width, width), 1) // group
    return jnp.where(r == c, 1.0, 0.0).astype(BF16)


def _group_sum(x, ones):
    return jnp.dot(x.astype(BF16), ones, preferred_element_type=F32)


def _proj_kernel(carry_mode, seq_len, tiles_per_seq,
                 x_ref, xp_ref, w_ref, mu_ref, wdec_ref, dbase_ref, waaa_ref, abase_ref, wgate_ref,
                 kk_ref, ka_ref, rk_ref, cos_ref, sin_ref,
                 r_o, lw_o, k_o, v_o, al_o, be_o, g_o, bonus_o, qb_o, kb_o, vb_o, gb_o,
                 carry_scr):
    tm = x_ref.shape[0]
    if carry_mode:
        xp = jnp.broadcast_to(xp_ref[0], (V7X_SUBLANES, D_MODEL))
        xb = jnp.concatenate([x_ref[...], xp], axis=0).astype(BF16)
        j = pl.program_id(0) % tiles_per_seq

        @pl.when(pl.program_id(0) == 0)
        def _():
            carry_scr[...] = jnp.zeros_like(carry_scr)
    else:
        xb = x_ref[...].astype(BF16)
        xpb = xp_ref[...].astype(BF16)

    def project(lo, hi):
        return jnp.dot(xb, w_ref[:, lo:hi], preferred_element_type=F32)

    def shifted(p, lo, hi):
        cur = p[:tm]
        row = lax.broadcasted_iota(jnp.int32, cur.shape, 0)
        rolled = pltpu.roll(cur, 1, 0)
        if carry_mode:
            first = jnp.where(j == 0, p[tm + V7X_SUBLANES - 1:], carry_scr[V7X_SUBLANES - 1:V7X_SUBLANES, lo:hi])
            prev = jnp.where(row == 0, first, rolled)
            carry_scr[:, lo:hi] = cur[tm - V7X_SUBLANES:]
        else:
            first = jnp.dot(xpb, w_ref[:, lo:hi], preferred_element_type=F32)
            prev = jnp.where((row & (seq_len - 1)) == 0, first, rolled)
        return cur + (prev - cur) * mu_ref[:, lo:hi]

    c_r, c_k, c_v, c_l = 0, A_WIDTH, 2 * A_WIDTH, 3 * A_WIDTH
    p_lora = project(c_l, SHIFT_WIDTH)
    p_k = project(c_k, c_v)
    p_r = project(c_r, c_k)

    lora = shifted(p_lora, c_l, SHIFT_WIDTH)
    w_lo = lora[:, :DECAY_LORA]
    a_lo = lora[:, DECAY_LORA:DECAY_LORA + AAA_LORA]
    g_lo = lora[:, DECAY_LORA + AAA_LORA:]
    z = -(dbase_ref[...] + _dot(jnp.tanh(w_lo), wdec_ref[...]))
    softplus = jnp.maximum(z, 0.0) + jnp.log(1.0 + jnp.exp(-jnp.abs(z)))
    log_w = -softplus - 0.5
    lw_o[...] = -jnp.exp(log_w)
    a = _sigmoid(abase_ref[...] + _dot(a_lo, waaa_ref[...]))
    g_o[...] = (_dot(_sigmoid(g_lo), wgate_ref[...])).astype(g_o.dtype)

    p_v = project(c_v, c_l)

    ones64 = _group_ones(A_WIDTH, A_HEAD_DIM)
    k0 = shifted(p_k, c_k, c_v)
    kk0 = k0 * kk_ref[...]
    ssq = _group_sum(kk0 * kk0, ones64)
    kk = kk0 * jnp.minimum(lax.rsqrt(ssq), 1e12)
    k = k0 * (1.0 + (a - 1.0) * ka_ref[...])
    k_o[...] = (k).astype(k_o.dtype)
    al_o[...] = (-kk).astype(al_o.dtype)
    be_o[...] = (kk * a).astype(be_o.dtype)

    o = SHIFT_WIDTH
    p_qk = project(o, o + 2 * B_QK_WIDTH)[:tm]

    r = shifted(p_r, c_r, c_k)
    r_o[...] = (r).astype(r_o.dtype)
    rk_sum = _group_sum(r * k * rk_ref[...], ones64)

    p_vb = project(o + 2 * B_QK_WIDTH, o + 2 * B_QK_WIDTH + B_WIDTH)[:tm]

    v = shifted(p_v, c_v, c_l)
    v_o[...] = (v).astype(v_o.dtype)
    bonus_o[...] = (rk_sum * v).astype(bonus_o.dtype)

    p_gb = project(o + 2 * B_QK_WIDTH + B_WIDTH, IN_WIDTH)[:tm]

    q_b = p_qk[:, :B_QK_WIDTH]
    k_b = p_qk[:, B_QK_WIDTH:]
    lane = lax.broadcasted_iota(jnp.int32, (tm, B_QK_WIDTH), 1)
    first_half = (lane & (B_QK_DIM - 1)) < (B_QK_DIM // 2)
    cos = cos_ref[...]
    sin = sin_ref[...]

    def rot(t):
        swapped = jnp.where(first_half, pltpu.roll(t, B_QK_WIDTH - B_QK_DIM // 2, 1), pltpu.roll(t, B_QK_DIM // 2, 1))
        return t * cos + swapped * sin

    qb_o[...] = (rot(q_b)).astype(qb_o.dtype)
    kb_o[...] = (rot(k_b) * (B_QK_DIM ** -0.5)).astype(kb_o.dtype)

    vb_o[...] = (p_vb).astype(vb_o.dtype)
    gb_o[...] = (p_gb * _sigmoid(p_gb)).astype(gb_o.dtype)


def _proj(x2, x_prev, seq_len, pos0, W):
    n = x2.shape[0]
    tm = PROJ_ROWS
    assert n % tm == 0
    carry_mode = seq_len % tm == 0
    if carry_mode:
        tiles_per_seq = seq_len // tm
        xp = x_prev.reshape(-1, 1, D_MODEL)
        xp_spec = pl.BlockSpec((1, 1, D_MODEL), lambda i: (i // tiles_per_seq, 0, 0))
        tab_rows = seq_len
    else:
        assert tm % seq_len == 0 and seq_len & (seq_len - 1) == 0
        tiles_per_seq = 1
        xp = jnp.repeat(x_prev, seq_len, axis=0)
        xp_spec = pl.BlockSpec((tm, D_MODEL), lambda i: (i, 0))
        tab_rows = tm
    half = B_QK_DIM // 2
    inv = ROPE_BASE ** (-jnp.arange(half, dtype=F32) / half)
    pos = (pos0 + jnp.arange(seq_len, dtype=jnp.int32)).astype(F32)
    ang = pos[:, None] * inv[None, :]
    cos = jnp.tile(jnp.concatenate([jnp.cos(ang), jnp.cos(ang)], -1), (tab_rows // seq_len, B_HEADS))
    sin = jnp.tile(jnp.concatenate([-jnp.sin(ang), jnp.sin(ang)], -1), (tab_rows // seq_len, B_HEADS))
    tab_tiles = tab_rows // tm
    tab_spec = pl.BlockSpec((tm, B_QK_WIDTH), lambda i: (i % tab_tiles, 0))

    def full(a):
        return pl.BlockSpec(a.shape, lambda i: (0,) * a.ndim)

    def rows(width):
        return pl.BlockSpec((tm, width), lambda i: (i, 0))

    params = [W["w_in"], W["mu_shift"], W["w_decay_up"], W["decay_base"], W["w_aaa_up"], W["aaa_base"],
              W["w_gate_up"], W["k_k"], W["k_a"], W["r_k"]]
    widths = [A_WIDTH] * 8 + [B_QK_WIDTH, B_QK_WIDTH, B_WIDTH, B_WIDTH]
    outs = pl.pallas_call(
        functools.partial(_proj_kernel, carry_mode, seq_len, tiles_per_seq),
        grid=(n // tm,),
        in_specs=[rows(D_MODEL), xp_spec, pl.BlockSpec(params[0].shape, lambda i: (0, 0), pipeline_mode=pl.Buffered(1))]
        + [full(a) for a in params[1:]] + [tab_spec, tab_spec],
        out_specs=[rows(w) for w in widths],
        out_shape=[jax.ShapeDtypeStruct((n, w), F32 if i == LOG_DECAY_OUT else BF16) for i, w in enumerate(widths)],
        scratch_shapes=[pltpu.VMEM((V7X_SUBLANES, SHIFT_WIDTH), F32)],
        compiler_params=pltpu.CompilerParams(dimension_semantics=("arbitrary",),
                                             vmem_limit_bytes=V7X_VMEM_LIMIT_BYTES),
        name="proj",
    )(x2, xp, *params, cos, sin)
    return outs


def _mixer_kernel(nb, seqs, clen,
                  r_ref, lw_ref, k_ref, v_ref, al_ref, be_ref, qb_ref, kb_ref, vb_ref, wkv0_ref, ret0_ref,
                  ya_o, ob_o, wkv_o, ret_o, s_scr, r_scr):
    R = seqs * clen
    log2c = int(math.log2(clen))
    c_idx = pl.program_id(1)
    hd = A_HEAD_DIM
    TH = [(t, h) for t in range(nb) for h in range(A_HEADS)]
    TG = [(t, h) for t in range(nb) for h in range(B_HEADS)]

    @pl.when(c_idx == 0)
    def _():
        for t, h in TH:
            blocks = [wkv0_ref[t * seqs + i, h] for i in range(seqs)]
            s_scr[t * A_HEADS + h] = jnp.concatenate(blocks, axis=1) if seqs > 1 else blocks[0]
        for t, h in TG:
            r_scr[t * B_HEADS + h] = ret0_ref[t * seqs:(t + 1) * seqs, h].reshape(seqs * B_QK_DIM, B_V_DIM)

    row = lax.broadcasted_iota(jnp.int32, (R, R), 0)
    col = lax.broadcasted_iota(jnp.int32, (R, R), 1)
    same = (row >> log2c) == (col >> log2c)
    incl = same & (col <= row)
    strict = same & (col < row)
    m_incl = jnp.where(incl, 1.0, 0.0).astype(BF16)
    m_same = jnp.where(same, 1.0, 0.0).astype(BF16)
    eye = jnp.where(row == col, 1.0, 0.0).astype(F32)

    def expand(t):
        if seqs == 1:
            return t
        w = t.shape[1]
        wide = jnp.concatenate([t] * seqs, axis=1)
        rr = lax.broadcasted_iota(jnp.int32, wide.shape, 0) >> log2c
        cc = lax.broadcasted_iota(jnp.int32, wide.shape, 1) // w
        return jnp.where(rr == cc, wide, 0.0)

    def head(x, h):
        return x[:, h * hd:(h + 1) * hd]

    a_bar, r_bar, b_til, k_til, b_dec, k_dec, d_end, vv = [], [], [], [], [], [], [], []
    for t in range(nb):
        lw = lw_ref[t]
        parts = _split3(lw)
        c = sum(jnp.dot(m_incl, p, preferred_element_type=F32) for p in parts)
        cend = sum(jnp.dot(m_same, p, preferred_element_type=F32) for p in parts)
        einv = jnp.exp(-c)
        edec = jnp.exp(cend - c)
        a_bar.append(al_ref[t] * jnp.exp(c - lw))
        r_bar.append(r_ref[t] * jnp.exp(c))
        b_til.append(be_ref[t] * einv)
        k_til.append(k_ref[t] * einv)
        b_dec.append(be_ref[t] * edec)
        k_dec.append(k_ref[t] * edec)
        d_end.append(jnp.exp(cend))
        vv.append(v_ref[t].astype(F32))
    last_row = (lax.broadcasted_iota(jnp.int32, (R, seqs * hd), 0) & (clen - 1)) == clen - 1

    amats = [_dot_nt(jnp.concatenate([head(a_bar[t], h), head(r_bar[t], h)], axis=0),
                     jnp.concatenate([head(b_til[t], h), head(k_til[t], h)], axis=0)) for t, h in TH]

    lgs = [float(np.log1p(-np.exp2(-5.0 - h))) for h in range(B_HEADS)]
    qb = [qb_ref[t].astype(F32) for t in range(nb)]
    kb = [kb_ref[t].astype(F32) for t in range(nb)]
    qs = [qb[t][:, h * B_QK_DIM:(h + 1) * B_QK_DIM] for t, h in TG]
    khs = [kb[t][:, h * B_QK_DIM:(h + 1) * B_QK_DIM] for t, h in TG]
    vbs = [vb_ref[t][:, h * B_V_DIM:(h + 1) * B_V_DIM] for t, h in TG]
    diff = (row - col).astype(F32)
    pos_v = (lax.broadcasted_iota(jnp.int32, (R, B_V_DIM), 0) & (clen - 1)).astype(F32)
    pos_k = (lax.broadcasted_iota(jnp.int32, (R, B_QK_DIM), 0) & (clen - 1)).astype(F32)
    intra = [jnp.where(incl, jnp.exp(lg * diff), 0.0) for lg in lgs]
    cross = [jnp.exp(lg * (pos_v + 1.0)) for lg in lgs]
    kdec = [jnp.exp(lg * (clen - 1.0 - pos_k)) for lg in lgs]
    rstates = [r_scr[t * B_HEADS + h] for t, h in TG]
    scs = [_dot_nt(qs[i], khs[i]) * intra[h] for i, (t, h) in enumerate(TG)]
    qst = [_dot(expand(qs[i]), rstates[i]) * cross[h] for i, (t, h) in enumerate(TG)]

    a_ab = [jnp.where(strict, m[:R, :R], 0.0) for m in amats]
    a_ak = [jnp.where(strict, m[:R, R:], 0.0) for m in amats]
    a_rb = [jnp.where(incl, m[R:, :R], 0.0) for m in amats]
    a_rk = [jnp.where(incl, m[R:, R:], 0.0) for m in amats]
    n = len(TH)
    tinv = [eye + a for a in a_ab]
    vhs = [head(vv[t], h) for t, h in TH]
    av = [_dot(a_ak[i], vhs[i]) for i in range(n)]
    if log2c > 1:
        pw = [_dot(a, a) for a in a_ab]
    for it in range(log2c - 1):
        if it < log2c - 2:
            tp = [_dot(pw[i], jnp.concatenate([tinv[i], pw[i]], axis=1)) for i in range(n)]
            tinv = [tinv[i] + tp[i][:, :R] for i in range(n)]
            pw = [tp[i][:, R:] for i in range(n)]
        else:
            tinv = [tinv[i] + _dot(pw[i], tinv[i]) for i in range(n)]

    os_ = [_dot(scs[i], vbs[i]) + qst[i] for i in range(len(TG))]
    for i, (t, h) in enumerate(TG):
        r_scr[t * B_HEADS + h] = (rstates[i] * float(np.exp(lgs[h] * clen))
                                  + _dot_tn(expand(khs[i] * kdec[h]), vbs[i]))
    for t in range(nb):
        ob_o[t] = jnp.concatenate(os_[t * B_HEADS:(t + 1) * B_HEADS], axis=1)

    wu = [_dot(tinv[i], jnp.concatenate([head(a_bar[t], h), av[i]], axis=1)) for i, (t, h) in enumerate(TH)]
    states = [s_scr[t * A_HEADS + h] for t, h in TH]
    ws = [_dot_nt(jnp.concatenate([expand(wu[i][:, :hd]), expand(head(r_bar[t], h))], axis=0), states[i])
          for i, (t, h) in enumerate(TH)]
    uv = [jnp.concatenate([ws[i][:R] + wu[i][:, hd:], vhs[i]], axis=0) for i in range(n)]
    ys = [ws[i][R:] + _dot(jnp.concatenate([a_rb[i], a_rk[i]], axis=1), uv[i]) for i in range(n)]
    for t in range(nb):
        ya_o[t] = jnp.concatenate(ys[t * A_HEADS:(t + 1) * A_HEADS], axis=1)
    for i, (t, h) in enumerate(TH):
        d_row = jnp.sum(jnp.where(last_row, expand(head(d_end[t], h)), 0.0), axis=0, keepdims=True)
        bk = jnp.concatenate([expand(head(b_dec[t], h)), expand(head(k_dec[t], h))], axis=0)
        s_scr[t * A_HEADS + h] = states[i] * d_row + _dot_tn(uv[i], bk)

    @pl.when(c_idx == pl.num_programs(1) - 1)
    def _():
        for t, h in TH:
            st = s_scr[t * A_HEADS + h]
            for i in range(seqs):
                wkv_o[t * seqs + i, h] = st[:, i * hd:(i + 1) * hd]
        for t, h in TG:
            ret_o[t * seqs:(t + 1) * seqs, h] = r_scr[t * B_HEADS + h].reshape(seqs, B_QK_DIM, B_V_DIM)


def _mixer(ops, wkv0, ret0, n_seq, seq_len):
    n = ops[0].shape[0]
    R = CHUNK_ROWS
    if seq_len >= R:
        assert seq_len % R == 0
        seqs, clen, nchunks = 1, R, seq_len // R
    else:
        assert R % seq_len == 0 and seq_len & (seq_len - 1) == 0 and n_seq % (R // seq_len) == 0
        seqs, clen, nchunks = R // seq_len, seq_len, 1
    ntiles = n_seq // seqs
    nb = MIXER_TILES if seqs == 1 else MIXER_TILES_PACKED
    assert ntiles % nb == 0
    ops3 = [a.reshape(ntiles, nchunks * R, a.shape[1]) for a in ops]

    def rows(width):
        return pl.BlockSpec((nb, R, width), lambda i, c: (i, c, 0))

    wkv_spec = pl.BlockSpec((nb * seqs, A_HEADS, A_HEAD_DIM, A_HEAD_DIM), lambda i, c: (i, 0, 0, 0))
    ret_spec = pl.BlockSpec((nb * seqs, B_HEADS, B_QK_DIM, B_V_DIM), lambda i, c: (i, 0, 0, 0))
    ya, ob, wkv1, ret1 = pl.pallas_call(
        functools.partial(_mixer_kernel, nb, seqs, clen),
        grid=(ntiles // nb, nchunks),
        in_specs=[rows(A_WIDTH)] * 6 + [rows(B_QK_WIDTH), rows(B_QK_WIDTH), rows(B_WIDTH), wkv_spec, ret_spec],
        out_specs=[rows(A_WIDTH), rows(B_WIDTH), wkv_spec, ret_spec],
        out_shape=[jax.ShapeDtypeStruct((ntiles, nchunks * R, A_WIDTH), F32),
                   jax.ShapeDtypeStruct((ntiles, nchunks * R, B_WIDTH), F32),
                   jax.ShapeDtypeStruct(wkv0.shape, F32), jax.ShapeDtypeStruct(ret0.shape, F32)],
        scratch_shapes=[pltpu.VMEM((nb * A_HEADS, A_HEAD_DIM, seqs * A_HEAD_DIM), F32),
                        pltpu.VMEM((nb * B_HEADS, seqs * B_QK_DIM, B_V_DIM), F32)],
        compiler_params=pltpu.CompilerParams(dimension_semantics=("parallel", "arbitrary"),
                                             vmem_limit_bytes=V7X_VMEM_LIMIT_BYTES),
        name="mixer",
    )(*ops3, wkv0, ret0)
    return ya.reshape(n, A_WIDTH), ob.reshape(n, B_WIDTH), wkv1, ret1


def _layer_norm(z, g, b):
    mu = jnp.mean(z, axis=-1, keepdims=True)
    d = z - mu
    var = jnp.mean(d * d, axis=-1, keepdims=True)
    return d * lax.rsqrt(var + LN_EPS) * g + b


def _post_tile(ya_ref, ob_ref, bonus_ref, g_ref, gb_ref, x_ref, lnxg_ref, lnxb_ref, rgg_ref, rgb_ref,
               wout_ref, ln1g_ref, ln1b_ref, wr_ref, br_ref, h_o, gate_o, cnt_o):
    tm = x_ref.shape[0]

    def head_norm(t, group, eps, gg, bb):
        ones = _group_ones(t.shape[1], group)
        mu = _group_sum(t, ones) * (1.0 / group)
        d = t - mu
        var = _group_sum(d * d, ones) * (1.0 / group)
        return d * lax.rsqrt(var + eps) * gg + bb

    y_a = (head_norm(ya_ref[...], A_HEAD_DIM, GN_EPS_RWKV, lnxg_ref[...], lnxb_ref[...]) + bonus_ref[...]) * g_ref[...]
    y_b = head_norm(ob_ref[...], B_V_DIM, GN_EPS, rgg_ref[...], rgb_ref[...]) * gb_ref[...]
    y = jnp.concatenate([y_a, y_b], axis=1)
    mix = _dot(y, wout_ref[...])
    h = _layer_norm(DEEPNORM_ALPHA * x_ref[...] + mix, ln1g_ref[...], ln1b_ref[...])
    h_o[...] = h

    h_hi, h_lo = _split2(h)
    w_hi, w_lo = _split2(wr_ref[...])
    logits = (jnp.dot(h_hi, w_hi, preferred_element_type=F32) + jnp.dot(h_hi, w_lo, preferred_element_type=F32)
              + jnp.dot(h_lo, w_hi, preferred_element_type=F32)) + br_ref[...]
    lane = lax.broadcasted_iota(jnp.int32, (tm, ROUTER_LANES), 1)
    neg = -jnp.inf
    big = ROUTER_LANES
    cl = jnp.where(lane < N_GROUPS, logits, neg)
    cmax = jnp.max(cl, axis=-1, keepdims=True)
    grp = jnp.min(jnp.where(cl == cmax, lane, big), axis=-1, keepdims=True)
    gprob = 1.0 / jnp.sum(jnp.exp(cl - cmax), axis=-1, keepdims=True)
    lo_lane = FINE_LANE0 + grp * EXPERTS_PER_GROUP
    fv = jnp.where((lane >= lo_lane) & (lane < lo_lane + EXPERTS_PER_GROUP), logits, neg)
    m1 = jnp.max(fv, axis=-1, keepdims=True)
    i1 = jnp.min(jnp.where(fv == m1, lane, big), axis=-1, keepdims=True)
    fv2 = jnp.where(lane == i1, neg, fv)
    m2 = jnp.max(fv2, axis=-1, keepdims=True)
    i2 = jnp.min(jnp.where(fv2 == m2, lane, big), axis=-1, keepdims=True)
    e2 = jnp.exp(m2 - m1)
    w1 = gprob / (1.0 + e2)
    w2 = gprob * e2 / (1.0 + e2)
    gate_o[...] = (jnp.where(lane == i1, w1, 0.0) + jnp.where(lane == i2, w2, 0.0)
                   + jnp.where(lane == GROUP_LANE, grp.astype(F32), 0.0))
    onehot = jnp.where((lane == grp) & (lane < N_GROUPS), 1.0, 0.0)
    cnt_o[0] = jnp.broadcast_to(jnp.sum(onehot, axis=0, keepdims=True), (V7X_SUBLANES, ROUTER_LANES)).astype(jnp.int32)


POST_OPERANDS = 6


def _post_kernel(group_tiles, *refs):
    per_group, shared = refs[:POST_OPERANDS * len(group_tiles)], refs[POST_OPERANDS * len(group_tiles):]
    i = pl.program_id(0)
    start = 0
    for k, tiles in enumerate(group_tiles):
        @pl.when((i >= start) & (i < start + tiles))
        def _(k=k):
            _post_tile(*per_group[POST_OPERANDS * k:POST_OPERANDS * (k + 1)], *shared)

        start += tiles


def _group_rows(tm, width, start, tiles):
    return pl.BlockSpec((tm, width), lambda i, *_: (jnp.clip(i - start, 0, tiles - 1), 0))


def _post(groups, W):
    tm = POST_ROWS
    group_tiles = tuple(g[-1].shape[0] // tm for g in groups)
    assert all(g[-1].shape[0] % tm == 0 for g in groups)
    n = tm * sum(group_tiles)

    def full(a):
        return pl.BlockSpec(a.shape, lambda i: (0,) * a.ndim)

    def rows(width):
        return pl.BlockSpec((tm, width), lambda i: (i, 0))

    widths = [A_WIDTH, B_WIDTH, A_WIDTH, A_WIDTH, B_WIDTH, D_MODEL]
    in_specs, start = [], 0
    for tiles in group_tiles:
        in_specs += [_group_rows(tm, w, start, tiles) for w in widths]
        start += tiles
    params = [W["lnx_g"], W["lnx_b"], W["ret_gn_g"], W["ret_gn_b"], W["w_out"], W["ln1_g"], W["ln1_b"],
              W["w_router"], W["b_router"]]
    return pl.pallas_call(
        functools.partial(_post_kernel, group_tiles),
        grid=(n // tm,),
        in_specs=in_specs + [full(a) for a in params],
        out_specs=[rows(D_MODEL), rows(ROUTER_LANES),
                   pl.BlockSpec((1, V7X_SUBLANES, ROUTER_LANES), lambda i: (i, 0, 0))],
        out_shape=[jax.ShapeDtypeStruct((n, D_MODEL), F32), jax.ShapeDtypeStruct((n, ROUTER_LANES), F32),
                   jax.ShapeDtypeStruct((n // tm, V7X_SUBLANES, ROUTER_LANES), jnp.int32)],
        compiler_params=pltpu.CompilerParams(dimension_semantics=("parallel",),
                                             vmem_limit_bytes=V7X_VMEM_LIMIT_BYTES),
        name="post",
    )(*[a for g in groups for a in g], *params)


def _sort_positions(gate, lofs):
    tm = gate.shape[0]
    lane = lax.broadcasted_iota(jnp.int32, (tm, ROUTER_LANES), 1)
    grp = gate[:, GROUP_LANE:GROUP_LANE + 1].astype(jnp.int32)
    onehot = jnp.where((lane == grp) & (lane < N_GROUPS), 1.0, 0.0)
    r = lax.broadcasted_iota(jnp.int32, (tm, tm), 0)
    c = lax.broadcasted_iota(jnp.int32, (tm, tm), 1)
    earlier = jnp.where(c < r, 1.0, 0.0).astype(BF16)
    prefix = jnp.dot(earlier, onehot.astype(BF16), preferred_element_type=F32)
    base = jnp.zeros((tm, ROUTER_LANES), F32)
    for g in range(N_GROUPS):
        base = jnp.where(lane == g, lofs[g].astype(F32), base)
    return jnp.sum(onehot * (base + prefix), axis=1, keepdims=True).astype(jnp.int32)


def _piece_copies(action, rows, bits, copy_of):
    k = rows // SORT_ALIGN
    for b in reversed(range(bits)):
        size = SORT_ALIGN << b

        @pl.when(((k >> b) & 1) == 1)
        def _():
            done = ((k >> (b + 1)) << (b + 1)) * SORT_ALIGN
            cp = copy_of(done, size)
            cp.start() if action == "start" else cp.wait()


def _run_copies(action, plan_ref, i, src_of, dst_of, sem):
    for g in range(N_GROUPS):
        goff = plan_ref[i, g]
        lofs = plan_ref[i, 2 * N_GROUPS + g]

        def copy_of(done, size, goff=goff, lofs=lofs):
            lo = pl.multiple_of(lofs + done, SORT_ALIGN)
            go = pl.multiple_of(goff + done, SORT_ALIGN)
            return pltpu.make_async_copy(src_of(lo, go, size), dst_of(lo, go, size), sem)

        _piece_copies(action, plan_ref[i, N_GROUPS + g], RUN_BITS, copy_of)


def _dispatch_kernel(plan_ref, gap_ref, h_ref, gate_ref, hs_o, h_loc, sem):
    i = pl.program_id(0)
    tm = h_ref.shape[0]
    lofs = [plan_ref[i, 2 * N_GROUPS + g] for g in range(N_GROUPS)]
    pos = _sort_positions(gate_ref[...], lofs)
    onehot_t = jnp.where(lax.broadcasted_iota(jnp.int32, (tm, SORT_LOCAL), 1) == pos, 1.0, 0.0).astype(BF16)
    idx = lax.broadcasted_iota(jnp.int32, (V7X_SUBLANES, SORT_LOCAL), 1)
    pos_row = (_dot_nt((idx // V7X_LANES).astype(F32), onehot_t) * float(V7X_LANES)
               + _dot_nt((idx % V7X_LANES).astype(F32), onehot_t))[0:1].astype(jnp.int32)
    perm = jnp.where(lax.broadcasted_iota(jnp.int32, (SORT_LOCAL, tm), 0) == pos_row, 1.0, 0.0).astype(BF16)
    payload = jnp.concatenate([h_ref[...].astype(BF16)] + list(_split3(gate_ref[...])), axis=1)
    sorted_rows = jnp.dot(perm, payload, preferred_element_type=F32).astype(BF16)

    def copies(action, tile):
        _run_copies(action, plan_ref, tile, lambda lo, go, sz: h_loc.at[pl.ds(lo, sz)],
                    lambda lo, go, sz: hs_o.at[pl.ds(go, sz)], sem)

    @pl.when(i > 0)
    def _():
        copies("wait", i - 1)

    h_loc[...] = sorted_rows
    copies("start", i)

    @pl.when(i == pl.num_programs(0) - 1)
    def _():
        copies("wait", i)
        h_loc[...] = jnp.zeros_like(h_loc)
        for action in ("start", "wait"):
            for g in range(N_GROUPS):
                def copy_of(done, size, g=g):
                    go = pl.multiple_of(gap_ref[g] + done, SORT_ALIGN)
                    return pltpu.make_async_copy(h_loc.at[pl.ds(0, size)], hs_o.at[pl.ds(go, size)], sem)

                _piece_copies(action, gap_ref[N_GROUPS + g], GAP_BITS, copy_of)

            def body(k, carry):
                go = pl.multiple_of(gap_ref[2 * N_GROUPS] + k * EXPERT_ROWS_SHORT, SORT_ALIGN)
                cp = pltpu.make_async_copy(h_loc.at[pl.ds(0, EXPERT_ROWS_SHORT)], hs_o.at[pl.ds(go, EXPERT_ROWS_SHORT)], sem)
                cp.start() if action == "start" else cp.wait()
                return carry

            lax.fori_loop(0, gap_ref[2 * N_GROUPS + 1] // EXPERT_ROWS_SHORT, body, 0)


def _experts_kernel(tile_group_ref, n_valid_ref, hs_ref, w1_ref, w3_ref, w2_ref, ys_o, w1_b, w3_b, w2_b):
    j = pl.program_id(0)
    last = n_valid_ref[0] - 1
    g = tile_group_ref[jnp.minimum(j, last)]
    g_prev = tile_group_ref[jnp.minimum(jnp.maximum(j - 1, 0), last)]

    @pl.when((j == 0) | (g != g_prev))
    def _():
        w1_b[...] = w1_ref[...].astype(BF16)
        w3_b[...] = w3_ref[...].astype(BF16)
        w2_b[...] = w2_ref[...].astype(BF16)

    @pl.when(j <= last)
    def _():
        x = hs_ref[:, :D_MODEL]
        gs = sum(hs_ref[:, D_MODEL + t * ROUTER_LANES:D_MODEL + (t + 1) * ROUTER_LANES].astype(F32) for t in range(3))
        lane = lax.broadcasted_iota(jnp.int32, gs.shape, 1)
        acc = jnp.zeros(ys_o.shape, F32)
        for e in range(EXPERTS_PER_GROUP):
            ge = jnp.sum(jnp.where(lane == FINE_LANE0 + g * EXPERTS_PER_GROUP + e, gs, 0.0), axis=-1, keepdims=True)
            a = jnp.dot(x, w1_b[e], preferred_element_type=F32)
            b = jnp.dot(x, w3_b[e], preferred_element_type=F32)
            hid = (a * _sigmoid(a)) * b * ge
            acc = acc + jnp.dot(hid.astype(BF16), w2_b[e], preferred_element_type=F32)
        ys_o[...] = acc.astype(ys_o.dtype)

    @pl.when(j > last)
    def _():
        ys_o[...] = jnp.zeros_like(ys_o)


def _combine_kernel(group_tiles, plan_ref, h_ref, gate_ref, *refs):
    ng = len(group_tiles)
    p_refs, (ys_ref, ln2g_ref, ln2b_ref, wple_ref, wpg_ref, pleg_ref) = refs[:ng], refs[ng:ng + 6]
    o_refs, (y_loc, sem) = refs[ng + 6:2 * ng + 6], refs[2 * ng + 6:]
    i = pl.program_id(0)
    starts = [sum(group_tiles[:k]) for k in range(ng)]

    def in_group(k):
        return (i >= starts[k]) & (i < starts[k] + group_tiles[k])

    tm = h_ref.shape[0]
    lofs = [plan_ref[i, 2 * N_GROUPS + g] for g in range(N_GROUPS)]

    def fetch(tile):
        slot = tile % 2
        y_loc[slot] = jnp.zeros(y_loc.shape[1:], y_loc.dtype)
        _run_copies("start", plan_ref, tile, lambda lo, go, sz: ys_ref.at[pl.ds(go, sz)],
                    lambda lo, go, sz: y_loc.at[slot, pl.ds(lo, sz)], sem.at[slot])

    @pl.when(i == 0)
    def _():
        fetch(i)

    @pl.when(i + 1 < pl.num_programs(0))
    def _():
        fetch(i + 1)

    pos = _sort_positions(gate_ref[...], lofs)
    onehot_t = jnp.where(lax.broadcasted_iota(jnp.int32, (tm, SORT_LOCAL), 1) == pos, 1.0, 0.0).astype(BF16)
    p_tile = p_refs[0][...]
    for k in range(1, ng):
        p_tile = jnp.where(in_group(k), p_refs[k][...], p_tile)
    ple_in = _dot(p_tile, wple_ref[...])
    slot = i % 2
    _run_copies("wait", plan_ref, i, lambda lo, go, sz: ys_ref.at[pl.ds(go, sz)],
                lambda lo, go, sz: y_loc.at[slot, pl.ds(lo, sz)], sem.at[slot])
    ffn = jnp.dot(onehot_t, y_loc[slot], preferred_element_type=F32)
    h2 = _layer_norm(DEEPNORM_ALPHA * h_ref[...] + ffn, ln2g_ref[...], ln2b_ref[...])
    ple = ple_in * _sigmoid(_dot(h2, wpg_ref[...]))
    ms = jnp.mean(ple * ple, axis=-1, keepdims=True)
    out = h2 + ple * lax.rsqrt(ms + LN_EPS) * pleg_ref[...]
    for k in range(ng):
        @pl.when(in_group(k))
        def _(k=k):
            o_refs[k][...] = out


def _ffn(h, gate, counts, p_groups, W):
    n = h.shape[0]
    tm = SORT_ROWS
    assert n % tm == 0 and tm == POST_ROWS and SORT_LOCAL >= tm + N_GROUPS * SORT_ALIGN
    ntiles = n // tm
    er = EXPERT_ROWS if n >= 2 * N_GROUPS * EXPERT_ROWS else EXPERT_ROWS_SHORT
    assert er <= SORT_ALIGN << GAP_BITS and SORT_ALIGN << (GAP_BITS - 1) <= SORT_LOCAL
    cnt = counts[:, 0, :N_GROUPS]
    run = (cnt + SORT_ALIGN - 1) // SORT_ALIGN * SORT_ALIGN
    lofs = jnp.cumsum(run, axis=1) - run
    seg = (jnp.sum(run, axis=0) + er - 1) // er * er
    gbase = jnp.cumsum(seg) - seg
    goff = gbase[None, :] + jnp.cumsum(run, axis=0) - run
    plan = jnp.concatenate([goff, run, lofs], axis=1).astype(jnp.int32)
    max_tiles = (n + ntiles * N_GROUPS * (SORT_ALIGN - 1)) // er + N_GROUPS
    cap = max_tiles * er
    n_valid = (jnp.sum(seg) // er).astype(jnp.int32).reshape(1)
    tile_start = jnp.arange(max_tiles, dtype=jnp.int32) * er
    tile_group = jnp.clip(jnp.sum(tile_start[:, None] >= (gbase + seg)[None, :], axis=1), 0, N_GROUPS - 1).astype(jnp.int32)

    cparams = dict(vmem_limit_bytes=V7X_VMEM_LIMIT_BYTES)
    any_spec = pl.BlockSpec(memory_space=pl.ANY)
    total = jnp.sum(run, axis=0)
    assert er % EXPERT_ROWS_SHORT == 0 and EXPERT_ROWS_SHORT <= SORT_LOCAL
    used = jnp.sum(seg)
    gaps = jnp.concatenate([gbase + total, seg - total, jnp.stack([used, cap - used])]).astype(jnp.int32)
    hs = pl.pallas_call(
        _dispatch_kernel,
        grid_spec=pltpu.PrefetchScalarGridSpec(
            num_scalar_prefetch=2, grid=(ntiles,),
            in_specs=[pl.BlockSpec((tm, D_MODEL), lambda i, plan, gaps: (i, 0)),
                      pl.BlockSpec((tm, ROUTER_LANES), lambda i, plan, gaps: (i, 0))],
            out_specs=any_spec,
            scratch_shapes=[pltpu.VMEM((SORT_LOCAL, SORTED_WIDTH), BF16), pltpu.SemaphoreType.DMA(())]),
        out_shape=jax.ShapeDtypeStruct((cap, SORTED_WIDTH), BF16),
        compiler_params=pltpu.CompilerParams(dimension_semantics=("arbitrary",), **cparams),
        name="dispatch",
    )(plan, gaps, h, gate)

    def tile_rows(width):
        return pl.BlockSpec((er, width), lambda j, tg, nv: (jnp.minimum(j, nv[0] - 1), 0))

    def group_w(shape):
        return pl.BlockSpec((EXPERTS_PER_GROUP,) + shape, lambda j, tg, nv: (tg[jnp.minimum(j, nv[0] - 1)], 0, 0))

    ys = pl.pallas_call(
        _experts_kernel,
        grid_spec=pltpu.PrefetchScalarGridSpec(
            num_scalar_prefetch=2, grid=(max_tiles,),
            in_specs=[tile_rows(SORTED_WIDTH), group_w((D_MODEL, D_EXPERT)),
                      group_w((D_MODEL, D_EXPERT)), group_w((D_EXPERT, D_MODEL))],
            out_specs=pl.BlockSpec((er, D_MODEL), lambda j, tg, nv: (j, 0)),
            scratch_shapes=[pltpu.VMEM((EXPERTS_PER_GROUP, D_MODEL, D_EXPERT), BF16),
                            pltpu.VMEM((EXPERTS_PER_GROUP, D_MODEL, D_EXPERT), BF16),
                            pltpu.VMEM((EXPERTS_PER_GROUP, D_EXPERT, D_MODEL), BF16)]),
        out_shape=jax.ShapeDtypeStruct((cap, D_MODEL), BF16),
        compiler_params=pltpu.CompilerParams(dimension_semantics=("arbitrary",), **cparams),
        name="experts",
    )(tile_group, n_valid, hs, W["expert_w1"], W["expert_w3"], W["expert_w2"])

    def full(a):
        return pl.BlockSpec(a.shape, lambda i, plan: (0,) * a.ndim)

    params = [W["ln2_g"], W["ln2_b"], W["w_ple"], W["w_ple_gate"], W["ple_norm_g"]]
    group_tiles = tuple(pg.shape[0] // tm for pg in p_groups)
    assert sum(group_tiles) == ntiles
    starts = [sum(group_tiles[:k]) for k in range(len(group_tiles))]
    return pl.pallas_call(
        functools.partial(_combine_kernel, group_tiles),
        grid_spec=pltpu.PrefetchScalarGridSpec(
            num_scalar_prefetch=1, grid=(ntiles,),
            in_specs=[pl.BlockSpec((tm, D_MODEL), lambda i, plan: (i, 0)),
                      pl.BlockSpec((tm, ROUTER_LANES), lambda i, plan: (i, 0))]
            + [_group_rows(tm, D_PLE, st, t) for st, t in zip(starts, group_tiles)]
            + [any_spec] + [full(a) for a in params],
            out_specs=[_group_rows(tm, D_MODEL, st, t) for st, t in zip(starts, group_tiles)],
            scratch_shapes=[pltpu.VMEM((2, SORT_LOCAL, D_MODEL), BF16), pltpu.SemaphoreType.DMA((2,))]),
        out_shape=[jax.ShapeDtypeStruct((t * tm, D_MODEL), F32) for t in group_tiles],
        compiler_params=pltpu.CompilerParams(dimension_semantics=("arbitrary",), **cparams),
        name="combine",
    )(plan, h, gate, *p_groups, ys, *params)


def _mix(x, x_prev, wkv0, ret0, pos0, W):
    n_seq, seq_len, _ = x.shape
    x2 = x.reshape(n_seq * seq_len, D_MODEL)
    r, lw, k, v, al, be, g, bonus, qb, kb, vb, gb = _proj(x2, x_prev, seq_len, pos0, W)
    ya, ob, wkv1, ret1 = _mixer((r, lw, k, v, al, be, qb, kb, vb), wkv0, ret0, n_seq, seq_len)
    return (ya, ob, bonus, g, gb, x2), wkv1, ret1


def _prep_weights(i, w_in, mu_shift, w_decay_up, decay_base, w_aaa_up, aaa_base, w_gate_up, k_k, k_a, r_k,
                  lnx_g, lnx_b, ret_gn_g, ret_gn_b, w_out, ln1_g, ln1_b,
                  router_coarse_w, router_coarse_b, router_fine_w, router_fine_b,
                  expert_w1, expert_w3, expert_w2, ln2_g, ln2_b, w_ple, w_ple_gate, ple_norm_g):
    def row(a):
        return a[i].reshape(1, -1).astype(F32)

    pad = ROUTER_LANES - N_GROUPS - N_EXPERTS
    w_router = jnp.concatenate([router_coarse_w[i], router_fine_w[i], jnp.zeros((D_MODEL, pad), F32)], axis=1)
    b_router = jnp.concatenate([router_coarse_b[i], router_fine_b[i], jnp.zeros((pad,), F32)]).reshape(1, -1)
    return {
        "w_in": w_in[i].astype(BF16), "mu_shift": row(mu_shift), "w_decay_up": w_decay_up[i].astype(BF16),
        "decay_base": row(decay_base), "w_aaa_up": w_aaa_up[i].astype(BF16), "aaa_base": row(aaa_base),
        "w_gate_up": w_gate_up[i].astype(BF16), "k_k": row(k_k), "k_a": row(k_a), "r_k": row(r_k),
        "lnx_g": row(lnx_g), "lnx_b": row(lnx_b), "ret_gn_g": row(ret_gn_g), "ret_gn_b": row(ret_gn_b),
        "w_out": w_out[i].astype(BF16), "ln1_g": row(ln1_g), "ln1_b": row(ln1_b),
        "w_router": w_router, "b_router": b_router,
        "expert_w1": expert_w1[i], "expert_w3": expert_w3[i], "expert_w2": expert_w2[i], "ln2_g": row(ln2_g), "ln2_b": row(ln2_b),
        "w_ple": w_ple[i].astype(BF16), "w_ple_gate": w_ple_gate[i].astype(BF16), "ple_norm_g": row(ple_norm_g),
    }


def kernel(x_prompt, x_sample, p_prompt, p_sample, state_wkv, state_shift, state_ret, w_in, mu_shift, w_decay_up, decay_base, w_aaa_up, aaa_base, w_gate_up, k_k, k_a, r_k, lnx_g, lnx_b, ret_gn_g, ret_gn_b, w_out, ln1_g, ln1_b, router_coarse_w, router_coarse_b, router_fine_w, router_fine_b, expert_w1, expert_w3, expert_w2, ln2_g, ln2_b, w_ple, w_ple_gate, ple_norm_g):
    yp, ys = x_prompt, x_sample
    nb = x_prompt.shape[0]
    depth = w_in.shape[0]
    wkv_p, shift_p, ret_p, wkv_s, shift_s, ret_s = [], [], [], [], [], []
    for i in range(depth):
        W = _prep_weights(i, w_in, mu_shift, w_decay_up, decay_base, w_aaa_up, aaa_base, w_gate_up, k_k, k_a, r_k,
                          lnx_g, lnx_b, ret_gn_g, ret_gn_b, w_out, ln1_g, ln1_b,
                          router_coarse_w, router_coarse_b, router_fine_w, router_fine_b,
                          expert_w1, expert_w3, expert_w2, ln2_g, ln2_b, w_ple, w_ple_gate, ple_norm_g)
        ops_p, wp, rp = _mix(yp, jnp.zeros((nb, D_MODEL), F32), jnp.zeros((nb, A_HEADS, A_HEAD_DIM, A_HEAD_DIM), F32),
                             jnp.zeros((nb, B_HEADS, B_QK_DIM, B_V_DIM), F32), 0, W)
        ops_s, wsm, rsm = _mix(ys, state_shift[i], state_wkv[i], state_ret[i], PAST_LEN, W)
        sp, ss = yp[:, -1], ys[:, -1]
        h, gate, counts = _post([ops_p, ops_s], W)
        out_p, out_s = _ffn(h, gate, counts, [p_prompt[i].reshape(-1, D_PLE), p_sample[i].reshape(-1, D_PLE)], W)
        yp, ys = out_p.reshape(yp.shape), out_s.reshape(ys.shape)
        wkv_p.append(wp); shift_p.append(sp); ret_p.append(rp)
        wkv_s.append(wsm); shift_s.append(ss); ret_s.append(rsm)
    return (yp, ys, jnp.stack(wkv_p, 0), jnp.stack(shift_p, 0), jnp.stack(ret_p, 0),
            jnp.stack(wkv_s, 0), jnp.stack(shift_s, 0), jnp.stack(ret_s, 0))
```

```python
import functools
import math

import numpy as np
import jax
import jax.numpy as jnp
from jax import lax
from jax.experimental import pallas as pl
from jax.experimental.pallas import tpu as pltpu

F32 = jnp.float32
BF16 = jnp.bfloat16

D_MODEL = 1024
D_PLE = 256
A_HEADS = 8
A_HEAD_DIM = 64
A_WIDTH = A_HEADS * A_HEAD_DIM
DECAY_LORA = 64
AAA_LORA = 64
GATE_LORA = 128
GN_EPS_RWKV = 64e-5
B_HEADS = 4
B_QK_DIM = 64
B_V_DIM = 128
B_QK_WIDTH = B_HEADS * B_QK_DIM
B_WIDTH = B_HEADS * B_V_DIM
ROPE_BASE = 10000.0
GN_EPS = 1e-5
SHIFT_WIDTH = 3 * A_WIDTH + DECAY_LORA + AAA_LORA + GATE_LORA
IN_WIDTH = SHIFT_WIDTH + 2 * B_QK_WIDTH + 2 * B_WIDTH
N_GROUPS = 4
EXPERTS_PER_GROUP = 4
N_EXPERTS = N_GROUPS * EXPERTS_PER_GROUP
D_EXPERT = 256
DEPTH = 1
PAST_LEN = 16384
DEEPNORM_ALPHA = (2 * DEPTH) ** 0.25
LN_EPS = 1e-5

V7X_LANES = 128
V7X_SUBLANES = 8
V7X_VMEM_LIMIT_BYTES = 56 * 1024 * 1024

PROJ_ROWS = 1024
LOG_DECAY_OUT = 1
CHUNK_ROWS = 64
MIXER_TILES = 4
MIXER_TILES_PACKED = 2
POST_ROWS = 512
SORT_ROWS = 512
SORT_ALIGN = 16
SORT_LOCAL = 640
EXPERT_ROWS = 1024
EXPERT_ROWS_SHORT = 256
GAP_BITS = 6
RUN_BITS = 6
GROUP_LANE = 0
ROUTER_LANES = V7X_LANES
FINE_LANE0 = N_GROUPS
SORTED_WIDTH = D_MODEL + 3 * ROUTER_LANES


def _dot(a, b):
    return jnp.dot(a.astype(BF16), b.astype(BF16), preferred_element_type=F32)


def _dot_nt(a, b):
    return lax.dot_general(a.astype(BF16), b.astype(BF16), (((1,), (1,)), ((), ())), preferred_element_type=F32)


def _dot_tn(a, b):
    return lax.dot_general(a.astype(BF16), b.astype(BF16), (((0,), (0,)), ((), ())), preferred_element_type=F32)


def _split2(x):
    hi = x.astype(BF16)
    lo = (x - hi.astype(F32)).astype(BF16)
    return hi, lo


def _split3(x):
    hi = x.astype(BF16)
    r1 = x - hi.astype(F32)
    mid = r1.astype(BF16)
    lo = (r1 - mid.astype(F32)).astype(BF16)
    return hi, mid, lo


def _sigmoid(x):
    return 1.0 / (1.0 + jnp.exp(-x))


def _group_ones(width, group):
    r = lax.broadcasted_iota(jnp.int32, (width, width), 0) // group
    c = lax.broadcasted_iota(jnp.int32, (width, width), 1) // group
    return jnp.where(r == c, 1.0, 0.0).astype(BF16)


def _group_sum(x, ones):
    return jnp.dot(x.astype(BF16), ones, preferred_element_type=F32)


def _proj_kernel(carry_mode, seq_len, tiles_per_seq,
                 x_ref, xp_ref, w_ref, mu_ref, wdec_ref, dbase_ref, waaa_ref, abase_ref, wgate_ref,
                 kk_ref, ka_ref, rk_ref, cos_ref, sin_ref,
                 r_o, lw_o, k_o, v_o, al_o, be_o, g_o, bonus_o, qb_o, kb_o, vb_o, gb_o,
                 carry_scr):
    tm = x_ref.shape[0]
    if carry_mode:
        xp = jnp.broadcast_to(xp_ref[0], (V7X_SUBLANES, D_MODEL))
        xb = jnp.concatenate([x_ref[...], xp], axis=0).astype(BF16)
        j = pl.program_id(0) % tiles_per_seq

        @pl.when(pl.program_id(0) == 0)
        def _():
            carry_scr[...] = jnp.zeros_like(carry_scr)
    else:
        xb = x_ref[...].astype(BF16)
        xpb = xp_ref[...].astype(BF16)

    def project(lo, hi):
        return jnp.dot(xb, w_ref[:, lo:hi], preferred_element_type=F32)

    def shifted(p, lo, hi):
        cur = p[:tm]
        row = lax.broadcasted_iota(jnp.int32, cur.shape, 0)
        rolled = pltpu.roll(cur, 1, 0)
        if carry_mode:
            first = jnp.where(j == 0, p[tm + V7X_SUBLANES - 1:], carry_scr[V7X_SUBLANES - 1:V7X_SUBLANES, lo:hi])
            prev = jnp.where(row == 0, first, rolled)
            carry_scr[:, lo:hi] = cur[tm - V7X_SUBLANES:]
        else:
            first = jnp.dot(xpb, w_ref[:, lo:hi], preferred_element_type=F32)
            prev = jnp.where((row & (seq_len - 1)) == 0, first, rolled)
        return cur + (prev - cur) * mu_ref[:, lo:hi]

    c_r, c_k, c_v, c_l = 0, A_WIDTH, 2 * A_WIDTH, 3 * A_WIDTH
    p_lora = project(c_l, SHIFT_WIDTH)
    p_k = project(c_k, c_v)
    p_r = project(c_r, c_k)

    lora = shifted(p_lora, c_l, SHIFT_WIDTH)
    w_lo = lora[:, :DECAY_LORA]
    a_lo = lora[:, DECAY_LORA:DECAY_LORA + AAA_LORA]
    g_lo = lora[:, DECAY_LORA + AAA_LORA:]
    z = -(dbase_ref[...] + _dot(jnp.tanh(w_lo), wdec_ref[...]))
    softplus = jnp.maximum(z, 0.0) + jnp.log(1.0 + jnp.exp(-jnp.abs(z)))
    log_w = -softplus - 0.5
    lw_o[...] = -jnp.exp(log_w)
    a = _sigmoid(abase_ref[...] + _dot(a_lo, waaa_ref[...]))
    g_o[...] = (_dot(_sigmoid(g_lo), wgate_ref[...])).astype(g_o.dtype)

    p_v = project(c_v, c_l)

    ones64 = _group_ones(A_WIDTH, A_HEAD_DIM)
    k0 = shifted(p_k, c_k, c_v)
    kk0 = k0 * kk_ref[...]
    ssq = _group_sum(kk0 * kk0, ones64)
    kk = kk0 * jnp.minimum(lax.rsqrt(ssq), 1e12)
    k = k0 * (1.0 + (a - 1.0) * ka_ref[...])
    k_o[...] = (k).astype(k_o.dtype)
    al_o[...] = (-kk).astype(al_o.dtype)
    be_o[...] = (kk * a).astype(be_o.dtype)

    o = SHIFT_WIDTH
    p_qk = project(o, o + 2 * B_QK_WIDTH)[:tm]

    r = shifted(p_r, c_r, c_k)
    r_o[...] = (r).astype(r_o.dtype)
    rk_sum = _group_sum(r * k * rk_ref[...], ones64)

    p_vb = project(o + 2 * B_QK_WIDTH, o + 2 * B_QK_WIDTH + B_WIDTH)[:tm]

    v = shifted(p_v, c_v, c_l)
    v_o[...] = (v).astype(v_o.dtype)
    bonus_o[...] = (rk_sum * v).astype(bonus_o.dtype)

    p_gb = project(o + 2 * B_QK_WIDTH + B_WIDTH, IN_WIDTH)[:tm]

    q_b = p_qk[:, :B_QK_WIDTH]
    k_b = p_qk[:, B_QK_WIDTH:]
    lane = lax.broadcasted_iota(jnp.int32, (tm, B_QK_WIDTH), 1)
    first_half = (lane & (B_QK_DIM - 1)) < (B_QK_DIM // 2)
    cos = cos_ref[...]
    sin = sin_ref[...]

    def rot(t):
        swapped = jnp.where(first_half, pltpu.roll(t, B_QK_WIDTH - B_QK_DIM // 2, 1), pltpu.roll(t, B_QK_DIM // 2, 1))
        return t * cos + swapped * sin

    qb_o[...] = (rot(q_b)).astype(qb_o.dtype)
    kb_o[...] = (rot(k_b) * (B_QK_DIM ** -0.5)).astype(kb_o.dtype)

    vb_o[...] = (p_vb).astype(vb_o.dtype)
    gb_o[...] = (p_gb * _sigmoid(p_gb)).astype(gb_o.dtype)


def _proj(x2, x_prev, seq_len, pos0, W):
    n = x2.shape[0]
    tm = PROJ_ROWS
    assert n % tm == 0
    carry_mode = seq_len % tm == 0
    if carry_mode:
        tiles_per_seq = seq_len // tm
        xp = x_prev.reshape(-1, 1, D_MODEL)
        xp_spec = pl.BlockSpec((1, 1, D_MODEL), lambda i: (i // tiles_per_seq, 0, 0))
        tab_rows = seq_len
    else:
        assert tm % seq_len == 0 and seq_len & (seq_len - 1) == 0
        tiles_per_seq = 1
        xp = jnp.repeat(x_prev, seq_len, axis=0)
        xp_spec = pl.BlockSpec((tm, D_MODEL), lambda i: (i, 0))
        tab_rows = tm
    half = B_QK_DIM // 2
    inv = ROPE_BASE ** (-jnp.arange(half, dtype=F32) / half)
    pos = (pos0 + jnp.arange(seq_len, dtype=jnp.int32)).astype(F32)
    ang = pos[:, None] * inv[None, :]
    cos = jnp.tile(jnp.concatenate([jnp.cos(ang), jnp.cos(ang)], -1), (tab_rows // seq_len, B_HEADS))
    sin = jnp.tile(jnp.concatenate([-jnp.sin(ang), jnp.sin(ang)], -1), (tab_rows // seq_len, B_HEADS))
    tab_tiles = tab_rows // tm
    tab_spec = pl.BlockSpec((tm, B_QK_WIDTH), lambda i: (i % tab_tiles, 0))

    def full(a):
        return pl.BlockSpec(a.shape, lambda i: (0,) * a.ndim)

    def rows(width):
        return pl.BlockSpec((tm, width), lambda i: (i, 0))

    params = [W["w_in"], W["mu_shift"], W["w_decay_up"], W["decay_base"], W["w_aaa_up"], W["aaa_base"],
              W["w_gate_up"], W["k_k"], W["k_a"], W["r_k"]]
    widths = [A_WIDTH] * 8 + [B_QK_WIDTH, B_QK_WIDTH, B_WIDTH, B_WIDTH]
    outs = pl.pallas_call(
        functools.partial(_proj_kernel, carry_mode, seq_len, tiles_per_seq),
        grid=(n // tm,),
        in_specs=[rows(D_MODEL), xp_spec, pl.BlockSpec(params[0].shape, lambda i: (0, 0), pipeline_mode=pl.Buffered(1))]
        + [full(a) for a in params[1:]] + [tab_spec, tab_spec],
        out_specs=[rows(w) for w in widths],
        out_shape=[jax.ShapeDtypeStruct((n, w), F32 if i == LOG_DECAY_OUT else BF16) for i, w in enumerate(widths)],
        scratch_shapes=[pltpu.VMEM((V7X_SUBLANES, SHIFT_WIDTH), F32)],
        compiler_params=pltpu.CompilerParams(dimension_semantics=("arbitrary",),
                                             vmem_limit_bytes=V7X_VMEM_LIMIT_BYTES),
        name="proj",
    )(x2, xp, *params, cos, sin)
    return outs


def _mixer_kernel(nb, seqs, clen,
                  r_ref, lw_ref, k_ref, v_ref, al_ref, be_ref, qb_ref, kb_ref, vb_ref, wkv0_ref, ret0_ref,
                  ya_o, ob_o, wkv_o, ret_o, s_scr, r_scr):
    R = seqs * clen
    log2c = int(math.log2(clen))
    c_idx = pl.program_id(1)
    hd = A_HEAD_DIM
    TH = [(t, h) for t in range(nb) for h in range(A_HEADS)]
    TG = [(t, h) for t in range(nb) for h in range(B_HEADS)]

    @pl.when(c_idx == 0)
    def _():
        for t, h in TH:
            blocks = [wkv0_ref[t * seqs + i, h] for i in range(seqs)]
            s_scr[t * A_HEADS + h] = jnp.concatenate(blocks, axis=1) if seqs > 1 else blocks[0]
        for t, h in TG:
            r_scr[t * B_HEADS + h] = ret0_ref[t * seqs:(t + 1) * seqs, h].reshape(seqs * B_QK_DIM, B_V_DIM)

    row = lax.broadcasted_iota(jnp.int32, (R, R), 0)
    col = lax.broadcasted_iota(jnp.int32, (R, R), 1)
    same = (row >> log2c) == (col >> log2c)
    incl = same & (col <= row)
    strict = same & (col < row)
    m_incl = jnp.where(incl, 1.0, 0.0).astype(BF16)
    m_same = jnp.where(same, 1.0, 0.0).astype(BF16)
    eye = jnp.where(row == col, 1.0, 0.0).astype(F32)

    def expand(t):
        if seqs == 1:
            return t
        w = t.shape[1]
        wide = jnp.concatenate([t] * seqs, axis=1)
        rr = lax.broadcasted_iota(jnp.int32, wide.shape, 0) >> log2c
        cc = lax.broadcasted_iota(jnp.int32, wide.shape, 1) // w
        return jnp.where(rr == cc, wide, 0.0)

    def head(x, h):
        return x[:, h * hd:(h + 1) * hd]

    a_bar, r_bar, b_til, k_til, b_dec, k_dec, d_end, vv = [], [], [], [], [], [], [], []
    for t in range(nb):
        lw = lw_ref[t]
        parts = _split3(lw)
        c = sum(jnp.dot(m_incl, p, preferred_element_type=F32) for p in parts)
        cend = sum(jnp.dot(m_same, p, preferred_element_type=F32) for p in parts)
        einv = jnp.exp(-c)
        edec = jnp.exp(cend - c)
        a_bar.append(al_ref[t] * jnp.exp(c - lw))
        r_bar.append(r_ref[t] * jnp.exp(c))
        b_til.append(be_ref[t] * einv)
        k_til.append(k_ref[t] * einv)
        b_dec.append(be_ref[t] * edec)
        k_dec.append(k_ref[t] * edec)
        d_end.append(jnp.exp(cend))
        vv.append(v_ref[t].astype(F32))
    last_row = (lax.broadcasted_iota(jnp.int32, (R, seqs * hd), 0) & (clen - 1)) == clen - 1

    amats = [_dot_nt(jnp.concatenate([head(a_bar[t], h), head(r_bar[t], h)], axis=0),
                     jnp.concatenate([head(b_til[t], h), head(k_til[t], h)], axis=0)) for t, h in TH]

    lgs = [float(np.log1p(-np.exp2(-5.0 - h))) for h in range(B_HEADS)]
    qb = [qb_ref[t].astype(F32) for t in range(nb)]
    kb = [kb_ref[t].astype(F32) for t in range(nb)]
    qs = [qb[t][:, h * B_QK_DIM:(h + 1) * B_QK_DIM] for t, h in TG]
    khs = [kb[t][:, h * B_QK_DIM:(h + 1) * B_QK_DIM] for t, h in TG]
    vbs = [vb_ref[t][:, h * B_V_DIM:(h + 1) * B_V_DIM] for t, h in TG]
    diff = (row - col).astype(F32)
    pos_v = (lax.broadcasted_iota(jnp.int32, (R, B_V_DIM), 0) & (clen - 1)).astype(F32)
    pos_k = (lax.broadcasted_iota(jnp.int32, (R, B_QK_DIM), 0) & (clen - 1)).astype(F32)
    intra = [jnp.where(incl, jnp.exp(lg * diff), 0.0) for lg in lgs]
    cross = [jnp.exp(lg * (pos_v + 1.0)) for lg in lgs]
    kdec = [jnp.exp(lg * (clen - 1.0 - pos_k)) for lg in lgs]
    rstates = [r_scr[t * B_HEADS + h] for t, h in TG]
    scs = [_dot_nt(qs[i], khs[i]) * intra[h] for i, (t, h) in enumerate(TG)]
    qst = [_dot(expand(qs[i]), rstates[i]) * cross[h] for i, (t, h) in enumerate(TG)]

    a_ab = [jnp.where(strict, m[:R, :R], 0.0) for m in amats]
    a_ak = [jnp.where(strict, m[:R, R:], 0.0) for m in amats]
    a_rb = [jnp.where(incl, m[R:, :R], 0.0) for m in amats]
    a_rk = [jnp.where(incl, m[R:, R:], 0.0) for m in amats]
    n = len(TH)
    tinv = [eye + a for a in a_ab]
    vhs = [head(vv[t], h) for t, h in TH]
    av = [_dot(a_ak[i], vhs[i]) for i in range(n)]
    if log2c > 1:
        pw = [_dot(a, a) for a in a_ab]
    for it in range(log2c - 1):
        if it < log2c - 2:
            tp = [_dot(pw[i], jnp.concatenate([tinv[i], pw[i]], axis=1)) for i in range(n)]
            tinv = [tinv[i] + tp[i][:, :R] for i in range(n)]
            pw = [tp[i][:, R:] for i in range(n)]
        else:
            tinv = [tinv[i] + _dot(pw[i], tinv[i]) for i in range(n)]

    os_ = [_dot(scs[i], vbs[i]) + qst[i] for i in range(len(TG))]
    for i, (t, h) in enumerate(TG):
        r_scr[t * B_HEADS + h] = (rstates[i] * float(np.exp(lgs[h] * clen))
                                  + _dot_tn(expand(khs[i] * kdec[h]), vbs[i]))
    for t in range(nb):
        ob_o[t] = jnp.concatenate(os_[t * B_HEADS:(t + 1) * B_HEADS], axis=1)

    wu = [_dot(tinv[i], jnp.concatenate([head(a_bar[t], h), av[i]], axis=1)) for i, (t, h) in enumerate(TH)]
    states = [s_scr[t * A_HEADS + h] for t, h in TH]
    ws = [_dot_nt(jnp.concatenate([expand(wu[i][:, :hd]), expand(head(r_bar[t], h))], axis=0), states[i])
          for i, (t, h) in enumerate(TH)]
    uv = [jnp.concatenate([ws[i][:R] + wu[i][:, hd:], vhs[i]], axis=0) for i in range(n)]
    ys = [ws[i][R:] + _dot(jnp.concatenate([a_rb[i], a_rk[i]], axis=1), uv[i]) for i in range(n)]
    for t in range(nb):
        ya_o[t] = jnp.concatenate(ys[t * A_HEADS:(t + 1) * A_HEADS], axis=1)
    for i, (t, h) in enumerate(TH):
        d_row = jnp.sum(jnp.where(last_row, expand(head(d_end[t], h)), 0.0), axis=0, keepdims=True)
        bk = jnp.concatenate([expand(head(b_dec[t], h)), expand(head(k_dec[t], h))], axis=0)
        s_scr[t * A_HEADS + h] = states[i] * d_row + _dot_tn(uv[i], bk)

    @pl.when(c_idx == pl.num_programs(1) - 1)
    def _():
        for t, h in TH:
            st = s_scr[t * A_HEADS + h]
            for i in range(seqs):
                wkv_o[t * seqs + i, h] = st[:, i * hd:(i + 1) * hd]
        for t, h in TG:
            ret_o[t * seqs:(t + 1) * seqs, h] = r_scr[t * B_HEADS + h].reshape(seqs, B_QK_DIM, B_V_DIM)


def _mixer(ops, wkv0, ret0, n_seq, seq_len):
    n = ops[0].shape[0]
    R = CHUNK_ROWS
    if seq_len >= R:
        assert seq_len % R == 0
        seqs, clen, nchunks = 1, R, seq_len // R
    else:
        assert R % seq_len == 0 and seq_len & (seq_len - 1) == 0 and n_seq % (R // seq_len) == 0
        seqs, clen, nchunks = R // seq_len, seq_len, 1
    ntiles = n_seq // seqs
    nb = MIXER_TILES if seqs == 1 else MIXER_TILES_PACKED
    assert ntiles % nb == 0
    ops3 = [a.reshape(ntiles, nchunks * R, a.shape[1]) for a in ops]

    def rows(width):
        return pl.BlockSpec((nb, R, width), lambda i, c: (i, c, 0))

    wkv_spec = pl.BlockSpec((nb * seqs, A_HEADS, A_HEAD_DIM, A_HEAD_DIM), lambda i, c: (i, 0, 0, 0))
    ret_spec = pl.BlockSpec((nb * seqs, B_HEADS, B_QK_DIM, B_V_DIM), lambda i, c: (i, 0, 0, 0))
    ya, ob, wkv1, ret1 = pl.pallas_call(
        functools.partial(_mixer_kernel, nb, seqs, clen),
        grid=(ntiles // nb, nchunks),
        in_specs=[rows(A_WIDTH)] * 6 + [rows(B_QK_WIDTH), rows(B_QK_WIDTH), rows(B_WIDTH), wkv_spec, ret_spec],
        out_specs=[rows(A_WIDTH), rows(B_WIDTH), wkv_spec, ret_spec],
        out_shape=[jax.ShapeDtypeStruct((ntiles, nchunks * R, A_WIDTH), F32),
                   jax.ShapeDtypeStruct((ntiles, nchunks * R, B_WIDTH), F32),
                   jax.ShapeDtypeStruct(wkv0.shape, F32), jax.ShapeDtypeStruct(ret0.shape, F32)],
        scratch_shapes=[pltpu.VMEM((nb * A_HEADS, A_HEAD_DIM, seqs * A_HEAD_DIM), F32),
                        pltpu.VMEM((nb * B_HEADS, seqs * B_QK_DIM, B_V_DIM), F32)],
        compiler_params=pltpu.CompilerParams(dimension_semantics=("parallel", "arbitrary"),
                                             vmem_limit_bytes=V7X_VMEM_LIMIT_BYTES),
        name="mixer",
    )(*ops3, wkv0, ret0)
    return ya.reshape(n, A_WIDTH), ob.reshape(n, B_WIDTH), wkv1, ret1


def _layer_norm(z, g, b):
    mu = jnp.mean(z, axis=-1, keepdims=True)
    d = z - mu
    var = jnp.mean(d * d, axis=-1, keepdims=True)
    return d * lax.rsqrt(var + LN_EPS) * g + b


def _post_tile(ya_ref, ob_ref, bonus_ref, g_ref, gb_ref, x_ref, lnxg_ref, lnxb_ref, rgg_ref, rgb_ref,
               wout_ref, ln1g_ref, ln1b_ref, wr_ref, br_ref, h_o, gate_o, cnt_o):
    tm = x_ref.shape[0]

    def head_norm(t, group, eps, gg, bb):
        ones = _group_ones(t.shape[1], group)
        mu = _group_sum(t, ones) * (1.0 / group)
        d = t - mu
        var = _group_sum(d * d, ones) * (1.0 / group)
        return d * lax.rsqrt(var + eps) * gg + bb

    y_a = (head_norm(ya_ref[...], A_HEAD_DIM, GN_EPS_RWKV, lnxg_ref[...], lnxb_ref[...]) + bonus_ref[...]) * g_ref[...]
    y_b = head_norm(ob_ref[...], B_V_DIM, GN_EPS, rgg_ref[...], rgb_ref[...]) * gb_ref[...]
    y = jnp.concatenate([y_a, y_b], axis=1)
    mix = _dot(y, wout_ref[...])
    h = _layer_norm(DEEPNORM_ALPHA * x_ref[...] + mix, ln1g_ref[...], ln1b_ref[...])
    h_o[...] = h

    h_hi, h_lo = _split2(h)
    w_hi, w_lo = _split2(wr_ref[...])
    logits = (jnp.dot(h_hi, w_hi, preferred_element_type=F32) + jnp.dot(h_hi, w_lo, preferred_element_type=F32)
              + jnp.dot(h_lo, w_hi, preferred_element_type=F32)) + br_ref[...]
    lane = lax.broadcasted_iota(jnp.int32, (tm, ROUTER_LANES), 1).astype(F32)
    neg = -jnp.inf
    big = float(ROUTER_LANES)
    cl = jnp.where(lane < N_GROUPS, logits, neg)
    cmax = jnp.max(cl, axis=-1, keepdims=True)
    grp = jnp.min(jnp.where(cl == cmax, lane, big), axis=-1, keepdims=True)
    gprob = 1.0 / jnp.sum(jnp.exp(cl - cmax), axis=-1, keepdims=True)
    lo_lane = FINE_LANE0 + grp * EXPERTS_PER_GROUP
    fv = jnp.where((lane >= lo_lane) & (lane < lo_lane + EXPERTS_PER_GROUP), logits, neg)
    m1 = jnp.max(fv, axis=-1, keepdims=True)
    i1 = jnp.min(jnp.where(fv == m1, lane, big), axis=-1, keepdims=True)
    fv2 = jnp.where(lane == i1, neg, fv)
    m2 = jnp.max(fv2, axis=-1, keepdims=True)
    i2 = jnp.min(jnp.where(fv2 == m2, lane, big), axis=-1, keepdims=True)
    e2 = jnp.exp(m2 - m1)
    w1 = gprob / (1.0 + e2)
    w2 = gprob * e2 / (1.0 + e2)
    gate_o[...] = (jnp.where(lane == i1, w1, 0.0) + jnp.where(lane == i2, w2, 0.0)
                   + jnp.where(lane == GROUP_LANE, grp, 0.0))
    onehot = jnp.where((lane == grp) & (lane < N_GROUPS), 1.0, 0.0)
    cnt_o[0] = jnp.broadcast_to(jnp.sum(onehot, axis=0, keepdims=True), (V7X_SUBLANES, ROUTER_LANES)).astype(jnp.int32)


POST_OPERANDS = 6


def _post_kernel(group_tiles, *refs):
    per_group, shared = refs[:POST_OPERANDS * len(group_tiles)], refs[POST_OPERANDS * len(group_tiles):]
    i = pl.program_id(0)
    start = 0
    for k, tiles in enumerate(group_tiles):
        @pl.when((i >= start) & (i < start + tiles))
        def _(k=k):
            _post_tile(*per_group[POST_OPERANDS * k:POST_OPERANDS * (k + 1)], *shared)

        start += tiles


def _group_rows(tm, width, start, tiles):
    return pl.BlockSpec((tm, width), lambda i, *_: (jnp.clip(i - start, 0, tiles - 1), 0))


def _post(groups, W):
    tm = POST_ROWS
    group_tiles = tuple(g[-1].shape[0] // tm for g in groups)
    assert all(g[-1].shape[0] % tm == 0 for g in groups)
    n = tm * sum(group_tiles)

    def full(a):
        return pl.BlockSpec(a.shape, lambda i: (0,) * a.ndim)

    def rows(width):
        return pl.BlockSpec((tm, width), lambda i: (i, 0))

    widths = [A_WIDTH, B_WIDTH, A_WIDTH, A_WIDTH, B_WIDTH, D_MODEL]
    in_specs, start = [], 0
    for tiles in group_tiles:
        in_specs += [_group_rows(tm, w, start, tiles) for w in widths]
        start += tiles
    params = [W["lnx_g"], W["lnx_b"], W["ret_gn_g"], W["ret_gn_b"], W["w_out"], W["ln1_g"], W["ln1_b"],
              W["w_router"], W["b_router"]]
    return pl.pallas_call(
        functools.partial(_post_kernel, group_tiles),
        grid=(n // tm,),
        in_specs=in_specs + [full(a) for a in params],
        out_specs=[rows(D_MODEL), rows(ROUTER_LANES),
                   pl.BlockSpec((1, V7X_SUBLANES, ROUTER_LANES), lambda i: (i, 0, 0))],
        out_shape=[jax.ShapeDtypeStruct((n, D_MODEL), F32), jax.ShapeDtypeStruct((n, ROUTER_LANES), F32),
                   jax.ShapeDtypeStruct((n // tm, V7X_SUBLANES, ROUTER_LANES), jnp.int32)],
        compiler_params=pltpu.CompilerParams(dimension_semantics=("parallel",),
                                             vmem_limit_bytes=V7X_VMEM_LIMIT_BYTES),
        name="post",
    )(*[a for g in groups for a in g], *params)


def _sort_positions(gate, lofs):
    tm = gate.shape[0]
    lane = lax.broadcasted_iota(jnp.int32, (tm, ROUTER_LANES), 1)
    grp = gate[:, GROUP_LANE:GROUP_LANE + 1].astype(jnp.int32)
    onehot = jnp.where((lane == grp) & (lane < N_GROUPS), 1.0, 0.0)
    r = lax.broadcasted_iota(jnp.int32, (tm, tm), 0)
    c = lax.broadcasted_iota(jnp.int32, (tm, tm), 1)
    earlier = jnp.where(c < r, 1.0, 0.0).astype(BF16)
    prefix = jnp.dot(earlier, onehot.astype(BF16), preferred_element_type=F32)
    base = jnp.zeros((tm, ROUTER_LANES), F32)
    for g in range(N_GROUPS):
        base = jnp.where(lane == g, lofs[g].astype(F32), base)
    return jnp.sum(onehot * (base + prefix), axis=1, keepdims=True).astype(jnp.int32)


def _piece_copies(action, rows, bits, copy_of):
    k = rows // SORT_ALIGN
    for b in reversed(range(bits)):
        size = SORT_ALIGN << b

        @pl.when(((k >> b) & 1) == 1)
        def _():
            done = ((k >> (b + 1)) << (b + 1)) * SORT_ALIGN
            cp = copy_of(done, size)
            cp.start() if action == "start" else cp.wait()


def _run_copies(action, plan_ref, i, src_of, dst_of, sem):
    for g in range(N_GROUPS):
        goff = plan_ref[i, g]
        lofs = plan_ref[i, 2 * N_GROUPS + g]

        def copy_of(done, size, goff=goff, lofs=lofs):
            lo = pl.multiple_of(lofs + done, SORT_ALIGN)
            go = pl.multiple_of(goff + done, SORT_ALIGN)
            return pltpu.make_async_copy(src_of(lo, go, size), dst_of(lo, go, size), sem)

        _piece_copies(action, plan_ref[i, N_GROUPS + g], RUN_BITS, copy_of)


def _dispatch_kernel(plan_ref, gap_ref, h_ref, gate_ref, hs_o, h_loc, sem):
    i = pl.program_id(0)
    tm = h_ref.shape[0]
    lofs = [plan_ref[i, 2 * N_GROUPS + g] for g in range(N_GROUPS)]
    pos = _sort_positions(gate_ref[...], lofs)
    onehot_t = jnp.where(lax.broadcasted_iota(jnp.int32, (tm, SORT_LOCAL), 1) == pos, 1.0, 0.0).astype(BF16)
    idx = lax.broadcasted_iota(jnp.int32, (V7X_SUBLANES, SORT_LOCAL), 1)
    pos_row = (_dot_nt((idx // V7X_LANES).astype(F32), onehot_t) * float(V7X_LANES)
               + _dot_nt((idx % V7X_LANES).astype(F32), onehot_t))[0:1].astype(jnp.int32)
    perm = jnp.where(lax.broadcasted_iota(jnp.int32, (SORT_LOCAL, tm), 0) == pos_row, 1.0, 0.0).astype(BF16)
    payload = jnp.concatenate([h_ref[...].astype(BF16)] + list(_split3(gate_ref[...])), axis=1)
    sorted_rows = jnp.dot(perm, payload, preferred_element_type=F32).astype(BF16)

    def copies(action, tile):
        _run_copies(action, plan_ref, tile, lambda lo, go, sz: h_loc.at[pl.ds(lo, sz)],
                    lambda lo, go, sz: hs_o.at[pl.ds(go, sz)], sem)

    @pl.when(i > 0)
    def _():
        copies("wait", i - 1)

    h_loc[...] = sorted_rows
    copies("start", i)

    @pl.when(i == pl.num_programs(0) - 1)
    def _():
        copies("wait", i)
        h_loc[...] = jnp.zeros_like(h_loc)
        for action in ("start", "wait"):
            for g in range(N_GROUPS):
                def copy_of(done, size, g=g):
                    go = pl.multiple_of(gap_ref[g] + done, SORT_ALIGN)
                    return pltpu.make_async_copy(h_loc.at[pl.ds(0, size)], hs_o.at[pl.ds(go, size)], sem)

                _piece_copies(action, gap_ref[N_GROUPS + g], GAP_BITS, copy_of)

            def body(k, carry):
                go = pl.multiple_of(gap_ref[2 * N_GROUPS] + k * EXPERT_ROWS_SHORT, SORT_ALIGN)
                cp = pltpu.make_async_copy(h_loc.at[pl.ds(0, EXPERT_ROWS_SHORT)], hs_o.at[pl.ds(go, EXPERT_ROWS_SHORT)], sem)
                cp.start() if action == "start" else cp.wait()
                return carry

            lax.fori_loop(0, gap_ref[2 * N_GROUPS + 1] // EXPERT_ROWS_SHORT, body, 0)


def _experts_kernel(tile_group_ref, n_valid_ref, hs_ref, w1_ref, w3_ref, w2_ref, ys_o, w1_b, w3_b, w2_b):
    j = pl.program_id(0)
    last = n_valid_ref[0] - 1
    g = tile_group_ref[jnp.minimum(j, last)]
    g_prev = tile_group_ref[jnp.minimum(jnp.maximum(j - 1, 0), last)]

    @pl.when((j == 0) | (g != g_prev))
    def _():
        w1_b[...] = w1_ref[...].astype(BF16)
        w3_b[...] = w3_ref[...].astype(BF16)
        w2_b[...] = w2_ref[...].astype(BF16)

    @pl.when(j <= last)
    def _():
        x = hs_ref[:, :D_MODEL]
        gs = sum(hs_ref[:, D_MODEL + t * ROUTER_LANES:D_MODEL + (t + 1) * ROUTER_LANES].astype(F32) for t in range(3))
        lane = lax.broadcasted_iota(jnp.int32, gs.shape, 1)
        acc = jnp.zeros(ys_o.shape, F32)
        for e in range(EXPERTS_PER_GROUP):
            ge = jnp.sum(jnp.where(lane == FINE_LANE0 + g * EXPERTS_PER_GROUP + e, gs, 0.0), axis=-1, keepdims=True)
            a = jnp.dot(x, w1_b[e], preferred_element_type=F32)
            b = jnp.dot(x, w3_b[e], preferred_element_type=F32)
            hid = (a * _sigmoid(a)) * b * ge
            acc = acc + jnp.dot(hid.astype(BF16), w2_b[e], preferred_element_type=F32)
        ys_o[...] = acc.astype(ys_o.dtype)

    @pl.when(j > last)
    def _():
        ys_o[...] = jnp.zeros_like(ys_o)


def _combine_kernel(group_tiles, plan_ref, h_ref, gate_ref, *refs):
    ng = len(group_tiles)
    p_refs, (ys_ref, ln2g_ref, ln2b_ref, wple_ref, wpg_ref, pleg_ref) = refs[:ng], refs[ng:ng + 6]
    o_refs, (y_loc, sem) = refs[ng + 6:2 * ng + 6], refs[2 * ng + 6:]
    i = pl.program_id(0)
    starts = [sum(group_tiles[:k]) for k in range(ng)]

    def in_group(k):
        return (i >= starts[k]) & (i < starts[k] + group_tiles[k])

    tm = h_ref.shape[0]
    lofs = [plan_ref[i, 2 * N_GROUPS + g] for g in range(N_GROUPS)]

    def fetch(tile):
        slot = tile % 2
        y_loc[slot] = jnp.zeros(y_loc.shape[1:], y_loc.dtype)
        _run_copies("start", plan_ref, tile, lambda lo, go, sz: ys_ref.at[pl.ds(go, sz)],
                    lambda lo, go, sz: y_loc.at[slot, pl.ds(lo, sz)], sem.at[slot])

    @pl.when(i == 0)
    def _():
        fetch(i)

    @pl.when(i + 1 < pl.num_programs(0))
    def _():
        fetch(i + 1)

    pos = _sort_positions(gate_ref[...], lofs)
    onehot_t = jnp.where(lax.broadcasted_iota(jnp.int32, (tm, SORT_LOCAL), 1) == pos, 1.0, 0.0).astype(BF16)
    p_tile = p_refs[0][...]
    for k in range(1, ng):
        p_tile = jnp.where(in_group(k), p_refs[k][...], p_tile)
    ple_in = _dot(p_tile, wple_ref[...])
    slot = i % 2
    _run_copies("wait", plan_ref, i, lambda lo, go, sz: ys_ref.at[pl.ds(go, sz)],
                lambda lo, go, sz: y_loc.at[slot, pl.ds(lo, sz)], sem.at[slot])
    ffn = jnp.dot(onehot_t, y_loc[slot], preferred_element_type=F32)
    h2 = _layer_norm(DEEPNORM_ALPHA * h_ref[...] + ffn, ln2g_ref[...], ln2b_ref[...])
    ple = ple_in * _sigmoid(_dot(h2, wpg_ref[...]))
    ms = jnp.mean(ple * ple, axis=-1, keepdims=True)
    out = h2 + ple * lax.rsqrt(ms + LN_EPS) * pleg_ref[...]
    for k in range(ng):
        @pl.when(in_group(k))
        def _(k=k):
            o_refs[k][...] = out


def _ffn(h, gate, counts, p_groups, W):
    n = h.shape[0]
    tm = SORT_ROWS
    assert n % tm == 0 and tm == POST_ROWS and SORT_LOCAL >= tm + N_GROUPS * SORT_ALIGN
    ntiles = n // tm
    er = EXPERT_ROWS if n >= 2 * N_GROUPS * EXPERT_ROWS else EXPERT_ROWS_SHORT
    assert er <= SORT_ALIGN << GAP_BITS and SORT_ALIGN << (GAP_BITS - 1) <= SORT_LOCAL
    cnt = counts[:, 0, :N_GROUPS]
    run = (cnt + SORT_ALIGN - 1) // SORT_ALIGN * SORT_ALIGN
    lofs = jnp.cumsum(run, axis=1) - run
    seg = (jnp.sum(run, axis=0) + er - 1) // er * er
    gbase = jnp.cumsum(seg) - seg
    goff = gbase[None, :] + jnp.cumsum(run, axis=0) - run
    plan = jnp.concatenate([goff, run, lofs], axis=1).astype(jnp.int32)
    max_tiles = (n + ntiles * N_GROUPS * (SORT_ALIGN - 1)) // er + N_GROUPS
    cap = max_tiles * er
    n_valid = (jnp.sum(seg) // er).astype(jnp.int32).reshape(1)
    tile_start = jnp.arange(max_tiles, dtype=jnp.int32) * er
    tile_group = jnp.clip(jnp.sum(tile_start[:, None] >= (gbase + seg)[None, :], axis=1), 0, N_GROUPS - 1).astype(jnp.int32)

    cparams = dict(vmem_limit_bytes=V7X_VMEM_LIMIT_BYTES)
    any_spec = pl.BlockSpec(memory_space=pl.ANY)
    total = jnp.sum(run, axis=0)
    assert er % EXPERT_ROWS_SHORT == 0 and EXPERT_ROWS_SHORT <= SORT_LOCAL
    used = jnp.sum(seg)
    gaps = jnp.concatenate([gbase + total, seg - total, jnp.stack([used, cap - used])]).astype(jnp.int32)
    hs = pl.pallas_call(
        _dispatch_kernel,
        grid_spec=pltpu.PrefetchScalarGridSpec(
            num_scalar_prefetch=2, grid=(ntiles,),
            in_specs=[pl.BlockSpec((tm, D_MODEL), lambda i, plan, gaps: (i, 0)),
                      pl.BlockSpec((tm, ROUTER_LANES), lambda i, plan, gaps: (i, 0))],
            out_specs=any_spec,
            scratch_shapes=[pltpu.VMEM((SORT_LOCAL, SORTED_WIDTH), BF16), pltpu.SemaphoreType.DMA(())]),
        out_shape=jax.ShapeDtypeStruct((cap, SORTED_WIDTH), BF16),
        compiler_params=pltpu.CompilerParams(dimension_semantics=("arbitrary",), **cparams),
        name="dispatch",
    )(plan, gaps, h, gate)

    def tile_rows(width):
        return pl.BlockSpec((er, width), lambda j, tg, nv: (jnp.minimum(j, nv[0] - 1), 0))

    def group_w(shape):
        return pl.BlockSpec((EXPERTS_PER_GROUP,) + shape, lambda j, tg, nv: (tg[jnp.minimum(j, nv[0] - 1)], 0, 0))

    ys = pl.pallas_call(
        _experts_kernel,
        grid_spec=pltpu.PrefetchScalarGridSpec(
            num_scalar_prefetch=2, grid=(max_tiles,),
            in_specs=[tile_rows(SORTED_WIDTH), group_w((D_MODEL, D_EXPERT)),
                      group_w((D_MODEL, D_EXPERT)), group_w((D_EXPERT, D_MODEL))],
            out_specs=pl.BlockSpec((er, D_MODEL), lambda j, tg, nv: (j, 0)),
            scratch_shapes=[pltpu.VMEM((EXPERTS_PER_GROUP, D_MODEL, D_EXPERT), BF16),
                            pltpu.VMEM((EXPERTS_PER_GROUP, D_MODEL, D_EXPERT), BF16),
                            pltpu.VMEM((EXPERTS_PER_GROUP, D_EXPERT, D_MODEL), BF16)]),
        out_shape=jax.ShapeDtypeStruct((cap, D_MODEL), BF16),
        compiler_params=pltpu.CompilerParams(dimension_semantics=("arbitrary",), **cparams),
        name="experts",
    )(tile_group, n_valid, hs, W["expert_w1"], W["expert_w3"], W["expert_w2"])

    def full(a):
        return pl.BlockSpec(a.shape, lambda i, plan: (0,) * a.ndim)

    params = [W["ln2_g"], W["ln2_b"], W["w_ple"], W["w_ple_gate"], W["ple_norm_g"]]
    group_tiles = tuple(pg.shape[0] // tm for pg in p_groups)
    assert sum(group_tiles) == ntiles
    starts = [sum(group_tiles[:k]) for k in range(len(group_tiles))]
    return pl.pallas_call(
        functools.partial(_combine_kernel, group_tiles),
        grid_spec=pltpu.PrefetchScalarGridSpec(
            num_scalar_prefetch=1, grid=(ntiles,),
            in_specs=[pl.BlockSpec((tm, D_MODEL), lambda i, plan: (i, 0)),
                      pl.BlockSpec((tm, ROUTER_LANES), lambda i, plan: (i, 0))]
            + [_group_rows(tm, D_PLE, st, t) for st, t in zip(starts, group_tiles)]
            + [any_spec] + [full(a) for a in params],
            out_specs=[_group_rows(tm, D_MODEL, st, t) for st, t in zip(starts, group_tiles)],
            scratch_shapes=[pltpu.VMEM((2, SORT_LOCAL, D_MODEL), BF16), pltpu.SemaphoreType.DMA((2,))]),
        out_shape=[jax.ShapeDtypeStruct((t * tm, D_MODEL), F32) for t in group_tiles],
        compiler_params=pltpu.CompilerParams(dimension_semantics=("arbitrary",), **cparams),
        name="combine",
    )(plan, h, gate, *p_groups, ys, *params)


def _mix(x, x_prev, wkv0, ret0, pos0, W):
    n_seq, seq_len, _ = x.shape
    x2 = x.reshape(n_seq * seq_len, D_MODEL)
    r, lw, k, v, al, be, g, bonus, qb, kb, vb, gb = _proj(x2, x_prev, seq_len, pos0, W)
    ya, ob, wkv1, ret1 = _mixer((r, lw, k, v, al, be, qb, kb, vb), wkv0, ret0, n_seq, seq_len)
    return (ya, ob, bonus, g, gb, x2), wkv1, ret1


def _prep_weights(i, w_in, mu_shift, w_decay_up, decay_base, w_aaa_up, aaa_base, w_gate_up, k_k, k_a, r_k,
                  lnx_g, lnx_b, ret_gn_g, ret_gn_b, w_out, ln1_g, ln1_b,
                  router_coarse_w, router_coarse_b, router_fine_w, router_fine_b,
                  expert_w1, expert_w3, expert_w2, ln2_g, ln2_b, w_ple, w_ple_gate, ple_norm_g):
    def row(a):
        return a[i].reshape(1, -1).astype(F32)

    pad = ROUTER_LANES - N_GROUPS - N_EXPERTS
    w_router = jnp.concatenate([router_coarse_w[i], router_fine_w[i], jnp.zeros((D_MODEL, pad), F32)], axis=1)
    b_router = jnp.concatenate([router_coarse_b[i], router_fine_b[i], jnp.zeros((pad,), F32)]).reshape(1, -1)
    return {
        "w_in": w_in[i].astype(BF16), "mu_shift": row(mu_shift), "w_decay_up": w_decay_up[i].astype(BF16),
        "decay_base": row(decay_base), "w_aaa_up": w_aaa_up[i].astype(BF16), "aaa_base": row(aaa_base),
        "w_gate_up": w_gate_up[i].astype(BF16), "k_k": row(k_k), "k_a": row(k_a), "r_k": row(r_k),
        "lnx_g": row(lnx_g), "lnx_b": row(lnx_b), "ret_gn_g": row(ret_gn_g), "ret_gn_b": row(ret_gn_b),
        "w_out": w_out[i].astype(BF16), "ln1_g": row(ln1_g), "ln1_b": row(ln1_b),
        "w_router": w_router, "b_router": b_router,
        "expert_w1": expert_w1[i], "expert_w3": expert_w3[i], "expert_w2": expert_w2[i], "ln2_g": row(ln2_g), "ln2_b": row(ln2_b),
        "w_ple": w_ple[i].astype(BF16), "w_ple_gate": w_ple_gate[i].astype(BF16), "ple_norm_g": row(ple_norm_g),
    }


def kernel(x_prompt, x_sample, p_prompt, p_sample, state_wkv, state_shift, state_ret, w_in, mu_shift, w_decay_up, decay_base, w_aaa_up, aaa_base, w_gate_up, k_k, k_a, r_k, lnx_g, lnx_b, ret_gn_g, ret_gn_b, w_out, ln1_g, ln1_b, router_coarse_w, router_coarse_b, router_fine_w, router_fine_b, expert_w1, expert_w3, expert_w2, ln2_g, ln2_b, w_ple, w_ple_gate, ple_norm_g):
    yp, ys = x_prompt, x_sample
    nb = x_prompt.shape[0]
    depth = w_in.shape[0]
    wkv_p, shift_p, ret_p, wkv_s, shift_s, ret_s = [], [], [], [], [], []
    for i in range(depth):
        W = _prep_weights(i, w_in, mu_shift, w_decay_up, decay_base, w_aaa_up, aaa_base, w_gate_up, k_k, k_a, r_k,
                          lnx_g, lnx_b, ret_gn_g, ret_gn_b, w_out, ln1_g, ln1_b,
                          router_coarse_w, router_coarse_b, router_fine_w, router_fine_b,
                          expert_w1, expert_w3, expert_w2, ln2_g, ln2_b, w_ple, w_ple_gate, ple_norm_g)
        ops_p, wp, rp = _mix(yp, jnp.zeros((nb, D_MODEL), F32), jnp.zeros((nb, A_HEADS, A_HEAD_DIM, A_HEAD_DIM), F32),
                             jnp.zeros((nb, B_HEADS, B_QK_DIM, B_V_DIM), F32), 0, W)
        ops_s, wsm, rsm = _mix(ys, state_shift[i], state_wkv[i], state_ret[i], PAST_LEN, W)
        sp, ss = yp[:, -1], ys[:, -1]
        h, gate, counts = _post([ops_p, ops_s], W)
        out_p, out_s = _ffn(h, gate, counts, [p_prompt[i].reshape(-1, D_PLE), p_sample[i].reshape(-1, D_PLE)], W)
        yp, ys = out_p.reshape(yp.shape), out_s.reshape(ys.shape)
        wkv_p.append(wp); shift_p.append(sp); ret_p.append(rp)
        wkv_s.append(wsm); shift_s.append(ss); ret_s.append(rsm)
    return (yp, ys, jnp.stack(wkv_p, 0), jnp.stack(shift_p, 0), jnp.stack(ret_p, 0),
            jnp.stack(wkv_s, 0), jnp.stack(shift_s, 0), jnp.stack(ret_s, 0))
```

```python
import functools
import math

import numpy as np
import jax
import jax.numpy as jnp
from jax import lax
from jax.experimental import pallas as pl
from jax.experimental.pallas import tpu as pltpu

F32 = jnp.float32
BF16 = jnp.bfloat16

D_MODEL = 1024
D_PLE = 256
A_HEADS = 8
A_HEAD_DIM = 64
A_WIDTH = A_HEADS * A_HEAD_DIM
DECAY_LORA = 64
AAA_LORA = 64
GATE_LORA = 128
GN_EPS_RWKV = 64e-5
B_HEADS = 4
B_QK_DIM = 64
B_V_DIM = 128
B_QK_WIDTH = B_HEADS * B_QK_DIM
B_WIDTH = B_HEADS * B_V_DIM
ROPE_BASE = 10000.0
GN_EPS = 1e-5
SHIFT_WIDTH = 3 * A_WIDTH + DECAY_LORA + AAA_LORA + GATE_LORA
IN_WIDTH = SHIFT_WIDTH + 2 * B_QK_WIDTH + 2 * B_WIDTH
N_GROUPS = 4
EXPERTS_PER_GROUP = 4
N_EXPERTS = N_GROUPS * EXPERTS_PER_GROUP
D_EXPERT = 256
DEPTH = 1
PAST_LEN = 16384
DEEPNORM_ALPHA = (2 * DEPTH) ** 0.25
LN_EPS = 1e-5

V7X_LANES = 128
V7X_SUBLANES = 8
V7X_VMEM_LIMIT_BYTES = 56 * 1024 * 1024

PROJ_ROWS = 1024
LOG_DECAY_OUT = 1
CHUNK_ROWS = 64
MIXER_TILES = 8
MIXER_TILES_PACKED = 2
POST_ROWS = 512
SORT_ROWS = 512
SORT_ALIGN = 16
SORT_LOCAL = 640
EXPERT_ROWS = 1024
EXPERT_ROWS_SHORT = 256
GAP_BITS = 6
RUN_BITS = 6
GROUP_LANE = 0
ROUTER_LANES = V7X_LANES
FINE_LANE0 = N_GROUPS
SORTED_WIDTH = D_MODEL + 3 * ROUTER_LANES


def _dot(a, b):
    return jnp.dot(a.astype(BF16), b.astype(BF16), preferred_element_type=F32)


def _dot_nt(a, b):
    return lax.dot_general(a.astype(BF16), b.astype(BF16), (((1,), (1,)), ((), ())), preferred_element_type=F32)


def _dot_tn(a, b):
    return lax.dot_general(a.astype(BF16), b.astype(BF16), (((0,), (0,)), ((), ())), preferred_element_type=F32)


def _split2(x):
    hi = x.astype(BF16)
    lo = (x - hi.astype(F32)).astype(BF16)
    return hi, lo


def _split3(x):
    hi = x.astype(BF16)
    r1 = x - hi.astype(F32)
    mid = r1.astype(BF16)
    lo = (r1 - mid.astype(F32)).astype(BF16)
    return hi, mid, lo


def _sigmoid(x):
    return 1.0 / (1.0 + jnp.exp(-x))


def _group_ones(width, group):
    r = lax.broadcasted_iota(jnp.int32, (width, width), 0) // group
    c = lax.broadcasted_iota(jnp.int32, (width, width), 1) // group
    return jnp.where(r == c, 1.0, 0.0).astype(BF16)


def _group_sum(x, ones):
    return jnp.dot(x.astype(BF16), ones, preferred_element_type=F32)


def _proj_kernel(carry_mode, seq_len, tiles_per_seq,
                 x_ref, xp_ref, w_ref, mu_ref, wdec_ref, dbase_ref, waaa_ref, abase_ref, wgate_ref,
                 kk_ref, ka_ref, rk_ref, cos_ref, sin_ref,
                 r_o, lw_o, k_o, v_o, al_o, be_o, g_o, bonus_o, qb_o, kb_o, vb_o, gb_o,
                 carry_scr):
    tm = x_ref.shape[0]
    if carry_mode:
        xp = jnp.broadcast_to(xp_ref[0], (V7X_SUBLANES, D_MODEL))
        xb = jnp.concatenate([x_ref[...], xp], axis=0).astype(BF16)
        j = pl.program_id(0) % tiles_per_seq

        @pl.when(pl.program_id(0) == 0)
        def _():
            carry_scr[...] = jnp.zeros_like(carry_scr)
    else:
        xb = x_ref[...].astype(BF16)
        xpb = xp_ref[...].astype(BF16)

    def project(lo, hi):
        return jnp.dot(xb, w_ref[:, lo:hi], preferred_element_type=F32)

    def shifted(p, lo, hi):
        cur = p[:tm]
        row = lax.broadcasted_iota(jnp.int32, cur.shape, 0)
        rolled = pltpu.roll(cur, 1, 0)
        if carry_mode:
            first = jnp.where(j == 0, p[tm + V7X_SUBLANES - 1:], carry_scr[V7X_SUBLANES - 1:V7X_SUBLANES, lo:hi])
            prev = jnp.where(row == 0, first, rolled)
            carry_scr[:, lo:hi] = cur[tm - V7X_SUBLANES:]
        else:
            first = jnp.dot(xpb, w_ref[:, lo:hi], preferred_element_type=F32)
            prev = jnp.where((row & (seq_len - 1)) == 0, first, rolled)
        return cur + (prev - cur) * mu_ref[:, lo:hi]

    c_r, c_k, c_v, c_l = 0, A_WIDTH, 2 * A_WIDTH, 3 * A_WIDTH
    p_lora = project(c_l, SHIFT_WIDTH)
    p_k = project(c_k, c_v)
    p_r = project(c_r, c_k)

    lora = shifted(p_lora, c_l, SHIFT_WIDTH)
    w_lo = lora[:, :DECAY_LORA]
    a_lo = lora[:, DECAY_LORA:DECAY_LORA + AAA_LORA]
    g_lo = lora[:, DECAY_LORA + AAA_LORA:]
    z = -(dbase_ref[...] + _dot(jnp.tanh(w_lo), wdec_ref[...]))
    softplus = jnp.maximum(z, 0.0) + jnp.log(1.0 + jnp.exp(-jnp.abs(z)))
    log_w = -softplus - 0.5
    lw_o[...] = -jnp.exp(log_w)
    a = _sigmoid(abase_ref[...] + _dot(a_lo, waaa_ref[...]))
    g_o[...] = (_dot(_sigmoid(g_lo), wgate_ref[...])).astype(g_o.dtype)

    p_v = project(c_v, c_l)

    ones64 = _group_ones(A_WIDTH, A_HEAD_DIM)
    k0 = shifted(p_k, c_k, c_v)
    kk0 = k0 * kk_ref[...]
    ssq = _group_sum(kk0 * kk0, ones64)
    kk = kk0 * jnp.minimum(lax.rsqrt(ssq), 1e12)
    k = k0 * (1.0 + (a - 1.0) * ka_ref[...])
    k_o[...] = (k).astype(k_o.dtype)
    al_o[...] = (-kk).astype(al_o.dtype)
    be_o[...] = (kk * a).astype(be_o.dtype)

    o = SHIFT_WIDTH
    p_qk = project(o, o + 2 * B_QK_WIDTH)[:tm]

    r = shifted(p_r, c_r, c_k)
    r_o[...] = (r).astype(r_o.dtype)
    rk_sum = _group_sum(r * k * rk_ref[...], ones64)

    p_vb = project(o + 2 * B_QK_WIDTH, o + 2 * B_QK_WIDTH + B_WIDTH)[:tm]

    v = shifted(p_v, c_v, c_l)
    v_o[...] = (v).astype(v_o.dtype)
    bonus_o[...] = (rk_sum * v).astype(bonus_o.dtype)

    p_gb = project(o + 2 * B_QK_WIDTH + B_WIDTH, IN_WIDTH)[:tm]

    q_b = p_qk[:, :B_QK_WIDTH]
    k_b = p_qk[:, B_QK_WIDTH:]
    lane = lax.broadcasted_iota(jnp.int32, (tm, B_QK_WIDTH), 1)
    first_half = (lane & (B_QK_DIM - 1)) < (B_QK_DIM // 2)
    cos = cos_ref[...]
    sin = sin_ref[...]

    def rot(t):
        swapped = jnp.where(first_half, pltpu.roll(t, B_QK_WIDTH - B_QK_DIM // 2, 1), pltpu.roll(t, B_QK_DIM // 2, 1))
        return t * cos + swapped * sin

    qb_o[...] = (rot(q_b)).astype(qb_o.dtype)
    kb_o[...] = (rot(k_b) * (B_QK_DIM ** -0.5)).astype(kb_o.dtype)

    vb_o[...] = (p_vb).astype(vb_o.dtype)
    gb_o[...] = (p_gb * _sigmoid(p_gb)).astype(gb_o.dtype)


def _proj(x2, x_prev, seq_len, pos0, W):
    n = x2.shape[0]
    tm = PROJ_ROWS
    assert n % tm == 0
    carry_mode = seq_len % tm == 0
    if carry_mode:
        tiles_per_seq = seq_len // tm
        xp = x_prev.reshape(-1, 1, D_MODEL)
        xp_spec = pl.BlockSpec((1, 1, D_MODEL), lambda i: (i // tiles_per_seq, 0, 0))
        tab_rows = seq_len
    else:
        assert tm % seq_len == 0 and seq_len & (seq_len - 1) == 0
        tiles_per_seq = 1
        xp = jnp.repeat(x_prev, seq_len, axis=0)
        xp_spec = pl.BlockSpec((tm, D_MODEL), lambda i: (i, 0))
        tab_rows = tm
    half = B_QK_DIM // 2
    inv = ROPE_BASE ** (-jnp.arange(half, dtype=F32) / half)
    pos = (pos0 + jnp.arange(seq_len, dtype=jnp.int32)).astype(F32)
    ang = pos[:, None] * inv[None, :]
    cos = jnp.tile(jnp.concatenate([jnp.cos(ang), jnp.cos(ang)], -1), (tab_rows // seq_len, B_HEADS))
    sin = jnp.tile(jnp.concatenate([-jnp.sin(ang), jnp.sin(ang)], -1), (tab_rows // seq_len, B_HEADS))
    tab_tiles = tab_rows // tm
    tab_spec = pl.BlockSpec((tm, B_QK_WIDTH), lambda i: (i % tab_tiles, 0))

    def full(a):
        return pl.BlockSpec(a.shape, lambda i: (0,) * a.ndim)

    def rows(width):
        return pl.BlockSpec((tm, width), lambda i: (i, 0))

    params = [W["w_in"], W["mu_shift"], W["w_decay_up"], W["decay_base"], W["w_aaa_up"], W["aaa_base"],
              W["w_gate_up"], W["k_k"], W["k_a"], W["r_k"]]
    widths = [A_WIDTH] * 8 + [B_QK_WIDTH, B_QK_WIDTH, B_WIDTH, B_WIDTH]
    outs = pl.pallas_call(
        functools.partial(_proj_kernel, carry_mode, seq_len, tiles_per_seq),
        grid=(n // tm,),
        in_specs=[rows(D_MODEL), xp_spec, pl.BlockSpec(params[0].shape, lambda i: (0, 0), pipeline_mode=pl.Buffered(1))]
        + [full(a) for a in params[1:]] + [tab_spec, tab_spec],
        out_specs=[rows(w) for w in widths],
        out_shape=[jax.ShapeDtypeStruct((n, w), F32 if i == LOG_DECAY_OUT else BF16) for i, w in enumerate(widths)],
        scratch_shapes=[pltpu.VMEM((V7X_SUBLANES, SHIFT_WIDTH), F32)],
        compiler_params=pltpu.CompilerParams(dimension_semantics=("arbitrary",),
                                             vmem_limit_bytes=V7X_VMEM_LIMIT_BYTES),
        name="proj",
    )(x2, xp, *params, cos, sin)
    return outs


def _mixer_kernel(nb, seqs, clen,
                  r_ref, lw_ref, k_ref, v_ref, al_ref, be_ref, qb_ref, kb_ref, vb_ref, wkv0_ref, ret0_ref,
                  ya_o, ob_o, wkv_o, ret_o, s_scr, r_scr):
    R = seqs * clen
    log2c = int(math.log2(clen))
    c_idx = pl.program_id(1)
    hd = A_HEAD_DIM
    pw2 = 2 * hd
    TP = [(t, j) for t in range(nb) for j in range(A_HEADS // 2)]
    TG = [(t, h) for t in range(nb) for h in range(B_HEADS)]

    @pl.when(c_idx == 0)
    def _():
        zero = jnp.zeros((hd, hd), F32)
        for ci, (t, j) in enumerate(TP):
            blocks = [jnp.concatenate([jnp.concatenate([wkv0_ref[t * seqs + i, 2 * j], zero], axis=1),
                                       jnp.concatenate([zero, wkv0_ref[t * seqs + i, 2 * j + 1]], axis=1)], axis=0)
                      for i in range(seqs)]
            s_scr[ci] = jnp.concatenate(blocks, axis=1) if seqs > 1 else blocks[0]
        for t, h in TG:
            r_scr[t * B_HEADS + h] = ret0_ref[t * seqs:(t + 1) * seqs, h].reshape(seqs * B_QK_DIM, B_V_DIM)

    row = lax.broadcasted_iota(jnp.int32, (R, R), 0)
    col = lax.broadcasted_iota(jnp.int32, (R, R), 1)
    same = (row >> log2c) == (col >> log2c)
    incl = same & (col <= row)
    m_incl = jnp.where(incl, 1.0, 0.0).astype(BF16)
    m_same = jnp.where(same, 1.0, 0.0).astype(BF16)

    def expand(t):
        if seqs == 1:
            return t
        w = t.shape[1]
        wide = jnp.concatenate([t] * seqs, axis=1)
        rr = lax.broadcasted_iota(jnp.int32, wide.shape, 0) >> log2c
        cc = lax.broadcasted_iota(jnp.int32, wide.shape, 1) // w
        return jnp.where(rr == cc, wide, 0.0)

    a_bar, r_bar, b_til, k_til, b_dec, k_dec, d_end, vv = [], [], [], [], [], [], [], []
    for t in range(nb):
        lw = lw_ref[t]
        parts = _split3(lw)
        c = sum(jnp.dot(m_incl, p, preferred_element_type=F32) for p in parts)
        cend = sum(jnp.dot(m_same, p, preferred_element_type=F32) for p in parts)
        einv = jnp.exp(-c)
        edec = jnp.exp(cend - c)
        a_bar.append(al_ref[t] * jnp.exp(c - lw))
        r_bar.append(r_ref[t] * jnp.exp(c))
        b_til.append(be_ref[t] * einv)
        k_til.append(k_ref[t] * einv)
        b_dec.append(be_ref[t] * edec)
        k_dec.append(k_ref[t] * edec)
        d_end.append(jnp.exp(cend))
        vv.append(v_ref[t].astype(F32))
    last_row = (lax.broadcasted_iota(jnp.int32, (R, seqs * pw2), 0) & (clen - 1)) == clen - 1

    lane2 = lax.broadcasted_iota(jnp.int32, (R, pw2), 1)
    first = lane2 < hd
    row2 = lax.broadcasted_iota(jnp.int32, (R, pw2), 0)
    col2 = lane2 & (hd - 1)
    same2 = (row2 >> log2c) == (col2 >> log2c)
    incl2 = same2 & (col2 <= row2)
    strict2 = same2 & (col2 < row2)
    eye2 = jnp.where(row2 == col2, 1.0, 0.0).astype(F32)

    def pair(x, j):
        return x[:, j * pw2:(j + 1) * pw2]

    def keep(x, second):
        m = first if x.shape[1] == pw2 else jnp.concatenate([first] * (x.shape[1] // pw2), axis=1)
        return jnp.where(m != second, x, 0.0)

    def blockdiag(y, swapped=False):
        return jnp.concatenate([keep(y, swapped), keep(y, not swapped)], axis=0)

    ams = []
    for t, j in TP:
        lhs = jnp.concatenate([pair(a_bar[t], j), pair(r_bar[t], j)], axis=0)
        bt, kt = pair(b_til[t], j), pair(k_til[t], j)
        lhs0 = jnp.where(jnp.concatenate([first, first], axis=0), lhs, 0.0)
        lhs1 = jnp.where(jnp.concatenate([first, first], axis=0), 0.0, lhs)
        ams.append((_dot_nt(lhs0, jnp.concatenate([bt, kt], axis=0)), _dot_nt(lhs1, jnp.concatenate([kt, bt], axis=0))))

    lgs = [float(np.log1p(-np.exp2(-5.0 - h))) for h in range(B_HEADS)]
    qb = [qb_ref[t].astype(F32) for t in range(nb)]
    kb = [kb_ref[t].astype(F32) for t in range(nb)]
    qs = [qb[t][:, h * B_QK_DIM:(h + 1) * B_QK_DIM] for t, h in TG]
    khs = [kb[t][:, h * B_QK_DIM:(h + 1) * B_QK_DIM] for t, h in TG]
    vbs = [vb_ref[t][:, h * B_V_DIM:(h + 1) * B_V_DIM] for t, h in TG]
    diff = (row - col).astype(F32)
    pos_v = (lax.broadcasted_iota(jnp.int32, (R, B_V_DIM), 0) & (clen - 1)).astype(F32)
    pos_k = (lax.broadcasted_iota(jnp.int32, (R, B_QK_DIM), 0) & (clen - 1)).astype(F32)
    intra = [jnp.where(incl, jnp.exp(lg * diff), 0.0) for lg in lgs]
    cross = [jnp.exp(lg * (pos_v + 1.0)) for lg in lgs]
    kdec = [jnp.exp(lg * (clen - 1.0 - pos_k)) for lg in lgs]
    rstates = [r_scr[t * B_HEADS + h] for t, h in TG]
    scs = [_dot_nt(qs[i], khs[i]) * intra[h] for i, (t, h) in enumerate(TG)]
    qst = [_dot(expand(qs[i]), rstates[i]) * cross[h] for i, (t, h) in enumerate(TG)]

    n = len(TP)
    a_ab = [jnp.where(strict2, jnp.where(first, m0[:R], m1[:R]), 0.0) for m0, m1 in ams]
    a_ak = [jnp.where(strict2, jnp.where(first, m1[:R], m0[:R]), 0.0) for m0, m1 in ams]
    a_rb = [jnp.where(incl2, jnp.where(first, m0[R:], m1[R:]), 0.0) for m0, m1 in ams]
    a_rk = [jnp.where(incl2, jnp.where(first, m1[R:], m0[R:]), 0.0) for m0, m1 in ams]
    vps = [pair(vv[t], j) for t, j in TP]
    av = [_dot(a_ak[i], blockdiag(vps[i], swapped=True)) for i in range(n)]
    tinv = [eye2 + a for a in a_ab]
    if log2c > 1:
        pw = [_dot(a, blockdiag(a)) for a in a_ab]
    for it in range(log2c - 1):
        if it < log2c - 2:
            tp = [_dot(pw[i], blockdiag(jnp.concatenate([tinv[i], pw[i]], axis=1))) for i in range(n)]
            tinv = [tinv[i] + tp[i][:, :pw2] for i in range(n)]
            pw = [tp[i][:, pw2:] for i in range(n)]
        else:
            tinv = [tinv[i] + _dot(pw[i], blockdiag(tinv[i])) for i in range(n)]

    os_ = [_dot(scs[i], vbs[i]) + qst[i] for i in range(len(TG))]
    for i, (t, h) in enumerate(TG):
        r_scr[t * B_HEADS + h] = (rstates[i] * float(np.exp(lgs[h] * clen))
                                  + _dot_tn(expand(khs[i] * kdec[h]), vbs[i]))
    for t in range(nb):
        ob_o[t] = jnp.concatenate(os_[t * B_HEADS:(t + 1) * B_HEADS], axis=1)

    wu = [_dot(tinv[i], blockdiag(jnp.concatenate([pair(a_bar[t], j), av[i]], axis=1)))
          for i, (t, j) in enumerate(TP)]
    states = [s_scr[i] for i in range(n)]
    ws = [_dot_nt(jnp.concatenate([expand(wu[i][:, :pw2]), expand(pair(r_bar[t], j))], axis=0), states[i])
          for i, (t, j) in enumerate(TP)]
    us = [ws[i][:R] + wu[i][:, pw2:] for i in range(n)]
    ys = [ws[i][R:] + _dot(jnp.concatenate([a_rb[i], a_rk[i]], axis=1),
                           jnp.concatenate([blockdiag(us[i]), blockdiag(vps[i], swapped=True)], axis=0))
          for i in range(n)]
    npair = A_HEADS // 2
    for t in range(nb):
        ya_o[t] = jnp.concatenate(ys[t * npair:(t + 1) * npair], axis=1)
    rowp = lax.broadcasted_iota(jnp.int32, (pw2, seqs * pw2), 0) >= hd
    lanep = (lax.broadcasted_iota(jnp.int32, (pw2, seqs * pw2), 1) & (pw2 - 1)) >= hd
    for i, (t, j) in enumerate(TP):
        d_row = jnp.sum(jnp.where(last_row, expand(pair(d_end[t], j)), 0.0), axis=0, keepdims=True)
        bk = jnp.concatenate([expand(pair(b_dec[t], j)), expand(pair(k_dec[t], j))], axis=0)
        upd = _dot_tn(jnp.concatenate([us[i], vps[i]], axis=0), bk)
        s_scr[i] = states[i] * d_row + jnp.where(rowp == lanep, upd, 0.0)

    @pl.when(c_idx == pl.num_programs(1) - 1)
    def _():
        for ci, (t, j) in enumerate(TP):
            st = s_scr[ci]
            for i in range(seqs):
                wkv_o[t * seqs + i, 2 * j] = st[:hd, i * pw2:i * pw2 + hd]
                wkv_o[t * seqs + i, 2 * j + 1] = st[hd:, i * pw2 + hd:(i + 1) * pw2]
        for t, h in TG:
            ret_o[t * seqs:(t + 1) * seqs, h] = r_scr[t * B_HEADS + h].reshape(seqs, B_QK_DIM, B_V_DIM)


def _mixer(ops, wkv0, ret0, n_seq, seq_len):
    n = ops[0].shape[0]
    R = CHUNK_ROWS
    if seq_len >= R:
        assert seq_len % R == 0
        seqs, clen, nchunks = 1, R, seq_len // R
    else:
        assert R % seq_len == 0 and seq_len & (seq_len - 1) == 0 and n_seq % (R // seq_len) == 0
        seqs, clen, nchunks = R // seq_len, seq_len, 1
    ntiles = n_seq // seqs
    nb = MIXER_TILES if seqs == 1 else MIXER_TILES_PACKED
    assert ntiles % nb == 0
    ops3 = [a.reshape(ntiles, nchunks * R, a.shape[1]) for a in ops]

    def rows(width):
        return pl.BlockSpec((nb, R, width), lambda i, c: (i, c, 0))

    wkv_spec = pl.BlockSpec((nb * seqs, A_HEADS, A_HEAD_DIM, A_HEAD_DIM), lambda i, c: (i, 0, 0, 0))
    ret_spec = pl.BlockSpec((nb * seqs, B_HEADS, B_QK_DIM, B_V_DIM), lambda i, c: (i, 0, 0, 0))
    ya, ob, wkv1, ret1 = pl.pallas_call(
        functools.partial(_mixer_kernel, nb, seqs, clen),
        grid=(ntiles // nb, nchunks),
        in_specs=[rows(A_WIDTH)] * 6 + [rows(B_QK_WIDTH), rows(B_QK_WIDTH), rows(B_WIDTH), wkv_spec, ret_spec],
        out_specs=[rows(A_WIDTH), rows(B_WIDTH), wkv_spec, ret_spec],
        out_shape=[jax.ShapeDtypeStruct((ntiles, nchunks * R, A_WIDTH), F32),
                   jax.ShapeDtypeStruct((ntiles, nchunks * R, B_WIDTH), F32),
                   jax.ShapeDtypeStruct(wkv0.shape, F32), jax.ShapeDtypeStruct(ret0.shape, F32)],
        scratch_shapes=[pltpu.VMEM((nb * A_HEADS // 2, 2 * A_HEAD_DIM, seqs * 2 * A_HEAD_DIM), F32),
                        pltpu.VMEM((nb * B_HEADS, seqs * B_QK_DIM, B_V_DIM), F32)],
        compiler_params=pltpu.CompilerParams(dimension_semantics=("parallel", "arbitrary"),
                                             vmem_limit_bytes=V7X_VMEM_LIMIT_BYTES),
        name="mixer",
    )(*ops3, wkv0, ret0)
    return ya.reshape(n, A_WIDTH), ob.reshape(n, B_WIDTH), wkv1, ret1


def _layer_norm(z, g, b):
    mu = jnp.mean(z, axis=-1, keepdims=True)
    d = z - mu
    var = jnp.mean(d * d, axis=-1, keepdims=True)
    return d * lax.rsqrt(var + LN_EPS) * g + b


def _post_tile(ya_ref, ob_ref, bonus_ref, g_ref, gb_ref, x_ref, lnxg_ref, lnxb_ref, rgg_ref, rgb_ref,
               wout_ref, ln1g_ref, ln1b_ref, wr_ref, br_ref, h_o, gate_o, cnt_o):
    tm = x_ref.shape[0]

    def head_norm(t, group, eps, gg, bb):
        ones = _group_ones(t.shape[1], group)
        mu = _group_sum(t, ones) * (1.0 / group)
        d = t - mu
        var = _group_sum(d * d, ones) * (1.0 / group)
        return d * lax.rsqrt(var + eps) * gg + bb

    y_a = (head_norm(ya_ref[...], A_HEAD_DIM, GN_EPS_RWKV, lnxg_ref[...], lnxb_ref[...]) + bonus_ref[...]) * g_ref[...]
    y_b = head_norm(ob_ref[...], B_V_DIM, GN_EPS, rgg_ref[...], rgb_ref[...]) * gb_ref[...]
    y = jnp.concatenate([y_a, y_b], axis=1)
    mix = _dot(y, wout_ref[...])
    h = _layer_norm(DEEPNORM_ALPHA * x_ref[...] + mix, ln1g_ref[...], ln1b_ref[...])
    h_o[...] = h

    h_hi, h_lo = _split2(h)
    w_hi, w_lo = _split2(wr_ref[...])
    logits = (jnp.dot(h_hi, w_hi, preferred_element_type=F32) + jnp.dot(h_hi, w_lo, preferred_element_type=F32)
              + jnp.dot(h_lo, w_hi, preferred_element_type=F32)) + br_ref[...]
    lane = lax.broadcasted_iota(jnp.int32, (tm, ROUTER_LANES), 1).astype(F32)
    neg = -jnp.inf
    big = float(ROUTER_LANES)
    cl = jnp.where(lane < N_GROUPS, logits, neg)
    cmax = jnp.max(cl, axis=-1, keepdims=True)
    grp = jnp.min(jnp.where(cl == cmax, lane, big), axis=-1, keepdims=True)
    gprob = 1.0 / jnp.sum(jnp.exp(cl - cmax), axis=-1, keepdims=True)
    lo_lane = FINE_LANE0 + grp * EXPERTS_PER_GROUP
    fv = jnp.where((lane >= lo_lane) & (lane < lo_lane + EXPERTS_PER_GROUP), logits, neg)
    m1 = jnp.max(fv, axis=-1, keepdims=True)
    i1 = jnp.min(jnp.where(fv == m1, lane, big), axis=-1, keepdims=True)
    fv2 = jnp.where(lane == i1, neg, fv)
    m2 = jnp.max(fv2, axis=-1, keepdims=True)
    i2 = jnp.min(jnp.where(fv2 == m2, lane, big), axis=-1, keepdims=True)
    e2 = jnp.exp(m2 - m1)
    w1 = gprob / (1.0 + e2)
    w2 = gprob * e2 / (1.0 + e2)
    gate_o[...] = (jnp.where(lane == i1, w1, 0.0) + jnp.where(lane == i2, w2, 0.0)
                   + jnp.where(lane == GROUP_LANE, grp, 0.0))
    onehot = jnp.where((lane == grp) & (lane < N_GROUPS), 1.0, 0.0)
    cnt_o[0] = jnp.broadcast_to(jnp.sum(onehot, axis=0, keepdims=True), (V7X_SUBLANES, ROUTER_LANES)).astype(jnp.int32)


POST_OPERANDS = 6


def _post_kernel(group_tiles, *refs):
    per_group, shared = refs[:POST_OPERANDS * len(group_tiles)], refs[POST_OPERANDS * len(group_tiles):]
    i = pl.program_id(0)
    start = 0
    for k, tiles in enumerate(group_tiles):
        @pl.when((i >= start) & (i < start + tiles))
        def _(k=k):
            _post_tile(*per_group[POST_OPERANDS * k:POST_OPERANDS * (k + 1)], *shared)

        start += tiles


def _group_rows(tm, width, start, tiles):
    return pl.BlockSpec((tm, width), lambda i, *_: (jnp.clip(i - start, 0, tiles - 1), 0))


def _post(groups, W):
    tm = POST_ROWS
    group_tiles = tuple(g[-1].shape[0] // tm for g in groups)
    assert all(g[-1].shape[0] % tm == 0 for g in groups)
    n = tm * sum(group_tiles)

    def full(a):
        return pl.BlockSpec(a.shape, lambda i: (0,) * a.ndim)

    def rows(width):
        return pl.BlockSpec((tm, width), lambda i: (i, 0))

    widths = [A_WIDTH, B_WIDTH, A_WIDTH, A_WIDTH, B_WIDTH, D_MODEL]
    in_specs, start = [], 0
    for tiles in group_tiles:
        in_specs += [_group_rows(tm, w, start, tiles) for w in widths]
        start += tiles
    params = [W["lnx_g"], W["lnx_b"], W["ret_gn_g"], W["ret_gn_b"], W["w_out"], W["ln1_g"], W["ln1_b"],
              W["w_router"], W["b_router"]]
    return pl.pallas_call(
        functools.partial(_post_kernel, group_tiles),
        grid=(n // tm,),
        in_specs=in_specs + [full(a) for a in params],
        out_specs=[rows(D_MODEL), rows(ROUTER_LANES),
                   pl.BlockSpec((1, V7X_SUBLANES, ROUTER_LANES), lambda i: (i, 0, 0))],
        out_shape=[jax.ShapeDtypeStruct((n, D_MODEL), F32), jax.ShapeDtypeStruct((n, ROUTER_LANES), F32),
                   jax.ShapeDtypeStruct((n // tm, V7X_SUBLANES, ROUTER_LANES), jnp.int32)],
        compiler_params=pltpu.CompilerParams(dimension_semantics=("parallel",),
                                             vmem_limit_bytes=V7X_VMEM_LIMIT_BYTES),
        name="post",
    )(*[a for g in groups for a in g], *params)


def _sort_positions(gate, lofs):
    tm = gate.shape[0]
    lane = lax.broadcasted_iota(jnp.int32, (tm, ROUTER_LANES), 1)
    grp = gate[:, GROUP_LANE:GROUP_LANE + 1].astype(jnp.int32)
    onehot = jnp.where((lane == grp) & (lane < N_GROUPS), 1.0, 0.0)
    r = lax.broadcasted_iota(jnp.int32, (tm, tm), 0)
    c = lax.broadcasted_iota(jnp.int32, (tm, tm), 1)
    earlier = jnp.where(c < r, 1.0, 0.0).astype(BF16)
    prefix = jnp.dot(earlier, onehot.astype(BF16), preferred_element_type=F32)
    base = jnp.zeros((tm, ROUTER_LANES), F32)
    for g in range(N_GROUPS):
        base = jnp.where(lane == g, lofs[g].astype(F32), base)
    return jnp.sum(onehot * (base + prefix), axis=1, keepdims=True).astype(jnp.int32)


def _piece_copies(action, rows, bits, copy_of):
    k = rows // SORT_ALIGN
    for b in reversed(range(bits)):
        size = SORT_ALIGN << b

        @pl.when(((k >> b) & 1) == 1)
        def _():
            done = ((k >> (b + 1)) << (b + 1)) * SORT_ALIGN
            cp = copy_of(done, size)
            cp.start() if action == "start" else cp.wait()


def _run_copies(action, plan_ref, i, src_of, dst_of, sem):
    for g in range(N_GROUPS):
        goff = plan_ref[i, g]
        lofs = plan_ref[i, 2 * N_GROUPS + g]

        def copy_of(done, size, goff=goff, lofs=lofs):
            lo = pl.multiple_of(lofs + done, SORT_ALIGN)
            go = pl.multiple_of(goff + done, SORT_ALIGN)
            return pltpu.make_async_copy(src_of(lo, go, size), dst_of(lo, go, size), sem)

        _piece_copies(action, plan_ref[i, N_GROUPS + g], RUN_BITS, copy_of)


def _dispatch_kernel(plan_ref, gap_ref, h_ref, gate_ref, hs_o, h_loc, sem):
    i = pl.program_id(0)
    tm = h_ref.shape[0]
    lofs = [plan_ref[i, 2 * N_GROUPS + g] for g in range(N_GROUPS)]
    pos = _sort_positions(gate_ref[...], lofs)
    onehot_t = jnp.where(lax.broadcasted_iota(jnp.int32, (tm, SORT_LOCAL), 1) == pos, 1.0, 0.0).astype(BF16)
    idx = lax.broadcasted_iota(jnp.int32, (V7X_SUBLANES, SORT_LOCAL), 1)
    pos_row = (_dot_nt((idx // V7X_LANES).astype(F32), onehot_t) * float(V7X_LANES)
               + _dot_nt((idx % V7X_LANES).astype(F32), onehot_t))[0:1].astype(jnp.int32)
    perm = jnp.where(lax.broadcasted_iota(jnp.int32, (SORT_LOCAL, tm), 0) == pos_row, 1.0, 0.0).astype(BF16)
    payload = jnp.concatenate([h_ref[...].astype(BF16)] + list(_split3(gate_ref[...])), axis=1)
    sorted_rows = jnp.dot(perm, payload, preferred_element_type=F32).astype(BF16)

    def copies(action, tile):
        _run_copies(action, plan_ref, tile, lambda lo, go, sz: h_loc.at[pl.ds(lo, sz)],
                    lambda lo, go, sz: hs_o.at[pl.ds(go, sz)], sem)

    @pl.when(i > 0)
    def _():
        copies("wait", i - 1)

    h_loc[...] = sorted_rows
    copies("start", i)

    @pl.when(i == pl.num_programs(0) - 1)
    def _():
        copies("wait", i)
        h_loc[...] = jnp.zeros_like(h_loc)
        for action in ("start", "wait"):
            for g in range(N_GROUPS):
                def copy_of(done, size, g=g):
                    go = pl.multiple_of(gap_ref[g] + done, SORT_ALIGN)
                    return pltpu.make_async_copy(h_loc.at[pl.ds(0, size)], hs_o.at[pl.ds(go, size)], sem)

                _piece_copies(action, gap_ref[N_GROUPS + g], GAP_BITS, copy_of)

            def body(k, carry):
                go = pl.multiple_of(gap_ref[2 * N_GROUPS] + k * EXPERT_ROWS_SHORT, SORT_ALIGN)
                cp = pltpu.make_async_copy(h_loc.at[pl.ds(0, EXPERT_ROWS_SHORT)], hs_o.at[pl.ds(go, EXPERT_ROWS_SHORT)], sem)
                cp.start() if action == "start" else cp.wait()
                return carry

            lax.fori_loop(0, gap_ref[2 * N_GROUPS + 1] // EXPERT_ROWS_SHORT, body, 0)


def _experts_kernel(tile_group_ref, n_valid_ref, hs_ref, w1_ref, w3_ref, w2_ref, ys_o, w1_b, w3_b, w2_b):
    j = pl.program_id(0)
    last = n_valid_ref[0] - 1
    g = tile_group_ref[jnp.minimum(j, last)]
    g_prev = tile_group_ref[jnp.minimum(jnp.maximum(j - 1, 0), last)]

    @pl.when((j == 0) | (g != g_prev))
    def _():
        w1_b[...] = w1_ref[...].astype(BF16)
        w3_b[...] = w3_ref[...].astype(BF16)
        w2_b[...] = w2_ref[...].astype(BF16)

    @pl.when(j <= last)
    def _():
        x = hs_ref[:, :D_MODEL]
        gs = sum(hs_ref[:, D_MODEL + t * ROUTER_LANES:D_MODEL + (t + 1) * ROUTER_LANES].astype(F32) for t in range(3))
        lane = lax.broadcasted_iota(jnp.int32, gs.shape, 1)
        acc = jnp.zeros(ys_o.shape, F32)
        for e in range(EXPERTS_PER_GROUP):
            ge = jnp.sum(jnp.where(lane == FINE_LANE0 + g * EXPERTS_PER_GROUP + e, gs, 0.0), axis=-1, keepdims=True)
            a = jnp.dot(x, w1_b[e], preferred_element_type=F32)
            b = jnp.dot(x, w3_b[e], preferred_element_type=F32)
            hid = (a * _sigmoid(a)) * b * ge
            acc = acc + jnp.dot(hid.astype(BF16), w2_b[e], preferred_element_type=F32)
        ys_o[...] = acc.astype(ys_o.dtype)

    @pl.when(j > last)
    def _():
        ys_o[...] = jnp.zeros_like(ys_o)


def _combine_kernel(group_tiles, plan_ref, h_ref, gate_ref, *refs):
    ng = len(group_tiles)
    p_refs, (ys_ref, ln2g_ref, ln2b_ref, wple_ref, wpg_ref, pleg_ref) = refs[:ng], refs[ng:ng + 6]
    o_refs, (y_loc, sem) = refs[ng + 6:2 * ng + 6], refs[2 * ng + 6:]
    i = pl.program_id(0)
    starts = [sum(group_tiles[:k]) for k in range(ng)]

    def in_group(k):
        return (i >= starts[k]) & (i < starts[k] + group_tiles[k])

    tm = h_ref.shape[0]
    lofs = [plan_ref[i, 2 * N_GROUPS + g] for g in range(N_GROUPS)]

    def fetch(tile):
        slot = tile % 2
        y_loc[slot] = jnp.zeros(y_loc.shape[1:], y_loc.dtype)
        _run_copies("start", plan_ref, tile, lambda lo, go, sz: ys_ref.at[pl.ds(go, sz)],
                    lambda lo, go, sz: y_loc.at[slot, pl.ds(lo, sz)], sem.at[slot])

    @pl.when(i == 0)
    def _():
        fetch(i)

    @pl.when(i + 1 < pl.num_programs(0))
    def _():
        fetch(i + 1)

    pos = _sort_positions(gate_ref[...], lofs)
    onehot_t = jnp.where(lax.broadcasted_iota(jnp.int32, (tm, SORT_LOCAL), 1) == pos, 1.0, 0.0).astype(BF16)
    p_tile = p_refs[0][...]
    for k in range(1, ng):
        p_tile = jnp.where(in_group(k), p_refs[k][...], p_tile)
    ple_in = _dot(p_tile, wple_ref[...])
    slot = i % 2
    _run_copies("wait", plan_ref, i, lambda lo, go, sz: ys_ref.at[pl.ds(go, sz)],
                lambda lo, go, sz: y_loc.at[slot, pl.ds(lo, sz)], sem.at[slot])
    ffn = jnp.dot(onehot_t, y_loc[slot], preferred_element_type=F32)
    h2 = _layer_norm(DEEPNORM_ALPHA * h_ref[...] + ffn, ln2g_ref[...], ln2b_ref[...])
    ple = ple_in * _sigmoid(_dot(h2, wpg_ref[...]))
    ms = jnp.mean(ple * ple, axis=-1, keepdims=True)
    out = h2 + ple * lax.rsqrt(ms + LN_EPS) * pleg_ref[...]
    for k in range(ng):
        @pl.when(in_group(k))
        def _(k=k):
            o_refs[k][...] = out


def _ffn(h, gate, counts, p_groups, W):
    n = h.shape[0]
    tm = SORT_ROWS
    assert n % tm == 0 and tm == POST_ROWS and SORT_LOCAL >= tm + N_GROUPS * SORT_ALIGN
    ntiles = n // tm
    er = EXPERT_ROWS if n >= 2 * N_GROUPS * EXPERT_ROWS else EXPERT_ROWS_SHORT
    assert er <= SORT_ALIGN << GAP_BITS and SORT_ALIGN << (GAP_BITS - 1) <= SORT_LOCAL
    cnt = counts[:, 0, :N_GROUPS]
    run = (cnt + SORT_ALIGN - 1) // SORT_ALIGN * SORT_ALIGN
    lofs = jnp.cumsum(run, axis=1) - run
    seg = (jnp.sum(run, axis=0) + er - 1) // er * er
    gbase = jnp.cumsum(seg) - seg
    goff = gbase[None, :] + jnp.cumsum(run, axis=0) - run
    plan = jnp.concatenate([goff, run, lofs], axis=1).astype(jnp.int32)
    max_tiles = (n + ntiles * N_GROUPS * (SORT_ALIGN - 1)) // er + N_GROUPS
    cap = max_tiles * er
    n_valid = (jnp.sum(seg) // er).astype(jnp.int32).reshape(1)
    tile_start = jnp.arange(max_tiles, dtype=jnp.int32) * er
    tile_group = jnp.clip(jnp.sum(tile_start[:, None] >= (gbase + seg)[None, :], axis=1), 0, N_GROUPS - 1).astype(jnp.int32)

    cparams = dict(vmem_limit_bytes=V7X_VMEM_LIMIT_BYTES)
    any_spec = pl.BlockSpec(memory_space=pl.ANY)
    total = jnp.sum(run, axis=0)
    assert er % EXPERT_ROWS_SHORT == 0 and EXPERT_ROWS_SHORT <= SORT_LOCAL
    used = jnp.sum(seg)
    gaps = jnp.concatenate([gbase + total, seg - total, jnp.stack([used, cap - used])]).astype(jnp.int32)
    hs = pl.pallas_call(
        _dispatch_kernel,
        grid_spec=pltpu.PrefetchScalarGridSpec(
            num_scalar_prefetch=2, grid=(ntiles,),
            in_specs=[pl.BlockSpec((tm, D_MODEL), lambda i, plan, gaps: (i, 0)),
                      pl.BlockSpec((tm, ROUTER_LANES), lambda i, plan, gaps: (i, 0))],
            out_specs=any_spec,
            scratch_shapes=[pltpu.VMEM((SORT_LOCAL, SORTED_WIDTH), BF16), pltpu.SemaphoreType.DMA(())]),
        out_shape=jax.ShapeDtypeStruct((cap, SORTED_WIDTH), BF16),
        compiler_params=pltpu.CompilerParams(dimension_semantics=("arbitrary",), **cparams),
        name="dispatch",
    )(plan, gaps, h, gate)

    def tile_rows(width):
        return pl.BlockSpec((er, width), lambda j, tg, nv: (jnp.minimum(j, nv[0] - 1), 0))

    def group_w(shape):
        return pl.BlockSpec((EXPERTS_PER_GROUP,) + shape, lambda j, tg, nv: (tg[jnp.minimum(j, nv[0] - 1)], 0, 0))

    ys = pl.pallas_call(
        _experts_kernel,
        grid_spec=pltpu.PrefetchScalarGridSpec(
            num_scalar_prefetch=2, grid=(max_tiles,),
            in_specs=[tile_rows(SORTED_WIDTH), group_w((D_MODEL, D_EXPERT)),
                      group_w((D_MODEL, D_EXPERT)), group_w((D_EXPERT, D_MODEL))],
            out_specs=pl.BlockSpec((er, D_MODEL), lambda j, tg, nv: (j, 0)),
            scratch_shapes=[pltpu.VMEM((EXPERTS_PER_GROUP, D_MODEL, D_EXPERT), BF16),
                            pltpu.VMEM((EXPERTS_PER_GROUP, D_MODEL, D_EXPERT), BF16),
                            pltpu.VMEM((EXPERTS_PER_GROUP, D_EXPERT, D_MODEL), BF16)]),
        out_shape=jax.ShapeDtypeStruct((cap, D_MODEL), BF16),
        compiler_params=pltpu.CompilerParams(dimension_semantics=("arbitrary",), **cparams),
        name="experts",
    )(tile_group, n_valid, hs, W["expert_w1"], W["expert_w3"], W["expert_w2"])

    def full(a):
        return pl.BlockSpec(a.shape, lambda i, plan: (0,) * a.ndim)

    params = [W["ln2_g"], W["ln2_b"], W["w_ple"], W["w_ple_gate"], W["ple_norm_g"]]
    group_tiles = tuple(pg.shape[0] // tm for pg in p_groups)
    assert sum(group_tiles) == ntiles
    starts = [sum(group_tiles[:k]) for k in range(len(group_tiles))]
    return pl.pallas_call(
        functools.partial(_combine_kernel, group_tiles),
        grid_spec=pltpu.PrefetchScalarGridSpec(
            num_scalar_prefetch=1, grid=(ntiles,),
            in_specs=[pl.BlockSpec((tm, D_MODEL), lambda i, plan: (i, 0)),
                      pl.BlockSpec((tm, ROUTER_LANES), lambda i, plan: (i, 0))]
            + [_group_rows(tm, D_PLE, st, t) for st, t in zip(starts, group_tiles)]
            + [any_spec] + [full(a) for a in params],
            out_specs=[_group_rows(tm, D_MODEL, st, t) for st, t in zip(starts, group_tiles)],
            scratch_shapes=[pltpu.VMEM((2, SORT_LOCAL, D_MODEL), BF16), pltpu.SemaphoreType.DMA((2,))]),
        out_shape=[jax.ShapeDtypeStruct((t * tm, D_MODEL), F32) for t in group_tiles],
        compiler_params=pltpu.CompilerParams(dimension_semantics=("arbitrary",), **cparams),
        name="combine",
    )(plan, h, gate, *p_groups, ys, *params)


def _mix(x, x_prev, wkv0, ret0, pos0, W):
    n_seq, seq_len, _ = x.shape
    x2 = x.reshape(n_seq * seq_len, D_MODEL)
    r, lw, k, v, al, be, g, bonus, qb, kb, vb, gb = _proj(x2, x_prev, seq_len, pos0, W)
    ya, ob, wkv1, ret1 = _mixer((r, lw, k, v, al, be, qb, kb, vb), wkv0, ret0, n_seq, seq_len)
    return (ya, ob, bonus, g, gb, x2), wkv1, ret1


def _prep_weights(i, w_in, mu_shift, w_decay_up, decay_base, w_aaa_up, aaa_base, w_gate_up, k_k, k_a, r_k,
                  lnx_g, lnx_b, ret_gn_g, ret_gn_b, w_out, ln1_g, ln1_b,
                  router_coarse_w, router_coarse_b, router_fine_w, router_fine_b,
                  expert_w1, expert_w3, expert_w2, ln2_g, ln2_b, w_ple, w_ple_gate, ple_norm_g):
    def row(a):
        return a[i].reshape(1, -1).astype(F32)

    pad = ROUTER_LANES - N_GROUPS - N_EXPERTS
    w_router = jnp.concatenate([router_coarse_w[i], router_fine_w[i], jnp.zeros((D_MODEL, pad), F32)], axis=1)
    b_router = jnp.concatenate([router_coarse_b[i], router_fine_b[i], jnp.zeros((pad,), F32)]).reshape(1, -1)
    return {
        "w_in": w_in[i].astype(BF16), "mu_shift": row(mu_shift), "w_decay_up": w_decay_up[i].astype(BF16),
        "decay_base": row(decay_base), "w_aaa_up": w_aaa_up[i].astype(BF16), "aaa_base": row(aaa_base),
        "w_gate_up": w_gate_up[i].astype(BF16), "k_k": row(k_k), "k_a": row(k_a), "r_k": row(r_k),
        "lnx_g": row(lnx_g), "lnx_b": row(lnx_b), "ret_gn_g": row(ret_gn_g), "ret_gn_b": row(ret_gn_b),
        "w_out": w_out[i].astype(BF16), "ln1_g": row(ln1_g), "ln1_b": row(ln1_b),
        "w_router": w_router, "b_router": b_router,
        "expert_w1": expert_w1[i], "expert_w3": expert_w3[i], "expert_w2": expert_w2[i], "ln2_g": row(ln2_g), "ln2_b": row(ln2_b),
        "w_ple": w_ple[i].astype(BF16), "w_ple_gate": w_ple_gate[i].astype(BF16), "ple_norm_g": row(ple_norm_g),
    }


def kernel(x_prompt, x_sample, p_prompt, p_sample, state_wkv, state_shift, state_ret, w_in, mu_shift, w_decay_up, decay_base, w_aaa_up, aaa_base, w_gate_up, k_k, k_a, r_k, lnx_g, lnx_b, ret_gn_g, ret_gn_b, w_out, ln1_g, ln1_b, router_coarse_w, router_coarse_b, router_fine_w, router_fine_b, expert_w1, expert_w3, expert_w2, ln2_g, ln2_b, w_ple, w_ple_gate, ple_norm_g):
    yp, ys = x_prompt, x_sample
    nb = x_prompt.shape[0]
    depth = w_in.shape[0]
    wkv_p, shift_p, ret_p, wkv_s, shift_s, ret_s = [], [], [], [], [], []
    for i in range(depth):
        W = _prep_weights(i, w_in, mu_shift, w_decay_up, decay_base, w_aaa_up, aaa_base, w_gate_up, k_k, k_a, r_k,
                          lnx_g, lnx_b, ret_gn_g, ret_gn_b, w_out, ln1_g, ln1_b,
                          router_coarse_w, router_coarse_b, router_fine_w, router_fine_b,
                          expert_w1, expert_w3, expert_w2, ln2_g, ln2_b, w_ple, w_ple_gate, ple_norm_g)
        ops_p, wp, rp = _mix(yp, jnp.zeros((nb, D_MODEL), F32), jnp.zeros((nb, A_HEADS, A_HEAD_DIM, A_HEAD_DIM), F32),
                             jnp.zeros((nb, B_HEADS, B_QK_DIM, B_V_DIM), F32), 0, W)
        ops_s, wsm, rsm = _mix(ys, state_shift[i], state_wkv[i], state_ret[i], PAST_LEN, W)
        sp, ss = yp[:, -1], ys[:, -1]
        h, gate, counts = _post([ops_p, ops_s], W)
        out_p, out_s = _ffn(h, gate, counts, [p_prompt[i].reshape(-1, D_PLE), p_sample[i].reshape(-1, D_PLE)], W)
        yp, ys = out_p.reshape(yp.shape), out_s.reshape(ys.shape)
        wkv_p.append(wp); shift_p.append(sp); ret_p.append(rp)
        wkv_s.append(wsm); shift_s.append(ss); ret_s.append(rsm)
    return (yp, ys, jnp.stack(wkv_p, 0), jnp.stack(shift_p, 0), jnp.stack(ret_p, 0),
            jnp.stack(wkv_s, 0), jnp.stack(shift_s, 0), jnp.stack(ret_s, 0))
```

```python
import functools
import math

import numpy as np
import jax
import jax.numpy as jnp
from jax import lax
from jax.experimental import pallas as pl
from jax.experimental.pallas import tpu as pltpu

F32 = jnp.float32
BF16 = jnp.bfloat16

D_MODEL = 1024
D_PLE = 256
A_HEADS = 8
A_HEAD_DIM = 64
A_WIDTH = A_HEADS * A_HEAD_DIM
DECAY_LORA = 64
AAA_LORA = 64
GATE_LORA = 128
GN_EPS_RWKV = 64e-5
B_HEADS = 4
B_QK_DIM = 64
B_V_DIM = 128
B_QK_WIDTH = B_HEADS * B_QK_DIM
B_WIDTH = B_HEADS * B_V_DIM
ROPE_BASE = 10000.0
GN_EPS = 1e-5
SHIFT_WIDTH = 3 * A_WIDTH + DECAY_LORA + AAA_LORA + GATE_LORA
IN_WIDTH = SHIFT_WIDTH + 2 * B_QK_WIDTH + 2 * B_WIDTH
N_GROUPS = 4
EXPERTS_PER_GROUP = 4
N_EXPERTS = N_GROUPS * EXPERTS_PER_GROUP
D_EXPERT = 256
DEPTH = 1
PAST_LEN = 16384
DEEPNORM_ALPHA = (2 * DEPTH) ** 0.25
LN_EPS = 1e-5

V7X_LANES = 128
V7X_SUBLANES = 8
V7X_VMEM_LIMIT_BYTES = 56 * 1024 * 1024

PROJ_ROWS = 1024
LOG_DECAY_OUT = 1
CHUNK_ROWS = 64
MIXER_TILES = 8
MIXER_TILES_PACKED = 2
POST_ROWS = 512
SORT_ROWS = 512
SORT_ALIGN = 16
SORT_LOCAL = 640
EXPERT_ROWS = 1024
EXPERT_ROWS_SHORT = 256
GAP_BITS = 6
RUN_BITS = 6
GROUP_LANE = 0
ROUTER_LANES = V7X_LANES
FINE_LANE0 = N_GROUPS
SORTED_WIDTH = D_MODEL + 3 * ROUTER_LANES


def _dot(a, b):
    return jnp.dot(a.astype(BF16), b.astype(BF16), preferred_element_type=F32)


def _dot_nt(a, b):
    return lax.dot_general(a.astype(BF16), b.astype(BF16), (((1,), (1,)), ((), ())), preferred_element_type=F32)


def _dot_tn(a, b):
    return lax.dot_general(a.astype(BF16), b.astype(BF16), (((0,), (0,)), ((), ())), preferred_element_type=F32)


def _split2(x):
    hi = x.astype(BF16)
    lo = (x - hi.astype(F32)).astype(BF16)
    return hi, lo


def _split3(x):
    hi = x.astype(BF16)
    r1 = x - hi.astype(F32)
    mid = r1.astype(BF16)
    lo = (r1 - mid.astype(F32)).astype(BF16)
    return hi, mid, lo


def _sigmoid(x):
    return 1.0 / (1.0 + jnp.exp(-x))


def _group_ones(width, group):
    r = lax.broadcasted_iota(jnp.int32, (width, width), 0) // group
    c = lax.broadcasted_iota(jnp.int32, (width, width), 1) // group
    return jnp.where(r == c, 1.0, 0.0).astype(BF16)


def _group_sum(x, ones):
    return jnp.dot(x.astype(BF16), ones, preferred_element_type=F32)


def _proj_kernel(carry_mode, seq_len, tiles_per_seq,
                 x_ref, xp_ref, w_ref, mu_ref, wdec_ref, dbase_ref, waaa_ref, abase_ref, wgate_ref,
                 kk_ref, ka_ref, rk_ref, cos_ref, sin_ref,
                 r_o, lw_o, k_o, v_o, al_o, be_o, g_o, bonus_o, qb_o, kb_o, vb_o, gb_o,
                 carry_scr):
    tm = x_ref.shape[0]
    if carry_mode:
        xp = jnp.broadcast_to(xp_ref[0], (V7X_SUBLANES, D_MODEL))
        xb = jnp.concatenate([x_ref[...], xp], axis=0).astype(BF16)
        j = pl.program_id(0) % tiles_per_seq

        @pl.when(pl.program_id(0) == 0)
        def _():
            carry_scr[...] = jnp.zeros_like(carry_scr)
    else:
        xb = x_ref[...].astype(BF16)
        xpb = xp_ref[...].astype(BF16)

    def project(lo, hi):
        return jnp.dot(xb, w_ref[:, lo:hi], preferred_element_type=F32)

    def shifted(p, lo, hi):
        cur = p[:tm]
        row = lax.broadcasted_iota(jnp.int32, cur.shape, 0)
        rolled = pltpu.roll(cur, 1, 0)
        if carry_mode:
            first = jnp.where(j == 0, p[tm + V7X_SUBLANES - 1:], carry_scr[V7X_SUBLANES - 1:V7X_SUBLANES, lo:hi])
            prev = jnp.where(row == 0, first, rolled)
            carry_scr[:, lo:hi] = cur[tm - V7X_SUBLANES:]
        else:
            first = jnp.dot(xpb, w_ref[:, lo:hi], preferred_element_type=F32)
            prev = jnp.where((row & (seq_len - 1)) == 0, first, rolled)
        return cur + (prev - cur) * mu_ref[:, lo:hi]

    c_r, c_k, c_v, c_l = 0, A_WIDTH, 2 * A_WIDTH, 3 * A_WIDTH
    p_lora = project(c_l, SHIFT_WIDTH)
    p_k = project(c_k, c_v)
    p_r = project(c_r, c_k)

    lora = shifted(p_lora, c_l, SHIFT_WIDTH)
    w_lo = lora[:, :DECAY_LORA]
    a_lo = lora[:, DECAY_LORA:DECAY_LORA + AAA_LORA]
    g_lo = lora[:, DECAY_LORA + AAA_LORA:]
    z = -(dbase_ref[...] + _dot(jnp.tanh(w_lo), wdec_ref[...]))
    softplus = jnp.maximum(z, 0.0) + jnp.log(1.0 + jnp.exp(-jnp.abs(z)))
    log_w = -softplus - 0.5
    lw_o[...] = -jnp.exp(log_w)
    a = _sigmoid(abase_ref[...] + _dot(a_lo, waaa_ref[...]))
    g_o[...] = (_dot(_sigmoid(g_lo), wgate_ref[...])).astype(g_o.dtype)

    p_v = project(c_v, c_l)

    ones64 = _group_ones(A_WIDTH, A_HEAD_DIM)
    k0 = shifted(p_k, c_k, c_v)
    kk0 = k0 * kk_ref[...]
    ssq = _group_sum(kk0 * kk0, ones64)
    kk = kk0 * jnp.minimum(lax.rsqrt(ssq), 1e12)
    k = k0 * (1.0 + (a - 1.0) * ka_ref[...])
    k_o[...] = (k).astype(k_o.dtype)
    al_o[...] = (-kk).astype(al_o.dtype)
    be_o[...] = (kk * a).astype(be_o.dtype)

    o = SHIFT_WIDTH
    p_qk = project(o, o + 2 * B_QK_WIDTH)[:tm]

    r = shifted(p_r, c_r, c_k)
    r_o[...] = (r).astype(r_o.dtype)
    rk_sum = _group_sum(r * k * rk_ref[...], ones64)

    p_vb = project(o + 2 * B_QK_WIDTH, o + 2 * B_QK_WIDTH + B_WIDTH)[:tm]

    v = shifted(p_v, c_v, c_l)
    v_o[...] = (v).astype(v_o.dtype)
    bonus_o[...] = (rk_sum * v).astype(bonus_o.dtype)

    p_gb = project(o + 2 * B_QK_WIDTH + B_WIDTH, IN_WIDTH)[:tm]

    q_b = p_qk[:, :B_QK_WIDTH]
    k_b = p_qk[:, B_QK_WIDTH:]
    lane = lax.broadcasted_iota(jnp.int32, (tm, B_QK_WIDTH), 1)
    first_half = (lane & (B_QK_DIM - 1)) < (B_QK_DIM // 2)
    cos = cos_ref[...]
    sin = sin_ref[...]

    def rot(t):
        swapped = jnp.where(first_half, pltpu.roll(t, B_QK_WIDTH - B_QK_DIM // 2, 1), pltpu.roll(t, B_QK_DIM // 2, 1))
        return t * cos + swapped * sin

    qb_o[...] = (rot(q_b)).astype(qb_o.dtype)
    kb_o[...] = (rot(k_b) * (B_QK_DIM ** -0.5)).astype(kb_o.dtype)

    vb_o[...] = (p_vb).astype(vb_o.dtype)
    gb_o[...] = (p_gb * _sigmoid(p_gb)).astype(gb_o.dtype)


def _proj(x2, x_prev, seq_len, pos0, W):
    n = x2.shape[0]
    tm = PROJ_ROWS
    assert n % tm == 0
    carry_mode = seq_len % tm == 0
    if carry_mode:
        tiles_per_seq = seq_len // tm
        xp = x_prev.reshape(-1, 1, D_MODEL)
        xp_spec = pl.BlockSpec((1, 1, D_MODEL), lambda i: (i // tiles_per_seq, 0, 0))
        tab_rows = seq_len
    else:
        assert tm % seq_len == 0 and seq_len & (seq_len - 1) == 0
        tiles_per_seq = 1
        xp = jnp.repeat(x_prev, seq_len, axis=0)
        xp_spec = pl.BlockSpec((tm, D_MODEL), lambda i: (i, 0))
        tab_rows = tm
    half = B_QK_DIM // 2
    inv = ROPE_BASE ** (-jnp.arange(half, dtype=F32) / half)
    pos = (pos0 + jnp.arange(seq_len, dtype=jnp.int32)).astype(F32)
    ang = pos[:, None] * inv[None, :]
    cos = jnp.tile(jnp.concatenate([jnp.cos(ang), jnp.cos(ang)], -1), (tab_rows // seq_len, B_HEADS))
    sin = jnp.tile(jnp.concatenate([-jnp.sin(ang), jnp.sin(ang)], -1), (tab_rows // seq_len, B_HEADS))
    tab_tiles = tab_rows // tm
    tab_spec = pl.BlockSpec((tm, B_QK_WIDTH), lambda i: (i % tab_tiles, 0))

    def full(a):
        return pl.BlockSpec(a.shape, lambda i: (0,) * a.ndim)

    def rows(width):
        return pl.BlockSpec((tm, width), lambda i: (i, 0))

    params = [W["w_in"], W["mu_shift"], W["w_decay_up"], W["decay_base"], W["w_aaa_up"], W["aaa_base"],
              W["w_gate_up"], W["k_k"], W["k_a"], W["r_k"]]
    widths = [A_WIDTH] * 8 + [B_QK_WIDTH, B_QK_WIDTH, B_WIDTH, B_WIDTH]
    outs = pl.pallas_call(
        functools.partial(_proj_kernel, carry_mode, seq_len, tiles_per_seq),
        grid=(n // tm,),
        in_specs=[rows(D_MODEL), xp_spec, pl.BlockSpec(params[0].shape, lambda i: (0, 0), pipeline_mode=pl.Buffered(1))]
        + [full(a) for a in params[1:]] + [tab_spec, tab_spec],
        out_specs=[rows(w) for w in widths],
        out_shape=[jax.ShapeDtypeStruct((n, w), F32 if i == LOG_DECAY_OUT else BF16) for i, w in enumerate(widths)],
        scratch_shapes=[pltpu.VMEM((V7X_SUBLANES, SHIFT_WIDTH), F32)],
        compiler_params=pltpu.CompilerParams(dimension_semantics=("arbitrary",),
                                             vmem_limit_bytes=V7X_VMEM_LIMIT_BYTES),
        name="proj",
    )(x2, xp, *params, cos, sin)
    return outs


def _mixer_kernel(nb, seqs, clen,
                  r_ref, lw_ref, k_ref, v_ref, al_ref, be_ref, qb_ref, kb_ref, vb_ref, wkv0_ref, ret0_ref,
                  ya_o, ob_o, wkv_o, ret_o, s_scr, r_scr):
    R = seqs * clen
    log2c = int(math.log2(clen))
    c_idx = pl.program_id(1)
    hd = A_HEAD_DIM
    pw2 = 2 * hd
    TP = [(t, j) for t in range(nb) for j in range(A_HEADS // 2)]
    TG = [(t, h) for t in range(nb) for h in range(B_HEADS)]

    @pl.when(c_idx == 0)
    def _():
        zero = jnp.zeros((hd, hd), F32)
        for ci, (t, j) in enumerate(TP):
            blocks = [jnp.concatenate([jnp.concatenate([wkv0_ref[t * seqs + i, 2 * j], zero], axis=1),
                                       jnp.concatenate([zero, wkv0_ref[t * seqs + i, 2 * j + 1]], axis=1)], axis=0)
                      for i in range(seqs)]
            s_scr[ci] = jnp.concatenate(blocks, axis=1) if seqs > 1 else blocks[0]
        for t, h in TG:
            r_scr[t * B_HEADS + h] = ret0_ref[t * seqs:(t + 1) * seqs, h].reshape(seqs * B_QK_DIM, B_V_DIM)

    row = lax.broadcasted_iota(jnp.int32, (R, R), 0)
    col = lax.broadcasted_iota(jnp.int32, (R, R), 1)
    same = (row >> log2c) == (col >> log2c)
    incl = same & (col <= row)
    m_incl = jnp.where(incl, 1.0, 0.0).astype(BF16)
    m_same = jnp.where(same, 1.0, 0.0).astype(BF16)

    def expand(t):
        if seqs == 1:
            return t
        w = t.shape[1]
        wide = jnp.concatenate([t] * seqs, axis=1)
        rr = lax.broadcasted_iota(jnp.int32, wide.shape, 0) >> log2c
        cc = lax.broadcasted_iota(jnp.int32, wide.shape, 1) // w
        return jnp.where(rr == cc, wide, 0.0)

    a_bar, r_bar, b_til, k_til, b_dec, k_dec, d_end, vv = [], [], [], [], [], [], [], []
    for t in range(nb):
        lw = lw_ref[t]
        parts = _split2(lw)
        c = sum(jnp.dot(m_incl, p, preferred_element_type=F32) for p in parts)
        if seqs == 1:
            is_last = lax.broadcasted_iota(jnp.int32, c.shape, 0) == R - 1
            cend = jnp.sum(jnp.where(is_last, c, 0.0), axis=0, keepdims=True)
        else:
            cend = sum(jnp.dot(m_same, p, preferred_element_type=F32) for p in parts)
        einv = jnp.exp(-c)
        edec = jnp.exp(cend - c)
        a_bar.append(al_ref[t] * jnp.exp(c - lw))
        r_bar.append(r_ref[t] * jnp.exp(c))
        b_til.append(be_ref[t] * einv)
        k_til.append(k_ref[t] * einv)
        b_dec.append(be_ref[t] * edec)
        k_dec.append(k_ref[t] * edec)
        d_end.append(jnp.exp(cend))
        vv.append(v_ref[t].astype(F32))
    last_row = (lax.broadcasted_iota(jnp.int32, (R, seqs * pw2), 0) & (clen - 1)) == clen - 1

    lane2 = lax.broadcasted_iota(jnp.int32, (R, pw2), 1)
    first = lane2 < hd
    row2 = lax.broadcasted_iota(jnp.int32, (R, pw2), 0)
    col2 = lane2 & (hd - 1)
    same2 = (row2 >> log2c) == (col2 >> log2c)
    incl2 = same2 & (col2 <= row2)
    strict2 = same2 & (col2 < row2)
    eye2 = jnp.where(row2 == col2, 1.0, 0.0).astype(F32)

    def pair(x, j):
        return x[:, j * pw2:(j + 1) * pw2]

    def keep(x, second):
        m = first if x.shape[1] == pw2 else jnp.concatenate([first] * (x.shape[1] // pw2), axis=1)
        return jnp.where(m != second, x, 0.0)

    def blockdiag(y, swapped=False):
        return jnp.concatenate([keep(y, swapped), keep(y, not swapped)], axis=0)

    ams = []
    for t, j in TP:
        lhs = jnp.concatenate([pair(a_bar[t], j), pair(r_bar[t], j)], axis=0)
        bt, kt = pair(b_til[t], j), pair(k_til[t], j)
        lhs0 = jnp.where(jnp.concatenate([first, first], axis=0), lhs, 0.0)
        lhs1 = jnp.where(jnp.concatenate([first, first], axis=0), 0.0, lhs)
        ams.append((_dot_nt(lhs0, jnp.concatenate([bt, kt], axis=0)), _dot_nt(lhs1, jnp.concatenate([kt, bt], axis=0))))

    lgs = [float(np.log1p(-np.exp2(-5.0 - h))) for h in range(B_HEADS)]
    qb = [qb_ref[t].astype(F32) for t in range(nb)]
    kb = [kb_ref[t].astype(F32) for t in range(nb)]
    qs = [qb[t][:, h * B_QK_DIM:(h + 1) * B_QK_DIM] for t, h in TG]
    khs = [kb[t][:, h * B_QK_DIM:(h + 1) * B_QK_DIM] for t, h in TG]
    vbs = [vb_ref[t][:, h * B_V_DIM:(h + 1) * B_V_DIM] for t, h in TG]
    diff = (row - col).astype(F32)
    pos_v = (lax.broadcasted_iota(jnp.int32, (R, B_V_DIM), 0) & (clen - 1)).astype(F32)
    pos_k = (lax.broadcasted_iota(jnp.int32, (R, B_QK_DIM), 0) & (clen - 1)).astype(F32)
    intra = [jnp.where(incl, jnp.exp(lg * diff), 0.0) for lg in lgs]
    cross = [jnp.exp(lg * (pos_v + 1.0)) for lg in lgs]
    kdec = [jnp.exp(lg * (clen - 1.0 - pos_k)) for lg in lgs]
    rstates = [r_scr[t * B_HEADS + h] for t, h in TG]
    scs = [_dot_nt(qs[i], khs[i]) * intra[h] for i, (t, h) in enumerate(TG)]
    qst = [_dot(expand(qs[i]), rstates[i]) * cross[h] for i, (t, h) in enumerate(TG)]

    n = len(TP)
    a_ab = [jnp.where(strict2, jnp.where(first, m0[:R], m1[:R]), 0.0) for m0, m1 in ams]
    a_ak = [jnp.where(strict2, jnp.where(first, m1[:R], m0[:R]), 0.0) for m0, m1 in ams]
    a_rb = [jnp.where(incl2, jnp.where(first, m0[R:], m1[R:]), 0.0) for m0, m1 in ams]
    a_rk = [jnp.where(incl2, jnp.where(first, m1[R:], m0[R:]), 0.0) for m0, m1 in ams]
    vps = [pair(vv[t], j) for t, j in TP]
    av = [_dot(a_ak[i], blockdiag(vps[i], swapped=True)) for i in range(n)]
    tinv = [eye2 + a for a in a_ab]
    if log2c > 1:
        pw = [_dot(a, blockdiag(a)) for a in a_ab]
    for it in range(log2c - 1):
        if it < log2c - 2:
            tp = [_dot(pw[i], blockdiag(jnp.concatenate([tinv[i], pw[i]], axis=1))) for i in range(n)]
            tinv = [tinv[i] + tp[i][:, :pw2] for i in range(n)]
            pw = [tp[i][:, pw2:] for i in range(n)]
        else:
            tinv = [tinv[i] + _dot(pw[i], blockdiag(tinv[i])) for i in range(n)]

    os_ = [_dot(scs[i], vbs[i]) + qst[i] for i in range(len(TG))]
    for i, (t, h) in enumerate(TG):
        r_scr[t * B_HEADS + h] = (rstates[i] * float(np.exp(lgs[h] * clen))
                                  + _dot_tn(expand(khs[i] * kdec[h]), vbs[i]))
    for t in range(nb):
        ob_o[t] = jnp.concatenate(os_[t * B_HEADS:(t + 1) * B_HEADS], axis=1)

    wu = [_dot(tinv[i], blockdiag(jnp.concatenate([pair(a_bar[t], j), av[i]], axis=1)))
          for i, (t, j) in enumerate(TP)]
    states = [s_scr[i] for i in range(n)]
    ws = [_dot_nt(jnp.concatenate([expand(wu[i][:, :pw2]), expand(pair(r_bar[t], j))], axis=0), states[i])
          for i, (t, j) in enumerate(TP)]
    us = [ws[i][:R] + wu[i][:, pw2:] for i in range(n)]
    ys = [ws[i][R:] + _dot(jnp.concatenate([a_rb[i], a_rk[i]], axis=1),
                           jnp.concatenate([blockdiag(us[i]), blockdiag(vps[i], swapped=True)], axis=0))
          for i in range(n)]
    npair = A_HEADS // 2
    for t in range(nb):
        ya_o[t] = jnp.concatenate(ys[t * npair:(t + 1) * npair], axis=1)
    rowp = lax.broadcasted_iota(jnp.int32, (pw2, seqs * pw2), 0) >= hd
    lanep = (lax.broadcasted_iota(jnp.int32, (pw2, seqs * pw2), 1) & (pw2 - 1)) >= hd
    for i, (t, j) in enumerate(TP):
        d_row = jnp.sum(jnp.where(last_row, expand(pair(d_end[t], j)), 0.0), axis=0, keepdims=True)
        bk = jnp.concatenate([expand(pair(b_dec[t], j)), expand(pair(k_dec[t], j))], axis=0)
        upd = _dot_tn(jnp.concatenate([us[i], vps[i]], axis=0), bk)
        s_scr[i] = states[i] * d_row + jnp.where(rowp == lanep, upd, 0.0)

    @pl.when(c_idx == pl.num_programs(1) - 1)
    def _():
        for ci, (t, j) in enumerate(TP):
            st = s_scr[ci]
            for i in range(seqs):
                wkv_o[t * seqs + i, 2 * j] = st[:hd, i * pw2:i * pw2 + hd]
                wkv_o[t * seqs + i, 2 * j + 1] = st[hd:, i * pw2 + hd:(i + 1) * pw2]
        for t, h in TG:
            ret_o[t * seqs:(t + 1) * seqs, h] = r_scr[t * B_HEADS + h].reshape(seqs, B_QK_DIM, B_V_DIM)


def _mixer(ops, wkv0, ret0, n_seq, seq_len):
    n = ops[0].shape[0]
    R = CHUNK_ROWS
    if seq_len >= R:
        assert seq_len % R == 0
        seqs, clen, nchunks = 1, R, seq_len // R
    else:
        assert R % seq_len == 0 and seq_len & (seq_len - 1) == 0 and n_seq % (R // seq_len) == 0
        seqs, clen, nchunks = R // seq_len, seq_len, 1
    ntiles = n_seq // seqs
    nb = MIXER_TILES if seqs == 1 else MIXER_TILES_PACKED
    assert ntiles % nb == 0
    ops3 = [a.reshape(ntiles, nchunks * R, a.shape[1]) for a in ops]

    def rows(width):
        return pl.BlockSpec((nb, R, width), lambda i, c: (i, c, 0))

    wkv_spec = pl.BlockSpec((nb * seqs, A_HEADS, A_HEAD_DIM, A_HEAD_DIM), lambda i, c: (i, 0, 0, 0))
    ret_spec = pl.BlockSpec((nb * seqs, B_HEADS, B_QK_DIM, B_V_DIM), lambda i, c: (i, 0, 0, 0))
    ya, ob, wkv1, ret1 = pl.pallas_call(
        functools.partial(_mixer_kernel, nb, seqs, clen),
        grid=(ntiles // nb, nchunks),
        in_specs=[rows(A_WIDTH)] * 6 + [rows(B_QK_WIDTH), rows(B_QK_WIDTH), rows(B_WIDTH), wkv_spec, ret_spec],
        out_specs=[rows(A_WIDTH), rows(B_WIDTH), wkv_spec, ret_spec],
        out_shape=[jax.ShapeDtypeStruct((ntiles, nchunks * R, A_WIDTH), F32),
                   jax.ShapeDtypeStruct((ntiles, nchunks * R, B_WIDTH), F32),
                   jax.ShapeDtypeStruct(wkv0.shape, F32), jax.ShapeDtypeStruct(ret0.shape, F32)],
        scratch_shapes=[pltpu.VMEM((nb * A_HEADS // 2, 2 * A_HEAD_DIM, seqs * 2 * A_HEAD_DIM), F32),
                        pltpu.VMEM((nb * B_HEADS, seqs * B_QK_DIM, B_V_DIM), F32)],
        compiler_params=pltpu.CompilerParams(dimension_semantics=("parallel", "arbitrary"),
                                             vmem_limit_bytes=V7X_VMEM_LIMIT_BYTES),
        name="mixer",
    )(*ops3, wkv0, ret0)
    return ya.reshape(n, A_WIDTH), ob.reshape(n, B_WIDTH), wkv1, ret1


def _layer_norm(z, g, b):
    mu = jnp.mean(z, axis=-1, keepdims=True)
    d = z - mu
    var = jnp.mean(d * d, axis=-1, keepdims=True)
    return d * lax.rsqrt(var + LN_EPS) * g + b


def _post_tile(ya_ref, ob_ref, bonus_ref, g_ref, gb_ref, x_ref, lnxg_ref, lnxb_ref, rgg_ref, rgb_ref,
               wout_ref, ln1g_ref, ln1b_ref, wr_ref, br_ref, h_o, gate_o, cnt_o):
    tm = x_ref.shape[0]

    def head_norm(t, group, eps, gg, bb):
        ones = _group_ones(t.shape[1], group)
        mu = _group_sum(t, ones) * (1.0 / group)
        d = t - mu
        var = _group_sum(d * d, ones) * (1.0 / group)
        return d * lax.rsqrt(var + eps) * gg + bb

    y_a = (head_norm(ya_ref[...], A_HEAD_DIM, GN_EPS_RWKV, lnxg_ref[...], lnxb_ref[...]) + bonus_ref[...]) * g_ref[...]
    y_b = head_norm(ob_ref[...], B_V_DIM, GN_EPS, rgg_ref[...], rgb_ref[...]) * gb_ref[...]
    y = jnp.concatenate([y_a, y_b], axis=1)
    mix = _dot(y, wout_ref[...])
    h = _layer_norm(DEEPNORM_ALPHA * x_ref[...] + mix, ln1g_ref[...], ln1b_ref[...])
    h_o[...] = h

    h_hi, h_lo = _split2(h)
    w_hi, w_lo = _split2(wr_ref[...])
    logits = (jnp.dot(h_hi, w_hi, preferred_element_type=F32) + jnp.dot(h_hi, w_lo, preferred_element_type=F32)
              + jnp.dot(h_lo, w_hi, preferred_element_type=F32)) + br_ref[...]
    lane = lax.broadcasted_iota(jnp.int32, (tm, ROUTER_LANES), 1).astype(F32)
    neg = -jnp.inf
    big = float(ROUTER_LANES)
    cl = jnp.where(lane < N_GROUPS, logits, neg)
    cmax = jnp.max(cl, axis=-1, keepdims=True)
    grp = jnp.min(jnp.where(cl == cmax, lane, big), axis=-1, keepdims=True)
    gprob = 1.0 / jnp.sum(jnp.exp(cl - cmax), axis=-1, keepdims=True)
    lo_lane = FINE_LANE0 + grp * EXPERTS_PER_GROUP
    fv = jnp.where((lane >= lo_lane) & (lane < lo_lane + EXPERTS_PER_GROUP), logits, neg)
    m1 = jnp.max(fv, axis=-1, keepdims=True)
    i1 = jnp.min(jnp.where(fv == m1, lane, big), axis=-1, keepdims=True)
    fv2 = jnp.where(lane == i1, neg, fv)
    m2 = jnp.max(fv2, axis=-1, keepdims=True)
    i2 = jnp.min(jnp.where(fv2 == m2, lane, big), axis=-1, keepdims=True)
    e2 = jnp.exp(m2 - m1)
    w1 = gprob / (1.0 + e2)
    w2 = gprob * e2 / (1.0 + e2)
    gate_o[...] = (jnp.where(lane == i1, w1, 0.0) + jnp.where(lane == i2, w2, 0.0)
                   + jnp.where(lane == GROUP_LANE, grp, 0.0))
    onehot = jnp.where((lane == grp) & (lane < N_GROUPS), 1.0, 0.0)
    cnt_o[0] = jnp.broadcast_to(jnp.sum(onehot, axis=0, keepdims=True), (V7X_SUBLANES, ROUTER_LANES)).astype(jnp.int32)


POST_OPERANDS = 6


def _post_kernel(group_tiles, *refs):
    per_group, shared = refs[:POST_OPERANDS * len(group_tiles)], refs[POST_OPERANDS * len(group_tiles):]
    i = pl.program_id(0)
    start = 0
    for k, tiles in enumerate(group_tiles):
        @pl.when((i >= start) & (i < start + tiles))
        def _(k=k):
            _post_tile(*per_group[POST_OPERANDS * k:POST_OPERANDS * (k + 1)], *shared)

        start += tiles


def _group_rows(tm, width, start, tiles):
    return pl.BlockSpec((tm, width), lambda i, *_: (jnp.clip(i - start, 0, tiles - 1), 0))


def _post(groups, W):
    tm = POST_ROWS
    group_tiles = tuple(g[-1].shape[0] // tm for g in groups)
    assert all(g[-1].shape[0] % tm == 0 for g in groups)
    n = tm * sum(group_tiles)

    def full(a):
        return pl.BlockSpec(a.shape, lambda i: (0,) * a.ndim)

    def rows(width):
        return pl.BlockSpec((tm, width), lambda i: (i, 0))

    widths = [A_WIDTH, B_WIDTH, A_WIDTH, A_WIDTH, B_WIDTH, D_MODEL]
    in_specs, start = [], 0
    for tiles in group_tiles:
        in_specs += [_group_rows(tm, w, start, tiles) for w in widths]
        start += tiles
    params = [W["lnx_g"], W["lnx_b"], W["ret_gn_g"], W["ret_gn_b"], W["w_out"], W["ln1_g"], W["ln1_b"],
              W["w_router"], W["b_router"]]
    return pl.pallas_call(
        functools.partial(_post_kernel, group_tiles),
        grid=(n // tm,),
        in_specs=in_specs + [full(a) for a in params],
        out_specs=[rows(D_MODEL), rows(ROUTER_LANES),
                   pl.BlockSpec((1, V7X_SUBLANES, ROUTER_LANES), lambda i: (i, 0, 0))],
        out_shape=[jax.ShapeDtypeStruct((n, D_MODEL), F32), jax.ShapeDtypeStruct((n, ROUTER_LANES), F32),
                   jax.ShapeDtypeStruct((n // tm, V7X_SUBLANES, ROUTER_LANES), jnp.int32)],
        compiler_params=pltpu.CompilerParams(dimension_semantics=("parallel",),
                                             vmem_limit_bytes=V7X_VMEM_LIMIT_BYTES),
        name="post",
    )(*[a for g in groups for a in g], *params)


def _sort_positions(gate, lofs):
    tm = gate.shape[0]
    lane = lax.broadcasted_iota(jnp.int32, (tm, ROUTER_LANES), 1)
    grp = gate[:, GROUP_LANE:GROUP_LANE + 1].astype(jnp.int32)
    onehot = jnp.where((lane == grp) & (lane < N_GROUPS), 1.0, 0.0)
    r = lax.broadcasted_iota(jnp.int32, (tm, tm), 0)
    c = lax.broadcasted_iota(jnp.int32, (tm, tm), 1)
    earlier = jnp.where(c < r, 1.0, 0.0).astype(BF16)
    prefix = jnp.dot(earlier, onehot.astype(BF16), preferred_element_type=F32)
    base = jnp.zeros((tm, ROUTER_LANES), F32)
    for g in range(N_GROUPS):
        base = jnp.where(lane == g, lofs[g].astype(F32), base)
    return jnp.sum(onehot * (base + prefix), axis=1, keepdims=True).astype(jnp.int32)


def _piece_copies(action, rows, bits, copy_of):
    k = rows // SORT_ALIGN
    for b in reversed(range(bits)):
        size = SORT_ALIGN << b

        @pl.when(((k >> b) & 1) == 1)
        def _():
            done = ((k >> (b + 1)) << (b + 1)) * SORT_ALIGN
            cp = copy_of(done, size)
            cp.start() if action == "start" else cp.wait()


def _run_copies(action, plan_ref, i, src_of, dst_of, sem):
    for g in range(N_GROUPS):
        goff = plan_ref[i, g]
        lofs = plan_ref[i, 2 * N_GROUPS + g]

        def copy_of(done, size, goff=goff, lofs=lofs):
            lo = pl.multiple_of(lofs + done, SORT_ALIGN)
            go = pl.multiple_of(goff + done, SORT_ALIGN)
            return pltpu.make_async_copy(src_of(lo, go, size), dst_of(lo, go, size), sem)

        _piece_copies(action, plan_ref[i, N_GROUPS + g], RUN_BITS, copy_of)


def _dispatch_kernel(plan_ref, gap_ref, h_ref, gate_ref, hs_o, h_loc, sem):
    i = pl.program_id(0)
    tm = h_ref.shape[0]
    lofs = [plan_ref[i, 2 * N_GROUPS + g] for g in range(N_GROUPS)]
    pos = _sort_positions(gate_ref[...], lofs)
    onehot_t = jnp.where(lax.broadcasted_iota(jnp.int32, (tm, SORT_LOCAL), 1) == pos, 1.0, 0.0).astype(BF16)
    idx = lax.broadcasted_iota(jnp.int32, (V7X_SUBLANES, SORT_LOCAL), 1)
    pos_row = (_dot_nt((idx // V7X_LANES).astype(F32), onehot_t) * float(V7X_LANES)
               + _dot_nt((idx % V7X_LANES).astype(F32), onehot_t))[0:1].astype(jnp.int32)
    perm = jnp.where(lax.broadcasted_iota(jnp.int32, (SORT_LOCAL, tm), 0) == pos_row, 1.0, 0.0).astype(BF16)
    payload = jnp.concatenate([h_ref[...].astype(BF16)] + list(_split3(gate_ref[...])), axis=1)
    sorted_rows = jnp.dot(perm, payload, preferred_element_type=F32).astype(BF16)

    def copies(action, tile):
        _run_copies(action, plan_ref, tile, lambda lo, go, sz: h_loc.at[pl.ds(lo, sz)],
                    lambda lo, go, sz: hs_o.at[pl.ds(go, sz)], sem)

    @pl.when(i > 0)
    def _():
        copies("wait", i - 1)

    h_loc[...] = sorted_rows
    copies("start", i)

    @pl.when(i == pl.num_programs(0) - 1)
    def _():
        copies("wait", i)
        h_loc[...] = jnp.zeros_like(h_loc)
        for action in ("start", "wait"):
            for g in range(N_GROUPS):
                def copy_of(done, size, g=g):
                    go = pl.multiple_of(gap_ref[g] + done, SORT_ALIGN)
                    return pltpu.make_async_copy(h_loc.at[pl.ds(0, size)], hs_o.at[pl.ds(go, size)], sem)

                _piece_copies(action, gap_ref[N_GROUPS + g], GAP_BITS, copy_of)

            def body(k, carry):
                go = pl.multiple_of(gap_ref[2 * N_GROUPS] + k * EXPERT_ROWS_SHORT, SORT_ALIGN)
                cp = pltpu.make_async_copy(h_loc.at[pl.ds(0, EXPERT_ROWS_SHORT)], hs_o.at[pl.ds(go, EXPERT_ROWS_SHORT)], sem)
                cp.start() if action == "start" else cp.wait()
                return carry

            lax.fori_loop(0, gap_ref[2 * N_GROUPS + 1] // EXPERT_ROWS_SHORT, body, 0)


def _experts_kernel(tile_group_ref, n_valid_ref, hs_ref, w1_ref, w3_ref, w2_ref, ys_o, w1_b, w3_b, w2_b):
    j = pl.program_id(0)
    last = n_valid_ref[0] - 1
    g = tile_group_ref[jnp.minimum(j, last)]
    g_prev = tile_group_ref[jnp.minimum(jnp.maximum(j - 1, 0), last)]

    @pl.when((j == 0) | (g != g_prev))
    def _():
        w1_b[...] = w1_ref[...].astype(BF16)
        w3_b[...] = w3_ref[...].astype(BF16)
        w2_b[...] = w2_ref[...].astype(BF16)

    @pl.when(j <= last)
    def _():
        x = hs_ref[:, :D_MODEL]
        gs = sum(hs_ref[:, D_MODEL + t * ROUTER_LANES:D_MODEL + (t + 1) * ROUTER_LANES].astype(F32) for t in range(3))
        lane = lax.broadcasted_iota(jnp.int32, gs.shape, 1)
        acc = jnp.zeros(ys_o.shape, F32)
        for e in range(EXPERTS_PER_GROUP):
            ge = jnp.sum(jnp.where(lane == FINE_LANE0 + g * EXPERTS_PER_GROUP + e, gs, 0.0), axis=-1, keepdims=True)
            a = jnp.dot(x, w1_b[e], preferred_element_type=F32)
            b = jnp.dot(x, w3_b[e], preferred_element_type=F32)
            hid = (a * _sigmoid(a)) * b * ge
            acc = acc + jnp.dot(hid.astype(BF16), w2_b[e], preferred_element_type=F32)
        ys_o[...] = acc.astype(ys_o.dtype)

    @pl.when(j > last)
    def _():
        ys_o[...] = jnp.zeros_like(ys_o)


def _combine_kernel(group_tiles, plan_ref, h_ref, gate_ref, *refs):
    ng = len(group_tiles)
    p_refs, (ys_ref, ln2g_ref, ln2b_ref, wple_ref, wpg_ref, pleg_ref) = refs[:ng], refs[ng:ng + 6]
    o_refs, (y_loc, sem) = refs[ng + 6:2 * ng + 6], refs[2 * ng + 6:]
    i = pl.program_id(0)
    starts = [sum(group_tiles[:k]) for k in range(ng)]

    def in_group(k):
        return (i >= starts[k]) & (i < starts[k] + group_tiles[k])

    tm = h_ref.shape[0]
    lofs = [plan_ref[i, 2 * N_GROUPS + g] for g in range(N_GROUPS)]

    def fetch(tile):
        slot = tile % 2
        y_loc[slot] = jnp.zeros(y_loc.shape[1:], y_loc.dtype)
        _run_copies("start", plan_ref, tile, lambda lo, go, sz: ys_ref.at[pl.ds(go, sz)],
                    lambda lo, go, sz: y_loc.at[slot, pl.ds(lo, sz)], sem.at[slot])

    @pl.when(i == 0)
    def _():
        fetch(i)

    @pl.when(i + 1 < pl.num_programs(0))
    def _():
        fetch(i + 1)

    pos = _sort_positions(gate_ref[...], lofs)
    onehot_t = jnp.where(lax.broadcasted_iota(jnp.int32, (tm, SORT_LOCAL), 1) == pos, 1.0, 0.0).astype(BF16)
    p_tile = p_refs[0][...]
    for k in range(1, ng):
        p_tile = jnp.where(in_group(k), p_refs[k][...], p_tile)
    ple_in = _dot(p_tile, wple_ref[...])
    slot = i % 2
    _run_copies("wait", plan_ref, i, lambda lo, go, sz: ys_ref.at[pl.ds(go, sz)],
                lambda lo, go, sz: y_loc.at[slot, pl.ds(lo, sz)], sem.at[slot])
    ffn = jnp.dot(onehot_t, y_loc[slot], preferred_element_type=F32)
    h2 = _layer_norm(DEEPNORM_ALPHA * h_ref[...] + ffn, ln2g_ref[...], ln2b_ref[...])
    ple = ple_in * _sigmoid(_dot(h2, wpg_ref[...]))
    ms = jnp.mean(ple * ple, axis=-1, keepdims=True)
    out = h2 + ple * lax.rsqrt(ms + LN_EPS) * pleg_ref[...]
    for k in range(ng):
        @pl.when(in_group(k))
        def _(k=k):
            o_refs[k][...] = out


def _ffn(h, gate, counts, p_groups, W):
    n = h.shape[0]
    tm = SORT_ROWS
    assert n % tm == 0 and tm == POST_ROWS and SORT_LOCAL >= tm + N_GROUPS * SORT_ALIGN
    ntiles = n // tm
    er = EXPERT_ROWS if n >= 2 * N_GROUPS * EXPERT_ROWS else EXPERT_ROWS_SHORT
    assert er <= SORT_ALIGN << GAP_BITS and SORT_ALIGN << (GAP_BITS - 1) <= SORT_LOCAL
    cnt = counts[:, 0, :N_GROUPS]
    run = (cnt + SORT_ALIGN - 1) // SORT_ALIGN * SORT_ALIGN
    lofs = jnp.cumsum(run, axis=1) - run
    seg = (jnp.sum(run, axis=0) + er - 1) // er * er
    gbase = jnp.cumsum(seg) - seg
    goff = gbase[None, :] + jnp.cumsum(run, axis=0) - run
    plan = jnp.concatenate([goff, run, lofs], axis=1).astype(jnp.int32)
    max_tiles = (n + ntiles * N_GROUPS * (SORT_ALIGN - 1)) // er + N_GROUPS
    cap = max_tiles * er
    n_valid = (jnp.sum(seg) // er).astype(jnp.int32).reshape(1)
    tile_start = jnp.arange(max_tiles, dtype=jnp.int32) * er
    tile_group = jnp.clip(jnp.sum(tile_start[:, None] >= (gbase + seg)[None, :], axis=1), 0, N_GROUPS - 1).astype(jnp.int32)

    cparams = dict(vmem_limit_bytes=V7X_VMEM_LIMIT_BYTES)
    any_spec = pl.BlockSpec(memory_space=pl.ANY)
    total = jnp.sum(run, axis=0)
    assert er % EXPERT_ROWS_SHORT == 0 and EXPERT_ROWS_SHORT <= SORT_LOCAL
    used = jnp.sum(seg)
    gaps = jnp.concatenate([gbase + total, seg - total, jnp.stack([used, cap - used])]).astype(jnp.int32)
    hs = pl.pallas_call(
        _dispatch_kernel,
        grid_spec=pltpu.PrefetchScalarGridSpec(
            num_scalar_prefetch=2, grid=(ntiles,),
            in_specs=[pl.BlockSpec((tm, D_MODEL), lambda i, plan, gaps: (i, 0)),
                      pl.BlockSpec((tm, ROUTER_LANES), lambda i, plan, gaps: (i, 0))],
            out_specs=any_spec,
            scratch_shapes=[pltpu.VMEM((SORT_LOCAL, SORTED_WIDTH), BF16), pltpu.SemaphoreType.DMA(())]),
        out_shape=jax.ShapeDtypeStruct((cap, SORTED_WIDTH), BF16),
        compiler_params=pltpu.CompilerParams(dimension_semantics=("arbitrary",), **cparams),
        name="dispatch",
    )(plan, gaps, h, gate)

    def tile_rows(width):
        return pl.BlockSpec((er, width), lambda j, tg, nv: (jnp.minimum(j, nv[0] - 1), 0))

    def group_w(shape):
        return pl.BlockSpec((EXPERTS_PER_GROUP,) + shape, lambda j, tg, nv: (tg[jnp.minimum(j, nv[0] - 1)], 0, 0))

    ys = pl.pallas_call(
        _experts_kernel,
        grid_spec=pltpu.PrefetchScalarGridSpec(
            num_scalar_prefetch=2, grid=(max_tiles,),
            in_specs=[tile_rows(SORTED_WIDTH), group_w((D_MODEL, D_EXPERT)),
                      group_w((D_MODEL, D_EXPERT)), group_w((D_EXPERT, D_MODEL))],
            out_specs=pl.BlockSpec((er, D_MODEL), lambda j, tg, nv: (j, 0)),
            scratch_shapes=[pltpu.VMEM((EXPERTS_PER_GROUP, D_MODEL, D_EXPERT), BF16),
                            pltpu.VMEM((EXPERTS_PER_GROUP, D_MODEL, D_EXPERT), BF16),
                            pltpu.VMEM((EXPERTS_PER_GROUP, D_EXPERT, D_MODEL), BF16)]),
        out_shape=jax.ShapeDtypeStruct((cap, D_MODEL), BF16),
        compiler_params=pltpu.CompilerParams(dimension_semantics=("arbitrary",), **cparams),
        name="experts",
    )(tile_group, n_valid, hs, W["expert_w1"], W["expert_w3"], W["expert_w2"])

    def full(a):
        return pl.BlockSpec(a.shape, lambda i, plan: (0,) * a.ndim)

    params = [W["ln2_g"], W["ln2_b"], W["w_ple"], W["w_ple_gate"], W["ple_norm_g"]]
    group_tiles = tuple(pg.shape[0] // tm for pg in p_groups)
    assert sum(group_tiles) == ntiles
    starts = [sum(group_tiles[:k]) for k in range(len(group_tiles))]
    return pl.pallas_call(
        functools.partial(_combine_kernel, group_tiles),
        grid_spec=pltpu.PrefetchScalarGridSpec(
            num_scalar_prefetch=1, grid=(ntiles,),
            in_specs=[pl.BlockSpec((tm, D_MODEL), lambda i, plan: (i, 0)),
                      pl.BlockSpec((tm, ROUTER_LANES), lambda i, plan: (i, 0))]
            + [_group_rows(tm, D_PLE, st, t) for st, t in zip(starts, group_tiles)]
            + [any_spec] + [full(a) for a in params],
            out_specs=[_group_rows(tm, D_MODEL, st, t) for st, t in zip(starts, group_tiles)],
            scratch_shapes=[pltpu.VMEM((2, SORT_LOCAL, D_MODEL), BF16), pltpu.SemaphoreType.DMA((2,))]),
        out_shape=[jax.ShapeDtypeStruct((t * tm, D_MODEL), F32) for t in group_tiles],
        compiler_params=pltpu.CompilerParams(dimension_semantics=("arbitrary",), **cparams),
        name="combine",
    )(plan, h, gate, *p_groups, ys, *params)


def _mix(x, x_prev, wkv0, ret0, pos0, W):
    n_seq, seq_len, _ = x.shape
    x2 = x.reshape(n_seq * seq_len, D_MODEL)
    r, lw, k, v, al, be, g, bonus, qb, kb, vb, gb = _proj(x2, x_prev, seq_len, pos0, W)
    ya, ob, wkv1, ret1 = _mixer((r, lw, k, v, al, be, qb, kb, vb), wkv0, ret0, n_seq, seq_len)
    return (ya, ob, bonus, g, gb, x2), wkv1, ret1


def _prep_weights(i, w_in, mu_shift, w_decay_up, decay_base, w_aaa_up, aaa_base, w_gate_up, k_k, k_a, r_k,
                  lnx_g, lnx_b, ret_gn_g, ret_gn_b, w_out, ln1_g, ln1_b,
                  router_coarse_w, router_coarse_b, router_fine_w, router_fine_b,
                  expert_w1, expert_w3, expert_w2, ln2_g, ln2_b, w_ple, w_ple_gate, ple_norm_g):
    def row(a):
        return a[i].reshape(1, -1).astype(F32)

    pad = ROUTER_LANES - N_GROUPS - N_EXPERTS
    w_router = jnp.concatenate([router_coarse_w[i], router_fine_w[i], jnp.zeros((D_MODEL, pad), F32)], axis=1)
    b_router = jnp.concatenate([router_coarse_b[i], router_fine_b[i], jnp.zeros((pad,), F32)]).reshape(1, -1)
    return {
        "w_in": w_in[i].astype(BF16), "mu_shift": row(mu_shift), "w_decay_up": w_decay_up[i].astype(BF16),
        "decay_base": row(decay_base), "w_aaa_up": w_aaa_up[i].astype(BF16), "aaa_base": row(aaa_base),
        "w_gate_up": w_gate_up[i].astype(BF16), "k_k": row(k_k), "k_a": row(k_a), "r_k": row(r_k),
        "lnx_g": row(lnx_g), "lnx_b": row(lnx_b), "ret_gn_g": row(ret_gn_g), "ret_gn_b": row(ret_gn_b),
        "w_out": w_out[i].astype(BF16), "ln1_g": row(ln1_g), "ln1_b": row(ln1_b),
        "w_router": w_router, "b_router": b_router,
        "expert_w1": expert_w1[i], "expert_w3": expert_w3[i], "expert_w2": expert_w2[i], "ln2_g": row(ln2_g), "ln2_b": row(ln2_b),
        "w_ple": w_ple[i].astype(BF16), "w_ple_gate": w_ple_gate[i].astype(BF16), "ple_norm_g": row(ple_norm_g),
    }


def kernel(x_prompt, x_sample, p_prompt, p_sample, state_wkv, state_shift, state_ret, w_in, mu_shift, w_decay_up, decay_base, w_aaa_up, aaa_base, w_gate_up, k_k, k_a, r_k, lnx_g, lnx_b, ret_gn_g, ret_gn_b, w_out, ln1_g, ln1_b, router_coarse_w, router_coarse_b, router_fine_w, router_fine_b, expert_w1, expert_w3, expert_w2, ln2_g, ln2_b, w_ple, w_ple_gate, ple_norm_g):
    yp, ys = x_prompt, x_sample
    nb = x_prompt.shape[0]
    depth = w_in.shape[0]
    wkv_p, shift_p, ret_p, wkv_s, shift_s, ret_s = [], [], [], [], [], []
    for i in range(depth):
        W = _prep_weights(i, w_in, mu_shift, w_decay_up, decay_base, w_aaa_up, aaa_base, w_gate_up, k_k, k_a, r_k,
                          lnx_g, lnx_b, ret_gn_g, ret_gn_b, w_out, ln1_g, ln1_b,
                          router_coarse_w, router_coarse_b, router_fine_w, router_fine_b,
                          expert_w1, expert_w3, expert_w2, ln2_g, ln2_b, w_ple, w_ple_gate, ple_norm_g)
        ops_p, wp, rp = _mix(yp, jnp.zeros((nb, D_MODEL), F32), jnp.zeros((nb, A_HEADS, A_HEAD_DIM, A_HEAD_DIM), F32),
                             jnp.zeros((nb, B_HEADS, B_QK_DIM, B_V_DIM), F32), 0, W)
        ops_s, wsm, rsm = _mix(ys, state_shift[i], state_wkv[i], state_ret[i], PAST_LEN, W)
        sp, ss = yp[:, -1], ys[:, -1]
        h, gate, counts = _post([ops_p, ops_s], W)
        out_p, out_s = _ffn(h, gate, counts, [p_prompt[i].reshape(-1, D_PLE), p_sample[i].reshape(-1, D_PLE)], W)
        yp, ys = out_p.reshape(yp.shape), out_s.reshape(ys.shape)
        wkv_p.append(wp); shift_p.append(sp); ret_p.append(rp)
        wkv_s.append(wsm); shift_s.append(ss); ret_s.append(rsm)
    return (yp, ys, jnp.stack(wkv_p, 0), jnp.stack(shift_p, 0), jnp.stack(ret_p, 0),
            jnp.stack(wkv_s, 0), jnp.stack(shift_s, 0), jnp.stack(ret_s, 0))
```

```python
import functools
import math

import numpy as np
import jax
import jax.numpy as jnp
from jax import lax
from jax.experimental import pallas as pl
from jax.experimental.pallas import tpu as pltpu

F32 = jnp.float32
BF16 = jnp.bfloat16

D_MODEL = 1024
D_PLE = 256
A_HEADS = 8
A_HEAD_DIM = 64
A_WIDTH = A_HEADS * A_HEAD_DIM
DECAY_LORA = 64
AAA_LORA = 64
GATE_LORA = 128
GN_EPS_RWKV = 64e-5
B_HEADS = 4
B_QK_DIM = 64
B_V_DIM = 128
B_QK_WIDTH = B_HEADS * B_QK_DIM
B_WIDTH = B_HEADS * B_V_DIM
ROPE_BASE = 10000.0
GN_EPS = 1e-5
SHIFT_WIDTH = 3 * A_WIDTH + DECAY_LORA + AAA_LORA + GATE_LORA
IN_WIDTH = SHIFT_WIDTH + 2 * B_QK_WIDTH + 2 * B_WIDTH
N_GROUPS = 4
EXPERTS_PER_GROUP = 4
N_EXPERTS = N_GROUPS * EXPERTS_PER_GROUP
D_EXPERT = 256
DEPTH = 1
PAST_LEN = 16384
DEEPNORM_ALPHA = (2 * DEPTH) ** 0.25
LN_EPS = 1e-5

V7X_LANES = 128
V7X_SUBLANES = 8
V7X_MXU_WIDTH = 256
V7X_VMEM_LIMIT_BYTES = 56 * 1024 * 1024

PROJ_ROWS = 1024
LOG_DECAY_OUT = 1
CHUNK_ROWS = 64
MIXER_TILES = 8
MIXER_TILES_PACKED = 2
POST_ROWS = 512
SORT_ROWS = 512
SORT_ALIGN = 16
SORT_LOCAL = 640
EXPERT_ROWS = 1024
EXPERT_ROWS_SHORT = 256
GAP_BITS = 6
RUN_BITS = 6
GROUP_LANE = 0
ROUTER_LANES = V7X_LANES
FINE_LANE0 = N_GROUPS
SORTED_WIDTH = D_MODEL + 3 * ROUTER_LANES


def _dot(a, b):
    return jnp.dot(a.astype(BF16), b.astype(BF16), preferred_element_type=F32)


def _dot_nt(a, b):
    return lax.dot_general(a.astype(BF16), b.astype(BF16), (((1,), (1,)), ((), ())), preferred_element_type=F32)


def _dot_tn(a, b):
    return lax.dot_general(a.astype(BF16), b.astype(BF16), (((0,), (0,)), ((), ())), preferred_element_type=F32)


def _split2(x):
    hi = x.astype(BF16)
    lo = (x - hi.astype(F32)).astype(BF16)
    return hi, lo


def _split3(x):
    hi = x.astype(BF16)
    r1 = x - hi.astype(F32)
    mid = r1.astype(BF16)
    lo = (r1 - mid.astype(F32)).astype(BF16)
    return hi, mid, lo


def _sigmoid(x):
    return 1.0 / (1.0 + jnp.exp(-x))


def _group_ones(group):
    assert V7X_MXU_WIDTH % group == 0
    r = lax.broadcasted_iota(jnp.int32, (V7X_MXU_WIDTH, V7X_MXU_WIDTH), 0) // group
    c = lax.broadcasted_iota(jnp.int32, (V7X_MXU_WIDTH, V7X_MXU_WIDTH), 1) // group
    return jnp.where(r == c, 1.0, 0.0).astype(BF16)


def _group_sum(x, ones):
    xb, w = x.astype(BF16), ones.shape[0]
    assert x.shape[1] % w == 0
    return jnp.concatenate([jnp.dot(xb[:, j:j + w], ones, preferred_element_type=F32)
                            for j in range(0, x.shape[1], w)], axis=1)


def _proj_kernel(carry_mode, seq_len, tiles_per_seq,
                 x_ref, xp_ref, w_ref, mu_ref, wdec_ref, dbase_ref, waaa_ref, abase_ref, wgate_ref,
                 kk_ref, ka_ref, rk_ref, cos_ref, sin_ref,
                 r_o, lw_o, k_o, v_o, al_o, be_o, g_o, bonus_o, qb_o, kb_o, vb_o, gb_o,
                 carry_scr):
    tm = x_ref.shape[0]
    if carry_mode:
        xp = jnp.broadcast_to(xp_ref[0], (V7X_SUBLANES, D_MODEL))
        xb = jnp.concatenate([x_ref[...], xp], axis=0).astype(BF16)
        j = pl.program_id(0) % tiles_per_seq

        @pl.when(pl.program_id(0) == 0)
        def _():
            carry_scr[...] = jnp.zeros_like(carry_scr)
    else:
        xb = x_ref[...].astype(BF16)
        xpb = xp_ref[...].astype(BF16)

    def project(lo, hi):
        return jnp.dot(xb, w_ref[:, lo:hi], preferred_element_type=F32)

    def shifted(p, lo, hi):
        cur = p[:tm]
        row = lax.broadcasted_iota(jnp.int32, cur.shape, 0)
        rolled = pltpu.roll(cur, 1, 0)
        if carry_mode:
            first = jnp.where(j == 0, p[tm + V7X_SUBLANES - 1:], carry_scr[V7X_SUBLANES - 1:V7X_SUBLANES, lo:hi])
            prev = jnp.where(row == 0, first, rolled)
            carry_scr[:, lo:hi] = cur[tm - V7X_SUBLANES:]
        else:
            first = jnp.dot(xpb, w_ref[:, lo:hi], preferred_element_type=F32)
            prev = jnp.where((row & (seq_len - 1)) == 0, first, rolled)
        return cur + (prev - cur) * mu_ref[:, lo:hi]

    c_r, c_k, c_v, c_l = 0, A_WIDTH, 2 * A_WIDTH, 3 * A_WIDTH
    p_lora = project(c_l, SHIFT_WIDTH)
    p_k = project(c_k, c_v)
    p_r = project(c_r, c_k)

    lora = shifted(p_lora, c_l, SHIFT_WIDTH)
    w_lo = lora[:, :DECAY_LORA]
    a_lo = lora[:, DECAY_LORA:DECAY_LORA + AAA_LORA]
    g_lo = lora[:, DECAY_LORA + AAA_LORA:]
    z = -(dbase_ref[...] + _dot(jnp.tanh(w_lo), wdec_ref[...]))
    softplus = jnp.maximum(z, 0.0) + jnp.log(1.0 + jnp.exp(-jnp.abs(z)))
    log_w = -softplus - 0.5
    lw_o[...] = -jnp.exp(log_w)
    a = _sigmoid(abase_ref[...] + _dot(a_lo, waaa_ref[...]))
    g_o[...] = (_dot(_sigmoid(g_lo), wgate_ref[...])).astype(g_o.dtype)

    p_v = project(c_v, c_l)

    ones64 = _group_ones(A_HEAD_DIM)
    k0 = shifted(p_k, c_k, c_v)
    kk0 = k0 * kk_ref[...]
    ssq = _group_sum(kk0 * kk0, ones64)
    kk = kk0 * jnp.minimum(lax.rsqrt(ssq), 1e12)
    k = k0 * (1.0 + (a - 1.0) * ka_ref[...])
    k_o[...] = (k).astype(k_o.dtype)
    al_o[...] = (-kk).astype(al_o.dtype)
    be_o[...] = (kk * a).astype(be_o.dtype)

    o = SHIFT_WIDTH
    p_qk = project(o, o + 2 * B_QK_WIDTH)[:tm]

    r = shifted(p_r, c_r, c_k)
    r_o[...] = (r).astype(r_o.dtype)
    rk_sum = _group_sum(r * k * rk_ref[...], ones64)

    p_vb = project(o + 2 * B_QK_WIDTH, o + 2 * B_QK_WIDTH + B_WIDTH)[:tm]

    v = shifted(p_v, c_v, c_l)
    v_o[...] = (v).astype(v_o.dtype)
    bonus_o[...] = (rk_sum * v).astype(bonus_o.dtype)

    p_gb = project(o + 2 * B_QK_WIDTH + B_WIDTH, IN_WIDTH)[:tm]

    q_b = p_qk[:, :B_QK_WIDTH]
    k_b = p_qk[:, B_QK_WIDTH:]
    lane = lax.broadcasted_iota(jnp.int32, (tm, B_QK_WIDTH), 1)
    first_half = (lane & (B_QK_DIM - 1)) < (B_QK_DIM // 2)
    cos = cos_ref[...]
    sin = sin_ref[...]

    def rot(t):
        swapped = jnp.where(first_half, pltpu.roll(t, B_QK_WIDTH - B_QK_DIM // 2, 1), pltpu.roll(t, B_QK_DIM // 2, 1))
        return t * cos + swapped * sin

    qb_o[...] = (rot(q_b)).astype(qb_o.dtype)
    kb_o[...] = (rot(k_b) * (B_QK_DIM ** -0.5)).astype(kb_o.dtype)

    vb_o[...] = (p_vb).astype(vb_o.dtype)
    gb_o[...] = (p_gb * _sigmoid(p_gb)).astype(gb_o.dtype)


def _proj(x2, x_prev, seq_len, pos0, W):
    n = x2.shape[0]
    tm = PROJ_ROWS
    assert n % tm == 0
    carry_mode = seq_len % tm == 0
    if carry_mode:
        tiles_per_seq = seq_len // tm
        xp = x_prev.reshape(-1, 1, D_MODEL)
        xp_spec = pl.BlockSpec((1, 1, D_MODEL), lambda i: (i // tiles_per_seq, 0, 0))
        tab_rows = seq_len
    else:
        assert tm % seq_len == 0 and seq_len & (seq_len - 1) == 0
        tiles_per_seq = 1
        xp = jnp.repeat(x_prev, seq_len, axis=0)
        xp_spec = pl.BlockSpec((tm, D_MODEL), lambda i: (i, 0))
        tab_rows = tm
    half = B_QK_DIM // 2
    inv = ROPE_BASE ** (-jnp.arange(half, dtype=F32) / half)
    pos = (pos0 + jnp.arange(seq_len, dtype=jnp.int32)).astype(F32)
    ang = pos[:, None] * inv[None, :]
    cos = jnp.tile(jnp.concatenate([jnp.cos(ang), jnp.cos(ang)], -1), (tab_rows // seq_len, B_HEADS))
    sin = jnp.tile(jnp.concatenate([-jnp.sin(ang), jnp.sin(ang)], -1), (tab_rows // seq_len, B_HEADS))
    tab_tiles = tab_rows // tm
    tab_spec = pl.BlockSpec((tm, B_QK_WIDTH), lambda i: (i % tab_tiles, 0))

    def full(a):
        return pl.BlockSpec(a.shape, lambda i: (0,) * a.ndim)

    def rows(width):
        return pl.BlockSpec((tm, width), lambda i: (i, 0))

    params = [W["w_in"], W["mu_shift"], W["w_decay_up"], W["decay_base"], W["w_aaa_up"], W["aaa_base"],
              W["w_gate_up"], W["k_k"], W["k_a"], W["r_k"]]
    widths = [A_WIDTH] * 8 + [B_QK_WIDTH, B_QK_WIDTH, B_WIDTH, B_WIDTH]
    outs = pl.pallas_call(
        functools.partial(_proj_kernel, carry_mode, seq_len, tiles_per_seq),
        grid=(n // tm,),
        in_specs=[rows(D_MODEL), xp_spec, pl.BlockSpec(params[0].shape, lambda i: (0, 0), pipeline_mode=pl.Buffered(1))]
        + [full(a) for a in params[1:]] + [tab_spec, tab_spec],
        out_specs=[rows(w) for w in widths],
        out_shape=[jax.ShapeDtypeStruct((n, w), F32 if i == LOG_DECAY_OUT else BF16) for i, w in enumerate(widths)],
        scratch_shapes=[pltpu.VMEM((V7X_SUBLANES, SHIFT_WIDTH), F32)],
        compiler_params=pltpu.CompilerParams(dimension_semantics=("arbitrary",),
                                             vmem_limit_bytes=V7X_VMEM_LIMIT_BYTES),
        name="proj",
    )(x2, xp, *params, cos, sin)
    return outs


def _mixer_kernel(nb, seqs, clen,
                  r_ref, lw_ref, k_ref, v_ref, al_ref, be_ref, qb_ref, kb_ref, vb_ref, wkv0_ref, ret0_ref,
                  ya_o, ob_o, wkv_o, ret_o, s_scr, r_scr):
    R = seqs * clen
    log2c = int(math.log2(clen))
    c_idx = pl.program_id(1)
    hd = A_HEAD_DIM
    pw2 = 2 * hd
    TP = [(t, j) for t in range(nb) for j in range(A_HEADS // 2)]
    TG = [(t, h) for t in range(nb) for h in range(B_HEADS)]

    @pl.when(c_idx == 0)
    def _():
        zero = jnp.zeros((hd, hd), F32)
        for ci, (t, j) in enumerate(TP):
            blocks = [jnp.concatenate([jnp.concatenate([wkv0_ref[t * seqs + i, 2 * j], zero], axis=1),
                                       jnp.concatenate([zero, wkv0_ref[t * seqs + i, 2 * j + 1]], axis=1)], axis=0)
                      for i in range(seqs)]
            s_scr[ci] = jnp.concatenate(blocks, axis=1) if seqs > 1 else blocks[0]
        for t, h in TG:
            r_scr[t * B_HEADS + h] = ret0_ref[t * seqs:(t + 1) * seqs, h].reshape(seqs * B_QK_DIM, B_V_DIM)

    row = lax.broadcasted_iota(jnp.int32, (R, R), 0)
    col = lax.broadcasted_iota(jnp.int32, (R, R), 1)
    same = (row >> log2c) == (col >> log2c)
    incl = same & (col <= row)
    m_incl = jnp.where(incl, 1.0, 0.0).astype(BF16)
    m_same = jnp.where(same, 1.0, 0.0).astype(BF16)

    def expand(t):
        if seqs == 1:
            return t
        w = t.shape[1]
        wide = jnp.concatenate([t] * seqs, axis=1)
        rr = lax.broadcasted_iota(jnp.int32, wide.shape, 0) >> log2c
        cc = lax.broadcasted_iota(jnp.int32, wide.shape, 1) // w
        return jnp.where(rr == cc, wide, 0.0)

    a_bar, r_bar, b_til, k_til, b_dec, k_dec, d_end, vv = [], [], [], [], [], [], [], []
    for t in range(nb):
        lw = lw_ref[t]
        parts = _split2(lw)
        c = sum(jnp.dot(m_incl, p, preferred_element_type=F32) for p in parts)
        if seqs == 1:
            is_last = lax.broadcasted_iota(jnp.int32, c.shape, 0) == R - 1
            cend = jnp.sum(jnp.where(is_last, c, 0.0), axis=0, keepdims=True)
        else:
            cend = sum(jnp.dot(m_same, p, preferred_element_type=F32) for p in parts)
        einv = jnp.exp(-c)
        edec = jnp.exp(cend - c)
        a_bar.append(al_ref[t] * jnp.exp(c - lw))
        r_bar.append(r_ref[t] * jnp.exp(c))
        b_til.append(be_ref[t] * einv)
        k_til.append(k_ref[t] * einv)
        b_dec.append(be_ref[t] * edec)
        k_dec.append(k_ref[t] * edec)
        d_end.append(jnp.exp(cend))
        vv.append(v_ref[t].astype(F32))
    last_row = (lax.broadcasted_iota(jnp.int32, (R, seqs * pw2), 0) & (clen - 1)) == clen - 1

    lane2 = lax.broadcasted_iota(jnp.int32, (R, pw2), 1)
    first = lane2 < hd
    row2 = lax.broadcasted_iota(jnp.int32, (R, pw2), 0)
    col2 = lane2 & (hd - 1)
    same2 = (row2 >> log2c) == (col2 >> log2c)
    incl2 = same2 & (col2 <= row2)
    strict2 = same2 & (col2 < row2)
    eye2 = jnp.where(row2 == col2, 1.0, 0.0).astype(F32)

    def pair(x, j):
        return x[:, j * pw2:(j + 1) * pw2]

    def keep(x, second):
        m = first if x.shape[1] == pw2 else jnp.concatenate([first] * (x.shape[1] // pw2), axis=1)
        return jnp.where(m != second, x, 0.0)

    def blockdiag(y, swapped=False):
        return jnp.concatenate([keep(y, swapped), keep(y, not swapped)], axis=0)

    ams = []
    for t, j in TP:
        lhs = jnp.concatenate([pair(a_bar[t], j), pair(r_bar[t], j)], axis=0)
        bt, kt = pair(b_til[t], j), pair(k_til[t], j)
        lhs0 = jnp.where(jnp.concatenate([first, first], axis=0), lhs, 0.0)
        lhs1 = jnp.where(jnp.concatenate([first, first], axis=0), 0.0, lhs)
        ams.append((_dot_nt(lhs0, jnp.concatenate([bt, kt], axis=0)), _dot_nt(lhs1, jnp.concatenate([kt, bt], axis=0))))

    lgs = [float(np.log1p(-np.exp2(-5.0 - h))) for h in range(B_HEADS)]
    qb = [qb_ref[t].astype(F32) for t in range(nb)]
    kb = [kb_ref[t].astype(F32) for t in range(nb)]
    qs = [qb[t][:, h * B_QK_DIM:(h + 1) * B_QK_DIM] for t, h in TG]
    khs = [kb[t][:, h * B_QK_DIM:(h + 1) * B_QK_DIM] for t, h in TG]
    vbs = [vb_ref[t][:, h * B_V_DIM:(h + 1) * B_V_DIM] for t, h in TG]
    diff = (row - col).astype(F32)
    pos_v = (lax.broadcasted_iota(jnp.int32, (R, B_V_DIM), 0) & (clen - 1)).astype(F32)
    pos_k = (lax.broadcasted_iota(jnp.int32, (R, B_QK_DIM), 0) & (clen - 1)).astype(F32)
    intra = [jnp.where(incl, jnp.exp(lg * diff), 0.0) for lg in lgs]
    cross = [jnp.exp(lg * (pos_v + 1.0)) for lg in lgs]
    kdec = [jnp.exp(lg * (clen - 1.0 - pos_k)) for lg in lgs]
    rstates = [r_scr[t * B_HEADS + h] for t, h in TG]
    scs = [_dot_nt(qs[i], khs[i]) * intra[h] for i, (t, h) in enumerate(TG)]
    qst = [_dot(expand(qs[i]), rstates[i]) * cross[h] for i, (t, h) in enumerate(TG)]

    n = len(TP)
    a_ab = [jnp.where(strict2, jnp.where(first, m0[:R], m1[:R]), 0.0) for m0, m1 in ams]
    a_ak = [jnp.where(strict2, jnp.where(first, m1[:R], m0[:R]), 0.0) for m0, m1 in ams]
    a_rb = [jnp.where(incl2, jnp.where(first, m0[R:], m1[R:]), 0.0) for m0, m1 in ams]
    a_rk = [jnp.where(incl2, jnp.where(first, m1[R:], m0[R:]), 0.0) for m0, m1 in ams]
    vps = [pair(vv[t], j) for t, j in TP]
    av = [_dot(a_ak[i], blockdiag(vps[i], swapped=True)) for i in range(n)]
    tinv = [eye2 + a for a in a_ab]
    if log2c > 1:
        pw = [_dot(a, blockdiag(a)) for a in a_ab]
    for it in range(log2c - 1):
        if it < log2c - 2:
            tp = [_dot(pw[i], blockdiag(jnp.concatenate([tinv[i], pw[i]], axis=1))) for i in range(n)]
            tinv = [tinv[i] + tp[i][:, :pw2] for i in range(n)]
            pw = [tp[i][:, pw2:] for i in range(n)]
        else:
            tinv = [tinv[i] + _dot(pw[i], blockdiag(tinv[i])) for i in range(n)]

    os_ = [_dot(scs[i], vbs[i]) + qst[i] for i in range(len(TG))]
    for i, (t, h) in enumerate(TG):
        r_scr[t * B_HEADS + h] = (rstates[i] * float(np.exp(lgs[h] * clen))
                                  + _dot_tn(expand(khs[i] * kdec[h]), vbs[i]))
    for t in range(nb):
        ob_o[t] = jnp.concatenate(os_[t * B_HEADS:(t + 1) * B_HEADS], axis=1)

    wu = [_dot(tinv[i], blockdiag(jnp.concatenate([pair(a_bar[t], j), av[i]], axis=1)))
          for i, (t, j) in enumerate(TP)]
    states = [s_scr[i] for i in range(n)]
    ws = [_dot_nt(jnp.concatenate([expand(wu[i][:, :pw2]), expand(pair(r_bar[t], j))], axis=0), states[i])
          for i, (t, j) in enumerate(TP)]
    us = [ws[i][:R] + wu[i][:, pw2:] for i in range(n)]
    ys = [ws[i][R:] + _dot(jnp.concatenate([a_rb[i], a_rk[i]], axis=1),
                           jnp.concatenate([blockdiag(us[i]), blockdiag(vps[i], swapped=True)], axis=0))
          for i in range(n)]
    npair = A_HEADS // 2
    for t in range(nb):
        ya_o[t] = jnp.concatenate(ys[t * npair:(t + 1) * npair], axis=1)
    rowp = lax.broadcasted_iota(jnp.int32, (pw2, seqs * pw2), 0) >= hd
    lanep = (lax.broadcasted_iota(jnp.int32, (pw2, seqs * pw2), 1) & (pw2 - 1)) >= hd
    for i, (t, j) in enumerate(TP):
        d_row = jnp.sum(jnp.where(last_row, expand(pair(d_end[t], j)), 0.0), axis=0, keepdims=True)
        bk = jnp.concatenate([expand(pair(b_dec[t], j)), expand(pair(k_dec[t], j))], axis=0)
        upd = _dot_tn(jnp.concatenate([us[i], vps[i]], axis=0), bk)
        s_scr[i] = states[i] * d_row + jnp.where(rowp == lanep, upd, 0.0)

    @pl.when(c_idx == pl.num_programs(1) - 1)
    def _():
        for ci, (t, j) in enumerate(TP):
            st = s_scr[ci]
            for i in range(seqs):
                wkv_o[t * seqs + i, 2 * j] = st[:hd, i * pw2:i * pw2 + hd]
                wkv_o[t * seqs + i, 2 * j + 1] = st[hd:, i * pw2 + hd:(i + 1) * pw2]
        for t, h in TG:
            ret_o[t * seqs:(t + 1) * seqs, h] = r_scr[t * B_HEADS + h].reshape(seqs, B_QK_DIM, B_V_DIM)


def _mixer(ops, wkv0, ret0, n_seq, seq_len):
    n = ops[0].shape[0]
    R = CHUNK_ROWS
    if seq_len >= R:
        assert seq_len % R == 0
        seqs, clen, nchunks = 1, R, seq_len // R
    else:
        assert R % seq_len == 0 and seq_len & (seq_len - 1) == 0 and n_seq % (R // seq_len) == 0
        seqs, clen, nchunks = R // seq_len, seq_len, 1
    ntiles = n_seq // seqs
    nb = MIXER_TILES if seqs == 1 else MIXER_TILES_PACKED
    assert ntiles % nb == 0
    ops3 = [a.reshape(ntiles, nchunks * R, a.shape[1]) for a in ops]

    def rows(width):
        return pl.BlockSpec((nb, R, width), lambda i, c: (i, c, 0))

    wkv_spec = pl.BlockSpec((nb * seqs, A_HEADS, A_HEAD_DIM, A_HEAD_DIM), lambda i, c: (i, 0, 0, 0))
    ret_spec = pl.BlockSpec((nb * seqs, B_HEADS, B_QK_DIM, B_V_DIM), lambda i, c: (i, 0, 0, 0))
    ya, ob, wkv1, ret1 = pl.pallas_call(
        functools.partial(_mixer_kernel, nb, seqs, clen),
        grid=(ntiles // nb, nchunks),
        in_specs=[rows(A_WIDTH)] * 6 + [rows(B_QK_WIDTH), rows(B_QK_WIDTH), rows(B_WIDTH), wkv_spec, ret_spec],
        out_specs=[rows(A_WIDTH), rows(B_WIDTH), wkv_spec, ret_spec],
        out_shape=[jax.ShapeDtypeStruct((ntiles, nchunks * R, A_WIDTH), F32),
                   jax.ShapeDtypeStruct((ntiles, nchunks * R, B_WIDTH), F32),
                   jax.ShapeDtypeStruct(wkv0.shape, F32), jax.ShapeDtypeStruct(ret0.shape, F32)],
        scratch_shapes=[pltpu.VMEM((nb * A_HEADS // 2, 2 * A_HEAD_DIM, seqs * 2 * A_HEAD_DIM), F32),
                        pltpu.VMEM((nb * B_HEADS, seqs * B_QK_DIM, B_V_DIM), F32)],
        compiler_params=pltpu.CompilerParams(dimension_semantics=("parallel", "arbitrary"),
                                             vmem_limit_bytes=V7X_VMEM_LIMIT_BYTES),
        name="mixer",
    )(*ops3, wkv0, ret0)
    return ya.reshape(n, A_WIDTH), ob.reshape(n, B_WIDTH), wkv1, ret1


def _layer_norm(z, g, b):
    mu = jnp.mean(z, axis=-1, keepdims=True)
    d = z - mu
    var = jnp.mean(d * d, axis=-1, keepdims=True)
    return d * lax.rsqrt(var + LN_EPS) * g + b


def _post_tile(ya_ref, ob_ref, bonus_ref, g_ref, gb_ref, x_ref, lnxg_ref, lnxb_ref, rgg_ref, rgb_ref,
               wout_ref, ln1g_ref, ln1b_ref, wr_ref, br_ref, h_o, gate_o, cnt_o):
    tm = x_ref.shape[0]

    def head_norm(t, group, eps, gg, bb):
        ones = _group_ones(group)
        mu = _group_sum(t, ones) * (1.0 / group)
        d = t - mu
        var = _group_sum(d * d, ones) * (1.0 / group)
        return d * lax.rsqrt(var + eps) * gg + bb

    y_a = (head_norm(ya_ref[...], A_HEAD_DIM, GN_EPS_RWKV, lnxg_ref[...], lnxb_ref[...]) + bonus_ref[...]) * g_ref[...]
    y_b = head_norm(ob_ref[...], B_V_DIM, GN_EPS, rgg_ref[...], rgb_ref[...]) * gb_ref[...]
    y = jnp.concatenate([y_a, y_b], axis=1)
    mix = _dot(y, wout_ref[...])
    h = _layer_norm(DEEPNORM_ALPHA * x_ref[...] + mix, ln1g_ref[...], ln1b_ref[...])
    h_o[...] = h

    h_hi, h_lo = _split2(h)
    w_hi, w_lo = _split2(wr_ref[...])
    logits = (jnp.dot(h_hi, w_hi, preferred_element_type=F32) + jnp.dot(h_hi, w_lo, preferred_element_type=F32)
              + jnp.dot(h_lo, w_hi, preferred_element_type=F32)) + br_ref[...]
    lane = lax.broadcasted_iota(jnp.int32, (tm, ROUTER_LANES), 1).astype(F32)
    neg = -jnp.inf
    big = float(ROUTER_LANES)
    cl = jnp.where(lane < N_GROUPS, logits, neg)
    cmax = jnp.max(cl, axis=-1, keepdims=True)
    grp = jnp.min(jnp.where(cl == cmax, lane, big), axis=-1, keepdims=True)
    gprob = 1.0 / jnp.sum(jnp.exp(cl - cmax), axis=-1, keepdims=True)
    lo_lane = FINE_LANE0 + grp * EXPERTS_PER_GROUP
    fv = jnp.where((lane >= lo_lane) & (lane < lo_lane + EXPERTS_PER_GROUP), logits, neg)
    m1 = jnp.max(fv, axis=-1, keepdims=True)
    i1 = jnp.min(jnp.where(fv == m1, lane, big), axis=-1, keepdims=True)
    fv2 = jnp.where(lane == i1, neg, fv)
    m2 = jnp.max(fv2, axis=-1, keepdims=True)
    i2 = jnp.min(jnp.where(fv2 == m2, lane, big), axis=-1, keepdims=True)
    e2 = jnp.exp(m2 - m1)
    w1 = gprob / (1.0 + e2)
    w2 = gprob * e2 / (1.0 + e2)
    gate_o[...] = (jnp.where(lane == i1, w1, 0.0) + jnp.where(lane == i2, w2, 0.0)
                   + jnp.where(lane == GROUP_LANE, grp, 0.0))
    onehot = jnp.where((lane == grp) & (lane < N_GROUPS), 1.0, 0.0)
    cnt_o[0] = jnp.broadcast_to(jnp.sum(onehot, axis=0, keepdims=True), (V7X_SUBLANES, ROUTER_LANES)).astype(jnp.int32)


POST_OPERANDS = 6


def _post_kernel(group_tiles, *refs):
    per_group, shared = refs[:POST_OPERANDS * len(group_tiles)], refs[POST_OPERANDS * len(group_tiles):]
    i = pl.program_id(0)
    start = 0
    for k, tiles in enumerate(group_tiles):
        @pl.when((i >= start) & (i < start + tiles))
        def _(k=k):
            _post_tile(*per_group[POST_OPERANDS * k:POST_OPERANDS * (k + 1)], *shared)

        start += tiles


def _group_rows(tm, width, start, tiles):
    return pl.BlockSpec((tm, width), lambda i, *_: (jnp.clip(i - start, 0, tiles - 1), 0))


def _post(groups, W):
    tm = POST_ROWS
    group_tiles = tuple(g[-1].shape[0] // tm for g in groups)
    assert all(g[-1].shape[0] % tm == 0 for g in groups)
    n = tm * sum(group_tiles)

    def full(a):
        return pl.BlockSpec(a.shape, lambda i: (0,) * a.ndim)

    def rows(width):
        return pl.BlockSpec((tm, width), lambda i: (i, 0))

    widths = [A_WIDTH, B_WIDTH, A_WIDTH, A_WIDTH, B_WIDTH, D_MODEL]
    in_specs, start = [], 0
    for tiles in group_tiles:
        in_specs += [_group_rows(tm, w, start, tiles) for w in widths]
        start += tiles
    params = [W["lnx_g"], W["lnx_b"], W["ret_gn_g"], W["ret_gn_b"], W["w_out"], W["ln1_g"], W["ln1_b"],
              W["w_router"], W["b_router"]]
    return pl.pallas_call(
        functools.partial(_post_kernel, group_tiles),
        grid=(n // tm,),
        in_specs=in_specs + [full(a) for a in params],
        out_specs=[rows(D_MODEL), rows(ROUTER_LANES),
                   pl.BlockSpec((1, V7X_SUBLANES, ROUTER_LANES), lambda i: (i, 0, 0))],
        out_shape=[jax.ShapeDtypeStruct((n, D_MODEL), F32), jax.ShapeDtypeStruct((n, ROUTER_LANES), F32),
                   jax.ShapeDtypeStruct((n // tm, V7X_SUBLANES, ROUTER_LANES), jnp.int32)],
        compiler_params=pltpu.CompilerParams(dimension_semantics=("parallel",),
                                             vmem_limit_bytes=V7X_VMEM_LIMIT_BYTES),
        name="post",
    )(*[a for g in groups for a in g], *params)


def _sort_positions(gate, lofs):
    tm = gate.shape[0]
    lane = lax.broadcasted_iota(jnp.int32, (tm, ROUTER_LANES), 1)
    grp = gate[:, GROUP_LANE:GROUP_LANE + 1].astype(jnp.int32)
    onehot = jnp.where((lane == grp) & (lane < N_GROUPS), 1.0, 0.0)
    r = lax.broadcasted_iota(jnp.int32, (tm, tm), 0)
    c = lax.broadcasted_iota(jnp.int32, (tm, tm), 1)
    earlier = jnp.where(c < r, 1.0, 0.0).astype(BF16)
    prefix = jnp.dot(earlier, onehot.astype(BF16), preferred_element_type=F32)
    base = jnp.zeros((tm, ROUTER_LANES), F32)
    for g in range(N_GROUPS):
        base = jnp.where(lane == g, lofs[g].astype(F32), base)
    return jnp.sum(onehot * (base + prefix), axis=1, keepdims=True).astype(jnp.int32)


def _piece_copies(action, rows, bits, copy_of):
    k = rows // SORT_ALIGN
    for b in reversed(range(bits)):
        size = SORT_ALIGN << b

        @pl.when(((k >> b) & 1) == 1)
        def _():
            done = ((k >> (b + 1)) << (b + 1)) * SORT_ALIGN
            cp = copy_of(done, size)
            cp.start() if action == "start" else cp.wait()


def _run_copies(action, plan_ref, i, src_of, dst_of, sem):
    for g in range(N_GROUPS):
        goff = plan_ref[i, g]
        lofs = plan_ref[i, 2 * N_GROUPS + g]

        def copy_of(done, size, goff=goff, lofs=lofs):
            lo = pl.multiple_of(lofs + done, SORT_ALIGN)
            go = pl.multiple_of(goff + done, SORT_ALIGN)
            return pltpu.make_async_copy(src_of(lo, go, size), dst_of(lo, go, size), sem)

        _piece_copies(action, plan_ref[i, N_GROUPS + g], RUN_BITS, copy_of)


def _dispatch_kernel(plan_ref, gap_ref, h_ref, gate_ref, hs_o, h_loc, sem):
    i = pl.program_id(0)
    tm = h_ref.shape[0]
    lofs = [plan_ref[i, 2 * N_GROUPS + g] for g in range(N_GROUPS)]
    pos = _sort_positions(gate_ref[...], lofs)
    onehot_t = jnp.where(lax.broadcasted_iota(jnp.int32, (tm, SORT_LOCAL), 1) == pos, 1.0, 0.0).astype(BF16)
    idx = lax.broadcasted_iota(jnp.int32, (V7X_SUBLANES, SORT_LOCAL), 1)
    pos_row = (_dot_nt((idx // V7X_LANES).astype(F32), onehot_t) * float(V7X_LANES)
               + _dot_nt((idx % V7X_LANES).astype(F32), onehot_t))[0:1].astype(jnp.int32)
    perm = jnp.where(lax.broadcasted_iota(jnp.int32, (SORT_LOCAL, tm), 0) == pos_row, 1.0, 0.0).astype(BF16)
    payload = jnp.concatenate([h_ref[...].astype(BF16)] + list(_split3(gate_ref[...])), axis=1)
    sorted_rows = jnp.dot(perm, payload, preferred_element_type=F32).astype(BF16)

    def copies(action, tile):
        _run_copies(action, plan_ref, tile, lambda lo, go, sz: h_loc.at[pl.ds(lo, sz)],
                    lambda lo, go, sz: hs_o.at[pl.ds(go, sz)], sem)

    @pl.when(i > 0)
    def _():
        copies("wait", i - 1)

    h_loc[...] = sorted_rows
    copies("start", i)

    @pl.when(i == pl.num_programs(0) - 1)
    def _():
        copies("wait", i)
        h_loc[...] = jnp.zeros_like(h_loc)
        for action in ("start", "wait"):
            for g in range(N_GROUPS):
                def copy_of(done, size, g=g):
                    go = pl.multiple_of(gap_ref[g] + done, SORT_ALIGN)
                    return pltpu.make_async_copy(h_loc.at[pl.ds(0, size)], hs_o.at[pl.ds(go, size)], sem)

                _piece_copies(action, gap_ref[N_GROUPS + g], GAP_BITS, copy_of)

            def body(k, carry):
                go = pl.multiple_of(gap_ref[2 * N_GROUPS] + k * EXPERT_ROWS_SHORT, SORT_ALIGN)
                cp = pltpu.make_async_copy(h_loc.at[pl.ds(0, EXPERT_ROWS_SHORT)], hs_o.at[pl.ds(go, EXPERT_ROWS_SHORT)], sem)
                cp.start() if action == "start" else cp.wait()
                return carry

            lax.fori_loop(0, gap_ref[2 * N_GROUPS + 1] // EXPERT_ROWS_SHORT, body, 0)


def _experts_kernel(tile_group_ref, n_valid_ref, hs_ref, w1_ref, w3_ref, w2_ref, ys_o, w1_b, w3_b, w2_b):
    j = pl.program_id(0)
    last = n_valid_ref[0] - 1
    g = tile_group_ref[jnp.minimum(j, last)]
    g_prev = tile_group_ref[jnp.minimum(jnp.maximum(j - 1, 0), last)]

    @pl.when((j == 0) | (g != g_prev))
    def _():
        w1_b[...] = w1_ref[...].astype(BF16)
        w3_b[...] = w3_ref[...].astype(BF16)
        w2_b[...] = w2_ref[...].astype(BF16)

    @pl.when(j <= last)
    def _():
        x = hs_ref[:, :D_MODEL]
        gs = sum(hs_ref[:, D_MODEL + t * ROUTER_LANES:D_MODEL + (t + 1) * ROUTER_LANES].astype(F32) for t in range(3))
        lane = lax.broadcasted_iota(jnp.int32, gs.shape, 1)
        acc = jnp.zeros(ys_o.shape, F32)
        for e in range(EXPERTS_PER_GROUP):
            ge = jnp.sum(jnp.where(lane == FINE_LANE0 + g * EXPERTS_PER_GROUP + e, gs, 0.0), axis=-1, keepdims=True)
            a = jnp.dot(x, w1_b[e], preferred_element_type=F32)
            b = jnp.dot(x, w3_b[e], preferred_element_type=F32)
            hid = (a * _sigmoid(a)) * b * ge
            acc = acc + jnp.dot(hid.astype(BF16), w2_b[e], preferred_element_type=F32)
        ys_o[...] = acc.astype(ys_o.dtype)

    @pl.when(j > last)
    def _():
        ys_o[...] = jnp.zeros_like(ys_o)


def _combine_kernel(group_tiles, plan_ref, h_ref, gate_ref, *refs):
    ng = len(group_tiles)
    p_refs, (ys_ref, ln2g_ref, ln2b_ref, wple_ref, wpg_ref, pleg_ref) = refs[:ng], refs[ng:ng + 6]
    o_refs, (y_loc, sem) = refs[ng + 6:2 * ng + 6], refs[2 * ng + 6:]
    i = pl.program_id(0)
    starts = [sum(group_tiles[:k]) for k in range(ng)]

    def in_group(k):
        return (i >= starts[k]) & (i < starts[k] + group_tiles[k])

    tm = h_ref.shape[0]
    lofs = [plan_ref[i, 2 * N_GROUPS + g] for g in range(N_GROUPS)]

    def fetch(tile):
        slot = tile % 2
        y_loc[slot] = jnp.zeros(y_loc.shape[1:], y_loc.dtype)
        _run_copies("start", plan_ref, tile, lambda lo, go, sz: ys_ref.at[pl.ds(go, sz)],
                    lambda lo, go, sz: y_loc.at[slot, pl.ds(lo, sz)], sem.at[slot])

    @pl.when(i == 0)
    def _():
        fetch(i)

    @pl.when(i + 1 < pl.num_programs(0))
    def _():
        fetch(i + 1)

    pos = _sort_positions(gate_ref[...], lofs)
    onehot_t = jnp.where(lax.broadcasted_iota(jnp.int32, (tm, SORT_LOCAL), 1) == pos, 1.0, 0.0).astype(BF16)
    p_tile = p_refs[0][...]
    for k in range(1, ng):
        p_tile = jnp.where(in_group(k), p_refs[k][...], p_tile)
    ple_in = _dot(p_tile, wple_ref[...])
    slot = i % 2
    _run_copies("wait", plan_ref, i, lambda lo, go, sz: ys_ref.at[pl.ds(go, sz)],
                lambda lo, go, sz: y_loc.at[slot, pl.ds(lo, sz)], sem.at[slot])
    ffn = jnp.dot(onehot_t, y_loc[slot], preferred_element_type=F32)
    h2 = _layer_norm(DEEPNORM_ALPHA * h_ref[...] + ffn, ln2g_ref[...], ln2b_ref[...])
    ple = ple_in * _sigmoid(_dot(h2, wpg_ref[...]))
    ms = jnp.mean(ple * ple, axis=-1, keepdims=True)
    out = h2 + ple * lax.rsqrt(ms + LN_EPS) * pleg_ref[...]
    for k in range(ng):
        @pl.when(in_group(k))
        def _(k=k):
            o_refs[k][...] = out


def _ffn(h, gate, counts, p_groups, W):
    n = h.shape[0]
    tm = SORT_ROWS
    assert n % tm == 0 and tm == POST_ROWS and SORT_LOCAL >= tm + N_GROUPS * SORT_ALIGN
    ntiles = n // tm
    er = EXPERT_ROWS if n >= 2 * N_GROUPS * EXPERT_ROWS else EXPERT_ROWS_SHORT
    assert er <= SORT_ALIGN << GAP_BITS and SORT_ALIGN << (GAP_BITS - 1) <= SORT_LOCAL
    cnt = counts[:, 0, :N_GROUPS]
    run = (cnt + SORT_ALIGN - 1) // SORT_ALIGN * SORT_ALIGN
    lofs = jnp.cumsum(run, axis=1) - run
    seg = (jnp.sum(run, axis=0) + er - 1) // er * er
    gbase = jnp.cumsum(seg) - seg
    goff = gbase[None, :] + jnp.cumsum(run, axis=0) - run
    plan = jnp.concatenate([goff, run, lofs], axis=1).astype(jnp.int32)
    max_tiles = (n + ntiles * N_GROUPS * (SORT_ALIGN - 1)) // er + N_GROUPS
    cap = max_tiles * er
    n_valid = (jnp.sum(seg) // er).astype(jnp.int32).reshape(1)
    tile_start = jnp.arange(max_tiles, dtype=jnp.int32) * er
    tile_group = jnp.clip(jnp.sum(tile_start[:, None] >= (gbase + seg)[None, :], axis=1), 0, N_GROUPS - 1).astype(jnp.int32)

    cparams = dict(vmem_limit_bytes=V7X_VMEM_LIMIT_BYTES)
    any_spec = pl.BlockSpec(memory_space=pl.ANY)
    total = jnp.sum(run, axis=0)
    assert er % EXPERT_ROWS_SHORT == 0 and EXPERT_ROWS_SHORT <= SORT_LOCAL
    used = jnp.sum(seg)
    gaps = jnp.concatenate([gbase + total, seg - total, jnp.stack([used, cap - used])]).astype(jnp.int32)
    hs = pl.pallas_call(
        _dispatch_kernel,
        grid_spec=pltpu.PrefetchScalarGridSpec(
            num_scalar_prefetch=2, grid=(ntiles,),
            in_specs=[pl.BlockSpec((tm, D_MODEL), lambda i, plan, gaps: (i, 0)),
                      pl.BlockSpec((tm, ROUTER_LANES), lambda i, plan, gaps: (i, 0))],
            out_specs=any_spec,
            scratch_shapes=[pltpu.VMEM((SORT_LOCAL, SORTED_WIDTH), BF16), pltpu.SemaphoreType.DMA(())]),
        out_shape=jax.ShapeDtypeStruct((cap, SORTED_WIDTH), BF16),
        compiler_params=pltpu.CompilerParams(dimension_semantics=("arbitrary",), **cparams),
        name="dispatch",
    )(plan, gaps, h, gate)

    def tile_rows(width):
        return pl.BlockSpec((er, width), lambda j, tg, nv: (jnp.minimum(j, nv[0] - 1), 0))

    def group_w(shape):
        return pl.BlockSpec((EXPERTS_PER_GROUP,) + shape, lambda j, tg, nv: (tg[jnp.minimum(j, nv[0] - 1)], 0, 0))

    ys = pl.pallas_call(
        _experts_kernel,
        grid_spec=pltpu.PrefetchScalarGridSpec(
            num_scalar_prefetch=2, grid=(max_tiles,),
            in_specs=[tile_rows(SORTED_WIDTH), group_w((D_MODEL, D_EXPERT)),
                      group_w((D_MODEL, D_EXPERT)), group_w((D_EXPERT, D_MODEL))],
            out_specs=pl.BlockSpec((er, D_MODEL), lambda j, tg, nv: (j, 0)),
            scratch_shapes=[pltpu.VMEM((EXPERTS_PER_GROUP, D_MODEL, D_EXPERT), BF16),
                            pltpu.VMEM((EXPERTS_PER_GROUP, D_MODEL, D_EXPERT), BF16),
                            pltpu.VMEM((EXPERTS_PER_GROUP, D_EXPERT, D_MODEL), BF16)]),
        out_shape=jax.ShapeDtypeStruct((cap, D_MODEL), BF16),
        compiler_params=pltpu.CompilerParams(dimension_semantics=("arbitrary",), **cparams),
        name="experts",
    )(tile_group, n_valid, hs, W["expert_w1"], W["expert_w3"], W["expert_w2"])

    def full(a):
        return pl.BlockSpec(a.shape, lambda i, plan: (0,) * a.ndim)

    params = [W["ln2_g"], W["ln2_b"], W["w_ple"], W["w_ple_gate"], W["ple_norm_g"]]
    group_tiles = tuple(pg.shape[0] // tm for pg in p_groups)
    assert sum(group_tiles) == ntiles
    starts = [sum(group_tiles[:k]) for k in range(len(group_tiles))]
    return pl.pallas_call(
        functools.partial(_combine_kernel, group_tiles),
        grid_spec=pltpu.PrefetchScalarGridSpec(
            num_scalar_prefetch=1, grid=(ntiles,),
            in_specs=[pl.BlockSpec((tm, D_MODEL), lambda i, plan: (i, 0)),
                      pl.BlockSpec((tm, ROUTER_LANES), lambda i, plan: (i, 0))]
            + [_group_rows(tm, D_PLE, st, t) for st, t in zip(starts, group_tiles)]
            + [any_spec] + [full(a) for a in params],
            out_specs=[_group_rows(tm, D_MODEL, st, t) for st, t in zip(starts, group_tiles)],
            scratch_shapes=[pltpu.VMEM((2, SORT_LOCAL, D_MODEL), BF16), pltpu.SemaphoreType.DMA((2,))]),
        out_shape=[jax.ShapeDtypeStruct((t * tm, D_MODEL), F32) for t in group_tiles],
        compiler_params=pltpu.CompilerParams(dimension_semantics=("arbitrary",), **cparams),
        name="combine",
    )(plan, h, gate, *p_groups, ys, *params)


def _mix(x, x_prev, wkv0, ret0, pos0, W):
    n_seq, seq_len, _ = x.shape
    x2 = x.reshape(n_seq * seq_len, D_MODEL)
    r, lw, k, v, al, be, g, bonus, qb, kb, vb, gb = _proj(x2, x_prev, seq_len, pos0, W)
    ya, ob, wkv1, ret1 = _mixer((r, lw, k, v, al, be, qb, kb, vb), wkv0, ret0, n_seq, seq_len)
    return (ya, ob, bonus, g, gb, x2), wkv1, ret1


def _prep_weights(i, w_in, mu_shift, w_decay_up, decay_base, w_aaa_up, aaa_base, w_gate_up, k_k, k_a, r_k,
                  lnx_g, lnx_b, ret_gn_g, ret_gn_b, w_out, ln1_g, ln1_b,
                  router_coarse_w, router_coarse_b, router_fine_w, router_fine_b,
                  expert_w1, expert_w3, expert_w2, ln2_g, ln2_b, w_ple, w_ple_gate, ple_norm_g):
    def row(a):
        return a[i].reshape(1, -1).astype(F32)

    pad = ROUTER_LANES - N_GROUPS - N_EXPERTS
    w_router = jnp.concatenate([router_coarse_w[i], router_fine_w[i], jnp.zeros((D_MODEL, pad), F32)], axis=1)
    b_router = jnp.concatenate([router_coarse_b[i], router_fine_b[i], jnp.zeros((pad,), F32)]).reshape(1, -1)
    return {
        "w_in": w_in[i].astype(BF16), "mu_shift": row(mu_shift), "w_decay_up": w_decay_up[i].astype(BF16),
        "decay_base": row(decay_base), "w_aaa_up": w_aaa_up[i].astype(BF16), "aaa_base": row(aaa_base),
        "w_gate_up": w_gate_up[i].astype(BF16), "k_k": row(k_k), "k_a": row(k_a), "r_k": row(r_k),
        "lnx_g": row(lnx_g), "lnx_b": row(lnx_b), "ret_gn_g": row(ret_gn_g), "ret_gn_b": row(ret_gn_b),
        "w_out": w_out[i].astype(BF16), "ln1_g": row(ln1_g), "ln1_b": row(ln1_b),
        "w_router": w_router, "b_router": b_router,
        "expert_w1": expert_w1[i], "expert_w3": expert_w3[i], "expert_w2": expert_w2[i], "ln2_g": row(ln2_g), "ln2_b": row(ln2_b),
        "w_ple": w_ple[i].astype(BF16), "w_ple_gate": w_ple_gate[i].astype(BF16), "ple_norm_g": row(ple_norm_g),
    }


def kernel(x_prompt, x_sample, p_prompt, p_sample, state_wkv, state_shift, state_ret, w_in, mu_shift, w_decay_up, decay_base, w_aaa_up, aaa_base, w_gate_up, k_k, k_a, r_k, lnx_g, lnx_b, ret_gn_g, ret_gn_b, w_out, ln1_g, ln1_b, router_coarse_w, router_coarse_b, router_fine_w, router_fine_b, expert_w1, expert_w3, expert_w2, ln2_g, ln2_b, w_ple, w_ple_gate, ple_norm_g):
    yp, ys = x_prompt, x_sample
    nb = x_prompt.shape[0]
    depth = w_in.shape[0]
    wkv_p, shift_p, ret_p, wkv_s, shift_s, ret_s = [], [], [], [], [], []
    for i in range(depth):
        W = _prep_weights(i, w_in, mu_shift, w_decay_up, decay_base, w_aaa_up, aaa_base, w_gate_up, k_k, k_a, r_k,
                          lnx_g, lnx_b, ret_gn_g, ret_gn_b, w_out, ln1_g, ln1_b,
                          router_coarse_w, router_coarse_b, router_fine_w, router_fine_b,
                          expert_w1, expert_w3, expert_w2, ln2_g, ln2_b, w_ple, w_ple_gate, ple_norm_g)
        ops_p, wp, rp = _mix(yp, jnp.zeros((nb, D_MODEL), F32), jnp.zeros((nb, A_HEADS, A_HEAD_DIM, A_HEAD_DIM), F32),
                             jnp.zeros((nb, B_HEADS, B_QK_DIM, B_V_DIM), F32), 0, W)
        ops_s, wsm, rsm = _mix(ys, state_shift[i], state_wkv[i], state_ret[i], PAST_LEN, W)
        sp, ss = yp[:, -1], ys[:, -1]
        h, gate, counts = _post([ops_p, ops_s], W)
        out_p, out_s = _ffn(h, gate, counts, [p_prompt[i].reshape(-1, D_PLE), p_sample[i].reshape(-1, D_PLE)], W)
        yp, ys = out_p.reshape(yp.shape), out_s.reshape(ys.shape)
        wkv_p.append(wp); shift_p.append(sp); ret_p.append(rp)
        wkv_s.append(wsm); shift_s.append(ss); ret_s.append(rsm)
    return (yp, ys, jnp.stack(wkv_p, 0), jnp.stack(shift_p, 0), jnp.stack(ret_p, 0),
            jnp.stack(wkv_s, 0), jnp.stack(shift_s, 0), jnp.stack(ret_s, 0))
```

```python
import functools
import math

import numpy as np
import jax
import jax.numpy as jnp
from jax import lax
from jax.experimental import pallas as pl
from jax.experimental.pallas import tpu as pltpu

F32 = jnp.float32
BF16 = jnp.bfloat16

D_MODEL = 1024
D_PLE = 256
A_HEADS = 8
A_HEAD_DIM = 64
A_WIDTH = A_HEADS * A_HEAD_DIM
DECAY_LORA = 64
AAA_LORA = 64
GATE_LORA = 128
GN_EPS_RWKV = 64e-5
B_HEADS = 4
B_QK_DIM = 64
B_V_DIM = 128
B_QK_WIDTH = B_HEADS * B_QK_DIM
B_WIDTH = B_HEADS * B_V_DIM
ROPE_BASE = 10000.0
GN_EPS = 1e-5
SHIFT_WIDTH = 3 * A_WIDTH + DECAY_LORA + AAA_LORA + GATE_LORA
IN_WIDTH = SHIFT_WIDTH + 2 * B_QK_WIDTH + 2 * B_WIDTH
N_GROUPS = 4
EXPERTS_PER_GROUP = 4
N_EXPERTS = N_GROUPS * EXPERTS_PER_GROUP
D_EXPERT = 256
DEPTH = 1
PAST_LEN = 16384
DEEPNORM_ALPHA = (2 * DEPTH) ** 0.25
LN_EPS = 1e-5

V7X_LANES = 128
V7X_SUBLANES = 8
V7X_MXU_WIDTH = 256
V7X_VMEM_LIMIT_BYTES = 56 * 1024 * 1024

PROJ_ROWS = 1024
LOG_DECAY_OUT = 1
CHUNK_ROWS = 64
MIXER_TILES = 8
MIXER_TILES_PACKED = 2
POST_ROWS = 512
SORT_ROWS = 512
SORT_ALIGN = 16
SORT_LOCAL = 640
EXPERT_ROWS = 1024
EXPERT_ROWS_SHORT = 256
GAP_BITS = 6
RUN_BITS = 6
GROUP_LANE = 0
ROUTER_LANES = V7X_LANES
FINE_LANE0 = N_GROUPS
SORTED_WIDTH = D_MODEL + 3 * ROUTER_LANES


def _dot(a, b):
    return jnp.dot(a.astype(BF16), b.astype(BF16), preferred_element_type=F32)


def _dot_nt(a, b):
    return lax.dot_general(a.astype(BF16), b.astype(BF16), (((1,), (1,)), ((), ())), preferred_element_type=F32)


def _dot_tn(a, b):
    return lax.dot_general(a.astype(BF16), b.astype(BF16), (((0,), (0,)), ((), ())), preferred_element_type=F32)


def _split2(x):
    hi = x.astype(BF16)
    lo = (x - hi.astype(F32)).astype(BF16)
    return hi, lo


def _split3(x):
    hi = x.astype(BF16)
    r1 = x - hi.astype(F32)
    mid = r1.astype(BF16)
    lo = (r1 - mid.astype(F32)).astype(BF16)
    return hi, mid, lo


def _sigmoid(x):
    return 1.0 / (1.0 + jnp.exp(-x))


def _group_ones(group):
    assert V7X_MXU_WIDTH % group == 0
    r = lax.broadcasted_iota(jnp.int32, (V7X_MXU_WIDTH, V7X_MXU_WIDTH), 0) // group
    c = lax.broadcasted_iota(jnp.int32, (V7X_MXU_WIDTH, V7X_MXU_WIDTH), 1) // group
    return jnp.where(r == c, 1.0, 0.0).astype(BF16)


def _group_sum(x, ones):
    xb, w = x.astype(BF16), ones.shape[0]
    assert x.shape[1] % w == 0
    return jnp.concatenate([jnp.dot(xb[:, j:j + w], ones, preferred_element_type=F32)
                            for j in range(0, x.shape[1], w)], axis=1)


def _proj_kernel(carry_mode, seq_len, tiles_per_seq,
                 x_ref, xp_ref, w_ref, mu_ref, wdec_ref, dbase_ref, waaa_ref, abase_ref, wgate_ref,
                 kk_ref, ka_ref, rk_ref, cos_ref, sin_ref,
                 r_o, lw_o, k_o, v_o, al_o, be_o, g_o, bonus_o, qb_o, kb_o, vb_o, gb_o,
                 carry_scr):
    tm = x_ref.shape[0]
    if carry_mode:
        xp = jnp.broadcast_to(xp_ref[0], (V7X_SUBLANES, D_MODEL))
        xb = jnp.concatenate([x_ref[...], xp], axis=0).astype(BF16)
        j = pl.program_id(0) % tiles_per_seq

        @pl.when(pl.program_id(0) == 0)
        def _():
            carry_scr[...] = jnp.zeros_like(carry_scr)
    else:
        xb = x_ref[...].astype(BF16)
        xpb = xp_ref[...].astype(BF16)

    def project(lo, hi):
        return jnp.dot(xb, w_ref[:, lo:hi], preferred_element_type=F32)

    def shifted(p, lo, hi):
        cur = p[:tm]
        row = lax.broadcasted_iota(jnp.int32, cur.shape, 0)
        rolled = pltpu.roll(cur, 1, 0)
        if carry_mode:
            first = jnp.where(j == 0, p[tm + V7X_SUBLANES - 1:], carry_scr[V7X_SUBLANES - 1:V7X_SUBLANES, lo:hi])
            prev = jnp.where(row == 0, first, rolled)
            carry_scr[:, lo:hi] = cur[tm - V7X_SUBLANES:]
        else:
            first = jnp.dot(xpb, w_ref[:, lo:hi], preferred_element_type=F32)
            prev = jnp.where((row & (seq_len - 1)) == 0, first, rolled)
        return cur + (prev - cur) * mu_ref[:, lo:hi]

    c_r, c_k, c_v, c_l = 0, A_WIDTH, 2 * A_WIDTH, 3 * A_WIDTH
    p_lora = project(c_l, SHIFT_WIDTH)
    p_k = project(c_k, c_v)
    p_r = project(c_r, c_k)

    lora = shifted(p_lora, c_l, SHIFT_WIDTH)
    w_lo = lora[:, :DECAY_LORA]
    a_lo = lora[:, DECAY_LORA:DECAY_LORA + AAA_LORA]
    g_lo = lora[:, DECAY_LORA + AAA_LORA:]
    z = -(dbase_ref[...] + _dot(jnp.tanh(w_lo), wdec_ref[...]))
    softplus = jnp.maximum(z, 0.0) + jnp.log(1.0 + jnp.exp(-jnp.abs(z)))
    log_w = -softplus - 0.5
    lw_o[...] = -jnp.exp(log_w)
    a = _sigmoid(abase_ref[...] + _dot(a_lo, waaa_ref[...]))
    g_o[...] = (_dot(_sigmoid(g_lo), wgate_ref[...])).astype(g_o.dtype)

    p_v = project(c_v, c_l)

    ones64 = _group_ones(A_HEAD_DIM)
    k0 = shifted(p_k, c_k, c_v)
    kk0 = k0 * kk_ref[...]
    ssq = _group_sum(kk0 * kk0, ones64)
    kk = kk0 * jnp.minimum(lax.rsqrt(ssq), 1e12)
    k = k0 * (1.0 + (a - 1.0) * ka_ref[...])
    k_o[...] = (k).astype(k_o.dtype)
    al_o[...] = (-kk).astype(al_o.dtype)
    be_o[...] = (kk * a).astype(be_o.dtype)

    o = SHIFT_WIDTH
    p_qk = project(o, o + 2 * B_QK_WIDTH)[:tm]

    r = shifted(p_r, c_r, c_k)
    r_o[...] = (r).astype(r_o.dtype)
    rk_sum = _group_sum(r * k * rk_ref[...], ones64)

    p_vb = project(o + 2 * B_QK_WIDTH, o + 2 * B_QK_WIDTH + B_WIDTH)[:tm]

    v = shifted(p_v, c_v, c_l)
    v_o[...] = (v).astype(v_o.dtype)
    bonus_o[...] = (rk_sum * v).astype(bonus_o.dtype)

    p_gb = project(o + 2 * B_QK_WIDTH + B_WIDTH, IN_WIDTH)[:tm]

    q_b = p_qk[:, :B_QK_WIDTH]
    k_b = p_qk[:, B_QK_WIDTH:]
    lane = lax.broadcasted_iota(jnp.int32, (tm, B_QK_WIDTH), 1)
    first_half = (lane & (B_QK_DIM - 1)) < (B_QK_DIM // 2)
    cos = cos_ref[...]
    sin = sin_ref[...]

    def rot(t):
        swapped = jnp.where(first_half, pltpu.roll(t, B_QK_WIDTH - B_QK_DIM // 2, 1), pltpu.roll(t, B_QK_DIM // 2, 1))
        return t * cos + swapped * sin

    qb_o[...] = (rot(q_b)).astype(qb_o.dtype)
    kb_o[...] = (rot(k_b) * (B_QK_DIM ** -0.5)).astype(kb_o.dtype)

    vb_o[...] = (p_vb).astype(vb_o.dtype)
    gb_o[...] = (p_gb * _sigmoid(p_gb)).astype(gb_o.dtype)


def _proj(x2, x_prev, seq_len, pos0, W):
    n = x2.shape[0]
    tm = PROJ_ROWS
    assert n % tm == 0
    carry_mode = seq_len % tm == 0
    if carry_mode:
        tiles_per_seq = seq_len // tm
        xp = x_prev.reshape(-1, 1, D_MODEL)
        xp_spec = pl.BlockSpec((1, 1, D_MODEL), lambda i: (i // tiles_per_seq, 0, 0))
        tab_rows = seq_len
    else:
        assert tm % seq_len == 0 and seq_len & (seq_len - 1) == 0
        tiles_per_seq = 1
        xp = jnp.repeat(x_prev, seq_len, axis=0)
        xp_spec = pl.BlockSpec((tm, D_MODEL), lambda i: (i, 0))
        tab_rows = tm
    half = B_QK_DIM // 2
    inv = ROPE_BASE ** (-jnp.arange(half, dtype=F32) / half)
    pos = (pos0 + jnp.arange(seq_len, dtype=jnp.int32)).astype(F32)
    ang = pos[:, None] * inv[None, :]
    cos = jnp.tile(jnp.concatenate([jnp.cos(ang), jnp.cos(ang)], -1), (tab_rows // seq_len, B_HEADS))
    sin = jnp.tile(jnp.concatenate([-jnp.sin(ang), jnp.sin(ang)], -1), (tab_rows // seq_len, B_HEADS))
    tab_tiles = tab_rows // tm
    tab_spec = pl.BlockSpec((tm, B_QK_WIDTH), lambda i: (i % tab_tiles, 0))

    def full(a):
        return pl.BlockSpec(a.shape, lambda i: (0,) * a.ndim)

    def rows(width):
        return pl.BlockSpec((tm, width), lambda i: (i, 0))

    params = [W["w_in"], W["mu_shift"], W["w_decay_up"], W["decay_base"], W["w_aaa_up"], W["aaa_base"],
              W["w_gate_up"], W["k_k"], W["k_a"], W["r_k"]]
    widths = [A_WIDTH] * 8 + [B_QK_WIDTH, B_QK_WIDTH, B_WIDTH, B_WIDTH]
    outs = pl.pallas_call(
        functools.partial(_proj_kernel, carry_mode, seq_len, tiles_per_seq),
        grid=(n // tm,),
        in_specs=[rows(D_MODEL), xp_spec, pl.BlockSpec(params[0].shape, lambda i: (0, 0), pipeline_mode=pl.Buffered(1))]
        + [full(a) for a in params[1:]] + [tab_spec, tab_spec],
        out_specs=[rows(w) for w in widths],
        out_shape=[jax.ShapeDtypeStruct((n, w), F32 if i == LOG_DECAY_OUT else BF16) for i, w in enumerate(widths)],
        scratch_shapes=[pltpu.VMEM((V7X_SUBLANES, SHIFT_WIDTH), F32)],
        compiler_params=pltpu.CompilerParams(dimension_semantics=("arbitrary",),
                                             vmem_limit_bytes=V7X_VMEM_LIMIT_BYTES),
        name="proj",
    )(x2, xp, *params, cos, sin)
    return outs


def _mixer_kernel(nb, seqs, clen,
                  r_ref, lw_ref, k_ref, v_ref, al_ref, be_ref, qb_ref, kb_ref, vb_ref, wkv0_ref, ret0_ref,
                  ya_o, ob_o, wkv_o, ret_o, s_scr, r_scr):
    R = seqs * clen
    log2c = int(math.log2(clen))
    c_idx = pl.program_id(1)
    hd = A_HEAD_DIM
    pw2 = 2 * hd
    TP = [(t, j) for t in range(nb) for j in range(A_HEADS // 2)]
    TG = [(t, h) for t in range(nb) for h in range(B_HEADS)]

    @pl.when(c_idx == 0)
    def _():
        zero = jnp.zeros((hd, hd), F32)
        for ci, (t, j) in enumerate(TP):
            blocks = [jnp.concatenate([jnp.concatenate([wkv0_ref[t * seqs + i, 2 * j], zero], axis=1),
                                       jnp.concatenate([zero, wkv0_ref[t * seqs + i, 2 * j + 1]], axis=1)], axis=0)
                      for i in range(seqs)]
            s_scr[ci] = jnp.concatenate(blocks, axis=1) if seqs > 1 else blocks[0]
        for t, h in TG:
            r_scr[t * B_HEADS + h] = ret0_ref[t * seqs:(t + 1) * seqs, h].reshape(seqs * B_QK_DIM, B_V_DIM)

    row = lax.broadcasted_iota(jnp.int32, (R, R), 0)
    col = lax.broadcasted_iota(jnp.int32, (R, R), 1)
    same = (row >> log2c) == (col >> log2c)
    incl = same & (col <= row)
    m_incl = jnp.where(incl, 1.0, 0.0).astype(BF16)
    m_same = jnp.where(same, 1.0, 0.0).astype(BF16)

    def expand(t):
        if seqs == 1:
            return t
        w = t.shape[1]
        wide = jnp.concatenate([t] * seqs, axis=1)
        rr = lax.broadcasted_iota(jnp.int32, wide.shape, 0) >> log2c
        cc = lax.broadcasted_iota(jnp.int32, wide.shape, 1) // w
        return jnp.where(rr == cc, wide, 0.0)

    a_bar, r_bar, b_til, k_til, b_dec, k_dec, d_end, vv = [], [], [], [], [], [], [], []
    for t in range(nb):
        lw = lw_ref[t]
        parts = _split2(lw)
        c = sum(jnp.dot(m_incl, p, preferred_element_type=F32) for p in parts)
        if seqs == 1:
            is_last = lax.broadcasted_iota(jnp.int32, c.shape, 0) == R - 1
            cend = jnp.sum(jnp.where(is_last, c, 0.0), axis=0, keepdims=True)
        else:
            cend = sum(jnp.dot(m_same, p, preferred_element_type=F32) for p in parts)
        einv = jnp.exp(-c)
        edec = jnp.exp(cend - c)
        a_bar.append(al_ref[t] * jnp.exp(c - lw))
        r_bar.append(r_ref[t] * jnp.exp(c))
        b_til.append(be_ref[t] * einv)
        k_til.append(k_ref[t] * einv)
        b_dec.append(be_ref[t] * edec)
        k_dec.append(k_ref[t] * edec)
        d_end.append(jnp.exp(cend))
        vv.append(v_ref[t].astype(F32))
    last_row = (lax.broadcasted_iota(jnp.int32, (R, seqs * pw2), 0) & (clen - 1)) == clen - 1

    lane2 = lax.broadcasted_iota(jnp.int32, (R, pw2), 1)
    first = lane2 < hd
    row2 = lax.broadcasted_iota(jnp.int32, (R, pw2), 0)
    col2 = lane2 & (hd - 1)
    same2 = (row2 >> log2c) == (col2 >> log2c)
    incl2 = same2 & (col2 <= row2)
    strict2 = same2 & (col2 < row2)
    eye2 = jnp.where(row2 == col2, 1.0, 0.0).astype(F32)

    def pair(x, j):
        return x[:, j * pw2:(j + 1) * pw2]

    def keep(x, second):
        m = first if x.shape[1] == pw2 else jnp.concatenate([first] * (x.shape[1] // pw2), axis=1)
        return jnp.where(m != second, x, 0.0)

    def blockdiag(y, swapped=False):
        return jnp.concatenate([keep(y, swapped), keep(y, not swapped)], axis=0)

    ams = []
    for t, j in TP:
        lhs = jnp.concatenate([pair(a_bar[t], j), pair(r_bar[t], j)], axis=0)
        bt, kt = pair(b_til[t], j), pair(k_til[t], j)
        lhs0 = jnp.where(jnp.concatenate([first, first], axis=0), lhs, 0.0)
        lhs1 = jnp.where(jnp.concatenate([first, first], axis=0), 0.0, lhs)
        ams.append((_dot_nt(lhs0, jnp.concatenate([bt, kt], axis=0)), _dot_nt(lhs1, jnp.concatenate([kt, bt], axis=0))))

    lgs = [float(np.log1p(-np.exp2(-5.0 - h))) for h in range(B_HEADS)]
    qb = [qb_ref[t].astype(F32) for t in range(nb)]
    kb = [kb_ref[t].astype(F32) for t in range(nb)]
    qs = [qb[t][:, h * B_QK_DIM:(h + 1) * B_QK_DIM] for t, h in TG]
    khs = [kb[t][:, h * B_QK_DIM:(h + 1) * B_QK_DIM] for t, h in TG]
    vbs = [vb_ref[t][:, h * B_V_DIM:(h + 1) * B_V_DIM] for t, h in TG]
    diff = (row - col).astype(F32)
    pos_v = (lax.broadcasted_iota(jnp.int32, (R, B_V_DIM), 0) & (clen - 1)).astype(F32)
    pos_k = (lax.broadcasted_iota(jnp.int32, (R, B_QK_DIM), 0) & (clen - 1)).astype(F32)
    intra = [jnp.where(incl, jnp.exp(lg * diff), 0.0) for lg in lgs]
    cross = [jnp.exp(lg * (pos_v + 1.0)) for lg in lgs]
    kdec = [jnp.exp(lg * (clen - 1.0 - pos_k)) for lg in lgs]
    rstates = [r_scr[t * B_HEADS + h] for t, h in TG]
    scs = [_dot_nt(qs[i], khs[i]) * intra[h] for i, (t, h) in enumerate(TG)]
    qst = [_dot(expand(qs[i]), rstates[i]) * cross[h] for i, (t, h) in enumerate(TG)]

    n = len(TP)
    a_ab = [jnp.where(strict2, jnp.where(first, m0[:R], m1[:R]), 0.0) for m0, m1 in ams]
    a_ak = [jnp.where(strict2, jnp.where(first, m1[:R], m0[:R]), 0.0) for m0, m1 in ams]
    a_rb = [jnp.where(incl2, jnp.where(first, m0[R:], m1[R:]), 0.0) for m0, m1 in ams]
    a_rk = [jnp.where(incl2, jnp.where(first, m1[R:], m0[R:]), 0.0) for m0, m1 in ams]
    vps = [pair(vv[t], j) for t, j in TP]
    av = [_dot(a_ak[i], blockdiag(vps[i], swapped=True)) for i in range(n)]
    tinv = [eye2 + a for a in a_ab]
    if log2c > 1:
        pw = [_dot(a, blockdiag(a)) for a in a_ab]
    for it in range(log2c - 1):
        if it < log2c - 2:
            tp = [_dot(pw[i], blockdiag(jnp.concatenate([tinv[i], pw[i]], axis=1))) for i in range(n)]
            tinv = [tinv[i] + tp[i][:, :pw2] for i in range(n)]
            pw = [tp[i][:, pw2:] for i in range(n)]
        else:
            tinv = [tinv[i] + _dot(pw[i], blockdiag(tinv[i])) for i in range(n)]

    os_ = [_dot(scs[i], vbs[i]) + qst[i] for i in range(len(TG))]
    for i, (t, h) in enumerate(TG):
        r_scr[t * B_HEADS + h] = (rstates[i] * float(np.exp(lgs[h] * clen))
                                  + _dot_tn(expand(khs[i] * kdec[h]), vbs[i]))
    for t in range(nb):
        ob_o[t] = jnp.concatenate(os_[t * B_HEADS:(t + 1) * B_HEADS], axis=1)

    wu = [_dot(tinv[i], blockdiag(jnp.concatenate([pair(a_bar[t], j), av[i]], axis=1)))
          for i, (t, j) in enumerate(TP)]
    states = [s_scr[i] for i in range(n)]
    ws = [_dot_nt(jnp.concatenate([expand(wu[i][:, :pw2]), expand(pair(r_bar[t], j))], axis=0), states[i])
          for i, (t, j) in enumerate(TP)]
    us = [ws[i][:R] + wu[i][:, pw2:] for i in range(n)]
    ys = [ws[i][R:] + _dot(jnp.concatenate([a_rb[i], a_rk[i]], axis=1),
                           jnp.concatenate([blockdiag(us[i]), blockdiag(vps[i], swapped=True)], axis=0))
          for i in range(n)]
    npair = A_HEADS // 2
    for t in range(nb):
        ya_o[t] = jnp.concatenate(ys[t * npair:(t + 1) * npair], axis=1)
    rowp = lax.broadcasted_iota(jnp.int32, (pw2, seqs * pw2), 0) >= hd
    lanep = (lax.broadcasted_iota(jnp.int32, (pw2, seqs * pw2), 1) & (pw2 - 1)) >= hd
    for i, (t, j) in enumerate(TP):
        d_row = jnp.sum(jnp.where(last_row, expand(pair(d_end[t], j)), 0.0), axis=0, keepdims=True)
        bk = jnp.concatenate([expand(pair(b_dec[t], j)), expand(pair(k_dec[t], j))], axis=0)
        upd = _dot_tn(jnp.concatenate([us[i], vps[i]], axis=0), bk)
        s_scr[i] = states[i] * d_row + jnp.where(rowp == lanep, upd, 0.0)

    @pl.when(c_idx == pl.num_programs(1) - 1)
    def _():
        for ci, (t, j) in enumerate(TP):
            st = s_scr[ci]
            for i in range(seqs):
                wkv_o[t * seqs + i, 2 * j] = st[:hd, i * pw2:i * pw2 + hd]
                wkv_o[t * seqs + i, 2 * j + 1] = st[hd:, i * pw2 + hd:(i + 1) * pw2]
        for t, h in TG:
            ret_o[t * seqs:(t + 1) * seqs, h] = r_scr[t * B_HEADS + h].reshape(seqs, B_QK_DIM, B_V_DIM)


def _mixer(ops, wkv0, ret0, n_seq, seq_len):
    n = ops[0].shape[0]
    R = CHUNK_ROWS
    if seq_len >= R:
        assert seq_len % R == 0
        seqs, clen, nchunks = 1, R, seq_len // R
    else:
        assert R % seq_len == 0 and seq_len & (seq_len - 1) == 0 and n_seq % (R // seq_len) == 0
        seqs, clen, nchunks = R // seq_len, seq_len, 1
    ntiles = n_seq // seqs
    nb = MIXER_TILES if seqs == 1 else MIXER_TILES_PACKED
    assert ntiles % nb == 0
    ops3 = [a.reshape(ntiles, nchunks * R, a.shape[1]) for a in ops]

    def rows(width):
        return pl.BlockSpec((nb, R, width), lambda i, c: (i, c, 0))

    wkv_spec = pl.BlockSpec((nb * seqs, A_HEADS, A_HEAD_DIM, A_HEAD_DIM), lambda i, c: (i, 0, 0, 0))
    ret_spec = pl.BlockSpec((nb * seqs, B_HEADS, B_QK_DIM, B_V_DIM), lambda i, c: (i, 0, 0, 0))
    ya, ob, wkv1, ret1 = pl.pallas_call(
        functools.partial(_mixer_kernel, nb, seqs, clen),
        grid=(ntiles // nb, nchunks),
        in_specs=[rows(A_WIDTH)] * 6 + [rows(B_QK_WIDTH), rows(B_QK_WIDTH), rows(B_WIDTH), wkv_spec, ret_spec],
        out_specs=[rows(A_WIDTH), rows(B_WIDTH), wkv_spec, ret_spec],
        out_shape=[jax.ShapeDtypeStruct((ntiles, nchunks * R, A_WIDTH), F32),
                   jax.ShapeDtypeStruct((ntiles, nchunks * R, B_WIDTH), F32),
                   jax.ShapeDtypeStruct(wkv0.shape, F32), jax.ShapeDtypeStruct(ret0.shape, F32)],
        scratch_shapes=[pltpu.VMEM((nb * A_HEADS // 2, 2 * A_HEAD_DIM, seqs * 2 * A_HEAD_DIM), F32),
                        pltpu.VMEM((nb * B_HEADS, seqs * B_QK_DIM, B_V_DIM), F32)],
        compiler_params=pltpu.CompilerParams(dimension_semantics=("parallel", "arbitrary"),
                                             vmem_limit_bytes=V7X_VMEM_LIMIT_BYTES),
        name="mixer",
    )(*ops3, wkv0, ret0)
    return ya.reshape(n, A_WIDTH), ob.reshape(n, B_WIDTH), wkv1, ret1


def _layer_norm(z, g, b):
    mu = jnp.mean(z, axis=-1, keepdims=True)
    d = z - mu
    var = jnp.mean(d * d, axis=-1, keepdims=True)
    return d * lax.rsqrt(var + LN_EPS) * g + b


def _post_tile(ya_ref, ob_ref, bonus_ref, g_ref, gb_ref, x_ref, lnxg_ref, lnxb_ref, rgg_ref, rgb_ref,
               wout_ref, ln1g_ref, ln1b_ref, wr_ref, br_ref, h_o, gate_o, cnt_o):
    tm = x_ref.shape[0]

    def head_norm(t, group, eps, gg, bb):
        ones = _group_ones(group)
        mu = _group_sum(t, ones) * (1.0 / group)
        d = t - mu
        var = _group_sum(d * d, ones) * (1.0 / group)
        return d * lax.rsqrt(var + eps) * gg + bb

    y_a = (head_norm(ya_ref[...], A_HEAD_DIM, GN_EPS_RWKV, lnxg_ref[...], lnxb_ref[...]) + bonus_ref[...]) * g_ref[...]
    y_b = head_norm(ob_ref[...], B_V_DIM, GN_EPS, rgg_ref[...], rgb_ref[...]) * gb_ref[...]
    y = jnp.concatenate([y_a, y_b], axis=1)
    mix = _dot(y, wout_ref[...])
    h = _layer_norm(DEEPNORM_ALPHA * x_ref[...] + mix, ln1g_ref[...], ln1b_ref[...])
    h_o[...] = h

    h_hi, h_lo = _split2(h)
    w_hi, w_lo = _split2(wr_ref[...])
    hi_terms = jnp.dot(h_hi, jnp.concatenate([w_hi, w_lo], axis=1), preferred_element_type=F32)
    logits = (hi_terms[:, :ROUTER_LANES] + hi_terms[:, ROUTER_LANES:]
              + jnp.dot(h_lo, w_hi, preferred_element_type=F32)) + br_ref[...]
    lane = lax.broadcasted_iota(jnp.int32, (tm, ROUTER_LANES), 1).astype(F32)
    neg = -jnp.inf
    big = float(ROUTER_LANES)
    cl = jnp.where(lane < N_GROUPS, logits, neg)
    cmax = jnp.max(cl, axis=-1, keepdims=True)
    grp = jnp.min(jnp.where(cl == cmax, lane, big), axis=-1, keepdims=True)
    gprob = 1.0 / jnp.sum(jnp.exp(cl - cmax), axis=-1, keepdims=True)
    lo_lane = FINE_LANE0 + grp * EXPERTS_PER_GROUP
    fv = jnp.where((lane >= lo_lane) & (lane < lo_lane + EXPERTS_PER_GROUP), logits, neg)
    m1 = jnp.max(fv, axis=-1, keepdims=True)
    i1 = jnp.min(jnp.where(fv == m1, lane, big), axis=-1, keepdims=True)
    fv2 = jnp.where(lane == i1, neg, fv)
    m2 = jnp.max(fv2, axis=-1, keepdims=True)
    i2 = jnp.min(jnp.where(fv2 == m2, lane, big), axis=-1, keepdims=True)
    e2 = jnp.exp(m2 - m1)
    w1 = gprob / (1.0 + e2)
    w2 = gprob * e2 / (1.0 + e2)
    gate_o[...] = (jnp.where(lane == i1, w1, 0.0) + jnp.where(lane == i2, w2, 0.0)
                   + jnp.where(lane == GROUP_LANE, grp, 0.0))
    onehot = jnp.where((lane == grp) & (lane < N_GROUPS), 1.0, 0.0)
    cnt_o[0] = jnp.broadcast_to(jnp.sum(onehot, axis=0, keepdims=True), (V7X_SUBLANES, ROUTER_LANES)).astype(jnp.int32)


POST_OPERANDS = 6


def _post_kernel(group_tiles, *refs):
    per_group, shared = refs[:POST_OPERANDS * len(group_tiles)], refs[POST_OPERANDS * len(group_tiles):]
    i = pl.program_id(0)
    start = 0
    for k, tiles in enumerate(group_tiles):
        @pl.when((i >= start) & (i < start + tiles))
        def _(k=k):
            _post_tile(*per_group[POST_OPERANDS * k:POST_OPERANDS * (k + 1)], *shared)

        start += tiles


def _group_rows(tm, width, start, tiles):
    return pl.BlockSpec((tm, width), lambda i, *_: (jnp.clip(i - start, 0, tiles - 1), 0))


def _post(groups, W):
    tm = POST_ROWS
    group_tiles = tuple(g[-1].shape[0] // tm for g in groups)
    assert all(g[-1].shape[0] % tm == 0 for g in groups)
    n = tm * sum(group_tiles)

    def full(a):
        return pl.BlockSpec(a.shape, lambda i: (0,) * a.ndim)

    def rows(width):
        return pl.BlockSpec((tm, width), lambda i: (i, 0))

    widths = [A_WIDTH, B_WIDTH, A_WIDTH, A_WIDTH, B_WIDTH, D_MODEL]
    in_specs, start = [], 0
    for tiles in group_tiles:
        in_specs += [_group_rows(tm, w, start, tiles) for w in widths]
        start += tiles
    params = [W["lnx_g"], W["lnx_b"], W["ret_gn_g"], W["ret_gn_b"], W["w_out"], W["ln1_g"], W["ln1_b"],
              W["w_router"], W["b_router"]]
    return pl.pallas_call(
        functools.partial(_post_kernel, group_tiles),
        grid=(n // tm,),
        in_specs=in_specs + [full(a) for a in params],
        out_specs=[rows(D_MODEL), rows(ROUTER_LANES),
                   pl.BlockSpec((1, V7X_SUBLANES, ROUTER_LANES), lambda i: (i, 0, 0))],
        out_shape=[jax.ShapeDtypeStruct((n, D_MODEL), F32), jax.ShapeDtypeStruct((n, ROUTER_LANES), F32),
                   jax.ShapeDtypeStruct((n // tm, V7X_SUBLANES, ROUTER_LANES), jnp.int32)],
        compiler_params=pltpu.CompilerParams(dimension_semantics=("parallel",),
                                             vmem_limit_bytes=V7X_VMEM_LIMIT_BYTES),
        name="post",
    )(*[a for g in groups for a in g], *params)


def _sort_positions(gate, lofs):
    tm = gate.shape[0]
    lane = lax.broadcasted_iota(jnp.int32, (tm, ROUTER_LANES), 1)
    grp = gate[:, GROUP_LANE:GROUP_LANE + 1].astype(jnp.int32)
    onehot = jnp.where((lane == grp) & (lane < N_GROUPS), 1.0, 0.0)
    r = lax.broadcasted_iota(jnp.int32, (tm, tm), 0)
    c = lax.broadcasted_iota(jnp.int32, (tm, tm), 1)
    earlier = jnp.where(c < r, 1.0, 0.0).astype(BF16)
    prefix = jnp.dot(earlier, onehot.astype(BF16), preferred_element_type=F32)
    base = jnp.zeros((tm, ROUTER_LANES), F32)
    for g in range(N_GROUPS):
        base = jnp.where(lane == g, lofs[g].astype(F32), base)
    return jnp.sum(onehot * (base + prefix), axis=1, keepdims=True).astype(jnp.int32)


def _piece_copies(action, rows, bits, copy_of):
    k = rows // SORT_ALIGN
    for b in reversed(range(bits)):
        size = SORT_ALIGN << b

        @pl.when(((k >> b) & 1) == 1)
        def _():
            done = ((k >> (b + 1)) << (b + 1)) * SORT_ALIGN
            cp = copy_of(done, size)
            cp.start() if action == "start" else cp.wait()


def _run_copies(action, plan_ref, i, src_of, dst_of, sem):
    for g in range(N_GROUPS):
        goff = plan_ref[i, g]
        lofs = plan_ref[i, 2 * N_GROUPS + g]

        def copy_of(done, size, goff=goff, lofs=lofs):
            lo = pl.multiple_of(lofs + done, SORT_ALIGN)
            go = pl.multiple_of(goff + done, SORT_ALIGN)
            return pltpu.make_async_copy(src_of(lo, go, size), dst_of(lo, go, size), sem)

        _piece_copies(action, plan_ref[i, N_GROUPS + g], RUN_BITS, copy_of)


def _dispatch_kernel(plan_ref, gap_ref, h_ref, gate_ref, hs_o, h_loc, sem):
    i = pl.program_id(0)
    tm = h_ref.shape[0]
    lofs = [plan_ref[i, 2 * N_GROUPS + g] for g in range(N_GROUPS)]
    pos = _sort_positions(gate_ref[...], lofs)
    onehot_t = jnp.where(lax.broadcasted_iota(jnp.int32, (tm, SORT_LOCAL), 1) == pos, 1.0, 0.0).astype(BF16)
    idx = lax.broadcasted_iota(jnp.int32, (V7X_SUBLANES, SORT_LOCAL), 1)
    pos_row = (_dot_nt((idx // V7X_LANES).astype(F32), onehot_t) * float(V7X_LANES)
               + _dot_nt((idx % V7X_LANES).astype(F32), onehot_t))[0:1].astype(jnp.int32)
    perm = jnp.where(lax.broadcasted_iota(jnp.int32, (SORT_LOCAL, tm), 0) == pos_row, 1.0, 0.0).astype(BF16)
    payload = jnp.concatenate([h_ref[...].astype(BF16)] + list(_split3(gate_ref[...])), axis=1)
    sorted_rows = jnp.dot(perm, payload, preferred_element_type=F32).astype(BF16)

    def copies(action, tile):
        _run_copies(action, plan_ref, tile, lambda lo, go, sz: h_loc.at[pl.ds(lo, sz)],
                    lambda lo, go, sz: hs_o.at[pl.ds(go, sz)], sem)

    @pl.when(i > 0)
    def _():
        copies("wait", i - 1)

    h_loc[...] = sorted_rows
    copies("start", i)

    @pl.when(i == pl.num_programs(0) - 1)
    def _():
        copies("wait", i)
        h_loc[...] = jnp.zeros_like(h_loc)
        for action in ("start", "wait"):
            for g in range(N_GROUPS):
                def copy_of(done, size, g=g):
                    go = pl.multiple_of(gap_ref[g] + done, SORT_ALIGN)
                    return pltpu.make_async_copy(h_loc.at[pl.ds(0, size)], hs_o.at[pl.ds(go, size)], sem)

                _piece_copies(action, gap_ref[N_GROUPS + g], GAP_BITS, copy_of)

            def body(k, carry):
                go = pl.multiple_of(gap_ref[2 * N_GROUPS] + k * EXPERT_ROWS_SHORT, SORT_ALIGN)
                cp = pltpu.make_async_copy(h_loc.at[pl.ds(0, EXPERT_ROWS_SHORT)], hs_o.at[pl.ds(go, EXPERT_ROWS_SHORT)], sem)
                cp.start() if action == "start" else cp.wait()
                return carry

            lax.fori_loop(0, gap_ref[2 * N_GROUPS + 1] // EXPERT_ROWS_SHORT, body, 0)


def _experts_kernel(tile_group_ref, n_valid_ref, hs_ref, w1_ref, w3_ref, w2_ref, ys_o, w1_b, w3_b, w2_b):
    j = pl.program_id(0)
    last = n_valid_ref[0] - 1
    g = tile_group_ref[jnp.minimum(j, last)]
    g_prev = tile_group_ref[jnp.minimum(jnp.maximum(j - 1, 0), last)]

    @pl.when((j == 0) | (g != g_prev))
    def _():
        w1_b[...] = w1_ref[...].astype(BF16)
        w3_b[...] = w3_ref[...].astype(BF16)
        w2_b[...] = w2_ref[...].astype(BF16)

    @pl.when(j <= last)
    def _():
        x = hs_ref[:, :D_MODEL]
        gs = sum(hs_ref[:, D_MODEL + t * ROUTER_LANES:D_MODEL + (t + 1) * ROUTER_LANES].astype(F32) for t in range(3))
        lane = lax.broadcasted_iota(jnp.int32, gs.shape, 1)
        acc = jnp.zeros(ys_o.shape, F32)
        for e in range(EXPERTS_PER_GROUP):
            ge = jnp.sum(jnp.where(lane == FINE_LANE0 + g * EXPERTS_PER_GROUP + e, gs, 0.0), axis=-1, keepdims=True)
            a = jnp.dot(x, w1_b[e], preferred_element_type=F32)
            b = jnp.dot(x, w3_b[e], preferred_element_type=F32)
            hid = (a * _sigmoid(a)) * b * ge
            acc = acc + jnp.dot(hid.astype(BF16), w2_b[e], preferred_element_type=F32)
        ys_o[...] = acc.astype(ys_o.dtype)

    @pl.when(j > last)
    def _():
        ys_o[...] = jnp.zeros_like(ys_o)


def _combine_kernel(group_tiles, plan_ref, h_ref, gate_ref, *refs):
    ng = len(group_tiles)
    p_refs, (ys_ref, ln2g_ref, ln2b_ref, wple_ref, wpg_ref, pleg_ref) = refs[:ng], refs[ng:ng + 6]
    o_refs, (y_loc, sem) = refs[ng + 6:2 * ng + 6], refs[2 * ng + 6:]
    i = pl.program_id(0)
    starts = [sum(group_tiles[:k]) for k in range(ng)]

    def in_group(k):
        return (i >= starts[k]) & (i < starts[k] + group_tiles[k])

    tm = h_ref.shape[0]
    lofs = [plan_ref[i, 2 * N_GROUPS + g] for g in range(N_GROUPS)]

    def fetch(tile):
        slot = tile % 2
        y_loc[slot] = jnp.zeros(y_loc.shape[1:], y_loc.dtype)
        _run_copies("start", plan_ref, tile, lambda lo, go, sz: ys_ref.at[pl.ds(go, sz)],
                    lambda lo, go, sz: y_loc.at[slot, pl.ds(lo, sz)], sem.at[slot])

    @pl.when(i == 0)
    def _():
        fetch(i)

    @pl.when(i + 1 < pl.num_programs(0))
    def _():
        fetch(i + 1)

    pos = _sort_positions(gate_ref[...], lofs)
    onehot_t = jnp.where(lax.broadcasted_iota(jnp.int32, (tm, SORT_LOCAL), 1) == pos, 1.0, 0.0).astype(BF16)
    p_tile = p_refs[0][...]
    for k in range(1, ng):
        p_tile = jnp.where(in_group(k), p_refs[k][...], p_tile)
    ple_in = _dot(p_tile, wple_ref[...])
    slot = i % 2
    _run_copies("wait", plan_ref, i, lambda lo, go, sz: ys_ref.at[pl.ds(go, sz)],
                lambda lo, go, sz: y_loc.at[slot, pl.ds(lo, sz)], sem.at[slot])
    ffn = jnp.dot(onehot_t, y_loc[slot], preferred_element_type=F32)
    h2 = _layer_norm(DEEPNORM_ALPHA * h_ref[...] + ffn, ln2g_ref[...], ln2b_ref[...])
    ple = ple_in * _sigmoid(_dot(h2, wpg_ref[...]))
    ms = jnp.mean(ple * ple, axis=-1, keepdims=True)
    out = h2 + ple * lax.rsqrt(ms + LN_EPS) * pleg_ref[...]
    for k in range(ng):
        @pl.when(in_group(k))
        def _(k=k):
            o_refs[k][...] = out


def _ffn(h, gate, counts, p_groups, W):
    n = h.shape[0]
    tm = SORT_ROWS
    assert n % tm == 0 and tm == POST_ROWS and SORT_LOCAL >= tm + N_GROUPS * SORT_ALIGN
    ntiles = n // tm
    er = EXPERT_ROWS if n >= 2 * N_GROUPS * EXPERT_ROWS else EXPERT_ROWS_SHORT
    assert er <= SORT_ALIGN << GAP_BITS and SORT_ALIGN << (GAP_BITS - 1) <= SORT_LOCAL
    cnt = counts[:, 0, :N_GROUPS]
    run = (cnt + SORT_ALIGN - 1) // SORT_ALIGN * SORT_ALIGN
    lofs = jnp.cumsum(run, axis=1) - run
    seg = (jnp.sum(run, axis=0) + er - 1) // er * er
    gbase = jnp.cumsum(seg) - seg
    goff = gbase[None, :] + jnp.cumsum(run, axis=0) - run
    plan = jnp.concatenate([goff, run, lofs], axis=1).astype(jnp.int32)
    max_tiles = (n + ntiles * N_GROUPS * (SORT_ALIGN - 1)) // er + N_GROUPS
    cap = max_tiles * er
    n_valid = (jnp.sum(seg) // er).astype(jnp.int32).reshape(1)
    tile_start = jnp.arange(max_tiles, dtype=jnp.int32) * er
    tile_group = jnp.clip(jnp.sum(tile_start[:, None] >= (gbase + seg)[None, :], axis=1), 0, N_GROUPS - 1).astype(jnp.int32)

    cparams = dict(vmem_limit_bytes=V7X_VMEM_LIMIT_BYTES)
    any_spec = pl.BlockSpec(memory_space=pl.ANY)
    total = jnp.sum(run, axis=0)
    assert er % EXPERT_ROWS_SHORT == 0 and EXPERT_ROWS_SHORT <= SORT_LOCAL
    used = jnp.sum(seg)
    gaps = jnp.concatenate([gbase + total, seg - total, jnp.stack([used, cap - used])]).astype(jnp.int32)
    hs = pl.pallas_call(
        _dispatch_kernel,
        grid_spec=pltpu.PrefetchScalarGridSpec(
            num_scalar_prefetch=2, grid=(ntiles,),
            in_specs=[pl.BlockSpec((tm, D_MODEL), lambda i, plan, gaps: (i, 0)),
                      pl.BlockSpec((tm, ROUTER_LANES), lambda i, plan, gaps: (i, 0))],
            out_specs=any_spec,
            scratch_shapes=[pltpu.VMEM((SORT_LOCAL, SORTED_WIDTH), BF16), pltpu.SemaphoreType.DMA(())]),
        out_shape=jax.ShapeDtypeStruct((cap, SORTED_WIDTH), BF16),
        compiler_params=pltpu.CompilerParams(dimension_semantics=("arbitrary",), **cparams),
        name="dispatch",
    )(plan, gaps, h, gate)

    def tile_rows(width):
        return pl.BlockSpec((er, width), lambda j, tg, nv: (jnp.minimum(j, nv[0] - 1), 0))

    def group_w(shape):
        return pl.BlockSpec((EXPERTS_PER_GROUP,) + shape, lambda j, tg, nv: (tg[jnp.minimum(j, nv[0] - 1)], 0, 0))

    ys = pl.pallas_call(
        _experts_kernel,
        grid_spec=pltpu.PrefetchScalarGridSpec(
            num_scalar_prefetch=2, grid=(max_tiles,),
            in_specs=[tile_rows(SORTED_WIDTH), group_w((D_MODEL, D_EXPERT)),
                      group_w((D_MODEL, D_EXPERT)), group_w((D_EXPERT, D_MODEL))],
            out_specs=pl.BlockSpec((er, D_MODEL), lambda j, tg, nv: (j, 0)),
            scratch_shapes=[pltpu.VMEM((EXPERTS_PER_GROUP, D_MODEL, D_EXPERT), BF16),
                            pltpu.VMEM((EXPERTS_PER_GROUP, D_MODEL, D_EXPERT), BF16),
                            pltpu.VMEM((EXPERTS_PER_GROUP, D_EXPERT, D_MODEL), BF16)]),
        out_shape=jax.ShapeDtypeStruct((cap, D_MODEL), BF16),
        compiler_params=pltpu.CompilerParams(dimension_semantics=("arbitrary",), **cparams),
        name="experts",
    )(tile_group, n_valid, hs, W["expert_w1"], W["expert_w3"], W["expert_w2"])

    def full(a):
        return pl.BlockSpec(a.shape, lambda i, plan: (0,) * a.ndim)

    params = [W["ln2_g"], W["ln2_b"], W["w_ple"], W["w_ple_gate"], W["ple_norm_g"]]
    group_tiles = tuple(pg.shape[0] // tm for pg in p_groups)
    assert sum(group_tiles) == ntiles
    starts = [sum(group_tiles[:k]) for k in range(len(group_tiles))]
    return pl.pallas_call(
        functools.partial(_combine_kernel, group_tiles),
        grid_spec=pltpu.PrefetchScalarGridSpec(
            num_scalar_prefetch=1, grid=(ntiles,),
            in_specs=[pl.BlockSpec((tm, D_MODEL), lambda i, plan: (i, 0)),
                      pl.BlockSpec((tm, ROUTER_LANES), lambda i, plan: (i, 0))]
            + [_group_rows(tm, D_PLE, st, t) for st, t in zip(starts, group_tiles)]
            + [any_spec] + [full(a) for a in params],
            out_specs=[_group_rows(tm, D_MODEL, st, t) for st, t in zip(starts, group_tiles)],
            scratch_shapes=[pltpu.VMEM((2, SORT_LOCAL, D_MODEL), BF16), pltpu.SemaphoreType.DMA((2,))]),
        out_shape=[jax.ShapeDtypeStruct((t * tm, D_MODEL), F32) for t in group_tiles],
        compiler_params=pltpu.CompilerParams(dimension_semantics=("arbitrary",), **cparams),
        name="combine",
    )(plan, h, gate, *p_groups, ys, *params)


def _mix(x, x_prev, wkv0, ret0, pos0, W):
    n_seq, seq_len, _ = x.shape
    x2 = x.reshape(n_seq * seq_len, D_MODEL)
    r, lw, k, v, al, be, g, bonus, qb, kb, vb, gb = _proj(x2, x_prev, seq_len, pos0, W)
    ya, ob, wkv1, ret1 = _mixer((r, lw, k, v, al, be, qb, kb, vb), wkv0, ret0, n_seq, seq_len)
    return (ya, ob, bonus, g, gb, x2), wkv1, ret1


def _prep_weights(i, w_in, mu_shift, w_decay_up, decay_base, w_aaa_up, aaa_base, w_gate_up, k_k, k_a, r_k,
                  lnx_g, lnx_b, ret_gn_g, ret_gn_b, w_out, ln1_g, ln1_b,
                  router_coarse_w, router_coarse_b, router_fine_w, router_fine_b,
                  expert_w1, expert_w3, expert_w2, ln2_g, ln2_b, w_ple, w_ple_gate, ple_norm_g):
    def row(a):
        return a[i].reshape(1, -1).astype(F32)

    pad = ROUTER_LANES - N_GROUPS - N_EXPERTS
    w_router = jnp.concatenate([router_coarse_w[i], router_fine_w[i], jnp.zeros((D_MODEL, pad), F32)], axis=1)
    b_router = jnp.concatenate([router_coarse_b[i], router_fine_b[i], jnp.zeros((pad,), F32)]).reshape(1, -1)
    return {
        "w_in": w_in[i].astype(BF16), "mu_shift": row(mu_shift), "w_decay_up": w_decay_up[i].astype(BF16),
        "decay_base": row(decay_base), "w_aaa_up": w_aaa_up[i].astype(BF16), "aaa_base": row(aaa_base),
        "w_gate_up": w_gate_up[i].astype(BF16), "k_k": row(k_k), "k_a": row(k_a), "r_k": row(r_k),
        "lnx_g": row(lnx_g), "lnx_b": row(lnx_b), "ret_gn_g": row(ret_gn_g), "ret_gn_b": row(ret_gn_b),
        "w_out": w_out[i].astype(BF16), "ln1_g": row(ln1_g), "ln1_b": row(ln1_b),
        "w_router": w_router, "b_router": b_router,
        "expert_w1": expert_w1[i], "expert_w3": expert_w3[i], "expert_w2": expert_w2[i], "ln2_g": row(ln2_g), "ln2_b": row(ln2_b),
        "w_ple": w_ple[i].astype(BF16), "w_ple_gate": w_ple_gate[i].astype(BF16), "ple_norm_g": row(ple_norm_g),
    }


def kernel(x_prompt, x_sample, p_prompt, p_sample, state_wkv, state_shift, state_ret, w_in, mu_shift, w_decay_up, decay_base, w_aaa_up, aaa_base, w_gate_up, k_k, k_a, r_k, lnx_g, lnx_b, ret_gn_g, ret_gn_b, w_out, ln1_g, ln1_b, router_coarse_w, router_coarse_b, router_fine_w, router_fine_b, expert_w1, expert_w3, expert_w2, ln2_g, ln2_b, w_ple, w_ple_gate, ple_norm_g):
    yp, ys = x_prompt, x_sample
    nb = x_prompt.shape[0]
    depth = w_in.shape[0]
    wkv_p, shift_p, ret_p, wkv_s, shift_s, ret_s = [], [], [], [], [], []
    for i in range(depth):
        W = _prep_weights(i, w_in, mu_shift, w_decay_up, decay_base, w_aaa_up, aaa_base, w_gate_up, k_k, k_a, r_k,
                          lnx_g, lnx_b, ret_gn_g, ret_gn_b, w_out, ln1_g, ln1_b,
                          router_coarse_w, router_coarse_b, router_fine_w, router_fine_b,
                          expert_w1, expert_w3, expert_w2, ln2_g, ln2_b, w_ple, w_ple_gate, ple_norm_g)
        ops_p, wp, rp = _mix(yp, jnp.zeros((nb, D_MODEL), F32), jnp.zeros((nb, A_HEADS, A_HEAD_DIM, A_HEAD_DIM), F32),
                             jnp.zeros((nb, B_HEADS, B_QK_DIM, B_V_DIM), F32), 0, W)
        ops_s, wsm, rsm = _mix(ys, state_shift[i], state_wkv[i], state_ret[i], PAST_LEN, W)
        sp, ss = yp[:, -1], ys[:, -1]
        h, gate, counts = _post([ops_p, ops_s], W)
        out_p, out_s = _ffn(h, gate, counts, [p_prompt[i].reshape(-1, D_PLE), p_sample[i].reshape(-1, D_PLE)], W)
        yp, ys = out_p.reshape(yp.shape), out_s.reshape(ys.shape)
        wkv_p.append(wp); shift_p.append(sp); ret_p.append(rp)
        wkv_s.append(wsm); shift_s.append(ss); ret_s.append(rsm)
    return (yp, ys, jnp.stack(wkv_p, 0), jnp.stack(shift_p, 0), jnp.stack(ret_p, 0),
            jnp.stack(wkv_s, 0), jnp.stack(shift_s, 0), jnp.stack(ret_s, 0))
```

```python
import functools
import math

import numpy as np
import jax
import jax.numpy as jnp
from jax import lax
from jax.experimental import pallas as pl
from jax.experimental.pallas import tpu as pltpu

F32 = jnp.float32
BF16 = jnp.bfloat16

D_MODEL = 1024
D_PLE = 256
A_HEADS = 8
A_HEAD_DIM = 64
A_WIDTH = A_HEADS * A_HEAD_DIM
DECAY_LORA = 64
AAA_LORA = 64
GATE_LORA = 128
GN_EPS_RWKV = 64e-5
B_HEADS = 4
B_QK_DIM = 64
B_V_DIM = 128
B_QK_WIDTH = B_HEADS * B_QK_DIM
B_WIDTH = B_HEADS * B_V_DIM
ROPE_BASE = 10000.0
GN_EPS = 1e-5
SHIFT_WIDTH = 3 * A_WIDTH + DECAY_LORA + AAA_LORA + GATE_LORA
IN_WIDTH = SHIFT_WIDTH + 2 * B_QK_WIDTH + 2 * B_WIDTH
N_GROUPS = 4
EXPERTS_PER_GROUP = 4
N_EXPERTS = N_GROUPS * EXPERTS_PER_GROUP
D_EXPERT = 256
DEPTH = 1
PAST_LEN = 16384
DEEPNORM_ALPHA = (2 * DEPTH) ** 0.25
LN_EPS = 1e-5

V7X_LANES = 128
V7X_SUBLANES = 8
V7X_MXU_WIDTH = 256
V7X_VMEM_LIMIT_BYTES = 56 * 1024 * 1024

PROJ_ROWS = 1024
LOG_DECAY_OUT = 1
CHUNK_ROWS = 64
MIXER_TILES = 8
MIXER_TILES_PACKED = 2
POST_ROWS = 512
SORT_ROWS = 512
SORT_ALIGN = 16
SORT_LOCAL = 640
EXPERT_ROWS = 1024
EXPERT_ROWS_SHORT = 256
GAP_BITS = 6
RUN_BITS = 6
GROUP_LANE = 0
ROUTER_LANES = V7X_LANES
ROUTE_ROWS = 32
FINE_LANE0 = N_GROUPS
SORTED_WIDTH = D_MODEL + 3 * ROUTER_LANES


def _dot(a, b):
    return jnp.dot(a.astype(BF16), b.astype(BF16), preferred_element_type=F32)


def _dot_nt(a, b):
    return lax.dot_general(a.astype(BF16), b.astype(BF16), (((1,), (1,)), ((), ())), preferred_element_type=F32)


def _dot_tn(a, b):
    return lax.dot_general(a.astype(BF16), b.astype(BF16), (((0,), (0,)), ((), ())), preferred_element_type=F32)


def _split2(x):
    hi = x.astype(BF16)
    lo = (x - hi.astype(F32)).astype(BF16)
    return hi, lo


def _split3(x):
    hi = x.astype(BF16)
    r1 = x - hi.astype(F32)
    mid = r1.astype(BF16)
    lo = (r1 - mid.astype(F32)).astype(BF16)
    return hi, mid, lo


def _sigmoid(x):
    return 1.0 / (1.0 + jnp.exp(-x))


def _group_ones(group):
    assert V7X_MXU_WIDTH % group == 0
    r = lax.broadcasted_iota(jnp.int32, (V7X_MXU_WIDTH, V7X_MXU_WIDTH), 0) // group
    c = lax.broadcasted_iota(jnp.int32, (V7X_MXU_WIDTH, V7X_MXU_WIDTH), 1) // group
    return jnp.where(r == c, 1.0, 0.0).astype(BF16)


def _group_sum(x, ones):
    xb, w = x.astype(BF16), ones.shape[0]
    assert x.shape[1] % w == 0
    return jnp.concatenate([jnp.dot(xb[:, j:j + w], ones, preferred_element_type=F32)
                            for j in range(0, x.shape[1], w)], axis=1)


def _proj_kernel(carry_mode, seq_len, tiles_per_seq,
                 x_ref, xp_ref, w_ref, mu_ref, wdec_ref, dbase_ref, waaa_ref, abase_ref, wgate_ref,
                 kk_ref, ka_ref, rk_ref, cos_ref, sin_ref,
                 r_o, lw_o, k_o, v_o, al_o, be_o, g_o, bonus_o, qb_o, kb_o, vb_o, gb_o,
                 carry_scr):
    tm = x_ref.shape[0]
    if carry_mode:
        xp = jnp.broadcast_to(xp_ref[0], (V7X_SUBLANES, D_MODEL))
        xb = jnp.concatenate([x_ref[...], xp], axis=0).astype(BF16)
        j = pl.program_id(0) % tiles_per_seq

        @pl.when(pl.program_id(0) == 0)
        def _():
            carry_scr[...] = jnp.zeros_like(carry_scr)
    else:
        xb = x_ref[...].astype(BF16)
        xpb = xp_ref[...].astype(BF16)

    def project(lo, hi):
        return jnp.dot(xb, w_ref[:, lo:hi], preferred_element_type=F32)

    def shifted(p, lo, hi):
        cur = p[:tm]
        row = lax.broadcasted_iota(jnp.int32, cur.shape, 0)
        rolled = pltpu.roll(cur, 1, 0)
        if carry_mode:
            first = jnp.where(j == 0, p[tm + V7X_SUBLANES - 1:], carry_scr[V7X_SUBLANES - 1:V7X_SUBLANES, lo:hi])
            prev = jnp.where(row == 0, first, rolled)
            carry_scr[:, lo:hi] = cur[tm - V7X_SUBLANES:]
        else:
            first = jnp.dot(xpb, w_ref[:, lo:hi], preferred_element_type=F32)
            prev = jnp.where((row & (seq_len - 1)) == 0, first, rolled)
        return cur + (prev - cur) * mu_ref[:, lo:hi]

    c_r, c_k, c_v, c_l = 0, A_WIDTH, 2 * A_WIDTH, 3 * A_WIDTH
    p_lora = project(c_l, SHIFT_WIDTH)
    p_k = project(c_k, c_v)
    p_r = project(c_r, c_k)

    lora = shifted(p_lora, c_l, SHIFT_WIDTH)
    w_lo = lora[:, :DECAY_LORA]
    a_lo = lora[:, DECAY_LORA:DECAY_LORA + AAA_LORA]
    g_lo = lora[:, DECAY_LORA + AAA_LORA:]
    z = -(dbase_ref[...] + _dot(jnp.tanh(w_lo), wdec_ref[...]))
    softplus = jnp.maximum(z, 0.0) + jnp.log(1.0 + jnp.exp(-jnp.abs(z)))
    log_w = -softplus - 0.5
    lw_o[...] = -jnp.exp(log_w)
    a = _sigmoid(abase_ref[...] + _dot(a_lo, waaa_ref[...]))
    g_o[...] = (_dot(_sigmoid(g_lo), wgate_ref[...])).astype(g_o.dtype)

    p_v = project(c_v, c_l)

    ones64 = _group_ones(A_HEAD_DIM)
    k0 = shifted(p_k, c_k, c_v)
    kk0 = k0 * kk_ref[...]
    ssq = _group_sum(kk0 * kk0, ones64)
    kk = kk0 * jnp.minimum(lax.rsqrt(ssq), 1e12)
    k = k0 * (1.0 + (a - 1.0) * ka_ref[...])
    k_o[...] = (k).astype(k_o.dtype)
    al_o[...] = (-kk).astype(al_o.dtype)
    be_o[...] = (kk * a).astype(be_o.dtype)

    o = SHIFT_WIDTH
    p_qk = project(o, o + 2 * B_QK_WIDTH)[:tm]

    r = shifted(p_r, c_r, c_k)
    r_o[...] = (r).astype(r_o.dtype)
    rk_sum = _group_sum(r * k * rk_ref[...], ones64)

    p_vb = project(o + 2 * B_QK_WIDTH, o + 2 * B_QK_WIDTH + B_WIDTH)[:tm]

    v = shifted(p_v, c_v, c_l)
    v_o[...] = (v).astype(v_o.dtype)
    bonus_o[...] = (rk_sum * v).astype(bonus_o.dtype)

    p_gb = project(o + 2 * B_QK_WIDTH + B_WIDTH, IN_WIDTH)[:tm]

    q_b = p_qk[:, :B_QK_WIDTH]
    k_b = p_qk[:, B_QK_WIDTH:]
    lane = lax.broadcasted_iota(jnp.int32, (tm, B_QK_WIDTH), 1)
    first_half = (lane & (B_QK_DIM - 1)) < (B_QK_DIM // 2)
    cos = cos_ref[...]
    sin = sin_ref[...]

    def rot(t):
        swapped = jnp.where(first_half, pltpu.roll(t, B_QK_WIDTH - B_QK_DIM // 2, 1), pltpu.roll(t, B_QK_DIM // 2, 1))
        return t * cos + swapped * sin

    qb_o[...] = (rot(q_b)).astype(qb_o.dtype)
    kb_o[...] = (rot(k_b) * (B_QK_DIM ** -0.5)).astype(kb_o.dtype)

    vb_o[...] = (p_vb).astype(vb_o.dtype)
    gb_o[...] = (p_gb * _sigmoid(p_gb)).astype(gb_o.dtype)


def _proj(x2, x_prev, seq_len, pos0, W):
    n = x2.shape[0]
    tm = PROJ_ROWS
    assert n % tm == 0
    carry_mode = seq_len % tm == 0
    if carry_mode:
        tiles_per_seq = seq_len // tm
        xp = x_prev.reshape(-1, 1, D_MODEL)
        xp_spec = pl.BlockSpec((1, 1, D_MODEL), lambda i: (i // tiles_per_seq, 0, 0))
        tab_rows = seq_len
    else:
        assert tm % seq_len == 0 and seq_len & (seq_len - 1) == 0
        tiles_per_seq = 1
        xp = jnp.repeat(x_prev, seq_len, axis=0)
        xp_spec = pl.BlockSpec((tm, D_MODEL), lambda i: (i, 0))
        tab_rows = tm
    half = B_QK_DIM // 2
    inv = ROPE_BASE ** (-jnp.arange(half, dtype=F32) / half)
    pos = (pos0 + jnp.arange(seq_len, dtype=jnp.int32)).astype(F32)
    ang = pos[:, None] * inv[None, :]
    cos = jnp.tile(jnp.concatenate([jnp.cos(ang), jnp.cos(ang)], -1), (tab_rows // seq_len, B_HEADS))
    sin = jnp.tile(jnp.concatenate([-jnp.sin(ang), jnp.sin(ang)], -1), (tab_rows // seq_len, B_HEADS))
    tab_tiles = tab_rows // tm
    tab_spec = pl.BlockSpec((tm, B_QK_WIDTH), lambda i: (i % tab_tiles, 0))

    def full(a):
        return pl.BlockSpec(a.shape, lambda i: (0,) * a.ndim)

    def rows(width):
        return pl.BlockSpec((tm, width), lambda i: (i, 0))

    params = [W["w_in"], W["mu_shift"], W["w_decay_up"], W["decay_base"], W["w_aaa_up"], W["aaa_base"],
              W["w_gate_up"], W["k_k"], W["k_a"], W["r_k"]]
    widths = [A_WIDTH] * 8 + [B_QK_WIDTH, B_QK_WIDTH, B_WIDTH, B_WIDTH]
    outs = pl.pallas_call(
        functools.partial(_proj_kernel, carry_mode, seq_len, tiles_per_seq),
        grid=(n // tm,),
        in_specs=[rows(D_MODEL), xp_spec, pl.BlockSpec(params[0].shape, lambda i: (0, 0), pipeline_mode=pl.Buffered(1))]
        + [full(a) for a in params[1:]] + [tab_spec, tab_spec],
        out_specs=[rows(w) for w in widths],
        out_shape=[jax.ShapeDtypeStruct((n, w), F32 if i == LOG_DECAY_OUT else BF16) for i, w in enumerate(widths)],
        scratch_shapes=[pltpu.VMEM((V7X_SUBLANES, SHIFT_WIDTH), F32)],
        compiler_params=pltpu.CompilerParams(dimension_semantics=("arbitrary",),
                                             vmem_limit_bytes=V7X_VMEM_LIMIT_BYTES),
        name="proj",
    )(x2, xp, *params, cos, sin)
    return outs


def _mixer_kernel(nb, seqs, clen,
                  r_ref, lw_ref, k_ref, v_ref, al_ref, be_ref, qb_ref, kb_ref, vb_ref, wkv0_ref, ret0_ref,
                  ya_o, ob_o, wkv_o, ret_o, s_scr, r_scr):
    R = seqs * clen
    log2c = int(math.log2(clen))
    c_idx = pl.program_id(1)
    hd = A_HEAD_DIM
    pw2 = 2 * hd
    TP = [(t, j) for t in range(nb) for j in range(A_HEADS // 2)]
    TG = [(t, h) for t in range(nb) for h in range(B_HEADS)]

    @pl.when(c_idx == 0)
    def _():
        zero = jnp.zeros((hd, hd), F32)
        for ci, (t, j) in enumerate(TP):
            blocks = [jnp.concatenate([jnp.concatenate([wkv0_ref[t * seqs + i, 2 * j], zero], axis=1),
                                       jnp.concatenate([zero, wkv0_ref[t * seqs + i, 2 * j + 1]], axis=1)], axis=0)
                      for i in range(seqs)]
            s_scr[ci] = jnp.concatenate(blocks, axis=1) if seqs > 1 else blocks[0]
        for t, h in TG:
            r_scr[t * B_HEADS + h] = ret0_ref[t * seqs:(t + 1) * seqs, h].reshape(seqs * B_QK_DIM, B_V_DIM)

    row = lax.broadcasted_iota(jnp.int32, (R, R), 0)
    col = lax.broadcasted_iota(jnp.int32, (R, R), 1)
    same = (row >> log2c) == (col >> log2c)
    incl = same & (col <= row)
    m_incl = jnp.where(incl, 1.0, 0.0).astype(BF16)
    m_same = jnp.where(same, 1.0, 0.0).astype(BF16)

    def expand(t):
        if seqs == 1:
            return t
        w = t.shape[1]
        wide = jnp.concatenate([t] * seqs, axis=1)
        rr = lax.broadcasted_iota(jnp.int32, wide.shape, 0) >> log2c
        cc = lax.broadcasted_iota(jnp.int32, wide.shape, 1) // w
        return jnp.where(rr == cc, wide, 0.0)

    a_bar, r_bar, b_til, k_til, b_dec, k_dec, d_end, vv = [], [], [], [], [], [], [], []
    for t in range(nb):
        lw = lw_ref[t]
        parts = _split2(lw)
        c = sum(jnp.dot(m_incl, p, preferred_element_type=F32) for p in parts)
        if seqs == 1:
            is_last = lax.broadcasted_iota(jnp.int32, c.shape, 0) == R - 1
            cend = jnp.sum(jnp.where(is_last, c, 0.0), axis=0, keepdims=True)
        else:
            cend = sum(jnp.dot(m_same, p, preferred_element_type=F32) for p in parts)
        einv = jnp.exp(-c)
        edec = jnp.exp(cend - c)
        a_bar.append(al_ref[t] * jnp.exp(c - lw))
        r_bar.append(r_ref[t] * jnp.exp(c))
        b_til.append(be_ref[t] * einv)
        k_til.append(k_ref[t] * einv)
        b_dec.append(be_ref[t] * edec)
        k_dec.append(k_ref[t] * edec)
        d_end.append(jnp.exp(cend))
        vv.append(v_ref[t].astype(F32))
    last_row = (lax.broadcasted_iota(jnp.int32, (R, seqs * pw2), 0) & (clen - 1)) == clen - 1

    lane2 = lax.broadcasted_iota(jnp.int32, (R, pw2), 1)
    first = lane2 < hd
    row2 = lax.broadcasted_iota(jnp.int32, (R, pw2), 0)
    col2 = lane2 & (hd - 1)
    same2 = (row2 >> log2c) == (col2 >> log2c)
    incl2 = same2 & (col2 <= row2)
    strict2 = same2 & (col2 < row2)
    eye2 = jnp.where(row2 == col2, 1.0, 0.0).astype(F32)

    def pair(x, j):
        return x[:, j * pw2:(j + 1) * pw2]

    def keep(x, second):
        m = first if x.shape[1] == pw2 else jnp.concatenate([first] * (x.shape[1] // pw2), axis=1)
        return jnp.where(m != second, x, 0.0)

    def blockdiag(y, swapped=False):
        return jnp.concatenate([keep(y, swapped), keep(y, not swapped)], axis=0)

    ams = []
    for t, j in TP:
        lhs = jnp.concatenate([pair(a_bar[t], j), pair(r_bar[t], j)], axis=0)
        bt, kt = pair(b_til[t], j), pair(k_til[t], j)
        lhs0 = jnp.where(jnp.concatenate([first, first], axis=0), lhs, 0.0)
        lhs1 = jnp.where(jnp.concatenate([first, first], axis=0), 0.0, lhs)
        ams.append((_dot_nt(lhs0, jnp.concatenate([bt, kt], axis=0)), _dot_nt(lhs1, jnp.concatenate([kt, bt], axis=0))))

    lgs = [float(np.log1p(-np.exp2(-5.0 - h))) for h in range(B_HEADS)]
    qb = [qb_ref[t].astype(F32) for t in range(nb)]
    kb = [kb_ref[t].astype(F32) for t in range(nb)]
    qs = [qb[t][:, h * B_QK_DIM:(h + 1) * B_QK_DIM] for t, h in TG]
    khs = [kb[t][:, h * B_QK_DIM:(h + 1) * B_QK_DIM] for t, h in TG]
    vbs = [vb_ref[t][:, h * B_V_DIM:(h + 1) * B_V_DIM] for t, h in TG]
    diff = (row - col).astype(F32)
    pos_v = (lax.broadcasted_iota(jnp.int32, (R, B_V_DIM), 0) & (clen - 1)).astype(F32)
    pos_k = (lax.broadcasted_iota(jnp.int32, (R, B_QK_DIM), 0) & (clen - 1)).astype(F32)
    intra = [jnp.where(incl, jnp.exp(lg * diff), 0.0) for lg in lgs]
    cross = [jnp.exp(lg * (pos_v + 1.0)) for lg in lgs]
    kdec = [jnp.exp(lg * (clen - 1.0 - pos_k)) for lg in lgs]
    rstates = [r_scr[t * B_HEADS + h] for t, h in TG]
    scs = [_dot_nt(qs[i], khs[i]) * intra[h] for i, (t, h) in enumerate(TG)]
    qst = [_dot(expand(qs[i]), rstates[i]) * cross[h] for i, (t, h) in enumerate(TG)]

    n = len(TP)
    a_ab = [jnp.where(strict2, jnp.where(first, m0[:R], m1[:R]), 0.0) for m0, m1 in ams]
    a_ak = [jnp.where(strict2, jnp.where(first, m1[:R], m0[:R]), 0.0) for m0, m1 in ams]
    a_rb = [jnp.where(incl2, jnp.where(first, m0[R:], m1[R:]), 0.0) for m0, m1 in ams]
    a_rk = [jnp.where(incl2, jnp.where(first, m1[R:], m0[R:]), 0.0) for m0, m1 in ams]
    vps = [pair(vv[t], j) for t, j in TP]
    av = [_dot(a_ak[i], blockdiag(vps[i], swapped=True)) for i in range(n)]
    tinv = [eye2 + a for a in a_ab]
    if log2c > 1:
        pw = [_dot(a, blockdiag(a)) for a in a_ab]
    for it in range(log2c - 1):
        if it < log2c - 2:
            tp = [_dot(pw[i], blockdiag(jnp.concatenate([tinv[i], pw[i]], axis=1))) for i in range(n)]
            tinv = [tinv[i] + tp[i][:, :pw2] for i in range(n)]
            pw = [tp[i][:, pw2:] for i in range(n)]
        else:
            tinv = [tinv[i] + _dot(pw[i], blockdiag(tinv[i])) for i in range(n)]

    os_ = [_dot(scs[i], vbs[i]) + qst[i] for i in range(len(TG))]
    for i, (t, h) in enumerate(TG):
        r_scr[t * B_HEADS + h] = (rstates[i] * float(np.exp(lgs[h] * clen))
                                  + _dot_tn(expand(khs[i] * kdec[h]), vbs[i]))
    for t in range(nb):
        ob_o[t] = jnp.concatenate(os_[t * B_HEADS:(t + 1) * B_HEADS], axis=1)

    wu = [_dot(tinv[i], blockdiag(jnp.concatenate([pair(a_bar[t], j), av[i]], axis=1)))
          for i, (t, j) in enumerate(TP)]
    states = [s_scr[i] for i in range(n)]
    ws = [_dot_nt(jnp.concatenate([expand(wu[i][:, :pw2]), expand(pair(r_bar[t], j))], axis=0), states[i])
          for i, (t, j) in enumerate(TP)]
    us = [ws[i][:R] + wu[i][:, pw2:] for i in range(n)]
    ys = [ws[i][R:] + _dot(jnp.concatenate([a_rb[i], a_rk[i]], axis=1),
                           jnp.concatenate([blockdiag(us[i]), blockdiag(vps[i], swapped=True)], axis=0))
          for i in range(n)]
    npair = A_HEADS // 2
    for t in range(nb):
        ya_o[t] = jnp.concatenate(ys[t * npair:(t + 1) * npair], axis=1)
    rowp = lax.broadcasted_iota(jnp.int32, (pw2, seqs * pw2), 0) >= hd
    lanep = (lax.broadcasted_iota(jnp.int32, (pw2, seqs * pw2), 1) & (pw2 - 1)) >= hd
    for i, (t, j) in enumerate(TP):
        d_row = jnp.sum(jnp.where(last_row, expand(pair(d_end[t], j)), 0.0), axis=0, keepdims=True)
        bk = jnp.concatenate([expand(pair(b_dec[t], j)), expand(pair(k_dec[t], j))], axis=0)
        upd = _dot_tn(jnp.concatenate([us[i], vps[i]], axis=0), bk)
        s_scr[i] = states[i] * d_row + jnp.where(rowp == lanep, upd, 0.0)

    @pl.when(c_idx == pl.num_programs(1) - 1)
    def _():
        for ci, (t, j) in enumerate(TP):
            st = s_scr[ci]
            for i in range(seqs):
                wkv_o[t * seqs + i, 2 * j] = st[:hd, i * pw2:i * pw2 + hd]
                wkv_o[t * seqs + i, 2 * j + 1] = st[hd:, i * pw2 + hd:(i + 1) * pw2]
        for t, h in TG:
            ret_o[t * seqs:(t + 1) * seqs, h] = r_scr[t * B_HEADS + h].reshape(seqs, B_QK_DIM, B_V_DIM)


def _mixer(ops, wkv0, ret0, n_seq, seq_len):
    n = ops[0].shape[0]
    R = CHUNK_ROWS
    if seq_len >= R:
        assert seq_len % R == 0
        seqs, clen, nchunks = 1, R, seq_len // R
    else:
        assert R % seq_len == 0 and seq_len & (seq_len - 1) == 0 and n_seq % (R // seq_len) == 0
        seqs, clen, nchunks = R // seq_len, seq_len, 1
    ntiles = n_seq // seqs
    nb = MIXER_TILES if seqs == 1 else MIXER_TILES_PACKED
    assert ntiles % nb == 0
    ops3 = [a.reshape(ntiles, nchunks * R, a.shape[1]) for a in ops]

    def rows(width):
        return pl.BlockSpec((nb, R, width), lambda i, c: (i, c, 0))

    wkv_spec = pl.BlockSpec((nb * seqs, A_HEADS, A_HEAD_DIM, A_HEAD_DIM), lambda i, c: (i, 0, 0, 0))
    ret_spec = pl.BlockSpec((nb * seqs, B_HEADS, B_QK_DIM, B_V_DIM), lambda i, c: (i, 0, 0, 0))
    ya, ob, wkv1, ret1 = pl.pallas_call(
        functools.partial(_mixer_kernel, nb, seqs, clen),
        grid=(ntiles // nb, nchunks),
        in_specs=[rows(A_WIDTH)] * 6 + [rows(B_QK_WIDTH), rows(B_QK_WIDTH), rows(B_WIDTH), wkv_spec, ret_spec],
        out_specs=[rows(A_WIDTH), rows(B_WIDTH), wkv_spec, ret_spec],
        out_shape=[jax.ShapeDtypeStruct((ntiles, nchunks * R, A_WIDTH), F32),
                   jax.ShapeDtypeStruct((ntiles, nchunks * R, B_WIDTH), F32),
                   jax.ShapeDtypeStruct(wkv0.shape, F32), jax.ShapeDtypeStruct(ret0.shape, F32)],
        scratch_shapes=[pltpu.VMEM((nb * A_HEADS // 2, 2 * A_HEAD_DIM, seqs * 2 * A_HEAD_DIM), F32),
                        pltpu.VMEM((nb * B_HEADS, seqs * B_QK_DIM, B_V_DIM), F32)],
        compiler_params=pltpu.CompilerParams(dimension_semantics=("parallel", "arbitrary"),
                                             vmem_limit_bytes=V7X_VMEM_LIMIT_BYTES),
        name="mixer",
    )(*ops3, wkv0, ret0)
    return ya.reshape(n, A_WIDTH), ob.reshape(n, B_WIDTH), wkv1, ret1


def _layer_norm(z, g, b):
    mu = jnp.mean(z, axis=-1, keepdims=True)
    d = z - mu
    var = jnp.mean(d * d, axis=-1, keepdims=True)
    return d * lax.rsqrt(var + LN_EPS) * g + b


def _post_tile(ya_ref, ob_ref, bonus_ref, g_ref, gb_ref, x_ref, lnxg_ref, lnxb_ref, rgg_ref, rgb_ref,
               wout_ref, ln1g_ref, ln1b_ref, wr_ref, br_ref, h_o, gate_o, cnt_o):
    tm = x_ref.shape[0]

    def head_norm(t, group, eps, gg, bb):
        ones = _group_ones(group)
        mu = _group_sum(t, ones) * (1.0 / group)
        d = t - mu
        var = _group_sum(d * d, ones) * (1.0 / group)
        return d * lax.rsqrt(var + eps) * gg + bb

    y_a = (head_norm(ya_ref[...], A_HEAD_DIM, GN_EPS_RWKV, lnxg_ref[...], lnxb_ref[...]) + bonus_ref[...]) * g_ref[...]
    y_b = head_norm(ob_ref[...], B_V_DIM, GN_EPS, rgg_ref[...], rgb_ref[...]) * gb_ref[...]
    y = jnp.concatenate([y_a, y_b], axis=1)
    mix = _dot(y, wout_ref[...])
    h = _layer_norm(DEEPNORM_ALPHA * x_ref[...] + mix, ln1g_ref[...], ln1b_ref[...])
    h_o[...] = h

    h_hi, h_lo = _split2(h)
    w_hi, w_lo = _split2(wr_ref[...])
    hi_terms = _dot_nt(jnp.concatenate([w_hi, w_lo], axis=0), h_hi)
    logits = (hi_terms[:ROUTE_ROWS] + hi_terms[ROUTE_ROWS:] + _dot_nt(w_hi, h_lo)) + br_ref[...]
    def first_max(vals):
        best = functools.reduce(jnp.maximum, vals)
        idx = float(len(vals) - 1)
        for j in range(len(vals) - 2, -1, -1):
            idx = jnp.where(vals[j] == best, float(j), idx)
        return best, idx

    coarse = [logits[g:g + 1] for g in range(N_GROUPS)]
    cmax, grp = first_max(coarse)
    gprob = 1.0 / sum(jnp.exp(c - cmax) for c in coarse)
    fine = []
    for e in range(EXPERTS_PER_GROUP):
        f = logits[FINE_LANE0 + e:FINE_LANE0 + e + 1]
        for g in range(1, N_GROUPS):
            r = FINE_LANE0 + g * EXPERTS_PER_GROUP + e
            f = jnp.where(grp == float(g), logits[r:r + 1], f)
        fine.append(f)
    m1, i1 = first_max(fine)
    m2, i2 = first_max([jnp.where(i1 == float(e), -jnp.inf, f) for e, f in enumerate(fine)])
    e2 = jnp.exp(m2 - m1)
    w1 = gprob / (1.0 + e2)
    w2 = gprob * e2 / (1.0 + e2)
    row = lax.broadcasted_iota(jnp.int32, (ROUTE_ROWS, tm), 0).astype(F32)
    row0 = FINE_LANE0 + grp * EXPERTS_PER_GROUP
    gate_t = (jnp.where(row == row0 + i1, w1, 0.0) + jnp.where(row == row0 + i2, w2, 0.0)
              + jnp.where(row == GROUP_LANE, grp, 0.0))
    gate = jnp.concatenate([gate_t, jnp.zeros((ROUTER_LANES - ROUTE_ROWS, tm), F32)], axis=0).T
    gate_o[...] = gate
    lane = lax.broadcasted_iota(jnp.int32, (tm, ROUTER_LANES), 1).astype(F32)
    onehot = jnp.where((lane == gate[:, GROUP_LANE:GROUP_LANE + 1]) & (lane < N_GROUPS), 1.0, 0.0)
    cnt_o[0] = jnp.broadcast_to(jnp.sum(onehot, axis=0, keepdims=True), (V7X_SUBLANES, ROUTER_LANES)).astype(jnp.int32)


POST_OPERANDS = 6


def _post_kernel(group_tiles, *refs):
    per_group, shared = refs[:POST_OPERANDS * len(group_tiles)], refs[POST_OPERANDS * len(group_tiles):]
    i = pl.program_id(0)
    start = 0
    for k, tiles in enumerate(group_tiles):
        @pl.when((i >= start) & (i < start + tiles))
        def _(k=k):
            _post_tile(*per_group[POST_OPERANDS * k:POST_OPERANDS * (k + 1)], *shared)

        start += tiles


def _group_rows(tm, width, start, tiles):
    return pl.BlockSpec((tm, width), lambda i, *_: (jnp.clip(i - start, 0, tiles - 1), 0))


def _post(groups, W):
    tm = POST_ROWS
    group_tiles = tuple(g[-1].shape[0] // tm for g in groups)
    assert all(g[-1].shape[0] % tm == 0 for g in groups)
    n = tm * sum(group_tiles)

    def full(a):
        return pl.BlockSpec(a.shape, lambda i: (0,) * a.ndim)

    def rows(width):
        return pl.BlockSpec((tm, width), lambda i: (i, 0))

    widths = [A_WIDTH, B_WIDTH, A_WIDTH, A_WIDTH, B_WIDTH, D_MODEL]
    in_specs, start = [], 0
    for tiles in group_tiles:
        in_specs += [_group_rows(tm, w, start, tiles) for w in widths]
        start += tiles
    params = [W["lnx_g"], W["lnx_b"], W["ret_gn_g"], W["ret_gn_b"], W["w_out"], W["ln1_g"], W["ln1_b"],
              W["w_router"], W["b_router"]]
    return pl.pallas_call(
        functools.partial(_post_kernel, group_tiles),
        grid=(n // tm,),
        in_specs=in_specs + [full(a) for a in params],
        out_specs=[rows(D_MODEL), rows(ROUTER_LANES),
                   pl.BlockSpec((1, V7X_SUBLANES, ROUTER_LANES), lambda i: (i, 0, 0))],
        out_shape=[jax.ShapeDtypeStruct((n, D_MODEL), F32), jax.ShapeDtypeStruct((n, ROUTER_LANES), F32),
                   jax.ShapeDtypeStruct((n // tm, V7X_SUBLANES, ROUTER_LANES), jnp.int32)],
        compiler_params=pltpu.CompilerParams(dimension_semantics=("parallel",),
                                             vmem_limit_bytes=V7X_VMEM_LIMIT_BYTES),
        name="post",
    )(*[a for g in groups for a in g], *params)


def _sort_positions(gate, lofs):
    tm = gate.shape[0]
    lane = lax.broadcasted_iota(jnp.int32, (tm, ROUTER_LANES), 1)
    grp = gate[:, GROUP_LANE:GROUP_LANE + 1].astype(jnp.int32)
    onehot = jnp.where((lane == grp) & (lane < N_GROUPS), 1.0, 0.0)
    r = lax.broadcasted_iota(jnp.int32, (tm, tm), 0)
    c = lax.broadcasted_iota(jnp.int32, (tm, tm), 1)
    earlier = jnp.where(c < r, 1.0, 0.0).astype(BF16)
    prefix = jnp.dot(earlier, onehot.astype(BF16), preferred_element_type=F32)
    base = jnp.zeros((tm, ROUTER_LANES), F32)
    for g in range(N_GROUPS):
        base = jnp.where(lane == g, lofs[g].astype(F32), base)
    return jnp.sum(onehot * (base + prefix), axis=1, keepdims=True).astype(jnp.int32)


def _piece_copies(action, rows, bits, copy_of):
    k = rows // SORT_ALIGN
    for b in reversed(range(bits)):
        size = SORT_ALIGN << b

        @pl.when(((k >> b) & 1) == 1)
        def _():
            done = ((k >> (b + 1)) << (b + 1)) * SORT_ALIGN
            cp = copy_of(done, size)
            cp.start() if action == "start" else cp.wait()


def _run_copies(action, plan_ref, i, src_of, dst_of, sem):
    for g in range(N_GROUPS):
        goff = plan_ref[i, g]
        lofs = plan_ref[i, 2 * N_GROUPS + g]

        def copy_of(done, size, goff=goff, lofs=lofs):
            lo = pl.multiple_of(lofs + done, SORT_ALIGN)
            go = pl.multiple_of(goff + done, SORT_ALIGN)
            return pltpu.make_async_copy(src_of(lo, go, size), dst_of(lo, go, size), sem)

        _piece_copies(action, plan_ref[i, N_GROUPS + g], RUN_BITS, copy_of)


def _dispatch_kernel(plan_ref, gap_ref, h_ref, gate_ref, hs_o, h_loc, sem):
    i = pl.program_id(0)
    tm = h_ref.shape[0]
    lofs = [plan_ref[i, 2 * N_GROUPS + g] for g in range(N_GROUPS)]
    pos = _sort_positions(gate_ref[...], lofs)
    onehot_t = jnp.where(lax.broadcasted_iota(jnp.int32, (tm, SORT_LOCAL), 1) == pos, 1.0, 0.0).astype(BF16)
    idx = lax.broadcasted_iota(jnp.int32, (V7X_SUBLANES, SORT_LOCAL), 1)
    pos_row = (_dot_nt((idx // V7X_LANES).astype(F32), onehot_t) * float(V7X_LANES)
               + _dot_nt((idx % V7X_LANES).astype(F32), onehot_t))[0:1].astype(jnp.int32)
    perm = jnp.where(lax.broadcasted_iota(jnp.int32, (SORT_LOCAL, tm), 0) == pos_row, 1.0, 0.0).astype(BF16)
    payload = jnp.concatenate([h_ref[...].astype(BF16)] + list(_split3(gate_ref[...])), axis=1)
    sorted_rows = jnp.dot(perm, payload, preferred_element_type=F32).astype(BF16)

    def copies(action, tile):
        _run_copies(action, plan_ref, tile, lambda lo, go, sz: h_loc.at[pl.ds(lo, sz)],
                    lambda lo, go, sz: hs_o.at[pl.ds(go, sz)], sem)

    @pl.when(i > 0)
    def _():
        copies("wait", i - 1)

    h_loc[...] = sorted_rows
    copies("start", i)

    @pl.when(i == pl.num_programs(0) - 1)
    def _():
        copies("wait", i)
        h_loc[...] = jnp.zeros_like(h_loc)
        for action in ("start", "wait"):
            for g in range(N_GROUPS):
                def copy_of(done, size, g=g):
                    go = pl.multiple_of(gap_ref[g] + done, SORT_ALIGN)
                    return pltpu.make_async_copy(h_loc.at[pl.ds(0, size)], hs_o.at[pl.ds(go, size)], sem)

                _piece_copies(action, gap_ref[N_GROUPS + g], GAP_BITS, copy_of)

            def body(k, carry):
                go = pl.multiple_of(gap_ref[2 * N_GROUPS] + k * EXPERT_ROWS_SHORT, SORT_ALIGN)
                cp = pltpu.make_async_copy(h_loc.at[pl.ds(0, EXPERT_ROWS_SHORT)], hs_o.at[pl.ds(go, EXPERT_ROWS_SHORT)], sem)
                cp.start() if action == "start" else cp.wait()
                return carry

            lax.fori_loop(0, gap_ref[2 * N_GROUPS + 1] // EXPERT_ROWS_SHORT, body, 0)


def _experts_kernel(tile_group_ref, n_valid_ref, hs_ref, w1_ref, w3_ref, w2_ref, ys_o, w1_b, w3_b, w2_b):
    j = pl.program_id(0)
    last = n_valid_ref[0] - 1
    g = tile_group_ref[jnp.minimum(j, last)]
    g_prev = tile_group_ref[jnp.minimum(jnp.maximum(j - 1, 0), last)]

    @pl.when((j == 0) | (g != g_prev))
    def _():
        w1_b[...] = w1_ref[...].astype(BF16)
        w3_b[...] = w3_ref[...].astype(BF16)
        w2_b[...] = w2_ref[...].astype(BF16)

    @pl.when(j <= last)
    def _():
        x = hs_ref[:, :D_MODEL]
        gs = sum(hs_ref[:, D_MODEL + t * ROUTER_LANES:D_MODEL + (t + 1) * ROUTER_LANES].astype(F32) for t in range(3))
        lane = lax.broadcasted_iota(jnp.int32, gs.shape, 1)
        acc = jnp.zeros(ys_o.shape, F32)
        for e in range(EXPERTS_PER_GROUP):
            ge = jnp.sum(jnp.where(lane == FINE_LANE0 + g * EXPERTS_PER_GROUP + e, gs, 0.0), axis=-1, keepdims=True)
            a = jnp.dot(x, w1_b[e], preferred_element_type=F32)
            b = jnp.dot(x, w3_b[e], preferred_element_type=F32)
            hid = (a * _sigmoid(a)) * b * ge
            acc = acc + jnp.dot(hid.astype(BF16), w2_b[e], preferred_element_type=F32)
        ys_o[...] = acc.astype(ys_o.dtype)

    @pl.when(j > last)
    def _():
        ys_o[...] = jnp.zeros_like(ys_o)


def _combine_kernel(group_tiles, plan_ref, h_ref, gate_ref, *refs):
    ng = len(group_tiles)
    p_refs, (ys_ref, ln2g_ref, ln2b_ref, wple_ref, wpg_ref, pleg_ref) = refs[:ng], refs[ng:ng + 6]
    o_refs, (y_loc, sem) = refs[ng + 6:2 * ng + 6], refs[2 * ng + 6:]
    i = pl.program_id(0)
    starts = [sum(group_tiles[:k]) for k in range(ng)]

    def in_group(k):
        return (i >= starts[k]) & (i < starts[k] + group_tiles[k])

    tm = h_ref.shape[0]
    lofs = [plan_ref[i, 2 * N_GROUPS + g] for g in range(N_GROUPS)]

    def fetch(tile):
        slot = tile % 2
        y_loc[slot] = jnp.zeros(y_loc.shape[1:], y_loc.dtype)
        _run_copies("start", plan_ref, tile, lambda lo, go, sz: ys_ref.at[pl.ds(go, sz)],
                    lambda lo, go, sz: y_loc.at[slot, pl.ds(lo, sz)], sem.at[slot])

    @pl.when(i == 0)
    def _():
        fetch(i)

    @pl.when(i + 1 < pl.num_programs(0))
    def _():
        fetch(i + 1)

    pos = _sort_positions(gate_ref[...], lofs)
    onehot_t = jnp.where(lax.broadcasted_iota(jnp.int32, (tm, SORT_LOCAL), 1) == pos, 1.0, 0.0).astype(BF16)
    p_tile = p_refs[0][...]
    for k in range(1, ng):
        p_tile = jnp.where(in_group(k), p_refs[k][...], p_tile)
    ple_in = _dot(p_tile, wple_ref[...])
    slot = i % 2
    _run_copies("wait", plan_ref, i, lambda lo, go, sz: ys_ref.at[pl.ds(go, sz)],
                lambda lo, go, sz: y_loc.at[slot, pl.ds(lo, sz)], sem.at[slot])
    ffn = jnp.dot(onehot_t, y_loc[slot], preferred_element_type=F32)
    h2 = _layer_norm(DEEPNORM_ALPHA * h_ref[...] + ffn, ln2g_ref[...], ln2b_ref[...])
    ple = ple_in * _sigmoid(_dot(h2, wpg_ref[...]))
    ms = jnp.mean(ple * ple, axis=-1, keepdims=True)
    out = h2 + ple * lax.rsqrt(ms + LN_EPS) * pleg_ref[...]
    for k in range(ng):
        @pl.when(in_group(k))
        def _(k=k):
            o_refs[k][...] = out


def _ffn(h, gate, counts, p_groups, W):
    n = h.shape[0]
    tm = SORT_ROWS
    assert n % tm == 0 and tm == POST_ROWS and SORT_LOCAL >= tm + N_GROUPS * SORT_ALIGN
    ntiles = n // tm
    er = EXPERT_ROWS if n >= 2 * N_GROUPS * EXPERT_ROWS else EXPERT_ROWS_SHORT
    assert er <= SORT_ALIGN << GAP_BITS and SORT_ALIGN << (GAP_BITS - 1) <= SORT_LOCAL
    cnt = counts[:, 0, :N_GROUPS]
    run = (cnt + SORT_ALIGN - 1) // SORT_ALIGN * SORT_ALIGN
    lofs = jnp.cumsum(run, axis=1) - run
    seg = (jnp.sum(run, axis=0) + er - 1) // er * er
    gbase = jnp.cumsum(seg) - seg
    goff = gbase[None, :] + jnp.cumsum(run, axis=0) - run
    plan = jnp.concatenate([goff, run, lofs], axis=1).astype(jnp.int32)
    max_tiles = (n + ntiles * N_GROUPS * (SORT_ALIGN - 1)) // er + N_GROUPS
    cap = max_tiles * er
    n_valid = (jnp.sum(seg) // er).astype(jnp.int32).reshape(1)
    tile_start = jnp.arange(max_tiles, dtype=jnp.int32) * er
    tile_group = jnp.clip(jnp.sum(tile_start[:, None] >= (gbase + seg)[None, :], axis=1), 0, N_GROUPS - 1).astype(jnp.int32)

    cparams = dict(vmem_limit_bytes=V7X_VMEM_LIMIT_BYTES)
    any_spec = pl.BlockSpec(memory_space=pl.ANY)
    total = jnp.sum(run, axis=0)
    assert er % EXPERT_ROWS_SHORT == 0 and EXPERT_ROWS_SHORT <= SORT_LOCAL
    used = jnp.sum(seg)
    gaps = jnp.concatenate([gbase + total, seg - total, jnp.stack([used, cap - used])]).astype(jnp.int32)
    hs = pl.pallas_call(
        _dispatch_kernel,
        grid_spec=pltpu.PrefetchScalarGridSpec(
            num_scalar_prefetch=2, grid=(ntiles,),
            in_specs=[pl.BlockSpec((tm, D_MODEL), lambda i, plan, gaps: (i, 0)),
                      pl.BlockSpec((tm, ROUTER_LANES), lambda i, plan, gaps: (i, 0))],
            out_specs=any_spec,
            scratch_shapes=[pltpu.VMEM((SORT_LOCAL, SORTED_WIDTH), BF16), pltpu.SemaphoreType.DMA(())]),
        out_shape=jax.ShapeDtypeStruct((cap, SORTED_WIDTH), BF16),
        compiler_params=pltpu.CompilerParams(dimension_semantics=("arbitrary",), **cparams),
        name="dispatch",
    )(plan, gaps, h, gate)

    def tile_rows(width):
        return pl.BlockSpec((er, width), lambda j, tg, nv: (jnp.minimum(j, nv[0] - 1), 0))

    def group_w(shape):
        return pl.BlockSpec((EXPERTS_PER_GROUP,) + shape, lambda j, tg, nv: (tg[jnp.minimum(j, nv[0] - 1)], 0, 0))

    ys = pl.pallas_call(
        _experts_kernel,
        grid_spec=pltpu.PrefetchScalarGridSpec(
            num_scalar_prefetch=2, grid=(max_tiles,),
            in_specs=[tile_rows(SORTED_WIDTH), group_w((D_MODEL, D_EXPERT)),
                      group_w((D_MODEL, D_EXPERT)), group_w((D_EXPERT, D_MODEL))],
            out_specs=pl.BlockSpec((er, D_MODEL), lambda j, tg, nv: (j, 0)),
            scratch_shapes=[pltpu.VMEM((EXPERTS_PER_GROUP, D_MODEL, D_EXPERT), BF16),
                            pltpu.VMEM((EXPERTS_PER_GROUP, D_MODEL, D_EXPERT), BF16),
                            pltpu.VMEM((EXPERTS_PER_GROUP, D_EXPERT, D_MODEL), BF16)]),
        out_shape=jax.ShapeDtypeStruct((cap, D_MODEL), BF16),
        compiler_params=pltpu.CompilerParams(dimension_semantics=("arbitrary",), **cparams),
        name="experts",
    )(tile_group, n_valid, hs, W["expert_w1"], W["expert_w3"], W["expert_w2"])

    def full(a):
        return pl.BlockSpec(a.shape, lambda i, plan: (0,) * a.ndim)

    params = [W["ln2_g"], W["ln2_b"], W["w_ple"], W["w_ple_gate"], W["ple_norm_g"]]
    group_tiles = tuple(pg.shape[0] // tm for pg in p_groups)
    assert sum(group_tiles) == ntiles
    starts = [sum(group_tiles[:k]) for k in range(len(group_tiles))]
    return pl.pallas_call(
        functools.partial(_combine_kernel, group_tiles),
        grid_spec=pltpu.PrefetchScalarGridSpec(
            num_scalar_prefetch=1, grid=(ntiles,),
            in_specs=[pl.BlockSpec((tm, D_MODEL), lambda i, plan: (i, 0)),
                      pl.BlockSpec((tm, ROUTER_LANES), lambda i, plan: (i, 0))]
            + [_group_rows(tm, D_PLE, st, t) for st, t in zip(starts, group_tiles)]
            + [any_spec] + [full(a) for a in params],
            out_specs=[_group_rows(tm, D_MODEL, st, t) for st, t in zip(starts, group_tiles)],
            scratch_shapes=[pltpu.VMEM((2, SORT_LOCAL, D_MODEL), BF16), pltpu.SemaphoreType.DMA((2,))]),
        out_shape=[jax.ShapeDtypeStruct((t * tm, D_MODEL), F32) for t in group_tiles],
        compiler_params=pltpu.CompilerParams(dimension_semantics=("arbitrary",), **cparams),
        name="combine",
    )(plan, h, gate, *p_groups, ys, *params)


def _mix(x, x_prev, wkv0, ret0, pos0, W):
    n_seq, seq_len, _ = x.shape
    x2 = x.reshape(n_seq * seq_len, D_MODEL)
    r, lw, k, v, al, be, g, bonus, qb, kb, vb, gb = _proj(x2, x_prev, seq_len, pos0, W)
    ya, ob, wkv1, ret1 = _mixer((r, lw, k, v, al, be, qb, kb, vb), wkv0, ret0, n_seq, seq_len)
    return (ya, ob, bonus, g, gb, x2), wkv1, ret1


def _prep_weights(i, w_in, mu_shift, w_decay_up, decay_base, w_aaa_up, aaa_base, w_gate_up, k_k, k_a, r_k,
                  lnx_g, lnx_b, ret_gn_g, ret_gn_b, w_out, ln1_g, ln1_b,
                  router_coarse_w, router_coarse_b, router_fine_w, router_fine_b,
                  expert_w1, expert_w3, expert_w2, ln2_g, ln2_b, w_ple, w_ple_gate, ple_norm_g):
    def row(a):
        return a[i].reshape(1, -1).astype(F32)

    pad = ROUTE_ROWS - N_GROUPS - N_EXPERTS
    w_router = jnp.concatenate([router_coarse_w[i].T, router_fine_w[i].T, jnp.zeros((pad, D_MODEL), F32)], axis=0)
    b_router = jnp.concatenate([router_coarse_b[i], router_fine_b[i], jnp.zeros((pad,), F32)]).reshape(-1, 1)
    return {
        "w_in": w_in[i].astype(BF16), "mu_shift": row(mu_shift), "w_decay_up": w_decay_up[i].astype(BF16),
        "decay_base": row(decay_base), "w_aaa_up": w_aaa_up[i].astype(BF16), "aaa_base": row(aaa_base),
        "w_gate_up": w_gate_up[i].astype(BF16), "k_k": row(k_k), "k_a": row(k_a), "r_k": row(r_k),
        "lnx_g": row(lnx_g), "lnx_b": row(lnx_b), "ret_gn_g": row(ret_gn_g), "ret_gn_b": row(ret_gn_b),
        "w_out": w_out[i].astype(BF16), "ln1_g": row(ln1_g), "ln1_b": row(ln1_b),
        "w_router": w_router, "b_router": b_router,
        "expert_w1": expert_w1[i], "expert_w3": expert_w3[i], "expert_w2": expert_w2[i], "ln2_g": row(ln2_g), "ln2_b": row(ln2_b),
        "w_ple": w_ple[i].astype(BF16), "w_ple_gate": w_ple_gate[i].astype(BF16), "ple_norm_g": row(ple_norm_g),
    }


def kernel(x_prompt, x_sample, p_prompt, p_sample, state_wkv, state_shift, state_ret, w_in, mu_shift, w_decay_up, decay_base, w_aaa_up, aaa_base, w_gate_up, k_k, k_a, r_k, lnx_g, lnx_b, ret_gn_g, ret_gn_b, w_out, ln1_g, ln1_b, router_coarse_w, router_coarse_b, router_fine_w, router_fine_b, expert_w1, expert_w3, expert_w2, ln2_g, ln2_b, w_ple, w_ple_gate, ple_norm_g):
    yp, ys = x_prompt, x_sample
    nb = x_prompt.shape[0]
    depth = w_in.shape[0]
    wkv_p, shift_p, ret_p, wkv_s, shift_s, ret_s = [], [], [], [], [], []
    for i in range(depth):
        W = _prep_weights(i, w_in, mu_shift, w_decay_up, decay_base, w_aaa_up, aaa_base, w_gate_up, k_k, k_a, r_k,
                          lnx_g, lnx_b, ret_gn_g, ret_gn_b, w_out, ln1_g, ln1_b,
                          router_coarse_w, router_coarse_b, router_fine_w, router_fine_b,
                          expert_w1, expert_w3, expert_w2, ln2_g, ln2_b, w_ple, w_ple_gate, ple_norm_g)
        ops_p, wp, rp = _mix(yp, jnp.zeros((nb, D_MODEL), F32), jnp.zeros((nb, A_HEADS, A_HEAD_DIM, A_HEAD_DIM), F32),
                             jnp.zeros((nb, B_HEADS, B_QK_DIM, B_V_DIM), F32), 0, W)
        ops_s, wsm, rsm = _mix(ys, state_shift[i], state_wkv[i], state_ret[i], PAST_LEN, W)
        sp, ss = yp[:, -1], ys[:, -1]
        h, gate, counts = _post([ops_p, ops_s], W)
        out_p, out_s = _ffn(h, gate, counts, [p_prompt[i].reshape(-1, D_PLE), p_sample[i].reshape(-1, D_PLE)], W)
        yp, ys = out_p.reshape(yp.shape), out_s.reshape(ys.shape)
        wkv_p.append(wp); shift_p.append(sp); ret_p.append(rp)
        wkv_s.append(wsm); shift_s.append(ss); ret_s.append(rsm)
    return (yp, ys, jnp.stack(wkv_p, 0), jnp.stack(shift_p, 0), jnp.stack(ret_p, 0),
            jnp.stack(wkv_s, 0), jnp.stack(shift_s, 0), jnp.stack(ret_s, 0))
```

```python
import functools
import math

import numpy as np
import jax
import jax.numpy as jnp
from jax import lax
from jax.experimental import pallas as pl
from jax.experimental.pallas import tpu as pltpu

F32 = jnp.float32
BF16 = jnp.bfloat16

D_MODEL = 1024
D_PLE = 256
A_HEADS = 8
A_HEAD_DIM = 64
A_WIDTH = A_HEADS * A_HEAD_DIM
DECAY_LORA = 64
AAA_LORA = 64
GATE_LORA = 128
GN_EPS_RWKV = 64e-5
B_HEADS = 4
B_QK_DIM = 64
B_V_DIM = 128
B_QK_WIDTH = B_HEADS * B_QK_DIM
B_WIDTH = B_HEADS * B_V_DIM
ROPE_BASE = 10000.0
GN_EPS = 1e-5
SHIFT_WIDTH = 3 * A_WIDTH + DECAY_LORA + AAA_LORA + GATE_LORA
IN_WIDTH = SHIFT_WIDTH + 2 * B_QK_WIDTH + 2 * B_WIDTH
N_GROUPS = 4
EXPERTS_PER_GROUP = 4
N_EXPERTS = N_GROUPS * EXPERTS_PER_GROUP
D_EXPERT = 256
DEPTH = 1
PAST_LEN = 16384
DEEPNORM_ALPHA = (2 * DEPTH) ** 0.25
LN_EPS = 1e-5

V7X_LANES = 128
V7X_SUBLANES = 8
V7X_MXU_WIDTH = 256
V7X_VMEM_LIMIT_BYTES = 56 * 1024 * 1024

PROJ_ROWS = 1024
LOG_DECAY_OUT = 1
CHUNK_ROWS = 64
MIXER_TILES = 8
MIXER_TILES_PACKED = 2
POST_ROWS = 512
SORT_ROWS = 512
SORT_ALIGN = 16
SORT_LOCAL = 640
EXPERT_ROWS = 1024
EXPERT_ROWS_SHORT = 256
GAP_BITS = 6
RUN_BITS = 6
GROUP_LANE = 0
ROUTER_LANES = V7X_LANES
ROUTE_ROWS = 32
FINE_LANE0 = N_GROUPS
SORTED_WIDTH = D_MODEL + 3 * ROUTER_LANES


def _dot(a, b):
    return jnp.dot(a.astype(BF16), b.astype(BF16), preferred_element_type=F32)


def _dot_nt(a, b):
    return lax.dot_general(a.astype(BF16), b.astype(BF16), (((1,), (1,)), ((), ())), preferred_element_type=F32)


def _dot_tn(a, b):
    return lax.dot_general(a.astype(BF16), b.astype(BF16), (((0,), (0,)), ((), ())), preferred_element_type=F32)


def _split2(x):
    hi = x.astype(BF16)
    lo = (x - hi.astype(F32)).astype(BF16)
    return hi, lo


def _split3(x):
    hi = x.astype(BF16)
    r1 = x - hi.astype(F32)
    mid = r1.astype(BF16)
    lo = (r1 - mid.astype(F32)).astype(BF16)
    return hi, mid, lo


def _sigmoid(x):
    return 1.0 / (1.0 + jnp.exp(-x))


def _group_ones(group):
    assert V7X_MXU_WIDTH % group == 0
    r = lax.broadcasted_iota(jnp.int32, (V7X_MXU_WIDTH, V7X_MXU_WIDTH), 0) // group
    c = lax.broadcasted_iota(jnp.int32, (V7X_MXU_WIDTH, V7X_MXU_WIDTH), 1) // group
    return jnp.where(r == c, 1.0, 0.0).astype(BF16)


def _group_sum(x, ones):
    xb, w = x.astype(BF16), ones.shape[0]
    assert x.shape[1] % w == 0
    return jnp.concatenate([jnp.dot(xb[:, j:j + w], ones, preferred_element_type=F32)
                            for j in range(0, x.shape[1], w)], axis=1)


def _proj_kernel(carry_mode, seq_len, tiles_per_seq,
                 x_ref, xp_ref, w_ref, mu_ref, wdec_ref, dbase_ref, waaa_ref, abase_ref, wgate_ref,
                 kk_ref, ka_ref, rk_ref, cos_ref, sin_ref,
                 r_o, lw_o, k_o, v_o, al_o, be_o, g_o, bonus_o, qb_o, kb_o, vb_o, gb_o,
                 carry_scr):
    tm = x_ref.shape[0]
    if carry_mode:
        xp = jnp.broadcast_to(xp_ref[0], (V7X_SUBLANES, D_MODEL))
        xb = jnp.concatenate([x_ref[...], xp], axis=0).astype(BF16)
        j = pl.program_id(0) % tiles_per_seq

        @pl.when(pl.program_id(0) == 0)
        def _():
            carry_scr[...] = jnp.zeros_like(carry_scr)
    else:
        xb = x_ref[...].astype(BF16)
        xpb = xp_ref[...].astype(BF16)

    def project(lo, hi):
        return jnp.dot(xb, w_ref[:, lo:hi], preferred_element_type=F32)

    def shifted(p, lo, hi):
        cur = p[:tm]
        row = lax.broadcasted_iota(jnp.int32, cur.shape, 0)
        rolled = pltpu.roll(cur, 1, 0)
        if carry_mode:
            first = jnp.where(j == 0, p[tm + V7X_SUBLANES - 1:], carry_scr[V7X_SUBLANES - 1:V7X_SUBLANES, lo:hi])
            prev = jnp.where(row == 0, first, rolled)
            carry_scr[:, lo:hi] = cur[tm - V7X_SUBLANES:]
        else:
            first = jnp.dot(xpb, w_ref[:, lo:hi], preferred_element_type=F32)
            prev = jnp.where((row & (seq_len - 1)) == 0, first, rolled)
        return cur + (prev - cur) * mu_ref[:, lo:hi]

    c_r, c_k, c_v, c_l = 0, A_WIDTH, 2 * A_WIDTH, 3 * A_WIDTH
    p_lora = project(c_l, SHIFT_WIDTH)
    p_k = project(c_k, c_v)
    p_r = project(c_r, c_k)

    lora = shifted(p_lora, c_l, SHIFT_WIDTH)
    w_lo = lora[:, :DECAY_LORA]
    a_lo = lora[:, DECAY_LORA:DECAY_LORA + AAA_LORA]
    g_lo = lora[:, DECAY_LORA + AAA_LORA:]
    z = -(dbase_ref[...] + _dot(jnp.tanh(w_lo), wdec_ref[...]))
    softplus = jnp.maximum(z, 0.0) + jnp.log(1.0 + jnp.exp(-jnp.abs(z)))
    log_w = -softplus - 0.5
    lw_o[...] = -jnp.exp(log_w)
    a = _sigmoid(abase_ref[...] + _dot(a_lo, waaa_ref[...]))
    g_o[...] = (_dot(_sigmoid(g_lo), wgate_ref[...])).astype(g_o.dtype)

    p_v = project(c_v, c_l)

    ones64 = _group_ones(A_HEAD_DIM)
    k0 = shifted(p_k, c_k, c_v)
    kk0 = k0 * kk_ref[...]
    ssq = _group_sum(kk0 * kk0, ones64)
    kk = kk0 * jnp.minimum(lax.rsqrt(ssq), 1e12)
    k = k0 * (1.0 + (a - 1.0) * ka_ref[...])
    k_o[...] = (k).astype(k_o.dtype)
    al_o[...] = (-kk).astype(al_o.dtype)
    be_o[...] = (kk * a).astype(be_o.dtype)

    o = SHIFT_WIDTH
    p_qk = project(o, o + 2 * B_QK_WIDTH)[:tm]

    r = shifted(p_r, c_r, c_k)
    r_o[...] = (r).astype(r_o.dtype)
    rk_sum = _group_sum(r * k * rk_ref[...], ones64)

    p_vb = project(o + 2 * B_QK_WIDTH, o + 2 * B_QK_WIDTH + B_WIDTH)[:tm]

    v = shifted(p_v, c_v, c_l)
    v_o[...] = (v).astype(v_o.dtype)
    bonus_o[...] = (rk_sum * v).astype(bonus_o.dtype)

    p_gb = project(o + 2 * B_QK_WIDTH + B_WIDTH, IN_WIDTH)[:tm]

    q_b = p_qk[:, :B_QK_WIDTH]
    k_b = p_qk[:, B_QK_WIDTH:]
    lane = lax.broadcasted_iota(jnp.int32, (tm, B_QK_WIDTH), 1)
    first_half = (lane & (B_QK_DIM - 1)) < (B_QK_DIM // 2)
    cos = cos_ref[...]
    sin = sin_ref[...]

    def rot(t):
        swapped = jnp.where(first_half, pltpu.roll(t, B_QK_WIDTH - B_QK_DIM // 2, 1), pltpu.roll(t, B_QK_DIM // 2, 1))
        return t * cos + swapped * sin

    qb_o[...] = (rot(q_b)).astype(qb_o.dtype)
    kb_o[...] = (rot(k_b) * (B_QK_DIM ** -0.5)).astype(kb_o.dtype)

    vb_o[...] = (p_vb).astype(vb_o.dtype)
    gb_o[...] = (p_gb * _sigmoid(p_gb)).astype(gb_o.dtype)


def _proj(x2, x_prev, seq_len, pos0, W):
    n = x2.shape[0]
    tm = PROJ_ROWS
    assert n % tm == 0
    carry_mode = seq_len % tm == 0
    if carry_mode:
        tiles_per_seq = seq_len // tm
        xp = x_prev.reshape(-1, 1, D_MODEL)
        xp_spec = pl.BlockSpec((1, 1, D_MODEL), lambda i: (i // tiles_per_seq, 0, 0))
        tab_rows = seq_len
    else:
        assert tm % seq_len == 0 and seq_len & (seq_len - 1) == 0
        tiles_per_seq = 1
        xp = jnp.repeat(x_prev, seq_len, axis=0)
        xp_spec = pl.BlockSpec((tm, D_MODEL), lambda i: (i, 0))
        tab_rows = tm
    half = B_QK_DIM // 2
    inv = ROPE_BASE ** (-jnp.arange(half, dtype=F32) / half)
    pos = (pos0 + jnp.arange(seq_len, dtype=jnp.int32)).astype(F32)
    ang = pos[:, None] * inv[None, :]
    cos = jnp.tile(jnp.concatenate([jnp.cos(ang), jnp.cos(ang)], -1), (tab_rows // seq_len, B_HEADS))
    sin = jnp.tile(jnp.concatenate([-jnp.sin(ang), jnp.sin(ang)], -1), (tab_rows // seq_len, B_HEADS))
    tab_tiles = tab_rows // tm
    tab_spec = pl.BlockSpec((tm, B_QK_WIDTH), lambda i: (i % tab_tiles, 0))

    def full(a):
        return pl.BlockSpec(a.shape, lambda i: (0,) * a.ndim)

    def rows(width):
        return pl.BlockSpec((tm, width), lambda i: (i, 0))

    params = [W["w_in"], W["mu_shift"], W["w_decay_up"], W["decay_base"], W["w_aaa_up"], W["aaa_base"],
              W["w_gate_up"], W["k_k"], W["k_a"], W["r_k"]]
    widths = [A_WIDTH] * 8 + [B_QK_WIDTH, B_QK_WIDTH, B_WIDTH, B_WIDTH]
    outs = pl.pallas_call(
        functools.partial(_proj_kernel, carry_mode, seq_len, tiles_per_seq),
        grid=(n // tm,),
        in_specs=[rows(D_MODEL), xp_spec, pl.BlockSpec(params[0].shape, lambda i: (0, 0), pipeline_mode=pl.Buffered(1))]
        + [full(a) for a in params[1:]] + [tab_spec, tab_spec],
        out_specs=[rows(w) for w in widths],
        out_shape=[jax.ShapeDtypeStruct((n, w), F32 if i == LOG_DECAY_OUT else BF16) for i, w in enumerate(widths)],
        scratch_shapes=[pltpu.VMEM((V7X_SUBLANES, SHIFT_WIDTH), F32)],
        compiler_params=pltpu.CompilerParams(dimension_semantics=("arbitrary",),
                                             vmem_limit_bytes=V7X_VMEM_LIMIT_BYTES),
        name="proj",
    )(x2, xp, *params, cos, sin)
    return outs


def _mixer_kernel(nb, seqs, clen,
                  r_ref, lw_ref, k_ref, v_ref, al_ref, be_ref, qb_ref, kb_ref, vb_ref, wkv0_ref, ret0_ref,
                  ya_o, ob_o, wkv_o, ret_o, s_scr, r_scr):
    R = seqs * clen
    log2c = int(math.log2(clen))
    c_idx = pl.program_id(1)
    hd = A_HEAD_DIM
    pw2 = 2 * hd
    TP = [(t, j) for t in range(nb) for j in range(A_HEADS // 2)]
    TG = [(t, h) for t in range(nb) for h in range(B_HEADS)]

    @pl.when(c_idx == 0)
    def _():
        zero = jnp.zeros((hd, hd), F32)
        for ci, (t, j) in enumerate(TP):
            blocks = [jnp.concatenate([jnp.concatenate([wkv0_ref[t * seqs + i, 2 * j], zero], axis=1),
                                       jnp.concatenate([zero, wkv0_ref[t * seqs + i, 2 * j + 1]], axis=1)], axis=0)
                      for i in range(seqs)]
            s_scr[ci] = jnp.concatenate(blocks, axis=1) if seqs > 1 else blocks[0]
        for t, h in TG:
            r_scr[t * B_HEADS + h] = ret0_ref[t * seqs:(t + 1) * seqs, h].reshape(seqs * B_QK_DIM, B_V_DIM)

    row = lax.broadcasted_iota(jnp.int32, (R, R), 0)
    col = lax.broadcasted_iota(jnp.int32, (R, R), 1)
    same = (row >> log2c) == (col >> log2c)
    incl = same & (col <= row)
    m_incl = jnp.where(incl, 1.0, 0.0).astype(BF16)
    m_same = jnp.where(same, 1.0, 0.0).astype(BF16)

    def expand(t):
        if seqs == 1:
            return t
        w = t.shape[1]
        wide = jnp.concatenate([t] * seqs, axis=1)
        rr = lax.broadcasted_iota(jnp.int32, wide.shape, 0) >> log2c
        cc = lax.broadcasted_iota(jnp.int32, wide.shape, 1) // w
        return jnp.where(rr == cc, wide, 0.0)

    a_bar, r_bar, b_til, k_til, b_dec, k_dec, d_end, vv = [], [], [], [], [], [], [], []
    for t in range(nb):
        lw = lw_ref[t]
        parts = _split2(lw)
        c = sum(jnp.dot(m_incl, p, preferred_element_type=F32) for p in parts)
        if seqs == 1:
            is_last = lax.broadcasted_iota(jnp.int32, c.shape, 0) == R - 1
            cend = jnp.sum(jnp.where(is_last, c, 0.0), axis=0, keepdims=True)
        else:
            cend = sum(jnp.dot(m_same, p, preferred_element_type=F32) for p in parts)
        einv = jnp.exp(-c)
        edec = jnp.exp(cend - c)
        a_bar.append(al_ref[t] * jnp.exp(c - lw))
        r_bar.append(r_ref[t] * jnp.exp(c))
        b_til.append(be_ref[t] * einv)
        k_til.append(k_ref[t] * einv)
        b_dec.append(be_ref[t] * edec)
        k_dec.append(k_ref[t] * edec)
        d_end.append(jnp.exp(cend))
        vv.append(v_ref[t].astype(F32))
    last_row = (lax.broadcasted_iota(jnp.int32, (R, seqs * pw2), 0) & (clen - 1)) == clen - 1

    lane2 = lax.broadcasted_iota(jnp.int32, (R, pw2), 1)
    first = lane2 < hd
    row2 = lax.broadcasted_iota(jnp.int32, (R, pw2), 0)
    col2 = lane2 & (hd - 1)
    same2 = (row2 >> log2c) == (col2 >> log2c)
    incl2 = same2 & (col2 <= row2)
    strict2 = same2 & (col2 < row2)
    eye2 = jnp.where(row2 == col2, 1.0, 0.0).astype(F32)

    def pair(x, j):
        return x[:, j * pw2:(j + 1) * pw2]

    def keep(x, second):
        m = first if x.shape[1] == pw2 else jnp.concatenate([first] * (x.shape[1] // pw2), axis=1)
        return jnp.where(m != second, x, 0.0)

    def blockdiag(y, swapped=False):
        return jnp.concatenate([keep(y, swapped), keep(y, not swapped)], axis=0)

    ams = []
    for t, j in TP:
        lhs = jnp.concatenate([pair(a_bar[t], j), pair(r_bar[t], j)], axis=0)
        bt, kt = pair(b_til[t], j), pair(k_til[t], j)
        lhs0 = jnp.where(jnp.concatenate([first, first], axis=0), lhs, 0.0)
        lhs1 = jnp.where(jnp.concatenate([first, first], axis=0), 0.0, lhs)
        ams.append((_dot_nt(lhs0, jnp.concatenate([bt, kt], axis=0)), _dot_nt(lhs1, jnp.concatenate([kt, bt], axis=0))))

    lgs = [float(np.log1p(-np.exp2(-5.0 - h))) for h in range(B_HEADS)]
    qb = [qb_ref[t].astype(F32) for t in range(nb)]
    kb = [kb_ref[t].astype(F32) for t in range(nb)]
    qs = [qb[t][:, h * B_QK_DIM:(h + 1) * B_QK_DIM] for t, h in TG]
    khs = [kb[t][:, h * B_QK_DIM:(h + 1) * B_QK_DIM] for t, h in TG]
    vbs = [vb_ref[t][:, h * B_V_DIM:(h + 1) * B_V_DIM] for t, h in TG]
    diff = (row - col).astype(F32)
    pos_v = (lax.broadcasted_iota(jnp.int32, (R, B_V_DIM), 0) & (clen - 1)).astype(F32)
    pos_k = (lax.broadcasted_iota(jnp.int32, (R, B_QK_DIM), 0) & (clen - 1)).astype(F32)
    intra = [jnp.where(incl, jnp.exp(lg * diff), 0.0) for lg in lgs]
    cross = [jnp.exp(lg * (pos_v + 1.0)) for lg in lgs]
    kdec = [jnp.exp(lg * (clen - 1.0 - pos_k)) for lg in lgs]
    rstates = [r_scr[t * B_HEADS + h] for t, h in TG]
    scs = [_dot_nt(qs[i], khs[i]) * intra[h] for i, (t, h) in enumerate(TG)]
    qst = [_dot(expand(qs[i]), rstates[i]) * cross[h] for i, (t, h) in enumerate(TG)]

    n = len(TP)
    a_ab = [jnp.where(strict2, jnp.where(first, m0[:R], m1[:R]), 0.0) for m0, m1 in ams]
    a_ak = [jnp.where(strict2, jnp.where(first, m1[:R], m0[:R]), 0.0) for m0, m1 in ams]
    a_rb = [jnp.where(incl2, jnp.where(first, m0[R:], m1[R:]), 0.0) for m0, m1 in ams]
    a_rk = [jnp.where(incl2, jnp.where(first, m1[R:], m0[R:]), 0.0) for m0, m1 in ams]
    vps = [pair(vv[t], j) for t, j in TP]
    av = [_dot(a_ak[i], blockdiag(vps[i], swapped=True)) for i in range(n)]
    tinv = [eye2 + a for a in a_ab]
    if log2c > 1:
        pw = [_dot(a, blockdiag(a)) for a in a_ab]
    for it in range(log2c - 1):
        if it < log2c - 2:
            tp = [_dot(pw[i], blockdiag(jnp.concatenate([tinv[i], pw[i]], axis=1))) for i in range(n)]
            tinv = [tinv[i] + tp[i][:, :pw2] for i in range(n)]
            pw = [tp[i][:, pw2:] for i in range(n)]
        else:
            tinv = [tinv[i] + _dot(pw[i], blockdiag(tinv[i])) for i in range(n)]

    os_ = [_dot(scs[i], vbs[i]) + qst[i] for i in range(len(TG))]
    for i, (t, h) in enumerate(TG):
        r_scr[t * B_HEADS + h] = (rstates[i] * float(np.exp(lgs[h] * clen))
                                  + _dot_tn(expand(khs[i] * kdec[h]), vbs[i]))
    for t in range(nb):
        ob_o[t] = jnp.concatenate(os_[t * B_HEADS:(t + 1) * B_HEADS], axis=1)

    wu = [_dot(tinv[i], blockdiag(jnp.concatenate([pair(a_bar[t], j), av[i]], axis=1)))
          for i, (t, j) in enumerate(TP)]
    states = [s_scr[i] for i in range(n)]
    ws = [_dot_nt(jnp.concatenate([expand(wu[i][:, :pw2]), expand(pair(r_bar[t], j))], axis=0), states[i])
          for i, (t, j) in enumerate(TP)]
    us = [ws[i][:R] + wu[i][:, pw2:] for i in range(n)]
    ys = [ws[i][R:] + _dot(jnp.concatenate([a_rb[i], a_rk[i]], axis=1),
                           jnp.concatenate([blockdiag(us[i]), blockdiag(vps[i], swapped=True)], axis=0))
          for i in range(n)]
    npair = A_HEADS // 2
    for t in range(nb):
        ya_o[t] = jnp.concatenate(ys[t * npair:(t + 1) * npair], axis=1)
    rowp = lax.broadcasted_iota(jnp.int32, (pw2, seqs * pw2), 0) >= hd
    lanep = (lax.broadcasted_iota(jnp.int32, (pw2, seqs * pw2), 1) & (pw2 - 1)) >= hd
    for i, (t, j) in enumerate(TP):
        d_row = jnp.sum(jnp.where(last_row, expand(pair(d_end[t], j)), 0.0), axis=0, keepdims=True)
        bk = jnp.concatenate([expand(pair(b_dec[t], j)), expand(pair(k_dec[t], j))], axis=0)
        upd = _dot_tn(jnp.concatenate([us[i], vps[i]], axis=0), bk)
        s_scr[i] = states[i] * d_row + jnp.where(rowp == lanep, upd, 0.0)

    @pl.when(c_idx == pl.num_programs(1) - 1)
    def _():
        for ci, (t, j) in enumerate(TP):
            st = s_scr[ci]
            for i in range(seqs):
                wkv_o[t * seqs + i, 2 * j] = st[:hd, i * pw2:i * pw2 + hd]
                wkv_o[t * seqs + i, 2 * j + 1] = st[hd:, i * pw2 + hd:(i + 1) * pw2]
        for t, h in TG:
            ret_o[t * seqs:(t + 1) * seqs, h] = r_scr[t * B_HEADS + h].reshape(seqs, B_QK_DIM, B_V_DIM)


def _mixer(ops, wkv0, ret0, n_seq, seq_len):
    n = ops[0].shape[0]
    R = CHUNK_ROWS
    if seq_len >= R:
        assert seq_len % R == 0
        seqs, clen, nchunks = 1, R, seq_len // R
    else:
        assert R % seq_len == 0 and seq_len & (seq_len - 1) == 0 and n_seq % (R // seq_len) == 0
        seqs, clen, nchunks = R // seq_len, seq_len, 1
    ntiles = n_seq // seqs
    nb = MIXER_TILES if seqs == 1 else MIXER_TILES_PACKED
    assert ntiles % nb == 0
    ops3 = [a.reshape(ntiles, nchunks * R, a.shape[1]) for a in ops]

    def rows(width):
        return pl.BlockSpec((nb, R, width), lambda i, c: (i, c, 0))

    wkv_spec = pl.BlockSpec((nb * seqs, A_HEADS, A_HEAD_DIM, A_HEAD_DIM), lambda i, c: (i, 0, 0, 0))
    ret_spec = pl.BlockSpec((nb * seqs, B_HEADS, B_QK_DIM, B_V_DIM), lambda i, c: (i, 0, 0, 0))
    ya, ob, wkv1, ret1 = pl.pallas_call(
        functools.partial(_mixer_kernel, nb, seqs, clen),
        grid=(ntiles // nb, nchunks),
        in_specs=[rows(A_WIDTH)] * 6 + [rows(B_QK_WIDTH), rows(B_QK_WIDTH), rows(B_WIDTH), wkv_spec, ret_spec],
        out_specs=[rows(A_WIDTH), rows(B_WIDTH), wkv_spec, ret_spec],
        out_shape=[jax.ShapeDtypeStruct((ntiles, nchunks * R, A_WIDTH), F32),
                   jax.ShapeDtypeStruct((ntiles, nchunks * R, B_WIDTH), F32),
                   jax.ShapeDtypeStruct(wkv0.shape, F32), jax.ShapeDtypeStruct(ret0.shape, F32)],
        scratch_shapes=[pltpu.VMEM((nb * A_HEADS // 2, 2 * A_HEAD_DIM, seqs * 2 * A_HEAD_DIM), F32),
                        pltpu.VMEM((nb * B_HEADS, seqs * B_QK_DIM, B_V_DIM), F32)],
        compiler_params=pltpu.CompilerParams(dimension_semantics=("parallel", "arbitrary"),
                                             vmem_limit_bytes=V7X_VMEM_LIMIT_BYTES),
        name="mixer",
    )(*ops3, wkv0, ret0)
    return ya.reshape(n, A_WIDTH), ob.reshape(n, B_WIDTH), wkv1, ret1


def _layer_norm(z, g, b):
    mu = jnp.mean(z, axis=-1, keepdims=True)
    d = z - mu
    var = jnp.mean(d * d, axis=-1, keepdims=True)
    return d * lax.rsqrt(var + LN_EPS) * g + b


def _post_tile(ya_ref, ob_ref, bonus_ref, g_ref, gb_ref, x_ref, lnxg_ref, lnxb_ref, rgg_ref, rgb_ref,
               wout_ref, ln1g_ref, ln1b_ref, wr_ref, br_ref, h_o, gate_o, cnt_o):
    tm = x_ref.shape[0]

    def head_norm(t, group, eps, gg, bb):
        ones = _group_ones(group)
        mu = _group_sum(t, ones) * (1.0 / group)
        d = t - mu
        var = _group_sum(d * d, ones) * (1.0 / group)
        return d * lax.rsqrt(var + eps) * gg + bb

    y_a = (head_norm(ya_ref[...], A_HEAD_DIM, GN_EPS_RWKV, lnxg_ref[...], lnxb_ref[...]) + bonus_ref[...]) * g_ref[...]
    y_b = head_norm(ob_ref[...], B_V_DIM, GN_EPS, rgg_ref[...], rgb_ref[...]) * gb_ref[...]
    y = jnp.concatenate([y_a, y_b], axis=1)
    mix = _dot(y, wout_ref[...])
    h = _layer_norm(DEEPNORM_ALPHA * x_ref[...] + mix, ln1g_ref[...], ln1b_ref[...])
    h_o[...] = h

    h_hi, h_lo = _split2(h)
    w_hi, w_lo = _split2(wr_ref[...])
    hi_terms = _dot_nt(jnp.concatenate([w_hi, w_lo], axis=0), h_hi)
    logits = (hi_terms[:ROUTE_ROWS] + hi_terms[ROUTE_ROWS:] + _dot_nt(w_hi, h_lo)) + br_ref[...]
    def first_max(vals):
        best = functools.reduce(jnp.maximum, vals)
        idx = float(len(vals) - 1)
        for j in range(len(vals) - 2, -1, -1):
            idx = jnp.where(vals[j] == best, float(j), idx)
        return best, idx

    coarse = [logits[g:g + 1] for g in range(N_GROUPS)]
    cmax, grp = first_max(coarse)
    gprob = 1.0 / sum(jnp.exp(c - cmax) for c in coarse)
    fine = []
    for e in range(EXPERTS_PER_GROUP):
        f = logits[FINE_LANE0 + e:FINE_LANE0 + e + 1]
        for g in range(1, N_GROUPS):
            r = FINE_LANE0 + g * EXPERTS_PER_GROUP + e
            f = jnp.where(grp == float(g), logits[r:r + 1], f)
        fine.append(f)
    m1, i1 = first_max(fine)
    m2, i2 = first_max([jnp.where(i1 == float(e), -jnp.inf, f) for e, f in enumerate(fine)])
    e2 = jnp.exp(m2 - m1)
    w1 = gprob / (1.0 + e2)
    w2 = gprob * e2 / (1.0 + e2)
    row = lax.broadcasted_iota(jnp.int32, (ROUTE_ROWS, tm), 0).astype(F32)
    row0 = FINE_LANE0 + grp * EXPERTS_PER_GROUP
    gate_t = (jnp.where(row == row0 + i1, w1, 0.0) + jnp.where(row == row0 + i2, w2, 0.0)
              + jnp.where(row == GROUP_LANE, grp, 0.0))
    gate = jnp.concatenate([gate_t, jnp.zeros((ROUTER_LANES - ROUTE_ROWS, tm), F32)], axis=0).T
    gate_o[...] = gate
    lane = lax.broadcasted_iota(jnp.int32, (tm, ROUTER_LANES), 1).astype(F32)
    onehot = jnp.where((lane == gate[:, GROUP_LANE:GROUP_LANE + 1]) & (lane < N_GROUPS), 1.0, 0.0)
    cnt_o[0] = jnp.broadcast_to(jnp.sum(onehot, axis=0, keepdims=True), (V7X_SUBLANES, ROUTER_LANES)).astype(jnp.int32)


POST_OPERANDS = 6


def _post_kernel(group_tiles, *refs):
    per_group, shared = refs[:POST_OPERANDS * len(group_tiles)], refs[POST_OPERANDS * len(group_tiles):]
    i = pl.program_id(0)
    start = 0
    for k, tiles in enumerate(group_tiles):
        @pl.when((i >= start) & (i < start + tiles))
        def _(k=k):
            _post_tile(*per_group[POST_OPERANDS * k:POST_OPERANDS * (k + 1)], *shared)

        start += tiles


def _group_rows(tm, width, start, tiles):
    return pl.BlockSpec((tm, width), lambda i, *_: (jnp.clip(i - start, 0, tiles - 1), 0))


def _post(groups, W):
    tm = POST_ROWS
    group_tiles = tuple(g[-1].shape[0] // tm for g in groups)
    assert all(g[-1].shape[0] % tm == 0 for g in groups)
    n = tm * sum(group_tiles)

    def full(a):
        return pl.BlockSpec(a.shape, lambda i: (0,) * a.ndim)

    def rows(width):
        return pl.BlockSpec((tm, width), lambda i: (i, 0))

    widths = [A_WIDTH, B_WIDTH, A_WIDTH, A_WIDTH, B_WIDTH, D_MODEL]
    in_specs, start = [], 0
    for tiles in group_tiles:
        in_specs += [_group_rows(tm, w, start, tiles) for w in widths]
        start += tiles
    params = [W["lnx_g"], W["lnx_b"], W["ret_gn_g"], W["ret_gn_b"], W["w_out"], W["ln1_g"], W["ln1_b"],
              W["w_router"], W["b_router"]]
    return pl.pallas_call(
        functools.partial(_post_kernel, group_tiles),
        grid=(n // tm,),
        in_specs=in_specs + [full(a) for a in params],
        out_specs=[rows(D_MODEL), rows(ROUTER_LANES),
                   pl.BlockSpec((1, V7X_SUBLANES, ROUTER_LANES), lambda i: (i, 0, 0))],
        out_shape=[jax.ShapeDtypeStruct((n, D_MODEL), F32), jax.ShapeDtypeStruct((n, ROUTER_LANES), F32),
                   jax.ShapeDtypeStruct((n // tm, V7X_SUBLANES, ROUTER_LANES), jnp.int32)],
        compiler_params=pltpu.CompilerParams(dimension_semantics=("parallel",),
                                             vmem_limit_bytes=V7X_VMEM_LIMIT_BYTES),
        name="post",
    )(*[a for g in groups for a in g], *params)


def _sort_positions(gate, lofs):
    tm = gate.shape[0]
    lane = lax.broadcasted_iota(jnp.int32, (tm, ROUTER_LANES), 1)
    grp = gate[:, GROUP_LANE:GROUP_LANE + 1].astype(jnp.int32)
    onehot = jnp.where((lane == grp) & (lane < N_GROUPS), 1.0, 0.0)
    r = lax.broadcasted_iota(jnp.int32, (tm, tm), 0)
    c = lax.broadcasted_iota(jnp.int32, (tm, tm), 1)
    earlier = jnp.where(c < r, 1.0, 0.0).astype(BF16)
    prefix = jnp.dot(earlier, onehot.astype(BF16), preferred_element_type=F32)
    base = jnp.zeros((tm, ROUTER_LANES), F32)
    for g in range(N_GROUPS):
        base = jnp.where(lane == g, lofs[g].astype(F32), base)
    return jnp.sum(onehot * (base + prefix), axis=1, keepdims=True).astype(jnp.int32)


def _piece_copies(action, rows, bits, copy_of):
    k = rows // SORT_ALIGN
    for b in reversed(range(bits)):
        size = SORT_ALIGN << b

        @pl.when(((k >> b) & 1) == 1)
        def _():
            done = ((k >> (b + 1)) << (b + 1)) * SORT_ALIGN
            cp = copy_of(done, size)
            cp.start() if action == "start" else cp.wait()


def _run_copies(action, plan_ref, i, src_of, dst_of, sem):
    for g in range(N_GROUPS):
        goff = plan_ref[i, g]
        lofs = plan_ref[i, 2 * N_GROUPS + g]

        def copy_of(done, size, goff=goff, lofs=lofs):
            lo = pl.multiple_of(lofs + done, SORT_ALIGN)
            go = pl.multiple_of(goff + done, SORT_ALIGN)
            return pltpu.make_async_copy(src_of(lo, go, size), dst_of(lo, go, size), sem)

        _piece_copies(action, plan_ref[i, N_GROUPS + g], RUN_BITS, copy_of)


def _dispatch_kernel(plan_ref, gap_ref, h_ref, gate_ref, hs_o, pos_o, h_loc, sem):
    i = pl.program_id(0)
    tm = h_ref.shape[0]
    lofs = [plan_ref[i, 2 * N_GROUPS + g] for g in range(N_GROUPS)]
    pos = _sort_positions(gate_ref[...], lofs)
    pos_o[...] = jnp.broadcast_to(pos, pos_o.shape)
    onehot_t = jnp.where(lax.broadcasted_iota(jnp.int32, (tm, SORT_LOCAL), 1) == pos, 1.0, 0.0).astype(BF16)
    idx = lax.broadcasted_iota(jnp.int32, (V7X_SUBLANES, SORT_LOCAL), 1)
    pos_row = (_dot_nt((idx // V7X_LANES).astype(F32), onehot_t) * float(V7X_LANES)
               + _dot_nt((idx % V7X_LANES).astype(F32), onehot_t))[0:1].astype(jnp.int32)
    perm = jnp.where(lax.broadcasted_iota(jnp.int32, (SORT_LOCAL, tm), 0) == pos_row, 1.0, 0.0).astype(BF16)
    payload = jnp.concatenate([h_ref[...].astype(BF16)] + list(_split3(gate_ref[...])), axis=1)
    sorted_rows = jnp.dot(perm, payload, preferred_element_type=F32).astype(BF16)

    def copies(action, tile):
        _run_copies(action, plan_ref, tile, lambda lo, go, sz: h_loc.at[pl.ds(lo, sz)],
                    lambda lo, go, sz: hs_o.at[pl.ds(go, sz)], sem)

    @pl.when(i > 0)
    def _():
        copies("wait", i - 1)

    h_loc[...] = sorted_rows
    copies("start", i)

    @pl.when(i == pl.num_programs(0) - 1)
    def _():
        copies("wait", i)
        h_loc[...] = jnp.zeros_like(h_loc)
        for action in ("start", "wait"):
            for g in range(N_GROUPS):
                def copy_of(done, size, g=g):
                    go = pl.multiple_of(gap_ref[g] + done, SORT_ALIGN)
                    return pltpu.make_async_copy(h_loc.at[pl.ds(0, size)], hs_o.at[pl.ds(go, size)], sem)

                _piece_copies(action, gap_ref[N_GROUPS + g], GAP_BITS, copy_of)

            def body(k, carry):
                go = pl.multiple_of(gap_ref[2 * N_GROUPS] + k * EXPERT_ROWS_SHORT, SORT_ALIGN)
                cp = pltpu.make_async_copy(h_loc.at[pl.ds(0, EXPERT_ROWS_SHORT)], hs_o.at[pl.ds(go, EXPERT_ROWS_SHORT)], sem)
                cp.start() if action == "start" else cp.wait()
                return carry

            lax.fori_loop(0, gap_ref[2 * N_GROUPS + 1] // EXPERT_ROWS_SHORT, body, 0)


def _experts_kernel(tile_group_ref, n_valid_ref, hs_ref, w1_ref, w3_ref, w2_ref, ys_o, w1_b, w3_b, w2_b):
    j = pl.program_id(0)
    last = n_valid_ref[0] - 1
    g = tile_group_ref[jnp.minimum(j, last)]
    g_prev = tile_group_ref[jnp.minimum(jnp.maximum(j - 1, 0), last)]

    @pl.when((j == 0) | (g != g_prev))
    def _():
        w1_b[...] = w1_ref[...].astype(BF16)
        w3_b[...] = w3_ref[...].astype(BF16)
        w2_b[...] = w2_ref[...].astype(BF16)

    @pl.when(j <= last)
    def _():
        x = hs_ref[:, :D_MODEL]
        gs = sum(hs_ref[:, D_MODEL + t * ROUTER_LANES:D_MODEL + (t + 1) * ROUTER_LANES].astype(F32) for t in range(3))
        lane = lax.broadcasted_iota(jnp.int32, gs.shape, 1)
        acc = jnp.zeros(ys_o.shape, F32)
        for e in range(EXPERTS_PER_GROUP):
            ge = jnp.sum(jnp.where(lane == FINE_LANE0 + g * EXPERTS_PER_GROUP + e, gs, 0.0), axis=-1, keepdims=True)
            a = jnp.dot(x, w1_b[e], preferred_element_type=F32)
            b = jnp.dot(x, w3_b[e], preferred_element_type=F32)
            hid = (a * _sigmoid(a)) * b * ge
            acc = acc + jnp.dot(hid.astype(BF16), w2_b[e], preferred_element_type=F32)
        ys_o[...] = acc.astype(ys_o.dtype)

    @pl.when(j > last)
    def _():
        ys_o[...] = jnp.zeros_like(ys_o)


def _combine_kernel(group_tiles, plan_ref, h_ref, pos_ref, *refs):
    ng = len(group_tiles)
    p_refs, (ys_ref, ln2g_ref, ln2b_ref, wple_ref, wpg_ref, pleg_ref) = refs[:ng], refs[ng:ng + 6]
    o_refs, (y_loc, sem) = refs[ng + 6:2 * ng + 6], refs[2 * ng + 6:]
    i = pl.program_id(0)
    starts = [sum(group_tiles[:k]) for k in range(ng)]

    def in_group(k):
        return (i >= starts[k]) & (i < starts[k] + group_tiles[k])

    tm = h_ref.shape[0]

    def fetch(tile):
        slot = tile % 2
        y_loc[slot] = jnp.zeros(y_loc.shape[1:], y_loc.dtype)
        _run_copies("start", plan_ref, tile, lambda lo, go, sz: ys_ref.at[pl.ds(go, sz)],
                    lambda lo, go, sz: y_loc.at[slot, pl.ds(lo, sz)], sem.at[slot])

    @pl.when(i == 0)
    def _():
        fetch(i)

    @pl.when(i + 1 < pl.num_programs(0))
    def _():
        fetch(i + 1)

    pos = pos_ref[:, 0:1]
    onehot_t = jnp.where(lax.broadcasted_iota(jnp.int32, (tm, SORT_LOCAL), 1) == pos, 1.0, 0.0).astype(BF16)
    p_tile = p_refs[0][...]
    for k in range(1, ng):
        p_tile = jnp.where(in_group(k), p_refs[k][...], p_tile)
    ple_in = _dot(p_tile, wple_ref[...])
    slot = i % 2
    _run_copies("wait", plan_ref, i, lambda lo, go, sz: ys_ref.at[pl.ds(go, sz)],
                lambda lo, go, sz: y_loc.at[slot, pl.ds(lo, sz)], sem.at[slot])
    ffn = jnp.dot(onehot_t, y_loc[slot], preferred_element_type=F32)
    h2 = _layer_norm(DEEPNORM_ALPHA * h_ref[...] + ffn, ln2g_ref[...], ln2b_ref[...])
    ple = ple_in * _sigmoid(_dot(h2, wpg_ref[...]))
    ms = jnp.mean(ple * ple, axis=-1, keepdims=True)
    out = h2 + ple * lax.rsqrt(ms + LN_EPS) * pleg_ref[...]
    for k in range(ng):
        @pl.when(in_group(k))
        def _(k=k):
            o_refs[k][...] = out


def _ffn(h, gate, counts, p_groups, W):
    n = h.shape[0]
    tm = SORT_ROWS
    assert n % tm == 0 and tm == POST_ROWS and SORT_LOCAL >= tm + N_GROUPS * SORT_ALIGN
    ntiles = n // tm
    er = EXPERT_ROWS if n >= 2 * N_GROUPS * EXPERT_ROWS else EXPERT_ROWS_SHORT
    assert er <= SORT_ALIGN << GAP_BITS and SORT_ALIGN << (GAP_BITS - 1) <= SORT_LOCAL
    cnt = counts[:, 0, :N_GROUPS]
    run = (cnt + SORT_ALIGN - 1) // SORT_ALIGN * SORT_ALIGN
    lofs = jnp.cumsum(run, axis=1) - run
    seg = (jnp.sum(run, axis=0) + er - 1) // er * er
    gbase = jnp.cumsum(seg) - seg
    goff = gbase[None, :] + jnp.cumsum(run, axis=0) - run
    plan = jnp.concatenate([goff, run, lofs], axis=1).astype(jnp.int32)
    max_tiles = (n + ntiles * N_GROUPS * (SORT_ALIGN - 1)) // er + N_GROUPS
    cap = max_tiles * er
    n_valid = (jnp.sum(seg) // er).astype(jnp.int32).reshape(1)
    tile_start = jnp.arange(max_tiles, dtype=jnp.int32) * er
    tile_group = jnp.clip(jnp.sum(tile_start[:, None] >= (gbase + seg)[None, :], axis=1), 0, N_GROUPS - 1).astype(jnp.int32)

    cparams = dict(vmem_limit_bytes=V7X_VMEM_LIMIT_BYTES)
    any_spec = pl.BlockSpec(memory_space=pl.ANY)
    total = jnp.sum(run, axis=0)
    assert er % EXPERT_ROWS_SHORT == 0 and EXPERT_ROWS_SHORT <= SORT_LOCAL
    used = jnp.sum(seg)
    gaps = jnp.concatenate([gbase + total, seg - total, jnp.stack([used, cap - used])]).astype(jnp.int32)
    hs, pos = pl.pallas_call(
        _dispatch_kernel,
        grid_spec=pltpu.PrefetchScalarGridSpec(
            num_scalar_prefetch=2, grid=(ntiles,),
            in_specs=[pl.BlockSpec((tm, D_MODEL), lambda i, plan, gaps: (i, 0)),
                      pl.BlockSpec((tm, ROUTER_LANES), lambda i, plan, gaps: (i, 0))],
            out_specs=[any_spec, pl.BlockSpec((tm, ROUTER_LANES), lambda i, plan, gaps: (i, 0))],
            scratch_shapes=[pltpu.VMEM((SORT_LOCAL, SORTED_WIDTH), BF16), pltpu.SemaphoreType.DMA(())]),
        out_shape=[jax.ShapeDtypeStruct((cap, SORTED_WIDTH), BF16), jax.ShapeDtypeStruct((n, ROUTER_LANES), jnp.int32)],
        compiler_params=pltpu.CompilerParams(dimension_semantics=("arbitrary",), **cparams),
        name="dispatch",
    )(plan, gaps, h, gate)

    def tile_rows(width):
        return pl.BlockSpec((er, width), lambda j, tg, nv: (jnp.minimum(j, nv[0] - 1), 0))

    def group_w(shape):
        return pl.BlockSpec((EXPERTS_PER_GROUP,) + shape, lambda j, tg, nv: (tg[jnp.minimum(j, nv[0] - 1)], 0, 0))

    ys = pl.pallas_call(
        _experts_kernel,
        grid_spec=pltpu.PrefetchScalarGridSpec(
            num_scalar_prefetch=2, grid=(max_tiles,),
            in_specs=[tile_rows(SORTED_WIDTH), group_w((D_MODEL, D_EXPERT)),
                      group_w((D_MODEL, D_EXPERT)), group_w((D_EXPERT, D_MODEL))],
            out_specs=pl.BlockSpec((er, D_MODEL), lambda j, tg, nv: (j, 0)),
            scratch_shapes=[pltpu.VMEM((EXPERTS_PER_GROUP, D_MODEL, D_EXPERT), BF16),
                            pltpu.VMEM((EXPERTS_PER_GROUP, D_MODEL, D_EXPERT), BF16),
                            pltpu.VMEM((EXPERTS_PER_GROUP, D_EXPERT, D_MODEL), BF16)]),
        out_shape=jax.ShapeDtypeStruct((cap, D_MODEL), BF16),
        compiler_params=pltpu.CompilerParams(dimension_semantics=("arbitrary",), **cparams),
        name="experts",
    )(tile_group, n_valid, hs, W["expert_w1"], W["expert_w3"], W["expert_w2"])

    def full(a):
        return pl.BlockSpec(a.shape, lambda i, plan: (0,) * a.ndim)

    params = [W["ln2_g"], W["ln2_b"], W["w_ple"], W["w_ple_gate"], W["ple_norm_g"]]
    group_tiles = tuple(pg.shape[0] // tm for pg in p_groups)
    assert sum(group_tiles) == ntiles
    starts = [sum(group_tiles[:k]) for k in range(len(group_tiles))]
    return pl.pallas_call(
        functools.partial(_combine_kernel, group_tiles),
        grid_spec=pltpu.PrefetchScalarGridSpec(
            num_scalar_prefetch=1, grid=(ntiles,),
            in_specs=[pl.BlockSpec((tm, D_MODEL), lambda i, plan: (i, 0)),
                      pl.BlockSpec((tm, ROUTER_LANES), lambda i, plan: (i, 0))]
            + [_group_rows(tm, D_PLE, st, t) for st, t in zip(starts, group_tiles)]
            + [any_spec] + [full(a) for a in params],
            out_specs=[_group_rows(tm, D_MODEL, st, t) for st, t in zip(starts, group_tiles)],
            scratch_shapes=[pltpu.VMEM((2, SORT_LOCAL, D_MODEL), BF16), pltpu.SemaphoreType.DMA((2,))]),
        out_shape=[jax.ShapeDtypeStruct((t * tm, D_MODEL), F32) for t in group_tiles],
        compiler_params=pltpu.CompilerParams(dimension_semantics=("arbitrary",), **cparams),
        name="combine",
    )(plan, h, pos, *p_groups, ys, *params)


def _mix(x, x_prev, wkv0, ret0, pos0, W):
    n_seq, seq_len, _ = x.shape
    x2 = x.reshape(n_seq * seq_len, D_MODEL)
    r, lw, k, v, al, be, g, bonus, qb, kb, vb, gb = _proj(x2, x_prev, seq_len, pos0, W)
    ya, ob, wkv1, ret1 = _mixer((r, lw, k, v, al, be, qb, kb, vb), wkv0, ret0, n_seq, seq_len)
    return (ya, ob, bonus, g, gb, x2), wkv1, ret1


def _prep_weights(i, w_in, mu_shift, w_decay_up, decay_base, w_aaa_up, aaa_base, w_gate_up, k_k, k_a, r_k,
                  lnx_g, lnx_b, ret_gn_g, ret_gn_b, w_out, ln1_g, ln1_b,
                  router_coarse_w, router_coarse_b, router_fine_w, router_fine_b,
                  expert_w1, expert_w3, expert_w2, ln2_g, ln2_b, w_ple, w_ple_gate, ple_norm_g):
    def row(a):
        return a[i].reshape(1, -1).astype(F32)

    pad = ROUTE_ROWS - N_GROUPS - N_EXPERTS
    w_router = jnp.concatenate([router_coarse_w[i].T, router_fine_w[i].T, jnp.zeros((pad, D_MODEL), F32)], axis=0)
    b_router = jnp.concatenate([router_coarse_b[i], router_fine_b[i], jnp.zeros((pad,), F32)]).reshape(-1, 1)
    return {
        "w_in": w_in[i].astype(BF16), "mu_shift": row(mu_shift), "w_decay_up": w_decay_up[i].astype(BF16),
        "decay_base": row(decay_base), "w_aaa_up": w_aaa_up[i].astype(BF16), "aaa_base": row(aaa_base),
        "w_gate_up": w_gate_up[i].astype(BF16), "k_k": row(k_k), "k_a": row(k_a), "r_k": row(r_k),
        "lnx_g": row(lnx_g), "lnx_b": row(lnx_b), "ret_gn_g": row(ret_gn_g), "ret_gn_b": row(ret_gn_b),
        "w_out": w_out[i].astype(BF16), "ln1_g": row(ln1_g), "ln1_b": row(ln1_b),
        "w_router": w_router, "b_router": b_router,
        "expert_w1": expert_w1[i], "expert_w3": expert_w3[i], "expert_w2": expert_w2[i], "ln2_g": row(ln2_g), "ln2_b": row(ln2_b),
        "w_ple": w_ple[i].astype(BF16), "w_ple_gate": w_ple_gate[i].astype(BF16), "ple_norm_g": row(ple_norm_g),
    }


def kernel(x_prompt, x_sample, p_prompt, p_sample, state_wkv, state_shift, state_ret, w_in, mu_shift, w_decay_up, decay_base, w_aaa_up, aaa_base, w_gate_up, k_k, k_a, r_k, lnx_g, lnx_b, ret_gn_g, ret_gn_b, w_out, ln1_g, ln1_b, router_coarse_w, router_coarse_b, router_fine_w, router_fine_b, expert_w1, expert_w3, expert_w2, ln2_g, ln2_b, w_ple, w_ple_gate, ple_norm_g):
    yp, ys = x_prompt, x_sample
    nb = x_prompt.shape[0]
    depth = w_in.shape[0]
    wkv_p, shift_p, ret_p, wkv_s, shift_s, ret_s = [], [], [], [], [], []
    for i in range(depth):
        W = _prep_weights(i, w_in, mu_shift, w_decay_up, decay_base, w_aaa_up, aaa_base, w_gate_up, k_k, k_a, r_k,
                          lnx_g, lnx_b, ret_gn_g, ret_gn_b, w_out, ln1_g, ln1_b,
                          router_coarse_w, router_coarse_b, router_fine_w, router_fine_b,
                          expert_w1, expert_w3, expert_w2, ln2_g, ln2_b, w_ple, w_ple_gate, ple_norm_g)
        ops_p, wp, rp = _mix(yp, jnp.zeros((nb, D_MODEL), F32), jnp.zeros((nb, A_HEADS, A_HEAD_DIM, A_HEAD_DIM), F32),
                             jnp.zeros((nb, B_HEADS, B_QK_DIM, B_V_DIM), F32), 0, W)
        ops_s, wsm, rsm = _mix(ys, state_shift[i], state_wkv[i], state_ret[i], PAST_LEN, W)
        sp, ss = yp[:, -1], ys[:, -1]
        h, gate, counts = _post([ops_p, ops_s], W)
        out_p, out_s = _ffn(h, gate, counts, [p_prompt[i].reshape(-1, D_PLE), p_sample[i].reshape(-1, D_PLE)], W)
        yp, ys = out_p.reshape(yp.shape), out_s.reshape(ys.shape)
        wkv_p.append(wp); shift_p.append(sp); ret_p.append(rp)
        wkv_s.append(wsm); shift_s.append(ss); ret_s.append(rsm)
    return (yp, ys, jnp.stack(wkv_p, 0), jnp.stack(shift_p, 0), jnp.stack(ret_p, 0),
            jnp.stack(wkv_s, 0), jnp.stack(shift_s, 0), jnp.stack(ret_s, 0))
```

```python
import functools
import math

import numpy as np
import jax
import jax.numpy as jnp
from jax import lax
from jax.experimental import pallas as pl
from jax.experimental.pallas import tpu as pltpu

F32 = jnp.float32
BF16 = jnp.bfloat16

D_MODEL = 1024
D_PLE = 256
A_HEADS = 8
A_HEAD_DIM = 64
A_WIDTH = A_HEADS * A_HEAD_DIM
DECAY_LORA = 64
AAA_LORA = 64
GATE_LORA = 128
GN_EPS_RWKV = 64e-5
B_HEADS = 4
B_QK_DIM = 64
B_V_DIM = 128
B_QK_WIDTH = B_HEADS * B_QK_DIM
B_WIDTH = B_HEADS * B_V_DIM
ROPE_BASE = 10000.0
GN_EPS = 1e-5
SHIFT_WIDTH = 3 * A_WIDTH + DECAY_LORA + AAA_LORA + GATE_LORA
IN_WIDTH = SHIFT_WIDTH + 2 * B_QK_WIDTH + 2 * B_WIDTH
N_GROUPS = 4
EXPERTS_PER_GROUP = 4
N_EXPERTS = N_GROUPS * EXPERTS_PER_GROUP
D_EXPERT = 256
DEPTH = 1
PAST_LEN = 16384
DEEPNORM_ALPHA = (2 * DEPTH) ** 0.25
LN_EPS = 1e-5

V7X_LANES = 128
V7X_SUBLANES = 8
V7X_MXU_WIDTH = 256
V7X_VMEM_LIMIT_BYTES = 56 * 1024 * 1024

PROJ_ROWS = 1024
LOG_DECAY_OUT = 1
CHUNK_ROWS = 64
MIXER_TILES = 8
MIXER_TILES_PACKED = 2
POST_ROWS = 512
SORT_ROWS = 512
SORT_ALIGN = 16
SORT_LOCAL = 640
EXPERT_ROWS = 1024
EXPERT_ROWS_SHORT = 256
GAP_BITS = 6
RUN_BITS = 6
GROUP_LANE = 0
ROUTER_LANES = V7X_LANES
ROUTE_ROWS = 32
FINE_LANE0 = N_GROUPS
SORTED_WIDTH = D_MODEL + 3 * ROUTER_LANES


def _dot(a, b):
    return jnp.dot(a.astype(BF16), b.astype(BF16), preferred_element_type=F32)


def _dot_nt(a, b):
    return lax.dot_general(a.astype(BF16), b.astype(BF16), (((1,), (1,)), ((), ())), preferred_element_type=F32)


def _dot_tn(a, b):
    return lax.dot_general(a.astype(BF16), b.astype(BF16), (((0,), (0,)), ((), ())), preferred_element_type=F32)


def _split2(x):
    hi = x.astype(BF16)
    lo = (x - hi.astype(F32)).astype(BF16)
    return hi, lo


def _split3(x):
    hi = x.astype(BF16)
    r1 = x - hi.astype(F32)
    mid = r1.astype(BF16)
    lo = (r1 - mid.astype(F32)).astype(BF16)
    return hi, mid, lo


def _sigmoid(x):
    return 1.0 / (1.0 + jnp.exp(-x))


def _group_ones(group):
    assert V7X_MXU_WIDTH % group == 0
    r = lax.broadcasted_iota(jnp.int32, (V7X_MXU_WIDTH, V7X_MXU_WIDTH), 0) // group
    c = lax.broadcasted_iota(jnp.int32, (V7X_MXU_WIDTH, V7X_MXU_WIDTH), 1) // group
    return jnp.where(r == c, 1.0, 0.0).astype(BF16)


def _group_sum(x, ones):
    xb, w = x.astype(BF16), ones.shape[0]
    assert x.shape[1] % w == 0
    return jnp.concatenate([jnp.dot(xb[:, j:j + w], ones, preferred_element_type=F32)
                            for j in range(0, x.shape[1], w)], axis=1)


def _proj_kernel(carry_mode, seq_len, tiles_per_seq,
                 x_ref, xp_ref, w_ref, mu_ref, wdec_ref, dbase_ref, waaa_ref, abase_ref, wgate_ref,
                 kk_ref, ka_ref, rk_ref, cos_ref, sin_ref,
                 r_o, lw_o, k_o, v_o, al_o, be_o, g_o, bonus_o, qb_o, kb_o, vb_o, gb_o,
                 carry_scr):
    tm = x_ref.shape[0]
    if carry_mode:
        xp = jnp.broadcast_to(xp_ref[0], (V7X_SUBLANES, D_MODEL))
        xb = jnp.concatenate([x_ref[...], xp], axis=0).astype(BF16)
        j = pl.program_id(0) % tiles_per_seq

        @pl.when(pl.program_id(0) == 0)
        def _():
            carry_scr[...] = jnp.zeros_like(carry_scr)
    else:
        xb = x_ref[...].astype(BF16)
        xpb = xp_ref[...].astype(BF16)

    def project(lo, hi):
        return jnp.dot(xb, w_ref[:, lo:hi], preferred_element_type=F32)

    def shifted(p, lo, hi):
        cur = p[:tm]
        row = lax.broadcasted_iota(jnp.int32, cur.shape, 0)
        rolled = pltpu.roll(cur, 1, 0)
        if carry_mode:
            first = jnp.where(j == 0, p[tm + V7X_SUBLANES - 1:], carry_scr[V7X_SUBLANES - 1:V7X_SUBLANES, lo:hi])
            prev = jnp.where(row == 0, first, rolled)
            carry_scr[:, lo:hi] = cur[tm - V7X_SUBLANES:]
        else:
            first = jnp.dot(xpb, w_ref[:, lo:hi], preferred_element_type=F32)
            prev = jnp.where((row & (seq_len - 1)) == 0, first, rolled)
        return cur + (prev - cur) * mu_ref[:, lo:hi]

    c_r, c_k, c_v, c_l = 0, A_WIDTH, 2 * A_WIDTH, 3 * A_WIDTH
    p_lora = project(c_l, SHIFT_WIDTH)
    p_k = project(c_k, c_v)
    p_r = project(c_r, c_k)

    lora = shifted(p_lora, c_l, SHIFT_WIDTH)
    w_lo = lora[:, :DECAY_LORA]
    a_lo = lora[:, DECAY_LORA:DECAY_LORA + AAA_LORA]
    g_lo = lora[:, DECAY_LORA + AAA_LORA:]
    z = -(dbase_ref[...] + _dot(jnp.tanh(w_lo), wdec_ref[...]))
    softplus = jnp.maximum(z, 0.0) + jnp.log(1.0 + jnp.exp(-jnp.abs(z)))
    log_w = -softplus - 0.5
    lw_o[...] = -jnp.exp(log_w)
    a = _sigmoid(abase_ref[...] + _dot(a_lo, waaa_ref[...]))
    g_o[...] = (_dot(_sigmoid(g_lo), wgate_ref[...])).astype(g_o.dtype)

    p_v = project(c_v, c_l)

    ones64 = _group_ones(A_HEAD_DIM)
    k0 = shifted(p_k, c_k, c_v)
    kk0 = k0 * kk_ref[...]
    ssq = _group_sum(kk0 * kk0, ones64)
    kk = kk0 * jnp.minimum(lax.rsqrt(ssq), 1e12)
    k = k0 * (1.0 + (a - 1.0) * ka_ref[...])
    k_o[...] = (k).astype(k_o.dtype)
    al_o[...] = (-kk).astype(al_o.dtype)
    be_o[...] = (kk * a).astype(be_o.dtype)

    o = SHIFT_WIDTH
    p_qk = project(o, o + 2 * B_QK_WIDTH)[:tm]

    r = shifted(p_r, c_r, c_k)
    r_o[...] = (r).astype(r_o.dtype)
    rk_sum = _group_sum(r * k * rk_ref[...], ones64)

    p_vb = project(o + 2 * B_QK_WIDTH, o + 2 * B_QK_WIDTH + B_WIDTH)[:tm]

    v = shifted(p_v, c_v, c_l)
    v_o[...] = (v).astype(v_o.dtype)
    bonus_o[...] = (rk_sum * v).astype(bonus_o.dtype)

    p_gb = project(o + 2 * B_QK_WIDTH + B_WIDTH, IN_WIDTH)[:tm]

    q_b = p_qk[:, :B_QK_WIDTH]
    k_b = p_qk[:, B_QK_WIDTH:]
    lane = lax.broadcasted_iota(jnp.int32, (tm, B_QK_WIDTH), 1)
    first_half = (lane & (B_QK_DIM - 1)) < (B_QK_DIM // 2)
    cos = cos_ref[...]
    sin = sin_ref[...]

    def rot(t):
        swapped = jnp.where(first_half, pltpu.roll(t, B_QK_WIDTH - B_QK_DIM // 2, 1), pltpu.roll(t, B_QK_DIM // 2, 1))
        return t * cos + swapped * sin

    qb_o[...] = (rot(q_b)).astype(qb_o.dtype)
    kb_o[...] = (rot(k_b) * (B_QK_DIM ** -0.5)).astype(kb_o.dtype)

    vb_o[...] = (p_vb).astype(vb_o.dtype)
    gb_o[...] = (p_gb * _sigmoid(p_gb)).astype(gb_o.dtype)


def _proj(x2, x_prev, seq_len, pos0, W):
    n = x2.shape[0]
    tm = PROJ_ROWS
    assert n % tm == 0
    carry_mode = seq_len % tm == 0
    if carry_mode:
        tiles_per_seq = seq_len // tm
        xp = x_prev.reshape(-1, 1, D_MODEL)
        xp_spec = pl.BlockSpec((1, 1, D_MODEL), lambda i: (i // tiles_per_seq, 0, 0))
        tab_rows = seq_len
    else:
        assert tm % seq_len == 0 and seq_len & (seq_len - 1) == 0
        tiles_per_seq = 1
        xp = jnp.repeat(x_prev, seq_len, axis=0)
        xp_spec = pl.BlockSpec((tm, D_MODEL), lambda i: (i, 0))
        tab_rows = tm
    half = B_QK_DIM // 2
    inv = ROPE_BASE ** (-jnp.arange(half, dtype=F32) / half)
    pos = (pos0 + jnp.arange(seq_len, dtype=jnp.int32)).astype(F32)
    ang = pos[:, None] * inv[None, :]
    cos = jnp.tile(jnp.concatenate([jnp.cos(ang), jnp.cos(ang)], -1), (tab_rows // seq_len, B_HEADS))
    sin = jnp.tile(jnp.concatenate([-jnp.sin(ang), jnp.sin(ang)], -1), (tab_rows // seq_len, B_HEADS))
    tab_tiles = tab_rows // tm
    tab_spec = pl.BlockSpec((tm, B_QK_WIDTH), lambda i: (i % tab_tiles, 0))

    def full(a):
        return pl.BlockSpec(a.shape, lambda i: (0,) * a.ndim)

    def rows(width):
        return pl.BlockSpec((tm, width), lambda i: (i, 0))

    params = [W["w_in"], W["mu_shift"], W["w_decay_up"], W["decay_base"], W["w_aaa_up"], W["aaa_base"],
              W["w_gate_up"], W["k_k"], W["k_a"], W["r_k"]]
    widths = [A_WIDTH] * 8 + [B_QK_WIDTH, B_QK_WIDTH, B_WIDTH, B_WIDTH]
    outs = pl.pallas_call(
        functools.partial(_proj_kernel, carry_mode, seq_len, tiles_per_seq),
        grid=(n // tm,),
        in_specs=[rows(D_MODEL), xp_spec, pl.BlockSpec(params[0].shape, lambda i: (0, 0), pipeline_mode=pl.Buffered(1))]
        + [full(a) for a in params[1:]] + [tab_spec, tab_spec],
        out_specs=[rows(w) for w in widths],
        out_shape=[jax.ShapeDtypeStruct((n, w), F32 if i == LOG_DECAY_OUT else BF16) for i, w in enumerate(widths)],
        scratch_shapes=[pltpu.VMEM((V7X_SUBLANES, SHIFT_WIDTH), F32)],
        compiler_params=pltpu.CompilerParams(dimension_semantics=("arbitrary",),
                                             vmem_limit_bytes=V7X_VMEM_LIMIT_BYTES),
        name="proj",
    )(x2, xp, *params, cos, sin)
    return outs


def _mixer_kernel(nb, seqs, clen,
                  r_ref, lw_ref, k_ref, v_ref, al_ref, be_ref, qb_ref, kb_ref, vb_ref, wkv0_ref, ret0_ref,
                  ya_o, ob_o, wkv_o, ret_o, s_scr, r_scr):
    R = seqs * clen
    log2c = int(math.log2(clen))
    c_idx = pl.program_id(1)
    hd = A_HEAD_DIM
    pw2 = 2 * hd
    TP = [(t, j) for t in range(nb) for j in range(A_HEADS // 2)]
    TG = [(t, h) for t in range(nb) for h in range(B_HEADS)]

    @pl.when(c_idx == 0)
    def _():
        zero = jnp.zeros((hd, hd), F32)
        for ci, (t, j) in enumerate(TP):
            blocks = [jnp.concatenate([jnp.concatenate([wkv0_ref[t * seqs + i, 2 * j], zero], axis=1),
                                       jnp.concatenate([zero, wkv0_ref[t * seqs + i, 2 * j + 1]], axis=1)], axis=0)
                      for i in range(seqs)]
            s_scr[ci] = jnp.concatenate(blocks, axis=1) if seqs > 1 else blocks[0]
        for t, h in TG:
            r_scr[t * B_HEADS + h] = ret0_ref[t * seqs:(t + 1) * seqs, h].reshape(seqs * B_QK_DIM, B_V_DIM)

    row = lax.broadcasted_iota(jnp.int32, (R, R), 0)
    col = lax.broadcasted_iota(jnp.int32, (R, R), 1)
    same = (row >> log2c) == (col >> log2c)
    incl = same & (col <= row)
    m_incl = jnp.where(incl, 1.0, 0.0).astype(BF16)
    m_same = jnp.where(same, 1.0, 0.0).astype(BF16)

    def expand(t):
        if seqs == 1:
            return t
        w = t.shape[1]
        wide = jnp.concatenate([t] * seqs, axis=1)
        rr = lax.broadcasted_iota(jnp.int32, wide.shape, 0) >> log2c
        cc = lax.broadcasted_iota(jnp.int32, wide.shape, 1) // w
        return jnp.where(rr == cc, wide, 0.0)

    a_bar, r_bar, b_til, k_til, b_dec, k_dec, d_end, vv = [], [], [], [], [], [], [], []
    for t in range(nb):
        lw = lw_ref[t]
        parts = _split2(lw)
        c = sum(jnp.dot(m_incl, p, preferred_element_type=F32) for p in parts)
        if seqs == 1:
            is_last = lax.broadcasted_iota(jnp.int32, c.shape, 0) == R - 1
            cend = jnp.sum(jnp.where(is_last, c, 0.0), axis=0, keepdims=True)
        else:
            cend = sum(jnp.dot(m_same, p, preferred_element_type=F32) for p in parts)
        einv = jnp.exp(-c)
        edec = jnp.exp(cend - c)
        a_bar.append(al_ref[t] * jnp.exp(c - lw))
        r_bar.append(r_ref[t] * jnp.exp(c))
        b_til.append(be_ref[t] * einv)
        k_til.append(k_ref[t] * einv)
        b_dec.append(be_ref[t] * edec)
        k_dec.append(k_ref[t] * edec)
        d_end.append(jnp.exp(cend))
        vv.append(v_ref[t].astype(F32))
    last_row = (lax.broadcasted_iota(jnp.int32, (R, seqs * pw2), 0) & (clen - 1)) == clen - 1

    lane2 = lax.broadcasted_iota(jnp.int32, (R, pw2), 1)
    first = lane2 < hd
    row2 = lax.broadcasted_iota(jnp.int32, (R, pw2), 0)
    col2 = lane2 & (hd - 1)
    same2 = (row2 >> log2c) == (col2 >> log2c)
    incl2 = same2 & (col2 <= row2)
    strict2 = same2 & (col2 < row2)
    eye2 = jnp.where(row2 == col2, 1.0, 0.0).astype(F32)

    def pair(x, j):
        return x[:, j * pw2:(j + 1) * pw2]

    def keep(x, second):
        m = first if x.shape[1] == pw2 else jnp.concatenate([first] * (x.shape[1] // pw2), axis=1)
        return jnp.where(m != second, x, 0.0)

    def blockdiag(y, swapped=False):
        return jnp.concatenate([keep(y, swapped), keep(y, not swapped)], axis=0)

    ams = []
    for t, j in TP:
        lhs = jnp.concatenate([pair(a_bar[t], j), pair(r_bar[t], j)], axis=0)
        bt, kt = pair(b_til[t], j), pair(k_til[t], j)
        lhs0 = jnp.where(jnp.concatenate([first, first], axis=0), lhs, 0.0)
        lhs1 = jnp.where(jnp.concatenate([first, first], axis=0), 0.0, lhs)
        ams.append((_dot_nt(lhs0, jnp.concatenate([bt, kt], axis=0)), _dot_nt(lhs1, jnp.concatenate([kt, bt], axis=0))))

    lgs = [float(np.log1p(-np.exp2(-5.0 - h))) for h in range(B_HEADS)]
    qb = [qb_ref[t].astype(F32) for t in range(nb)]
    kb = [kb_ref[t].astype(F32) for t in range(nb)]
    qs = [qb[t][:, h * B_QK_DIM:(h + 1) * B_QK_DIM] for t, h in TG]
    khs = [kb[t][:, h * B_QK_DIM:(h + 1) * B_QK_DIM] for t, h in TG]
    vbs = [vb_ref[t][:, h * B_V_DIM:(h + 1) * B_V_DIM] for t, h in TG]
    diff = (row - col).astype(F32)
    pos_v = (lax.broadcasted_iota(jnp.int32, (R, B_V_DIM), 0) & (clen - 1)).astype(F32)
    pos_k = (lax.broadcasted_iota(jnp.int32, (R, B_QK_DIM), 0) & (clen - 1)).astype(F32)
    intra = [jnp.where(incl, jnp.exp(lg * diff), 0.0) for lg in lgs]
    cross = [jnp.exp(lg * (pos_v + 1.0)) for lg in lgs]
    kdec = [jnp.exp(lg * (clen - 1.0 - pos_k)) for lg in lgs]
    rstates = [r_scr[t * B_HEADS + h] for t, h in TG]
    scs = [_dot_nt(qs[i], khs[i]) * intra[h] for i, (t, h) in enumerate(TG)]
    qst = [_dot(expand(qs[i]), rstates[i]) * cross[h] for i, (t, h) in enumerate(TG)]

    n = len(TP)
    a_ab = [jnp.where(strict2, jnp.where(first, m0[:R], m1[:R]), 0.0) for m0, m1 in ams]
    a_ak = [jnp.where(strict2, jnp.where(first, m1[:R], m0[:R]), 0.0) for m0, m1 in ams]
    a_rb = [jnp.where(incl2, jnp.where(first, m0[R:], m1[R:]), 0.0) for m0, m1 in ams]
    a_rk = [jnp.where(incl2, jnp.where(first, m1[R:], m0[R:]), 0.0) for m0, m1 in ams]
    vps = [pair(vv[t], j) for t, j in TP]
    av = [_dot(a_ak[i], blockdiag(vps[i], swapped=True)) for i in range(n)]
    tinv = [eye2 + a for a in a_ab]
    if log2c > 1:
        pw = [_dot(a, blockdiag(a)) for a in a_ab]
    for it in range(log2c - 1):
        if it < log2c - 2:
            tp = [_dot(pw[i], blockdiag(jnp.concatenate([tinv[i], pw[i]], axis=1))) for i in range(n)]
            tinv = [tinv[i] + tp[i][:, :pw2] for i in range(n)]
            pw = [tp[i][:, pw2:] for i in range(n)]
        else:
            tinv = [tinv[i] + _dot(pw[i], blockdiag(tinv[i])) for i in range(n)]

    os_ = [_dot(scs[i], vbs[i]) + qst[i] for i in range(len(TG))]
    for i, (t, h) in enumerate(TG):
        r_scr[t * B_HEADS + h] = (rstates[i] * float(np.exp(lgs[h] * clen))
                                  + _dot_tn(expand(khs[i] * kdec[h]), vbs[i]))
    for t in range(nb):
        ob_o[t] = jnp.concatenate(os_[t * B_HEADS:(t + 1) * B_HEADS], axis=1)

    wu = [_dot(tinv[i], blockdiag(jnp.concatenate([pair(a_bar[t], j), av[i]], axis=1)))
          for i, (t, j) in enumerate(TP)]
    states = [s_scr[i] for i in range(n)]
    ws = [_dot_nt(jnp.concatenate([expand(wu[i][:, :pw2]), expand(pair(r_bar[t], j))], axis=0), states[i])
          for i, (t, j) in enumerate(TP)]
    us = [ws[i][:R] + wu[i][:, pw2:] for i in range(n)]
    ys = [ws[i][R:] + _dot(jnp.concatenate([a_rb[i], a_rk[i]], axis=1),
                           jnp.concatenate([blockdiag(us[i]), blockdiag(vps[i], swapped=True)], axis=0))
          for i in range(n)]
    npair = A_HEADS // 2
    for t in range(nb):
        ya_o[t] = jnp.concatenate(ys[t * npair:(t + 1) * npair], axis=1)
    rowp = lax.broadcasted_iota(jnp.int32, (pw2, seqs * pw2), 0) >= hd
    lanep = (lax.broadcasted_iota(jnp.int32, (pw2, seqs * pw2), 1) & (pw2 - 1)) >= hd
    for i, (t, j) in enumerate(TP):
        d_row = jnp.sum(jnp.where(last_row, expand(pair(d_end[t], j)), 0.0), axis=0, keepdims=True)
        bk = jnp.concatenate([expand(pair(b_dec[t], j)), expand(pair(k_dec[t], j))], axis=0)
        upd = _dot_tn(jnp.concatenate([us[i], vps[i]], axis=0), bk)
        s_scr[i] = states[i] * d_row + jnp.where(rowp == lanep, upd, 0.0)

    @pl.when(c_idx == pl.num_programs(1) - 1)
    def _():
        for ci, (t, j) in enumerate(TP):
            st = s_scr[ci]
            for i in range(seqs):
                wkv_o[t * seqs + i, 2 * j] = st[:hd, i * pw2:i * pw2 + hd]
                wkv_o[t * seqs + i, 2 * j + 1] = st[hd:, i * pw2 + hd:(i + 1) * pw2]
        for t, h in TG:
            ret_o[t * seqs:(t + 1) * seqs, h] = r_scr[t * B_HEADS + h].reshape(seqs, B_QK_DIM, B_V_DIM)


def _mixer(ops, wkv0, ret0, n_seq, seq_len):
    n = ops[0].shape[0]
    R = CHUNK_ROWS
    if seq_len >= R:
        assert seq_len % R == 0
        seqs, clen, nchunks = 1, R, seq_len // R
    else:
        assert R % seq_len == 0 and seq_len & (seq_len - 1) == 0 and n_seq % (R // seq_len) == 0
        seqs, clen, nchunks = R // seq_len, seq_len, 1
    ntiles = n_seq // seqs
    nb = MIXER_TILES if seqs == 1 else MIXER_TILES_PACKED
    assert ntiles % nb == 0
    ops3 = [a.reshape(ntiles, nchunks * R, a.shape[1]) for a in ops]

    def rows(width):
        return pl.BlockSpec((nb, R, width), lambda i, c: (i, c, 0))

    wkv_spec = pl.BlockSpec((nb * seqs, A_HEADS, A_HEAD_DIM, A_HEAD_DIM), lambda i, c: (i, 0, 0, 0))
    ret_spec = pl.BlockSpec((nb * seqs, B_HEADS, B_QK_DIM, B_V_DIM), lambda i, c: (i, 0, 0, 0))
    ya, ob, wkv1, ret1 = pl.pallas_call(
        functools.partial(_mixer_kernel, nb, seqs, clen),
        grid=(ntiles // nb, nchunks),
        in_specs=[rows(A_WIDTH)] * 6 + [rows(B_QK_WIDTH), rows(B_QK_WIDTH), rows(B_WIDTH), wkv_spec, ret_spec],
        out_specs=[rows(A_WIDTH), rows(B_WIDTH), wkv_spec, ret_spec],
        out_shape=[jax.ShapeDtypeStruct((ntiles, nchunks * R, A_WIDTH), F32),
                   jax.ShapeDtypeStruct((ntiles, nchunks * R, B_WIDTH), F32),
                   jax.ShapeDtypeStruct(wkv0.shape, F32), jax.ShapeDtypeStruct(ret0.shape, F32)],
        scratch_shapes=[pltpu.VMEM((nb * A_HEADS // 2, 2 * A_HEAD_DIM, seqs * 2 * A_HEAD_DIM), F32),
                        pltpu.VMEM((nb * B_HEADS, seqs * B_QK_DIM, B_V_DIM), F32)],
        compiler_params=pltpu.CompilerParams(dimension_semantics=("parallel", "arbitrary"),
                                             vmem_limit_bytes=V7X_VMEM_LIMIT_BYTES),
        name="mixer",
    )(*ops3, wkv0, ret0)
    return ya.reshape(n, A_WIDTH), ob.reshape(n, B_WIDTH), wkv1, ret1


def _layer_norm(z, g, b):
    mu = jnp.mean(z, axis=-1, keepdims=True)
    d = z - mu
    var = jnp.mean(d * d, axis=-1, keepdims=True)
    return d * lax.rsqrt(var + LN_EPS) * g + b


def _post_tile(ya_ref, ob_ref, bonus_ref, g_ref, gb_ref, x_ref, lnxg_ref, lnxb_ref, rgg_ref, rgb_ref,
               wout_ref, ln1g_ref, ln1b_ref, wr_ref, br_ref, h_o, gate_o, cnt_o):
    tm = x_ref.shape[0]

    def head_norm(t, group, eps, gg, bb):
        ones = _group_ones(group)
        mu = _group_sum(t, ones) * (1.0 / group)
        d = t - mu
        var = _group_sum(d * d, ones) * (1.0 / group)
        return d * lax.rsqrt(var + eps) * gg + bb

    y_a = (head_norm(ya_ref[...], A_HEAD_DIM, GN_EPS_RWKV, lnxg_ref[...], lnxb_ref[...]) + bonus_ref[...]) * g_ref[...]
    y_b = head_norm(ob_ref[...], B_V_DIM, GN_EPS, rgg_ref[...], rgb_ref[...]) * gb_ref[...]
    y = jnp.concatenate([y_a, y_b], axis=1)
    mix = _dot(y, wout_ref[...])
    h = _layer_norm(DEEPNORM_ALPHA * x_ref[...] + mix, ln1g_ref[...], ln1b_ref[...])
    h_o[...] = h

    h_hi, h_lo = _split2(h)
    w_hi, w_lo = _split2(wr_ref[...])
    hi_terms = _dot_nt(jnp.concatenate([w_hi, w_lo], axis=0), h_hi)
    logits = (hi_terms[:ROUTE_ROWS] + hi_terms[ROUTE_ROWS:] + _dot_nt(w_hi, h_lo)) + br_ref[...]
    def first_max(vals):
        best = functools.reduce(jnp.maximum, vals)
        idx = float(len(vals) - 1)
        for j in range(len(vals) - 2, -1, -1):
            idx = jnp.where(vals[j] == best, float(j), idx)
        return best, idx

    coarse = [logits[g:g + 1] for g in range(N_GROUPS)]
    cmax, grp = first_max(coarse)
    gprob = 1.0 / sum(jnp.exp(c - cmax) for c in coarse)
    fine = []
    for e in range(EXPERTS_PER_GROUP):
        f = logits[FINE_LANE0 + e:FINE_LANE0 + e + 1]
        for g in range(1, N_GROUPS):
            r = FINE_LANE0 + g * EXPERTS_PER_GROUP + e
            f = jnp.where(grp == float(g), logits[r:r + 1], f)
        fine.append(f)
    m1, i1 = first_max(fine)
    m2, i2 = first_max([jnp.where(i1 == float(e), -jnp.inf, f) for e, f in enumerate(fine)])
    e2 = jnp.exp(m2 - m1)
    w1 = gprob / (1.0 + e2)
    w2 = gprob * e2 / (1.0 + e2)
    row = lax.broadcasted_iota(jnp.int32, (ROUTE_ROWS, tm), 0).astype(F32)
    row0 = FINE_LANE0 + grp * EXPERTS_PER_GROUP
    gate_t = (jnp.where(row == row0 + i1, w1, 0.0) + jnp.where(row == row0 + i2, w2, 0.0)
              + jnp.where(row == GROUP_LANE, grp, 0.0))
    gate = jnp.concatenate([gate_t, jnp.zeros((ROUTER_LANES - ROUTE_ROWS, tm), F32)], axis=0).T
    gate_o[...] = gate
    lane = lax.broadcasted_iota(jnp.int32, (tm, ROUTER_LANES), 1).astype(F32)
    onehot = jnp.where((lane == gate[:, GROUP_LANE:GROUP_LANE + 1]) & (lane < N_GROUPS), 1.0, 0.0)
    cnt_o[0] = jnp.broadcast_to(jnp.sum(onehot, axis=0, keepdims=True), (V7X_SUBLANES, ROUTER_LANES)).astype(jnp.int32)


POST_OPERANDS = 6


def _post_kernel(group_tiles, *refs):
    per_group, shared = refs[:POST_OPERANDS * len(group_tiles)], refs[POST_OPERANDS * len(group_tiles):]
    i = pl.program_id(0)
    start = 0
    for k, tiles in enumerate(group_tiles):
        @pl.when((i >= start) & (i < start + tiles))
        def _(k=k):
            _post_tile(*per_group[POST_OPERANDS * k:POST_OPERANDS * (k + 1)], *shared)

        start += tiles


def _group_rows(tm, width, start, tiles):
    return pl.BlockSpec((tm, width), lambda i, *_: (jnp.clip(i - start, 0, tiles - 1), 0))


def _post(groups, W):
    tm = POST_ROWS
    group_tiles = tuple(g[-1].shape[0] // tm for g in groups)
    assert all(g[-1].shape[0] % tm == 0 for g in groups)
    n = tm * sum(group_tiles)

    def full(a):
        return pl.BlockSpec(a.shape, lambda i: (0,) * a.ndim)

    def rows(width):
        return pl.BlockSpec((tm, width), lambda i: (i, 0))

    widths = [A_WIDTH, B_WIDTH, A_WIDTH, A_WIDTH, B_WIDTH, D_MODEL]
    in_specs, start = [], 0
    for tiles in group_tiles:
        in_specs += [_group_rows(tm, w, start, tiles) for w in widths]
        start += tiles
    params = [W["lnx_g"], W["lnx_b"], W["ret_gn_g"], W["ret_gn_b"], W["w_out"], W["ln1_g"], W["ln1_b"],
              W["w_router"], W["b_router"]]
    return pl.pallas_call(
        functools.partial(_post_kernel, group_tiles),
        grid=(n // tm,),
        in_specs=in_specs + [full(a) for a in params],
        out_specs=[rows(D_MODEL), rows(ROUTER_LANES),
                   pl.BlockSpec((1, V7X_SUBLANES, ROUTER_LANES), lambda i: (i, 0, 0))],
        out_shape=[jax.ShapeDtypeStruct((n, D_MODEL), F32), jax.ShapeDtypeStruct((n, ROUTER_LANES), F32),
                   jax.ShapeDtypeStruct((n // tm, V7X_SUBLANES, ROUTER_LANES), jnp.int32)],
        compiler_params=pltpu.CompilerParams(dimension_semantics=("parallel",),
                                             vmem_limit_bytes=V7X_VMEM_LIMIT_BYTES),
        name="post",
    )(*[a for g in groups for a in g], *params)


def _sort_positions(gate, lofs):
    tm = gate.shape[0]
    lane = lax.broadcasted_iota(jnp.int32, (tm, ROUTER_LANES), 1)
    grp = gate[:, GROUP_LANE:GROUP_LANE + 1].astype(jnp.int32)
    onehot = jnp.where((lane == grp) & (lane < N_GROUPS), 1.0, 0.0)
    r = lax.broadcasted_iota(jnp.int32, (tm, tm), 0)
    c = lax.broadcasted_iota(jnp.int32, (tm, tm), 1)
    earlier = jnp.where(c < r, 1.0, 0.0).astype(BF16)
    prefix = jnp.dot(earlier, onehot.astype(BF16), preferred_element_type=F32)
    base = jnp.zeros((tm, ROUTER_LANES), F32)
    for g in range(N_GROUPS):
        base = jnp.where(lane == g, lofs[g].astype(F32), base)
    return jnp.sum(onehot * (base + prefix), axis=1, keepdims=True).astype(jnp.int32)


def _piece_copies(action, rows, bits, copy_of):
    k = rows // SORT_ALIGN
    for b in reversed(range(bits)):
        size = SORT_ALIGN << b

        @pl.when(((k >> b) & 1) == 1)
        def _():
            done = ((k >> (b + 1)) << (b + 1)) * SORT_ALIGN
            cp = copy_of(done, size)
            cp.start() if action == "start" else cp.wait()


def _run_copies(action, plan_ref, i, src_of, dst_of, sem):
    for g in range(N_GROUPS):
        goff = plan_ref[i, g]
        lofs = plan_ref[i, 2 * N_GROUPS + g]

        def copy_of(done, size, goff=goff, lofs=lofs):
            lo = pl.multiple_of(lofs + done, SORT_ALIGN)
            go = pl.multiple_of(goff + done, SORT_ALIGN)
            return pltpu.make_async_copy(src_of(lo, go, size), dst_of(lo, go, size), sem)

        _piece_copies(action, plan_ref[i, N_GROUPS + g], RUN_BITS, copy_of)


def _dispatch_kernel(plan_ref, gap_ref, h_ref, gate_ref, hs_o, pos_o, h_loc, sem):
    i = pl.program_id(0)
    tm = h_ref.shape[0]
    lofs = [plan_ref[i, 2 * N_GROUPS + g] for g in range(N_GROUPS)]
    pos = _sort_positions(gate_ref[...], lofs)
    pos_o[...] = jnp.broadcast_to(pos, pos_o.shape)
    onehot_t = jnp.where(lax.broadcasted_iota(jnp.int32, (tm, SORT_LOCAL), 1) == pos, 1.0, 0.0).astype(BF16)
    payload = jnp.concatenate([h_ref[...].astype(BF16)] + list(_split3(gate_ref[...])), axis=1)
    sorted_rows = _dot_tn(onehot_t, payload).astype(BF16)

    def copies(action, tile):
        _run_copies(action, plan_ref, tile, lambda lo, go, sz: h_loc.at[pl.ds(lo, sz)],
                    lambda lo, go, sz: hs_o.at[pl.ds(go, sz)], sem)

    @pl.when(i > 0)
    def _():
        copies("wait", i - 1)

    h_loc[...] = sorted_rows
    copies("start", i)

    @pl.when(i == pl.num_programs(0) - 1)
    def _():
        copies("wait", i)
        h_loc[...] = jnp.zeros_like(h_loc)
        for action in ("start", "wait"):
            for g in range(N_GROUPS):
                def copy_of(done, size, g=g):
                    go = pl.multiple_of(gap_ref[g] + done, SORT_ALIGN)
                    return pltpu.make_async_copy(h_loc.at[pl.ds(0, size)], hs_o.at[pl.ds(go, size)], sem)

                _piece_copies(action, gap_ref[N_GROUPS + g], GAP_BITS, copy_of)

            def body(k, carry):
                go = pl.multiple_of(gap_ref[2 * N_GROUPS] + k * EXPERT_ROWS_SHORT, SORT_ALIGN)
                cp = pltpu.make_async_copy(h_loc.at[pl.ds(0, EXPERT_ROWS_SHORT)], hs_o.at[pl.ds(go, EXPERT_ROWS_SHORT)], sem)
                cp.start() if action == "start" else cp.wait()
                return carry

            lax.fori_loop(0, gap_ref[2 * N_GROUPS + 1] // EXPERT_ROWS_SHORT, body, 0)


def _experts_kernel(tile_group_ref, n_valid_ref, hs_ref, w1_ref, w3_ref, w2_ref, ys_o, w1_b, w3_b, w2_b):
    j = pl.program_id(0)
    last = n_valid_ref[0] - 1
    g = tile_group_ref[jnp.minimum(j, last)]
    g_prev = tile_group_ref[jnp.minimum(jnp.maximum(j - 1, 0), last)]

    @pl.when((j == 0) | (g != g_prev))
    def _():
        w1_b[...] = w1_ref[...].astype(BF16)
        w3_b[...] = w3_ref[...].astype(BF16)
        w2_b[...] = w2_ref[...].astype(BF16)

    @pl.when(j <= last)
    def _():
        x = hs_ref[:, :D_MODEL]
        gs = sum(hs_ref[:, D_MODEL + t * ROUTER_LANES:D_MODEL + (t + 1) * ROUTER_LANES].astype(F32) for t in range(3))
        lane = lax.broadcasted_iota(jnp.int32, gs.shape, 1)
        acc = jnp.zeros(ys_o.shape, F32)
        for e in range(EXPERTS_PER_GROUP):
            ge = jnp.sum(jnp.where(lane == FINE_LANE0 + g * EXPERTS_PER_GROUP + e, gs, 0.0), axis=-1, keepdims=True)
            a = jnp.dot(x, w1_b[e], preferred_element_type=F32)
            b = jnp.dot(x, w3_b[e], preferred_element_type=F32)
            hid = (a * _sigmoid(a)) * b * ge
            acc = acc + jnp.dot(hid.astype(BF16), w2_b[e], preferred_element_type=F32)
        ys_o[...] = acc.astype(ys_o.dtype)

    @pl.when(j > last)
    def _():
        ys_o[...] = jnp.zeros_like(ys_o)


def _combine_kernel(group_tiles, plan_ref, h_ref, pos_ref, *refs):
    ng = len(group_tiles)
    p_refs, (ys_ref, ln2g_ref, ln2b_ref, wple_ref, wpg_ref, pleg_ref) = refs[:ng], refs[ng:ng + 6]
    o_refs, (y_loc, sem) = refs[ng + 6:2 * ng + 6], refs[2 * ng + 6:]
    i = pl.program_id(0)
    starts = [sum(group_tiles[:k]) for k in range(ng)]

    def in_group(k):
        return (i >= starts[k]) & (i < starts[k] + group_tiles[k])

    tm = h_ref.shape[0]

    def fetch(tile):
        slot = tile % 2
        y_loc[slot] = jnp.zeros(y_loc.shape[1:], y_loc.dtype)
        _run_copies("start", plan_ref, tile, lambda lo, go, sz: ys_ref.at[pl.ds(go, sz)],
                    lambda lo, go, sz: y_loc.at[slot, pl.ds(lo, sz)], sem.at[slot])

    @pl.when(i == 0)
    def _():
        fetch(i)

    @pl.when(i + 1 < pl.num_programs(0))
    def _():
        fetch(i + 1)

    pos = pos_ref[:, 0:1]
    onehot_t = jnp.where(lax.broadcasted_iota(jnp.int32, (tm, SORT_LOCAL), 1) == pos, 1.0, 0.0).astype(BF16)
    p_tile = p_refs[0][...]
    for k in range(1, ng):
        p_tile = jnp.where(in_group(k), p_refs[k][...], p_tile)
    ple_in = _dot(p_tile, wple_ref[...])
    slot = i % 2
    _run_copies("wait", plan_ref, i, lambda lo, go, sz: ys_ref.at[pl.ds(go, sz)],
                lambda lo, go, sz: y_loc.at[slot, pl.ds(lo, sz)], sem.at[slot])
    ffn = jnp.dot(onehot_t, y_loc[slot], preferred_element_type=F32)
    h2 = _layer_norm(DEEPNORM_ALPHA * h_ref[...] + ffn, ln2g_ref[...], ln2b_ref[...])
    ple = ple_in * _sigmoid(_dot(h2, wpg_ref[...]))
    ms = jnp.mean(ple * ple, axis=-1, keepdims=True)
    out = h2 + ple * lax.rsqrt(ms + LN_EPS) * pleg_ref[...]
    for k in range(ng):
        @pl.when(in_group(k))
        def _(k=k):
            o_refs[k][...] = out


def _ffn(h, gate, counts, p_groups, W):
    n = h.shape[0]
    tm = SORT_ROWS
    assert n % tm == 0 and tm == POST_ROWS and SORT_LOCAL >= tm + N_GROUPS * SORT_ALIGN
    ntiles = n // tm
    er = EXPERT_ROWS if n >= 2 * N_GROUPS * EXPERT_ROWS else EXPERT_ROWS_SHORT
    assert er <= SORT_ALIGN << GAP_BITS and SORT_ALIGN << (GAP_BITS - 1) <= SORT_LOCAL
    cnt = counts[:, 0, :N_GROUPS]
    run = (cnt + SORT_ALIGN - 1) // SORT_ALIGN * SORT_ALIGN
    lofs = jnp.cumsum(run, axis=1) - run
    seg = (jnp.sum(run, axis=0) + er - 1) // er * er
    gbase = jnp.cumsum(seg) - seg
    goff = gbase[None, :] + jnp.cumsum(run, axis=0) - run
    plan = jnp.concatenate([goff, run, lofs], axis=1).astype(jnp.int32)
    max_tiles = (n + ntiles * N_GROUPS * (SORT_ALIGN - 1)) // er + N_GROUPS
    cap = max_tiles * er
    n_valid = (jnp.sum(seg) // er).astype(jnp.int32).reshape(1)
    tile_start = jnp.arange(max_tiles, dtype=jnp.int32) * er
    tile_group = jnp.clip(jnp.sum(tile_start[:, None] >= (gbase + seg)[None, :], axis=1), 0, N_GROUPS - 1).astype(jnp.int32)

    cparams = dict(vmem_limit_bytes=V7X_VMEM_LIMIT_BYTES)
    any_spec = pl.BlockSpec(memory_space=pl.ANY)
    total = jnp.sum(run, axis=0)
    assert er % EXPERT_ROWS_SHORT == 0 and EXPERT_ROWS_SHORT <= SORT_LOCAL
    used = jnp.sum(seg)
    gaps = jnp.concatenate([gbase + total, seg - total, jnp.stack([used, cap - used])]).astype(jnp.int32)
    hs, pos = pl.pallas_call(
        _dispatch_kernel,
        grid_spec=pltpu.PrefetchScalarGridSpec(
            num_scalar_prefetch=2, grid=(ntiles,),
            in_specs=[pl.BlockSpec((tm, D_MODEL), lambda i, plan, gaps: (i, 0)),
                      pl.BlockSpec((tm, ROUTER_LANES), lambda i, plan, gaps: (i, 0))],
            out_specs=[any_spec, pl.BlockSpec((tm, ROUTER_LANES), lambda i, plan, gaps: (i, 0))],
            scratch_shapes=[pltpu.VMEM((SORT_LOCAL, SORTED_WIDTH), BF16), pltpu.SemaphoreType.DMA(())]),
        out_shape=[jax.ShapeDtypeStruct((cap, SORTED_WIDTH), BF16), jax.ShapeDtypeStruct((n, ROUTER_LANES), jnp.int32)],
        compiler_params=pltpu.CompilerParams(dimension_semantics=("arbitrary",), **cparams),
        name="dispatch",
    )(plan, gaps, h, gate)

    def tile_rows(width):
        return pl.BlockSpec((er, width), lambda j, tg, nv: (jnp.minimum(j, nv[0] - 1), 0))

    def group_w(shape):
        return pl.BlockSpec((EXPERTS_PER_GROUP,) + shape, lambda j, tg, nv: (tg[jnp.minimum(j, nv[0] - 1)], 0, 0))

    ys = pl.pallas_call(
        _experts_kernel,
        grid_spec=pltpu.PrefetchScalarGridSpec(
            num_scalar_prefetch=2, grid=(max_tiles,),
            in_specs=[tile_rows(SORTED_WIDTH), group_w((D_MODEL, D_EXPERT)),
                      group_w((D_MODEL, D_EXPERT)), group_w((D_EXPERT, D_MODEL))],
            out_specs=pl.BlockSpec((er, D_MODEL), lambda j, tg, nv: (j, 0)),
            scratch_shapes=[pltpu.VMEM((EXPERTS_PER_GROUP, D_MODEL, D_EXPERT), BF16),
                            pltpu.VMEM((EXPERTS_PER_GROUP, D_MODEL, D_EXPERT), BF16),
                            pltpu.VMEM((EXPERTS_PER_GROUP, D_EXPERT, D_MODEL), BF16)]),
        out_shape=jax.ShapeDtypeStruct((cap, D_MODEL), BF16),
        compiler_params=pltpu.CompilerParams(dimension_semantics=("arbitrary",), **cparams),
        name="experts",
    )(tile_group, n_valid, hs, W["expert_w1"], W["expert_w3"], W["expert_w2"])

    def full(a):
        return pl.BlockSpec(a.shape, lambda i, plan: (0,) * a.ndim)

    params = [W["ln2_g"], W["ln2_b"], W["w_ple"], W["w_ple_gate"], W["ple_norm_g"]]
    group_tiles = tuple(pg.shape[0] // tm for pg in p_groups)
    assert sum(group_tiles) == ntiles
    starts = [sum(group_tiles[:k]) for k in range(len(group_tiles))]
    return pl.pallas_call(
        functools.partial(_combine_kernel, group_tiles),
        grid_spec=pltpu.PrefetchScalarGridSpec(
            num_scalar_prefetch=1, grid=(ntiles,),
            in_specs=[pl.BlockSpec((tm, D_MODEL), lambda i, plan: (i, 0)),
                      pl.BlockSpec((tm, ROUTER_LANES), lambda i, plan: (i, 0))]
            + [_group_rows(tm, D_PLE, st, t) for st, t in zip(starts, group_tiles)]
            + [any_spec] + [full(a) for a in params],
            out_specs=[_group_rows(tm, D_MODEL, st, t) for st, t in zip(starts, group_tiles)],
            scratch_shapes=[pltpu.VMEM((2, SORT_LOCAL, D_MODEL), BF16), pltpu.SemaphoreType.DMA((2,))]),
        out_shape=[jax.ShapeDtypeStruct((t * tm, D_MODEL), F32) for t in group_tiles],
        compiler_params=pltpu.CompilerParams(dimension_semantics=("arbitrary",), **cparams),
        name="combine",
    )(plan, h, pos, *p_groups, ys, *params)


def _mix(x, x_prev, wkv0, ret0, pos0, W):
    n_seq, seq_len, _ = x.shape
    x2 = x.reshape(n_seq * seq_len, D_MODEL)
    r, lw, k, v, al, be, g, bonus, qb, kb, vb, gb = _proj(x2, x_prev, seq_len, pos0, W)
    ya, ob, wkv1, ret1 = _mixer((r, lw, k, v, al, be, qb, kb, vb), wkv0, ret0, n_seq, seq_len)
    return (ya, ob, bonus, g, gb, x2), wkv1, ret1


def _prep_weights(i, w_in, mu_shift, w_decay_up, decay_base, w_aaa_up, aaa_base, w_gate_up, k_k, k_a, r_k,
                  lnx_g, lnx_b, ret_gn_g, ret_gn_b, w_out, ln1_g, ln1_b,
                  router_coarse_w, router_coarse_b, router_fine_w, router_fine_b,
                  expert_w1, expert_w3, expert_w2, ln2_g, ln2_b, w_ple, w_ple_gate, ple_norm_g):
    def row(a):
        return a[i].reshape(1, -1).astype(F32)

    pad = ROUTE_ROWS - N_GROUPS - N_EXPERTS
    w_router = jnp.concatenate([router_coarse_w[i].T, router_fine_w[i].T, jnp.zeros((pad, D_MODEL), F32)], axis=0)
    b_router = jnp.concatenate([router_coarse_b[i], router_fine_b[i], jnp.zeros((pad,), F32)]).reshape(-1, 1)
    return {
        "w_in": w_in[i].astype(BF16), "mu_shift": row(mu_shift), "w_decay_up": w_decay_up[i].astype(BF16),
        "decay_base": row(decay_base), "w_aaa_up": w_aaa_up[i].astype(BF16), "aaa_base": row(aaa_base),
        "w_gate_up": w_gate_up[i].astype(BF16), "k_k": row(k_k), "k_a": row(k_a), "r_k": row(r_k),
        "lnx_g": row(lnx_g), "lnx_b": row(lnx_b), "ret_gn_g": row(ret_gn_g), "ret_gn_b": row(ret_gn_b),
        "w_out": w_out[i].astype(BF16), "ln1_g": row(ln1_g), "ln1_b": row(ln1_b),
        "w_router": w_router, "b_router": b_router,
        "expert_w1": expert_w1[i], "expert_w3": expert_w3[i], "expert_w2": expert_w2[i], "ln2_g": row(ln2_g), "ln2_b": row(ln2_b),
        "w_ple": w_ple[i].astype(BF16), "w_ple_gate": w_ple_gate[i].astype(BF16), "ple_norm_g": row(ple_norm_g),
    }


def kernel(x_prompt, x_sample, p_prompt, p_sample, state_wkv, state_shift, state_ret, w_in, mu_shift, w_decay_up, decay_base, w_aaa_up, aaa_base, w_gate_up, k_k, k_a, r_k, lnx_g, lnx_b, ret_gn_g, ret_gn_b, w_out, ln1_g, ln1_b, router_coarse_w, router_coarse_b, router_fine_w, router_fine_b, expert_w1, expert_w3, expert_w2, ln2_g, ln2_b, w_ple, w_ple_gate, ple_norm_g):
    yp, ys = x_prompt, x_sample
    nb = x_prompt.shape[0]
    depth = w_in.shape[0]
    wkv_p, shift_p, ret_p, wkv_s, shift_s, ret_s = [], [], [], [], [], []
    for i in range(depth):
        W = _prep_weights(i, w_in, mu_shift, w_decay_up, decay_base, w_aaa_up, aaa_base, w_gate_up, k_k, k_a, r_k,
                          lnx_g, lnx_b, ret_gn_g, ret_gn_b, w_out, ln1_g, ln1_b,
                          router_coarse_w, router_coarse_b, router_fine_w, router_fine_b,
                          expert_w1, expert_w3, expert_w2, ln2_g, ln2_b, w_ple, w_ple_gate, ple_norm_g)
        ops_p, wp, rp = _mix(yp, jnp.zeros((nb, D_MODEL), F32), jnp.zeros((nb, A_HEADS, A_HEAD_DIM, A_HEAD_DIM), F32),
                             jnp.zeros((nb, B_HEADS, B_QK_DIM, B_V_DIM), F32), 0, W)
        ops_s, wsm, rsm = _mix(ys, state_shift[i], state_wkv[i], state_ret[i], PAST_LEN, W)
        sp, ss = yp[:, -1], ys[:, -1]
        h, gate, counts = _post([ops_p, ops_s], W)
        out_p, out_s = _ffn(h, gate, counts, [p_prompt[i].reshape(-1, D_PLE), p_sample[i].reshape(-1, D_PLE)], W)
        yp, ys = out_p.reshape(yp.shape), out_s.reshape(ys.shape)
        wkv_p.append(wp); shift_p.append(sp); ret_p.append(rp)
        wkv_s.append(wsm); shift_s.append(ss); ret_s.append(rsm)
    return (yp, ys, jnp.stack(wkv_p, 0), jnp.stack(shift_p, 0), jnp.stack(ret_p, 0),
            jnp.stack(wkv_s, 0), jnp.stack(shift_s, 0), jnp.stack(ret_s, 0))
```

```python
import functools
import math

import numpy as np
import jax
import jax.numpy as jnp
from jax import lax
from jax.experimental import pallas as pl
from jax.experimental.pallas import tpu as pltpu

F32 = jnp.float32
BF16 = jnp.bfloat16

D_MODEL = 1024
D_PLE = 256
A_HEADS = 8
A_HEAD_DIM = 64
A_WIDTH = A_HEADS * A_HEAD_DIM
DECAY_LORA = 64
AAA_LORA = 64
GATE_LORA = 128
GN_EPS_RWKV = 64e-5
B_HEADS = 4
B_QK_DIM = 64
B_V_DIM = 128
B_QK_WIDTH = B_HEADS * B_QK_DIM
B_WIDTH = B_HEADS * B_V_DIM
ROPE_BASE = 10000.0
GN_EPS = 1e-5
SHIFT_WIDTH = 3 * A_WIDTH + DECAY_LORA + AAA_LORA + GATE_LORA
IN_WIDTH = SHIFT_WIDTH + 2 * B_QK_WIDTH + 2 * B_WIDTH
N_GROUPS = 4
EXPERTS_PER_GROUP = 4
N_EXPERTS = N_GROUPS * EXPERTS_PER_GROUP
D_EXPERT = 256
DEPTH = 1
PAST_LEN = 16384
DEEPNORM_ALPHA = (2 * DEPTH) ** 0.25
LN_EPS = 1e-5

V7X_LANES = 128
V7X_SUBLANES = 8
V7X_MXU_WIDTH = 256
V7X_VMEM_LIMIT_BYTES = 56 * 1024 * 1024

PROJ_ROWS = 1024
LOG_DECAY_OUT = 1
CHUNK_ROWS = 64
MIXER_TILES = 8
MIXER_TILES_PACKED = 2
POST_ROWS = 512
SORT_ROWS = 512
SORT_ALIGN = 16
SORT_LOCAL = 640
EXPERT_ROWS = 1024
EXPERT_ROWS_SHORT = 256
GAP_BITS = 6
RUN_BITS = 6
GROUP_LANE = 0
ROUTER_LANES = V7X_LANES
ROUTE_ROWS = 32
FINE_LANE0 = N_GROUPS
SORTED_WIDTH = D_MODEL + 3 * ROUTER_LANES


def _dot(a, b):
    return jnp.dot(a.astype(BF16), b.astype(BF16), preferred_element_type=F32)


def _dot_nt(a, b):
    return lax.dot_general(a.astype(BF16), b.astype(BF16), (((1,), (1,)), ((), ())), preferred_element_type=F32)


def _dot_tn(a, b):
    return lax.dot_general(a.astype(BF16), b.astype(BF16), (((0,), (0,)), ((), ())), preferred_element_type=F32)


def _split2(x):
    hi = x.astype(BF16)
    lo = (x - hi.astype(F32)).astype(BF16)
    return hi, lo


def _split3(x):
    hi = x.astype(BF16)
    r1 = x - hi.astype(F32)
    mid = r1.astype(BF16)
    lo = (r1 - mid.astype(F32)).astype(BF16)
    return hi, mid, lo


def _sigmoid(x):
    return 1.0 / (1.0 + jnp.exp(-x))


def _group_ones(group):
    assert V7X_MXU_WIDTH % group == 0
    r = lax.broadcasted_iota(jnp.int32, (V7X_MXU_WIDTH, V7X_MXU_WIDTH), 0) // group
    c = lax.broadcasted_iota(jnp.int32, (V7X_MXU_WIDTH, V7X_MXU_WIDTH), 1) // group
    return jnp.where(r == c, 1.0, 0.0).astype(BF16)


def _group_sum(x, ones):
    xb, w = x.astype(BF16), ones.shape[0]
    assert x.shape[1] % w == 0
    return jnp.concatenate([jnp.dot(xb[:, j:j + w], ones, preferred_element_type=F32)
                            for j in range(0, x.shape[1], w)], axis=1)


def _proj_kernel(carry_mode, seq_len, tiles_per_seq,
                 x_ref, xp_ref, w_ref, mu_ref, wdec_ref, dbase_ref, waaa_ref, abase_ref, wgate_ref,
                 kk_ref, ka_ref, rk_ref, cos_ref, sin_ref,
                 r_o, lw_o, k_o, v_o, al_o, be_o, g_o, bonus_o, qb_o, kb_o, vb_o, gb_o,
                 carry_scr):
    tm = x_ref.shape[0]
    if carry_mode:
        xp = jnp.broadcast_to(xp_ref[0], (V7X_SUBLANES, D_MODEL))
        xb = jnp.concatenate([x_ref[...], xp], axis=0).astype(BF16)
        j = pl.program_id(0) % tiles_per_seq

        @pl.when(pl.program_id(0) == 0)
        def _():
            carry_scr[...] = jnp.zeros_like(carry_scr)
    else:
        xb = x_ref[...].astype(BF16)
        xpb = xp_ref[...].astype(BF16)

    def project(lo, hi):
        return jnp.dot(xb, w_ref[:, lo:hi], preferred_element_type=F32)

    def shifted(p, lo, hi):
        cur = p[:tm]
        row = lax.broadcasted_iota(jnp.int32, cur.shape, 0)
        rolled = pltpu.roll(cur, 1, 0)
        if carry_mode:
            first = jnp.where(j == 0, p[tm + V7X_SUBLANES - 1:], carry_scr[V7X_SUBLANES - 1:V7X_SUBLANES, lo:hi])
            prev = jnp.where(row == 0, first, rolled)
            carry_scr[:, lo:hi] = cur[tm - V7X_SUBLANES:]
        else:
            first = jnp.dot(xpb, w_ref[:, lo:hi], preferred_element_type=F32)
            prev = jnp.where((row & (seq_len - 1)) == 0, first, rolled)
        return cur + (prev - cur) * mu_ref[:, lo:hi]

    c_r, c_k, c_v, c_l = 0, A_WIDTH, 2 * A_WIDTH, 3 * A_WIDTH
    p_lora = project(c_l, SHIFT_WIDTH)
    p_k = project(c_k, c_v)
    p_r = project(c_r, c_k)

    lora = shifted(p_lora, c_l, SHIFT_WIDTH)
    w_lo = lora[:, :DECAY_LORA]
    a_lo = lora[:, DECAY_LORA:DECAY_LORA + AAA_LORA]
    g_lo = lora[:, DECAY_LORA + AAA_LORA:]
    z = -(dbase_ref[...] + _dot(jnp.tanh(w_lo), wdec_ref[...]))
    softplus = jnp.maximum(z, 0.0) + jnp.log(1.0 + jnp.exp(-jnp.abs(z)))
    log_w = -softplus - 0.5
    lw_o[...] = -jnp.exp(log_w)
    a = _sigmoid(abase_ref[...] + _dot(a_lo, waaa_ref[...]))
    g_o[...] = (_dot(_sigmoid(g_lo), wgate_ref[...])).astype(g_o.dtype)

    p_v = project(c_v, c_l)

    ones64 = _group_ones(A_HEAD_DIM)
    k0 = shifted(p_k, c_k, c_v)
    kk0 = k0 * kk_ref[...]
    ssq = _group_sum(kk0 * kk0, ones64)
    kk = kk0 * jnp.minimum(lax.rsqrt(ssq), 1e12)
    k = k0 * (1.0 + (a - 1.0) * ka_ref[...])
    k_o[...] = (k).astype(k_o.dtype)
    al_o[...] = (-kk).astype(al_o.dtype)
    be_o[...] = (kk * a).astype(be_o.dtype)

    o = SHIFT_WIDTH
    p_qk = project(o, o + 2 * B_QK_WIDTH)[:tm]

    r = shifted(p_r, c_r, c_k)
    r_o[...] = (r).astype(r_o.dtype)
    rk_sum = _group_sum(r * k * rk_ref[...], ones64)

    p_vb = project(o + 2 * B_QK_WIDTH, o + 2 * B_QK_WIDTH + B_WIDTH)[:tm]

    v = shifted(p_v, c_v, c_l)
    v_o[...] = (v).astype(v_o.dtype)
    bonus_o[...] = (rk_sum * v).astype(bonus_o.dtype)

    p_gb = project(o + 2 * B_QK_WIDTH + B_WIDTH, IN_WIDTH)[:tm]

    q_b = p_qk[:, :B_QK_WIDTH]
    k_b = p_qk[:, B_QK_WIDTH:]
    lane = lax.broadcasted_iota(jnp.int32, (tm, B_QK_WIDTH), 1)
    first_half = (lane & (B_QK_DIM - 1)) < (B_QK_DIM // 2)
    cos = cos_ref[...]
    sin = sin_ref[...]

    def rot(t):
        swapped = jnp.where(first_half, pltpu.roll(t, B_QK_WIDTH - B_QK_DIM // 2, 1), pltpu.roll(t, B_QK_DIM // 2, 1))
        return t * cos + swapped * sin

    qb_o[...] = (rot(q_b)).astype(qb_o.dtype)
    kb_o[...] = (rot(k_b) * (B_QK_DIM ** -0.5)).astype(kb_o.dtype)

    vb_o[...] = (p_vb).astype(vb_o.dtype)
    gb_o[...] = (p_gb * _sigmoid(p_gb)).astype(gb_o.dtype)


def _proj(x2, x_prev, seq_len, pos0, W):
    n = x2.shape[0]
    tm = PROJ_ROWS
    assert n % tm == 0
    carry_mode = seq_len % tm == 0
    if carry_mode:
        tiles_per_seq = seq_len // tm
        xp = x_prev.reshape(-1, 1, D_MODEL)
        xp_spec = pl.BlockSpec((1, 1, D_MODEL), lambda i: (i // tiles_per_seq, 0, 0))
        tab_rows = seq_len
    else:
        assert tm % seq_len == 0 and seq_len & (seq_len - 1) == 0
        tiles_per_seq = 1
        xp = jnp.repeat(x_prev, seq_len, axis=0)
        xp_spec = pl.BlockSpec((tm, D_MODEL), lambda i: (i, 0))
        tab_rows = tm
    half = B_QK_DIM // 2
    inv = ROPE_BASE ** (-jnp.arange(half, dtype=F32) / half)
    pos = (pos0 + jnp.arange(seq_len, dtype=jnp.int32)).astype(F32)
    ang = pos[:, None] * inv[None, :]
    cos = jnp.tile(jnp.concatenate([jnp.cos(ang), jnp.cos(ang)], -1), (tab_rows // seq_len, B_HEADS))
    sin = jnp.tile(jnp.concatenate([-jnp.sin(ang), jnp.sin(ang)], -1), (tab_rows // seq_len, B_HEADS))
    tab_tiles = tab_rows // tm
    tab_spec = pl.BlockSpec((tm, B_QK_WIDTH), lambda i: (i % tab_tiles, 0))

    def full(a):
        return pl.BlockSpec(a.shape, lambda i: (0,) * a.ndim)

    def rows(width):
        return pl.BlockSpec((tm, width), lambda i: (i, 0))

    params = [W["w_in"], W["mu_shift"], W["w_decay_up"], W["decay_base"], W["w_aaa_up"], W["aaa_base"],
              W["w_gate_up"], W["k_k"], W["k_a"], W["r_k"]]
    widths = [A_WIDTH] * 8 + [B_QK_WIDTH, B_QK_WIDTH, B_WIDTH, B_WIDTH]
    outs = pl.pallas_call(
        functools.partial(_proj_kernel, carry_mode, seq_len, tiles_per_seq),
        grid=(n // tm,),
        in_specs=[rows(D_MODEL), xp_spec, pl.BlockSpec(params[0].shape, lambda i: (0, 0), pipeline_mode=pl.Buffered(1))]
        + [full(a) for a in params[1:]] + [tab_spec, tab_spec],
        out_specs=[rows(w) for w in widths],
        out_shape=[jax.ShapeDtypeStruct((n, w), F32 if i == LOG_DECAY_OUT else BF16) for i, w in enumerate(widths)],
        scratch_shapes=[pltpu.VMEM((V7X_SUBLANES, SHIFT_WIDTH), F32)],
        compiler_params=pltpu.CompilerParams(dimension_semantics=("arbitrary",),
                                             vmem_limit_bytes=V7X_VMEM_LIMIT_BYTES),
        name="proj",
    )(x2, xp, *params, cos, sin)
    return outs


def _mixer_kernel(nb, seqs, clen,
                  r_ref, lw_ref, k_ref, v_ref, al_ref, be_ref, qb_ref, kb_ref, vb_ref, wkv0_ref, ret0_ref,
                  ya_o, ob_o, wkv_o, ret_o, s_scr, r_scr):
    R = seqs * clen
    log2c = int(math.log2(clen))
    c_idx = pl.program_id(1)
    hd = A_HEAD_DIM
    pw2 = 2 * hd
    TP = [(t, j) for t in range(nb) for j in range(A_HEADS // 2)]
    TG = [(t, h) for t in range(nb) for h in range(B_HEADS)]

    @pl.when(c_idx == 0)
    def _():
        zero = jnp.zeros((hd, hd), F32)
        for ci, (t, j) in enumerate(TP):
            blocks = [jnp.concatenate([jnp.concatenate([wkv0_ref[t * seqs + i, 2 * j], zero], axis=1),
                                       jnp.concatenate([zero, wkv0_ref[t * seqs + i, 2 * j + 1]], axis=1)], axis=0)
                      for i in range(seqs)]
            s_scr[ci] = jnp.concatenate(blocks, axis=1) if seqs > 1 else blocks[0]
        for t, h in TG:
            r_scr[t * B_HEADS + h] = ret0_ref[t * seqs:(t + 1) * seqs, h].reshape(seqs * B_QK_DIM, B_V_DIM)

    row = lax.broadcasted_iota(jnp.int32, (R, R), 0)
    col = lax.broadcasted_iota(jnp.int32, (R, R), 1)
    same = (row >> log2c) == (col >> log2c)
    incl = same & (col <= row)
    m_incl = jnp.where(incl, 1.0, 0.0).astype(BF16)
    m_same = jnp.where(same, 1.0, 0.0).astype(BF16)

    def expand(t):
        if seqs == 1:
            return t
        w = t.shape[1]
        wide = jnp.concatenate([t] * seqs, axis=1)
        rr = lax.broadcasted_iota(jnp.int32, wide.shape, 0) >> log2c
        cc = lax.broadcasted_iota(jnp.int32, wide.shape, 1) // w
        return jnp.where(rr == cc, wide, 0.0)

    a_bar, r_bar, b_til, k_til, b_dec, k_dec, d_end, vv = [], [], [], [], [], [], [], []
    for t in range(nb):
        lw = lw_ref[t]
        parts = _split2(lw)
        c = sum(jnp.dot(m_incl, p, preferred_element_type=F32) for p in parts)
        if seqs == 1:
            is_last = lax.broadcasted_iota(jnp.int32, c.shape, 0) == R - 1
            cend = jnp.sum(jnp.where(is_last, c, 0.0), axis=0, keepdims=True)
        else:
            cend = sum(jnp.dot(m_same, p, preferred_element_type=F32) for p in parts)
        einv = jnp.exp(-c)
        edec = jnp.exp(cend - c)
        a_bar.append(al_ref[t] * jnp.exp(c - lw))
        r_bar.append(r_ref[t] * jnp.exp(c))
        b_til.append(be_ref[t] * einv)
        k_til.append(k_ref[t] * einv)
        b_dec.append(be_ref[t] * edec)
        k_dec.append(k_ref[t] * edec)
        d_end.append(jnp.exp(cend))
        vv.append(v_ref[t].astype(F32))
    last_row = (lax.broadcasted_iota(jnp.int32, (R, seqs * pw2), 0) & (clen - 1)) == clen - 1

    lane2 = lax.broadcasted_iota(jnp.int32, (R, pw2), 1)
    first = lane2 < hd
    row2 = lax.broadcasted_iota(jnp.int32, (R, pw2), 0)
    col2 = lane2 & (hd - 1)
    same2 = (row2 >> log2c) == (col2 >> log2c)
    incl2 = same2 & (col2 <= row2)
    strict2 = same2 & (col2 < row2)
    eye2 = jnp.where(row2 == col2, 1.0, 0.0).astype(F32)

    def pair(x, j):
        return x[:, j * pw2:(j + 1) * pw2]

    def keep(x, second):
        m = first if x.shape[1] == pw2 else jnp.concatenate([first] * (x.shape[1] // pw2), axis=1)
        return jnp.where(m != second, x, 0.0)

    def blockdiag(y, swapped=False):
        return jnp.concatenate([keep(y, swapped), keep(y, not swapped)], axis=0)

    ams = []
    for t, j in TP:
        lhs = jnp.concatenate([pair(a_bar[t], j), pair(r_bar[t], j)], axis=0)
        bt, kt = pair(b_til[t], j), pair(k_til[t], j)
        lhs0 = jnp.where(jnp.concatenate([first, first], axis=0), lhs, 0.0)
        lhs1 = jnp.where(jnp.concatenate([first, first], axis=0), 0.0, lhs)
        ams.append((_dot_nt(lhs0, jnp.concatenate([bt, kt], axis=0)), _dot_nt(lhs1, jnp.concatenate([kt, bt], axis=0))))

    lgs = [float(np.log1p(-np.exp2(-5.0 - h))) for h in range(B_HEADS)]
    qb = [qb_ref[t].astype(F32) for t in range(nb)]
    kb = [kb_ref[t].astype(F32) for t in range(nb)]
    qs = [qb[t][:, h * B_QK_DIM:(h + 1) * B_QK_DIM] for t, h in TG]
    khs = [kb[t][:, h * B_QK_DIM:(h + 1) * B_QK_DIM] for t, h in TG]
    vbs = [vb_ref[t][:, h * B_V_DIM:(h + 1) * B_V_DIM] for t, h in TG]
    diff = (row - col).astype(F32)
    pos_v = (lax.broadcasted_iota(jnp.int32, (R, B_V_DIM), 0) & (clen - 1)).astype(F32)
    pos_k = (lax.broadcasted_iota(jnp.int32, (R, B_QK_DIM), 0) & (clen - 1)).astype(F32)
    intra = [jnp.where(incl, jnp.exp(lg * diff), 0.0) for lg in lgs]
    cross = [jnp.exp(lg * (pos_v + 1.0)) for lg in lgs]
    kdec = [jnp.exp(lg * (clen - 1.0 - pos_k)) for lg in lgs]
    rstates = [r_scr[t * B_HEADS + h] for t, h in TG]
    scs = [_dot_nt(qs[i], khs[i]) * intra[h] for i, (t, h) in enumerate(TG)]
    qst = [_dot(expand(qs[i]), rstates[i]) * cross[h] for i, (t, h) in enumerate(TG)]

    n = len(TP)
    a_ab = [jnp.where(strict2, jnp.where(first, m0[:R], m1[:R]), 0.0) for m0, m1 in ams]
    a_ak = [jnp.where(strict2, jnp.where(first, m1[:R], m0[:R]), 0.0) for m0, m1 in ams]
    a_rb = [jnp.where(incl2, jnp.where(first, m0[R:], m1[R:]), 0.0) for m0, m1 in ams]
    a_rk = [jnp.where(incl2, jnp.where(first, m1[R:], m0[R:]), 0.0) for m0, m1 in ams]
    vps = [pair(vv[t], j) for t, j in TP]
    av = [_dot(a_ak[i], blockdiag(vps[i], swapped=True)) for i in range(n)]
    tinv = [eye2 + a for a in a_ab]
    if log2c > 1:
        pw = [_dot(a, blockdiag(a)) for a in a_ab]
    for it in range(log2c - 1):
        if it < log2c - 2:
            tp = [_dot(pw[i], blockdiag(jnp.concatenate([tinv[i], pw[i]], axis=1))) for i in range(n)]
            tinv = [tinv[i] + tp[i][:, :pw2] for i in range(n)]
            pw = [tp[i][:, pw2:] for i in range(n)]
        else:
            tinv = [tinv[i] + _dot(pw[i], blockdiag(tinv[i])) for i in range(n)]

    os_ = [_dot(scs[i], vbs[i]) + qst[i] for i in range(len(TG))]
    for i, (t, h) in enumerate(TG):
        r_scr[t * B_HEADS + h] = (rstates[i] * float(np.exp(lgs[h] * clen))
                                  + _dot_tn(expand(khs[i] * kdec[h]), vbs[i]))
    for t in range(nb):
        ob_o[t] = jnp.concatenate(os_[t * B_HEADS:(t + 1) * B_HEADS], axis=1)

    wu = [_dot(tinv[i], blockdiag(jnp.concatenate([pair(a_bar[t], j), av[i]], axis=1)))
          for i, (t, j) in enumerate(TP)]
    states = [s_scr[i] for i in range(n)]
    ws = [_dot_nt(jnp.concatenate([expand(wu[i][:, :pw2]), expand(pair(r_bar[t], j))], axis=0), states[i])
          for i, (t, j) in enumerate(TP)]
    us = [ws[i][:R] + wu[i][:, pw2:] for i in range(n)]
    ys = [ws[i][R:] + _dot(jnp.concatenate([a_rb[i], a_rk[i]], axis=1),
                           jnp.concatenate([blockdiag(us[i]), blockdiag(vps[i], swapped=True)], axis=0))
          for i in range(n)]
    npair = A_HEADS // 2
    for t in range(nb):
        ya_o[t] = jnp.concatenate(ys[t * npair:(t + 1) * npair], axis=1)
    rowp = lax.broadcasted_iota(jnp.int32, (pw2, seqs * pw2), 0) >= hd
    lanep = (lax.broadcasted_iota(jnp.int32, (pw2, seqs * pw2), 1) & (pw2 - 1)) >= hd
    for i, (t, j) in enumerate(TP):
        d_row = jnp.sum(jnp.where(last_row, expand(pair(d_end[t], j)), 0.0), axis=0, keepdims=True)
        bk = jnp.concatenate([expand(pair(b_dec[t], j)), expand(pair(k_dec[t], j))], axis=0)
        upd = _dot_tn(jnp.concatenate([us[i], vps[i]], axis=0), bk)
        s_scr[i] = states[i] * d_row + jnp.where(rowp == lanep, upd, 0.0)

    @pl.when(c_idx == pl.num_programs(1) - 1)
    def _():
        for ci, (t, j) in enumerate(TP):
            st = s_scr[ci]
            for i in range(seqs):
                wkv_o[t * seqs + i, 2 * j] = st[:hd, i * pw2:i * pw2 + hd]
                wkv_o[t * seqs + i, 2 * j + 1] = st[hd:, i * pw2 + hd:(i + 1) * pw2]
        for t, h in TG:
            ret_o[t * seqs:(t + 1) * seqs, h] = r_scr[t * B_HEADS + h].reshape(seqs, B_QK_DIM, B_V_DIM)


def _mixer(ops, wkv0, ret0, n_seq, seq_len):
    n = ops[0].shape[0]
    R = CHUNK_ROWS
    if seq_len >= R:
        assert seq_len % R == 0
        seqs, clen, nchunks = 1, R, seq_len // R
    else:
        assert R % seq_len == 0 and seq_len & (seq_len - 1) == 0 and n_seq % (R // seq_len) == 0
        seqs, clen, nchunks = R // seq_len, seq_len, 1
    ntiles = n_seq // seqs
    nb = MIXER_TILES if seqs == 1 else MIXER_TILES_PACKED
    assert ntiles % nb == 0
    ops3 = [a.reshape(ntiles, nchunks * R, a.shape[1]) for a in ops]

    def rows(width):
        return pl.BlockSpec((nb, R, width), lambda i, c: (i, c, 0))

    wkv_spec = pl.BlockSpec((nb * seqs, A_HEADS, A_HEAD_DIM, A_HEAD_DIM), lambda i, c: (i, 0, 0, 0))
    ret_spec = pl.BlockSpec((nb * seqs, B_HEADS, B_QK_DIM, B_V_DIM), lambda i, c: (i, 0, 0, 0))
    ya, ob, wkv1, ret1 = pl.pallas_call(
        functools.partial(_mixer_kernel, nb, seqs, clen),
        grid=(ntiles // nb, nchunks),
        in_specs=[rows(A_WIDTH)] * 6 + [rows(B_QK_WIDTH), rows(B_QK_WIDTH), rows(B_WIDTH), wkv_spec, ret_spec],
        out_specs=[rows(A_WIDTH), rows(B_WIDTH), wkv_spec, ret_spec],
        out_shape=[jax.ShapeDtypeStruct((ntiles, nchunks * R, A_WIDTH), F32),
                   jax.ShapeDtypeStruct((ntiles, nchunks * R, B_WIDTH), F32),
                   jax.ShapeDtypeStruct(wkv0.shape, F32), jax.ShapeDtypeStruct(ret0.shape, F32)],
        scratch_shapes=[pltpu.VMEM((nb * A_HEADS // 2, 2 * A_HEAD_DIM, seqs * 2 * A_HEAD_DIM), F32),
                        pltpu.VMEM((nb * B_HEADS, seqs * B_QK_DIM, B_V_DIM), F32)],
        compiler_params=pltpu.CompilerParams(dimension_semantics=("parallel", "arbitrary"),
                                             vmem_limit_bytes=V7X_VMEM_LIMIT_BYTES),
        name="mixer",
    )(*ops3, wkv0, ret0)
    return ya.reshape(n, A_WIDTH), ob.reshape(n, B_WIDTH), wkv1, ret1


def _layer_norm(z, g, b):
    mu = jnp.mean(z, axis=-1, keepdims=True)
    d = z - mu
    var = jnp.mean(d * d, axis=-1, keepdims=True)
    return d * lax.rsqrt(var + LN_EPS) * g + b


def _post_tile(ya_ref, ob_ref, bonus_ref, g_ref, gb_ref, x_ref, lnxg_ref, lnxb_ref, rgg_ref, rgb_ref,
               wout_ref, ln1g_ref, ln1b_ref, wr_ref, br_ref, h_o, gate_o, cnt_o):
    tm = x_ref.shape[0]

    def head_norm(t, group, eps, gg, bb):
        ones = _group_ones(group)
        mu = _group_sum(t, ones) * (1.0 / group)
        d = t - mu
        var = _group_sum(d * d, ones) * (1.0 / group)
        return d * lax.rsqrt(var + eps) * gg + bb

    y_a = (head_norm(ya_ref[...], A_HEAD_DIM, GN_EPS_RWKV, lnxg_ref[...], lnxb_ref[...]) + bonus_ref[...]) * g_ref[...]
    y_b = head_norm(ob_ref[...], B_V_DIM, GN_EPS, rgg_ref[...], rgb_ref[...]) * gb_ref[...]
    y = jnp.concatenate([y_a, y_b], axis=1)
    mix = _dot(y, wout_ref[...])
    h = _layer_norm(DEEPNORM_ALPHA * x_ref[...] + mix, ln1g_ref[...], ln1b_ref[...])
    h_o[...] = h

    h_hi, h_lo = _split2(h)
    w_hi, w_lo = _split2(wr_ref[...])
    hi_terms = _dot_nt(jnp.concatenate([w_hi, w_lo], axis=0), h_hi)
    logits = (hi_terms[:ROUTE_ROWS] + hi_terms[ROUTE_ROWS:] + _dot_nt(w_hi, h_lo)) + br_ref[...]
    def first_max(vals):
        best = functools.reduce(jnp.maximum, vals)
        idx = float(len(vals) - 1)
        for j in range(len(vals) - 2, -1, -1):
            idx = jnp.where(vals[j] == best, float(j), idx)
        return best, idx

    coarse = [logits[g:g + 1] for g in range(N_GROUPS)]
    cmax, grp = first_max(coarse)
    gprob = 1.0 / sum(jnp.exp(c - cmax) for c in coarse)
    fine = []
    for e in range(EXPERTS_PER_GROUP):
        f = logits[FINE_LANE0 + e:FINE_LANE0 + e + 1]
        for g in range(1, N_GROUPS):
            r = FINE_LANE0 + g * EXPERTS_PER_GROUP + e
            f = jnp.where(grp == float(g), logits[r:r + 1], f)
        fine.append(f)
    m1, i1 = first_max(fine)
    m2, i2 = first_max([jnp.where(i1 == float(e), -jnp.inf, f) for e, f in enumerate(fine)])
    e2 = jnp.exp(m2 - m1)
    w1 = gprob / (1.0 + e2)
    w2 = gprob * e2 / (1.0 + e2)
    row = lax.broadcasted_iota(jnp.int32, (ROUTE_ROWS, tm), 0).astype(F32)
    row0 = FINE_LANE0 + grp * EXPERTS_PER_GROUP
    gate_t = (jnp.where(row == row0 + i1, w1, 0.0) + jnp.where(row == row0 + i2, w2, 0.0)
              + jnp.where(row == GROUP_LANE, grp, 0.0))
    gate = jnp.concatenate([gate_t, jnp.zeros((ROUTER_LANES - ROUTE_ROWS, tm), F32)], axis=0).T
    gate_o[...] = gate
    lane = lax.broadcasted_iota(jnp.int32, (tm, ROUTER_LANES), 1).astype(F32)
    onehot = jnp.where((lane == gate[:, GROUP_LANE:GROUP_LANE + 1]) & (lane < N_GROUPS), 1.0, 0.0)
    cnt_o[0] = jnp.broadcast_to(jnp.sum(onehot, axis=0, keepdims=True), (V7X_SUBLANES, ROUTER_LANES)).astype(jnp.int32)


POST_OPERANDS = 6


def _post_kernel(group_tiles, *refs):
    per_group, shared = refs[:POST_OPERANDS * len(group_tiles)], refs[POST_OPERANDS * len(group_tiles):]
    i = pl.program_id(0)
    start = 0
    for k, tiles in enumerate(group_tiles):
        @pl.when((i >= start) & (i < start + tiles))
        def _(k=k):
            _post_tile(*per_group[POST_OPERANDS * k:POST_OPERANDS * (k + 1)], *shared)

        start += tiles


def _group_rows(tm, width, start, tiles):
    return pl.BlockSpec((tm, width), lambda i, *_: (jnp.clip(i - start, 0, tiles - 1), 0))


def _post(groups, W):
    tm = POST_ROWS
    group_tiles = tuple(g[-1].shape[0] // tm for g in groups)
    assert all(g[-1].shape[0] % tm == 0 for g in groups)
    n = tm * sum(group_tiles)

    def full(a):
        return pl.BlockSpec(a.shape, lambda i: (0,) * a.ndim)

    def rows(width):
        return pl.BlockSpec((tm, width), lambda i: (i, 0))

    widths = [A_WIDTH, B_WIDTH, A_WIDTH, A_WIDTH, B_WIDTH, D_MODEL]
    in_specs, start = [], 0
    for tiles in group_tiles:
        in_specs += [_group_rows(tm, w, start, tiles) for w in widths]
        start += tiles
    params = [W["lnx_g"], W["lnx_b"], W["ret_gn_g"], W["ret_gn_b"], W["w_out"], W["ln1_g"], W["ln1_b"],
              W["w_router"], W["b_router"]]
    return pl.pallas_call(
        functools.partial(_post_kernel, group_tiles),
        grid=(n // tm,),
        in_specs=in_specs + [full(a) for a in params],
        out_specs=[rows(D_MODEL), rows(ROUTER_LANES),
                   pl.BlockSpec((1, V7X_SUBLANES, ROUTER_LANES), lambda i: (i, 0, 0))],
        out_shape=[jax.ShapeDtypeStruct((n, D_MODEL), F32), jax.ShapeDtypeStruct((n, ROUTER_LANES), F32),
                   jax.ShapeDtypeStruct((n // tm, V7X_SUBLANES, ROUTER_LANES), jnp.int32)],
        compiler_params=pltpu.CompilerParams(dimension_semantics=("parallel",),
                                             vmem_limit_bytes=V7X_VMEM_LIMIT_BYTES),
        name="post",
    )(*[a for g in groups for a in g], *params)


def _sort_positions(gate, lofs):
    tm = gate.shape[0]
    lane = lax.broadcasted_iota(jnp.int32, (tm, ROUTER_LANES), 1)
    grp = gate[:, GROUP_LANE:GROUP_LANE + 1].astype(jnp.int32)
    onehot = jnp.where((lane == grp) & (lane < N_GROUPS), 1.0, 0.0)
    r = lax.broadcasted_iota(jnp.int32, (tm, tm), 0)
    c = lax.broadcasted_iota(jnp.int32, (tm, tm), 1)
    earlier = jnp.where(c < r, 1.0, 0.0).astype(BF16)
    prefix = jnp.dot(earlier, onehot.astype(BF16), preferred_element_type=F32)
    base = jnp.zeros((tm, ROUTER_LANES), F32)
    for g in range(N_GROUPS):
        base = jnp.where(lane == g, lofs[g].astype(F32), base)
    return jnp.sum(onehot * (base + prefix), axis=1, keepdims=True).astype(jnp.int32)


def _piece_copies(action, rows, bits, copy_of, priority=0):
    k = rows // SORT_ALIGN
    for b in reversed(range(bits)):
        size = SORT_ALIGN << b

        @pl.when(((k >> b) & 1) == 1)
        def _():
            done = ((k >> (b + 1)) << (b + 1)) * SORT_ALIGN
            cp = copy_of(done, size)
            cp.start(priority=priority) if action == "start" else cp.wait()


def _run_copies(action, plan_ref, i, src_of, dst_of, sem):
    for g in range(N_GROUPS):
        goff = plan_ref[i, g]
        lofs = plan_ref[i, 2 * N_GROUPS + g]

        def copy_of(done, size, goff=goff, lofs=lofs):
            lo = pl.multiple_of(lofs + done, SORT_ALIGN)
            go = pl.multiple_of(goff + done, SORT_ALIGN)
            return pltpu.make_async_copy(src_of(lo, go, size), dst_of(lo, go, size), sem)

        _piece_copies(action, plan_ref[i, N_GROUPS + g], RUN_BITS, copy_of, priority=g % 2)


def _dispatch_kernel(plan_ref, gap_ref, h_ref, gate_ref, hs_o, pos_o, h_loc, sem):
    i = pl.program_id(0)
    tm = h_ref.shape[0]
    lofs = [plan_ref[i, 2 * N_GROUPS + g] for g in range(N_GROUPS)]
    pos = _sort_positions(gate_ref[...], lofs)
    pos_o[...] = jnp.broadcast_to(pos, pos_o.shape)
    onehot_t = jnp.where(lax.broadcasted_iota(jnp.int32, (tm, SORT_LOCAL), 1) == pos, 1.0, 0.0).astype(BF16)
    payload = jnp.concatenate([h_ref[...].astype(BF16)] + list(_split3(gate_ref[...])), axis=1)
    sorted_rows = _dot_tn(onehot_t, payload).astype(BF16)

    def copies(action, tile):
        _run_copies(action, plan_ref, tile, lambda lo, go, sz: h_loc.at[pl.ds(lo, sz)],
                    lambda lo, go, sz: hs_o.at[pl.ds(go, sz)], sem)

    @pl.when(i > 0)
    def _():
        copies("wait", i - 1)

    h_loc[...] = sorted_rows
    copies("start", i)

    @pl.when(i == pl.num_programs(0) - 1)
    def _():
        copies("wait", i)
        h_loc[...] = jnp.zeros_like(h_loc)
        for action in ("start", "wait"):
            for g in range(N_GROUPS):
                def copy_of(done, size, g=g):
                    go = pl.multiple_of(gap_ref[g] + done, SORT_ALIGN)
                    return pltpu.make_async_copy(h_loc.at[pl.ds(0, size)], hs_o.at[pl.ds(go, size)], sem)

                _piece_copies(action, gap_ref[N_GROUPS + g], GAP_BITS, copy_of)

            def body(k, carry):
                go = pl.multiple_of(gap_ref[2 * N_GROUPS] + k * EXPERT_ROWS_SHORT, SORT_ALIGN)
                cp = pltpu.make_async_copy(h_loc.at[pl.ds(0, EXPERT_ROWS_SHORT)], hs_o.at[pl.ds(go, EXPERT_ROWS_SHORT)], sem)
                cp.start() if action == "start" else cp.wait()
                return carry

            lax.fori_loop(0, gap_ref[2 * N_GROUPS + 1] // EXPERT_ROWS_SHORT, body, 0)


def _experts_kernel(tile_group_ref, n_valid_ref, hs_ref, w1_ref, w3_ref, w2_ref, ys_o, w1_b, w3_b, w2_b):
    j = pl.program_id(0)
    last = n_valid_ref[0] - 1
    g = tile_group_ref[jnp.minimum(j, last)]
    g_prev = tile_group_ref[jnp.minimum(jnp.maximum(j - 1, 0), last)]

    @pl.when((j == 0) | (g != g_prev))
    def _():
        w1_b[...] = w1_ref[...].astype(BF16)
        w3_b[...] = w3_ref[...].astype(BF16)
        w2_b[...] = w2_ref[...].astype(BF16)

    @pl.when(j <= last)
    def _():
        x = hs_ref[:, :D_MODEL]
        gs = sum(hs_ref[:, D_MODEL + t * ROUTER_LANES:D_MODEL + (t + 1) * ROUTER_LANES].astype(F32) for t in range(3))
        lane = lax.broadcasted_iota(jnp.int32, gs.shape, 1)
        acc = jnp.zeros(ys_o.shape, F32)
        for e in range(EXPERTS_PER_GROUP):
            ge = jnp.sum(jnp.where(lane == FINE_LANE0 + g * EXPERTS_PER_GROUP + e, gs, 0.0), axis=-1, keepdims=True)
            a = jnp.dot(x, w1_b[e], preferred_element_type=F32)
            b = jnp.dot(x, w3_b[e], preferred_element_type=F32)
            hid = (a * _sigmoid(a)) * b * ge
            acc = acc + jnp.dot(hid.astype(BF16), w2_b[e], preferred_element_type=F32)
        ys_o[...] = acc.astype(ys_o.dtype)

    @pl.when(j > last)
    def _():
        ys_o[...] = jnp.zeros_like(ys_o)


def _combine_kernel(group_tiles, plan_ref, h_ref, pos_ref, *refs):
    ng = len(group_tiles)
    p_refs, (ys_ref, ln2g_ref, ln2b_ref, wple_ref, wpg_ref, pleg_ref) = refs[:ng], refs[ng:ng + 6]
    o_refs, (y_loc, sem) = refs[ng + 6:2 * ng + 6], refs[2 * ng + 6:]
    i = pl.program_id(0)
    starts = [sum(group_tiles[:k]) for k in range(ng)]

    def in_group(k):
        return (i >= starts[k]) & (i < starts[k] + group_tiles[k])

    tm = h_ref.shape[0]

    def fetch(tile):
        slot = tile % 2
        y_loc[slot] = jnp.zeros(y_loc.shape[1:], y_loc.dtype)
        _run_copies("start", plan_ref, tile, lambda lo, go, sz: ys_ref.at[pl.ds(go, sz)],
                    lambda lo, go, sz: y_loc.at[slot, pl.ds(lo, sz)], sem.at[slot])

    @pl.when(i == 0)
    def _():
        fetch(i)

    @pl.when(i + 1 < pl.num_programs(0))
    def _():
        fetch(i + 1)

    pos = pos_ref[:, 0:1]
    onehot_t = jnp.where(lax.broadcasted_iota(jnp.int32, (tm, SORT_LOCAL), 1) == pos, 1.0, 0.0).astype(BF16)
    p_tile = p_refs[0][...]
    for k in range(1, ng):
        p_tile = jnp.where(in_group(k), p_refs[k][...], p_tile)
    ple_in = _dot(p_tile, wple_ref[...])
    slot = i % 2
    _run_copies("wait", plan_ref, i, lambda lo, go, sz: ys_ref.at[pl.ds(go, sz)],
                lambda lo, go, sz: y_loc.at[slot, pl.ds(lo, sz)], sem.at[slot])
    ffn = jnp.dot(onehot_t, y_loc[slot], preferred_element_type=F32)
    h2 = _layer_norm(DEEPNORM_ALPHA * h_ref[...] + ffn, ln2g_ref[...], ln2b_ref[...])
    ple = ple_in * _sigmoid(_dot(h2, wpg_ref[...]))
    ms = jnp.mean(ple * ple, axis=-1, keepdims=True)
    out = h2 + ple * lax.rsqrt(ms + LN_EPS) * pleg_ref[...]
    for k in range(ng):
        @pl.when(in_group(k))
        def _(k=k):
            o_refs[k][...] = out


def _ffn(h, gate, counts, p_groups, W):
    n = h.shape[0]
    tm = SORT_ROWS
    assert n % tm == 0 and tm == POST_ROWS and SORT_LOCAL >= tm + N_GROUPS * SORT_ALIGN
    ntiles = n // tm
    er = EXPERT_ROWS if n >= 2 * N_GROUPS * EXPERT_ROWS else EXPERT_ROWS_SHORT
    assert er <= SORT_ALIGN << GAP_BITS and SORT_ALIGN << (GAP_BITS - 1) <= SORT_LOCAL
    cnt = counts[:, 0, :N_GROUPS]
    run = (cnt + SORT_ALIGN - 1) // SORT_ALIGN * SORT_ALIGN
    lofs = jnp.cumsum(run, axis=1) - run
    seg = (jnp.sum(run, axis=0) + er - 1) // er * er
    gbase = jnp.cumsum(seg) - seg
    goff = gbase[None, :] + jnp.cumsum(run, axis=0) - run
    plan = jnp.concatenate([goff, run, lofs], axis=1).astype(jnp.int32)
    max_tiles = (n + ntiles * N_GROUPS * (SORT_ALIGN - 1)) // er + N_GROUPS
    cap = max_tiles * er
    n_valid = (jnp.sum(seg) // er).astype(jnp.int32).reshape(1)
    tile_start = jnp.arange(max_tiles, dtype=jnp.int32) * er
    tile_group = jnp.clip(jnp.sum(tile_start[:, None] >= (gbase + seg)[None, :], axis=1), 0, N_GROUPS - 1).astype(jnp.int32)

    cparams = dict(vmem_limit_bytes=V7X_VMEM_LIMIT_BYTES)
    any_spec = pl.BlockSpec(memory_space=pl.ANY)
    total = jnp.sum(run, axis=0)
    assert er % EXPERT_ROWS_SHORT == 0 and EXPERT_ROWS_SHORT <= SORT_LOCAL
    used = jnp.sum(seg)
    gaps = jnp.concatenate([gbase + total, seg - total, jnp.stack([used, cap - used])]).astype(jnp.int32)
    hs, pos = pl.pallas_call(
        _dispatch_kernel,
        grid_spec=pltpu.PrefetchScalarGridSpec(
            num_scalar_prefetch=2, grid=(ntiles,),
            in_specs=[pl.BlockSpec((tm, D_MODEL), lambda i, plan, gaps: (i, 0)),
                      pl.BlockSpec((tm, ROUTER_LANES), lambda i, plan, gaps: (i, 0))],
            out_specs=[any_spec, pl.BlockSpec((tm, ROUTER_LANES), lambda i, plan, gaps: (i, 0))],
            scratch_shapes=[pltpu.VMEM((SORT_LOCAL, SORTED_WIDTH), BF16), pltpu.SemaphoreType.DMA(())]),
        out_shape=[jax.ShapeDtypeStruct((cap, SORTED_WIDTH), BF16), jax.ShapeDtypeStruct((n, ROUTER_LANES), jnp.int32)],
        compiler_params=pltpu.CompilerParams(dimension_semantics=("arbitrary",), **cparams),
        name="dispatch",
    )(plan, gaps, h, gate)

    def tile_rows(width):
        return pl.BlockSpec((er, width), lambda j, tg, nv: (jnp.minimum(j, nv[0] - 1), 0))

    def group_w(shape):
        return pl.BlockSpec((EXPERTS_PER_GROUP,) + shape, lambda j, tg, nv: (tg[jnp.minimum(j, nv[0] - 1)], 0, 0))

    ys = pl.pallas_call(
        _experts_kernel,
        grid_spec=pltpu.PrefetchScalarGridSpec(
            num_scalar_prefetch=2, grid=(max_tiles,),
            in_specs=[tile_rows(SORTED_WIDTH), group_w((D_MODEL, D_EXPERT)),
                      group_w((D_MODEL, D_EXPERT)), group_w((D_EXPERT, D_MODEL))],
            out_specs=pl.BlockSpec((er, D_MODEL), lambda j, tg, nv: (j, 0)),
            scratch_shapes=[pltpu.VMEM((EXPERTS_PER_GROUP, D_MODEL, D_EXPERT), BF16),
                            pltpu.VMEM((EXPERTS_PER_GROUP, D_MODEL, D_EXPERT), BF16),
                            pltpu.VMEM((EXPERTS_PER_GROUP, D_EXPERT, D_MODEL), BF16)]),
        out_shape=jax.ShapeDtypeStruct((cap, D_MODEL), BF16),
        compiler_params=pltpu.CompilerParams(dimension_semantics=("arbitrary",), **cparams),
        name="experts",
    )(tile_group, n_valid, hs, W["expert_w1"], W["expert_w3"], W["expert_w2"])

    def full(a):
        return pl.BlockSpec(a.shape, lambda i, plan: (0,) * a.ndim)

    params = [W["ln2_g"], W["ln2_b"], W["w_ple"], W["w_ple_gate"], W["ple_norm_g"]]
    group_tiles = tuple(pg.shape[0] // tm for pg in p_groups)
    assert sum(group_tiles) == ntiles
    starts = [sum(group_tiles[:k]) for k in range(len(group_tiles))]
    return pl.pallas_call(
        functools.partial(_combine_kernel, group_tiles),
        grid_spec=pltpu.PrefetchScalarGridSpec(
            num_scalar_prefetch=1, grid=(ntiles,),
            in_specs=[pl.BlockSpec((tm, D_MODEL), lambda i, plan: (i, 0)),
                      pl.BlockSpec((tm, ROUTER_LANES), lambda i, plan: (i, 0))]
            + [_group_rows(tm, D_PLE, st, t) for st, t in zip(starts, group_tiles)]
            + [any_spec] + [full(a) for a in params],
            out_specs=[_group_rows(tm, D_MODEL, st, t) for st, t in zip(starts, group_tiles)],
            scratch_shapes=[pltpu.VMEM((2, SORT_LOCAL, D_MODEL), BF16), pltpu.SemaphoreType.DMA((2,))]),
        out_shape=[jax.ShapeDtypeStruct((t * tm, D_MODEL), F32) for t in group_tiles],
        compiler_params=pltpu.CompilerParams(dimension_semantics=("arbitrary",), **cparams),
        name="combine",
    )(plan, h, pos, *p_groups, ys, *params)


def _mix(x, x_prev, wkv0, ret0, pos0, W):
    n_seq, seq_len, _ = x.shape
    x2 = x.reshape(n_seq * seq_len, D_MODEL)
    r, lw, k, v, al, be, g, bonus, qb, kb, vb, gb = _proj(x2, x_prev, seq_len, pos0, W)
    ya, ob, wkv1, ret1 = _mixer((r, lw, k, v, al, be, qb, kb, vb), wkv0, ret0, n_seq, seq_len)
    return (ya, ob, bonus, g, gb, x2), wkv1, ret1


def _prep_weights(i, w_in, mu_shift, w_decay_up, decay_base, w_aaa_up, aaa_base, w_gate_up, k_k, k_a, r_k,
                  lnx_g, lnx_b, ret_gn_g, ret_gn_b, w_out, ln1_g, ln1_b,
                  router_coarse_w, router_coarse_b, router_fine_w, router_fine_b,
                  expert_w1, expert_w3, expert_w2, ln2_g, ln2_b, w_ple, w_ple_gate, ple_norm_g):
    def row(a):
        return a[i].reshape(1, -1).astype(F32)

    pad = ROUTE_ROWS - N_GROUPS - N_EXPERTS
    w_router = jnp.concatenate([router_coarse_w[i].T, router_fine_w[i].T, jnp.zeros((pad, D_MODEL), F32)], axis=0)
    b_router = jnp.concatenate([router_coarse_b[i], router_fine_b[i], jnp.zeros((pad,), F32)]).reshape(-1, 1)
    return {
        "w_in": w_in[i].astype(BF16), "mu_shift": row(mu_shift), "w_decay_up": w_decay_up[i].astype(BF16),
        "decay_base": row(decay_base), "w_aaa_up": w_aaa_up[i].astype(BF16), "aaa_base": row(aaa_base),
        "w_gate_up": w_gate_up[i].astype(BF16), "k_k": row(k_k), "k_a": row(k_a), "r_k": row(r_k),
        "lnx_g": row(lnx_g), "lnx_b": row(lnx_b), "ret_gn_g": row(ret_gn_g), "ret_gn_b": row(ret_gn_b),
        "w_out": w_out[i].astype(BF16), "ln1_g": row(ln1_g), "ln1_b": row(ln1_b),
        "w_router": w_router, "b_router": b_router,
        "expert_w1": expert_w1[i], "expert_w3": expert_w3[i], "expert_w2": expert_w2[i], "ln2_g": row(ln2_g), "ln2_b": row(ln2_b),
        "w_ple": w_ple[i].astype(BF16), "w_ple_gate": w_ple_gate[i].astype(BF16), "ple_norm_g": row(ple_norm_g),
    }


def kernel(x_prompt, x_sample, p_prompt, p_sample, state_wkv, state_shift, state_ret, w_in, mu_shift, w_decay_up, decay_base, w_aaa_up, aaa_base, w_gate_up, k_k, k_a, r_k, lnx_g, lnx_b, ret_gn_g, ret_gn_b, w_out, ln1_g, ln1_b, router_coarse_w, router_coarse_b, router_fine_w, router_fine_b, expert_w1, expert_w3, expert_w2, ln2_g, ln2_b, w_ple, w_ple_gate, ple_norm_g):
    yp, ys = x_prompt, x_sample
    nb = x_prompt.shape[0]
    depth = w_in.shape[0]
    wkv_p, shift_p, ret_p, wkv_s, shift_s, ret_s = [], [], [], [], [], []
    for i in range(depth):
        W = _prep_weights(i, w_in, mu_shift, w_decay_up, decay_base, w_aaa_up, aaa_base, w_gate_up, k_k, k_a, r_k,
                          lnx_g, lnx_b, ret_gn_g, ret_gn_b, w_out, ln1_g, ln1_b,
                          router_coarse_w, router_coarse_b, router_fine_w, router_fine_b,
                          expert_w1, expert_w3, expert_w2, ln2_g, ln2_b, w_ple, w_ple_gate, ple_norm_g)
        ops_p, wp, rp = _mix(yp, jnp.zeros((nb, D_MODEL), F32), jnp.zeros((nb, A_HEADS, A_HEAD_DIM, A_HEAD_DIM), F32),
                             jnp.zeros((nb, B_HEADS, B_QK_DIM, B_V_DIM), F32), 0, W)
        ops_s, wsm, rsm = _mix(ys, state_shift[i], state_wkv[i], state_ret[i], PAST_LEN, W)
        sp, ss = yp[:, -1], ys[:, -1]
        h, gate, counts = _post([ops_p, ops_s], W)
        out_p, out_s = _ffn(h, gate, counts, [p_prompt[i].reshape(-1, D_PLE), p_sample[i].reshape(-1, D_PLE)], W)
        yp, ys = out_p.reshape(yp.shape), out_s.reshape(ys.shape)
        wkv_p.append(wp); shift_p.append(sp); ret_p.append(rp)
        wkv_s.append(wsm); shift_s.append(ss); ret_s.append(rsm)
    return (yp, ys, jnp.stack(wkv_p, 0), jnp.stack(shift_p, 0), jnp.stack(ret_p, 0),
            jnp.stack(wkv_s, 0), jnp.stack(shift_s, 0), jnp.stack(ret_s, 0))
```
